```python
import jax, jax.numpy as jnp
from jax import lax
import numpy as np

D_MODEL = 1024
BATCH = 4
SEQ = 4096
DEPTH = 1

CHUNK = 64

MIX_WIDTH = D_MODEL
CONV_WIDTH = MIX_WIDTH // 2
CONV_HEADS = 8
CONV_K = 3
POOL_WIDTH = MIX_WIDTH - CONV_WIDTH
POOL_WINDOWS = (2, 4, 8, 16)
POOL_GROUPS = len(POOL_WINDOWS)
POOL_GW = POOL_WIDTH // POOL_GROUPS
IN_PROJ = 3 * CONV_WIDTH + POOL_WIDTH

N_GROUPS = 4
EXPERTS_PER_GROUP = 8
TOP_K_EXPERTS = 2
D_EXPERT = D_MODEL // 4

PLE_DIM = 256
LN_EPS = 1e-5
DEEPNORM_ALPHA = (2.0 * DEPTH) ** 0.25
DEEPNORM_BETA = (8.0 * DEPTH) ** -0.25

kernel_name = "hybrid_conv_pool_hmoe_deepnorm_block"


def _layernorm(x, g, b):
    xf = x.astype(jnp.float32)
    mu = jnp.mean(xf, axis=-1, keepdims=True)
    var = jnp.mean(jnp.square(xf - mu), axis=-1, keepdims=True)
    y = (xf - mu) * lax.rsqrt(var + LN_EPS) * g.astype(jnp.float32) + b.astype(jnp.float32)
    return y.astype(x.dtype)


def _token_mixers(h, w_in, conv_w, conv_b, pool_w, pool_scale, w_out):
    bsz, seq, _ = h.shape
    u = h @ w_in
    b_g, c_g, v_c, v_p = jnp.split(u, [CONV_WIDTH, 2 * CONV_WIDTH, 3 * CONV_WIDTH], axis=-1)

    z = c_g * v_c
    zp = jnp.pad(z, ((0, 0), (CONV_K - 1, 0), (0, 0)))
    conv = sum(zp[:, k:k + seq] * conv_w[k] for k in range(CONV_K)) + conv_b
    y_conv = b_g * conv

    vp = v_p.reshape(bsz, seq, POOL_GROUPS, POOL_GW)
    cs0 = jnp.pad(jnp.cumsum(vp.astype(jnp.float32), axis=1), ((0, 0), (1, 0), (0, 0), (0, 0)))
    t = jnp.arange(seq)
    means = []
    for j, w in enumerate(POOL_WINDOWS):
        c = jnp.pad(cs0[:, :, j], ((0, 0), (w - 1, 0), (0, 0)))
        wsum = c[:, w:w + seq] - c[:, :seq]
        cnt = jnp.minimum(t + 1, w).astype(jnp.float32)[None, :, None]
        means.append(wsum / cnt)
    pooled = jnp.stack(means, axis=2).astype(h.dtype) - vp
    y_pool = jnp.einsum('bsgc,gcd->bsgd', pooled, pool_w).reshape(bsz, seq, POOL_WIDTH) * pool_scale

    return jnp.concatenate([y_conv, y_pool], axis=-1) @ w_out


def _hier_moe(h, w_rg, b_rg, w_re, b_re, w_gate, w_up, w_down):
    bsz, seq, d = h.shape
    tok = h.reshape(-1, d)
    g_logits = (tok @ w_rg).astype(jnp.float32) + b_rg.astype(jnp.float32)
    g_prob = jax.nn.softmax(g_logits, axis=-1)
    g_val, g_idx = lax.top_k(g_logits, 1)
    g_idx = g_idx[:, 0]
    g_w = jnp.take_along_axis(g_prob, g_idx[:, None], axis=-1)[:, 0]
    e_all = jnp.einsum('td,gde->tge', tok, w_re).astype(jnp.float32) + b_re.astype(jnp.float32)
    e_logits = jnp.take_along_axis(e_all, g_idx[:, None, None], axis=1)[:, 0]
    top_v, top_i = lax.top_k(e_logits, TOP_K_EXPERTS)
    top_w = jax.nn.softmax(top_v, axis=-1) * g_w[:, None]
    exp_w = jnp.sum(jax.nn.one_hot(top_i, EXPERTS_PER_GROUP, dtype=jnp.float32) * top_w[..., None], axis=1)
    combine = jax.nn.one_hot(g_idx, N_GROUPS, dtype=jnp.float32)[:, :, None] * exp_w[:, None, :]
    combine = combine.astype(h.dtype)
    out = jnp.zeros_like(tok)
    for g in range(N_GROUPS):
        hg = jnp.einsum('td,edf->tef', tok, w_gate[g])
        hu = jnp.einsum('td,edf->tef', tok, w_up[g])
        act = jax.nn.silu(hg) * hu * combine[:, g, :, None]
        out = out + jnp.einsum('tef,efd->td', act, w_down[g])
    return out.reshape(bsz, seq, d)


def setup_inputs(seed: int = 0) -> dict:
    key = jax.random.key(seed)
    ks = jax.random.split(key, 26)
    f32 = jnp.float32
    nrm = lambda k, shape, s: (jax.random.normal(k, shape, f32) * s)
    E = EXPERTS_PER_GROUP
    return {
        "x": nrm(ks[0], (BATCH, SEQ, D_MODEL), 1.0),
        "p": nrm(ks[1], (DEPTH, BATCH, SEQ, PLE_DIM), 1.0),
        "ln_in_g": 1.0 + nrm(ks[2], (D_MODEL,), 0.02),
        "ln_in_b": nrm(ks[3], (D_MODEL,), 0.02),
        "w_in": nrm(ks[4], (DEPTH, D_MODEL, IN_PROJ), D_MODEL ** -0.5),
        "conv_w": nrm(ks[5], (DEPTH, CONV_K, CONV_WIDTH), CONV_K ** -0.5),
        "conv_b": nrm(ks[6], (DEPTH, CONV_WIDTH), 0.02),
        "pool_w": nrm(ks[7], (DEPTH, POOL_GROUPS, POOL_GW, POOL_GW), POOL_GW ** -0.5),
        "pool_scale": 1.0 + nrm(ks[8], (DEPTH, POOL_WIDTH), 0.02),
        "w_out": nrm(ks[9], (DEPTH, MIX_WIDTH, D_MODEL), MIX_WIDTH ** -0.5 * DEEPNORM_BETA),
        "ln1_g": 1.0 + nrm(ks[10], (DEPTH, D_MODEL), 0.02),
        "ln1_b": nrm(ks[11], (DEPTH, D_MODEL), 0.02),
        "w_rg": nrm(ks[12], (DEPTH, D_MODEL, N_GROUPS), D_MODEL ** -0.5),
        "b_rg": nrm(ks[13], (DEPTH, N_GROUPS), 0.01),
        "w_re": nrm(ks[14], (DEPTH, N_GROUPS, D_MODEL, E), D_MODEL ** -0.5),
        "b_re": nrm(ks[15], (DEPTH, N_GROUPS, E), 0.01),
        "w_gate": nrm(ks[16], (DEPTH, N_GROUPS, E, D_MODEL, D_EXPERT), D_MODEL ** -0.5),
        "w_up": nrm(ks[17], (DEPTH, N_GROUPS, E, D_MODEL, D_EXPERT), D_MODEL ** -0.5),
        "w_down": nrm(ks[18], (DEPTH, N_GROUPS, E, D_EXPERT, D_MODEL), D_EXPERT ** -0.5 * DEEPNORM_BETA),
        "w_pg": nrm(ks[19], (DEPTH, D_MODEL, D_MODEL), D_MODEL ** -0.5),
        "b_pg": nrm(ks[20], (DEPTH, D_MODEL), 0.02),
        "w_ple": nrm(ks[21], (DEPTH, PLE_DIM, D_MODEL), PLE_DIM ** -0.5 * DEEPNORM_BETA),
        "ln2_g": 1.0 + nrm(ks[22], (DEPTH, D_MODEL), 0.02),
        "ln2_b": nrm(ks[23], (DEPTH, D_MODEL), 0.02),
    }


def reference(x, p, ln_in_g, ln_in_b, w_in, conv_w, conv_b, pool_w, pool_scale, w_out, ln1_g, ln1_b,
              w_rg, b_rg, w_re, b_re, w_gate, w_up, w_down, w_pg, b_pg, w_ple, ln2_g, ln2_b):
    h = _layernorm(x, ln_in_g, ln_in_b)
    for i in range(DEPTH):
        mix = _token_mixers(h, w_in[i], conv_w[i], conv_b[i], pool_w[i], pool_scale[i], w_out[i])
        h = _layernorm(DEEPNORM_ALPHA * h + mix, ln1_g[i], ln1_b[i])
        moe = _hier_moe(h, w_rg[i], b_rg[i], w_re[i], b_re[i], w_gate[i], w_up[i], w_down[i])
        ple = (p[i] @ w_ple[i]) * jax.nn.sigmoid(h @ w_pg[i] + b_pg[i])
        h = _layernorm(DEEPNORM_ALPHA * h + moe + ple, ln2_g[i], ln2_b[i])
    return h
```

```python
import functools

import jax
import jax.numpy as jnp
from jax import lax
from jax.experimental import pallas as pl
from jax.experimental.pallas import tpu as pltpu

D_MODEL = 1024
CONV_WIDTH = 512
CONV_K = 3
POOL_WIDTH = 512
POOL_WINDOWS = (2, 4, 8, 16)
POOL_GW = 128
IN_PROJ = 3 * CONV_WIDTH + POOL_WIDTH
N_GROUPS = 4
EXPERTS_PER_GROUP = 8
N_EXPERTS = N_GROUPS * EXPERTS_PER_GROUP
D_EXPERT = 256
PLE_DIM = 256
LN_EPS = 1e-5
DEEPNORM_ALPHA = 2.0 ** 0.25

LANES = 128
HALF = D_MODEL // 2
CONV_HALO = 8
POOL_HALO = 16
SEQ_TILE = 512
ROW_TILE = 256
GATHER_BLOCK = 512
TOKEN_TILE = 512
VMEM_LIMIT = 56 * 1024 * 1024

R_ID1, R_ID2, R_W1, R_W2, R_RANK1, R_RANK2 = range(6)


def _layernorm(x, g, b):
    mu = jnp.mean(x, axis=-1, keepdims=True)
    xc = x - mu
    var = jnp.mean(xc * xc, axis=-1, keepdims=True)
    return xc * lax.rsqrt(var + LN_EPS) * g + b


def _pack_bf16_pairs(v):
    bits = lax.bitcast_convert_type(v.astype(jnp.bfloat16).astype(jnp.float32), jnp.uint32)
    return bits[:, HALF:] | (bits[:, :HALF] >> 16)


def _unpack_bf16_pairs(w):
    lo = lax.bitcast_convert_type(w << 16, jnp.float32)
    hi = lax.bitcast_convert_type(w & jnp.uint32(0xFFFF0000), jnp.float32)
    return lo, hi


def _mixer_kernel(x_ref, p_ref, lnin_g, lnin_b, w_in, conv_w, conv_b, pool_w, pool_scale, w_out,
                  ln1_g, ln1_b, w_r, b_r, w_pg, b_pg, w_ple,
                  xp_ref, r_ref, route_ref, counts_ref,
                  zbuf, vbuf, carry):
    b = pl.program_id(0)
    s = pl.program_id(1)
    ts = x_ref.shape[0]

    @pl.when(s == 0)
    def _():
        zbuf[0:CONV_HALO, :] = jnp.zeros((CONV_HALO, CONV_WIDTH), jnp.float32)
        vbuf[0:POOL_HALO, :] = jnp.zeros((POOL_HALO, POOL_WIDTH), jnp.float32)

    @pl.when((b == 0) & (s == 0))
    def _():
        carry[...] = jnp.zeros_like(carry)

    h0 = _layernorm(x_ref[...], lnin_g[...], lnin_b[...])
    u = jnp.dot(h0.astype(jnp.bfloat16), w_in[...], preferred_element_type=jnp.float32)
    b_g = u[:, 0:CONV_WIDTH]
    c_g = u[:, CONV_WIDTH:2 * CONV_WIDTH]
    v_c = u[:, 2 * CONV_WIDTH:3 * CONV_WIDTH]
    v_p = u[:, 3 * CONV_WIDTH:]

    zbuf[CONV_HALO:, :] = c_g * v_c
    zext = zbuf[...]
    z1 = pltpu.roll(zext, 1, axis=0)[CONV_HALO:, :]
    z2 = pltpu.roll(zext, 2, axis=0)[CONV_HALO:, :]
    conv = z2 * conv_w[0:1, :] + z1 * conv_w[1:2, :] + zext[CONV_HALO:, :] * conv_w[2:3, :] + conv_b[...]
    y_conv = b_g * conv
    zbuf[0:CONV_HALO, :] = zbuf[ts:ts + CONV_HALO, :]

    vbuf[POOL_HALO:, :] = v_p
    vext = vbuf[...]
    s2 = vext + pltpu.roll(vext, 1, axis=0)
    s4 = s2[:, POOL_GW:] + pltpu.roll(s2[:, POOL_GW:], 2, axis=0)
    s8 = s4[:, POOL_GW:] + pltpu.roll(s4[:, POOL_GW:], 4, axis=0)
    s16 = s8[:, POOL_GW:] + pltpu.roll(s8[:, POOL_GW:], 8, axis=0)
    wsums = (s2[POOL_HALO:, 0:POOL_GW], s4[POOL_HALO:, 0:POOL_GW],
             s8[POOL_HALO:, 0:POOL_GW], s16[POOL_HALO:, 0:POOL_GW])
    vbuf[0:POOL_HALO, :] = vbuf[ts:ts + POOL_HALO, :]

    t_pos = s * ts + lax.broadcasted_iota(jnp.int32, (ts, 1), 0)
    y_pool = []
    for j, w in enumerate(POOL_WINDOWS):
        inv_cnt = 1.0 / jnp.minimum(t_pos + 1, w).astype(jnp.float32)
        pooled = wsums[j] * inv_cnt - v_p[:, j * POOL_GW:(j + 1) * POOL_GW]
        y_pool.append(jnp.dot(pooled.astype(jnp.bfloat16), pool_w[j], preferred_element_type=jnp.float32))
    y_pool = jnp.concatenate(y_pool, axis=-1) * pool_scale[...]

    ycat = jnp.concatenate([y_conv, y_pool], axis=-1).astype(jnp.bfloat16)
    mix = jnp.dot(ycat, w_out[...], preferred_element_type=jnp.float32)
    h1 = _layernorm(DEEPNORM_ALPHA * h0 + mix, ln1_g[...], ln1_b[...])

    h_hi = h1.astype(jnp.bfloat16)
    h_lo = (h1 - h_hi.astype(jnp.float32)).astype(jnp.bfloat16)
    xp_ref[...] = _pack_bf16_pairs(h1)

    gate = jax.nn.sigmoid(jnp.dot(h_hi, w_pg[...], preferred_element_type=jnp.float32) + b_pg[...])
    ple = jnp.dot(p_ref[...].astype(jnp.bfloat16), w_ple[...], preferred_element_type=jnp.float32) * gate
    r_ref[...] = DEEPNORM_ALPHA * h1 + ple

    hcat = jnp.concatenate([h_hi, h_lo, h_hi], axis=-1)
    logits = jnp.dot(hcat, w_r[...], preferred_element_type=jnp.float32) + b_r[...]
    lane = lax.broadcasted_iota(jnp.int32, (ts, LANES), 1)
    neg = jnp.float32(-jnp.inf)

    def first_argmax(vals):
        m = jnp.max(vals, axis=-1, keepdims=True)
        idx = jnp.min(jnp.where(vals == m, lane, LANES), axis=-1, keepdims=True)
        return m, idx

    g_mask = lane < N_GROUPS
    g_max, g_idx = first_argmax(jnp.where(g_mask, logits, neg))
    g_w = 1.0 / jnp.sum(jnp.where(g_mask, jnp.exp(logits - g_max), 0.0), axis=-1, keepdims=True)

    e_lo = N_GROUPS + EXPERTS_PER_GROUP * g_idx
    e_vals = jnp.where((lane >= e_lo) & (lane < e_lo + EXPERTS_PER_GROUP), logits, neg)
    m1, i1 = first_argmax(e_vals)
    m2, i2 = first_argmax(jnp.where(lane == i1, neg, e_vals))
    e21 = jnp.exp(m2 - m1)
    w1 = g_w / (1.0 + e21)
    w2 = g_w * e21 / (1.0 + e21)
    id1 = i1 - N_GROUPS
    id2 = i2 - N_GROUPS

    sel1 = lane == id1
    sel2 = lane == id2
    onehot = (sel1 | sel2).astype(jnp.float32)
    row = lax.broadcasted_iota(jnp.int32, (ts, ts), 0)
    col = lax.broadcasted_iota(jnp.int32, (ts, ts), 1)
    tri = (col < row).astype(jnp.bfloat16)
    before = jnp.dot(tri, onehot.astype(jnp.bfloat16), preferred_element_type=jnp.float32) + carry[...]
    rank1 = jnp.sum(jnp.where(sel1, before, 0.0), axis=-1, keepdims=True)
    rank2 = jnp.sum(jnp.where(sel2, before, 0.0), axis=-1, keepdims=True)
    carry[...] = carry[...] + jnp.sum(onehot, axis=0, keepdims=True)
    counts_ref[...] = carry[...]

    rec = jnp.zeros((ts, LANES), jnp.float32)
    for k, val in ((R_ID1, id1.astype(jnp.float32)), (R_ID2, id2.astype(jnp.float32)), (R_W1, w1), (R_W2, w2),
                   (R_RANK1, rank1), (R_RANK2, rank2)):
        rec = jnp.where(lane == k, val, rec)
    route_ref[...] = rec


def _run_mixer(x2, p2, lnin_g, lnin_b, w_in, conv_w, conv_b, pool_w, pool_scale, w_out, ln1_g, ln1_b,
               w_r, b_r, w_pg, b_pg, w_ple, batch, seq):
    n_tok = batch * seq
    n_s = seq // SEQ_TILE
    tok_map = lambda b, s: (b * n_s + s, 0)

    def const(shape):
        return pl.BlockSpec(shape, lambda b, s: (0,) * len(shape), pipeline_mode=pl.Buffered(1))

    in_specs = [
        pl.BlockSpec((SEQ_TILE, D_MODEL), tok_map),
        pl.BlockSpec((SEQ_TILE, PLE_DIM), tok_map),
        const((1, D_MODEL)), const((1, D_MODEL)),
        const((D_MODEL, IN_PROJ)),
        const((CONV_K, CONV_WIDTH)), const((1, CONV_WIDTH)),
        const((len(POOL_WINDOWS), POOL_GW, POOL_GW)), const((1, POOL_WIDTH)),
        const((D_MODEL, D_MODEL)),
        const((1, D_MODEL)), const((1, D_MODEL)),
        const((3 * D_MODEL, LANES)), const((1, LANES)),
        const((D_MODEL, D_MODEL)), const((1, D_MODEL)),
        const((PLE_DIM, D_MODEL)),
    ]
    out_specs = [
        pl.BlockSpec((SEQ_TILE, HALF), tok_map),
        pl.BlockSpec((SEQ_TILE, D_MODEL), tok_map),
        pl.BlockSpec((SEQ_TILE, LANES), tok_map),
        pl.BlockSpec((1, LANES), lambda b, s: (0, 0)),
    ]
    out_shape = [
        jax.ShapeDtypeStruct((n_tok, HALF), jnp.uint32),
        jax.ShapeDtypeStruct((n_tok, D_MODEL), jnp.float32),
        jax.ShapeDtypeStruct((n_tok, LANES), jnp.float32),
        jax.ShapeDtypeStruct((1, LANES), jnp.float32),
    ]
    return pl.pallas_call(
        _mixer_kernel,
        grid=(batch, n_s),
        in_specs=in_specs,
        out_specs=out_specs,
        out_shape=out_shape,
        scratch_shapes=[
            pltpu.VMEM((SEQ_TILE + CONV_HALO, CONV_WIDTH), jnp.float32),
            pltpu.VMEM((SEQ_TILE + POOL_HALO, POOL_WIDTH), jnp.float32),
            pltpu.VMEM((1, LANES), jnp.float32),
        ],
        compiler_params=pltpu.CompilerParams(
            dimension_semantics=("arbitrary", "arbitrary"), vmem_limit_bytes=VMEM_LIMIT),
    )(x2, p2, lnin_g, lnin_b, w_in, conv_w, conv_b, pool_w, pool_scale, w_out, ln1_g, ln1_b,
      w_r, b_r, w_pg, b_pg, w_ple)


def _row_copy(src_hbm, dst_hbm, sem, src_row, dst_row):
    return pltpu.make_async_copy(src_hbm.at[pl.ds(src_row, 1)], dst_hbm.at[pl.ds(dst_row, 1)], sem)


def _gather_kernel(idx_ref, src_hbm, dst_hbm, sem):
    base = pl.program_id(0) * GATHER_BLOCK

    def start(j, c):
        _row_copy(src_hbm, dst_hbm, sem, idx_ref[0, 0, j], base + j).start()
        return c

    def wait(j, c):
        _row_copy(src_hbm, dst_hbm, sem, idx_ref[0, 0, j], base + j).wait()
        return c

    lax.fori_loop(0, GATHER_BLOCK, start, 0)
    lax.fori_loop(0, GATHER_BLOCK, wait, 0)


def _gather_rows(src, idx):
    n_rows = idx.shape[0]
    n_blocks = n_rows // GATHER_BLOCK
    return pl.pallas_call(
        _gather_kernel,
        grid=(n_blocks,),
        in_specs=[
            pl.BlockSpec((1, 1, GATHER_BLOCK), lambda i: (i, 0, 0), memory_space=pltpu.SMEM),
            pl.BlockSpec(memory_space=pl.ANY),
        ],
        out_specs=pl.BlockSpec(memory_space=pl.ANY),
        out_shape=jax.ShapeDtypeStruct((n_rows, src.shape[1]), src.dtype),
        scratch_shapes=[pltpu.SemaphoreType.DMA],
        compiler_params=pltpu.CompilerParams(dimension_semantics=("arbitrary",)),
    )(idx.reshape(n_blocks, 1, GATHER_BLOCK), src)


def _expert_kernel(te_ref, nt_ref, xs_ref, wg_ref, wu_ref, wd_ref, y_ref, wgu_bf, wd_bf):
    i = pl.program_id(0)

    @pl.when(i < nt_ref[0])
    def _():
        prev = te_ref[jnp.maximum(i - 1, 0)]

        @pl.when((i == 0) | (te_ref[i] != prev))
        def _():
            wgu_bf[:, 0:D_EXPERT] = wg_ref[0].astype(jnp.bfloat16)
            wgu_bf[:, D_EXPERT:] = wu_ref[0].astype(jnp.bfloat16)
            wd_bf[...] = wd_ref[0].astype(jnp.bfloat16)

        lo, hi = _unpack_bf16_pairs(xs_ref[...])
        hgu = (jnp.dot(lo.astype(jnp.bfloat16), wgu_bf[0:HALF, :], preferred_element_type=jnp.float32)
               + jnp.dot(hi.astype(jnp.bfloat16), wgu_bf[HALF:, :], preferred_element_type=jnp.float32))
        hg = hgu[:, 0:D_EXPERT]
        act = hg * jax.nn.sigmoid(hg) * hgu[:, D_EXPERT:]
        y = jnp.dot(act.astype(jnp.bfloat16), wd_bf[...], preferred_element_type=jnp.float32)
        y_ref[...] = _pack_bf16_pairs(y)

    @pl.when(i >= nt_ref[0])
    def _():
        y_ref[...] = jnp.zeros_like(y_ref)


def _run_experts(tile_expert, n_tiles, x_sorted, w_gate, w_up, w_down):
    n_rows = x_sorted.shape[0]
    max_tiles = n_rows // ROW_TILE

    def row_map(i, te, nt):
        return (jnp.minimum(i, nt[0] - 1), 0)

    def w_map(i, te, nt):
        return (te[i], 0, 0)

    grid_spec = pltpu.PrefetchScalarGridSpec(
        num_scalar_prefetch=2,
        grid=(max_tiles,),
        in_specs=[
            pl.BlockSpec((ROW_TILE, HALF), row_map),
            pl.BlockSpec((1, D_MODEL, D_EXPERT), w_map),
            pl.BlockSpec((1, D_MODEL, D_EXPERT), w_map),
            pl.BlockSpec((1, D_EXPERT, D_MODEL), w_map),
        ],
        out_specs=pl.BlockSpec((ROW_TILE, HALF), lambda i, te, nt: (i, 0)),
        scratch_shapes=[
            pltpu.VMEM((D_MODEL, 2 * D_EXPERT), jnp.bfloat16),
            pltpu.VMEM((D_EXPERT, D_MODEL), jnp.bfloat16),
        ],
    )
    return pl.pallas_call(
        _expert_kernel,
        grid_spec=grid_spec,
        out_shape=jax.ShapeDtypeStruct((n_rows, HALF), jnp.uint32),
        compiler_params=pltpu.CompilerParams(
            dimension_semantics=("arbitrary",), vmem_limit_bytes=VMEM_LIMIT),
    )(tile_expert, n_tiles, x_sorted, w_gate, w_up, w_down)


def _combine_kernel(r_ref, route_ref, ya_ref, yb_ref, g_ref, b_ref, o_ref):
    route = route_ref[...]
    w1 = route[:, R_W1:R_W1 + 1]
    w2 = route[:, R_W2:R_W2 + 1]
    a_lo, a_hi = _unpack_bf16_pairs(ya_ref[...])
    b_lo, b_hi = _unpack_bf16_pairs(yb_ref[...])
    moe = jnp.concatenate([w1 * a_lo + w2 * b_lo, w1 * a_hi + w2 * b_hi], axis=-1)
    o_ref[...] = _layernorm(r_ref[...] + moe, g_ref[...], b_ref[...])


def _run_combine(r, route, y_tok, ln2_g, ln2_b):
    n_tok = r.shape[0]
    n_t = n_tok // TOKEN_TILE
    return pl.pallas_call(
        _combine_kernel,
        grid=(n_t,),
        in_specs=[
            pl.BlockSpec((TOKEN_TILE, D_MODEL), lambda i: (i, 0)),
            pl.BlockSpec((TOKEN_TILE, LANES), lambda i: (i, 0)),
            pl.BlockSpec((TOKEN_TILE, HALF), lambda i: (i, 0)),
            pl.BlockSpec((TOKEN_TILE, HALF), lambda i: (i + n_t, 0)),
            pl.BlockSpec((1, D_MODEL), lambda i: (0, 0)),
            pl.BlockSpec((1, D_MODEL), lambda i: (0, 0)),
        ],
        out_specs=pl.BlockSpec((TOKEN_TILE, D_MODEL), lambda i: (i, 0)),
        out_shape=jax.ShapeDtypeStruct((n_tok, D_MODEL), jnp.float32),
        compiler_params=pltpu.CompilerParams(dimension_semantics=("arbitrary",)),
    )(r, route, y_tok, y_tok, ln2_g, ln2_b)


def _split_bf16(w):
    hi = w.astype(jnp.bfloat16)
    lo = (w - hi.astype(jnp.float32)).astype(jnp.bfloat16)
    return hi, lo


def kernel(x, p, ln_in_g, ln_in_b, w_in, conv_w, conv_b, pool_w, pool_scale, w_out, ln1_g, ln1_b,
           w_rg, b_rg, w_re, b_re, w_gate, w_up, w_down, w_pg, b_pg, w_ple, ln2_g, ln2_b):
    batch, seq, _ = x.shape
    n_tok = batch * seq
    bf = jnp.bfloat16
    row = lambda v: v.reshape(1, -1)

    w_r = jnp.concatenate([w_rg[0], jnp.transpose(w_re[0], (1, 0, 2)).reshape(D_MODEL, N_EXPERTS)], axis=1)
    w_r = jnp.pad(w_r, ((0, 0), (0, LANES - w_r.shape[1])))
    w_r_hi, w_r_lo = _split_bf16(w_r)
    w_r_cat = jnp.concatenate([w_r_hi, w_r_hi, w_r_lo], axis=0)
    b_r = jnp.pad(jnp.concatenate([b_rg[0], b_re[0].reshape(-1)]), (0, LANES - N_GROUPS - N_EXPERTS)).reshape(1, LANES)

    xp, r, route, counts = _run_mixer(
        x.reshape(n_tok, D_MODEL), p[0].reshape(n_tok, PLE_DIM), row(ln_in_g), row(ln_in_b),
        w_in[0].astype(bf), conv_w[0], row(conv_b[0]), pool_w[0].astype(bf), row(pool_scale[0]),
        w_out[0].astype(bf), row(ln1_g[0]), row(ln1_b[0]), w_r_cat, b_r,
        w_pg[0].astype(bf), row(b_pg[0]), w_ple[0].astype(bf), batch, seq)

    counts = counts[0, :N_EXPERTS].astype(jnp.int32)
    tiles_per_expert = (counts + ROW_TILE - 1) // ROW_TILE
    tile_end = jnp.cumsum(tiles_per_expert)
    tile_start = tile_end - tiles_per_expert
    n_tiles = tile_end[-1:]
    n_rows = ((2 * n_tok + N_EXPERTS * (ROW_TILE - 1)) // GATHER_BLOCK + 1) * GATHER_BLOCK
    max_tiles = n_rows // ROW_TILE
    tile_ids = jnp.minimum(jnp.arange(max_tiles, dtype=jnp.int32), n_tiles[0] - 1)
    tile_expert = jnp.sum((tile_end[None, :] <= tile_ids[:, None]).astype(jnp.int32), axis=1)
    id1 = route[:, R_ID1].astype(jnp.int32)
    id2 = route[:, R_ID2].astype(jnp.int32)
    pos1 = tile_start[id1] * ROW_TILE + route[:, R_RANK1].astype(jnp.int32)
    pos2 = tile_start[id2] * ROW_TILE + route[:, R_RANK2].astype(jnp.int32)
    tok = jnp.arange(n_tok, dtype=jnp.int32)
    tok_sorted = jnp.zeros((n_rows,), jnp.int32).at[jnp.concatenate([pos1, pos2])].set(jnp.concatenate([tok, tok]))

    x_sorted = _gather_rows(xp, tok_sorted)
    y_sorted = _run_experts(tile_expert, n_tiles, x_sorted,
                            w_gate[0].reshape(N_EXPERTS, D_MODEL, D_EXPERT),
                            w_up[0].reshape(N_EXPERTS, D_MODEL, D_EXPERT),
                            w_down[0].reshape(N_EXPERTS, D_EXPERT, D_MODEL))
    y_tok = _gather_rows(y_sorted, jnp.concatenate([pos1, pos2]))
    out = _run_combine(r, route, y_tok, row(ln2_g[0]), row(ln2_b[0]))
    return out.reshape(batch, seq, D_MODEL)
```

```python
import functools

import jax
import jax.numpy as jnp
from jax import lax
from jax.experimental import pallas as pl
from jax.experimental.pallas import tpu as pltpu
from jax.experimental.pallas import tpu_sc as plsc

D_MODEL = 1024
CONV_WIDTH = 512
CONV_K = 3
POOL_WIDTH = 512
POOL_WINDOWS = (2, 4, 8, 16)
POOL_GW = 128
IN_PROJ = 3 * CONV_WIDTH + POOL_WIDTH
N_GROUPS = 4
EXPERTS_PER_GROUP = 8
N_EXPERTS = N_GROUPS * EXPERTS_PER_GROUP
D_EXPERT = 256
PLE_DIM = 256
LN_EPS = 1e-5
DEEPNORM_ALPHA = 2.0 ** 0.25

LANES = 128
HALF = D_MODEL // 2
CONV_HALO = 8
POOL_HALO = 16
SEQ_TILE = 512
ROW_TILE = 256
GATHER_BLOCK = 512
GATHER_WINDOW = 128
GATHER_SPLIT = 2
TOKEN_TILE = 512
VMEM_LIMIT = 56 * 1024 * 1024

R_ID1, R_ID2, R_W1, R_W2, R_RANK1, R_RANK2 = range(6)


def _layernorm(x, g, b):
    mu = jnp.mean(x, axis=-1, keepdims=True)
    xc = x - mu
    var = jnp.mean(xc * xc, axis=-1, keepdims=True)
    return xc * lax.rsqrt(var + LN_EPS) * g + b


def _pack_bf16_pairs(v):
    bits = lax.bitcast_convert_type(v.astype(jnp.bfloat16).astype(jnp.float32), jnp.uint32)
    return bits[:, HALF:] | (bits[:, :HALF] >> 16)


def _unpack_bf16_pairs(w):
    lo = lax.bitcast_convert_type(w << 16, jnp.float32)
    hi = lax.bitcast_convert_type(w & jnp.uint32(0xFFFF0000), jnp.float32)
    return lo, hi


def _mixer_kernel(x_ref, p_ref, lnin_g, lnin_b, w_in, conv_w, conv_b, pool_w, pool_scale, w_out,
                  ln1_g, ln1_b, w_r, b_r, w_pg, b_pg, w_ple,
                  xp_ref, r_ref, route_ref, counts_ref,
                  zbuf, vbuf, carry):
    b = pl.program_id(0)
    s = pl.program_id(1)
    ts = x_ref.shape[0]

    @pl.when(s == 0)
    def _():
        zbuf[0:CONV_HALO, :] = jnp.zeros((CONV_HALO, CONV_WIDTH), jnp.float32)
        vbuf[0:POOL_HALO, :] = jnp.zeros((POOL_HALO, POOL_WIDTH), jnp.float32)

    @pl.when((b == 0) & (s == 0))
    def _():
        carry[...] = jnp.zeros_like(carry)

    h0 = _layernorm(x_ref[...], lnin_g[...], lnin_b[...])
    u = jnp.dot(h0.astype(jnp.bfloat16), w_in[...], preferred_element_type=jnp.float32)
    b_g = u[:, 0:CONV_WIDTH]
    c_g = u[:, CONV_WIDTH:2 * CONV_WIDTH]
    v_c = u[:, 2 * CONV_WIDTH:3 * CONV_WIDTH]
    v_p = u[:, 3 * CONV_WIDTH:]

    zbuf[CONV_HALO:, :] = c_g * v_c
    zext = zbuf[...]
    z1 = pltpu.roll(zext, 1, axis=0)[CONV_HALO:, :]
    z2 = pltpu.roll(zext, 2, axis=0)[CONV_HALO:, :]
    conv = z2 * conv_w[0:1, :] + z1 * conv_w[1:2, :] + zext[CONV_HALO:, :] * conv_w[2:3, :] + conv_b[...]
    y_conv = b_g * conv
    zbuf[0:CONV_HALO, :] = zbuf[ts:ts + CONV_HALO, :]

    vbuf[POOL_HALO:, :] = v_p
    vext = vbuf[...]
    s2 = vext + pltpu.roll(vext, 1, axis=0)
    s4 = s2[:, POOL_GW:] + pltpu.roll(s2[:, POOL_GW:], 2, axis=0)
    s8 = s4[:, POOL_GW:] + pltpu.roll(s4[:, POOL_GW:], 4, axis=0)
    s16 = s8[:, POOL_GW:] + pltpu.roll(s8[:, POOL_GW:], 8, axis=0)
    wsums = (s2[POOL_HALO:, 0:POOL_GW], s4[POOL_HALO:, 0:POOL_GW],
             s8[POOL_HALO:, 0:POOL_GW], s16[POOL_HALO:, 0:POOL_GW])
    vbuf[0:POOL_HALO, :] = vbuf[ts:ts + POOL_HALO, :]

    t_pos = s * ts + lax.broadcasted_iota(jnp.int32, (ts, 1), 0)
    y_pool = []
    for j, w in enumerate(POOL_WINDOWS):
        inv_cnt = 1.0 / jnp.minimum(t_pos + 1, w).astype(jnp.float32)
        pooled = wsums[j] * inv_cnt - v_p[:, j * POOL_GW:(j + 1) * POOL_GW]
        y_pool.append(jnp.dot(pooled.astype(jnp.bfloat16), pool_w[j], preferred_element_type=jnp.float32))
    y_pool = jnp.concatenate(y_pool, axis=-1) * pool_scale[...]

    ycat = jnp.concatenate([y_conv, y_pool], axis=-1).astype(jnp.bfloat16)
    mix = jnp.dot(ycat, w_out[...], preferred_element_type=jnp.float32)
    h1 = _layernorm(DEEPNORM_ALPHA * h0 + mix, ln1_g[...], ln1_b[...])

    h_hi = h1.astype(jnp.bfloat16)
    h_lo = (h1 - h_hi.astype(jnp.float32)).astype(jnp.bfloat16)
    xp_ref[...] = _pack_bf16_pairs(h1)

    gate = jax.nn.sigmoid(jnp.dot(h_hi, w_pg[...], preferred_element_type=jnp.float32) + b_pg[...])
    ple = jnp.dot(p_ref[...].astype(jnp.bfloat16), w_ple[...], preferred_element_type=jnp.float32) * gate
    r_ref[...] = DEEPNORM_ALPHA * h1 + ple

    hcat = jnp.concatenate([h_hi, h_lo, h_hi], axis=-1)
    logits = jnp.dot(hcat, w_r[...], preferred_element_type=jnp.float32) + b_r[...]
    lane = lax.broadcasted_iota(jnp.int32, (ts, LANES), 1)
    neg = jnp.float32(-jnp.inf)

    def first_argmax(vals):
        m = jnp.max(vals, axis=-1, keepdims=True)
        idx = jnp.min(jnp.where(vals == m, lane, LANES), axis=-1, keepdims=True)
        return m, idx

    g_mask = lane < N_GROUPS
    g_max, g_idx = first_argmax(jnp.where(g_mask, logits, neg))
    g_w = 1.0 / jnp.sum(jnp.where(g_mask, jnp.exp(logits - g_max), 0.0), axis=-1, keepdims=True)

    e_lo = N_GROUPS + EXPERTS_PER_GROUP * g_idx
    e_vals = jnp.where((lane >= e_lo) & (lane < e_lo + EXPERTS_PER_GROUP), logits, neg)
    m1, i1 = first_argmax(e_vals)
    m2, i2 = first_argmax(jnp.where(lane == i1, neg, e_vals))
    e21 = jnp.exp(m2 - m1)
    w1 = g_w / (1.0 + e21)
    w2 = g_w * e21 / (1.0 + e21)
    id1 = i1 - N_GROUPS
    id2 = i2 - N_GROUPS

    sel1 = lane == id1
    sel2 = lane == id2
    onehot = (sel1 | sel2).astype(jnp.float32)
    row = lax.broadcasted_iota(jnp.int32, (ts, ts), 0)
    col = lax.broadcasted_iota(jnp.int32, (ts, ts), 1)
    tri = (col < row).astype(jnp.bfloat16)
    before = jnp.dot(tri, onehot.astype(jnp.bfloat16), preferred_element_type=jnp.float32) + carry[...]
    rank1 = jnp.sum(jnp.where(sel1, before, 0.0), axis=-1, keepdims=True)
    rank2 = jnp.sum(jnp.where(sel2, before, 0.0), axis=-1, keepdims=True)
    carry[...] = carry[...] + jnp.sum(onehot, axis=0, keepdims=True)
    counts_ref[...] = carry[...]

    rec = jnp.zeros((ts, LANES), jnp.float32)
    for k, val in ((R_ID1, id1.astype(jnp.float32)), (R_ID2, id2.astype(jnp.float32)), (R_W1, w1), (R_W2, w2),
                   (R_RANK1, rank1), (R_RANK2, rank2)):
        rec = jnp.where(lane == k, val, rec)
    route_ref[...] = rec


def _run_mixer(x2, p2, lnin_g, lnin_b, w_in, conv_w, conv_b, pool_w, pool_scale, w_out, ln1_g, ln1_b,
               w_r, b_r, w_pg, b_pg, w_ple, batch, seq):
    n_tok = batch * seq
    n_s = seq // SEQ_TILE
    tok_map = lambda b, s: (b * n_s + s, 0)

    def const(shape):
        return pl.BlockSpec(shape, lambda b, s: (0,) * len(shape), pipeline_mode=pl.Buffered(1))

    in_specs = [
        pl.BlockSpec((SEQ_TILE, D_MODEL), tok_map),
        pl.BlockSpec((SEQ_TILE, PLE_DIM), tok_map),
        const((1, D_MODEL)), const((1, D_MODEL)),
        const((D_MODEL, IN_PROJ)),
        const((CONV_K, CONV_WIDTH)), const((1, CONV_WIDTH)),
        const((len(POOL_WINDOWS), POOL_GW, POOL_GW)), const((1, POOL_WIDTH)),
        const((D_MODEL, D_MODEL)),
        const((1, D_MODEL)), const((1, D_MODEL)),
        const((3 * D_MODEL, LANES)), const((1, LANES)),
        const((D_MODEL, D_MODEL)), const((1, D_MODEL)),
        const((PLE_DIM, D_MODEL)),
    ]
    out_specs = [
        pl.BlockSpec((SEQ_TILE, HALF), tok_map),
        pl.BlockSpec((SEQ_TILE, D_MODEL), tok_map),
        pl.BlockSpec((SEQ_TILE, LANES), tok_map),
        pl.BlockSpec((1, LANES), lambda b, s: (0, 0)),
    ]
    out_shape = [
        jax.ShapeDtypeStruct((n_tok, HALF), jnp.uint32),
        jax.ShapeDtypeStruct((n_tok, D_MODEL), jnp.float32),
        jax.ShapeDtypeStruct((n_tok, LANES), jnp.float32),
        jax.ShapeDtypeStruct((1, LANES), jnp.float32),
    ]
    return pl.pallas_call(
        _mixer_kernel,
        grid=(batch, n_s),
        in_specs=in_specs,
        out_specs=out_specs,
        out_shape=out_shape,
        scratch_shapes=[
            pltpu.VMEM((SEQ_TILE + CONV_HALO, CONV_WIDTH), jnp.float32),
            pltpu.VMEM((SEQ_TILE + POOL_HALO, POOL_WIDTH), jnp.float32),
            pltpu.VMEM((1, LANES), jnp.float32),
        ],
        compiler_params=pltpu.CompilerParams(
            dimension_semantics=("arbitrary", "arbitrary"), vmem_limit_bytes=VMEM_LIMIT),
    )(x2, p2, lnin_g, lnin_b, w_in, conv_w, conv_b, pool_w, pool_scale, w_out, ln1_g, ln1_b,
      w_r, b_r, w_pg, b_pg, w_ple)


def _gather_rows(src, idx):
    out_rows, out_width = idx.shape[0], src.shape[1]
    width = out_width // GATHER_SPLIT
    src = src.reshape(src.shape[0] * GATHER_SPLIT, width)
    idx = (idx[:, None] * GATHER_SPLIT + jnp.arange(GATHER_SPLIT, dtype=jnp.int32)[None, :]).reshape(-1)
    n_rows = idx.shape[0]
    mesh = plsc.VectorSubcoreMesh(core_axis_name="core", subcore_axis_name="subcore")

    @functools.partial(pl.kernel, out_type=jax.ShapeDtypeStruct((n_rows, width), src.dtype), mesh=mesh)
    def gather(src_hbm, idx_hbm, dst_hbm):
        def body(idx_vmem, dst_vmem):
            pltpu.sync_copy(src_hbm.at[idx_vmem.at[0]], dst_vmem)

        pltpu.emit_pipeline(
            body,
            grid=(n_rows // GATHER_WINDOW,),
            in_specs=[pl.BlockSpec((1, GATHER_WINDOW), index_map=lambda i: (0, i))],
            out_specs=[pl.BlockSpec((GATHER_WINDOW, width), index_map=lambda i: (i, 0))],
            core_axis_name=("core", "subcore"),
            dimension_semantics=(pltpu.PARALLEL,),
        )(idx_hbm, dst_hbm)

    return gather(src, idx.reshape(1, n_rows)).reshape(out_rows, out_width)


def _expert_kernel(te_ref, nt_ref, xs_ref, wg_ref, wu_ref, wd_ref, y_ref, wgu_bf, wd_bf):
    i = pl.program_id(0)

    @pl.when(i < nt_ref[0])
    def _():
        prev = te_ref[jnp.maximum(i - 1, 0)]

        @pl.when((i == 0) | (te_ref[i] != prev))
        def _():
            wgu_bf[:, 0:D_EXPERT] = wg_ref[0].astype(jnp.bfloat16)
            wgu_bf[:, D_EXPERT:] = wu_ref[0].astype(jnp.bfloat16)
            wd_bf[...] = wd_ref[0].astype(jnp.bfloat16)

        lo, hi = _unpack_bf16_pairs(xs_ref[...])
        hgu = (jnp.dot(lo.astype(jnp.bfloat16), wgu_bf[0:HALF, :], preferred_element_type=jnp.float32)
               + jnp.dot(hi.astype(jnp.bfloat16), wgu_bf[HALF:, :], preferred_element_type=jnp.float32))
        hg = hgu[:, 0:D_EXPERT]
        act = hg * jax.nn.sigmoid(hg) * hgu[:, D_EXPERT:]
        y = jnp.dot(act.astype(jnp.bfloat16), wd_bf[...], preferred_element_type=jnp.float32)
        y_ref[...] = _pack_bf16_pairs(y)

    @pl.when(i >= nt_ref[0])
    def _():
        y_ref[...] = jnp.zeros_like(y_ref)


def _run_experts(tile_expert, n_tiles, x_sorted, w_gate, w_up, w_down):
    n_rows = x_sorted.shape[0]
    max_tiles = n_rows // ROW_TILE

    def row_map(i, te, nt):
        return (jnp.minimum(i, nt[0] - 1), 0)

    def w_map(i, te, nt):
        return (te[i], 0, 0)

    grid_spec = pltpu.PrefetchScalarGridSpec(
        num_scalar_prefetch=2,
        grid=(max_tiles,),
        in_specs=[
            pl.BlockSpec((ROW_TILE, HALF), row_map),
            pl.BlockSpec((1, D_MODEL, D_EXPERT), w_map),
            pl.BlockSpec((1, D_MODEL, D_EXPERT), w_map),
            pl.BlockSpec((1, D_EXPERT, D_MODEL), w_map),
        ],
        out_specs=pl.BlockSpec((ROW_TILE, HALF), lambda i, te, nt: (i, 0)),
        scratch_shapes=[
            pltpu.VMEM((D_MODEL, 2 * D_EXPERT), jnp.bfloat16),
            pltpu.VMEM((D_EXPERT, D_MODEL), jnp.bfloat16),
        ],
    )
    return pl.pallas_call(
        _expert_kernel,
        grid_spec=grid_spec,
        out_shape=jax.ShapeDtypeStruct((n_rows, HALF), jnp.uint32),
        compiler_params=pltpu.CompilerParams(
            dimension_semantics=("arbitrary",), vmem_limit_bytes=VMEM_LIMIT),
    )(tile_expert, n_tiles, x_sorted, w_gate, w_up, w_down)


def _combine_kernel(r_ref, route_ref, ya_ref, yb_ref, g_ref, b_ref, o_ref):
    route = route_ref[...]
    w1 = route[:, R_W1:R_W1 + 1]
    w2 = route[:, R_W2:R_W2 + 1]
    a_lo, a_hi = _unpack_bf16_pairs(ya_ref[...])
    b_lo, b_hi = _unpack_bf16_pairs(yb_ref[...])
    moe = jnp.concatenate([w1 * a_lo + w2 * b_lo, w1 * a_hi + w2 * b_hi], axis=-1)
    o_ref[...] = _layernorm(r_ref[...] + moe, g_ref[...], b_ref[...])


def _run_combine(r, route, y_tok, ln2_g, ln2_b):
    n_tok = r.shape[0]
    n_t = n_tok // TOKEN_TILE
    return pl.pallas_call(
        _combine_kernel,
        grid=(n_t,),
        in_specs=[
            pl.BlockSpec((TOKEN_TILE, D_MODEL), lambda i: (i, 0)),
            pl.BlockSpec((TOKEN_TILE, LANES), lambda i: (i, 0)),
            pl.BlockSpec((TOKEN_TILE, HALF), lambda i: (i, 0)),
            pl.BlockSpec((TOKEN_TILE, HALF), lambda i: (i + n_t, 0)),
            pl.BlockSpec((1, D_MODEL), lambda i: (0, 0)),
            pl.BlockSpec((1, D_MODEL), lambda i: (0, 0)),
        ],
        out_specs=pl.BlockSpec((TOKEN_TILE, D_MODEL), lambda i: (i, 0)),
        out_shape=jax.ShapeDtypeStruct((n_tok, D_MODEL), jnp.float32),
        compiler_params=pltpu.CompilerParams(dimension_semantics=("arbitrary",)),
    )(r, route, y_tok, y_tok, ln2_g, ln2_b)


def _split_bf16(w):
    hi = w.astype(jnp.bfloat16)
    lo = (w - hi.astype(jnp.float32)).astype(jnp.bfloat16)
    return hi, lo


def kernel(x, p, ln_in_g, ln_in_b, w_in, conv_w, conv_b, pool_w, pool_scale, w_out, ln1_g, ln1_b,
           w_rg, b_rg, w_re, b_re, w_gate, w_up, w_down, w_pg, b_pg, w_ple, ln2_g, ln2_b):
    batch, seq, _ = x.shape
    n_tok = batch * seq
    bf = jnp.bfloat16
    row = lambda v: v.reshape(1, -1)

    w_r = jnp.concatenate([w_rg[0], jnp.transpose(w_re[0], (1, 0, 2)).reshape(D_MODEL, N_EXPERTS)], axis=1)
    w_r = jnp.pad(w_r, ((0, 0), (0, LANES - w_r.shape[1])))
    w_r_hi, w_r_lo = _split_bf16(w_r)
    w_r_cat = jnp.concatenate([w_r_hi, w_r_hi, w_r_lo], axis=0)
    b_r = jnp.pad(jnp.concatenate([b_rg[0], b_re[0].reshape(-1)]), (0, LANES - N_GROUPS - N_EXPERTS)).reshape(1, LANES)

    xp, r, route, counts = _run_mixer(
        x.reshape(n_tok, D_MODEL), p[0].reshape(n_tok, PLE_DIM), row(ln_in_g), row(ln_in_b),
        w_in[0].astype(bf), conv_w[0], row(conv_b[0]), pool_w[0].astype(bf), row(pool_scale[0]),
        w_out[0].astype(bf), row(ln1_g[0]), row(ln1_b[0]), w_r_cat, b_r,
        w_pg[0].astype(bf), row(b_pg[0]), w_ple[0].astype(bf), batch, seq)

    counts = counts[0, :N_EXPERTS].astype(jnp.int32)
    tiles_per_expert = (counts + ROW_TILE - 1) // ROW_TILE
    tile_end = jnp.cumsum(tiles_per_expert)
    tile_start = tile_end - tiles_per_expert
    n_tiles = tile_end[-1:]
    n_rows = ((2 * n_tok + N_EXPERTS * (ROW_TILE - 1)) // GATHER_BLOCK + 1) * GATHER_BLOCK
    max_tiles = n_rows // ROW_TILE
    tile_ids = jnp.minimum(jnp.arange(max_tiles, dtype=jnp.int32), n_tiles[0] - 1)
    tile_expert = jnp.sum((tile_end[None, :] <= tile_ids[:, None]).astype(jnp.int32), axis=1)
    id1 = route[:, R_ID1].astype(jnp.int32)
    id2 = route[:, R_ID2].astype(jnp.int32)
    pos1 = tile_start[id1] * ROW_TILE + route[:, R_RANK1].astype(jnp.int32)
    pos2 = tile_start[id2] * ROW_TILE + route[:, R_RANK2].astype(jnp.int32)
    tok = jnp.arange(n_tok, dtype=jnp.int32)
    tok_sorted = jnp.zeros((n_rows,), jnp.int32).at[jnp.concatenate([pos1, pos2])].set(jnp.concatenate([tok, tok]))

    x_sorted = _gather_rows(xp, tok_sorted)
    y_sorted = _run_experts(tile_expert, n_tiles, x_sorted,
                            w_gate[0].reshape(N_EXPERTS, D_MODEL, D_EXPERT),
                            w_up[0].reshape(N_EXPERTS, D_MODEL, D_EXPERT),
                            w_down[0].reshape(N_EXPERTS, D_EXPERT, D_MODEL))
    y_tok = _gather_rows(y_sorted, jnp.concatenate([pos1, pos2]))
    out = _run_combine(r, route, y_tok, row(ln2_g[0]), row(ln2_b[0]))
    return out.reshape(batch, seq, D_MODEL)
```

```python
import functools

import jax
import jax.numpy as jnp
from jax import lax
from jax.experimental import pallas as pl
from jax.experimental.pallas import tpu as pltpu
from jax.experimental.pallas import tpu_sc as plsc

D_MODEL = 1024
CONV_WIDTH = 512
CONV_K = 3
POOL_WIDTH = 512
POOL_WINDOWS = (2, 4, 8, 16)
POOL_GW = 128
IN_PROJ = 3 * CONV_WIDTH + POOL_WIDTH
N_GROUPS = 4
EXPERTS_PER_GROUP = 8
N_EXPERTS = N_GROUPS * EXPERTS_PER_GROUP
D_EXPERT = 256
PLE_DIM = 256
LN_EPS = 1e-5
DEEPNORM_ALPHA = 2.0 ** 0.25

LANES = 128
HALF = D_MODEL // 2
CONV_HALO = 8
POOL_HALO = 16
SEQ_TILE = 512
ROW_TILE = 256
SC_WORKERS = 32
SC_WINDOW = 64
TOKEN_TILE = 512
VMEM_LIMIT = 56 * 1024 * 1024

R_ID1, R_ID2, R_RANK1, R_RANK2, R_W1, R_W2 = range(6)
ROUTE_ROWS = 8


def _layernorm(x, g, b):
    mu = jnp.mean(x, axis=-1, keepdims=True)
    xc = x - mu
    var = jnp.mean(xc * xc, axis=-1, keepdims=True)
    return xc * lax.rsqrt(var + LN_EPS) * g + b


def _pack_bf16_pairs(v):
    bits = lax.bitcast_convert_type(v.astype(jnp.bfloat16).astype(jnp.float32), jnp.uint32)
    return bits[:, HALF:] | (bits[:, :HALF] >> 16)


def _unpack_bf16_pairs(w):
    lo = lax.bitcast_convert_type(w << 16, jnp.float32)
    hi = lax.bitcast_convert_type(w & jnp.uint32(0xFFFF0000), jnp.float32)
    return lo, hi


def _mixer_kernel(x_ref, p_ref, lnin_g, lnin_b, w_in, conv_w, conv_b, pool_w, pool_scale, w_out,
                  ln1_g, ln1_b, w_r, b_r, w_pg, b_pg, w_ple,
                  xp_ref, r_ref, route_ref, route_t_ref, counts_ref,
                  zbuf, vbuf, carry):
    b = pl.program_id(0)
    s = pl.program_id(1)
    ts = x_ref.shape[0]

    @pl.when(s == 0)
    def _():
        zbuf[0:CONV_HALO, :] = jnp.zeros((CONV_HALO, CONV_WIDTH), jnp.float32)
        vbuf[0:POOL_HALO, :] = jnp.zeros((POOL_HALO, POOL_WIDTH), jnp.float32)

    @pl.when((b == 0) & (s == 0))
    def _():
        carry[...] = jnp.zeros_like(carry)

    h0 = _layernorm(x_ref[...], lnin_g[...], lnin_b[...])
    u = jnp.dot(h0.astype(jnp.bfloat16), w_in[...], preferred_element_type=jnp.float32)
    b_g = u[:, 0:CONV_WIDTH]
    c_g = u[:, CONV_WIDTH:2 * CONV_WIDTH]
    v_c = u[:, 2 * CONV_WIDTH:3 * CONV_WIDTH]
    v_p = u[:, 3 * CONV_WIDTH:]

    zbuf[CONV_HALO:, :] = c_g * v_c
    zext = zbuf[...]
    z1 = pltpu.roll(zext, 1, axis=0)[CONV_HALO:, :]
    z2 = pltpu.roll(zext, 2, axis=0)[CONV_HALO:, :]
    conv = z2 * conv_w[0:1, :] + z1 * conv_w[1:2, :] + zext[CONV_HALO:, :] * conv_w[2:3, :] + conv_b[...]
    y_conv = b_g * conv
    zbuf[0:CONV_HALO, :] = zbuf[ts:ts + CONV_HALO, :]

    vbuf[POOL_HALO:, :] = v_p
    vext = vbuf[...]
    s2 = vext + pltpu.roll(vext, 1, axis=0)
    s4 = s2[:, POOL_GW:] + pltpu.roll(s2[:, POOL_GW:], 2, axis=0)
    s8 = s4[:, POOL_GW:] + pltpu.roll(s4[:, POOL_GW:], 4, axis=0)
    s16 = s8[:, POOL_GW:] + pltpu.roll(s8[:, POOL_GW:], 8, axis=0)
    wsums = (s2[POOL_HALO:, 0:POOL_GW], s4[POOL_HALO:, 0:POOL_GW],
             s8[POOL_HALO:, 0:POOL_GW], s16[POOL_HALO:, 0:POOL_GW])
    vbuf[0:POOL_HALO, :] = vbuf[ts:ts + POOL_HALO, :]

    t_pos = s * ts + lax.broadcasted_iota(jnp.int32, (ts, 1), 0)
    y_pool = []
    for j, w in enumerate(POOL_WINDOWS):
        inv_cnt = 1.0 / jnp.minimum(t_pos + 1, w).astype(jnp.float32)
        pooled = wsums[j] * inv_cnt - v_p[:, j * POOL_GW:(j + 1) * POOL_GW]
        y_pool.append(jnp.dot(pooled.astype(jnp.bfloat16), pool_w[j], preferred_element_type=jnp.float32))
    y_pool = jnp.concatenate(y_pool, axis=-1) * pool_scale[...]

    ycat = jnp.concatenate([y_conv, y_pool], axis=-1).astype(jnp.bfloat16)
    mix = jnp.dot(ycat, w_out[...], preferred_element_type=jnp.float32)
    h1 = _layernorm(DEEPNORM_ALPHA * h0 + mix, ln1_g[...], ln1_b[...])

    h_hi = h1.astype(jnp.bfloat16)
    h_lo = (h1 - h_hi.astype(jnp.float32)).astype(jnp.bfloat16)
    xp_ref[...] = _pack_bf16_pairs(h1)

    gate = jax.nn.sigmoid(jnp.dot(h_hi, w_pg[...], preferred_element_type=jnp.float32) + b_pg[...])
    ple = jnp.dot(p_ref[...].astype(jnp.bfloat16), w_ple[...], preferred_element_type=jnp.float32) * gate
    r_ref[...] = DEEPNORM_ALPHA * h1 + ple

    hcat = jnp.concatenate([h_hi, h_lo, h_hi], axis=-1)
    logits = jnp.dot(hcat, w_r[...], preferred_element_type=jnp.float32) + b_r[...]
    lane = lax.broadcasted_iota(jnp.int32, (ts, LANES), 1)
    neg = jnp.float32(-jnp.inf)

    def first_argmax(vals):
        m = jnp.max(vals, axis=-1, keepdims=True)
        idx = jnp.min(jnp.where(vals == m, lane, LANES), axis=-1, keepdims=True)
        return m, idx

    g_mask = lane < N_GROUPS
    g_max, g_idx = first_argmax(jnp.where(g_mask, logits, neg))
    g_w = 1.0 / jnp.sum(jnp.where(g_mask, jnp.exp(logits - g_max), 0.0), axis=-1, keepdims=True)

    e_lo = N_GROUPS + EXPERTS_PER_GROUP * g_idx
    e_vals = jnp.where((lane >= e_lo) & (lane < e_lo + EXPERTS_PER_GROUP), logits, neg)
    m1, i1 = first_argmax(e_vals)
    m2, i2 = first_argmax(jnp.where(lane == i1, neg, e_vals))
    e21 = jnp.exp(m2 - m1)
    w1 = g_w / (1.0 + e21)
    w2 = g_w * e21 / (1.0 + e21)
    id1 = i1 - N_GROUPS
    id2 = i2 - N_GROUPS

    sel1 = lane == id1
    sel2 = lane == id2
    onehot = (sel1 | sel2).astype(jnp.float32)
    row = lax.broadcasted_iota(jnp.int32, (ts, ts), 0)
    col = lax.broadcasted_iota(jnp.int32, (ts, ts), 1)
    tri = (col < row).astype(jnp.bfloat16)
    before = jnp.dot(tri, onehot.astype(jnp.bfloat16), preferred_element_type=jnp.float32) + carry[...]
    rank1 = jnp.sum(jnp.where(sel1, before, 0.0), axis=-1, keepdims=True)
    rank2 = jnp.sum(jnp.where(sel2, before, 0.0), axis=-1, keepdims=True)
    carry[...] = carry[...] + jnp.sum(onehot, axis=0, keepdims=True)
    counts_ref[...] = carry[...]

    rec = jnp.zeros((ts, LANES), jnp.float32)
    for k, val in ((R_ID1, id1.astype(jnp.float32)), (R_ID2, id2.astype(jnp.float32)), (R_W1, w1), (R_W2, w2),
                   (R_RANK1, rank1), (R_RANK2, rank2)):
        rec = jnp.where(lane == k, val, rec)
    route_ref[...] = rec
    route_t_ref[...] = jnp.transpose(rec)[0:ROUTE_ROWS, :]


def _run_mixer(x2, p2, lnin_g, lnin_b, w_in, conv_w, conv_b, pool_w, pool_scale, w_out, ln1_g, ln1_b,
               w_r, b_r, w_pg, b_pg, w_ple, batch, seq):
    n_tok = batch * seq
    n_s = seq // SEQ_TILE
    tok_map = lambda b, s: (b * n_s + s, 0)

    def const(shape):
        return pl.BlockSpec(shape, lambda b, s: (0,) * len(shape), pipeline_mode=pl.Buffered(1))

    in_specs = [
        pl.BlockSpec((SEQ_TILE, D_MODEL), tok_map),
        pl.BlockSpec((SEQ_TILE, PLE_DIM), tok_map),
        const((1, D_MODEL)), const((1, D_MODEL)),
        const((D_MODEL, IN_PROJ)),
        const((CONV_K, CONV_WIDTH)), const((1, CONV_WIDTH)),
        const((len(POOL_WINDOWS), POOL_GW, POOL_GW)), const((1, POOL_WIDTH)),
        const((D_MODEL, D_MODEL)),
        const((1, D_MODEL)), const((1, D_MODEL)),
        const((3 * D_MODEL, LANES)), const((1, LANES)),
        const((D_MODEL, D_MODEL)), const((1, D_MODEL)),
        const((PLE_DIM, D_MODEL)),
    ]
    out_specs = [
        pl.BlockSpec((SEQ_TILE, HALF), tok_map),
        pl.BlockSpec((SEQ_TILE, D_MODEL), tok_map),
        pl.BlockSpec((SEQ_TILE, LANES), tok_map),
        pl.BlockSpec((ROUTE_ROWS, SEQ_TILE), lambda b, s: (0, b * n_s + s)),
        pl.BlockSpec((1, LANES), lambda b, s: (0, 0)),
    ]
    out_shape = [
        jax.ShapeDtypeStruct((n_tok, HALF), jnp.uint32),
        jax.ShapeDtypeStruct((n_tok, D_MODEL), jnp.float32),
        jax.ShapeDtypeStruct((n_tok, LANES), jnp.float32),
        jax.ShapeDtypeStruct((ROUTE_ROWS, n_tok), jnp.float32),
        jax.ShapeDtypeStruct((1, LANES), jnp.float32),
    ]
    return pl.pallas_call(
        _mixer_kernel,
        grid=(batch, n_s),
        in_specs=in_specs,
        out_specs=out_specs,
        out_shape=out_shape,
        scratch_shapes=[
            pltpu.VMEM((SEQ_TILE + CONV_HALO, CONV_WIDTH), jnp.float32),
            pltpu.VMEM((SEQ_TILE + POOL_HALO, POOL_WIDTH), jnp.float32),
            pltpu.VMEM((1, LANES), jnp.float32),
        ],
        compiler_params=pltpu.CompilerParams(
            dimension_semantics=("arbitrary", "arbitrary"), vmem_limit_bytes=VMEM_LIMIT),
    )(x2, p2, lnin_g, lnin_b, w_in, conv_w, conv_b, pool_w, pool_scale, w_out, ln1_g, ln1_b,
      w_r, b_r, w_pg, b_pg, w_ple)


def _plan_kernel(rt_ref, counts_ref, pos_ref, te_ref, nt_ref, pad_ref, *, n_rows):
    lane = lax.broadcasted_iota(jnp.int32, (ROUTE_ROWS, LANES), 1)
    counts = jnp.broadcast_to(counts_ref[...], (ROUTE_ROWS, LANES))
    tiles = jnp.floor((counts + (ROW_TILE - 1)) * (1.0 / ROW_TILE))
    tile_end = tiles
    shift = 1
    while shift < N_EXPERTS:
        tile_end = tile_end + jnp.where(lane >= shift, pltpu.roll(tile_end, shift, axis=1), 0.0)
        shift *= 2
    row_start = (tile_end - tiles) * ROW_TILE
    n_tiles = tile_end[0:1, N_EXPERTS - 1:N_EXPERTS]
    nt_ref[...] = jnp.broadcast_to(n_tiles, (1, LANES)).astype(jnp.int32)

    rt = rt_ref[...]
    ids = rt[R_ID1:R_ID2 + 1, :]
    start = jnp.zeros_like(ids)
    for e in range(N_EXPERTS):
        start = jnp.where(ids == e, row_start[0:1, e:e + 1], start)
    pos_ref[...] = (start + rt[R_RANK1:R_RANK2 + 1, :]).astype(jnp.int32)

    tile_id = lax.broadcasted_iota(jnp.int32, te_ref.shape, 1).astype(jnp.float32)
    tile_id = jnp.minimum(tile_id, n_tiles - 1.0)
    te = jnp.zeros(te_ref.shape, jnp.float32)
    for e in range(N_EXPERTS):
        te = te + (tile_end[0:1, e:e + 1] <= tile_id).astype(jnp.float32)
    te_ref[...] = te.astype(jnp.int32)

    sub = lax.broadcasted_iota(jnp.int32, (N_EXPERTS, LANES), 0)
    lane_e = lax.broadcasted_iota(jnp.int32, (N_EXPERTS, LANES), 1)
    diag = sub == lane_e
    pad_lo = jnp.sum(jnp.where(diag, (row_start + counts)[0:1, :], 0.0), axis=1, keepdims=True)
    pad_n = jnp.sum(jnp.where(diag, (tiles * ROW_TILE - counts)[0:1, :], 0.0), axis=1, keepdims=True)
    j = lax.broadcasted_iota(jnp.int32, (N_EXPERTS, ROW_TILE), 1).astype(jnp.float32)
    pad_ref[...] = jnp.where(j < pad_n, pad_lo + j, n_rows + j).astype(jnp.int32)


def _run_plan(route_t, counts, n_rows):
    n_tok = route_t.shape[1]
    max_tiles = n_rows // ROW_TILE
    te_lanes = -(-max_tiles // LANES) * LANES
    full = lambda shape: pl.BlockSpec(shape, lambda i: (0, 0))
    return pl.pallas_call(
        functools.partial(_plan_kernel, n_rows=n_rows),
        grid=(1,),
        in_specs=[full((ROUTE_ROWS, n_tok)), full((1, LANES))],
        out_specs=[full((2, n_tok)), full((1, te_lanes)), full((1, LANES)), full((N_EXPERTS, ROW_TILE))],
        out_shape=[
            jax.ShapeDtypeStruct((2, n_tok), jnp.int32),
            jax.ShapeDtypeStruct((1, te_lanes), jnp.int32),
            jax.ShapeDtypeStruct((1, LANES), jnp.int32),
            jax.ShapeDtypeStruct((N_EXPERTS, ROW_TILE), jnp.int32),
        ],
        compiler_params=pltpu.CompilerParams(dimension_semantics=("arbitrary",)),
    )(route_t, counts)


def _sc_mesh():
    return plsc.VectorSubcoreMesh(core_axis_name="core", subcore_axis_name="subcore")


def _sc_worker_id():
    return lax.axis_index("core") * (SC_WORKERS // 2) + lax.axis_index("subcore")


def _dispatch_rows(xp, pos1, pos2, pad_pos, zero_rows, n_rows):
    n_tok, width = xp.shape
    per_worker = n_tok // SC_WORKERS
    pads_per_worker = pad_pos.shape[0] // SC_WORKERS

    @functools.partial(
        pl.kernel, out_type=jax.ShapeDtypeStruct((n_rows + ROW_TILE, width), xp.dtype), mesh=_sc_mesh(),
        scratch_types=[pltpu.VMEM((SC_WINDOW,), jnp.int32), pltpu.VMEM((SC_WINDOW,), jnp.int32),
                       pltpu.VMEM((SC_WINDOW, width), xp.dtype)])
    def dispatch(xp_hbm, pos1_hbm, pos2_hbm, pad_hbm, zero_hbm, out_hbm, idx1, idx2, buf):
        wid = _sc_worker_id()

        @pl.loop(0, per_worker // SC_WINDOW)
        def _(j):
            base = wid * per_worker + j * SC_WINDOW
            pltpu.sync_copy(pos1_hbm.at[pl.ds(base, SC_WINDOW)], idx1)
            pltpu.sync_copy(pos2_hbm.at[pl.ds(base, SC_WINDOW)], idx2)
            pltpu.sync_copy(xp_hbm.at[pl.ds(base, SC_WINDOW)], buf)
            pltpu.sync_copy(buf, out_hbm.at[idx1])
            pltpu.sync_copy(buf, out_hbm.at[idx2])

        pltpu.sync_copy(zero_hbm, buf)

        @pl.loop(0, pads_per_worker // SC_WINDOW)
        def _(j):
            base = wid * pads_per_worker + j * SC_WINDOW
            pltpu.sync_copy(pad_hbm.at[pl.ds(base, SC_WINDOW)], idx1)
            pltpu.sync_copy(buf, out_hbm.at[idx1])

    return dispatch(xp, pos1, pos2, pad_pos, zero_rows)


def _gather_rows(src, idx):
    n_out, width = idx.shape[0], src.shape[1]
    per_worker = n_out // SC_WORKERS

    @functools.partial(
        pl.kernel, out_type=jax.ShapeDtypeStruct((n_out, width), src.dtype), mesh=_sc_mesh(),
        scratch_types=[pltpu.VMEM((SC_WINDOW,), jnp.int32), pltpu.VMEM((SC_WINDOW, width), src.dtype)])
    def gather(src_hbm, idx_hbm, dst_hbm, idx_v, buf):
        wid = _sc_worker_id()

        @pl.loop(0, per_worker // SC_WINDOW)
        def _(j):
            base = wid * per_worker + j * SC_WINDOW
            pltpu.sync_copy(idx_hbm.at[pl.ds(base, SC_WINDOW)], idx_v)
            pltpu.sync_copy(src_hbm.at[idx_v], buf)
            pltpu.sync_copy(buf, dst_hbm.at[pl.ds(base, SC_WINDOW)])

    return gather(src, idx)


def _expert_kernel(te_ref, nt_ref, xs_ref, wg_ref, wu_ref, wd_ref, y_ref, wgu_bf, wd_bf):
    i = pl.program_id(0)

    @pl.when(i < nt_ref[0])
    def _():
        prev = te_ref[jnp.maximum(i - 1, 0)]

        @pl.when((i == 0) | (te_ref[i] != prev))
        def _():
            wgu_bf[:, 0:D_EXPERT] = wg_ref[0].astype(jnp.bfloat16)
            wgu_bf[:, D_EXPERT:] = wu_ref[0].astype(jnp.bfloat16)
            wd_bf[...] = wd_ref[0].astype(jnp.bfloat16)

        lo, hi = _unpack_bf16_pairs(xs_ref[...])
        hgu = (jnp.dot(lo.astype(jnp.bfloat16), wgu_bf[0:HALF, :], preferred_element_type=jnp.float32)
               + jnp.dot(hi.astype(jnp.bfloat16), wgu_bf[HALF:, :], preferred_element_type=jnp.float32))
        hg = hgu[:, 0:D_EXPERT]
        act = hg * jax.nn.sigmoid(hg) * hgu[:, D_EXPERT:]
        y = jnp.dot(act.astype(jnp.bfloat16), wd_bf[...], preferred_element_type=jnp.float32)
        y_ref[...] = _pack_bf16_pairs(y)

    @pl.when(i >= nt_ref[0])
    def _():
        y_ref[...] = jnp.zeros_like(y_ref)


def _run_experts(tile_expert, n_tiles, x_sorted, w_gate, w_up, w_down):
    n_rows = x_sorted.shape[0]
    max_tiles = n_rows // ROW_TILE

    def row_map(i, te, nt):
        return (jnp.minimum(i, nt[0] - 1), 0)

    def w_map(i, te, nt):
        return (te[i], 0, 0)

    grid_spec = pltpu.PrefetchScalarGridSpec(
        num_scalar_prefetch=2,
        grid=(max_tiles,),
        in_specs=[
            pl.BlockSpec((ROW_TILE, HALF), row_map),
            pl.BlockSpec((1, D_MODEL, D_EXPERT), w_map),
            pl.BlockSpec((1, D_MODEL, D_EXPERT), w_map),
            pl.BlockSpec((1, D_EXPERT, D_MODEL), w_map),
        ],
        out_specs=pl.BlockSpec((ROW_TILE, HALF), lambda i, te, nt: (i, 0)),
        scratch_shapes=[
            pltpu.VMEM((D_MODEL, 2 * D_EXPERT), jnp.bfloat16),
            pltpu.VMEM((D_EXPERT, D_MODEL), jnp.bfloat16),
        ],
    )
    return pl.pallas_call(
        _expert_kernel,
        grid_spec=grid_spec,
        out_shape=jax.ShapeDtypeStruct((n_rows, HALF), jnp.uint32),
        compiler_params=pltpu.CompilerParams(
            dimension_semantics=("arbitrary",), vmem_limit_bytes=VMEM_LIMIT),
    )(tile_expert, n_tiles, x_sorted, w_gate, w_up, w_down)


def _combine_kernel(r_ref, route_ref, ya_ref, yb_ref, g_ref, b_ref, o_ref):
    route = route_ref[...]
    w1 = route[:, R_W1:R_W1 + 1]
    w2 = route[:, R_W2:R_W2 + 1]
    a_lo, a_hi = _unpack_bf16_pairs(ya_ref[...])
    b_lo, b_hi = _unpack_bf16_pairs(yb_ref[...])
    moe = jnp.concatenate([w1 * a_lo + w2 * b_lo, w1 * a_hi + w2 * b_hi], axis=-1)
    o_ref[...] = _layernorm(r_ref[...] + moe, g_ref[...], b_ref[...])


def _run_combine(r, route, y_tok, ln2_g, ln2_b):
    n_tok = r.shape[0]
    n_t = n_tok // TOKEN_TILE
    return pl.pallas_call(
        _combine_kernel,
        grid=(n_t,),
        in_specs=[
            pl.BlockSpec((TOKEN_TILE, D_MODEL), lambda i: (i, 0)),
            pl.BlockSpec((TOKEN_TILE, LANES), lambda i: (i, 0)),
            pl.BlockSpec((TOKEN_TILE, HALF), lambda i: (i, 0)),
            pl.BlockSpec((TOKEN_TILE, HALF), lambda i: (i + n_t, 0)),
            pl.BlockSpec((1, D_MODEL), lambda i: (0, 0)),
            pl.BlockSpec((1, D_MODEL), lambda i: (0, 0)),
        ],
        out_specs=pl.BlockSpec((TOKEN_TILE, D_MODEL), lambda i: (i, 0)),
        out_shape=jax.ShapeDtypeStruct((n_tok, D_MODEL), jnp.float32),
        compiler_params=pltpu.CompilerParams(dimension_semantics=("arbitrary",)),
    )(r, route, y_tok, y_tok, ln2_g, ln2_b)


def _split_bf16(w):
    hi = w.astype(jnp.bfloat16)
    lo = (w - hi.astype(jnp.float32)).astype(jnp.bfloat16)
    return hi, lo


def kernel(x, p, ln_in_g, ln_in_b, w_in, conv_w, conv_b, pool_w, pool_scale, w_out, ln1_g, ln1_b,
           w_rg, b_rg, w_re, b_re, w_gate, w_up, w_down, w_pg, b_pg, w_ple, ln2_g, ln2_b):
    batch, seq, _ = x.shape
    n_tok = batch * seq
    bf = jnp.bfloat16
    row = lambda v: v.reshape(1, -1)

    w_r = jnp.concatenate([w_rg[0], jnp.transpose(w_re[0], (1, 0, 2)).reshape(D_MODEL, N_EXPERTS)], axis=1)
    w_r = jnp.pad(w_r, ((0, 0), (0, LANES - w_r.shape[1])))
    w_r_hi, w_r_lo = _split_bf16(w_r)
    w_r_cat = jnp.concatenate([w_r_hi, w_r_hi, w_r_lo], axis=0)
    b_r = jnp.pad(jnp.concatenate([b_rg[0], b_re[0].reshape(-1)]), (0, LANES - N_GROUPS - N_EXPERTS)).reshape(1, LANES)

    xp, r, route, route_t, counts = _run_mixer(
        x.reshape(n_tok, D_MODEL), p[0].reshape(n_tok, PLE_DIM), row(ln_in_g), row(ln_in_b),
        w_in[0].astype(bf), conv_w[0], row(conv_b[0]), pool_w[0].astype(bf), row(pool_scale[0]),
        w_out[0].astype(bf), row(ln1_g[0]), row(ln1_b[0]), w_r_cat, b_r,
        w_pg[0].astype(bf), row(b_pg[0]), w_ple[0].astype(bf), batch, seq)

    n_rows = -(-(2 * n_tok + N_EXPERTS * (ROW_TILE - 1)) // ROW_TILE) * ROW_TILE
    pos, tile_expert, n_tiles, pad_pos = _run_plan(route_t, counts, n_rows)
    tile_expert = tile_expert.reshape(-1)
    n_tiles = n_tiles[0, 0:1]

    zero_rows = jnp.zeros((SC_WINDOW, HALF), jnp.uint32)
    x_sorted = _dispatch_rows(xp, pos[0], pos[1], pad_pos.reshape(-1), zero_rows, n_rows)
    y_sorted = _run_experts(tile_expert, n_tiles, x_sorted,
                            w_gate[0].reshape(N_EXPERTS, D_MODEL, D_EXPERT),
                            w_up[0].reshape(N_EXPERTS, D_MODEL, D_EXPERT),
                            w_down[0].reshape(N_EXPERTS, D_EXPERT, D_MODEL))
    y_tok = _gather_rows(y_sorted, pos.reshape(-1))
    out = _run_combine(r, route, y_tok, row(ln2_g[0]), row(ln2_b[0]))
    return out.reshape(batch, seq, D_MODEL)
```

```python
import functools

import jax
import jax.numpy as jnp
from jax import lax
from jax.experimental import pallas as pl
from jax.experimental.pallas import tpu as pltpu
from jax.experimental.pallas import tpu_sc as plsc

D_MODEL = 1024
CONV_WIDTH = 512
CONV_K = 3
POOL_WIDTH = 512
POOL_WINDOWS = (2, 4, 8, 16)
POOL_GW = 128
IN_PROJ = 3 * CONV_WIDTH + POOL_WIDTH
N_GROUPS = 4
EXPERTS_PER_GROUP = 8
N_EXPERTS = N_GROUPS * EXPERTS_PER_GROUP
D_EXPERT = 256
PLE_DIM = 256
LN_EPS = 1e-5
DEEPNORM_ALPHA = 2.0 ** 0.25

LANES = 128
HALF = D_MODEL // 2
CONV_HALO = 8
POOL_HALO = 16
SEQ_TILE = 512
SUB_TILE = 256
ROW_TILE = 256
SC_WORKERS = 32
SC_WINDOW = 64
TOKEN_TILE = 512
VMEM_LIMIT = 56 * 1024 * 1024

R_ID1, R_ID2, R_RANK1, R_RANK2, R_W1, R_W2 = range(6)
ROUTE_ROWS = 8


def _layernorm(x, g, b):
    mu = jnp.mean(x, axis=-1, keepdims=True)
    xc = x - mu
    var = jnp.mean(xc * xc, axis=-1, keepdims=True)
    return xc * lax.rsqrt(var + LN_EPS) * g + b


def _pack_bf16_pairs(v):
    bits = lax.bitcast_convert_type(v.astype(jnp.bfloat16).astype(jnp.float32), jnp.uint32)
    return bits[:, HALF:] | (bits[:, :HALF] >> 16)


def _unpack_bf16_pairs(w):
    lo = lax.bitcast_convert_type(w << 16, jnp.float32)
    hi = lax.bitcast_convert_type(w & jnp.uint32(0xFFFF0000), jnp.float32)
    return lo, hi


def _mixer_kernel(x_ref, p_ref, lnin_g, lnin_b, w_in, conv_w, conv_b, pool_w, pool_scale, w_out,
                  ln1_g, ln1_b, w_r, b_r, w_pg, b_pg, w_ple,
                  xp_ref, r_ref, route_ref, route_t_ref, counts_ref,
                  zbuf, vbuf, carry):
    b = pl.program_id(0)
    s = pl.program_id(1)
    ts = x_ref.shape[0]

    @pl.when(s == 0)
    def _():
        zbuf[0:CONV_HALO, :] = jnp.zeros((CONV_HALO, CONV_WIDTH), jnp.float32)
        vbuf[0:POOL_HALO, :] = jnp.zeros((POOL_HALO, POOL_WIDTH), jnp.float32)

    @pl.when((b == 0) & (s == 0))
    def _():
        carry[...] = jnp.zeros_like(carry)

    for r0 in range(0, ts, SUB_TILE):
        _mixer_rows(r0, s * ts + r0, x_ref, p_ref, lnin_g, lnin_b, w_in, conv_w, conv_b, pool_w, pool_scale,
                    w_out, ln1_g, ln1_b, w_r, b_r, w_pg, b_pg, w_ple,
                    xp_ref, r_ref, route_ref, route_t_ref, zbuf, vbuf, carry)
    zbuf[0:CONV_HALO, :] = zbuf[ts:ts + CONV_HALO, :]
    vbuf[0:POOL_HALO, :] = vbuf[ts:ts + POOL_HALO, :]
    counts_ref[...] = carry[...]


def _mixer_rows(r0, seq0, x_ref, p_ref, lnin_g, lnin_b, w_in, conv_w, conv_b, pool_w, pool_scale, w_out,
                ln1_g, ln1_b, w_r, b_r, w_pg, b_pg, w_ple,
                xp_ref, r_ref, route_ref, route_t_ref, zbuf, vbuf, carry):
    n = SUB_TILE
    rows = pl.ds(r0, n)
    h0 = _layernorm(x_ref[rows, :], lnin_g[...], lnin_b[...])
    u = jnp.dot(h0.astype(jnp.bfloat16), w_in[...], preferred_element_type=jnp.float32)
    b_g = u[:, 0:CONV_WIDTH]
    c_g = u[:, CONV_WIDTH:2 * CONV_WIDTH]
    v_c = u[:, 2 * CONV_WIDTH:3 * CONV_WIDTH]
    v_p = u[:, 3 * CONV_WIDTH:]

    zbuf[pl.ds(CONV_HALO + r0, n), :] = c_g * v_c
    zext = zbuf[pl.ds(r0, n + CONV_HALO), :]
    z1 = pltpu.roll(zext, 1, axis=0)[CONV_HALO:, :]
    z2 = pltpu.roll(zext, 2, axis=0)[CONV_HALO:, :]
    conv = z2 * conv_w[0:1, :] + z1 * conv_w[1:2, :] + zext[CONV_HALO:, :] * conv_w[2:3, :] + conv_b[...]
    y_conv = b_g * conv

    vbuf[pl.ds(POOL_HALO + r0, n), :] = v_p
    vext = vbuf[pl.ds(r0, n + POOL_HALO), :]
    s2 = vext + pltpu.roll(vext, 1, axis=0)
    s4 = s2[:, POOL_GW:] + pltpu.roll(s2[:, POOL_GW:], 2, axis=0)
    s8 = s4[:, POOL_GW:] + pltpu.roll(s4[:, POOL_GW:], 4, axis=0)
    s16 = s8[:, POOL_GW:] + pltpu.roll(s8[:, POOL_GW:], 8, axis=0)
    wsums = (s2[POOL_HALO:, 0:POOL_GW], s4[POOL_HALO:, 0:POOL_GW],
             s8[POOL_HALO:, 0:POOL_GW], s16[POOL_HALO:, 0:POOL_GW])

    t_pos = seq0 + lax.broadcasted_iota(jnp.int32, (n, 1), 0)
    y_pool = []
    for j, w in enumerate(POOL_WINDOWS):
        inv_cnt = 1.0 / jnp.minimum(t_pos + 1, w).astype(jnp.float32)
        pooled = wsums[j] * inv_cnt - v_p[:, j * POOL_GW:(j + 1) * POOL_GW]
        y_pool.append(jnp.dot(pooled.astype(jnp.bfloat16), pool_w[j], preferred_element_type=jnp.float32))
    y_pool = jnp.concatenate(y_pool, axis=-1) * pool_scale[...]

    ycat = jnp.concatenate([y_conv, y_pool], axis=-1).astype(jnp.bfloat16)
    mix = jnp.dot(ycat, w_out[...], preferred_element_type=jnp.float32)
    h1 = _layernorm(DEEPNORM_ALPHA * h0 + mix, ln1_g[...], ln1_b[...])

    h_hi = h1.astype(jnp.bfloat16)
    h_lo = (h1 - h_hi.astype(jnp.float32)).astype(jnp.bfloat16)
    xp_ref[rows, :] = _pack_bf16_pairs(h1)

    gate = jax.nn.sigmoid(jnp.dot(h_hi, w_pg[...], preferred_element_type=jnp.float32) + b_pg[...])
    ple = jnp.dot(p_ref[rows, :].astype(jnp.bfloat16), w_ple[...], preferred_element_type=jnp.float32) * gate
    r_ref[rows, :] = DEEPNORM_ALPHA * h1 + ple

    hcat = jnp.concatenate([h_hi, h_lo, h_hi], axis=-1)
    logits = jnp.dot(hcat, w_r[...], preferred_element_type=jnp.float32) + b_r[...]
    lane = lax.broadcasted_iota(jnp.int32, (n, LANES), 1)
    neg = jnp.float32(-jnp.inf)

    def first_argmax(vals):
        m = jnp.max(vals, axis=-1, keepdims=True)
        idx = jnp.min(jnp.where(vals == m, lane, LANES), axis=-1, keepdims=True)
        return m, idx

    g_mask = lane < N_GROUPS
    g_max, g_idx = first_argmax(jnp.where(g_mask, logits, neg))
    g_w = 1.0 / jnp.sum(jnp.where(g_mask, jnp.exp(logits - g_max), 0.0), axis=-1, keepdims=True)

    e_lo = N_GROUPS + EXPERTS_PER_GROUP * g_idx
    e_vals = jnp.where((lane >= e_lo) & (lane < e_lo + EXPERTS_PER_GROUP), logits, neg)
    m1, i1 = first_argmax(e_vals)
    m2, i2 = first_argmax(jnp.where(lane == i1, neg, e_vals))
    e21 = jnp.exp(m2 - m1)
    w1 = g_w / (1.0 + e21)
    w2 = g_w * e21 / (1.0 + e21)
    id1 = i1 - N_GROUPS
    id2 = i2 - N_GROUPS

    sel1 = lane == id1
    sel2 = lane == id2
    onehot = (sel1 | sel2).astype(jnp.float32)
    row = lax.broadcasted_iota(jnp.int32, (n, n), 0)
    col = lax.broadcasted_iota(jnp.int32, (n, n), 1)
    tri = (col < row).astype(jnp.bfloat16)
    before = jnp.dot(tri, onehot.astype(jnp.bfloat16), preferred_element_type=jnp.float32) + carry[...]
    rank1 = jnp.sum(jnp.where(sel1, before, 0.0), axis=-1, keepdims=True)
    rank2 = jnp.sum(jnp.where(sel2, before, 0.0), axis=-1, keepdims=True)
    carry[...] = carry[...] + jnp.sum(onehot, axis=0, keepdims=True)

    rec = jnp.zeros((n, LANES), jnp.float32)
    for k, val in ((R_ID1, id1.astype(jnp.float32)), (R_ID2, id2.astype(jnp.float32)), (R_W1, w1), (R_W2, w2),
                   (R_RANK1, rank1), (R_RANK2, rank2)):
        rec = jnp.where(lane == k, val, rec)
    route_ref[rows, :] = rec
    route_t_ref[:, rows] = jnp.transpose(rec)[0:ROUTE_ROWS, :]


def _run_mixer(x2, p2, lnin_g, lnin_b, w_in, conv_w, conv_b, pool_w, pool_scale, w_out, ln1_g, ln1_b,
               w_r, b_r, w_pg, b_pg, w_ple, batch, seq):
    n_tok = batch * seq
    n_s = seq // SEQ_TILE
    tok_map = lambda b, s: (b * n_s + s, 0)

    def const(shape):
        return pl.BlockSpec(shape, lambda b, s: (0,) * len(shape), pipeline_mode=pl.Buffered(1))

    in_specs = [
        pl.BlockSpec((SEQ_TILE, D_MODEL), tok_map),
        pl.BlockSpec((SEQ_TILE, PLE_DIM), tok_map),
        const((1, D_MODEL)), const((1, D_MODEL)),
        const((D_MODEL, IN_PROJ)),
        const((CONV_K, CONV_WIDTH)), const((1, CONV_WIDTH)),
        const((len(POOL_WINDOWS), POOL_GW, POOL_GW)), const((1, POOL_WIDTH)),
        const((D_MODEL, D_MODEL)),
        const((1, D_MODEL)), const((1, D_MODEL)),
        const((3 * D_MODEL, LANES)), const((1, LANES)),
        const((D_MODEL, D_MODEL)), const((1, D_MODEL)),
        const((PLE_DIM, D_MODEL)),
    ]
    out_specs = [
        pl.BlockSpec((SEQ_TILE, HALF), tok_map),
        pl.BlockSpec((SEQ_TILE, D_MODEL), tok_map),
        pl.BlockSpec((SEQ_TILE, LANES), tok_map),
        pl.BlockSpec((ROUTE_ROWS, SEQ_TILE), lambda b, s: (0, b * n_s + s)),
        pl.BlockSpec((1, LANES), lambda b, s: (0, 0)),
    ]
    out_shape = [
        jax.ShapeDtypeStruct((n_tok, HALF), jnp.uint32),
        jax.ShapeDtypeStruct((n_tok, D_MODEL), jnp.float32),
        jax.ShapeDtypeStruct((n_tok, LANES), jnp.float32),
        jax.ShapeDtypeStruct((ROUTE_ROWS, n_tok), jnp.float32),
        jax.ShapeDtypeStruct((1, LANES), jnp.float32),
    ]
    return pl.pallas_call(
        _mixer_kernel,
        grid=(batch, n_s),
        in_specs=in_specs,
        out_specs=out_specs,
        out_shape=out_shape,
        scratch_shapes=[
            pltpu.VMEM((SEQ_TILE + CONV_HALO, CONV_WIDTH), jnp.float32),
            pltpu.VMEM((SEQ_TILE + POOL_HALO, POOL_WIDTH), jnp.float32),
            pltpu.VMEM((1, LANES), jnp.float32),
        ],
        compiler_params=pltpu.CompilerParams(
            dimension_semantics=("arbitrary", "arbitrary"), vmem_limit_bytes=VMEM_LIMIT),
    )(x2, p2, lnin_g, lnin_b, w_in, conv_w, conv_b, pool_w, pool_scale, w_out, ln1_g, ln1_b,
      w_r, b_r, w_pg, b_pg, w_ple)


def _plan_kernel(rt_ref, counts_ref, pos_ref, tile_start_ref, tiles_ref, pad_ref, *, n_rows):
    lane = lax.broadcasted_iota(jnp.int32, (ROUTE_ROWS, LANES), 1)
    counts = jnp.broadcast_to(counts_ref[...], (ROUTE_ROWS, LANES))
    tiles = jnp.floor((counts + (ROW_TILE - 1)) * (1.0 / ROW_TILE))
    tile_end = tiles
    shift = 1
    while shift < N_EXPERTS:
        tile_end = tile_end + jnp.where(lane >= shift, pltpu.roll(tile_end, shift, axis=1), 0.0)
        shift *= 2
    row_start = (tile_end - tiles) * ROW_TILE
    tile_start_ref[...] = (tile_end - tiles)[0:1, :].astype(jnp.int32)
    tiles_ref[...] = tiles[0:1, :].astype(jnp.int32)

    rt = rt_ref[...]
    ids = rt[R_ID1:R_ID2 + 1, :]
    start = jnp.zeros_like(ids)
    for e in range(N_EXPERTS):
        start = jnp.where(ids == e, row_start[0:1, e:e + 1], start)
    pos_ref[...] = (start + rt[R_RANK1:R_RANK2 + 1, :]).astype(jnp.int32)

    sub = lax.broadcasted_iota(jnp.int32, (N_EXPERTS, LANES), 0)
    lane_e = lax.broadcasted_iota(jnp.int32, (N_EXPERTS, LANES), 1)
    diag = sub == lane_e
    pad_lo = jnp.sum(jnp.where(diag, (row_start + counts)[0:1, :], 0.0), axis=1, keepdims=True)
    pad_n = jnp.sum(jnp.where(diag, (tiles * ROW_TILE - counts)[0:1, :], 0.0), axis=1, keepdims=True)
    j = lax.broadcasted_iota(jnp.int32, (N_EXPERTS, ROW_TILE), 1).astype(jnp.float32)
    pad_ref[...] = jnp.where(j < pad_n, pad_lo + j, n_rows + j).astype(jnp.int32)


def _run_plan(route_t, counts, n_rows):
    n_tok = route_t.shape[1]
    full = lambda shape: pl.BlockSpec(shape, lambda i: (0, 0))
    return pl.pallas_call(
        functools.partial(_plan_kernel, n_rows=n_rows),
        grid=(1,),
        in_specs=[full((ROUTE_ROWS, n_tok)), full((1, LANES))],
        out_specs=[full((2, n_tok)), full((1, LANES)), full((1, LANES)), full((N_EXPERTS, ROW_TILE))],
        out_shape=[
            jax.ShapeDtypeStruct((2, n_tok), jnp.int32),
            jax.ShapeDtypeStruct((1, LANES), jnp.int32),
            jax.ShapeDtypeStruct((1, LANES), jnp.int32),
            jax.ShapeDtypeStruct((N_EXPERTS, ROW_TILE), jnp.int32),
        ],
        compiler_params=pltpu.CompilerParams(dimension_semantics=("arbitrary",)),
    )(route_t, counts)


def _sc_mesh():
    return plsc.VectorSubcoreMesh(core_axis_name="core", subcore_axis_name="subcore")


def _sc_worker_id():
    return lax.axis_index("core") * (SC_WORKERS // 2) + lax.axis_index("subcore")


def _dispatch_rows(xp, pos1, pos2, pad_pos, zero_rows, n_rows):
    n_tok, width = xp.shape
    per_worker = n_tok // SC_WORKERS
    pads_per_worker = pad_pos.shape[0] // SC_WORKERS

    @functools.partial(
        pl.kernel, out_type=jax.ShapeDtypeStruct((n_rows + ROW_TILE, width), xp.dtype), mesh=_sc_mesh(),
        scratch_types=[pltpu.VMEM((SC_WINDOW,), jnp.int32), pltpu.VMEM((SC_WINDOW,), jnp.int32),
                       pltpu.VMEM((SC_WINDOW, width), xp.dtype)])
    def dispatch(xp_hbm, pos1_hbm, pos2_hbm, pad_hbm, zero_hbm, out_hbm, idx1, idx2, buf):
        wid = _sc_worker_id()

        @pl.loop(0, per_worker // SC_WINDOW)
        def _(j):
            base = wid * per_worker + j * SC_WINDOW
            pltpu.sync_copy(pos1_hbm.at[pl.ds(base, SC_WINDOW)], idx1)
            pltpu.sync_copy(pos2_hbm.at[pl.ds(base, SC_WINDOW)], idx2)
            pltpu.sync_copy(xp_hbm.at[pl.ds(base, SC_WINDOW)], buf)
            pltpu.sync_copy(buf, out_hbm.at[idx1])
            pltpu.sync_copy(buf, out_hbm.at[idx2])

        pltpu.sync_copy(zero_hbm, buf)

        @pl.loop(0, pads_per_worker // SC_WINDOW)
        def _(j):
            base = wid * pads_per_worker + j * SC_WINDOW
            pltpu.sync_copy(pad_hbm.at[pl.ds(base, SC_WINDOW)], idx1)
            pltpu.sync_copy(buf, out_hbm.at[idx1])

    return dispatch(xp, pos1, pos2, pad_pos, zero_rows)


def _gather_rows(src, idx):
    n_out, width = idx.shape[0], src.shape[1]
    per_worker = n_out // SC_WORKERS

    @functools.partial(
        pl.kernel, out_type=jax.ShapeDtypeStruct((n_out, width), src.dtype), mesh=_sc_mesh(),
        scratch_types=[pltpu.VMEM((SC_WINDOW,), jnp.int32), pltpu.VMEM((SC_WINDOW, width), src.dtype)])
    def gather(src_hbm, idx_hbm, dst_hbm, idx_v, buf):
        wid = _sc_worker_id()

        @pl.loop(0, per_worker // SC_WINDOW)
        def _(j):
            base = wid * per_worker + j * SC_WINDOW
            pltpu.sync_copy(idx_hbm.at[pl.ds(base, SC_WINDOW)], idx_v)
            pltpu.sync_copy(src_hbm.at[idx_v], buf)
            pltpu.sync_copy(buf, dst_hbm.at[pl.ds(base, SC_WINDOW)])

    return gather(src, idx)


def _tile_copy(hbm, buf, sem, tile, slot, to_hbm):
    rows = hbm.at[pl.ds(pl.multiple_of(tile * ROW_TILE, ROW_TILE), ROW_TILE)]
    if to_hbm:
        return pltpu.make_async_copy(buf.at[slot], rows, sem.at[slot])
    return pltpu.make_async_copy(rows, buf.at[slot], sem.at[slot])


def _expert_kernel(ts_ref, nte_ref, x_hbm, wg_ref, wu_ref, wd_ref, y_hbm,
                   xbuf, ybuf, xsem, ysem, wgu_bf, wd_bf, *, max_tiles):
    e = pl.program_id(0)
    first = ts_ref[e]
    count = nte_ref[e]
    n_tiles = ts_ref[N_EXPERTS - 1] + nte_ref[N_EXPERTS - 1]

    @pl.when(e == 0)
    def _():
        _tile_copy(x_hbm, xbuf, xsem, 0, 0, False).start()

    @pl.when(count > 0)
    def _():
        wgu_bf[:, 0:D_EXPERT] = wg_ref[0].astype(jnp.bfloat16)
        wgu_bf[:, D_EXPERT:] = wu_ref[0].astype(jnp.bfloat16)
        wd_bf[...] = wd_ref[0].astype(jnp.bfloat16)

    def tile_body(g, c):
        slot = g % 2
        _tile_copy(x_hbm, xbuf, xsem, g, slot, False).wait()

        @pl.when(g + 1 < n_tiles)
        def _():
            _tile_copy(x_hbm, xbuf, xsem, g + 1, 1 - slot, False).start()

        lo, hi = _unpack_bf16_pairs(xbuf[slot])
        hgu = (jnp.dot(lo.astype(jnp.bfloat16), wgu_bf[0:HALF, :], preferred_element_type=jnp.float32)
               + jnp.dot(hi.astype(jnp.bfloat16), wgu_bf[HALF:, :], preferred_element_type=jnp.float32))
        hg = hgu[:, 0:D_EXPERT]
        act = hg * jax.nn.sigmoid(hg) * hgu[:, D_EXPERT:]
        y = jnp.dot(act.astype(jnp.bfloat16), wd_bf[...], preferred_element_type=jnp.float32)

        @pl.when(g >= 2)
        def _():
            _tile_copy(y_hbm, ybuf, ysem, g - 2, slot, True).wait()

        ybuf[slot] = _pack_bf16_pairs(y)
        _tile_copy(y_hbm, ybuf, ysem, g, slot, True).start()
        return c

    lax.fori_loop(first, first + count, tile_body, 0)

    @pl.when(e == N_EXPERTS - 1)
    def _():
        @pl.when(n_tiles >= 2)
        def _():
            _tile_copy(y_hbm, ybuf, ysem, n_tiles - 2, n_tiles % 2, True).wait()

        _tile_copy(y_hbm, ybuf, ysem, n_tiles - 1, (n_tiles - 1) % 2, True).wait()
        ybuf[0] = jnp.zeros((ROW_TILE, HALF), jnp.uint32)

        def fill(g, c):
            cp = _tile_copy(y_hbm, ybuf, ysem, g, 0, True)
            cp.start()
            cp.wait()
            return c

        lax.fori_loop(n_tiles, max_tiles, fill, 0)


def _run_experts(tile_start, tiles_per_expert, x_sorted, w_gate, w_up, w_down):
    n_rows = x_sorted.shape[0]
    max_tiles = n_rows // ROW_TILE
    w_map = lambda e, ts, nte: (e, 0, 0)

    grid_spec = pltpu.PrefetchScalarGridSpec(
        num_scalar_prefetch=2,
        grid=(N_EXPERTS,),
        in_specs=[
            pl.BlockSpec(memory_space=pl.ANY),
            pl.BlockSpec((1, D_MODEL, D_EXPERT), w_map),
            pl.BlockSpec((1, D_MODEL, D_EXPERT), w_map),
            pl.BlockSpec((1, D_EXPERT, D_MODEL), w_map),
        ],
        out_specs=pl.BlockSpec(memory_space=pl.ANY),
        scratch_shapes=[
            pltpu.VMEM((2, ROW_TILE, HALF), jnp.uint32),
            pltpu.VMEM((2, ROW_TILE, HALF), jnp.uint32),
            pltpu.SemaphoreType.DMA((2,)),
            pltpu.SemaphoreType.DMA((2,)),
            pltpu.VMEM((D_MODEL, 2 * D_EXPERT), jnp.bfloat16),
            pltpu.VMEM((D_EXPERT, D_MODEL), jnp.bfloat16),
        ],
    )
    return pl.pallas_call(
        functools.partial(_expert_kernel, max_tiles=max_tiles),
        grid_spec=grid_spec,
        out_shape=jax.ShapeDtypeStruct((n_rows, HALF), jnp.uint32),
        compiler_params=pltpu.CompilerParams(
            dimension_semantics=("arbitrary",), vmem_limit_bytes=VMEM_LIMIT),
    )(tile_start, tiles_per_expert, x_sorted, w_gate, w_up, w_down)


def _combine_kernel(r_ref, route_ref, ya_ref, yb_ref, g_ref, b_ref, o_ref):
    route = route_ref[...]
    w1 = route[:, R_W1:R_W1 + 1]
    w2 = route[:, R_W2:R_W2 + 1]
    a_lo, a_hi = _unpack_bf16_pairs(ya_ref[...])
    b_lo, b_hi = _unpack_bf16_pairs(yb_ref[...])
    moe = jnp.concatenate([w1 * a_lo + w2 * b_lo, w1 * a_hi + w2 * b_hi], axis=-1)
    o_ref[...] = _layernorm(r_ref[...] + moe, g_ref[...], b_ref[...])


def _run_combine(r, route, y_tok, ln2_g, ln2_b):
    n_tok = r.shape[0]
    n_t = n_tok // TOKEN_TILE
    return pl.pallas_call(
        _combine_kernel,
        grid=(n_t,),
        in_specs=[
            pl.BlockSpec((TOKEN_TILE, D_MODEL), lambda i: (i, 0)),
            pl.BlockSpec((TOKEN_TILE, LANES), lambda i: (i, 0)),
            pl.BlockSpec((TOKEN_TILE, HALF), lambda i: (i, 0)),
            pl.BlockSpec((TOKEN_TILE, HALF), lambda i: (i + n_t, 0)),
            pl.BlockSpec((1, D_MODEL), lambda i: (0, 0)),
            pl.BlockSpec((1, D_MODEL), lambda i: (0, 0)),
        ],
        out_specs=pl.BlockSpec((TOKEN_TILE, D_MODEL), lambda i: (i, 0)),
        out_shape=jax.ShapeDtypeStruct((n_tok, D_MODEL), jnp.float32),
        compiler_params=pltpu.CompilerParams(dimension_semantics=("arbitrary",)),
    )(r, route, y_tok, y_tok, ln2_g, ln2_b)


def _split_bf16(w):
    hi = w.astype(jnp.bfloat16)
    lo = (w - hi.astype(jnp.float32)).astype(jnp.bfloat16)
    return hi, lo


def kernel(x, p, ln_in_g, ln_in_b, w_in, conv_w, conv_b, pool_w, pool_scale, w_out, ln1_g, ln1_b,
           w_rg, b_rg, w_re, b_re, w_gate, w_up, w_down, w_pg, b_pg, w_ple, ln2_g, ln2_b):
    batch, seq, _ = x.shape
    n_tok = batch * seq
    bf = jnp.bfloat16
    row = lambda v: v.reshape(1, -1)

    w_r = jnp.concatenate([w_rg[0], jnp.transpose(w_re[0], (1, 0, 2)).reshape(D_MODEL, N_EXPERTS)], axis=1)
    w_r = jnp.pad(w_r, ((0, 0), (0, LANES - w_r.shape[1])))
    w_r_hi, w_r_lo = _split_bf16(w_r)
    w_r_cat = jnp.concatenate([w_r_hi, w_r_hi, w_r_lo], axis=0)
    b_r = jnp.pad(jnp.concatenate([b_rg[0], b_re[0].reshape(-1)]), (0, LANES - N_GROUPS - N_EXPERTS)).reshape(1, LANES)

    xp, r, route, route_t, counts = _run_mixer(
        x.reshape(n_tok, D_MODEL), p[0].reshape(n_tok, PLE_DIM), row(ln_in_g), row(ln_in_b),
        w_in[0].astype(bf), conv_w[0], row(conv_b[0]), pool_w[0].astype(bf), row(pool_scale[0]),
        w_out[0].astype(bf), row(ln1_g[0]), row(ln1_b[0]), w_r_cat, b_r,
        w_pg[0].astype(bf), row(b_pg[0]), w_ple[0].astype(bf), batch, seq)

    n_rows = -(-(2 * n_tok + N_EXPERTS * (ROW_TILE - 1)) // ROW_TILE) * ROW_TILE
    pos, tile_start, tiles_per_expert, pad_pos = _run_plan(route_t, counts, n_rows)

    zero_rows = jnp.zeros((SC_WINDOW, HALF), jnp.uint32)
    x_sorted = _dispatch_rows(xp, pos[0], pos[1], pad_pos.reshape(-1), zero_rows, n_rows)
    y_sorted = _run_experts(tile_start[0, :N_EXPERTS], tiles_per_expert[0, :N_EXPERTS], x_sorted,
                            w_gate[0].reshape(N_EXPERTS, D_MODEL, D_EXPERT),
                            w_up[0].reshape(N_EXPERTS, D_MODEL, D_EXPERT),
                            w_down[0].reshape(N_EXPERTS, D_EXPERT, D_MODEL))
    y_tok = _gather_rows(y_sorted, pos.reshape(-1))
    out = _run_combine(r, route, y_tok, row(ln2_g[0]), row(ln2_b[0]))
    return out.reshape(batch, seq, D_MODEL)
```

```python
import functools

import jax
import jax.numpy as jnp
from jax import lax
from jax.experimental import pallas as pl
from jax.experimental.pallas import tpu as pltpu
from jax.experimental.pallas import tpu_sc as plsc

D_MODEL = 1024
CONV_WIDTH = 512
CONV_K = 3
POOL_WIDTH = 512
POOL_WINDOWS = (2, 4, 8, 16)
POOL_GW = 128
IN_PROJ = 3 * CONV_WIDTH + POOL_WIDTH
N_GROUPS = 4
EXPERTS_PER_GROUP = 8
N_EXPERTS = N_GROUPS * EXPERTS_PER_GROUP
D_EXPERT = 256
PLE_DIM = 256
LN_EPS = 1e-5
DEEPNORM_ALPHA = 2.0 ** 0.25

LANES = 128
HALF = D_MODEL // 2
CONV_HALO = 8
POOL_HALO = 16
SEQ_TILE = 512
SUB_TILE = 256
ROW_TILE = 256
TILE_BUFFERS = 4
SC_WORKERS = 32
SC_WINDOW = 64
TOKEN_TILE = 512
VMEM_LIMIT = 56 * 1024 * 1024

R_ID1, R_ID2, R_RANK1, R_RANK2, R_W1, R_W2 = range(6)
ROUTE_ROWS = 8


def _layernorm(x, g, b):
    mu = jnp.mean(x, axis=-1, keepdims=True)
    xc = x - mu
    var = jnp.mean(xc * xc, axis=-1, keepdims=True)
    return xc * lax.rsqrt(var + LN_EPS) * g + b


def _pack_bf16_pairs(v):
    bits = lax.bitcast_convert_type(v.astype(jnp.bfloat16).astype(jnp.float32), jnp.uint32)
    return bits[:, HALF:] | (bits[:, :HALF] >> 16)


def _unpack_bf16_pairs(w):
    lo = lax.bitcast_convert_type(w << 16, jnp.float32)
    hi = lax.bitcast_convert_type(w & jnp.uint32(0xFFFF0000), jnp.float32)
    return lo, hi


def _mixer_kernel(x_ref, p_ref, lnin_g, lnin_b, w_in, conv_w, conv_b, pool_w, pool_scale, w_out,
                  ln1_g, ln1_b, w_r, b_r, w_pg, b_pg, w_ple,
                  xp_ref, r_ref, route_ref, route_t_ref, counts_ref,
                  zbuf, vbuf, carry):
    b = pl.program_id(0)
    s = pl.program_id(1)
    ts = x_ref.shape[0]

    @pl.when(s == 0)
    def _():
        zbuf[0:CONV_HALO, :] = jnp.zeros((CONV_HALO, CONV_WIDTH), jnp.float32)
        vbuf[0:POOL_HALO, :] = jnp.zeros((POOL_HALO, POOL_WIDTH), jnp.float32)

    @pl.when((b == 0) & (s == 0))
    def _():
        carry[...] = jnp.zeros_like(carry)

    for r0 in range(0, ts, SUB_TILE):
        _mixer_rows(r0, s * ts + r0, x_ref, p_ref, lnin_g, lnin_b, w_in, conv_w, conv_b, pool_w, pool_scale,
                    w_out, ln1_g, ln1_b, w_r, b_r, w_pg, b_pg, w_ple,
                    xp_ref, r_ref, route_ref, route_t_ref, zbuf, vbuf, carry)
    zbuf[0:CONV_HALO, :] = zbuf[ts:ts + CONV_HALO, :]
    vbuf[0:POOL_HALO, :] = vbuf[ts:ts + POOL_HALO, :]
    counts_ref[...] = carry[...]


def _mixer_rows(r0, seq0, x_ref, p_ref, lnin_g, lnin_b, w_in, conv_w, conv_b, pool_w, pool_scale, w_out,
                ln1_g, ln1_b, w_r, b_r, w_pg, b_pg, w_ple,
                xp_ref, r_ref, route_ref, route_t_ref, zbuf, vbuf, carry):
    n = SUB_TILE
    rows = pl.ds(r0, n)
    h0 = _layernorm(x_ref[rows, :], lnin_g[...], lnin_b[...])
    u = jnp.dot(h0.astype(jnp.bfloat16), w_in[...], preferred_element_type=jnp.float32)
    b_g = u[:, 0:CONV_WIDTH]
    c_g = u[:, CONV_WIDTH:2 * CONV_WIDTH]
    v_c = u[:, 2 * CONV_WIDTH:3 * CONV_WIDTH]
    v_p = u[:, 3 * CONV_WIDTH:]

    zbuf[pl.ds(CONV_HALO + r0, n), :] = c_g * v_c
    zext = zbuf[pl.ds(r0, n + CONV_HALO), :]
    z1 = pltpu.roll(zext, 1, axis=0)[CONV_HALO:, :]
    z2 = pltpu.roll(zext, 2, axis=0)[CONV_HALO:, :]
    conv = z2 * conv_w[0:1, :] + z1 * conv_w[1:2, :] + zext[CONV_HALO:, :] * conv_w[2:3, :] + conv_b[...]
    y_conv = b_g * conv

    vbuf[pl.ds(POOL_HALO + r0, n), :] = v_p
    vext = vbuf[pl.ds(r0, n + POOL_HALO), :]
    s2 = vext + pltpu.roll(vext, 1, axis=0)
    s4 = s2[:, POOL_GW:] + pltpu.roll(s2[:, POOL_GW:], 2, axis=0)
    s8 = s4[:, POOL_GW:] + pltpu.roll(s4[:, POOL_GW:], 4, axis=0)
    s16 = s8[:, POOL_GW:] + pltpu.roll(s8[:, POOL_GW:], 8, axis=0)
    wsums = (s2[POOL_HALO:, 0:POOL_GW], s4[POOL_HALO:, 0:POOL_GW],
             s8[POOL_HALO:, 0:POOL_GW], s16[POOL_HALO:, 0:POOL_GW])

    t_pos = seq0 + lax.broadcasted_iota(jnp.int32, (n, 1), 0)
    y_pool = []
    for j, w in enumerate(POOL_WINDOWS):
        inv_cnt = 1.0 / jnp.minimum(t_pos + 1, w).astype(jnp.float32)
        pooled = wsums[j] * inv_cnt - v_p[:, j * POOL_GW:(j + 1) * POOL_GW]
        y_pool.append(jnp.dot(pooled.astype(jnp.bfloat16), pool_w[j], preferred_element_type=jnp.float32))
    y_pool = jnp.concatenate(y_pool, axis=-1) * pool_scale[...]

    ycat = jnp.concatenate([y_conv, y_pool], axis=-1).astype(jnp.bfloat16)
    mix = jnp.dot(ycat, w_out[...], preferred_element_type=jnp.float32)
    h1 = _layernorm(DEEPNORM_ALPHA * h0 + mix, ln1_g[...], ln1_b[...])

    h_hi = h1.astype(jnp.bfloat16)
    h_lo = (h1 - h_hi.astype(jnp.float32)).astype(jnp.bfloat16)
    xp_ref[rows, :] = _pack_bf16_pairs(h1)

    gate = jax.nn.sigmoid(jnp.dot(h_hi, w_pg[...], preferred_element_type=jnp.float32) + b_pg[...])
    ple = jnp.dot(p_ref[rows, :].astype(jnp.bfloat16), w_ple[...], preferred_element_type=jnp.float32) * gate
    r_ref[rows, :] = DEEPNORM_ALPHA * h1 + ple

    hcat = jnp.concatenate([h_hi, h_lo, h_hi], axis=-1)
    logits = jnp.dot(hcat, w_r[...], preferred_element_type=jnp.float32) + b_r[...]
    lane = lax.broadcasted_iota(jnp.int32, (n, LANES), 1)
    neg = jnp.float32(-jnp.inf)

    def first_argmax(vals):
        m = jnp.max(vals, axis=-1, keepdims=True)
        idx = jnp.min(jnp.where(vals == m, lane, LANES), axis=-1, keepdims=True)
        return m, idx

    g_mask = lane < N_GROUPS
    g_max, g_idx = first_argmax(jnp.where(g_mask, logits, neg))
    g_w = 1.0 / jnp.sum(jnp.where(g_mask, jnp.exp(logits - g_max), 0.0), axis=-1, keepdims=True)

    e_lo = N_GROUPS + EXPERTS_PER_GROUP * g_idx
    e_vals = jnp.where((lane >= e_lo) & (lane < e_lo + EXPERTS_PER_GROUP), logits, neg)
    m1, i1 = first_argmax(e_vals)
    m2, i2 = first_argmax(jnp.where(lane == i1, neg, e_vals))
    e21 = jnp.exp(m2 - m1)
    w1 = g_w / (1.0 + e21)
    w2 = g_w * e21 / (1.0 + e21)
    id1 = i1 - N_GROUPS
    id2 = i2 - N_GROUPS

    sel1 = lane == id1
    sel2 = lane == id2
    onehot = (sel1 | sel2).astype(jnp.float32)
    row = lax.broadcasted_iota(jnp.int32, (n, n), 0)
    col = lax.broadcasted_iota(jnp.int32, (n, n), 1)
    tri = (col < row).astype(jnp.bfloat16)
    before = jnp.dot(tri, onehot.astype(jnp.bfloat16), preferred_element_type=jnp.float32) + carry[...]
    rank1 = jnp.sum(jnp.where(sel1, before, 0.0), axis=-1, keepdims=True)
    rank2 = jnp.sum(jnp.where(sel2, before, 0.0), axis=-1, keepdims=True)
    carry[...] = carry[...] + jnp.sum(onehot, axis=0, keepdims=True)

    rec = jnp.zeros((n, LANES), jnp.float32)
    for k, val in ((R_ID1, id1.astype(jnp.float32)), (R_ID2, id2.astype(jnp.float32)), (R_W1, w1), (R_W2, w2),
                   (R_RANK1, rank1), (R_RANK2, rank2)):
        rec = jnp.where(lane == k, val, rec)
    route_ref[rows, :] = rec
    route_t_ref[:, rows] = jnp.transpose(rec)[0:ROUTE_ROWS, :]


def _run_mixer(x2, p2, lnin_g, lnin_b, w_in, conv_w, conv_b, pool_w, pool_scale, w_out, ln1_g, ln1_b,
               w_r, b_r, w_pg, b_pg, w_ple, batch, seq):
    n_tok = batch * seq
    n_s = seq // SEQ_TILE
    tok_map = lambda b, s: (b * n_s + s, 0)

    def const(shape):
        return pl.BlockSpec(shape, lambda b, s: (0,) * len(shape), pipeline_mode=pl.Buffered(1))

    in_specs = [
        pl.BlockSpec((SEQ_TILE, D_MODEL), tok_map),
        pl.BlockSpec((SEQ_TILE, PLE_DIM), tok_map),
        const((1, D_MODEL)), const((1, D_MODEL)),
        const((D_MODEL, IN_PROJ)),
        const((CONV_K, CONV_WIDTH)), const((1, CONV_WIDTH)),
        const((len(POOL_WINDOWS), POOL_GW, POOL_GW)), const((1, POOL_WIDTH)),
        const((D_MODEL, D_MODEL)),
        const((1, D_MODEL)), const((1, D_MODEL)),
        const((3 * D_MODEL, LANES)), const((1, LANES)),
        const((D_MODEL, D_MODEL)), const((1, D_MODEL)),
        const((PLE_DIM, D_MODEL)),
    ]
    out_specs = [
        pl.BlockSpec((SEQ_TILE, HALF), tok_map),
        pl.BlockSpec((SEQ_TILE, D_MODEL), tok_map),
        pl.BlockSpec((SEQ_TILE, LANES), tok_map),
        pl.BlockSpec((ROUTE_ROWS, SEQ_TILE), lambda b, s: (0, b * n_s + s)),
        pl.BlockSpec((1, LANES), lambda b, s: (0, 0)),
    ]
    out_shape = [
        jax.ShapeDtypeStruct((n_tok, HALF), jnp.uint32),
        jax.ShapeDtypeStruct((n_tok, D_MODEL), jnp.float32),
        jax.ShapeDtypeStruct((n_tok, LANES), jnp.float32),
        jax.ShapeDtypeStruct((ROUTE_ROWS, n_tok), jnp.float32),
        jax.ShapeDtypeStruct((1, LANES), jnp.float32),
    ]
    return pl.pallas_call(
        _mixer_kernel,
        grid=(batch, n_s),
        in_specs=in_specs,
        out_specs=out_specs,
        out_shape=out_shape,
        scratch_shapes=[
            pltpu.VMEM((SEQ_TILE + CONV_HALO, CONV_WIDTH), jnp.float32),
            pltpu.VMEM((SEQ_TILE + POOL_HALO, POOL_WIDTH), jnp.float32),
            pltpu.VMEM((1, LANES), jnp.float32),
        ],
        compiler_params=pltpu.CompilerParams(
            dimension_semantics=("arbitrary", "arbitrary"), vmem_limit_bytes=VMEM_LIMIT),
    )(x2, p2, lnin_g, lnin_b, w_in, conv_w, conv_b, pool_w, pool_scale, w_out, ln1_g, ln1_b,
      w_r, b_r, w_pg, b_pg, w_ple)


def _plan_kernel(rt_ref, counts_ref, pos_ref, tile_start_ref, tiles_ref, pad_ref, *, n_rows):
    lane = lax.broadcasted_iota(jnp.int32, (ROUTE_ROWS, LANES), 1)
    counts = jnp.broadcast_to(counts_ref[...], (ROUTE_ROWS, LANES))
    tiles = jnp.floor((counts + (ROW_TILE - 1)) * (1.0 / ROW_TILE))
    tile_end = tiles
    shift = 1
    while shift < N_EXPERTS:
        tile_end = tile_end + jnp.where(lane >= shift, pltpu.roll(tile_end, shift, axis=1), 0.0)
        shift *= 2
    row_start = (tile_end - tiles) * ROW_TILE
    tile_start_ref[...] = (tile_end - tiles)[0:1, :].astype(jnp.int32)
    tiles_ref[...] = tiles[0:1, :].astype(jnp.int32)

    rt = rt_ref[...]
    ids = rt[R_ID1:R_ID2 + 1, :]
    start = jnp.zeros_like(ids)
    for e in range(N_EXPERTS):
        start = jnp.where(ids == e, row_start[0:1, e:e + 1], start)
    pos_ref[...] = (start + rt[R_RANK1:R_RANK2 + 1, :]).astype(jnp.int32)

    sub = lax.broadcasted_iota(jnp.int32, (N_EXPERTS, LANES), 0)
    lane_e = lax.broadcasted_iota(jnp.int32, (N_EXPERTS, LANES), 1)
    diag = sub == lane_e
    pad_lo = jnp.sum(jnp.where(diag, (row_start + counts)[0:1, :], 0.0), axis=1, keepdims=True)
    pad_n = jnp.sum(jnp.where(diag, (tiles * ROW_TILE - counts)[0:1, :], 0.0), axis=1, keepdims=True)
    j = lax.broadcasted_iota(jnp.int32, (N_EXPERTS, ROW_TILE), 1).astype(jnp.float32)
    pad_ref[...] = jnp.where(j < pad_n, pad_lo + j, n_rows + j).astype(jnp.int32)


def _run_plan(route_t, counts, n_rows):
    n_tok = route_t.shape[1]
    full = lambda shape: pl.BlockSpec(shape, lambda i: (0, 0))
    return pl.pallas_call(
        functools.partial(_plan_kernel, n_rows=n_rows),
        grid=(1,),
        in_specs=[full((ROUTE_ROWS, n_tok)), full((1, LANES))],
        out_specs=[full((2, n_tok)), full((1, LANES)), full((1, LANES)), full((N_EXPERTS, ROW_TILE))],
        out_shape=[
            jax.ShapeDtypeStruct((2, n_tok), jnp.int32),
            jax.ShapeDtypeStruct((1, LANES), jnp.int32),
            jax.ShapeDtypeStruct((1, LANES), jnp.int32),
            jax.ShapeDtypeStruct((N_EXPERTS, ROW_TILE), jnp.int32),
        ],
        compiler_params=pltpu.CompilerParams(dimension_semantics=("arbitrary",)),
    )(route_t, counts)


def _sc_mesh():
    return plsc.VectorSubcoreMesh(core_axis_name="core", subcore_axis_name="subcore")


def _sc_worker_id():
    return lax.axis_index("core") * (SC_WORKERS // 2) + lax.axis_index("subcore")


def _dispatch_rows(xp, pos1, pos2, pad_pos, zero_rows, n_rows):
    n_tok, width = xp.shape
    per_worker = n_tok // SC_WORKERS
    pads_per_worker = pad_pos.shape[0] // SC_WORKERS

    @functools.partial(
        pl.kernel, out_type=jax.ShapeDtypeStruct((n_rows + ROW_TILE, width), xp.dtype), mesh=_sc_mesh(),
        scratch_types=[pltpu.VMEM((SC_WINDOW,), jnp.int32), pltpu.VMEM((SC_WINDOW,), jnp.int32),
                       pltpu.VMEM((SC_WINDOW, width), xp.dtype)])
    def dispatch(xp_hbm, pos1_hbm, pos2_hbm, pad_hbm, zero_hbm, out_hbm, idx1, idx2, buf):
        wid = _sc_worker_id()

        @pl.loop(0, per_worker // SC_WINDOW)
        def _(j):
            base = wid * per_worker + j * SC_WINDOW
            pltpu.sync_copy(pos1_hbm.at[pl.ds(base, SC_WINDOW)], idx1)
            pltpu.sync_copy(pos2_hbm.at[pl.ds(base, SC_WINDOW)], idx2)
            pltpu.sync_copy(xp_hbm.at[pl.ds(base, SC_WINDOW)], buf)
            pltpu.sync_copy(buf, out_hbm.at[idx1])
            pltpu.sync_copy(buf, out_hbm.at[idx2])

        pltpu.sync_copy(zero_hbm, buf)

        @pl.loop(0, pads_per_worker // SC_WINDOW)
        def _(j):
            base = wid * pads_per_worker + j * SC_WINDOW
            pltpu.sync_copy(pad_hbm.at[pl.ds(base, SC_WINDOW)], idx1)
            pltpu.sync_copy(buf, out_hbm.at[idx1])

    return dispatch(xp, pos1, pos2, pad_pos, zero_rows)


def _gather_rows(src, idx):
    n_out, width = idx.shape[0], src.shape[1]
    per_worker = n_out // SC_WORKERS

    @functools.partial(
        pl.kernel, out_type=jax.ShapeDtypeStruct((n_out, width), src.dtype), mesh=_sc_mesh(),
        scratch_types=[pltpu.VMEM((SC_WINDOW,), jnp.int32), pltpu.VMEM((SC_WINDOW, width), src.dtype)])
    def gather(src_hbm, idx_hbm, dst_hbm, idx_v, buf):
        wid = _sc_worker_id()

        @pl.loop(0, per_worker // SC_WINDOW)
        def _(j):
            base = wid * per_worker + j * SC_WINDOW
            pltpu.sync_copy(idx_hbm.at[pl.ds(base, SC_WINDOW)], idx_v)
            pltpu.sync_copy(src_hbm.at[idx_v], buf)
            pltpu.sync_copy(buf, dst_hbm.at[pl.ds(base, SC_WINDOW)])

    return gather(src, idx)


def _tile_copy(hbm, buf, sem, tile, slot, to_hbm):
    rows = hbm.at[pl.ds(pl.multiple_of(tile * ROW_TILE, ROW_TILE), ROW_TILE)]
    if to_hbm:
        return pltpu.make_async_copy(buf.at[slot], rows, sem.at[slot])
    return pltpu.make_async_copy(rows, buf.at[slot], sem.at[slot])


def _expert_kernel(ts_ref, nte_ref, x_hbm, wg_ref, wu_ref, wd_ref, y_hbm,
                   xbuf, ybuf, xsem, ysem, wgu_bf, wd_bf, *, max_tiles):
    e = pl.program_id(0)
    first = ts_ref[e]
    count = nte_ref[e]
    n_tiles = ts_ref[N_EXPERTS - 1] + nte_ref[N_EXPERTS - 1]
    ahead = TILE_BUFFERS - 1

    @pl.when(e == 0)
    def _():
        for g in range(ahead):
            @pl.when(g < n_tiles)
            def _():
                _tile_copy(x_hbm, xbuf, xsem, g, g, False).start()

    @pl.when(count > 0)
    def _():
        wgu_bf[:, 0:D_EXPERT] = wg_ref[0].astype(jnp.bfloat16)
        wgu_bf[:, D_EXPERT:] = wu_ref[0].astype(jnp.bfloat16)
        wd_bf[...] = wd_ref[0].astype(jnp.bfloat16)

    def tile_body(g, c):
        slot = g % TILE_BUFFERS
        _tile_copy(x_hbm, xbuf, xsem, g, slot, False).wait()

        @pl.when(g + ahead < n_tiles)
        def _():
            _tile_copy(x_hbm, xbuf, xsem, g + ahead, (g + ahead) % TILE_BUFFERS, False).start()

        lo, hi = _unpack_bf16_pairs(xbuf[slot])
        hgu = (jnp.dot(lo.astype(jnp.bfloat16), wgu_bf[0:HALF, :], preferred_element_type=jnp.float32)
               + jnp.dot(hi.astype(jnp.bfloat16), wgu_bf[HALF:, :], preferred_element_type=jnp.float32))
        hg = hgu[:, 0:D_EXPERT]
        act = hg * jax.nn.sigmoid(hg) * hgu[:, D_EXPERT:]
        y = jnp.dot(act.astype(jnp.bfloat16), wd_bf[...], preferred_element_type=jnp.float32)

        @pl.when(g >= TILE_BUFFERS)
        def _():
            _tile_copy(y_hbm, ybuf, ysem, g - TILE_BUFFERS, slot, True).wait()

        ybuf[slot] = _pack_bf16_pairs(y)
        _tile_copy(y_hbm, ybuf, ysem, g, slot, True).start()
        return c

    lax.fori_loop(first, first + count, tile_body, 0)

    @pl.when(e == N_EXPERTS - 1)
    def _():
        for k in range(TILE_BUFFERS, 0, -1):
            @pl.when(n_tiles >= k)
            def _():
                _tile_copy(y_hbm, ybuf, ysem, n_tiles - k, (n_tiles - k) % TILE_BUFFERS, True).wait()

        ybuf[0] = jnp.zeros((ROW_TILE, HALF), jnp.uint32)

        def fill(g, c):
            cp = _tile_copy(y_hbm, ybuf, ysem, g, 0, True)
            cp.start()
            cp.wait()
            return c

        lax.fori_loop(n_tiles, max_tiles, fill, 0)


def _run_experts(tile_start, tiles_per_expert, x_sorted, w_gate, w_up, w_down):
    n_rows = x_sorted.shape[0]
    max_tiles = n_rows // ROW_TILE
    w_map = lambda e, ts, nte: (e, 0, 0)

    grid_spec = pltpu.PrefetchScalarGridSpec(
        num_scalar_prefetch=2,
        grid=(N_EXPERTS,),
        in_specs=[
            pl.BlockSpec(memory_space=pl.ANY),
            pl.BlockSpec((1, D_MODEL, D_EXPERT), w_map),
            pl.BlockSpec((1, D_MODEL, D_EXPERT), w_map),
            pl.BlockSpec((1, D_EXPERT, D_MODEL), w_map),
        ],
        out_specs=pl.BlockSpec(memory_space=pl.ANY),
        scratch_shapes=[
            pltpu.VMEM((TILE_BUFFERS, ROW_TILE, HALF), jnp.uint32),
            pltpu.VMEM((TILE_BUFFERS, ROW_TILE, HALF), jnp.uint32),
            pltpu.SemaphoreType.DMA((TILE_BUFFERS,)),
            pltpu.SemaphoreType.DMA((TILE_BUFFERS,)),
            pltpu.VMEM((D_MODEL, 2 * D_EXPERT), jnp.bfloat16),
            pltpu.VMEM((D_EXPERT, D_MODEL), jnp.bfloat16),
        ],
    )
    return pl.pallas_call(
        functools.partial(_expert_kernel, max_tiles=max_tiles),
        grid_spec=grid_spec,
        out_shape=jax.ShapeDtypeStruct((n_rows, HALF), jnp.uint32),
        compiler_params=pltpu.CompilerParams(
            dimension_semantics=("arbitrary",), vmem_limit_bytes=VMEM_LIMIT),
    )(tile_start, tiles_per_expert, x_sorted, w_gate, w_up, w_down)


def _combine_kernel(r_ref, route_ref, ya_ref, yb_ref, g_ref, b_ref, o_ref):
    route = route_ref[...]
    w1 = route[:, R_W1:R_W1 + 1]
    w2 = route[:, R_W2:R_W2 + 1]
    a_lo, a_hi = _unpack_bf16_pairs(ya_ref[...])
    b_lo, b_hi = _unpack_bf16_pairs(yb_ref[...])
    moe = jnp.concatenate([w1 * a_lo + w2 * b_lo, w1 * a_hi + w2 * b_hi], axis=-1)
    o_ref[...] = _layernorm(r_ref[...] + moe, g_ref[...], b_ref[...])


def _run_combine(r, route, y_tok, ln2_g, ln2_b):
    n_tok = r.shape[0]
    n_t = n_tok // TOKEN_TILE
    return pl.pallas_call(
        _combine_kernel,
        grid=(n_t,),
        in_specs=[
            pl.BlockSpec((TOKEN_TILE, D_MODEL), lambda i: (i, 0)),
            pl.BlockSpec((TOKEN_TILE, LANES), lambda i: (i, 0)),
            pl.BlockSpec((TOKEN_TILE, HALF), lambda i: (i, 0)),
            pl.BlockSpec((TOKEN_TILE, HALF), lambda i: (i + n_t, 0)),
            pl.BlockSpec((1, D_MODEL), lambda i: (0, 0)),
            pl.BlockSpec((1, D_MODEL), lambda i: (0, 0)),
        ],
        out_specs=pl.BlockSpec((TOKEN_TILE, D_MODEL), lambda i: (i, 0)),
        out_shape=jax.ShapeDtypeStruct((n_tok, D_MODEL), jnp.float32),
        compiler_params=pltpu.CompilerParams(dimension_semantics=("arbitrary",)),
    )(r, route, y_tok, y_tok, ln2_g, ln2_b)


def _split_bf16(w):
    hi = w.astype(jnp.bfloat16)
    lo = (w - hi.astype(jnp.float32)).astype(jnp.bfloat16)
    return hi, lo


def kernel(x, p, ln_in_g, ln_in_b, w_in, conv_w, conv_b, pool_w, pool_scale, w_out, ln1_g, ln1_b,
           w_rg, b_rg, w_re, b_re, w_gate, w_up, w_down, w_pg, b_pg, w_ple, ln2_g, ln2_b):
    batch, seq, _ = x.shape
    n_tok = batch * seq
    bf = jnp.bfloat16
    row = lambda v: v.reshape(1, -1)

    w_r = jnp.concatenate([w_rg[0], jnp.transpose(w_re[0], (1, 0, 2)).reshape(D_MODEL, N_EXPERTS)], axis=1)
    w_r = jnp.pad(w_r, ((0, 0), (0, LANES - w_r.shape[1])))
    w_r_hi, w_r_lo = _split_bf16(w_r)
    w_r_cat = jnp.concatenate([w_r_hi, w_r_hi, w_r_lo], axis=0)
    b_r = jnp.pad(jnp.concatenate([b_rg[0], b_re[0].reshape(-1)]), (0, LANES - N_GROUPS - N_EXPERTS)).reshape(1, LANES)

    xp, r, route, route_t, counts = _run_mixer(
        x.reshape(n_tok, D_MODEL), p[0].reshape(n_tok, PLE_DIM), row(ln_in_g), row(ln_in_b),
        w_in[0].astype(bf), conv_w[0], row(conv_b[0]), pool_w[0].astype(bf), row(pool_scale[0]),
        w_out[0].astype(bf), row(ln1_g[0]), row(ln1_b[0]), w_r_cat, b_r,
        w_pg[0].astype(bf), row(b_pg[0]), w_ple[0].astype(bf), batch, seq)

    n_rows = -(-(2 * n_tok + N_EXPERTS * (ROW_TILE - 1)) // ROW_TILE) * ROW_TILE
    pos, tile_start, tiles_per_expert, pad_pos = _run_plan(route_t, counts, n_rows)

    zero_rows = jnp.zeros((SC_WINDOW, HALF), jnp.uint32)
    x_sorted = _dispatch_rows(xp, pos[0], pos[1], pad_pos.reshape(-1), zero_rows, n_rows)
    y_sorted = _run_experts(tile_start[0, :N_EXPERTS], tiles_per_expert[0, :N_EXPERTS], x_sorted,
                            w_gate[0].reshape(N_EXPERTS, D_MODEL, D_EXPERT),
                            w_up[0].reshape(N_EXPERTS, D_MODEL, D_EXPERT),
                            w_down[0].reshape(N_EXPERTS, D_EXPERT, D_MODEL))
    y_tok = _gather_rows(y_sorted, pos.reshape(-1))
    out = _run_combine(r, route, y_tok, row(ln2_g[0]), row(ln2_b[0]))
    return out.reshape(batch, seq, D_MODEL)
```

```python
import functools

import jax
import jax.numpy as jnp
from jax import lax
from jax.experimental import pallas as pl
from jax.experimental.pallas import tpu as pltpu
from jax.experimental.pallas import tpu_sc as plsc

D_MODEL = 1024
CONV_WIDTH = 512
CONV_K = 3
POOL_WIDTH = 512
POOL_WINDOWS = (2, 4, 8, 16)
POOL_GW = 128
IN_PROJ = 3 * CONV_WIDTH + POOL_WIDTH
N_GROUPS = 4
EXPERTS_PER_GROUP = 8
N_EXPERTS = N_GROUPS * EXPERTS_PER_GROUP
D_EXPERT = 256
PLE_DIM = 256
LN_EPS = 1e-5
DEEPNORM_ALPHA = 2.0 ** 0.25

LANES = 128
HALF = D_MODEL // 2
CONV_HALO = 8
POOL_HALO = 16
SEQ_TILE = 512
SUB_TILE = 256
ROW_TILE = 256
TILE_BUFFERS = 4
N_CHUNKS = 2
SC_WORKERS = 32
SC_WINDOW = 64
TOKEN_TILE = 512
VMEM_LIMIT = 56 * 1024 * 1024

R_ID1, R_ID2, R_RANK1, R_RANK2, R_W1, R_W2 = range(6)
ROUTE_ROWS = 8


def _layernorm(x, g, b):
    mu = jnp.mean(x, axis=-1, keepdims=True)
    xc = x - mu
    var = jnp.mean(xc * xc, axis=-1, keepdims=True)
    return xc * lax.rsqrt(var + LN_EPS) * g + b


def _pack_bf16_pairs(v):
    bits = lax.bitcast_convert_type(v.astype(jnp.bfloat16).astype(jnp.float32), jnp.uint32)
    return bits[:, HALF:] | (bits[:, :HALF] >> 16)


def _unpack_bf16_pairs(w):
    lo = lax.bitcast_convert_type(w << 16, jnp.float32)
    hi = lax.bitcast_convert_type(w & jnp.uint32(0xFFFF0000), jnp.float32)
    return lo, hi


def _mixer_kernel(x_ref, p_ref, lnin_g, lnin_b, w_in, conv_w, conv_b, pool_w, pool_scale, w_out,
                  ln1_g, ln1_b, w_r, b_r, w_pg, b_pg, w_ple,
                  xp_ref, r_ref, route_ref, route_t_ref, counts_ref,
                  zbuf, vbuf, carry):
    b = pl.program_id(0)
    s = pl.program_id(1)
    ts = x_ref.shape[0]

    @pl.when(s == 0)
    def _():
        zbuf[0:CONV_HALO, :] = jnp.zeros((CONV_HALO, CONV_WIDTH), jnp.float32)
        vbuf[0:POOL_HALO, :] = jnp.zeros((POOL_HALO, POOL_WIDTH), jnp.float32)

    @pl.when((b == 0) & (s == 0))
    def _():
        carry[...] = jnp.zeros_like(carry)

    for r0 in range(0, ts, SUB_TILE):
        _mixer_rows(r0, s * ts + r0, x_ref, p_ref, lnin_g, lnin_b, w_in, conv_w, conv_b, pool_w, pool_scale,
                    w_out, ln1_g, ln1_b, w_r, b_r, w_pg, b_pg, w_ple,
                    xp_ref, r_ref, route_ref, route_t_ref, zbuf, vbuf, carry)
    zbuf[0:CONV_HALO, :] = zbuf[ts:ts + CONV_HALO, :]
    vbuf[0:POOL_HALO, :] = vbuf[ts:ts + POOL_HALO, :]
    counts_ref[...] = carry[...]


def _mixer_rows(r0, seq0, x_ref, p_ref, lnin_g, lnin_b, w_in, conv_w, conv_b, pool_w, pool_scale, w_out,
                ln1_g, ln1_b, w_r, b_r, w_pg, b_pg, w_ple,
                xp_ref, r_ref, route_ref, route_t_ref, zbuf, vbuf, carry):
    n = SUB_TILE
    rows = pl.ds(r0, n)
    h0 = _layernorm(x_ref[rows, :], lnin_g[...], lnin_b[...])
    u = jnp.dot(h0.astype(jnp.bfloat16), w_in[...], preferred_element_type=jnp.float32)
    b_g = u[:, 0:CONV_WIDTH]
    c_g = u[:, CONV_WIDTH:2 * CONV_WIDTH]
    v_c = u[:, 2 * CONV_WIDTH:3 * CONV_WIDTH]
    v_p = u[:, 3 * CONV_WIDTH:]

    zbuf[pl.ds(CONV_HALO + r0, n), :] = c_g * v_c
    zext = zbuf[pl.ds(r0, n + CONV_HALO), :]
    z1 = pltpu.roll(zext, 1, axis=0)[CONV_HALO:, :]
    z2 = pltpu.roll(zext, 2, axis=0)[CONV_HALO:, :]
    conv = z2 * conv_w[0:1, :] + z1 * conv_w[1:2, :] + zext[CONV_HALO:, :] * conv_w[2:3, :] + conv_b[...]
    y_conv = b_g * conv

    vbuf[pl.ds(POOL_HALO + r0, n), :] = v_p
    vext = vbuf[pl.ds(r0, n + POOL_HALO), :]
    s2 = vext + pltpu.roll(vext, 1, axis=0)
    s4 = s2[:, POOL_GW:] + pltpu.roll(s2[:, POOL_GW:], 2, axis=0)
    s8 = s4[:, POOL_GW:] + pltpu.roll(s4[:, POOL_GW:], 4, axis=0)
    s16 = s8[:, POOL_GW:] + pltpu.roll(s8[:, POOL_GW:], 8, axis=0)
    wsums = (s2[POOL_HALO:, 0:POOL_GW], s4[POOL_HALO:, 0:POOL_GW],
             s8[POOL_HALO:, 0:POOL_GW], s16[POOL_HALO:, 0:POOL_GW])

    t_pos = seq0 + lax.broadcasted_iota(jnp.int32, (n, 1), 0)
    y_pool = []
    for j, w in enumerate(POOL_WINDOWS):
        inv_cnt = 1.0 / jnp.minimum(t_pos + 1, w).astype(jnp.float32)
        pooled = wsums[j] * inv_cnt - v_p[:, j * POOL_GW:(j + 1) * POOL_GW]
        y_pool.append(jnp.dot(pooled.astype(jnp.bfloat16), pool_w[j], preferred_element_type=jnp.float32))
    y_pool = jnp.concatenate(y_pool, axis=-1) * pool_scale[...]

    ycat = jnp.concatenate([y_conv, y_pool], axis=-1).astype(jnp.bfloat16)
    mix = jnp.dot(ycat, w_out[...], preferred_element_type=jnp.float32)
    h1 = _layernorm(DEEPNORM_ALPHA * h0 + mix, ln1_g[...], ln1_b[...])

    h_hi = h1.astype(jnp.bfloat16)
    h_lo = (h1 - h_hi.astype(jnp.float32)).astype(jnp.bfloat16)
    xp_ref[rows, :] = _pack_bf16_pairs(h1)

    gate = jax.nn.sigmoid(jnp.dot(h_hi, w_pg[...], preferred_element_type=jnp.float32) + b_pg[...])
    ple = jnp.dot(p_ref[rows, :].astype(jnp.bfloat16), w_ple[...], preferred_element_type=jnp.float32) * gate
    r_ref[rows, :] = DEEPNORM_ALPHA * h1 + ple

    hcat = jnp.concatenate([h_hi, h_lo, h_hi], axis=-1)
    logits = jnp.dot(hcat, w_r[...], preferred_element_type=jnp.float32) + b_r[...]
    lane = lax.broadcasted_iota(jnp.int32, (n, LANES), 1)
    neg = jnp.float32(-jnp.inf)

    def first_argmax(vals):
        m = jnp.max(vals, axis=-1, keepdims=True)
        idx = jnp.min(jnp.where(vals == m, lane, LANES), axis=-1, keepdims=True)
        return m, idx

    g_mask = lane < N_GROUPS
    g_max, g_idx = first_argmax(jnp.where(g_mask, logits, neg))
    g_w = 1.0 / jnp.sum(jnp.where(g_mask, jnp.exp(logits - g_max), 0.0), axis=-1, keepdims=True)

    e_lo = N_GROUPS + EXPERTS_PER_GROUP * g_idx
    e_vals = jnp.where((lane >= e_lo) & (lane < e_lo + EXPERTS_PER_GROUP), logits, neg)
    m1, i1 = first_argmax(e_vals)
    m2, i2 = first_argmax(jnp.where(lane == i1, neg, e_vals))
    e21 = jnp.exp(m2 - m1)
    w1 = g_w / (1.0 + e21)
    w2 = g_w * e21 / (1.0 + e21)
    id1 = i1 - N_GROUPS
    id2 = i2 - N_GROUPS

    sel1 = lane == id1
    sel2 = lane == id2
    onehot = (sel1 | sel2).astype(jnp.float32)
    row = lax.broadcasted_iota(jnp.int32, (n, n), 0)
    col = lax.broadcasted_iota(jnp.int32, (n, n), 1)
    tri = (col < row).astype(jnp.bfloat16)
    before = jnp.dot(tri, onehot.astype(jnp.bfloat16), preferred_element_type=jnp.float32) + carry[...]
    rank1 = jnp.sum(jnp.where(sel1, before, 0.0), axis=-1, keepdims=True)
    rank2 = jnp.sum(jnp.where(sel2, before, 0.0), axis=-1, keepdims=True)
    carry[...] = carry[...] + jnp.sum(onehot, axis=0, keepdims=True)

    rec = jnp.zeros((n, LANES), jnp.float32)
    for k, val in ((R_ID1, id1.astype(jnp.float32)), (R_ID2, id2.astype(jnp.float32)), (R_W1, w1), (R_W2, w2),
                   (R_RANK1, rank1), (R_RANK2, rank2)):
        rec = jnp.where(lane == k, val, rec)
    route_ref[rows, :] = rec
    route_t_ref[:, rows] = jnp.transpose(rec)[0:ROUTE_ROWS, :]


def _run_mixer(x2, p2, lnin_g, lnin_b, w_in, conv_w, conv_b, pool_w, pool_scale, w_out, ln1_g, ln1_b,
               w_r, b_r, w_pg, b_pg, w_ple, batch_lo, batch, seq):
    n_tok = batch * seq
    n_s = seq // SEQ_TILE
    tok_map = lambda b, s: (b * n_s + s, 0)
    in_map = lambda b, s: ((batch_lo + b) * n_s + s, 0)

    def const(shape):
        return pl.BlockSpec(shape, lambda b, s: (0,) * len(shape), pipeline_mode=pl.Buffered(1))

    in_specs = [
        pl.BlockSpec((SEQ_TILE, D_MODEL), in_map),
        pl.BlockSpec((SEQ_TILE, PLE_DIM), in_map),
        const((1, D_MODEL)), const((1, D_MODEL)),
        const((D_MODEL, IN_PROJ)),
        const((CONV_K, CONV_WIDTH)), const((1, CONV_WIDTH)),
        const((len(POOL_WINDOWS), POOL_GW, POOL_GW)), const((1, POOL_WIDTH)),
        const((D_MODEL, D_MODEL)),
        const((1, D_MODEL)), const((1, D_MODEL)),
        const((3 * D_MODEL, LANES)), const((1, LANES)),
        const((D_MODEL, D_MODEL)), const((1, D_MODEL)),
        const((PLE_DIM, D_MODEL)),
    ]
    out_specs = [
        pl.BlockSpec((SEQ_TILE, HALF), tok_map),
        pl.BlockSpec((SEQ_TILE, D_MODEL), tok_map),
        pl.BlockSpec((SEQ_TILE, LANES), tok_map),
        pl.BlockSpec((ROUTE_ROWS, SEQ_TILE), lambda b, s: (0, b * n_s + s)),
        pl.BlockSpec((1, LANES), lambda b, s: (0, 0)),
    ]
    out_shape = [
        jax.ShapeDtypeStruct((n_tok, HALF), jnp.uint32),
        jax.ShapeDtypeStruct((n_tok, D_MODEL), jnp.float32),
        jax.ShapeDtypeStruct((n_tok, LANES), jnp.float32),
        jax.ShapeDtypeStruct((ROUTE_ROWS, n_tok), jnp.float32),
        jax.ShapeDtypeStruct((1, LANES), jnp.float32),
    ]
    return pl.pallas_call(
        _mixer_kernel,
        grid=(batch, n_s),
        in_specs=in_specs,
        out_specs=out_specs,
        out_shape=out_shape,
        scratch_shapes=[
            pltpu.VMEM((SEQ_TILE + CONV_HALO, CONV_WIDTH), jnp.float32),
            pltpu.VMEM((SEQ_TILE + POOL_HALO, POOL_WIDTH), jnp.float32),
            pltpu.VMEM((1, LANES), jnp.float32),
        ],
        compiler_params=pltpu.CompilerParams(
            dimension_semantics=("arbitrary", "arbitrary"), vmem_limit_bytes=VMEM_LIMIT),
    )(x2, p2, lnin_g, lnin_b, w_in, conv_w, conv_b, pool_w, pool_scale, w_out, ln1_g, ln1_b,
      w_r, b_r, w_pg, b_pg, w_ple)


def _plan_kernel(rt_ref, counts_ref, pos_ref, tile_start_ref, tiles_ref, pad_ref, *, n_rows):
    lane = lax.broadcasted_iota(jnp.int32, (ROUTE_ROWS, LANES), 1)
    counts = jnp.broadcast_to(counts_ref[...], (ROUTE_ROWS, LANES))
    tiles = jnp.floor((counts + (ROW_TILE - 1)) * (1.0 / ROW_TILE))
    tile_end = tiles
    shift = 1
    while shift < N_EXPERTS:
        tile_end = tile_end + jnp.where(lane >= shift, pltpu.roll(tile_end, shift, axis=1), 0.0)
        shift *= 2
    row_start = (tile_end - tiles) * ROW_TILE
    tile_start_ref[...] = (tile_end - tiles)[0:1, :].astype(jnp.int32)
    tiles_ref[...] = tiles[0:1, :].astype(jnp.int32)

    rt = rt_ref[...]
    ids = rt[R_ID1:R_ID2 + 1, :]
    start = jnp.zeros_like(ids)
    for e in range(N_EXPERTS):
        start = jnp.where(ids == e, row_start[0:1, e:e + 1], start)
    pos_ref[...] = (start + rt[R_RANK1:R_RANK2 + 1, :]).astype(jnp.int32)

    sub = lax.broadcasted_iota(jnp.int32, (N_EXPERTS, LANES), 0)
    lane_e = lax.broadcasted_iota(jnp.int32, (N_EXPERTS, LANES), 1)
    diag = sub == lane_e
    pad_lo = jnp.sum(jnp.where(diag, (row_start + counts)[0:1, :], 0.0), axis=1, keepdims=True)
    pad_n = jnp.sum(jnp.where(diag, (tiles * ROW_TILE - counts)[0:1, :], 0.0), axis=1, keepdims=True)
    j = lax.broadcasted_iota(jnp.int32, (N_EXPERTS, ROW_TILE), 1).astype(jnp.float32)
    pad_ref[...] = jnp.where(j < pad_n, pad_lo + j, n_rows + j).astype(jnp.int32)


def _run_plan(route_t, counts, n_rows):
    n_tok = route_t.shape[1]
    full = lambda shape: pl.BlockSpec(shape, lambda i: (0, 0))
    return pl.pallas_call(
        functools.partial(_plan_kernel, n_rows=n_rows),
        grid=(1,),
        in_specs=[full((ROUTE_ROWS, n_tok)), full((1, LANES))],
        out_specs=[full((2, n_tok)), full((1, LANES)), full((1, LANES)), full((N_EXPERTS, ROW_TILE))],
        out_shape=[
            jax.ShapeDtypeStruct((2, n_tok), jnp.int32),
            jax.ShapeDtypeStruct((1, LANES), jnp.int32),
            jax.ShapeDtypeStruct((1, LANES), jnp.int32),
            jax.ShapeDtypeStruct((N_EXPERTS, ROW_TILE), jnp.int32),
        ],
        compiler_params=pltpu.CompilerParams(dimension_semantics=("arbitrary",)),
    )(route_t, counts)


def _sc_mesh():
    return plsc.VectorSubcoreMesh(core_axis_name="core", subcore_axis_name="subcore")


def _sc_worker_id():
    return lax.axis_index("core") * (SC_WORKERS // 2) + lax.axis_index("subcore")


def _dispatch_rows(xp, pos1, pos2, pad_pos, zero_rows, n_rows):
    n_tok, width = xp.shape
    per_worker = n_tok // SC_WORKERS
    pads_per_worker = pad_pos.shape[0] // SC_WORKERS

    @functools.partial(
        pl.kernel, out_type=jax.ShapeDtypeStruct((n_rows + ROW_TILE, width), xp.dtype), mesh=_sc_mesh(),
        scratch_types=[pltpu.VMEM((SC_WINDOW,), jnp.int32), pltpu.VMEM((SC_WINDOW,), jnp.int32),
                       pltpu.VMEM((SC_WINDOW, width), xp.dtype)])
    def dispatch(xp_hbm, pos1_hbm, pos2_hbm, pad_hbm, zero_hbm, out_hbm, idx1, idx2, buf):
        wid = _sc_worker_id()

        @pl.loop(0, per_worker // SC_WINDOW)
        def _(j):
            base = wid * per_worker + j * SC_WINDOW
            pltpu.sync_copy(pos1_hbm.at[pl.ds(base, SC_WINDOW)], idx1)
            pltpu.sync_copy(pos2_hbm.at[pl.ds(base, SC_WINDOW)], idx2)
            pltpu.sync_copy(xp_hbm.at[pl.ds(base, SC_WINDOW)], buf)
            pltpu.sync_copy(buf, out_hbm.at[idx1])
            pltpu.sync_copy(buf, out_hbm.at[idx2])

        pltpu.sync_copy(zero_hbm, buf)

        @pl.loop(0, pads_per_worker // SC_WINDOW)
        def _(j):
            base = wid * pads_per_worker + j * SC_WINDOW
            pltpu.sync_copy(pad_hbm.at[pl.ds(base, SC_WINDOW)], idx1)
            pltpu.sync_copy(buf, out_hbm.at[idx1])

    return dispatch(xp, pos1, pos2, pad_pos, zero_rows)


def _gather_rows(src, idx):
    n_out, width = idx.shape[0], src.shape[1]
    per_worker = n_out // SC_WORKERS

    @functools.partial(
        pl.kernel, out_type=jax.ShapeDtypeStruct((n_out, width), src.dtype), mesh=_sc_mesh(),
        scratch_types=[pltpu.VMEM((SC_WINDOW,), jnp.int32), pltpu.VMEM((SC_WINDOW, width), src.dtype)])
    def gather(src_hbm, idx_hbm, dst_hbm, idx_v, buf):
        wid = _sc_worker_id()

        @pl.loop(0, per_worker // SC_WINDOW)
        def _(j):
            base = wid * per_worker + j * SC_WINDOW
            pltpu.sync_copy(idx_hbm.at[pl.ds(base, SC_WINDOW)], idx_v)
            pltpu.sync_copy(src_hbm.at[idx_v], buf)
            pltpu.sync_copy(buf, dst_hbm.at[pl.ds(base, SC_WINDOW)])

    return gather(src, idx)


def _tile_copy(hbm, buf, sem, tile, slot, to_hbm):
    rows = hbm.at[pl.ds(pl.multiple_of(tile * ROW_TILE, ROW_TILE), ROW_TILE)]
    if to_hbm:
        return pltpu.make_async_copy(buf.at[slot], rows, sem.at[slot])
    return pltpu.make_async_copy(rows, buf.at[slot], sem.at[slot])


def _expert_kernel(ts_ref, nte_ref, x_hbm, wg_ref, wu_ref, wd_ref, y_hbm,
                   xbuf, ybuf, xsem, ysem, wgu_bf, wd_bf, *, max_tiles):
    e = pl.program_id(0)
    first = ts_ref[e]
    count = nte_ref[e]
    n_tiles = ts_ref[N_EXPERTS - 1] + nte_ref[N_EXPERTS - 1]
    ahead = TILE_BUFFERS - 1

    @pl.when(e == 0)
    def _():
        for g in range(ahead):
            @pl.when(g < n_tiles)
            def _():
                _tile_copy(x_hbm, xbuf, xsem, g, g, False).start()

    @pl.when(count > 0)
    def _():
        wgu_bf[:, 0:D_EXPERT] = wg_ref[0].astype(jnp.bfloat16)
        wgu_bf[:, D_EXPERT:] = wu_ref[0].astype(jnp.bfloat16)
        wd_bf[...] = wd_ref[0].astype(jnp.bfloat16)

    def tile_body(g, c):
        slot = g % TILE_BUFFERS
        _tile_copy(x_hbm, xbuf, xsem, g, slot, False).wait()

        @pl.when(g + ahead < n_tiles)
        def _():
            _tile_copy(x_hbm, xbuf, xsem, g + ahead, (g + ahead) % TILE_BUFFERS, False).start()

        lo, hi = _unpack_bf16_pairs(xbuf[slot])
        hgu = (jnp.dot(lo.astype(jnp.bfloat16), wgu_bf[0:HALF, :], preferred_element_type=jnp.float32)
               + jnp.dot(hi.astype(jnp.bfloat16), wgu_bf[HALF:, :], preferred_element_type=jnp.float32))
        hg = hgu[:, 0:D_EXPERT]
        act = hg * jax.nn.sigmoid(hg) * hgu[:, D_EXPERT:]
        y = jnp.dot(act.astype(jnp.bfloat16), wd_bf[...], preferred_element_type=jnp.float32)

        @pl.when(g >= TILE_BUFFERS)
        def _():
            _tile_copy(y_hbm, ybuf, ysem, g - TILE_BUFFERS, slot, True).wait()

        ybuf[slot] = _pack_bf16_pairs(y)
        _tile_copy(y_hbm, ybuf, ysem, g, slot, True).start()
        return c

    lax.fori_loop(first, first + count, tile_body, 0)

    @pl.when(e == N_EXPERTS - 1)
    def _():
        for k in range(TILE_BUFFERS, 0, -1):
            @pl.when(n_tiles >= k)
            def _():
                _tile_copy(y_hbm, ybuf, ysem, n_tiles - k, (n_tiles - k) % TILE_BUFFERS, True).wait()

        ybuf[0] = jnp.zeros((ROW_TILE, HALF), jnp.uint32)

        def fill(g, c):
            cp = _tile_copy(y_hbm, ybuf, ysem, g, 0, True)
            cp.start()
            cp.wait()
            return c

        lax.fori_loop(n_tiles, max_tiles, fill, 0)


def _run_experts(tile_start, tiles_per_expert, x_sorted, w_gate, w_up, w_down):
    n_rows = x_sorted.shape[0]
    max_tiles = n_rows // ROW_TILE
    w_map = lambda e, ts, nte: (e, 0, 0)

    grid_spec = pltpu.PrefetchScalarGridSpec(
        num_scalar_prefetch=2,
        grid=(N_EXPERTS,),
        in_specs=[
            pl.BlockSpec(memory_space=pl.ANY),
            pl.BlockSpec((1, D_MODEL, D_EXPERT), w_map),
            pl.BlockSpec((1, D_MODEL, D_EXPERT), w_map),
            pl.BlockSpec((1, D_EXPERT, D_MODEL), w_map),
        ],
        out_specs=pl.BlockSpec(memory_space=pl.ANY),
        scratch_shapes=[
            pltpu.VMEM((TILE_BUFFERS, ROW_TILE, HALF), jnp.uint32),
            pltpu.VMEM((TILE_BUFFERS, ROW_TILE, HALF), jnp.uint32),
            pltpu.SemaphoreType.DMA((TILE_BUFFERS,)),
            pltpu.SemaphoreType.DMA((TILE_BUFFERS,)),
            pltpu.VMEM((D_MODEL, 2 * D_EXPERT), jnp.bfloat16),
            pltpu.VMEM((D_EXPERT, D_MODEL), jnp.bfloat16),
        ],
    )
    return pl.pallas_call(
        functools.partial(_expert_kernel, max_tiles=max_tiles),
        grid_spec=grid_spec,
        out_shape=jax.ShapeDtypeStruct((n_rows, HALF), jnp.uint32),
        compiler_params=pltpu.CompilerParams(
            dimension_semantics=("arbitrary",), vmem_limit_bytes=VMEM_LIMIT),
    )(tile_start, tiles_per_expert, x_sorted, w_gate, w_up, w_down)


def _combine_kernel(r_ref, route_ref, ya_ref, yb_ref, g_ref, b_ref, *rest):
    o_ref = rest[-1]
    route = route_ref[...]
    w1 = route[:, R_W1:R_W1 + 1]
    w2 = route[:, R_W2:R_W2 + 1]
    a_lo, a_hi = _unpack_bf16_pairs(ya_ref[...])
    b_lo, b_hi = _unpack_bf16_pairs(yb_ref[...])
    moe = jnp.concatenate([w1 * a_lo + w2 * b_lo, w1 * a_hi + w2 * b_hi], axis=-1)
    o_ref[...] = _layernorm(r_ref[...] + moe, g_ref[...], b_ref[...])


def _run_combine(r, route, y_tok, ln2_g, ln2_b, out_prev, chunk, n_tok_total):
    n_tok = r.shape[0]
    n_t = n_tok // TOKEN_TILE
    in_specs = [
        pl.BlockSpec((TOKEN_TILE, D_MODEL), lambda i: (i, 0)),
        pl.BlockSpec((TOKEN_TILE, LANES), lambda i: (i, 0)),
        pl.BlockSpec((TOKEN_TILE, HALF), lambda i: (i, 0)),
        pl.BlockSpec((TOKEN_TILE, HALF), lambda i: (i + n_t, 0)),
        pl.BlockSpec((1, D_MODEL), lambda i: (0, 0)),
        pl.BlockSpec((1, D_MODEL), lambda i: (0, 0)),
    ]
    args = [r, route, y_tok, y_tok, ln2_g, ln2_b]
    aliases = {}
    if out_prev is not None:
        in_specs.append(pl.BlockSpec(memory_space=pl.ANY))
        args.append(out_prev)
        aliases = {len(args) - 1: 0}
    return pl.pallas_call(
        _combine_kernel,
        grid=(n_t,),
        in_specs=in_specs,
        out_specs=pl.BlockSpec((TOKEN_TILE, D_MODEL), lambda i: (chunk * n_t + i, 0)),
        out_shape=jax.ShapeDtypeStruct((n_tok_total, D_MODEL), jnp.float32),
        input_output_aliases=aliases,
        compiler_params=pltpu.CompilerParams(dimension_semantics=("arbitrary",)),
    )(*args)


def _split_bf16(w):
    hi = w.astype(jnp.bfloat16)
    lo = (w - hi.astype(jnp.float32)).astype(jnp.bfloat16)
    return hi, lo


def kernel(x, p, ln_in_g, ln_in_b, w_in, conv_w, conv_b, pool_w, pool_scale, w_out, ln1_g, ln1_b,
           w_rg, b_rg, w_re, b_re, w_gate, w_up, w_down, w_pg, b_pg, w_ple, ln2_g, ln2_b):
    batch, seq, _ = x.shape
    n_tok = batch * seq
    bf = jnp.bfloat16
    row = lambda v: v.reshape(1, -1)

    w_r = jnp.concatenate([w_rg[0], jnp.transpose(w_re[0], (1, 0, 2)).reshape(D_MODEL, N_EXPERTS)], axis=1)
    w_r = jnp.pad(w_r, ((0, 0), (0, LANES - w_r.shape[1])))
    w_r_hi, w_r_lo = _split_bf16(w_r)
    w_r_cat = jnp.concatenate([w_r_hi, w_r_hi, w_r_lo], axis=0)
    b_r = jnp.pad(jnp.concatenate([b_rg[0], b_re[0].reshape(-1)]), (0, LANES - N_GROUPS - N_EXPERTS)).reshape(1, LANES)

    x2 = x.reshape(n_tok, D_MODEL)
    p2 = p[0].reshape(n_tok, PLE_DIM)
    mixer_weights = (row(ln_in_g), row(ln_in_b), w_in[0].astype(bf), conv_w[0], row(conv_b[0]),
                     pool_w[0].astype(bf), row(pool_scale[0]), w_out[0].astype(bf), row(ln1_g[0]), row(ln1_b[0]),
                     w_r_cat, b_r, w_pg[0].astype(bf), row(b_pg[0]), w_ple[0].astype(bf))
    expert_weights = (w_gate[0].reshape(N_EXPERTS, D_MODEL, D_EXPERT),
                      w_up[0].reshape(N_EXPERTS, D_MODEL, D_EXPERT),
                      w_down[0].reshape(N_EXPERTS, D_EXPERT, D_MODEL))
    zero_rows = jnp.zeros((SC_WINDOW, HALF), jnp.uint32)

    n_chunks = N_CHUNKS if batch % N_CHUNKS == 0 else 1
    chunk_batch = batch // n_chunks
    chunk_tok = chunk_batch * seq
    n_rows = -(-(2 * chunk_tok + N_EXPERTS * (ROW_TILE - 1)) // ROW_TILE) * ROW_TILE
    out = None
    for c in range(n_chunks):
        xp, r, route, route_t, counts = _run_mixer(x2, p2, *mixer_weights, c * chunk_batch, chunk_batch, seq)
        pos, tile_start, tiles_per_expert, pad_pos = _run_plan(route_t, counts, n_rows)
        x_sorted = _dispatch_rows(xp, pos[0], pos[1], pad_pos.reshape(-1), zero_rows, n_rows)
        y_sorted = _run_experts(tile_start[0, :N_EXPERTS], tiles_per_expert[0, :N_EXPERTS], x_sorted,
                                *expert_weights)
        y_tok = _gather_rows(y_sorted, pos.reshape(-1))
        out = _run_combine(r, route, y_tok, row(ln2_g[0]), row(ln2_b[0]), out, c, n_tok)
    return out.reshape(batch, seq, D_MODEL)
```

```python
import functools

import jax
import jax.numpy as jnp
from jax import lax
from jax.experimental import pallas as pl
from jax.experimental.pallas import tpu as pltpu
from jax.experimental.pallas import tpu_sc as plsc

D_MODEL = 1024
CONV_WIDTH = 512
CONV_K = 3
POOL_WIDTH = 512
POOL_WINDOWS = (2, 4, 8, 16)
POOL_GW = 128
IN_PROJ = 3 * CONV_WIDTH + POOL_WIDTH
N_GROUPS = 4
EXPERTS_PER_GROUP = 8
N_EXPERTS = N_GROUPS * EXPERTS_PER_GROUP
D_EXPERT = 256
PLE_DIM = 256
LN_EPS = 1e-5
DEEPNORM_ALPHA = 2.0 ** 0.25

LANES = 128
HALF = D_MODEL // 2
CONV_HALO = 8
POOL_HALO = 16
SEQ_TILE = 512
SUB_TILE = 256
ROW_TILE = 256
TILE_BUFFERS = 4
N_CHUNKS = 2
SC_WORKERS = 32
SC_WINDOW = 64
TOKEN_TILE = 512
VMEM_LIMIT = 56 * 1024 * 1024

R_ID1, R_ID2, R_RANK1, R_RANK2, R_W1, R_W2 = range(6)
ROUTE_ROWS = 8


def _layernorm(x, g, b):
    mu = jnp.mean(x, axis=-1, keepdims=True)
    xc = x - mu
    var = jnp.mean(xc * xc, axis=-1, keepdims=True)
    return xc * lax.rsqrt(var + LN_EPS) * g + b


def _pack_bf16_pairs(v):
    bits = lax.bitcast_convert_type(v.astype(jnp.bfloat16).astype(jnp.float32), jnp.uint32)
    return bits[:, HALF:] | (bits[:, :HALF] >> 16)


def _unpack_bf16_pairs(w):
    lo = lax.bitcast_convert_type(w << 16, jnp.float32)
    hi = lax.bitcast_convert_type(w & jnp.uint32(0xFFFF0000), jnp.float32)
    return lo, hi


def _mixer_kernel(x_ref, p_ref, lnin_g, lnin_b, w_in, conv_w, conv_b, pool_w, pool_scale, w_out,
                  ln1_g, ln1_b, w_r, b_r, w_pg, b_pg, w_ple,
                  xp_ref, r_ref, route_ref, route_t_ref, counts_ref,
                  zbuf, vbuf, carry):
    b = pl.program_id(0)
    s = pl.program_id(1)
    ts = x_ref.shape[0]

    @pl.when(s == 0)
    def _():
        zbuf[0:CONV_HALO, :] = jnp.zeros((CONV_HALO, CONV_WIDTH), jnp.float32)
        vbuf[0:POOL_HALO, :] = jnp.zeros((POOL_HALO, POOL_WIDTH), jnp.float32)

    @pl.when((b == 0) & (s == 0))
    def _():
        carry[...] = jnp.zeros_like(carry)

    for r0 in range(0, ts, SUB_TILE):
        _mixer_rows(r0, s * ts + r0, x_ref, p_ref, lnin_g, lnin_b, w_in, conv_w, conv_b, pool_w, pool_scale,
                    w_out, ln1_g, ln1_b, w_r, b_r, w_pg, b_pg, w_ple,
                    xp_ref, r_ref, route_ref, route_t_ref, zbuf, vbuf, carry)
    zbuf[0:CONV_HALO, :] = zbuf[ts:ts + CONV_HALO, :]
    vbuf[0:POOL_HALO, :] = vbuf[ts:ts + POOL_HALO, :]
    counts_ref[...] = carry[...]


def _mixer_rows(r0, seq0, x_ref, p_ref, lnin_g, lnin_b, w_in, conv_w, conv_b, pool_w, pool_scale, w_out,
                ln1_g, ln1_b, w_r, b_r, w_pg, b_pg, w_ple,
                xp_ref, r_ref, route_ref, route_t_ref, zbuf, vbuf, carry):
    n = SUB_TILE
    rows = pl.ds(r0, n)
    h0 = _layernorm(x_ref[rows, :], lnin_g[...], lnin_b[...])
    u = jnp.dot(h0.astype(jnp.bfloat16), w_in[...], preferred_element_type=jnp.float32)
    b_g = u[:, 0:CONV_WIDTH]
    c_g = u[:, CONV_WIDTH:2 * CONV_WIDTH]
    v_c = u[:, 2 * CONV_WIDTH:3 * CONV_WIDTH]
    v_p = u[:, 3 * CONV_WIDTH:]

    zbuf[pl.ds(CONV_HALO + r0, n), :] = c_g * v_c
    zext = zbuf[pl.ds(r0, n + CONV_HALO), :]
    z1 = pltpu.roll(zext, 1, axis=0)[CONV_HALO:, :]
    z2 = pltpu.roll(zext, 2, axis=0)[CONV_HALO:, :]
    conv = z2 * conv_w[0:1, :] + z1 * conv_w[1:2, :] + zext[CONV_HALO:, :] * conv_w[2:3, :] + conv_b[...]
    y_conv = b_g * conv

    vbuf[pl.ds(POOL_HALO + r0, n), :] = v_p
    vext = vbuf[pl.ds(r0, n + POOL_HALO), :]
    s2 = vext + pltpu.roll(vext, 1, axis=0)
    s4 = s2[:, POOL_GW:] + pltpu.roll(s2[:, POOL_GW:], 2, axis=0)
    s8 = s4[:, POOL_GW:] + pltpu.roll(s4[:, POOL_GW:], 4, axis=0)
    s16 = s8[:, POOL_GW:] + pltpu.roll(s8[:, POOL_GW:], 8, axis=0)
    wsums = (s2[POOL_HALO:, 0:POOL_GW], s4[POOL_HALO:, 0:POOL_GW],
             s8[POOL_HALO:, 0:POOL_GW], s16[POOL_HALO:, 0:POOL_GW])

    t_pos = seq0 + lax.broadcasted_iota(jnp.int32, (n, 1), 0)
    y_pool = []
    for j, w in enumerate(POOL_WINDOWS):
        inv_cnt = 1.0 / jnp.minimum(t_pos + 1, w).astype(jnp.float32)
        pooled = wsums[j] * inv_cnt - v_p[:, j * POOL_GW:(j + 1) * POOL_GW]
        y_pool.append(jnp.dot(pooled.astype(jnp.bfloat16), pool_w[j], preferred_element_type=jnp.float32))
    y_pool = jnp.concatenate(y_pool, axis=-1) * pool_scale[...]

    ycat = jnp.concatenate([y_conv, y_pool], axis=-1).astype(jnp.bfloat16)
    mix = jnp.dot(ycat, w_out[...], preferred_element_type=jnp.float32)
    h1 = _layernorm(DEEPNORM_ALPHA * h0 + mix, ln1_g[...], ln1_b[...])

    h_hi = h1.astype(jnp.bfloat16)
    h_lo = (h1 - h_hi.astype(jnp.float32)).astype(jnp.bfloat16)
    xp_ref[rows, :] = _pack_bf16_pairs(h1)

    gate = jax.nn.sigmoid(jnp.dot(h_hi, w_pg[...], preferred_element_type=jnp.float32) + b_pg[...])
    ple = jnp.dot(p_ref[rows, :].astype(jnp.bfloat16), w_ple[...], preferred_element_type=jnp.float32) * gate
    r_ref[rows, :] = DEEPNORM_ALPHA * h1 + ple

    hcat = jnp.concatenate([h_hi, h_lo, h_hi], axis=-1)
    logits = jnp.dot(hcat, w_r[...], preferred_element_type=jnp.float32) + b_r[...]
    lane = lax.broadcasted_iota(jnp.int32, (n, LANES), 1)
    neg = jnp.float32(-jnp.inf)

    def first_argmax(vals):
        m = jnp.max(vals, axis=-1, keepdims=True)
        idx = jnp.min(jnp.where(vals == m, lane, LANES), axis=-1, keepdims=True)
        return m, idx

    g_mask = lane < N_GROUPS
    g_max, g_idx = first_argmax(jnp.where(g_mask, logits, neg))
    g_w = 1.0 / jnp.sum(jnp.where(g_mask, jnp.exp(logits - g_max), 0.0), axis=-1, keepdims=True)

    e_lo = N_GROUPS + EXPERTS_PER_GROUP * g_idx
    e_vals = jnp.where((lane >= e_lo) & (lane < e_lo + EXPERTS_PER_GROUP), logits, neg)
    m1, i1 = first_argmax(e_vals)
    m2, i2 = first_argmax(jnp.where(lane == i1, neg, e_vals))
    e21 = jnp.exp(m2 - m1)
    w1 = g_w / (1.0 + e21)
    w2 = g_w * e21 / (1.0 + e21)
    id1 = i1 - N_GROUPS
    id2 = i2 - N_GROUPS

    sel1 = lane == id1
    sel2 = lane == id2
    onehot = (sel1 | sel2).astype(jnp.float32)
    row = lax.broadcasted_iota(jnp.int32, (n, n), 0)
    col = lax.broadcasted_iota(jnp.int32, (n, n), 1)
    tri = (col < row).astype(jnp.bfloat16)
    before = jnp.dot(tri, onehot.astype(jnp.bfloat16), preferred_element_type=jnp.float32) + carry[...]
    rank1 = jnp.sum(jnp.where(sel1, before, 0.0), axis=-1, keepdims=True)
    rank2 = jnp.sum(jnp.where(sel2, before, 0.0), axis=-1, keepdims=True)
    carry[...] = carry[...] + jnp.sum(onehot, axis=0, keepdims=True)

    rec = jnp.zeros((n, LANES), jnp.float32)
    for k, val in ((R_ID1, id1.astype(jnp.float32)), (R_ID2, id2.astype(jnp.float32)), (R_W1, w1), (R_W2, w2),
                   (R_RANK1, rank1), (R_RANK2, rank2)):
        rec = jnp.where(lane == k, val, rec)
    route_ref[rows, :] = rec
    route_t_ref[:, rows] = jnp.transpose(rec)[0:ROUTE_ROWS, :]


def _run_mixer(x2, p2, lnin_g, lnin_b, w_in, conv_w, conv_b, pool_w, pool_scale, w_out, ln1_g, ln1_b,
               w_r, b_r, w_pg, b_pg, w_ple, batch_lo, batch, seq):
    n_tok = batch * seq
    n_s = seq // SEQ_TILE
    tok_map = lambda b, s: (b * n_s + s, 0)
    in_map = lambda b, s: ((batch_lo + b) * n_s + s, 0)

    def const(shape):
        return pl.BlockSpec(shape, lambda b, s: (0,) * len(shape), pipeline_mode=pl.Buffered(1))

    in_specs = [
        pl.BlockSpec((SEQ_TILE, D_MODEL), in_map),
        pl.BlockSpec((SEQ_TILE, PLE_DIM), in_map),
        const((1, D_MODEL)), const((1, D_MODEL)),
        const((D_MODEL, IN_PROJ)),
        const((CONV_K, CONV_WIDTH)), const((1, CONV_WIDTH)),
        const((len(POOL_WINDOWS), POOL_GW, POOL_GW)), const((1, POOL_WIDTH)),
        const((D_MODEL, D_MODEL)),
        const((1, D_MODEL)), const((1, D_MODEL)),
        const((3 * D_MODEL, LANES)), const((1, LANES)),
        const((D_MODEL, D_MODEL)), const((1, D_MODEL)),
        const((PLE_DIM, D_MODEL)),
    ]
    out_specs = [
        pl.BlockSpec((SEQ_TILE, HALF), tok_map),
        pl.BlockSpec((SEQ_TILE, D_MODEL), tok_map),
        pl.BlockSpec((SEQ_TILE, LANES), tok_map),
        pl.BlockSpec((ROUTE_ROWS, SEQ_TILE), lambda b, s: (0, b * n_s + s)),
        pl.BlockSpec((1, LANES), lambda b, s: (0, 0)),
    ]
    out_shape = [
        jax.ShapeDtypeStruct((n_tok, HALF), jnp.uint32),
        jax.ShapeDtypeStruct((n_tok, D_MODEL), jnp.float32),
        jax.ShapeDtypeStruct((n_tok, LANES), jnp.float32),
        jax.ShapeDtypeStruct((ROUTE_ROWS, n_tok), jnp.float32),
        jax.ShapeDtypeStruct((1, LANES), jnp.float32),
    ]
    return pl.pallas_call(
        _mixer_kernel,
        grid=(batch, n_s),
        in_specs=in_specs,
        out_specs=out_specs,
        out_shape=out_shape,
        scratch_shapes=[
            pltpu.VMEM((SEQ_TILE + CONV_HALO, CONV_WIDTH), jnp.float32),
            pltpu.VMEM((SEQ_TILE + POOL_HALO, POOL_WIDTH), jnp.float32),
            pltpu.VMEM((1, LANES), jnp.float32),
        ],
        compiler_params=pltpu.CompilerParams(
            dimension_semantics=("arbitrary", "arbitrary"), vmem_limit_bytes=VMEM_LIMIT),
    )(x2, p2, lnin_g, lnin_b, w_in, conv_w, conv_b, pool_w, pool_scale, w_out, ln1_g, ln1_b,
      w_r, b_r, w_pg, b_pg, w_ple)


def _plan_kernel(rt_ref, counts_ref, pos_ref, tile_start_ref, tiles_ref, pad_ref, *, n_rows):
    lane = lax.broadcasted_iota(jnp.int32, (ROUTE_ROWS, LANES), 1)
    counts = jnp.broadcast_to(counts_ref[...], (ROUTE_ROWS, LANES))
    tiles = jnp.floor((counts + (ROW_TILE - 1)) * (1.0 / ROW_TILE))
    tile_end = tiles
    shift = 1
    while shift < N_EXPERTS:
        tile_end = tile_end + jnp.where(lane >= shift, pltpu.roll(tile_end, shift, axis=1), 0.0)
        shift *= 2
    row_start = (tile_end - tiles) * ROW_TILE
    tile_start_ref[...] = (tile_end - tiles)[0:1, :].astype(jnp.int32)
    tiles_ref[...] = tiles[0:1, :].astype(jnp.int32)

    rt = rt_ref[...]
    ids = rt[R_ID1:R_ID2 + 1, :]
    start = jnp.zeros_like(ids)
    for e in range(N_EXPERTS):
        start = jnp.where(ids == e, row_start[0:1, e:e + 1], start)
    pos_ref[...] = (start + rt[R_RANK1:R_RANK2 + 1, :]).astype(jnp.int32)

    sub = lax.broadcasted_iota(jnp.int32, (N_EXPERTS, LANES), 0)
    lane_e = lax.broadcasted_iota(jnp.int32, (N_EXPERTS, LANES), 1)
    diag = sub == lane_e
    pad_lo = jnp.sum(jnp.where(diag, (row_start + counts)[0:1, :], 0.0), axis=1, keepdims=True)
    pad_n = jnp.sum(jnp.where(diag, (tiles * ROW_TILE - counts)[0:1, :], 0.0), axis=1, keepdims=True)
    j = lax.broadcasted_iota(jnp.int32, (N_EXPERTS, ROW_TILE), 1).astype(jnp.float32)
    pad_ref[...] = jnp.where(j < pad_n, pad_lo + j, n_rows + j).astype(jnp.int32)


def _run_plan(route_t, counts, n_rows):
    n_tok = route_t.shape[1]
    full = lambda shape: pl.BlockSpec(shape, lambda i: (0, 0))
    return pl.pallas_call(
        functools.partial(_plan_kernel, n_rows=n_rows),
        grid=(1,),
        in_specs=[full((ROUTE_ROWS, n_tok)), full((1, LANES))],
        out_specs=[full((2, n_tok)), full((1, LANES)), full((1, LANES)), full((N_EXPERTS, ROW_TILE))],
        out_shape=[
            jax.ShapeDtypeStruct((2, n_tok), jnp.int32),
            jax.ShapeDtypeStruct((1, LANES), jnp.int32),
            jax.ShapeDtypeStruct((1, LANES), jnp.int32),
            jax.ShapeDtypeStruct((N_EXPERTS, ROW_TILE), jnp.int32),
        ],
        compiler_params=pltpu.CompilerParams(dimension_semantics=("arbitrary",)),
    )(route_t, counts)


def _sc_mesh():
    return plsc.VectorSubcoreMesh(core_axis_name="core", subcore_axis_name="subcore")


def _sc_worker_id():
    return lax.axis_index("core") * (SC_WORKERS // 2) + lax.axis_index("subcore")


def _dispatch_rows(xp, pos1, pos2, pad_pos, zero_rows, n_rows):
    n_tok, width = xp.shape
    n_win = n_tok // SC_WORKERS // SC_WINDOW
    n_pad = pad_pos.shape[0] // SC_WORKERS // SC_WINDOW
    as_windows = lambda v: v.reshape(-1, SC_WINDOW)

    @functools.partial(
        pl.kernel, out_type=jax.ShapeDtypeStruct((n_rows + ROW_TILE, width), xp.dtype), mesh=_sc_mesh(),
        scratch_types=[pltpu.VMEM((n_win, SC_WINDOW), jnp.int32), pltpu.VMEM((n_win, SC_WINDOW), jnp.int32),
                       pltpu.VMEM((n_pad, SC_WINDOW), jnp.int32),
                       pltpu.VMEM((2, SC_WINDOW, width), xp.dtype),
                       pltpu.SemaphoreType.DMA((2,)), pltpu.SemaphoreType.DMA((2,))])
    def dispatch(xp_hbm, pos1_hbm, pos2_hbm, pad_hbm, zero_hbm, out_hbm, idx1, idx2, idxp, buf, lsem, ssem):
        wid = _sc_worker_id()
        pltpu.sync_copy(pos1_hbm.at[pl.ds(wid * n_win, n_win)], idx1)
        pltpu.sync_copy(pos2_hbm.at[pl.ds(wid * n_win, n_win)], idx2)
        pltpu.sync_copy(pad_hbm.at[pl.ds(wid * n_pad, n_pad)], idxp)

        def load(j):
            rows = xp_hbm.at[pl.ds((wid * n_win + j) * SC_WINDOW, SC_WINDOW)]
            return pltpu.make_async_copy(rows, buf.at[j % 2], lsem.at[j % 2])

        def scatters(j):
            return [pltpu.make_async_copy(buf.at[j % 2], out_hbm.at[idx.at[j]], ssem.at[j % 2])
                    for idx in (idx1, idx2)]

        load(0).start()
        for j in range(n_win):
            load(j).wait()
            for cp in scatters(j):
                cp.start()
            if j >= 1:
                for cp in scatters(j - 1):
                    cp.wait()
            if j + 1 < n_win:
                load(j + 1).start()
        for cp in scatters(n_win - 1):
            cp.wait()

        pltpu.sync_copy(zero_hbm, buf.at[0])
        pads = [pltpu.make_async_copy(buf.at[0], out_hbm.at[idxp.at[j]], ssem.at[0]) for j in range(n_pad)]
        for cp in pads:
            cp.start()
        for cp in pads:
            cp.wait()

    return dispatch(xp, as_windows(pos1), as_windows(pos2), as_windows(pad_pos), zero_rows)


def _gather_rows(src, idx):
    n_out, width = idx.shape[0], src.shape[1]
    n_win = n_out // SC_WORKERS // SC_WINDOW

    @functools.partial(
        pl.kernel, out_type=jax.ShapeDtypeStruct((n_out, width), src.dtype), mesh=_sc_mesh(),
        scratch_types=[pltpu.VMEM((n_win, SC_WINDOW), jnp.int32), pltpu.VMEM((2, SC_WINDOW, width), src.dtype),
                       pltpu.SemaphoreType.DMA((2,)), pltpu.SemaphoreType.DMA((2,))])
    def gather(src_hbm, idx_hbm, dst_hbm, idx_v, buf, gsem, ssem):
        wid = _sc_worker_id()
        pltpu.sync_copy(idx_hbm.at[pl.ds(wid * n_win, n_win)], idx_v)

        def fetch(j):
            return pltpu.make_async_copy(src_hbm.at[idx_v.at[j]], buf.at[j % 2], gsem.at[j % 2])

        def store(j):
            rows = dst_hbm.at[pl.ds((wid * n_win + j) * SC_WINDOW, SC_WINDOW)]
            return pltpu.make_async_copy(buf.at[j % 2], rows, ssem.at[j % 2])

        fetch(0).start()
        for j in range(n_win):
            fetch(j).wait()
            store(j).start()
            if j >= 1:
                store(j - 1).wait()
            if j + 1 < n_win:
                fetch(j + 1).start()
        store(n_win - 1).wait()

    return gather(src, idx.reshape(-1, SC_WINDOW))


def _tile_copy(hbm, buf, sem, tile, slot, to_hbm):
    rows = hbm.at[pl.ds(pl.multiple_of(tile * ROW_TILE, ROW_TILE), ROW_TILE)]
    if to_hbm:
        return pltpu.make_async_copy(buf.at[slot], rows, sem.at[slot])
    return pltpu.make_async_copy(rows, buf.at[slot], sem.at[slot])


def _expert_kernel(ts_ref, nte_ref, x_hbm, wg_ref, wu_ref, wd_ref, y_hbm,
                   xbuf, ybuf, xsem, ysem, wgu_bf, wd_bf, *, max_tiles):
    e = pl.program_id(0)
    first = ts_ref[e]
    count = nte_ref[e]
    n_tiles = ts_ref[N_EXPERTS - 1] + nte_ref[N_EXPERTS - 1]
    ahead = TILE_BUFFERS - 1

    @pl.when(e == 0)
    def _():
        for g in range(ahead):
            @pl.when(g < n_tiles)
            def _():
                _tile_copy(x_hbm, xbuf, xsem, g, g, False).start()

    @pl.when(count > 0)
    def _():
        wgu_bf[:, 0:D_EXPERT] = wg_ref[0].astype(jnp.bfloat16)
        wgu_bf[:, D_EXPERT:] = wu_ref[0].astype(jnp.bfloat16)
        wd_bf[...] = wd_ref[0].astype(jnp.bfloat16)

    def tile_body(g, c):
        slot = g % TILE_BUFFERS
        _tile_copy(x_hbm, xbuf, xsem, g, slot, False).wait()

        @pl.when(g + ahead < n_tiles)
        def _():
            _tile_copy(x_hbm, xbuf, xsem, g + ahead, (g + ahead) % TILE_BUFFERS, False).start()

        lo, hi = _unpack_bf16_pairs(xbuf[slot])
        hgu = (jnp.dot(lo.astype(jnp.bfloat16), wgu_bf[0:HALF, :], preferred_element_type=jnp.float32)
               + jnp.dot(hi.astype(jnp.bfloat16), wgu_bf[HALF:, :], preferred_element_type=jnp.float32))
        hg = hgu[:, 0:D_EXPERT]
        act = hg * jax.nn.sigmoid(hg) * hgu[:, D_EXPERT:]
        y = jnp.dot(act.astype(jnp.bfloat16), wd_bf[...], preferred_element_type=jnp.float32)

        @pl.when(g >= TILE_BUFFERS)
        def _():
            _tile_copy(y_hbm, ybuf, ysem, g - TILE_BUFFERS, slot, True).wait()

        ybuf[slot] = _pack_bf16_pairs(y)
        _tile_copy(y_hbm, ybuf, ysem, g, slot, True).start()
        return c

    lax.fori_loop(first, first + count, tile_body, 0)

    @pl.when(e == N_EXPERTS - 1)
    def _():
        for k in range(TILE_BUFFERS, 0, -1):
            @pl.when(n_tiles >= k)
            def _():
                _tile_copy(y_hbm, ybuf, ysem, n_tiles - k, (n_tiles - k) % TILE_BUFFERS, True).wait()

        ybuf[0] = jnp.zeros((ROW_TILE, HALF), jnp.uint32)

        def fill(g, c):
            cp = _tile_copy(y_hbm, ybuf, ysem, g, 0, True)
            cp.start()
            cp.wait()
            return c

        lax.fori_loop(n_tiles, max_tiles, fill, 0)


def _run_experts(tile_start, tiles_per_expert, x_sorted, w_gate, w_up, w_down):
    n_rows = x_sorted.shape[0]
    max_tiles = n_rows // ROW_TILE
    w_map = lambda e, ts, nte: (e, 0, 0)

    grid_spec = pltpu.PrefetchScalarGridSpec(
        num_scalar_prefetch=2,
        grid=(N_EXPERTS,),
        in_specs=[
            pl.BlockSpec(memory_space=pl.ANY),
            pl.BlockSpec((1, D_MODEL, D_EXPERT), w_map),
            pl.BlockSpec((1, D_MODEL, D_EXPERT), w_map),
            pl.BlockSpec((1, D_EXPERT, D_MODEL), w_map),
        ],
        out_specs=pl.BlockSpec(memory_space=pl.ANY),
        scratch_shapes=[
            pltpu.VMEM((TILE_BUFFERS, ROW_TILE, HALF), jnp.uint32),
            pltpu.VMEM((TILE_BUFFERS, ROW_TILE, HALF), jnp.uint32),
            pltpu.SemaphoreType.DMA((TILE_BUFFERS,)),
            pltpu.SemaphoreType.DMA((TILE_BUFFERS,)),
            pltpu.VMEM((D_MODEL, 2 * D_EXPERT), jnp.bfloat16),
            pltpu.VMEM((D_EXPERT, D_MODEL), jnp.bfloat16),
        ],
    )
    return pl.pallas_call(
        functools.partial(_expert_kernel, max_tiles=max_tiles),
        grid_spec=grid_spec,
        out_shape=jax.ShapeDtypeStruct((n_rows, HALF), jnp.uint32),
        compiler_params=pltpu.CompilerParams(
            dimension_semantics=("arbitrary",), vmem_limit_bytes=VMEM_LIMIT),
    )(tile_start, tiles_per_expert, x_sorted, w_gate, w_up, w_down)


def _combine_kernel(r_ref, route_ref, ya_ref, yb_ref, g_ref, b_ref, *rest):
    o_ref = rest[-1]
    route = route_ref[...]
    w1 = route[:, R_W1:R_W1 + 1]
    w2 = route[:, R_W2:R_W2 + 1]
    a_lo, a_hi = _unpack_bf16_pairs(ya_ref[...])
    b_lo, b_hi = _unpack_bf16_pairs(yb_ref[...])
    moe = jnp.concatenate([w1 * a_lo + w2 * b_lo, w1 * a_hi + w2 * b_hi], axis=-1)
    o_ref[...] = _layernorm(r_ref[...] + moe, g_ref[...], b_ref[...])


def _run_combine(r, route, y_tok, ln2_g, ln2_b, out_prev, chunk, n_tok_total):
    n_tok = r.shape[0]
    n_t = n_tok // TOKEN_TILE
    in_specs = [
        pl.BlockSpec((TOKEN_TILE, D_MODEL), lambda i: (i, 0)),
        pl.BlockSpec((TOKEN_TILE, LANES), lambda i: (i, 0)),
        pl.BlockSpec((TOKEN_TILE, HALF), lambda i: (i, 0)),
        pl.BlockSpec((TOKEN_TILE, HALF), lambda i: (i + n_t, 0)),
        pl.BlockSpec((1, D_MODEL), lambda i: (0, 0)),
        pl.BlockSpec((1, D_MODEL), lambda i: (0, 0)),
    ]
    args = [r, route, y_tok, y_tok, ln2_g, ln2_b]
    aliases = {}
    if out_prev is not None:
        in_specs.append(pl.BlockSpec(memory_space=pl.ANY))
        args.append(out_prev)
        aliases = {len(args) - 1: 0}
    return pl.pallas_call(
        _combine_kernel,
        grid=(n_t,),
        in_specs=in_specs,
        out_specs=pl.BlockSpec((TOKEN_TILE, D_MODEL), lambda i: (chunk * n_t + i, 0)),
        out_shape=jax.ShapeDtypeStruct((n_tok_total, D_MODEL), jnp.float32),
        input_output_aliases=aliases,
        compiler_params=pltpu.CompilerParams(dimension_semantics=("arbitrary",)),
    )(*args)


def _split_bf16(w):
    hi = w.astype(jnp.bfloat16)
    lo = (w - hi.astype(jnp.float32)).astype(jnp.bfloat16)
    return hi, lo


def kernel(x, p, ln_in_g, ln_in_b, w_in, conv_w, conv_b, pool_w, pool_scale, w_out, ln1_g, ln1_b,
           w_rg, b_rg, w_re, b_re, w_gate, w_up, w_down, w_pg, b_pg, w_ple, ln2_g, ln2_b):
    batch, seq, _ = x.shape
    n_tok = batch * seq
    bf = jnp.bfloat16
    row = lambda v: v.reshape(1, -1)

    w_r = jnp.concatenate([w_rg[0], jnp.transpose(w_re[0], (1, 0, 2)).reshape(D_MODEL, N_EXPERTS)], axis=1)
    w_r = jnp.pad(w_r, ((0, 0), (0, LANES - w_r.shape[1])))
    w_r_hi, w_r_lo = _split_bf16(w_r)
    w_r_cat = jnp.concatenate([w_r_hi, w_r_hi, w_r_lo], axis=0)
    b_r = jnp.pad(jnp.concatenate([b_rg[0], b_re[0].reshape(-1)]), (0, LANES - N_GROUPS - N_EXPERTS)).reshape(1, LANES)

    x2 = x.reshape(n_tok, D_MODEL)
    p2 = p[0].reshape(n_tok, PLE_DIM)
    mixer_weights = (row(ln_in_g), row(ln_in_b), w_in[0].astype(bf), conv_w[0], row(conv_b[0]),
                     pool_w[0].astype(bf), row(pool_scale[0]), w_out[0].astype(bf), row(ln1_g[0]), row(ln1_b[0]),
                     w_r_cat, b_r, w_pg[0].astype(bf), row(b_pg[0]), w_ple[0].astype(bf))
    expert_weights = (w_gate[0].reshape(N_EXPERTS, D_MODEL, D_EXPERT),
                      w_up[0].reshape(N_EXPERTS, D_MODEL, D_EXPERT),
                      w_down[0].reshape(N_EXPERTS, D_EXPERT, D_MODEL))
    zero_rows = jnp.zeros((SC_WINDOW, HALF), jnp.uint32)

    n_chunks = N_CHUNKS if batch % N_CHUNKS == 0 else 1
    chunk_batch = batch // n_chunks
    chunk_tok = chunk_batch * seq
    n_rows = -(-(2 * chunk_tok + N_EXPERTS * (ROW_TILE - 1)) // ROW_TILE) * ROW_TILE
    out = None
    for c in range(n_chunks):
        xp, r, route, route_t, counts = _run_mixer(x2, p2, *mixer_weights, c * chunk_batch, chunk_batch, seq)
        pos, tile_start, tiles_per_expert, pad_pos = _run_plan(route_t, counts, n_rows)
        x_sorted = _dispatch_rows(xp, pos[0], pos[1], pad_pos.reshape(-1), zero_rows, n_rows)
        y_sorted = _run_experts(tile_start[0, :N_EXPERTS], tiles_per_expert[0, :N_EXPERTS], x_sorted,
                                *expert_weights)
        y_tok = _gather_rows(y_sorted, pos.reshape(-1))
        out = _run_combine(r, route, y_tok, row(ln2_g[0]), row(ln2_b[0]), out, c, n_tok)
    return out.reshape(batch, seq, D_MODEL)
```

```python
import functools

import jax
import jax.numpy as jnp
from jax import lax
from jax.experimental import pallas as pl
from jax.experimental.pallas import tpu as pltpu
from jax.experimental.pallas import tpu_sc as plsc

D_MODEL = 1024
CONV_WIDTH = 512
CONV_K = 3
POOL_WIDTH = 512
POOL_WINDOWS = (2, 4, 8, 16)
POOL_GW = 128
IN_PROJ = 3 * CONV_WIDTH + POOL_WIDTH
N_GROUPS = 4
EXPERTS_PER_GROUP = 8
N_EXPERTS = N_GROUPS * EXPERTS_PER_GROUP
D_EXPERT = 256
PLE_DIM = 256
LN_EPS = 1e-5
DEEPNORM_ALPHA = 2.0 ** 0.25

LANES = 128
HALF = D_MODEL // 2
CONV_HALO = 8
POOL_HALO = 16
SEQ_TILE = 1024
SUB_TILE = 256
MIXER_PHASES = 7
ROW_TILE = 256
TILE_BUFFERS = 4
N_CHUNKS = 2
SC_WORKERS = 32
SC_WINDOW = 64
TOKEN_TILE = 512
VMEM_LIMIT = 56 * 1024 * 1024

R_ID1, R_ID2, R_RANK1, R_RANK2, R_W1, R_W2 = range(6)
ROUTE_ROWS = 8


def _layernorm(x, g, b):
    mu = jnp.mean(x, axis=-1, keepdims=True)
    xc = x - mu
    var = jnp.mean(xc * xc, axis=-1, keepdims=True)
    return xc * lax.rsqrt(var + LN_EPS) * g + b


def _pack_bf16_pairs(v):
    bits = lax.bitcast_convert_type(v.astype(jnp.bfloat16).astype(jnp.float32), jnp.uint32)
    return bits[:, HALF:] | (bits[:, :HALF] >> 16)


def _unpack_bf16_pairs(w):
    lo = lax.bitcast_convert_type(w << 16, jnp.float32)
    hi = lax.bitcast_convert_type(w & jnp.uint32(0xFFFF0000), jnp.float32)
    return lo, hi


def _mixer_kernel(x_ref, p_ref, lnin_g, lnin_b, w_in, conv_w, conv_b, pool_w, pool_scale, w_out,
                  ln1_g, ln1_b, w_r, b_r, w_pg, b_pg, w_ple,
                  xp_ref, r_ref, route_ref, route_t_ref, counts_ref,
                  zbuf, vbuf, carry):
    b = pl.program_id(0)
    s = pl.program_id(1)
    ts = x_ref.shape[0]

    @pl.when(s == 0)
    def _():
        zbuf[0:CONV_HALO, :] = jnp.zeros((CONV_HALO, CONV_WIDTH), jnp.float32)
        vbuf[0:POOL_HALO, :] = jnp.zeros((POOL_HALO, POOL_WIDTH), jnp.float32)

    @pl.when((b == 0) & (s == 0))
    def _():
        carry[...] = jnp.zeros_like(carry)

    chains = [_mixer_rows(r0, s * ts + r0, x_ref, p_ref, lnin_g, lnin_b, w_in, conv_w, conv_b, pool_w, pool_scale,
                          w_out, ln1_g, ln1_b, w_r, b_r, w_pg, b_pg, w_ple,
                          xp_ref, r_ref, route_ref, route_t_ref, zbuf, vbuf, carry)
              for r0 in range(0, ts, SUB_TILE)]
    for t in range(MIXER_PHASES + len(chains) - 1):
        for k, chain in enumerate(chains):
            if 0 <= t - k < MIXER_PHASES:
                next(chain, None)
    zbuf[0:CONV_HALO, :] = zbuf[ts:ts + CONV_HALO, :]
    vbuf[0:POOL_HALO, :] = vbuf[ts:ts + POOL_HALO, :]
    counts_ref[...] = carry[...]


def _mixer_rows(r0, seq0, x_ref, p_ref, lnin_g, lnin_b, w_in, conv_w, conv_b, pool_w, pool_scale, w_out,
                ln1_g, ln1_b, w_r, b_r, w_pg, b_pg, w_ple,
                xp_ref, r_ref, route_ref, route_t_ref, zbuf, vbuf, carry):
    n = SUB_TILE
    rows = pl.ds(r0, n)
    h0 = _layernorm(x_ref[rows, :], lnin_g[...], lnin_b[...])
    h0b = h0.astype(jnp.bfloat16)
    yield
    u = jnp.dot(h0b, w_in[...], preferred_element_type=jnp.float32)
    yield
    b_g = u[:, 0:CONV_WIDTH]
    c_g = u[:, CONV_WIDTH:2 * CONV_WIDTH]
    v_c = u[:, 2 * CONV_WIDTH:3 * CONV_WIDTH]
    v_p = u[:, 3 * CONV_WIDTH:]

    zbuf[pl.ds(CONV_HALO + r0, n), :] = c_g * v_c
    zext = zbuf[pl.ds(r0, n + CONV_HALO), :]
    z1 = pltpu.roll(zext, 1, axis=0)[CONV_HALO:, :]
    z2 = pltpu.roll(zext, 2, axis=0)[CONV_HALO:, :]
    conv = z2 * conv_w[0:1, :] + z1 * conv_w[1:2, :] + zext[CONV_HALO:, :] * conv_w[2:3, :] + conv_b[...]
    y_conv = b_g * conv

    vbuf[pl.ds(POOL_HALO + r0, n), :] = v_p
    vext = vbuf[pl.ds(r0, n + POOL_HALO), :]
    s2 = vext + pltpu.roll(vext, 1, axis=0)
    s4 = s2[:, POOL_GW:] + pltpu.roll(s2[:, POOL_GW:], 2, axis=0)
    s8 = s4[:, POOL_GW:] + pltpu.roll(s4[:, POOL_GW:], 4, axis=0)
    s16 = s8[:, POOL_GW:] + pltpu.roll(s8[:, POOL_GW:], 8, axis=0)
    wsums = (s2[POOL_HALO:, 0:POOL_GW], s4[POOL_HALO:, 0:POOL_GW],
             s8[POOL_HALO:, 0:POOL_GW], s16[POOL_HALO:, 0:POOL_GW])

    t_pos = seq0 + lax.broadcasted_iota(jnp.int32, (n, 1), 0)
    y_pool = []
    for j, w in enumerate(POOL_WINDOWS):
        inv_cnt = 1.0 / jnp.minimum(t_pos + 1, w).astype(jnp.float32)
        pooled = wsums[j] * inv_cnt - v_p[:, j * POOL_GW:(j + 1) * POOL_GW]
        y_pool.append(jnp.dot(pooled.astype(jnp.bfloat16), pool_w[j], preferred_element_type=jnp.float32))
    y_pool = jnp.concatenate(y_pool, axis=-1) * pool_scale[...]

    ycat = jnp.concatenate([y_conv, y_pool], axis=-1).astype(jnp.bfloat16)
    yield
    mix = jnp.dot(ycat, w_out[...], preferred_element_type=jnp.float32)
    yield
    h1 = _layernorm(DEEPNORM_ALPHA * h0 + mix, ln1_g[...], ln1_b[...])

    h_hi = h1.astype(jnp.bfloat16)
    h_lo = (h1 - h_hi.astype(jnp.float32)).astype(jnp.bfloat16)
    xp_ref[rows, :] = _pack_bf16_pairs(h1)
    p_b = p_ref[rows, :].astype(jnp.bfloat16)
    hcat = jnp.concatenate([h_hi, h_lo, h_hi], axis=-1)
    yield
    gate_pre = jnp.dot(h_hi, w_pg[...], preferred_element_type=jnp.float32)
    ple_pre = jnp.dot(p_b, w_ple[...], preferred_element_type=jnp.float32)
    logits = jnp.dot(hcat, w_r[...], preferred_element_type=jnp.float32) + b_r[...]
    yield

    r_ref[rows, :] = DEEPNORM_ALPHA * h1 + ple_pre * jax.nn.sigmoid(gate_pre + b_pg[...])
    lane = lax.broadcasted_iota(jnp.int32, (n, LANES), 1)
    neg = jnp.float32(-jnp.inf)

    def first_argmax(vals):
        m = jnp.max(vals, axis=-1, keepdims=True)
        idx = jnp.min(jnp.where(vals == m, lane, LANES), axis=-1, keepdims=True)
        return m, idx

    g_mask = lane < N_GROUPS
    g_max, g_idx = first_argmax(jnp.where(g_mask, logits, neg))
    g_w = 1.0 / jnp.sum(jnp.where(g_mask, jnp.exp(logits - g_max), 0.0), axis=-1, keepdims=True)

    e_lo = N_GROUPS + EXPERTS_PER_GROUP * g_idx
    e_vals = jnp.where((lane >= e_lo) & (lane < e_lo + EXPERTS_PER_GROUP), logits, neg)
    m1, i1 = first_argmax(e_vals)
    m2, i2 = first_argmax(jnp.where(lane == i1, neg, e_vals))
    e21 = jnp.exp(m2 - m1)
    w1 = g_w / (1.0 + e21)
    w2 = g_w * e21 / (1.0 + e21)
    id1 = i1 - N_GROUPS
    id2 = i2 - N_GROUPS

    sel1 = lane == id1
    sel2 = lane == id2
    onehot = (sel1 | sel2).astype(jnp.float32)
    row = lax.broadcasted_iota(jnp.int32, (n, n), 0)
    col = lax.broadcasted_iota(jnp.int32, (n, n), 1)
    tri = (col < row).astype(jnp.bfloat16)
    before = jnp.dot(tri, onehot.astype(jnp.bfloat16), preferred_element_type=jnp.float32) + carry[...]
    rank1 = jnp.sum(jnp.where(sel1, before, 0.0), axis=-1, keepdims=True)
    rank2 = jnp.sum(jnp.where(sel2, before, 0.0), axis=-1, keepdims=True)
    carry[...] = carry[...] + jnp.sum(onehot, axis=0, keepdims=True)

    rec = jnp.zeros((n, LANES), jnp.float32)
    for k, val in ((R_ID1, id1.astype(jnp.float32)), (R_ID2, id2.astype(jnp.float32)), (R_W1, w1), (R_W2, w2),
                   (R_RANK1, rank1), (R_RANK2, rank2)):
        rec = jnp.where(lane == k, val, rec)
    route_ref[rows, :] = rec
    route_t_ref[:, rows] = jnp.transpose(rec)[0:ROUTE_ROWS, :]


def _run_mixer(x2, p2, lnin_g, lnin_b, w_in, conv_w, conv_b, pool_w, pool_scale, w_out, ln1_g, ln1_b,
               w_r, b_r, w_pg, b_pg, w_ple, batch_lo, batch, seq):
    n_tok = batch * seq
    n_s = seq // SEQ_TILE
    tok_map = lambda b, s: (b * n_s + s, 0)
    in_map = lambda b, s: ((batch_lo + b) * n_s + s, 0)

    def const(shape):
        return pl.BlockSpec(shape, lambda b, s: (0,) * len(shape), pipeline_mode=pl.Buffered(1))

    in_specs = [
        pl.BlockSpec((SEQ_TILE, D_MODEL), in_map),
        pl.BlockSpec((SEQ_TILE, PLE_DIM), in_map),
        const((1, D_MODEL)), const((1, D_MODEL)),
        const((D_MODEL, IN_PROJ)),
        const((CONV_K, CONV_WIDTH)), const((1, CONV_WIDTH)),
        const((len(POOL_WINDOWS), POOL_GW, POOL_GW)), const((1, POOL_WIDTH)),
        const((D_MODEL, D_MODEL)),
        const((1, D_MODEL)), const((1, D_MODEL)),
        const((3 * D_MODEL, LANES)), const((1, LANES)),
        const((D_MODEL, D_MODEL)), const((1, D_MODEL)),
        const((PLE_DIM, D_MODEL)),
    ]
    out_specs = [
        pl.BlockSpec((SEQ_TILE, HALF), tok_map),
        pl.BlockSpec((SEQ_TILE, D_MODEL), tok_map),
        pl.BlockSpec((SEQ_TILE, LANES), tok_map),
        pl.BlockSpec((ROUTE_ROWS, SEQ_TILE), lambda b, s: (0, b * n_s + s)),
        pl.BlockSpec((1, LANES), lambda b, s: (0, 0)),
    ]
    out_shape = [
        jax.ShapeDtypeStruct((n_tok, HALF), jnp.uint32),
        jax.ShapeDtypeStruct((n_tok, D_MODEL), jnp.float32),
        jax.ShapeDtypeStruct((n_tok, LANES), jnp.float32),
        jax.ShapeDtypeStruct((ROUTE_ROWS, n_tok), jnp.float32),
        jax.ShapeDtypeStruct((1, LANES), jnp.float32),
    ]
    return pl.pallas_call(
        _mixer_kernel,
        grid=(batch, n_s),
        in_specs=in_specs,
        out_specs=out_specs,
        out_shape=out_shape,
        scratch_shapes=[
            pltpu.VMEM((SEQ_TILE + CONV_HALO, CONV_WIDTH), jnp.float32),
            pltpu.VMEM((SEQ_TILE + POOL_HALO, POOL_WIDTH), jnp.float32),
            pltpu.VMEM((1, LANES), jnp.float32),
        ],
        compiler_params=pltpu.CompilerParams(
            dimension_semantics=("arbitrary", "arbitrary"), vmem_limit_bytes=VMEM_LIMIT),
    )(x2, p2, lnin_g, lnin_b, w_in, conv_w, conv_b, pool_w, pool_scale, w_out, ln1_g, ln1_b,
      w_r, b_r, w_pg, b_pg, w_ple)


def _plan_kernel(rt_ref, counts_ref, pos_ref, tile_start_ref, tiles_ref, pad_ref, *, n_rows):
    lane = lax.broadcasted_iota(jnp.int32, (ROUTE_ROWS, LANES), 1)
    counts = jnp.broadcast_to(counts_ref[...], (ROUTE_ROWS, LANES))
    tiles = jnp.floor((counts + (ROW_TILE - 1)) * (1.0 / ROW_TILE))
    tile_end = tiles
    shift = 1
    while shift < N_EXPERTS:
        tile_end = tile_end + jnp.where(lane >= shift, pltpu.roll(tile_end, shift, axis=1), 0.0)
        shift *= 2
    row_start = (tile_end - tiles) * ROW_TILE
    tile_start_ref[...] = (tile_end - tiles)[0:1, :].astype(jnp.int32)
    tiles_ref[...] = tiles[0:1, :].astype(jnp.int32)

    rt = rt_ref[...]
    ids = rt[R_ID1:R_ID2 + 1, :]
    start = jnp.zeros_like(ids)
    for e in range(N_EXPERTS):
        start = jnp.where(ids == e, row_start[0:1, e:e + 1], start)
    pos_ref[...] = (start + rt[R_RANK1:R_RANK2 + 1, :]).astype(jnp.int32)

    sub = lax.broadcasted_iota(jnp.int32, (N_EXPERTS, LANES), 0)
    lane_e = lax.broadcasted_iota(jnp.int32, (N_EXPERTS, LANES), 1)
    diag = sub == lane_e
    pad_lo = jnp.sum(jnp.where(diag, (row_start + counts)[0:1, :], 0.0), axis=1, keepdims=True)
    pad_n = jnp.sum(jnp.where(diag, (tiles * ROW_TILE - counts)[0:1, :], 0.0), axis=1, keepdims=True)
    j = lax.broadcasted_iota(jnp.int32, (N_EXPERTS, ROW_TILE), 1).astype(jnp.float32)
    pad_ref[...] = jnp.where(j < pad_n, pad_lo + j, n_rows + j).astype(jnp.int32)


def _run_plan(route_t, counts, n_rows):
    n_tok = route_t.shape[1]
    full = lambda shape: pl.BlockSpec(shape, lambda i: (0, 0))
    return pl.pallas_call(
        functools.partial(_plan_kernel, n_rows=n_rows),
        grid=(1,),
        in_specs=[full((ROUTE_ROWS, n_tok)), full((1, LANES))],
        out_specs=[full((2, n_tok)), full((1, LANES)), full((1, LANES)), full((N_EXPERTS, ROW_TILE))],
        out_shape=[
            jax.ShapeDtypeStruct((2, n_tok), jnp.int32),
            jax.ShapeDtypeStruct((1, LANES), jnp.int32),
            jax.ShapeDtypeStruct((1, LANES), jnp.int32),
            jax.ShapeDtypeStruct((N_EXPERTS, ROW_TILE), jnp.int32),
        ],
        compiler_params=pltpu.CompilerParams(dimension_semantics=("arbitrary",)),
    )(route_t, counts)


def _sc_mesh():
    return plsc.VectorSubcoreMesh(core_axis_name="core", subcore_axis_name="subcore")


def _sc_worker_id():
    return lax.axis_index("core") * (SC_WORKERS // 2) + lax.axis_index("subcore")


def _dispatch_rows(xp, pos1, pos2, pad_pos, zero_rows, n_rows):
    n_tok, width = xp.shape
    n_win = n_tok // SC_WORKERS // SC_WINDOW
    n_pad = pad_pos.shape[0] // SC_WORKERS // SC_WINDOW
    as_windows = lambda v: v.reshape(-1, SC_WINDOW)

    @functools.partial(
        pl.kernel, out_type=jax.ShapeDtypeStruct((n_rows + ROW_TILE, width), xp.dtype), mesh=_sc_mesh(),
        scratch_types=[pltpu.VMEM((n_win, SC_WINDOW), jnp.int32), pltpu.VMEM((n_win, SC_WINDOW), jnp.int32),
                       pltpu.VMEM((n_pad, SC_WINDOW), jnp.int32),
                       pltpu.VMEM((2, SC_WINDOW, width), xp.dtype),
                       pltpu.SemaphoreType.DMA((2,)), pltpu.SemaphoreType.DMA((2,))])
    def dispatch(xp_hbm, pos1_hbm, pos2_hbm, pad_hbm, zero_hbm, out_hbm, idx1, idx2, idxp, buf, lsem, ssem):
        wid = _sc_worker_id()
        pltpu.sync_copy(pos1_hbm.at[pl.ds(wid * n_win, n_win)], idx1)
        pltpu.sync_copy(pos2_hbm.at[pl.ds(wid * n_win, n_win)], idx2)
        pltpu.sync_copy(pad_hbm.at[pl.ds(wid * n_pad, n_pad)], idxp)

        def load(j):
            rows = xp_hbm.at[pl.ds((wid * n_win + j) * SC_WINDOW, SC_WINDOW)]
            return pltpu.make_async_copy(rows, buf.at[j % 2], lsem.at[j % 2])

        def scatters(j):
            return [pltpu.make_async_copy(buf.at[j % 2], out_hbm.at[idx.at[j]], ssem.at[j % 2])
                    for idx in (idx1, idx2)]

        load(0).start()
        for j in range(n_win):
            load(j).wait()
            for cp in scatters(j):
                cp.start()
            if j >= 1:
                for cp in scatters(j - 1):
                    cp.wait()
            if j + 1 < n_win:
                load(j + 1).start()
        for cp in scatters(n_win - 1):
            cp.wait()

        pltpu.sync_copy(zero_hbm, buf.at[0])
        pads = [pltpu.make_async_copy(buf.at[0], out_hbm.at[idxp.at[j]], ssem.at[0]) for j in range(n_pad)]
        for cp in pads:
            cp.start()
        for cp in pads:
            cp.wait()

    return dispatch(xp, as_windows(pos1), as_windows(pos2), as_windows(pad_pos), zero_rows)


def _gather_rows(src, idx):
    n_out, width = idx.shape[0], src.shape[1]
    n_win = n_out // SC_WORKERS // SC_WINDOW

    @functools.partial(
        pl.kernel, out_type=jax.ShapeDtypeStruct((n_out, width), src.dtype), mesh=_sc_mesh(),
        scratch_types=[pltpu.VMEM((n_win, SC_WINDOW), jnp.int32), pltpu.VMEM((2, SC_WINDOW, width), src.dtype),
                       pltpu.SemaphoreType.DMA((2,)), pltpu.SemaphoreType.DMA((2,))])
    def gather(src_hbm, idx_hbm, dst_hbm, idx_v, buf, gsem, ssem):
        wid = _sc_worker_id()
        pltpu.sync_copy(idx_hbm.at[pl.ds(wid * n_win, n_win)], idx_v)

        def fetch(j):
            return pltpu.make_async_copy(src_hbm.at[idx_v.at[j]], buf.at[j % 2], gsem.at[j % 2])

        def store(j):
            rows = dst_hbm.at[pl.ds((wid * n_win + j) * SC_WINDOW, SC_WINDOW)]
            return pltpu.make_async_copy(buf.at[j % 2], rows, ssem.at[j % 2])

        fetch(0).start()
        for j in range(n_win):
            fetch(j).wait()
            store(j).start()
            if j >= 1:
                store(j - 1).wait()
            if j + 1 < n_win:
                fetch(j + 1).start()
        store(n_win - 1).wait()

    return gather(src, idx.reshape(-1, SC_WINDOW))


def _tile_copy(hbm, buf, sem, tile, slot, to_hbm):
    rows = hbm.at[pl.ds(pl.multiple_of(tile * ROW_TILE, ROW_TILE), ROW_TILE)]
    if to_hbm:
        return pltpu.make_async_copy(buf.at[slot], rows, sem.at[slot])
    return pltpu.make_async_copy(rows, buf.at[slot], sem.at[slot])


def _expert_kernel(ts_ref, nte_ref, x_hbm, wg_ref, wu_ref, wd_ref, y_hbm,
                   xbuf, ybuf, xsem, ysem, wgu_bf, wd_bf, *, max_tiles):
    e = pl.program_id(0)
    first = ts_ref[e]
    count = nte_ref[e]
    n_tiles = ts_ref[N_EXPERTS - 1] + nte_ref[N_EXPERTS - 1]
    ahead = TILE_BUFFERS - 1

    @pl.when(e == 0)
    def _():
        for g in range(ahead):
            @pl.when(g < n_tiles)
            def _():
                _tile_copy(x_hbm, xbuf, xsem, g, g, False).start()

    @pl.when(count > 0)
    def _():
        wgu_bf[:, 0:D_EXPERT] = wg_ref[0].astype(jnp.bfloat16)
        wgu_bf[:, D_EXPERT:] = wu_ref[0].astype(jnp.bfloat16)
        wd_bf[...] = wd_ref[0].astype(jnp.bfloat16)

    def tile_body(g, c):
        slot = g % TILE_BUFFERS
        _tile_copy(x_hbm, xbuf, xsem, g, slot, False).wait()

        @pl.when(g + ahead < n_tiles)
        def _():
            _tile_copy(x_hbm, xbuf, xsem, g + ahead, (g + ahead) % TILE_BUFFERS, False).start()

        lo, hi = _unpack_bf16_pairs(xbuf[slot])
        hgu = (jnp.dot(lo.astype(jnp.bfloat16), wgu_bf[0:HALF, :], preferred_element_type=jnp.float32)
               + jnp.dot(hi.astype(jnp.bfloat16), wgu_bf[HALF:, :], preferred_element_type=jnp.float32))
        hg = hgu[:, 0:D_EXPERT]
        act = hg * jax.nn.sigmoid(hg) * hgu[:, D_EXPERT:]
        y = jnp.dot(act.astype(jnp.bfloat16), wd_bf[...], preferred_element_type=jnp.float32)

        @pl.when(g >= TILE_BUFFERS)
        def _():
            _tile_copy(y_hbm, ybuf, ysem, g - TILE_BUFFERS, slot, True).wait()

        ybuf[slot] = _pack_bf16_pairs(y)
        _tile_copy(y_hbm, ybuf, ysem, g, slot, True).start()
        return c

    lax.fori_loop(first, first + count, tile_body, 0)

    @pl.when(e == N_EXPERTS - 1)
    def _():
        for k in range(TILE_BUFFERS, 0, -1):
            @pl.when(n_tiles >= k)
            def _():
                _tile_copy(y_hbm, ybuf, ysem, n_tiles - k, (n_tiles - k) % TILE_BUFFERS, True).wait()

        ybuf[0] = jnp.zeros((ROW_TILE, HALF), jnp.uint32)

        def fill(g, c):
            cp = _tile_copy(y_hbm, ybuf, ysem, g, 0, True)
            cp.start()
            cp.wait()
            return c

        lax.fori_loop(n_tiles, max_tiles, fill, 0)


def _run_experts(tile_start, tiles_per_expert, x_sorted, w_gate, w_up, w_down):
    n_rows = x_sorted.shape[0]
    max_tiles = n_rows // ROW_TILE
    w_map = lambda e, ts, nte: (e, 0, 0)

    grid_spec = pltpu.PrefetchScalarGridSpec(
        num_scalar_prefetch=2,
        grid=(N_EXPERTS,),
        in_specs=[
            pl.BlockSpec(memory_space=pl.ANY),
            pl.BlockSpec((1, D_MODEL, D_EXPERT), w_map),
            pl.BlockSpec((1, D_MODEL, D_EXPERT), w_map),
            pl.BlockSpec((1, D_EXPERT, D_MODEL), w_map),
        ],
        out_specs=pl.BlockSpec(memory_space=pl.ANY),
        scratch_shapes=[
            pltpu.VMEM((TILE_BUFFERS, ROW_TILE, HALF), jnp.uint32),
            pltpu.VMEM((TILE_BUFFERS, ROW_TILE, HALF), jnp.uint32),
            pltpu.SemaphoreType.DMA((TILE_BUFFERS,)),
            pltpu.SemaphoreType.DMA((TILE_BUFFERS,)),
            pltpu.VMEM((D_MODEL, 2 * D_EXPERT), jnp.bfloat16),
            pltpu.VMEM((D_EXPERT, D_MODEL), jnp.bfloat16),
        ],
    )
    return pl.pallas_call(
        functools.partial(_expert_kernel, max_tiles=max_tiles),
        grid_spec=grid_spec,
        out_shape=jax.ShapeDtypeStruct((n_rows, HALF), jnp.uint32),
        compiler_params=pltpu.CompilerParams(
            dimension_semantics=("arbitrary",), vmem_limit_bytes=VMEM_LIMIT),
    )(tile_start, tiles_per_expert, x_sorted, w_gate, w_up, w_down)


def _combine_kernel(r_ref, route_ref, ya_ref, yb_ref, g_ref, b_ref, *rest):
    o_ref = rest[-1]
    route = route_ref[...]
    w1 = route[:, R_W1:R_W1 + 1]
    w2 = route[:, R_W2:R_W2 + 1]
    a_lo, a_hi = _unpack_bf16_pairs(ya_ref[...])
    b_lo, b_hi = _unpack_bf16_pairs(yb_ref[...])
    moe = jnp.concatenate([w1 * a_lo + w2 * b_lo, w1 * a_hi + w2 * b_hi], axis=-1)
    o_ref[...] = _layernorm(r_ref[...] + moe, g_ref[...], b_ref[...])


def _run_combine(r, route, y_tok, ln2_g, ln2_b, out_prev, chunk, n_tok_total):
    n_tok = r.shape[0]
    n_t = n_tok // TOKEN_TILE
    in_specs = [
        pl.BlockSpec((TOKEN_TILE, D_MODEL), lambda i: (i, 0)),
        pl.BlockSpec((TOKEN_TILE, LANES), lambda i: (i, 0)),
        pl.BlockSpec((TOKEN_TILE, HALF), lambda i: (i, 0)),
        pl.BlockSpec((TOKEN_TILE, HALF), lambda i: (i + n_t, 0)),
        pl.BlockSpec((1, D_MODEL), lambda i: (0, 0)),
        pl.BlockSpec((1, D_MODEL), lambda i: (0, 0)),
    ]
    args = [r, route, y_tok, y_tok, ln2_g, ln2_b]
    aliases = {}
    if out_prev is not None:
        in_specs.append(pl.BlockSpec(memory_space=pl.ANY))
        args.append(out_prev)
        aliases = {len(args) - 1: 0}
    return pl.pallas_call(
        _combine_kernel,
        grid=(n_t,),
        in_specs=in_specs,
        out_specs=pl.BlockSpec((TOKEN_TILE, D_MODEL), lambda i: (chunk * n_t + i, 0)),
        out_shape=jax.ShapeDtypeStruct((n_tok_total, D_MODEL), jnp.float32),
        input_output_aliases=aliases,
        compiler_params=pltpu.CompilerParams(dimension_semantics=("arbitrary",)),
    )(*args)


def _split_bf16(w):
    hi = w.astype(jnp.bfloat16)
    lo = (w - hi.astype(jnp.float32)).astype(jnp.bfloat16)
    return hi, lo


def kernel(x, p, ln_in_g, ln_in_b, w_in, conv_w, conv_b, pool_w, pool_scale, w_out, ln1_g, ln1_b,
           w_rg, b_rg, w_re, b_re, w_gate, w_up, w_down, w_pg, b_pg, w_ple, ln2_g, ln2_b):
    batch, seq, _ = x.shape
    n_tok = batch * seq
    bf = jnp.bfloat16
    row = lambda v: v.reshape(1, -1)

    w_r = jnp.concatenate([w_rg[0], jnp.transpose(w_re[0], (1, 0, 2)).reshape(D_MODEL, N_EXPERTS)], axis=1)
    w_r = jnp.pad(w_r, ((0, 0), (0, LANES - w_r.shape[1])))
    w_r_hi, w_r_lo = _split_bf16(w_r)
    w_r_cat = jnp.concatenate([w_r_hi, w_r_hi, w_r_lo], axis=0)
    b_r = jnp.pad(jnp.concatenate([b_rg[0], b_re[0].reshape(-1)]), (0, LANES - N_GROUPS - N_EXPERTS)).reshape(1, LANES)

    x2 = x.reshape(n_tok, D_MODEL)
    p2 = p[0].reshape(n_tok, PLE_DIM)
    mixer_weights = (row(ln_in_g), row(ln_in_b), w_in[0].astype(bf), conv_w[0], row(conv_b[0]),
                     pool_w[0].astype(bf), row(pool_scale[0]), w_out[0].astype(bf), row(ln1_g[0]), row(ln1_b[0]),
                     w_r_cat, b_r, w_pg[0].astype(bf), row(b_pg[0]), w_ple[0].astype(bf))
    expert_weights = (w_gate[0].reshape(N_EXPERTS, D_MODEL, D_EXPERT),
                      w_up[0].reshape(N_EXPERTS, D_MODEL, D_EXPERT),
                      w_down[0].reshape(N_EXPERTS, D_EXPERT, D_MODEL))
    zero_rows = jnp.zeros((SC_WINDOW, HALF), jnp.uint32)

    n_chunks = N_CHUNKS if batch % N_CHUNKS == 0 else 1
    chunk_batch = batch // n_chunks
    chunk_tok = chunk_batch * seq
    n_rows = -(-(2 * chunk_tok + N_EXPERTS * (ROW_TILE - 1)) // ROW_TILE) * ROW_TILE
    out = None
    for c in range(n_chunks):
        xp, r, route, route_t, counts = _run_mixer(x2, p2, *mixer_weights, c * chunk_batch, chunk_batch, seq)
        pos, tile_start, tiles_per_expert, pad_pos = _run_plan(route_t, counts, n_rows)
        x_sorted = _dispatch_rows(xp, pos[0], pos[1], pad_pos.reshape(-1), zero_rows, n_rows)
        y_sorted = _run_experts(tile_start[0, :N_EXPERTS], tiles_per_expert[0, :N_EXPERTS], x_sorted,
                                *expert_weights)
        y_tok = _gather_rows(y_sorted, pos.reshape(-1))
        out = _run_combine(r, route, y_tok, row(ln2_g[0]), row(ln2_b[0]), out, c, n_tok)
    return out.reshape(batch, seq, D_MODEL)
```

```python
import functools

import jax
import jax.numpy as jnp
from jax import lax
from jax.experimental import pallas as pl
from jax.experimental.pallas import tpu as pltpu
from jax.experimental.pallas import tpu_sc as plsc

D_MODEL = 1024
CONV_WIDTH = 512
CONV_K = 3
POOL_WIDTH = 512
POOL_WINDOWS = (2, 4, 8, 16)
POOL_GW = 128
IN_PROJ = 3 * CONV_WIDTH + POOL_WIDTH
N_GROUPS = 4
EXPERTS_PER_GROUP = 8
N_EXPERTS = N_GROUPS * EXPERTS_PER_GROUP
D_EXPERT = 256
PLE_DIM = 256
LN_EPS = 1e-5
DEEPNORM_ALPHA = 2.0 ** 0.25

LANES = 128
HALF = D_MODEL // 2
CONV_HALO = 8
POOL_HALO = 16
SEQ_TILE = 1024
SUB_TILE = 256
MIXER_PHASES = 7
ROW_TILE = 256
TILE_BUFFERS = 4
N_CHUNKS = 1
SC_WORKERS = 32
SC_WINDOW = 64
TOKEN_TILE = 512
VMEM_LIMIT = 56 * 1024 * 1024

R_ID1, R_ID2, R_RANK1, R_RANK2, R_W1, R_W2 = range(6)
ROUTE_ROWS = 8


def _layernorm(x, g, b):
    mu = jnp.mean(x, axis=-1, keepdims=True)
    xc = x - mu
    var = jnp.mean(xc * xc, axis=-1, keepdims=True)
    return xc * lax.rsqrt(var + LN_EPS) * g + b


def _pack_bf16_pairs(v):
    bits = lax.bitcast_convert_type(v.astype(jnp.bfloat16).astype(jnp.float32), jnp.uint32)
    return bits[:, HALF:] | (bits[:, :HALF] >> 16)


def _unpack_bf16_pairs(w):
    lo = lax.bitcast_convert_type(w << 16, jnp.float32)
    hi = lax.bitcast_convert_type(w & jnp.uint32(0xFFFF0000), jnp.float32)
    return lo, hi


def _mixer_kernel(x_ref, p_ref, lnin_g, lnin_b, w_in, conv_w, conv_b, pool_w, pool_scale, w_out,
                  ln1_g, ln1_b, w_r, b_r, w_pg, b_pg, w_ple,
                  xp_ref, r_ref, route_ref, route_t_ref, counts_ref,
                  zbuf, vbuf, carry):
    b = pl.program_id(0)
    s = pl.program_id(1)
    ts = x_ref.shape[0]

    @pl.when(s == 0)
    def _():
        zbuf[0:CONV_HALO, :] = jnp.zeros((CONV_HALO, CONV_WIDTH), jnp.float32)
        vbuf[0:POOL_HALO, :] = jnp.zeros((POOL_HALO, POOL_WIDTH), jnp.float32)

    @pl.when((b == 0) & (s == 0))
    def _():
        carry[...] = jnp.zeros_like(carry)

    chains = [_mixer_rows(r0, s * ts + r0, x_ref, p_ref, lnin_g, lnin_b, w_in, conv_w, conv_b, pool_w, pool_scale,
                          w_out, ln1_g, ln1_b, w_r, b_r, w_pg, b_pg, w_ple,
                          xp_ref, r_ref, route_ref, route_t_ref, zbuf, vbuf, carry)
              for r0 in range(0, ts, SUB_TILE)]
    for t in range(MIXER_PHASES + len(chains) - 1):
        for k, chain in enumerate(chains):
            if 0 <= t - k < MIXER_PHASES:
                next(chain, None)
    zbuf[0:CONV_HALO, :] = zbuf[ts:ts + CONV_HALO, :]
    vbuf[0:POOL_HALO, :] = vbuf[ts:ts + POOL_HALO, :]
    counts_ref[...] = carry[...]


def _mixer_rows(r0, seq0, x_ref, p_ref, lnin_g, lnin_b, w_in, conv_w, conv_b, pool_w, pool_scale, w_out,
                ln1_g, ln1_b, w_r, b_r, w_pg, b_pg, w_ple,
                xp_ref, r_ref, route_ref, route_t_ref, zbuf, vbuf, carry):
    n = SUB_TILE
    rows = pl.ds(r0, n)
    h0 = _layernorm(x_ref[rows, :], lnin_g[...], lnin_b[...])
    h0b = h0.astype(jnp.bfloat16)
    yield
    u = jnp.dot(h0b, w_in[...], preferred_element_type=jnp.float32)
    yield
    b_g = u[:, 0:CONV_WIDTH]
    c_g = u[:, CONV_WIDTH:2 * CONV_WIDTH]
    v_c = u[:, 2 * CONV_WIDTH:3 * CONV_WIDTH]
    v_p = u[:, 3 * CONV_WIDTH:]

    zbuf[pl.ds(CONV_HALO + r0, n), :] = c_g * v_c
    zext = zbuf[pl.ds(r0, n + CONV_HALO), :]
    z1 = pltpu.roll(zext, 1, axis=0)[CONV_HALO:, :]
    z2 = pltpu.roll(zext, 2, axis=0)[CONV_HALO:, :]
    conv = z2 * conv_w[0:1, :] + z1 * conv_w[1:2, :] + zext[CONV_HALO:, :] * conv_w[2:3, :] + conv_b[...]
    y_conv = b_g * conv

    vbuf[pl.ds(POOL_HALO + r0, n), :] = v_p
    vext = vbuf[pl.ds(r0, n + POOL_HALO), :]
    s2 = vext + pltpu.roll(vext, 1, axis=0)
    s4 = s2[:, POOL_GW:] + pltpu.roll(s2[:, POOL_GW:], 2, axis=0)
    s8 = s4[:, POOL_GW:] + pltpu.roll(s4[:, POOL_GW:], 4, axis=0)
    s16 = s8[:, POOL_GW:] + pltpu.roll(s8[:, POOL_GW:], 8, axis=0)
    wsums = (s2[POOL_HALO:, 0:POOL_GW], s4[POOL_HALO:, 0:POOL_GW],
             s8[POOL_HALO:, 0:POOL_GW], s16[POOL_HALO:, 0:POOL_GW])

    t_pos = seq0 + lax.broadcasted_iota(jnp.int32, (n, 1), 0)
    y_pool = []
    for j, w in enumerate(POOL_WINDOWS):
        inv_cnt = 1.0 / jnp.minimum(t_pos + 1, w).astype(jnp.float32)
        pooled = wsums[j] * inv_cnt - v_p[:, j * POOL_GW:(j + 1) * POOL_GW]
        y_pool.append(jnp.dot(pooled.astype(jnp.bfloat16), pool_w[j], preferred_element_type=jnp.float32))
    y_pool = jnp.concatenate(y_pool, axis=-1) * pool_scale[...]

    ycat = jnp.concatenate([y_conv, y_pool], axis=-1).astype(jnp.bfloat16)
    yield
    mix = jnp.dot(ycat, w_out[...], preferred_element_type=jnp.float32)
    yield
    h1 = _layernorm(DEEPNORM_ALPHA * h0 + mix, ln1_g[...], ln1_b[...])

    h_hi = h1.astype(jnp.bfloat16)
    h_lo = (h1 - h_hi.astype(jnp.float32)).astype(jnp.bfloat16)
    xp_ref[rows, :] = _pack_bf16_pairs(h1)
    p_b = p_ref[rows, :].astype(jnp.bfloat16)
    hcat = jnp.concatenate([h_hi, h_lo, h_hi], axis=-1)
    yield
    gate_pre = jnp.dot(h_hi, w_pg[...], preferred_element_type=jnp.float32)
    ple_pre = jnp.dot(p_b, w_ple[...], preferred_element_type=jnp.float32)
    logits = jnp.dot(hcat, w_r[...], preferred_element_type=jnp.float32) + b_r[...]
    yield

    r_ref[rows, :] = DEEPNORM_ALPHA * h1 + ple_pre * jax.nn.sigmoid(gate_pre + b_pg[...])
    lane = lax.broadcasted_iota(jnp.int32, (n, LANES), 1)
    neg = jnp.float32(-jnp.inf)

    def first_argmax(vals):
        m = jnp.max(vals, axis=-1, keepdims=True)
        idx = jnp.min(jnp.where(vals == m, lane, LANES), axis=-1, keepdims=True)
        return m, idx

    g_mask = lane < N_GROUPS
    g_max, g_idx = first_argmax(jnp.where(g_mask, logits, neg))
    g_w = 1.0 / jnp.sum(jnp.where(g_mask, jnp.exp(logits - g_max), 0.0), axis=-1, keepdims=True)

    e_lo = N_GROUPS + EXPERTS_PER_GROUP * g_idx
    e_vals = jnp.where((lane >= e_lo) & (lane < e_lo + EXPERTS_PER_GROUP), logits, neg)
    m1, i1 = first_argmax(e_vals)
    m2, i2 = first_argmax(jnp.where(lane == i1, neg, e_vals))
    e21 = jnp.exp(m2 - m1)
    w1 = g_w / (1.0 + e21)
    w2 = g_w * e21 / (1.0 + e21)
    id1 = i1 - N_GROUPS
    id2 = i2 - N_GROUPS

    sel1 = lane == id1
    sel2 = lane == id2
    onehot = (sel1 | sel2).astype(jnp.float32)
    row = lax.broadcasted_iota(jnp.int32, (n, n), 0)
    col = lax.broadcasted_iota(jnp.int32, (n, n), 1)
    tri = (col < row).astype(jnp.bfloat16)
    before = jnp.dot(tri, onehot.astype(jnp.bfloat16), preferred_element_type=jnp.float32) + carry[...]
    rank1 = jnp.sum(jnp.where(sel1, before, 0.0), axis=-1, keepdims=True)
    rank2 = jnp.sum(jnp.where(sel2, before, 0.0), axis=-1, keepdims=True)
    carry[...] = carry[...] + jnp.sum(onehot, axis=0, keepdims=True)

    rec = jnp.zeros((n, LANES), jnp.float32)
    for k, val in ((R_ID1, id1.astype(jnp.float32)), (R_ID2, id2.astype(jnp.float32)), (R_W1, w1), (R_W2, w2),
                   (R_RANK1, rank1), (R_RANK2, rank2)):
        rec = jnp.where(lane == k, val, rec)
    route_ref[rows, :] = rec
    route_t_ref[:, rows] = jnp.transpose(rec)[0:ROUTE_ROWS, :]


def _run_mixer(x2, p2, lnin_g, lnin_b, w_in, conv_w, conv_b, pool_w, pool_scale, w_out, ln1_g, ln1_b,
               w_r, b_r, w_pg, b_pg, w_ple, batch_lo, batch, seq):
    n_tok = batch * seq
    n_s = seq // SEQ_TILE
    tok_map = lambda b, s: (b * n_s + s, 0)
    in_map = lambda b, s: ((batch_lo + b) * n_s + s, 0)

    def const(shape):
        return pl.BlockSpec(shape, lambda b, s: (0,) * len(shape), pipeline_mode=pl.Buffered(1))

    in_specs = [
        pl.BlockSpec((SEQ_TILE, D_MODEL), in_map),
        pl.BlockSpec((SEQ_TILE, PLE_DIM), in_map),
        const((1, D_MODEL)), const((1, D_MODEL)),
        const((D_MODEL, IN_PROJ)),
        const((CONV_K, CONV_WIDTH)), const((1, CONV_WIDTH)),
        const((len(POOL_WINDOWS), POOL_GW, POOL_GW)), const((1, POOL_WIDTH)),
        const((D_MODEL, D_MODEL)),
        const((1, D_MODEL)), const((1, D_MODEL)),
        const((3 * D_MODEL, LANES)), const((1, LANES)),
        const((D_MODEL, D_MODEL)), const((1, D_MODEL)),
        const((PLE_DIM, D_MODEL)),
    ]
    out_specs = [
        pl.BlockSpec((SEQ_TILE, HALF), tok_map),
        pl.BlockSpec((SEQ_TILE, D_MODEL), tok_map),
        pl.BlockSpec((SEQ_TILE, LANES), tok_map),
        pl.BlockSpec((ROUTE_ROWS, SEQ_TILE), lambda b, s: (0, b * n_s + s)),
        pl.BlockSpec((1, LANES), lambda b, s: (0, 0)),
    ]
    out_shape = [
        jax.ShapeDtypeStruct((n_tok, HALF), jnp.uint32),
        jax.ShapeDtypeStruct((n_tok, D_MODEL), jnp.float32),
        jax.ShapeDtypeStruct((n_tok, LANES), jnp.float32),
        jax.ShapeDtypeStruct((ROUTE_ROWS, n_tok), jnp.float32),
        jax.ShapeDtypeStruct((1, LANES), jnp.float32),
    ]
    return pl.pallas_call(
        _mixer_kernel,
        grid=(batch, n_s),
        in_specs=in_specs,
        out_specs=out_specs,
        out_shape=out_shape,
        scratch_shapes=[
            pltpu.VMEM((SEQ_TILE + CONV_HALO, CONV_WIDTH), jnp.float32),
            pltpu.VMEM((SEQ_TILE + POOL_HALO, POOL_WIDTH), jnp.float32),
            pltpu.VMEM((1, LANES), jnp.float32),
        ],
        compiler_params=pltpu.CompilerParams(
            dimension_semantics=("arbitrary", "arbitrary"), vmem_limit_bytes=VMEM_LIMIT),
    )(x2, p2, lnin_g, lnin_b, w_in, conv_w, conv_b, pool_w, pool_scale, w_out, ln1_g, ln1_b,
      w_r, b_r, w_pg, b_pg, w_ple)


def _plan_kernel(rt_ref, counts_ref, pos_ref, tile_start_ref, tiles_ref, pad_ref, *, n_rows):
    lane = lax.broadcasted_iota(jnp.int32, (ROUTE_ROWS, LANES), 1)
    counts = jnp.broadcast_to(counts_ref[...], (ROUTE_ROWS, LANES))
    tiles = jnp.floor((counts + (ROW_TILE - 1)) * (1.0 / ROW_TILE))
    tile_end = tiles
    shift = 1
    while shift < N_EXPERTS:
        tile_end = tile_end + jnp.where(lane >= shift, pltpu.roll(tile_end, shift, axis=1), 0.0)
        shift *= 2
    row_start = (tile_end - tiles) * ROW_TILE
    tile_start_ref[...] = (tile_end - tiles)[0:1, :].astype(jnp.int32)
    tiles_ref[...] = tiles[0:1, :].astype(jnp.int32)

    rt = rt_ref[...]
    ids = rt[R_ID1:R_ID2 + 1, :]
    start = jnp.zeros_like(ids)
    for e in range(N_EXPERTS):
        start = jnp.where(ids == e, row_start[0:1, e:e + 1], start)
    pos_ref[...] = (start + rt[R_RANK1:R_RANK2 + 1, :]).astype(jnp.int32)

    sub = lax.broadcasted_iota(jnp.int32, (N_EXPERTS, LANES), 0)
    lane_e = lax.broadcasted_iota(jnp.int32, (N_EXPERTS, LANES), 1)
    diag = sub == lane_e
    pad_lo = jnp.sum(jnp.where(diag, (row_start + counts)[0:1, :], 0.0), axis=1, keepdims=True)
    pad_n = jnp.sum(jnp.where(diag, (tiles * ROW_TILE - counts)[0:1, :], 0.0), axis=1, keepdims=True)
    j = lax.broadcasted_iota(jnp.int32, (N_EXPERTS, ROW_TILE), 1).astype(jnp.float32)
    pad_ref[...] = jnp.where(j < pad_n, pad_lo + j, n_rows + j).astype(jnp.int32)


def _run_plan(route_t, counts, n_rows):
    n_tok = route_t.shape[1]
    full = lambda shape: pl.BlockSpec(shape, lambda i: (0, 0))
    return pl.pallas_call(
        functools.partial(_plan_kernel, n_rows=n_rows),
        grid=(1,),
        in_specs=[full((ROUTE_ROWS, n_tok)), full((1, LANES))],
        out_specs=[full((2, n_tok)), full((1, LANES)), full((1, LANES)), full((N_EXPERTS, ROW_TILE))],
        out_shape=[
            jax.ShapeDtypeStruct((2, n_tok), jnp.int32),
            jax.ShapeDtypeStruct((1, LANES), jnp.int32),
            jax.ShapeDtypeStruct((1, LANES), jnp.int32),
            jax.ShapeDtypeStruct((N_EXPERTS, ROW_TILE), jnp.int32),
        ],
        compiler_params=pltpu.CompilerParams(dimension_semantics=("arbitrary",)),
    )(route_t, counts)


def _sc_mesh():
    return plsc.VectorSubcoreMesh(core_axis_name="core", subcore_axis_name="subcore")


def _sc_worker_id():
    return lax.axis_index("core") * (SC_WORKERS // 2) + lax.axis_index("subcore")


def _dispatch_rows(xp, pos1, pos2, pad_pos, zero_rows, n_rows):
    n_tok, width = xp.shape
    n_win = n_tok // SC_WORKERS // SC_WINDOW
    n_pad = pad_pos.shape[0] // SC_WORKERS // SC_WINDOW
    as_windows = lambda v: v.reshape(-1, SC_WINDOW)

    @functools.partial(
        pl.kernel, out_type=jax.ShapeDtypeStruct((n_rows + ROW_TILE, width), xp.dtype), mesh=_sc_mesh(),
        scratch_types=[pltpu.VMEM((n_win, SC_WINDOW), jnp.int32), pltpu.VMEM((n_win, SC_WINDOW), jnp.int32),
                       pltpu.VMEM((n_pad, SC_WINDOW), jnp.int32),
                       pltpu.VMEM((2, SC_WINDOW, width), xp.dtype),
                       pltpu.SemaphoreType.DMA((2,)), pltpu.SemaphoreType.DMA((2,))])
    def dispatch(xp_hbm, pos1_hbm, pos2_hbm, pad_hbm, zero_hbm, out_hbm, idx1, idx2, idxp, buf, lsem, ssem):
        wid = _sc_worker_id()
        pltpu.sync_copy(pos1_hbm.at[pl.ds(wid * n_win, n_win)], idx1)
        pltpu.sync_copy(pos2_hbm.at[pl.ds(wid * n_win, n_win)], idx2)
        pltpu.sync_copy(pad_hbm.at[pl.ds(wid * n_pad, n_pad)], idxp)

        def load(j):
            rows = xp_hbm.at[pl.ds((wid * n_win + j) * SC_WINDOW, SC_WINDOW)]
            return pltpu.make_async_copy(rows, buf.at[j % 2], lsem.at[j % 2])

        def scatters(j):
            return [pltpu.make_async_copy(buf.at[j % 2], out_hbm.at[idx.at[j]], ssem.at[j % 2])
                    for idx in (idx1, idx2)]

        load(0).start()
        for j in range(n_win):
            load(j).wait()
            for cp in scatters(j):
                cp.start()
            if j >= 1:
                for cp in scatters(j - 1):
                    cp.wait()
            if j + 1 < n_win:
                load(j + 1).start()
        for cp in scatters(n_win - 1):
            cp.wait()

        pltpu.sync_copy(zero_hbm, buf.at[0])
        pads = [pltpu.make_async_copy(buf.at[0], out_hbm.at[idxp.at[j]], ssem.at[0]) for j in range(n_pad)]
        for cp in pads:
            cp.start()
        for cp in pads:
            cp.wait()

    return dispatch(xp, as_windows(pos1), as_windows(pos2), as_windows(pad_pos), zero_rows)


def _gather_rows(src, idx):
    n_out, width = idx.shape[0], src.shape[1]
    n_win = n_out // SC_WORKERS // SC_WINDOW

    @functools.partial(
        pl.kernel, out_type=jax.ShapeDtypeStruct((n_out, width), src.dtype), mesh=_sc_mesh(),
        scratch_types=[pltpu.VMEM((n_win, SC_WINDOW), jnp.int32), pltpu.VMEM((2, SC_WINDOW, width), src.dtype),
                       pltpu.SemaphoreType.DMA((2,)), pltpu.SemaphoreType.DMA((2,))])
    def gather(src_hbm, idx_hbm, dst_hbm, idx_v, buf, gsem, ssem):
        wid = _sc_worker_id()
        pltpu.sync_copy(idx_hbm.at[pl.ds(wid * n_win, n_win)], idx_v)

        def fetch(j):
            return pltpu.make_async_copy(src_hbm.at[idx_v.at[j]], buf.at[j % 2], gsem.at[j % 2])

        def store(j):
            rows = dst_hbm.at[pl.ds((wid * n_win + j) * SC_WINDOW, SC_WINDOW)]
            return pltpu.make_async_copy(buf.at[j % 2], rows, ssem.at[j % 2])

        fetch(0).start()
        for j in range(n_win):
            fetch(j).wait()
            store(j).start()
            if j >= 1:
                store(j - 1).wait()
            if j + 1 < n_win:
                fetch(j + 1).start()
        store(n_win - 1).wait()

    return gather(src, idx.reshape(-1, SC_WINDOW))


def _tile_copy(hbm, buf, sem, tile, slot, to_hbm):
    rows = hbm.at[pl.ds(pl.multiple_of(tile * ROW_TILE, ROW_TILE), ROW_TILE)]
    if to_hbm:
        return pltpu.make_async_copy(buf.at[slot], rows, sem.at[slot])
    return pltpu.make_async_copy(rows, buf.at[slot], sem.at[slot])


def _expert_kernel(ts_ref, nte_ref, x_hbm, wg_ref, wu_ref, wd_ref, y_hbm,
                   xbuf, ybuf, xsem, ysem, wgu_bf, wd_bf, *, max_tiles):
    e = pl.program_id(0)
    first = ts_ref[e]
    count = nte_ref[e]
    n_tiles = ts_ref[N_EXPERTS - 1] + nte_ref[N_EXPERTS - 1]
    ahead = TILE_BUFFERS - 1

    @pl.when(e == 0)
    def _():
        for g in range(ahead):
            @pl.when(g < n_tiles)
            def _():
                _tile_copy(x_hbm, xbuf, xsem, g, g, False).start()

    @pl.when(count > 0)
    def _():
        wgu_bf[:, 0:D_EXPERT] = wg_ref[0].astype(jnp.bfloat16)
        wgu_bf[:, D_EXPERT:] = wu_ref[0].astype(jnp.bfloat16)
        wd_bf[...] = wd_ref[0].astype(jnp.bfloat16)

    def tile_body(g, c):
        slot = g % TILE_BUFFERS
        _tile_copy(x_hbm, xbuf, xsem, g, slot, False).wait()

        @pl.when(g + ahead < n_tiles)
        def _():
            _tile_copy(x_hbm, xbuf, xsem, g + ahead, (g + ahead) % TILE_BUFFERS, False).start()

        lo, hi = _unpack_bf16_pairs(xbuf[slot])
        hgu = (jnp.dot(lo.astype(jnp.bfloat16), wgu_bf[0:HALF, :], preferred_element_type=jnp.float32)
               + jnp.dot(hi.astype(jnp.bfloat16), wgu_bf[HALF:, :], preferred_element_type=jnp.float32))
        hg = hgu[:, 0:D_EXPERT]
        act = hg * jax.nn.sigmoid(hg) * hgu[:, D_EXPERT:]
        y = jnp.dot(act.astype(jnp.bfloat16), wd_bf[...], preferred_element_type=jnp.float32)

        @pl.when(g >= TILE_BUFFERS)
        def _():
            _tile_copy(y_hbm, ybuf, ysem, g - TILE_BUFFERS, slot, True).wait()

        ybuf[slot] = _pack_bf16_pairs(y)
        _tile_copy(y_hbm, ybuf, ysem, g, slot, True).start()
        return c

    lax.fori_loop(first, first + count, tile_body, 0)

    @pl.when(e == N_EXPERTS - 1)
    def _():
        for k in range(TILE_BUFFERS, 0, -1):
            @pl.when(n_tiles >= k)
            def _():
                _tile_copy(y_hbm, ybuf, ysem, n_tiles - k, (n_tiles - k) % TILE_BUFFERS, True).wait()

        ybuf[0] = jnp.zeros((ROW_TILE, HALF), jnp.uint32)

        def fill(g, c):
            cp = _tile_copy(y_hbm, ybuf, ysem, g, 0, True)
            cp.start()
            cp.wait()
            return c

        lax.fori_loop(n_tiles, max_tiles, fill, 0)


def _run_experts(tile_start, tiles_per_expert, x_sorted, w_gate, w_up, w_down):
    n_rows = x_sorted.shape[0]
    max_tiles = n_rows // ROW_TILE
    w_map = lambda e, ts, nte: (e, 0, 0)

    grid_spec = pltpu.PrefetchScalarGridSpec(
        num_scalar_prefetch=2,
        grid=(N_EXPERTS,),
        in_specs=[
            pl.BlockSpec(memory_space=pl.ANY),
            pl.BlockSpec((1, D_MODEL, D_EXPERT), w_map),
            pl.BlockSpec((1, D_MODEL, D_EXPERT), w_map),
            pl.BlockSpec((1, D_EXPERT, D_MODEL), w_map),
        ],
        out_specs=pl.BlockSpec(memory_space=pl.ANY),
        scratch_shapes=[
            pltpu.VMEM((TILE_BUFFERS, ROW_TILE, HALF), jnp.uint32),
            pltpu.VMEM((TILE_BUFFERS, ROW_TILE, HALF), jnp.uint32),
            pltpu.SemaphoreType.DMA((TILE_BUFFERS,)),
            pltpu.SemaphoreType.DMA((TILE_BUFFERS,)),
            pltpu.VMEM((D_MODEL, 2 * D_EXPERT), jnp.bfloat16),
            pltpu.VMEM((D_EXPERT, D_MODEL), jnp.bfloat16),
        ],
    )
    return pl.pallas_call(
        functools.partial(_expert_kernel, max_tiles=max_tiles),
        grid_spec=grid_spec,
        out_shape=jax.ShapeDtypeStruct((n_rows, HALF), jnp.uint32),
        compiler_params=pltpu.CompilerParams(
            dimension_semantics=("arbitrary",), vmem_limit_bytes=VMEM_LIMIT),
    )(tile_start, tiles_per_expert, x_sorted, w_gate, w_up, w_down)


def _combine_kernel(r_ref, route_ref, ya_ref, yb_ref, g_ref, b_ref, *rest):
    o_ref = rest[-1]
    route = route_ref[...]
    w1 = route[:, R_W1:R_W1 + 1]
    w2 = route[:, R_W2:R_W2 + 1]
    a_lo, a_hi = _unpack_bf16_pairs(ya_ref[...])
    b_lo, b_hi = _unpack_bf16_pairs(yb_ref[...])
    moe = jnp.concatenate([w1 * a_lo + w2 * b_lo, w1 * a_hi + w2 * b_hi], axis=-1)
    o_ref[...] = _layernorm(r_ref[...] + moe, g_ref[...], b_ref[...])


def _run_combine(r, route, y_tok, ln2_g, ln2_b, out_prev, chunk, n_tok_total):
    n_tok = r.shape[0]
    n_t = n_tok // TOKEN_TILE
    in_specs = [
        pl.BlockSpec((TOKEN_TILE, D_MODEL), lambda i: (i, 0)),
        pl.BlockSpec((TOKEN_TILE, LANES), lambda i: (i, 0)),
        pl.BlockSpec((TOKEN_TILE, HALF), lambda i: (i, 0)),
        pl.BlockSpec((TOKEN_TILE, HALF), lambda i: (i + n_t, 0)),
        pl.BlockSpec((1, D_MODEL), lambda i: (0, 0)),
        pl.BlockSpec((1, D_MODEL), lambda i: (0, 0)),
    ]
    args = [r, route, y_tok, y_tok, ln2_g, ln2_b]
    aliases = {}
    if out_prev is not None:
        in_specs.append(pl.BlockSpec(memory_space=pl.ANY))
        args.append(out_prev)
        aliases = {len(args) - 1: 0}
    return pl.pallas_call(
        _combine_kernel,
        grid=(n_t,),
        in_specs=in_specs,
        out_specs=pl.BlockSpec((TOKEN_TILE, D_MODEL), lambda i: (chunk * n_t + i, 0)),
        out_shape=jax.ShapeDtypeStruct((n_tok_total, D_MODEL), jnp.float32),
        input_output_aliases=aliases,
        compiler_params=pltpu.CompilerParams(dimension_semantics=("arbitrary",)),
    )(*args)


def _split_bf16(w):
    hi = w.astype(jnp.bfloat16)
    lo = (w - hi.astype(jnp.float32)).astype(jnp.bfloat16)
    return hi, lo


def kernel(x, p, ln_in_g, ln_in_b, w_in, conv_w, conv_b, pool_w, pool_scale, w_out, ln1_g, ln1_b,
           w_rg, b_rg, w_re, b_re, w_gate, w_up, w_down, w_pg, b_pg, w_ple, ln2_g, ln2_b):
    batch, seq, _ = x.shape
    n_tok = batch * seq
    bf = jnp.bfloat16
    row = lambda v: v.reshape(1, -1)

    w_r = jnp.concatenate([w_rg[0], jnp.transpose(w_re[0], (1, 0, 2)).reshape(D_MODEL, N_EXPERTS)], axis=1)
    w_r = jnp.pad(w_r, ((0, 0), (0, LANES - w_r.shape[1])))
    w_r_hi, w_r_lo = _split_bf16(w_r)
    w_r_cat = jnp.concatenate([w_r_hi, w_r_hi, w_r_lo], axis=0)
    b_r = jnp.pad(jnp.concatenate([b_rg[0], b_re[0].reshape(-1)]), (0, LANES - N_GROUPS - N_EXPERTS)).reshape(1, LANES)

    x2 = x.reshape(n_tok, D_MODEL)
    p2 = p[0].reshape(n_tok, PLE_DIM)
    mixer_weights = (row(ln_in_g), row(ln_in_b), w_in[0].astype(bf), conv_w[0], row(conv_b[0]),
                     pool_w[0].astype(bf), row(pool_scale[0]), w_out[0].astype(bf), row(ln1_g[0]), row(ln1_b[0]),
                     w_r_cat, b_r, w_pg[0].astype(bf), row(b_pg[0]), w_ple[0].astype(bf))
    expert_weights = (w_gate[0].reshape(N_EXPERTS, D_MODEL, D_EXPERT),
                      w_up[0].reshape(N_EXPERTS, D_MODEL, D_EXPERT),
                      w_down[0].reshape(N_EXPERTS, D_EXPERT, D_MODEL))
    zero_rows = jnp.zeros((SC_WINDOW, HALF), jnp.uint32)

    n_chunks = N_CHUNKS if batch % N_CHUNKS == 0 else 1
    chunk_batch = batch // n_chunks
    chunk_tok = chunk_batch * seq
    n_rows = -(-(2 * chunk_tok + N_EXPERTS * (ROW_TILE - 1)) // ROW_TILE) * ROW_TILE
    out = None
    for c in range(n_chunks):
        xp, r, route, route_t, counts = _run_mixer(x2, p2, *mixer_weights, c * chunk_batch, chunk_batch, seq)
        pos, tile_start, tiles_per_expert, pad_pos = _run_plan(route_t, counts, n_rows)
        x_sorted = _dispatch_rows(xp, pos[0], pos[1], pad_pos.reshape(-1), zero_rows, n_rows)
        y_sorted = _run_experts(tile_start[0, :N_EXPERTS], tiles_per_expert[0, :N_EXPERTS], x_sorted,
                                *expert_weights)
        y_tok = _gather_rows(y_sorted, pos.reshape(-1))
        out = _run_combine(r, route, y_tok, row(ln2_g[0]), row(ln2_b[0]), out, c, n_tok)
    return out.reshape(batch, seq, D_MODEL)
```

```python
import functools

import jax
import jax.numpy as jnp
from jax import lax
from jax.experimental import pallas as pl
from jax.experimental.pallas import tpu as pltpu
from jax.experimental.pallas import tpu_sc as plsc

D_MODEL = 1024
CONV_WIDTH = 512
CONV_K = 3
POOL_WIDTH = 512
POOL_WINDOWS = (2, 4, 8, 16)
POOL_GW = 128
IN_PROJ = 3 * CONV_WIDTH + POOL_WIDTH
N_GROUPS = 4
EXPERTS_PER_GROUP = 8
N_EXPERTS = N_GROUPS * EXPERTS_PER_GROUP
D_EXPERT = 256
PLE_DIM = 256
LN_EPS = 1e-5
DEEPNORM_ALPHA = 2.0 ** 0.25

LANES = 128
HALF = D_MODEL // 2
CONV_HALO = 8
POOL_HALO = 16
SEQ_TILE = 1024
SUB_TILE = 256
MIXER_PHASES = 7
ROW_TILE = 256
TILE_BUFFERS = 4
N_CHUNKS = 1
SC_WORKERS = 32
SC_WINDOW = 64
WEIGHT_BUFFERS = 3
TOKEN_TILE = 1024
VMEM_LIMIT = 56 * 1024 * 1024

R_ID1, R_ID2, R_RANK1, R_RANK2, R_W1, R_W2 = range(6)
ROUTE_ROWS = 8


def _layernorm(x, g, b):
    mu = jnp.mean(x, axis=-1, keepdims=True)
    xc = x - mu
    var = jnp.mean(xc * xc, axis=-1, keepdims=True)
    return xc * lax.rsqrt(var + LN_EPS) * g + b


def _pack_bf16_pairs(v):
    bits = lax.bitcast_convert_type(v.astype(jnp.bfloat16).astype(jnp.float32), jnp.uint32)
    return bits[:, HALF:] | (bits[:, :HALF] >> 16)


def _unpack_bf16_pairs(w):
    lo = lax.bitcast_convert_type(w << 16, jnp.float32)
    hi = lax.bitcast_convert_type(w & jnp.uint32(0xFFFF0000), jnp.float32)
    return lo, hi


def _mixer_kernel(x_ref, p_ref, lnin_g, lnin_b, w_in, conv_w, conv_b, pool_w, pool_scale, w_out,
                  ln1_g, ln1_b, w_r, b_r, w_pg, b_pg, w_ple,
                  xp_ref, r_ref, route_ref, route_t_ref, counts_ref,
                  zbuf, vbuf, carry):
    b = pl.program_id(0)
    s = pl.program_id(1)
    ts = x_ref.shape[0]

    @pl.when(s == 0)
    def _():
        zbuf[0:CONV_HALO, :] = jnp.zeros((CONV_HALO, CONV_WIDTH), jnp.float32)
        vbuf[0:POOL_HALO, :] = jnp.zeros((POOL_HALO, POOL_WIDTH), jnp.float32)

    @pl.when((b == 0) & (s == 0))
    def _():
        carry[...] = jnp.zeros_like(carry)

    chains = [_mixer_rows(r0, s * ts + r0, x_ref, p_ref, lnin_g, lnin_b, w_in, conv_w, conv_b, pool_w, pool_scale,
                          w_out, ln1_g, ln1_b, w_r, b_r, w_pg, b_pg, w_ple,
                          xp_ref, r_ref, route_ref, route_t_ref, zbuf, vbuf, carry)
              for r0 in range(0, ts, SUB_TILE)]
    for t in range(MIXER_PHASES + len(chains) - 1):
        for k, chain in enumerate(chains):
            if 0 <= t - k < MIXER_PHASES:
                next(chain, None)
    zbuf[0:CONV_HALO, :] = zbuf[ts:ts + CONV_HALO, :]
    vbuf[0:POOL_HALO, :] = vbuf[ts:ts + POOL_HALO, :]
    counts_ref[...] = carry[...]


def _mixer_rows(r0, seq0, x_ref, p_ref, lnin_g, lnin_b, w_in, conv_w, conv_b, pool_w, pool_scale, w_out,
                ln1_g, ln1_b, w_r, b_r, w_pg, b_pg, w_ple,
                xp_ref, r_ref, route_ref, route_t_ref, zbuf, vbuf, carry):
    n = SUB_TILE
    rows = pl.ds(r0, n)
    h0 = _layernorm(x_ref[rows, :], lnin_g[...], lnin_b[...])
    h0b = h0.astype(jnp.bfloat16)
    yield
    u = jnp.dot(h0b, w_in[...], preferred_element_type=jnp.float32)
    yield
    b_g = u[:, 0:CONV_WIDTH]
    c_g = u[:, CONV_WIDTH:2 * CONV_WIDTH]
    v_c = u[:, 2 * CONV_WIDTH:3 * CONV_WIDTH]
    v_p = u[:, 3 * CONV_WIDTH:]

    zbuf[pl.ds(CONV_HALO + r0, n), :] = c_g * v_c
    zext = zbuf[pl.ds(r0, n + CONV_HALO), :]
    z1 = pltpu.roll(zext, 1, axis=0)[CONV_HALO:, :]
    z2 = pltpu.roll(zext, 2, axis=0)[CONV_HALO:, :]
    conv = z2 * conv_w[0:1, :] + z1 * conv_w[1:2, :] + zext[CONV_HALO:, :] * conv_w[2:3, :] + conv_b[...]
    y_conv = b_g * conv

    vbuf[pl.ds(POOL_HALO + r0, n), :] = v_p
    vext = vbuf[pl.ds(r0, n + POOL_HALO), :]
    s2 = vext + pltpu.roll(vext, 1, axis=0)
    s4 = s2[:, POOL_GW:] + pltpu.roll(s2[:, POOL_GW:], 2, axis=0)
    s8 = s4[:, POOL_GW:] + pltpu.roll(s4[:, POOL_GW:], 4, axis=0)
    s16 = s8[:, POOL_GW:] + pltpu.roll(s8[:, POOL_GW:], 8, axis=0)
    wsums = (s2[POOL_HALO:, 0:POOL_GW], s4[POOL_HALO:, 0:POOL_GW],
             s8[POOL_HALO:, 0:POOL_GW], s16[POOL_HALO:, 0:POOL_GW])

    t_pos = seq0 + lax.broadcasted_iota(jnp.int32, (n, 1), 0)
    y_pool = []
    for j, w in enumerate(POOL_WINDOWS):
        inv_cnt = 1.0 / jnp.minimum(t_pos + 1, w).astype(jnp.float32)
        pooled = wsums[j] * inv_cnt - v_p[:, j * POOL_GW:(j + 1) * POOL_GW]
        y_pool.append(jnp.dot(pooled.astype(jnp.bfloat16), pool_w[j], preferred_element_type=jnp.float32))
    y_pool = jnp.concatenate(y_pool, axis=-1) * pool_scale[...]

    ycat = jnp.concatenate([y_conv, y_pool], axis=-1).astype(jnp.bfloat16)
    yield
    mix = jnp.dot(ycat, w_out[...], preferred_element_type=jnp.float32)
    yield
    h1 = _layernorm(DEEPNORM_ALPHA * h0 + mix, ln1_g[...], ln1_b[...])

    h_hi = h1.astype(jnp.bfloat16)
    h_lo = (h1 - h_hi.astype(jnp.float32)).astype(jnp.bfloat16)
    xp_ref[rows, :] = _pack_bf16_pairs(h1)
    p_b = p_ref[rows, :].astype(jnp.bfloat16)
    hcat = jnp.concatenate([h_hi, h_lo, h_hi], axis=-1)
    yield
    gate_pre = jnp.dot(h_hi, w_pg[...], preferred_element_type=jnp.float32)
    ple_pre = jnp.dot(p_b, w_ple[...], preferred_element_type=jnp.float32)
    logits = jnp.dot(hcat, w_r[...], preferred_element_type=jnp.float32) + b_r[...]
    yield

    r_ref[rows, :] = DEEPNORM_ALPHA * h1 + ple_pre * jax.nn.sigmoid(gate_pre + b_pg[...])
    lane = lax.broadcasted_iota(jnp.int32, (n, LANES), 1)
    neg = jnp.float32(-jnp.inf)

    def first_argmax(vals):
        m = jnp.max(vals, axis=-1, keepdims=True)
        idx = jnp.min(jnp.where(vals == m, lane, LANES), axis=-1, keepdims=True)
        return m, idx

    g_mask = lane < N_GROUPS
    g_max, g_idx = first_argmax(jnp.where(g_mask, logits, neg))
    g_w = 1.0 / jnp.sum(jnp.where(g_mask, jnp.exp(logits - g_max), 0.0), axis=-1, keepdims=True)

    e_lo = N_GROUPS + EXPERTS_PER_GROUP * g_idx
    e_vals = jnp.where((lane >= e_lo) & (lane < e_lo + EXPERTS_PER_GROUP), logits, neg)
    m1, i1 = first_argmax(e_vals)
    m2, i2 = first_argmax(jnp.where(lane == i1, neg, e_vals))
    e21 = jnp.exp(m2 - m1)
    w1 = g_w / (1.0 + e21)
    w2 = g_w * e21 / (1.0 + e21)
    id1 = i1 - N_GROUPS
    id2 = i2 - N_GROUPS

    sel1 = lane == id1
    sel2 = lane == id2
    onehot = (sel1 | sel2).astype(jnp.float32)
    row = lax.broadcasted_iota(jnp.int32, (n, n), 0)
    col = lax.broadcasted_iota(jnp.int32, (n, n), 1)
    tri = (col < row).astype(jnp.bfloat16)
    before = jnp.dot(tri, onehot.astype(jnp.bfloat16), preferred_element_type=jnp.float32) + carry[...]
    rank1 = jnp.sum(jnp.where(sel1, before, 0.0), axis=-1, keepdims=True)
    rank2 = jnp.sum(jnp.where(sel2, before, 0.0), axis=-1, keepdims=True)
    carry[...] = carry[...] + jnp.sum(onehot, axis=0, keepdims=True)

    rec = jnp.zeros((n, LANES), jnp.float32)
    for k, val in ((R_ID1, id1.astype(jnp.float32)), (R_ID2, id2.astype(jnp.float32)), (R_W1, w1), (R_W2, w2),
                   (R_RANK1, rank1), (R_RANK2, rank2)):
        rec = jnp.where(lane == k, val, rec)
    route_ref[rows, :] = rec
    route_t_ref[:, rows] = jnp.transpose(rec)[0:ROUTE_ROWS, :]


def _run_mixer(x2, p2, lnin_g, lnin_b, w_in, conv_w, conv_b, pool_w, pool_scale, w_out, ln1_g, ln1_b,
               w_r, b_r, w_pg, b_pg, w_ple, batch_lo, batch, seq):
    n_tok = batch * seq
    n_s = seq // SEQ_TILE
    tok_map = lambda b, s: (b * n_s + s, 0)
    in_map = lambda b, s: ((batch_lo + b) * n_s + s, 0)

    def const(shape):
        return pl.BlockSpec(shape, lambda b, s: (0,) * len(shape), pipeline_mode=pl.Buffered(1))

    in_specs = [
        pl.BlockSpec((SEQ_TILE, D_MODEL), in_map),
        pl.BlockSpec((SEQ_TILE, PLE_DIM), in_map),
        const((1, D_MODEL)), const((1, D_MODEL)),
        const((D_MODEL, IN_PROJ)),
        const((CONV_K, CONV_WIDTH)), const((1, CONV_WIDTH)),
        const((len(POOL_WINDOWS), POOL_GW, POOL_GW)), const((1, POOL_WIDTH)),
        const((D_MODEL, D_MODEL)),
        const((1, D_MODEL)), const((1, D_MODEL)),
        const((3 * D_MODEL, LANES)), const((1, LANES)),
        const((D_MODEL, D_MODEL)), const((1, D_MODEL)),
        const((PLE_DIM, D_MODEL)),
    ]
    out_specs = [
        pl.BlockSpec((SEQ_TILE, HALF), tok_map),
        pl.BlockSpec((SEQ_TILE, D_MODEL), tok_map),
        pl.BlockSpec((SEQ_TILE, LANES), tok_map),
        pl.BlockSpec((ROUTE_ROWS, SEQ_TILE), lambda b, s: (0, b * n_s + s)),
        pl.BlockSpec((1, LANES), lambda b, s: (0, 0)),
    ]
    out_shape = [
        jax.ShapeDtypeStruct((n_tok, HALF), jnp.uint32),
        jax.ShapeDtypeStruct((n_tok, D_MODEL), jnp.float32),
        jax.ShapeDtypeStruct((n_tok, LANES), jnp.float32),
        jax.ShapeDtypeStruct((ROUTE_ROWS, n_tok), jnp.float32),
        jax.ShapeDtypeStruct((1, LANES), jnp.float32),
    ]
    return pl.pallas_call(
        _mixer_kernel,
        grid=(batch, n_s),
        in_specs=in_specs,
        out_specs=out_specs,
        out_shape=out_shape,
        scratch_shapes=[
            pltpu.VMEM((SEQ_TILE + CONV_HALO, CONV_WIDTH), jnp.float32),
            pltpu.VMEM((SEQ_TILE + POOL_HALO, POOL_WIDTH), jnp.float32),
            pltpu.VMEM((1, LANES), jnp.float32),
        ],
        compiler_params=pltpu.CompilerParams(
            dimension_semantics=("arbitrary", "arbitrary"), vmem_limit_bytes=VMEM_LIMIT),
    )(x2, p2, lnin_g, lnin_b, w_in, conv_w, conv_b, pool_w, pool_scale, w_out, ln1_g, ln1_b,
      w_r, b_r, w_pg, b_pg, w_ple)


def _plan_kernel(rt_ref, counts_ref, pos_ref, tile_start_ref, tiles_ref, pad_ref, *, n_rows):
    lane = lax.broadcasted_iota(jnp.int32, (ROUTE_ROWS, LANES), 1)
    counts = jnp.broadcast_to(counts_ref[...], (ROUTE_ROWS, LANES))
    tiles = jnp.floor((counts + (ROW_TILE - 1)) * (1.0 / ROW_TILE))
    tile_end = tiles
    shift = 1
    while shift < N_EXPERTS:
        tile_end = tile_end + jnp.where(lane >= shift, pltpu.roll(tile_end, shift, axis=1), 0.0)
        shift *= 2
    row_start = (tile_end - tiles) * ROW_TILE
    tile_start_ref[...] = (tile_end - tiles)[0:1, :].astype(jnp.int32)
    tiles_ref[...] = tiles[0:1, :].astype(jnp.int32)

    rt = rt_ref[...]
    ids = rt[R_ID1:R_ID2 + 1, :]
    start = jnp.zeros_like(ids)
    for e in range(N_EXPERTS):
        start = jnp.where(ids == e, row_start[0:1, e:e + 1], start)
    pos_ref[...] = (start + rt[R_RANK1:R_RANK2 + 1, :]).astype(jnp.int32)

    sub = lax.broadcasted_iota(jnp.int32, (N_EXPERTS, LANES), 0)
    lane_e = lax.broadcasted_iota(jnp.int32, (N_EXPERTS, LANES), 1)
    diag = sub == lane_e
    pad_lo = jnp.sum(jnp.where(diag, (row_start + counts)[0:1, :], 0.0), axis=1, keepdims=True)
    pad_n = jnp.sum(jnp.where(diag, (tiles * ROW_TILE - counts)[0:1, :], 0.0), axis=1, keepdims=True)
    j = lax.broadcasted_iota(jnp.int32, (N_EXPERTS, ROW_TILE), 1).astype(jnp.float32)
    pad_ref[...] = jnp.where(j < pad_n, pad_lo + j, n_rows + j).astype(jnp.int32)


def _run_plan(route_t, counts, n_rows):
    n_tok = route_t.shape[1]
    full = lambda shape: pl.BlockSpec(shape, lambda i: (0, 0))
    return pl.pallas_call(
        functools.partial(_plan_kernel, n_rows=n_rows),
        grid=(1,),
        in_specs=[full((ROUTE_ROWS, n_tok)), full((1, LANES))],
        out_specs=[full((2, n_tok)), full((1, LANES)), full((1, LANES)), full((N_EXPERTS, ROW_TILE))],
        out_shape=[
            jax.ShapeDtypeStruct((2, n_tok), jnp.int32),
            jax.ShapeDtypeStruct((1, LANES), jnp.int32),
            jax.ShapeDtypeStruct((1, LANES), jnp.int32),
            jax.ShapeDtypeStruct((N_EXPERTS, ROW_TILE), jnp.int32),
        ],
        compiler_params=pltpu.CompilerParams(dimension_semantics=("arbitrary",)),
    )(route_t, counts)


def _sc_mesh():
    return plsc.VectorSubcoreMesh(core_axis_name="core", subcore_axis_name="subcore")


def _sc_worker_id():
    return lax.axis_index("core") * (SC_WORKERS // 2) + lax.axis_index("subcore")


def _dispatch_rows(xp, pos1, pos2, pad_pos, zero_rows, n_rows):
    n_tok, width = xp.shape
    n_win = n_tok // SC_WORKERS // SC_WINDOW
    n_pad = pad_pos.shape[0] // SC_WORKERS // SC_WINDOW
    as_windows = lambda v: v.reshape(-1, SC_WINDOW)

    @functools.partial(
        pl.kernel, out_type=jax.ShapeDtypeStruct((n_rows + ROW_TILE, width), xp.dtype), mesh=_sc_mesh(),
        scratch_types=[pltpu.VMEM((n_win, SC_WINDOW), jnp.int32), pltpu.VMEM((n_win, SC_WINDOW), jnp.int32),
                       pltpu.VMEM((n_pad, SC_WINDOW), jnp.int32),
                       pltpu.VMEM((2, SC_WINDOW, width), xp.dtype),
                       pltpu.SemaphoreType.DMA((2,)), pltpu.SemaphoreType.DMA((2,))])
    def dispatch(xp_hbm, pos1_hbm, pos2_hbm, pad_hbm, zero_hbm, out_hbm, idx1, idx2, idxp, buf, lsem, ssem):
        wid = _sc_worker_id()
        pltpu.sync_copy(pos1_hbm.at[pl.ds(wid * n_win, n_win)], idx1)
        pltpu.sync_copy(pos2_hbm.at[pl.ds(wid * n_win, n_win)], idx2)
        pltpu.sync_copy(pad_hbm.at[pl.ds(wid * n_pad, n_pad)], idxp)

        def load(j):
            rows = xp_hbm.at[pl.ds((wid * n_win + j) * SC_WINDOW, SC_WINDOW)]
            return pltpu.make_async_copy(rows, buf.at[j % 2], lsem.at[j % 2])

        def scatters(j):
            return [pltpu.make_async_copy(buf.at[j % 2], out_hbm.at[idx.at[j]], ssem.at[j % 2])
                    for idx in (idx1, idx2)]

        load(0).start()
        for j in range(n_win):
            load(j).wait()
            for cp in scatters(j):
                cp.start()
            if j >= 1:
                for cp in scatters(j - 1):
                    cp.wait()
            if j + 1 < n_win:
                load(j + 1).start()
        for cp in scatters(n_win - 1):
            cp.wait()

        pltpu.sync_copy(zero_hbm, buf.at[0])
        pads = [pltpu.make_async_copy(buf.at[0], out_hbm.at[idxp.at[j]], ssem.at[0]) for j in range(n_pad)]
        for cp in pads:
            cp.start()
        for cp in pads:
            cp.wait()

    return dispatch(xp, as_windows(pos1), as_windows(pos2), as_windows(pad_pos), zero_rows)


def _gather_rows(src, idx):
    n_out, width = idx.shape[0], src.shape[1]
    n_win = n_out // SC_WORKERS // SC_WINDOW

    @functools.partial(
        pl.kernel, out_type=jax.ShapeDtypeStruct((n_out, width), src.dtype), mesh=_sc_mesh(),
        scratch_types=[pltpu.VMEM((n_win, SC_WINDOW), jnp.int32), pltpu.VMEM((2, SC_WINDOW, width), src.dtype),
                       pltpu.SemaphoreType.DMA((2,)), pltpu.SemaphoreType.DMA((2,))])
    def gather(src_hbm, idx_hbm, dst_hbm, idx_v, buf, gsem, ssem):
        wid = _sc_worker_id()
        pltpu.sync_copy(idx_hbm.at[pl.ds(wid * n_win, n_win)], idx_v)

        def fetch(j):
            return pltpu.make_async_copy(src_hbm.at[idx_v.at[j]], buf.at[j % 2], gsem.at[j % 2])

        def store(j):
            rows = dst_hbm.at[pl.ds((wid * n_win + j) * SC_WINDOW, SC_WINDOW)]
            return pltpu.make_async_copy(buf.at[j % 2], rows, ssem.at[j % 2])

        fetch(0).start()
        for j in range(n_win):
            fetch(j).wait()
            store(j).start()
            if j >= 1:
                store(j - 1).wait()
            if j + 1 < n_win:
                fetch(j + 1).start()
        store(n_win - 1).wait()

    return gather(src, idx.reshape(-1, SC_WINDOW))


def _tile_copy(hbm, buf, sem, tile, slot, to_hbm):
    rows = hbm.at[pl.ds(pl.multiple_of(tile * ROW_TILE, ROW_TILE), ROW_TILE)]
    if to_hbm:
        return pltpu.make_async_copy(buf.at[slot], rows, sem.at[slot])
    return pltpu.make_async_copy(rows, buf.at[slot], sem.at[slot])


def _weight_copies(w_hbm, wbuf, wsem, expert):
    slot = expert % WEIGHT_BUFFERS
    return [pltpu.make_async_copy(w.at[expert], buf.at[slot], wsem.at[slot]) for w, buf in zip(w_hbm, wbuf)]


def _expert_kernel(ts_ref, nte_ref, x_hbm, wg_hbm, wu_hbm, wd_hbm, y_hbm,
                   xbuf, ybuf, xsem, ysem, wg_buf, wu_buf, wd_buf, wsem, wgu_bf, wd_bf, act_ref, *, max_tiles):
    e = pl.program_id(0)
    first = ts_ref[e]
    count = nte_ref[e]
    n_tiles = ts_ref[N_EXPERTS - 1] + nte_ref[N_EXPERTS - 1]
    ahead = TILE_BUFFERS - 1
    w_hbm = (wg_hbm, wu_hbm, wd_hbm)
    wbuf = (wg_buf, wu_buf, wd_buf)

    @pl.when(e == 0)
    def _():
        for g in range(ahead):
            @pl.when(g < n_tiles)
            def _():
                _tile_copy(x_hbm, xbuf, xsem, g, g, False).start()
        for k in range(WEIGHT_BUFFERS - 1):
            for cp in _weight_copies(w_hbm, wbuf, wsem, k):
                cp.start()

    @pl.when(e + WEIGHT_BUFFERS - 1 < N_EXPERTS)
    def _():
        for cp in _weight_copies(w_hbm, wbuf, wsem, e + WEIGHT_BUFFERS - 1):
            cp.start()

    for cp in _weight_copies(w_hbm, wbuf, wsem, e):
        cp.wait()

    def sync_x(g):
        _tile_copy(x_hbm, xbuf, xsem, g, g % TILE_BUFFERS, False).wait()

        @pl.when(g + ahead < n_tiles)
        def _():
            _tile_copy(x_hbm, xbuf, xsem, g + ahead, (g + ahead) % TILE_BUFFERS, False).start()

    def sync_y_slot(g):
        @pl.when(g >= TILE_BUFFERS)
        def _():
            _tile_copy(y_hbm, ybuf, ysem, g - TILE_BUFFERS, g % TILE_BUFFERS, True).wait()

    def up_proj(g):
        lo, hi = _unpack_bf16_pairs(xbuf[g % TILE_BUFFERS])
        return (jnp.dot(lo.astype(jnp.bfloat16), wgu_bf[0:HALF, :], preferred_element_type=jnp.float32)
                + jnp.dot(hi.astype(jnp.bfloat16), wgu_bf[HALF:, :], preferred_element_type=jnp.float32))

    def put_act(hgu):
        hg = hgu[:, 0:D_EXPERT]
        act_ref[...] = (hg * jax.nn.sigmoid(hg) * hgu[:, D_EXPERT:]).astype(jnp.bfloat16)

    def down_proj(g):
        y = jnp.dot(act_ref[...], wd_bf[...], preferred_element_type=jnp.float32)
        ybuf[g % TILE_BUFFERS] = _pack_bf16_pairs(y)

    @pl.when(count > 0)
    def _():
        slot = e % WEIGHT_BUFFERS
        wgu_bf[:, 0:D_EXPERT] = wg_buf[slot].astype(jnp.bfloat16)
        wgu_bf[:, D_EXPERT:] = wu_buf[slot].astype(jnp.bfloat16)
        wd_bf[...] = wd_buf[slot].astype(jnp.bfloat16)

        sync_x(first)
        put_act(up_proj(first))

        def tile_body(g, c):
            sync_x(g)
            sync_y_slot(g - 1)
            down_proj(g - 1)
            hgu = up_proj(g)
            put_act(hgu)
            _tile_copy(y_hbm, ybuf, ysem, g - 1, (g - 1) % TILE_BUFFERS, True).start()
            return c

        lax.fori_loop(first + 1, first + count, tile_body, 0)
        last = first + count - 1
        sync_y_slot(last)
        down_proj(last)
        _tile_copy(y_hbm, ybuf, ysem, last, last % TILE_BUFFERS, True).start()

    @pl.when(e == N_EXPERTS - 1)
    def _():
        for k in range(TILE_BUFFERS, 0, -1):
            @pl.when(n_tiles >= k)
            def _():
                _tile_copy(y_hbm, ybuf, ysem, n_tiles - k, (n_tiles - k) % TILE_BUFFERS, True).wait()

        ybuf[0] = jnp.zeros((ROW_TILE, HALF), jnp.uint32)

        def fill(g, c):
            cp = _tile_copy(y_hbm, ybuf, ysem, g, 0, True)
            cp.start()
            cp.wait()
            return c

        lax.fori_loop(n_tiles, max_tiles, fill, 0)


def _run_experts(tile_start, tiles_per_expert, x_sorted, w_gate, w_up, w_down):
    n_rows = x_sorted.shape[0]
    max_tiles = n_rows // ROW_TILE
    hbm = pl.BlockSpec(memory_space=pl.ANY)

    grid_spec = pltpu.PrefetchScalarGridSpec(
        num_scalar_prefetch=2,
        grid=(N_EXPERTS,),
        in_specs=[hbm, hbm, hbm, hbm],
        out_specs=hbm,
        scratch_shapes=[
            pltpu.VMEM((TILE_BUFFERS, ROW_TILE, HALF), jnp.uint32),
            pltpu.VMEM((TILE_BUFFERS, ROW_TILE, HALF), jnp.uint32),
            pltpu.SemaphoreType.DMA((TILE_BUFFERS,)),
            pltpu.SemaphoreType.DMA((TILE_BUFFERS,)),
            pltpu.VMEM((WEIGHT_BUFFERS, D_MODEL, D_EXPERT), jnp.float32),
            pltpu.VMEM((WEIGHT_BUFFERS, D_MODEL, D_EXPERT), jnp.float32),
            pltpu.VMEM((WEIGHT_BUFFERS, D_EXPERT, D_MODEL), jnp.float32),
            pltpu.SemaphoreType.DMA((WEIGHT_BUFFERS,)),
            pltpu.VMEM((D_MODEL, 2 * D_EXPERT), jnp.bfloat16),
            pltpu.VMEM((D_EXPERT, D_MODEL), jnp.bfloat16),
            pltpu.VMEM((ROW_TILE, D_EXPERT), jnp.bfloat16),
        ],
    )
    return pl.pallas_call(
        functools.partial(_expert_kernel, max_tiles=max_tiles),
        grid_spec=grid_spec,
        out_shape=jax.ShapeDtypeStruct((n_rows, HALF), jnp.uint32),
        compiler_params=pltpu.CompilerParams(
            dimension_semantics=("arbitrary",), vmem_limit_bytes=VMEM_LIMIT),
    )(tile_start, tiles_per_expert, x_sorted, w_gate, w_up, w_down)


def _combine_kernel(r_ref, route_ref, ya_ref, yb_ref, g_ref, b_ref, *rest):
    o_ref = rest[-1]
    route = route_ref[...]
    w1 = route[:, R_W1:R_W1 + 1]
    w2 = route[:, R_W2:R_W2 + 1]
    a_lo, a_hi = _unpack_bf16_pairs(ya_ref[...])
    b_lo, b_hi = _unpack_bf16_pairs(yb_ref[...])
    moe = jnp.concatenate([w1 * a_lo + w2 * b_lo, w1 * a_hi + w2 * b_hi], axis=-1)
    o_ref[...] = _layernorm(r_ref[...] + moe, g_ref[...], b_ref[...])


def _run_combine(r, route, y_tok, ln2_g, ln2_b, out_prev, chunk, n_tok_total):
    n_tok = r.shape[0]
    n_t = n_tok // TOKEN_TILE
    in_specs = [
        pl.BlockSpec((TOKEN_TILE, D_MODEL), lambda i: (i, 0)),
        pl.BlockSpec((TOKEN_TILE, LANES), lambda i: (i, 0)),
        pl.BlockSpec((TOKEN_TILE, HALF), lambda i: (i, 0)),
        pl.BlockSpec((TOKEN_TILE, HALF), lambda i: (i + n_t, 0)),
        pl.BlockSpec((1, D_MODEL), lambda i: (0, 0)),
        pl.BlockSpec((1, D_MODEL), lambda i: (0, 0)),
    ]
    args = [r, route, y_tok, y_tok, ln2_g, ln2_b]
    aliases = {}
    if out_prev is not None:
        in_specs.append(pl.BlockSpec(memory_space=pl.ANY))
        args.append(out_prev)
        aliases = {len(args) - 1: 0}
    return pl.pallas_call(
        _combine_kernel,
        grid=(n_t,),
        in_specs=in_specs,
        out_specs=pl.BlockSpec((TOKEN_TILE, D_MODEL), lambda i: (chunk * n_t + i, 0)),
        out_shape=jax.ShapeDtypeStruct((n_tok_total, D_MODEL), jnp.float32),
        input_output_aliases=aliases,
        compiler_params=pltpu.CompilerParams(dimension_semantics=("arbitrary",), vmem_limit_bytes=VMEM_LIMIT),
    )(*args)


def _split_bf16(w):
    hi = w.astype(jnp.bfloat16)
    lo = (w - hi.astype(jnp.float32)).astype(jnp.bfloat16)
    return hi, lo


def kernel(x, p, ln_in_g, ln_in_b, w_in, conv_w, conv_b, pool_w, pool_scale, w_out, ln1_g, ln1_b,
           w_rg, b_rg, w_re, b_re, w_gate, w_up, w_down, w_pg, b_pg, w_ple, ln2_g, ln2_b):
    batch, seq, _ = x.shape
    n_tok = batch * seq
    bf = jnp.bfloat16
    row = lambda v: v.reshape(1, -1)

    w_r = jnp.concatenate([w_rg[0], jnp.transpose(w_re[0], (1, 0, 2)).reshape(D_MODEL, N_EXPERTS)], axis=1)
    w_r = jnp.pad(w_r, ((0, 0), (0, LANES - w_r.shape[1])))
    w_r_hi, w_r_lo = _split_bf16(w_r)
    w_r_cat = jnp.concatenate([w_r_hi, w_r_hi, w_r_lo], axis=0)
    b_r = jnp.pad(jnp.concatenate([b_rg[0], b_re[0].reshape(-1)]), (0, LANES - N_GROUPS - N_EXPERTS)).reshape(1, LANES)

    x2 = x.reshape(n_tok, D_MODEL)
    p2 = p[0].reshape(n_tok, PLE_DIM)
    mixer_weights = (row(ln_in_g), row(ln_in_b), w_in[0].astype(bf), conv_w[0], row(conv_b[0]),
                     pool_w[0].astype(bf), row(pool_scale[0]), w_out[0].astype(bf), row(ln1_g[0]), row(ln1_b[0]),
                     w_r_cat, b_r, w_pg[0].astype(bf), row(b_pg[0]), w_ple[0].astype(bf))
    expert_weights = (w_gate[0].reshape(N_EXPERTS, D_MODEL, D_EXPERT),
                      w_up[0].reshape(N_EXPERTS, D_MODEL, D_EXPERT),
                      w_down[0].reshape(N_EXPERTS, D_EXPERT, D_MODEL))
    zero_rows = jnp.zeros((SC_WINDOW, HALF), jnp.uint32)

    n_chunks = N_CHUNKS if batch % N_CHUNKS == 0 else 1
    chunk_batch = batch // n_chunks
    chunk_tok = chunk_batch * seq
    n_rows = -(-(2 * chunk_tok + N_EXPERTS * (ROW_TILE - 1)) // ROW_TILE) * ROW_TILE
    out = None
    for c in range(n_chunks):
        xp, r, route, route_t, counts = _run_mixer(x2, p2, *mixer_weights, c * chunk_batch, chunk_batch, seq)
        pos, tile_start, tiles_per_expert, pad_pos = _run_plan(route_t, counts, n_rows)
        x_sorted = _dispatch_rows(xp, pos[0], pos[1], pad_pos.reshape(-1), zero_rows, n_rows)
        y_sorted = _run_experts(tile_start[0, :N_EXPERTS], tiles_per_expert[0, :N_EXPERTS], x_sorted,
                                *expert_weights)
        y_tok = _gather_rows(y_sorted, pos.reshape(-1))
        out = _run_combine(r, route, y_tok, row(ln2_g[0]), row(ln2_b[0]), out, c, n_tok)
    return out.reshape(batch, seq, D_MODEL)
```

```python
import functools

import jax
import jax.numpy as jnp
from jax import lax
from jax.experimental import pallas as pl
from jax.experimental.pallas import tpu as pltpu
from jax.experimental.pallas import tpu_sc as plsc

D_MODEL = 1024
CONV_WIDTH = 512
CONV_K = 3
POOL_WIDTH = 512
POOL_WINDOWS = (2, 4, 8, 16)
POOL_GW = 128
IN_PROJ = 3 * CONV_WIDTH + POOL_WIDTH
N_GROUPS = 4
EXPERTS_PER_GROUP = 8
N_EXPERTS = N_GROUPS * EXPERTS_PER_GROUP
D_EXPERT = 256
PLE_DIM = 256
LN_EPS = 1e-5
DEEPNORM_ALPHA = 2.0 ** 0.25

LANES = 128
HALF = D_MODEL // 2
CONV_HALO = 8
POOL_HALO = 16
SEQ_TILE = 1024
SUB_TILE = 256
MIXER_PHASES = 7
ROW_TILE = 256
TILE_BUFFERS = 8
N_CHUNKS = 1
SC_WORKERS = 32
SC_WINDOW = 64
WEIGHT_BUFFERS = 3
TOKEN_TILE = 1024
VMEM_LIMIT = 56 * 1024 * 1024

R_ID1, R_ID2, R_RANK1, R_RANK2, R_W1, R_W2 = range(6)
ROUTE_ROWS = 8


def _layernorm(x, g, b):
    mu = jnp.mean(x, axis=-1, keepdims=True)
    xc = x - mu
    var = jnp.mean(xc * xc, axis=-1, keepdims=True)
    return xc * lax.rsqrt(var + LN_EPS) * g + b


def _pack_bf16_pairs(v):
    bits = lax.bitcast_convert_type(v.astype(jnp.bfloat16).astype(jnp.float32), jnp.uint32)
    return bits[:, HALF:] | (bits[:, :HALF] >> 16)


def _unpack_bf16_pairs(w):
    lo = lax.bitcast_convert_type(w << 16, jnp.float32)
    hi = lax.bitcast_convert_type(w & jnp.uint32(0xFFFF0000), jnp.float32)
    return lo, hi


def _mixer_kernel(x_ref, p_ref, lnin_g, lnin_b, w_in, conv_w, conv_b, pool_w, pool_scale, w_out,
                  ln1_g, ln1_b, w_r, b_r, w_pg, b_pg, w_ple,
                  xp_ref, r_ref, route_ref, route_t_ref, counts_ref,
                  zbuf, vbuf, carry):
    b = pl.program_id(0)
    s = pl.program_id(1)
    ts = x_ref.shape[0]

    @pl.when(s == 0)
    def _():
        zbuf[0:CONV_HALO, :] = jnp.zeros((CONV_HALO, CONV_WIDTH), jnp.float32)
        vbuf[0:POOL_HALO, :] = jnp.zeros((POOL_HALO, POOL_WIDTH), jnp.float32)

    @pl.when((b == 0) & (s == 0))
    def _():
        carry[...] = jnp.zeros_like(carry)

    chains = [_mixer_rows(r0, s * ts + r0, x_ref, p_ref, lnin_g, lnin_b, w_in, conv_w, conv_b, pool_w, pool_scale,
                          w_out, ln1_g, ln1_b, w_r, b_r, w_pg, b_pg, w_ple,
                          xp_ref, r_ref, route_ref, route_t_ref, zbuf, vbuf, carry)
              for r0 in range(0, ts, SUB_TILE)]
    for t in range(MIXER_PHASES + len(chains) - 1):
        for k, chain in enumerate(chains):
            if 0 <= t - k < MIXER_PHASES:
                next(chain, None)
    zbuf[0:CONV_HALO, :] = zbuf[ts:ts + CONV_HALO, :]
    vbuf[0:POOL_HALO, :] = vbuf[ts:ts + POOL_HALO, :]
    counts_ref[...] = carry[...]


def _mixer_rows(r0, seq0, x_ref, p_ref, lnin_g, lnin_b, w_in, conv_w, conv_b, pool_w, pool_scale, w_out,
                ln1_g, ln1_b, w_r, b_r, w_pg, b_pg, w_ple,
                xp_ref, r_ref, route_ref, route_t_ref, zbuf, vbuf, carry):
    n = SUB_TILE
    rows = pl.ds(r0, n)
    h0 = _layernorm(x_ref[rows, :], lnin_g[...], lnin_b[...])
    h0b = h0.astype(jnp.bfloat16)
    yield
    u = jnp.dot(h0b, w_in[...], preferred_element_type=jnp.float32)
    yield
    b_g = u[:, 0:CONV_WIDTH]
    c_g = u[:, CONV_WIDTH:2 * CONV_WIDTH]
    v_c = u[:, 2 * CONV_WIDTH:3 * CONV_WIDTH]
    v_p = u[:, 3 * CONV_WIDTH:]

    zbuf[pl.ds(CONV_HALO + r0, n), :] = c_g * v_c
    zext = zbuf[pl.ds(r0, n + CONV_HALO), :]
    z1 = pltpu.roll(zext, 1, axis=0)[CONV_HALO:, :]
    z2 = pltpu.roll(zext, 2, axis=0)[CONV_HALO:, :]
    conv = z2 * conv_w[0:1, :] + z1 * conv_w[1:2, :] + zext[CONV_HALO:, :] * conv_w[2:3, :] + conv_b[...]
    y_conv = b_g * conv

    vbuf[pl.ds(POOL_HALO + r0, n), :] = v_p
    vext = vbuf[pl.ds(r0, n + POOL_HALO), :]
    s2 = vext + pltpu.roll(vext, 1, axis=0)
    s4 = s2[:, POOL_GW:] + pltpu.roll(s2[:, POOL_GW:], 2, axis=0)
    s8 = s4[:, POOL_GW:] + pltpu.roll(s4[:, POOL_GW:], 4, axis=0)
    s16 = s8[:, POOL_GW:] + pltpu.roll(s8[:, POOL_GW:], 8, axis=0)
    wsums = (s2[POOL_HALO:, 0:POOL_GW], s4[POOL_HALO:, 0:POOL_GW],
             s8[POOL_HALO:, 0:POOL_GW], s16[POOL_HALO:, 0:POOL_GW])

    t_pos = seq0 + lax.broadcasted_iota(jnp.int32, (n, 1), 0)
    y_pool = []
    for j, w in enumerate(POOL_WINDOWS):
        inv_cnt = 1.0 / jnp.minimum(t_pos + 1, w).astype(jnp.float32)
        pooled = wsums[j] * inv_cnt - v_p[:, j * POOL_GW:(j + 1) * POOL_GW]
        y_pool.append(jnp.dot(pooled.astype(jnp.bfloat16), pool_w[j], preferred_element_type=jnp.float32))
    y_pool = jnp.concatenate(y_pool, axis=-1) * pool_scale[...]

    ycat = jnp.concatenate([y_conv, y_pool], axis=-1).astype(jnp.bfloat16)
    yield
    mix = jnp.dot(ycat, w_out[...], preferred_element_type=jnp.float32)
    yield
    h1 = _layernorm(DEEPNORM_ALPHA * h0 + mix, ln1_g[...], ln1_b[...])

    h_hi = h1.astype(jnp.bfloat16)
    h_lo = (h1 - h_hi.astype(jnp.float32)).astype(jnp.bfloat16)
    xp_ref[rows, :] = _pack_bf16_pairs(h1)
    p_b = p_ref[rows, :].astype(jnp.bfloat16)
    hcat = jnp.concatenate([h_hi, h_lo, h_hi], axis=-1)
    yield
    gate_pre = jnp.dot(h_hi, w_pg[...], preferred_element_type=jnp.float32)
    ple_pre = jnp.dot(p_b, w_ple[...], preferred_element_type=jnp.float32)
    logits = jnp.dot(hcat, w_r[...], preferred_element_type=jnp.float32) + b_r[...]
    yield

    r_ref[rows, :] = DEEPNORM_ALPHA * h1 + ple_pre * jax.nn.sigmoid(gate_pre + b_pg[...])
    lane = lax.broadcasted_iota(jnp.int32, (n, LANES), 1)
    neg = jnp.float32(-jnp.inf)

    def first_argmax(vals):
        m = jnp.max(vals, axis=-1, keepdims=True)
        idx = jnp.min(jnp.where(vals == m, lane, LANES), axis=-1, keepdims=True)
        return m, idx

    g_mask = lane < N_GROUPS
    g_max, g_idx = first_argmax(jnp.where(g_mask, logits, neg))
    g_w = 1.0 / jnp.sum(jnp.where(g_mask, jnp.exp(logits - g_max), 0.0), axis=-1, keepdims=True)

    e_lo = N_GROUPS + EXPERTS_PER_GROUP * g_idx
    e_vals = jnp.where((lane >= e_lo) & (lane < e_lo + EXPERTS_PER_GROUP), logits, neg)
    m1, i1 = first_argmax(e_vals)
    m2, i2 = first_argmax(jnp.where(lane == i1, neg, e_vals))
    e21 = jnp.exp(m2 - m1)
    w1 = g_w / (1.0 + e21)
    w2 = g_w * e21 / (1.0 + e21)
    id1 = i1 - N_GROUPS
    id2 = i2 - N_GROUPS

    sel1 = lane == id1
    sel2 = lane == id2
    onehot = (sel1 | sel2).astype(jnp.float32)
    row = lax.broadcasted_iota(jnp.int32, (n, n), 0)
    col = lax.broadcasted_iota(jnp.int32, (n, n), 1)
    tri = (col < row).astype(jnp.bfloat16)
    before = jnp.dot(tri, onehot.astype(jnp.bfloat16), preferred_element_type=jnp.float32) + carry[...]
    rank1 = jnp.sum(jnp.where(sel1, before, 0.0), axis=-1, keepdims=True)
    rank2 = jnp.sum(jnp.where(sel2, before, 0.0), axis=-1, keepdims=True)
    carry[...] = carry[...] + jnp.sum(onehot, axis=0, keepdims=True)

    rec = jnp.zeros((n, LANES), jnp.float32)
    for k, val in ((R_ID1, id1.astype(jnp.float32)), (R_ID2, id2.astype(jnp.float32)), (R_W1, w1), (R_W2, w2),
                   (R_RANK1, rank1), (R_RANK2, rank2)):
        rec = jnp.where(lane == k, val, rec)
    route_ref[rows, :] = rec
    route_t_ref[:, rows] = jnp.transpose(rec)[0:ROUTE_ROWS, :]


def _run_mixer(x2, p2, lnin_g, lnin_b, w_in, conv_w, conv_b, pool_w, pool_scale, w_out, ln1_g, ln1_b,
               w_r, b_r, w_pg, b_pg, w_ple, batch_lo, batch, seq):
    n_tok = batch * seq
    n_s = seq // SEQ_TILE
    tok_map = lambda b, s: (b * n_s + s, 0)
    in_map = lambda b, s: ((batch_lo + b) * n_s + s, 0)

    def const(shape):
        return pl.BlockSpec(shape, lambda b, s: (0,) * len(shape), pipeline_mode=pl.Buffered(1))

    in_specs = [
        pl.BlockSpec((SEQ_TILE, D_MODEL), in_map),
        pl.BlockSpec((SEQ_TILE, PLE_DIM), in_map),
        const((1, D_MODEL)), const((1, D_MODEL)),
        const((D_MODEL, IN_PROJ)),
        const((CONV_K, CONV_WIDTH)), const((1, CONV_WIDTH)),
        const((len(POOL_WINDOWS), POOL_GW, POOL_GW)), const((1, POOL_WIDTH)),
        const((D_MODEL, D_MODEL)),
        const((1, D_MODEL)), const((1, D_MODEL)),
        const((3 * D_MODEL, LANES)), const((1, LANES)),
        const((D_MODEL, D_MODEL)), const((1, D_MODEL)),
        const((PLE_DIM, D_MODEL)),
    ]
    out_specs = [
        pl.BlockSpec((SEQ_TILE, HALF), tok_map),
        pl.BlockSpec((SEQ_TILE, D_MODEL), tok_map),
        pl.BlockSpec((SEQ_TILE, LANES), tok_map),
        pl.BlockSpec((ROUTE_ROWS, SEQ_TILE), lambda b, s: (0, b * n_s + s)),
        pl.BlockSpec((1, LANES), lambda b, s: (0, 0)),
    ]
    out_shape = [
        jax.ShapeDtypeStruct((n_tok, HALF), jnp.uint32),
        jax.ShapeDtypeStruct((n_tok, D_MODEL), jnp.float32),
        jax.ShapeDtypeStruct((n_tok, LANES), jnp.float32),
        jax.ShapeDtypeStruct((ROUTE_ROWS, n_tok), jnp.float32),
        jax.ShapeDtypeStruct((1, LANES), jnp.float32),
    ]
    return pl.pallas_call(
        _mixer_kernel,
        grid=(batch, n_s),
        in_specs=in_specs,
        out_specs=out_specs,
        out_shape=out_shape,
        scratch_shapes=[
            pltpu.VMEM((SEQ_TILE + CONV_HALO, CONV_WIDTH), jnp.float32),
            pltpu.VMEM((SEQ_TILE + POOL_HALO, POOL_WIDTH), jnp.float32),
            pltpu.VMEM((1, LANES), jnp.float32),
        ],
        compiler_params=pltpu.CompilerParams(
            dimension_semantics=("arbitrary", "arbitrary"), vmem_limit_bytes=VMEM_LIMIT),
    )(x2, p2, lnin_g, lnin_b, w_in, conv_w, conv_b, pool_w, pool_scale, w_out, ln1_g, ln1_b,
      w_r, b_r, w_pg, b_pg, w_ple)


def _plan_kernel(rt_ref, counts_ref, pos_ref, tile_start_ref, tiles_ref, pad_ref, *, n_rows):
    lane = lax.broadcasted_iota(jnp.int32, (ROUTE_ROWS, LANES), 1)
    counts = jnp.broadcast_to(counts_ref[...], (ROUTE_ROWS, LANES))
    tiles = jnp.floor((counts + (ROW_TILE - 1)) * (1.0 / ROW_TILE))
    tile_end = tiles
    shift = 1
    while shift < N_EXPERTS:
        tile_end = tile_end + jnp.where(lane >= shift, pltpu.roll(tile_end, shift, axis=1), 0.0)
        shift *= 2
    row_start = (tile_end - tiles) * ROW_TILE
    tile_start_ref[...] = (tile_end - tiles)[0:1, :].astype(jnp.int32)
    tiles_ref[...] = tiles[0:1, :].astype(jnp.int32)

    rt = rt_ref[...]
    ids = rt[R_ID1:R_ID2 + 1, :]
    start = jnp.zeros_like(ids)
    for e in range(N_EXPERTS):
        start = jnp.where(ids == e, row_start[0:1, e:e + 1], start)
    pos_ref[...] = (start + rt[R_RANK1:R_RANK2 + 1, :]).astype(jnp.int32)

    sub = lax.broadcasted_iota(jnp.int32, (N_EXPERTS, LANES), 0)
    lane_e = lax.broadcasted_iota(jnp.int32, (N_EXPERTS, LANES), 1)
    diag = sub == lane_e
    pad_lo = jnp.sum(jnp.where(diag, (row_start + counts)[0:1, :], 0.0), axis=1, keepdims=True)
    pad_n = jnp.sum(jnp.where(diag, (tiles * ROW_TILE - counts)[0:1, :], 0.0), axis=1, keepdims=True)
    j = lax.broadcasted_iota(jnp.int32, (N_EXPERTS, ROW_TILE), 1).astype(jnp.float32)
    pad_ref[...] = jnp.where(j < pad_n, pad_lo + j, n_rows + j).astype(jnp.int32)


def _run_plan(route_t, counts, n_rows):
    n_tok = route_t.shape[1]
    full = lambda shape: pl.BlockSpec(shape, lambda i: (0, 0))
    return pl.pallas_call(
        functools.partial(_plan_kernel, n_rows=n_rows),
        grid=(1,),
        in_specs=[full((ROUTE_ROWS, n_tok)), full((1, LANES))],
        out_specs=[full((2, n_tok)), full((1, LANES)), full((1, LANES)), full((N_EXPERTS, ROW_TILE))],
        out_shape=[
            jax.ShapeDtypeStruct((2, n_tok), jnp.int32),
            jax.ShapeDtypeStruct((1, LANES), jnp.int32),
            jax.ShapeDtypeStruct((1, LANES), jnp.int32),
            jax.ShapeDtypeStruct((N_EXPERTS, ROW_TILE), jnp.int32),
        ],
        compiler_params=pltpu.CompilerParams(dimension_semantics=("arbitrary",)),
    )(route_t, counts)


def _sc_mesh():
    return plsc.VectorSubcoreMesh(core_axis_name="core", subcore_axis_name="subcore")


def _sc_worker_id():
    return lax.axis_index("core") * (SC_WORKERS // 2) + lax.axis_index("subcore")


def _dispatch_rows(xp, pos1, pos2, pad_pos, zero_rows, n_rows):
    n_tok, width = xp.shape
    n_win = n_tok // SC_WORKERS // SC_WINDOW
    n_pad = pad_pos.shape[0] // SC_WORKERS // SC_WINDOW
    as_windows = lambda v: v.reshape(-1, SC_WINDOW)

    @functools.partial(
        pl.kernel, out_type=jax.ShapeDtypeStruct((n_rows + ROW_TILE, width), xp.dtype), mesh=_sc_mesh(),
        scratch_types=[pltpu.VMEM((n_win, SC_WINDOW), jnp.int32), pltpu.VMEM((n_win, SC_WINDOW), jnp.int32),
                       pltpu.VMEM((n_pad, SC_WINDOW), jnp.int32),
                       pltpu.VMEM((2, SC_WINDOW, width), xp.dtype),
                       pltpu.SemaphoreType.DMA((2,)), pltpu.SemaphoreType.DMA((2,))])
    def dispatch(xp_hbm, pos1_hbm, pos2_hbm, pad_hbm, zero_hbm, out_hbm, idx1, idx2, idxp, buf, lsem, ssem):
        wid = _sc_worker_id()
        pltpu.sync_copy(pos1_hbm.at[pl.ds(wid * n_win, n_win)], idx1)
        pltpu.sync_copy(pos2_hbm.at[pl.ds(wid * n_win, n_win)], idx2)
        pltpu.sync_copy(pad_hbm.at[pl.ds(wid * n_pad, n_pad)], idxp)

        def load(j):
            rows = xp_hbm.at[pl.ds((wid * n_win + j) * SC_WINDOW, SC_WINDOW)]
            return pltpu.make_async_copy(rows, buf.at[j % 2], lsem.at[j % 2])

        def scatters(j):
            return [pltpu.make_async_copy(buf.at[j % 2], out_hbm.at[idx.at[j]], ssem.at[j % 2])
                    for idx in (idx1, idx2)]

        load(0).start()
        for j in range(n_win):
            load(j).wait()
            for cp in scatters(j):
                cp.start()
            if j >= 1:
                for cp in scatters(j - 1):
                    cp.wait()
            if j + 1 < n_win:
                load(j + 1).start()
        for cp in scatters(n_win - 1):
            cp.wait()

        pltpu.sync_copy(zero_hbm, buf.at[0])
        pads = [pltpu.make_async_copy(buf.at[0], out_hbm.at[idxp.at[j]], ssem.at[0]) for j in range(n_pad)]
        for cp in pads:
            cp.start()
        for cp in pads:
            cp.wait()

    return dispatch(xp, as_windows(pos1), as_windows(pos2), as_windows(pad_pos), zero_rows)


def _gather_rows(src, idx):
    n_out, width = idx.shape[0], src.shape[1]
    n_win = n_out // SC_WORKERS // SC_WINDOW

    @functools.partial(
        pl.kernel, out_type=jax.ShapeDtypeStruct((n_out, width), src.dtype), mesh=_sc_mesh(),
        scratch_types=[pltpu.VMEM((n_win, SC_WINDOW), jnp.int32), pltpu.VMEM((2, SC_WINDOW, width), src.dtype),
                       pltpu.SemaphoreType.DMA((2,)), pltpu.SemaphoreType.DMA((2,))])
    def gather(src_hbm, idx_hbm, dst_hbm, idx_v, buf, gsem, ssem):
        wid = _sc_worker_id()
        pltpu.sync_copy(idx_hbm.at[pl.ds(wid * n_win, n_win)], idx_v)

        def fetch(j):
            return pltpu.make_async_copy(src_hbm.at[idx_v.at[j]], buf.at[j % 2], gsem.at[j % 2])

        def store(j):
            rows = dst_hbm.at[pl.ds((wid * n_win + j) * SC_WINDOW, SC_WINDOW)]
            return pltpu.make_async_copy(buf.at[j % 2], rows, ssem.at[j % 2])

        fetch(0).start()
        for j in range(n_win):
            fetch(j).wait()
            store(j).start()
            if j >= 1:
                store(j - 1).wait()
            if j + 1 < n_win:
                fetch(j + 1).start()
        store(n_win - 1).wait()

    return gather(src, idx.reshape(-1, SC_WINDOW))


def _tile_copy(hbm, buf, sem, tile, slot, to_hbm):
    rows = hbm.at[pl.ds(pl.multiple_of(tile * ROW_TILE, ROW_TILE), ROW_TILE)]
    if to_hbm:
        return pltpu.make_async_copy(buf.at[slot], rows, sem.at[slot])
    return pltpu.make_async_copy(rows, buf.at[slot], sem.at[slot])


def _weight_copies(w_hbm, wbuf, wsem, expert):
    slot = expert % WEIGHT_BUFFERS
    return [pltpu.make_async_copy(w.at[expert], buf.at[slot], wsem.at[slot]) for w, buf in zip(w_hbm, wbuf)]


def _expert_kernel(ts_ref, nte_ref, x_hbm, wg_hbm, wu_hbm, wd_hbm, y_hbm,
                   xbuf, ybuf, xsem, ysem, wg_buf, wu_buf, wd_buf, wsem, wgu_bf, wd_bf, act_ref, *, max_tiles):
    e = pl.program_id(0)
    first = ts_ref[e]
    count = nte_ref[e]
    n_tiles = ts_ref[N_EXPERTS - 1] + nte_ref[N_EXPERTS - 1]
    ahead = TILE_BUFFERS - 1
    w_hbm = (wg_hbm, wu_hbm, wd_hbm)
    wbuf = (wg_buf, wu_buf, wd_buf)

    @pl.when(e == 0)
    def _():
        for g in range(ahead):
            @pl.when(g < n_tiles)
            def _():
                _tile_copy(x_hbm, xbuf, xsem, g, g, False).start()
        for k in range(WEIGHT_BUFFERS - 1):
            for cp in _weight_copies(w_hbm, wbuf, wsem, k):
                cp.start()

    @pl.when(e + WEIGHT_BUFFERS - 1 < N_EXPERTS)
    def _():
        for cp in _weight_copies(w_hbm, wbuf, wsem, e + WEIGHT_BUFFERS - 1):
            cp.start()

    for cp in _weight_copies(w_hbm, wbuf, wsem, e):
        cp.wait()

    def sync_x(g):
        _tile_copy(x_hbm, xbuf, xsem, g, g % TILE_BUFFERS, False).wait()

        @pl.when(g + ahead < n_tiles)
        def _():
            _tile_copy(x_hbm, xbuf, xsem, g + ahead, (g + ahead) % TILE_BUFFERS, False).start()

    def sync_y_slot(g):
        @pl.when(g >= TILE_BUFFERS)
        def _():
            _tile_copy(y_hbm, ybuf, ysem, g - TILE_BUFFERS, g % TILE_BUFFERS, True).wait()

    def up_proj(g):
        lo, hi = _unpack_bf16_pairs(xbuf[g % TILE_BUFFERS])
        return (jnp.dot(lo.astype(jnp.bfloat16), wgu_bf[0:HALF, :], preferred_element_type=jnp.float32)
                + jnp.dot(hi.astype(jnp.bfloat16), wgu_bf[HALF:, :], preferred_element_type=jnp.float32))

    def put_act(hgu):
        hg = hgu[:, 0:D_EXPERT]
        act_ref[...] = (hg * jax.nn.sigmoid(hg) * hgu[:, D_EXPERT:]).astype(jnp.bfloat16)

    def down_proj(g):
        y = jnp.dot(act_ref[...], wd_bf[...], preferred_element_type=jnp.float32)
        ybuf[g % TILE_BUFFERS] = _pack_bf16_pairs(y)

    @pl.when(count > 0)
    def _():
        slot = e % WEIGHT_BUFFERS
        wgu_bf[:, 0:D_EXPERT] = wg_buf[slot].astype(jnp.bfloat16)
        wgu_bf[:, D_EXPERT:] = wu_buf[slot].astype(jnp.bfloat16)
        wd_bf[...] = wd_buf[slot].astype(jnp.bfloat16)

        sync_x(first)
        put_act(up_proj(first))

        def tile_body(g, c):
            sync_x(g)
            sync_y_slot(g - 1)
            down_proj(g - 1)
            hgu = up_proj(g)
            put_act(hgu)
            _tile_copy(y_hbm, ybuf, ysem, g - 1, (g - 1) % TILE_BUFFERS, True).start()
            return c

        lax.fori_loop(first + 1, first + count, tile_body, 0)
        last = first + count - 1
        sync_y_slot(last)
        down_proj(last)
        _tile_copy(y_hbm, ybuf, ysem, last, last % TILE_BUFFERS, True).start()

    @pl.when(e == N_EXPERTS - 1)
    def _():
        for k in range(TILE_BUFFERS, 0, -1):
            @pl.when(n_tiles >= k)
            def _():
                _tile_copy(y_hbm, ybuf, ysem, n_tiles - k, (n_tiles - k) % TILE_BUFFERS, True).wait()

        ybuf[0] = jnp.zeros((ROW_TILE, HALF), jnp.uint32)

        def fill(g, c):
            cp = _tile_copy(y_hbm, ybuf, ysem, g, 0, True)
            cp.start()
            cp.wait()
            return c

        lax.fori_loop(n_tiles, max_tiles, fill, 0)


def _run_experts(tile_start, tiles_per_expert, x_sorted, w_gate, w_up, w_down):
    n_rows = x_sorted.shape[0]
    max_tiles = n_rows // ROW_TILE
    hbm = pl.BlockSpec(memory_space=pl.ANY)

    grid_spec = pltpu.PrefetchScalarGridSpec(
        num_scalar_prefetch=2,
        grid=(N_EXPERTS,),
        in_specs=[hbm, hbm, hbm, hbm],
        out_specs=hbm,
        scratch_shapes=[
            pltpu.VMEM((TILE_BUFFERS, ROW_TILE, HALF), jnp.uint32),
            pltpu.VMEM((TILE_BUFFERS, ROW_TILE, HALF), jnp.uint32),
            pltpu.SemaphoreType.DMA((TILE_BUFFERS,)),
            pltpu.SemaphoreType.DMA((TILE_BUFFERS,)),
            pltpu.VMEM((WEIGHT_BUFFERS, D_MODEL, D_EXPERT), jnp.float32),
            pltpu.VMEM((WEIGHT_BUFFERS, D_MODEL, D_EXPERT), jnp.float32),
            pltpu.VMEM((WEIGHT_BUFFERS, D_EXPERT, D_MODEL), jnp.float32),
            pltpu.SemaphoreType.DMA((WEIGHT_BUFFERS,)),
            pltpu.VMEM((D_MODEL, 2 * D_EXPERT), jnp.bfloat16),
            pltpu.VMEM((D_EXPERT, D_MODEL), jnp.bfloat16),
            pltpu.VMEM((ROW_TILE, D_EXPERT), jnp.bfloat16),
        ],
    )
    return pl.pallas_call(
        functools.partial(_expert_kernel, max_tiles=max_tiles),
        grid_spec=grid_spec,
        out_shape=jax.ShapeDtypeStruct((n_rows, HALF), jnp.uint32),
        compiler_params=pltpu.CompilerParams(
            dimension_semantics=("arbitrary",), vmem_limit_bytes=VMEM_LIMIT),
    )(tile_start, tiles_per_expert, x_sorted, w_gate, w_up, w_down)


def _combine_kernel(r_ref, route_ref, ya_ref, yb_ref, g_ref, b_ref, *rest):
    o_ref = rest[-1]
    route = route_ref[...]
    w1 = route[:, R_W1:R_W1 + 1]
    w2 = route[:, R_W2:R_W2 + 1]
    a_lo, a_hi = _unpack_bf16_pairs(ya_ref[...])
    b_lo, b_hi = _unpack_bf16_pairs(yb_ref[...])
    moe = jnp.concatenate([w1 * a_lo + w2 * b_lo, w1 * a_hi + w2 * b_hi], axis=-1)
    o_ref[...] = _layernorm(r_ref[...] + moe, g_ref[...], b_ref[...])


def _run_combine(r, route, y_tok, ln2_g, ln2_b, out_prev, chunk, n_tok_total):
    n_tok = r.shape[0]
    n_t = n_tok // TOKEN_TILE
    in_specs = [
        pl.BlockSpec((TOKEN_TILE, D_MODEL), lambda i: (i, 0)),
        pl.BlockSpec((TOKEN_TILE, LANES), lambda i: (i, 0)),
        pl.BlockSpec((TOKEN_TILE, HALF), lambda i: (i, 0)),
        pl.BlockSpec((TOKEN_TILE, HALF), lambda i: (i + n_t, 0)),
        pl.BlockSpec((1, D_MODEL), lambda i: (0, 0)),
        pl.BlockSpec((1, D_MODEL), lambda i: (0, 0)),
    ]
    args = [r, route, y_tok, y_tok, ln2_g, ln2_b]
    aliases = {}
    if out_prev is not None:
        in_specs.append(pl.BlockSpec(memory_space=pl.ANY))
        args.append(out_prev)
        aliases = {len(args) - 1: 0}
    return pl.pallas_call(
        _combine_kernel,
        grid=(n_t,),
        in_specs=in_specs,
        out_specs=pl.BlockSpec((TOKEN_TILE, D_MODEL), lambda i: (chunk * n_t + i, 0)),
        out_shape=jax.ShapeDtypeStruct((n_tok_total, D_MODEL), jnp.float32),
        input_output_aliases=aliases,
        compiler_params=pltpu.CompilerParams(dimension_semantics=("arbitrary",), vmem_limit_bytes=VMEM_LIMIT),
    )(*args)


def _split_bf16(w):
    hi = w.astype(jnp.bfloat16)
    lo = (w - hi.astype(jnp.float32)).astype(jnp.bfloat16)
    return hi, lo


def kernel(x, p, ln_in_g, ln_in_b, w_in, conv_w, conv_b, pool_w, pool_scale, w_out, ln1_g, ln1_b,
           w_rg, b_rg, w_re, b_re, w_gate, w_up, w_down, w_pg, b_pg, w_ple, ln2_g, ln2_b):
    batch, seq, _ = x.shape
    n_tok = batch * seq
    bf = jnp.bfloat16
    row = lambda v: v.reshape(1, -1)

    w_r = jnp.concatenate([w_rg[0], jnp.transpose(w_re[0], (1, 0, 2)).reshape(D_MODEL, N_EXPERTS)], axis=1)
    w_r = jnp.pad(w_r, ((0, 0), (0, LANES - w_r.shape[1])))
    w_r_hi, w_r_lo = _split_bf16(w_r)
    w_r_cat = jnp.concatenate([w_r_hi, w_r_hi, w_r_lo], axis=0)
    b_r = jnp.pad(jnp.concatenate([b_rg[0], b_re[0].reshape(-1)]), (0, LANES - N_GROUPS - N_EXPERTS)).reshape(1, LANES)

    x2 = x.reshape(n_tok, D_MODEL)
    p2 = p[0].reshape(n_tok, PLE_DIM)
    mixer_weights = (row(ln_in_g), row(ln_in_b), w_in[0].astype(bf), conv_w[0], row(conv_b[0]),
                     pool_w[0].astype(bf), row(pool_scale[0]), w_out[0].astype(bf), row(ln1_g[0]), row(ln1_b[0]),
                     w_r_cat, b_r, w_pg[0].astype(bf), row(b_pg[0]), w_ple[0].astype(bf))
    expert_weights = (w_gate[0].reshape(N_EXPERTS, D_MODEL, D_EXPERT),
                      w_up[0].reshape(N_EXPERTS, D_MODEL, D_EXPERT),
                      w_down[0].reshape(N_EXPERTS, D_EXPERT, D_MODEL))
    zero_rows = jnp.zeros((SC_WINDOW, HALF), jnp.uint32)

    n_chunks = N_CHUNKS if batch % N_CHUNKS == 0 else 1
    chunk_batch = batch // n_chunks
    chunk_tok = chunk_batch * seq
    n_rows = -(-(2 * chunk_tok + N_EXPERTS * (ROW_TILE - 1)) // ROW_TILE) * ROW_TILE
    out = None
    for c in range(n_chunks):
        xp, r, route, route_t, counts = _run_mixer(x2, p2, *mixer_weights, c * chunk_batch, chunk_batch, seq)
        pos, tile_start, tiles_per_expert, pad_pos = _run_plan(route_t, counts, n_rows)
        x_sorted = _dispatch_rows(xp, pos[0], pos[1], pad_pos.reshape(-1), zero_rows, n_rows)
        y_sorted = _run_experts(tile_start[0, :N_EXPERTS], tiles_per_expert[0, :N_EXPERTS], x_sorted,
                                *expert_weights)
        y_tok = _gather_rows(y_sorted, pos.reshape(-1))
        out = _run_combine(r, route, y_tok, row(ln2_g[0]), row(ln2_b[0]), out, c, n_tok)
    return out.reshape(batch, seq, D_MODEL)
```

```python
import functools

import jax
import jax.numpy as jnp
from jax import lax
from jax.experimental import pallas as pl
from jax.experimental.pallas import tpu as pltpu
from jax.experimental.pallas import tpu_sc as plsc

D_MODEL = 1024
CONV_WIDTH = 512
CONV_K = 3
POOL_WIDTH = 512
POOL_WINDOWS = (2, 4, 8, 16)
POOL_GW = 128
IN_PROJ = 3 * CONV_WIDTH + POOL_WIDTH
N_GROUPS = 4
EXPERTS_PER_GROUP = 8
N_EXPERTS = N_GROUPS * EXPERTS_PER_GROUP
D_EXPERT = 256
PLE_DIM = 256
LN_EPS = 1e-5
DEEPNORM_ALPHA = 2.0 ** 0.25

LANES = 128
HALF = D_MODEL // 2
CONV_HALO = 8
POOL_HALO = 16
SEQ_TILE = 1024
SUB_TILE = 256
MIXER_PHASES = 7
COMBINE_PHASES = 3
ROW_TILE = 256
TILE_BUFFERS = 8
N_CHUNKS = 1
SC_WORKERS = 32
SC_WINDOW = 64
WEIGHT_BUFFERS = 3
TOKEN_TILE = 1024
VMEM_LIMIT = 56 * 1024 * 1024

R_ID1, R_ID2, R_RANK1, R_RANK2, R_W1, R_W2 = range(6)
ROUTE_ROWS = 8


def _layernorm(x, g, b):
    mu = jnp.mean(x, axis=-1, keepdims=True)
    xc = x - mu
    var = jnp.mean(xc * xc, axis=-1, keepdims=True)
    return xc * lax.rsqrt(var + LN_EPS) * g + b


def _pack_bf16_pairs(v):
    bits = lax.bitcast_convert_type(v.astype(jnp.bfloat16).astype(jnp.float32), jnp.uint32)
    return bits[:, HALF:] | (bits[:, :HALF] >> 16)


def _unpack_bf16_pairs(w):
    lo = lax.bitcast_convert_type(w << 16, jnp.float32)
    hi = lax.bitcast_convert_type(w & jnp.uint32(0xFFFF0000), jnp.float32)
    return lo, hi


def _mixer_kernel(x_ref, lnin_g, lnin_b, w_in, conv_w, conv_b, pool_w, pool_scale, w_out,
                  ln1_g, ln1_b, w_r, b_r,
                  xp_ref, h1_ref, route_ref, route_t_ref, counts_ref,
                  zbuf, vbuf, carry):
    b = pl.program_id(0)
    s = pl.program_id(1)
    ts = x_ref.shape[0]

    @pl.when(s == 0)
    def _():
        zbuf[0:CONV_HALO, :] = jnp.zeros((CONV_HALO, CONV_WIDTH), jnp.float32)
        vbuf[0:POOL_HALO, :] = jnp.zeros((POOL_HALO, POOL_WIDTH), jnp.float32)

    @pl.when((b == 0) & (s == 0))
    def _():
        carry[...] = jnp.zeros_like(carry)

    chains = [_mixer_rows(r0, s * ts + r0, x_ref, lnin_g, lnin_b, w_in, conv_w, conv_b, pool_w, pool_scale,
                          w_out, ln1_g, ln1_b, w_r, b_r,
                          xp_ref, h1_ref, route_ref, route_t_ref, zbuf, vbuf, carry)
              for r0 in range(0, ts, SUB_TILE)]
    for t in range(MIXER_PHASES + len(chains) - 1):
        for k, chain in enumerate(chains):
            if 0 <= t - k < MIXER_PHASES:
                next(chain, None)
    zbuf[0:CONV_HALO, :] = zbuf[ts:ts + CONV_HALO, :]
    vbuf[0:POOL_HALO, :] = vbuf[ts:ts + POOL_HALO, :]
    counts_ref[...] = carry[...]


def _mixer_rows(r0, seq0, x_ref, lnin_g, lnin_b, w_in, conv_w, conv_b, pool_w, pool_scale, w_out,
                ln1_g, ln1_b, w_r, b_r,
                xp_ref, h1_ref, route_ref, route_t_ref, zbuf, vbuf, carry):
    n = SUB_TILE
    rows = pl.ds(r0, n)
    h0 = _layernorm(x_ref[rows, :], lnin_g[...], lnin_b[...])
    h0b = h0.astype(jnp.bfloat16)
    yield
    u = jnp.dot(h0b, w_in[...], preferred_element_type=jnp.float32)
    yield
    b_g = u[:, 0:CONV_WIDTH]
    c_g = u[:, CONV_WIDTH:2 * CONV_WIDTH]
    v_c = u[:, 2 * CONV_WIDTH:3 * CONV_WIDTH]
    v_p = u[:, 3 * CONV_WIDTH:]

    zbuf[pl.ds(CONV_HALO + r0, n), :] = c_g * v_c
    zext = zbuf[pl.ds(r0, n + CONV_HALO), :]
    z1 = pltpu.roll(zext, 1, axis=0)[CONV_HALO:, :]
    z2 = pltpu.roll(zext, 2, axis=0)[CONV_HALO:, :]
    conv = z2 * conv_w[0:1, :] + z1 * conv_w[1:2, :] + zext[CONV_HALO:, :] * conv_w[2:3, :] + conv_b[...]
    y_conv = b_g * conv

    vbuf[pl.ds(POOL_HALO + r0, n), :] = v_p
    vext = vbuf[pl.ds(r0, n + POOL_HALO), :]
    s2 = vext + pltpu.roll(vext, 1, axis=0)
    s4 = s2[:, POOL_GW:] + pltpu.roll(s2[:, POOL_GW:], 2, axis=0)
    s8 = s4[:, POOL_GW:] + pltpu.roll(s4[:, POOL_GW:], 4, axis=0)
    s16 = s8[:, POOL_GW:] + pltpu.roll(s8[:, POOL_GW:], 8, axis=0)
    wsums = (s2[POOL_HALO:, 0:POOL_GW], s4[POOL_HALO:, 0:POOL_GW],
             s8[POOL_HALO:, 0:POOL_GW], s16[POOL_HALO:, 0:POOL_GW])

    t_pos = seq0 + lax.broadcasted_iota(jnp.int32, (n, 1), 0)
    y_pool = []
    for j, w in enumerate(POOL_WINDOWS):
        inv_cnt = 1.0 / jnp.minimum(t_pos + 1, w).astype(jnp.float32)
        pooled = wsums[j] * inv_cnt - v_p[:, j * POOL_GW:(j + 1) * POOL_GW]
        y_pool.append(jnp.dot(pooled.astype(jnp.bfloat16), pool_w[j], preferred_element_type=jnp.float32))
    y_pool = jnp.concatenate(y_pool, axis=-1) * pool_scale[...]

    ycat = jnp.concatenate([y_conv, y_pool], axis=-1).astype(jnp.bfloat16)
    yield
    mix = jnp.dot(ycat, w_out[...], preferred_element_type=jnp.float32)
    yield
    h1 = _layernorm(DEEPNORM_ALPHA * h0 + mix, ln1_g[...], ln1_b[...])

    h_hi = h1.astype(jnp.bfloat16)
    h_lo = (h1 - h_hi.astype(jnp.float32)).astype(jnp.bfloat16)
    xp_ref[rows, :] = _pack_bf16_pairs(h1)
    h1_ref[rows, :] = h1
    hcat = jnp.concatenate([h_hi, h_lo, h_hi], axis=-1)
    yield
    logits = jnp.dot(hcat, w_r[...], preferred_element_type=jnp.float32) + b_r[...]
    yield
    lane = lax.broadcasted_iota(jnp.int32, (n, LANES), 1)
    neg = jnp.float32(-jnp.inf)

    def first_argmax(vals):
        m = jnp.max(vals, axis=-1, keepdims=True)
        idx = jnp.min(jnp.where(vals == m, lane, LANES), axis=-1, keepdims=True)
        return m, idx

    g_mask = lane < N_GROUPS
    g_max, g_idx = first_argmax(jnp.where(g_mask, logits, neg))
    g_w = 1.0 / jnp.sum(jnp.where(g_mask, jnp.exp(logits - g_max), 0.0), axis=-1, keepdims=True)

    e_lo = N_GROUPS + EXPERTS_PER_GROUP * g_idx
    e_vals = jnp.where((lane >= e_lo) & (lane < e_lo + EXPERTS_PER_GROUP), logits, neg)
    m1, i1 = first_argmax(e_vals)
    m2, i2 = first_argmax(jnp.where(lane == i1, neg, e_vals))
    e21 = jnp.exp(m2 - m1)
    w1 = g_w / (1.0 + e21)
    w2 = g_w * e21 / (1.0 + e21)
    id1 = i1 - N_GROUPS
    id2 = i2 - N_GROUPS

    sel1 = lane == id1
    sel2 = lane == id2
    onehot = (sel1 | sel2).astype(jnp.float32)
    row = lax.broadcasted_iota(jnp.int32, (n, n), 0)
    col = lax.broadcasted_iota(jnp.int32, (n, n), 1)
    tri = (col < row).astype(jnp.bfloat16)
    before = jnp.dot(tri, onehot.astype(jnp.bfloat16), preferred_element_type=jnp.float32) + carry[...]
    rank1 = jnp.sum(jnp.where(sel1, before, 0.0), axis=-1, keepdims=True)
    rank2 = jnp.sum(jnp.where(sel2, before, 0.0), axis=-1, keepdims=True)
    carry[...] = carry[...] + jnp.sum(onehot, axis=0, keepdims=True)

    rec = jnp.zeros((n, LANES), jnp.float32)
    for k, val in ((R_ID1, id1.astype(jnp.float32)), (R_ID2, id2.astype(jnp.float32)), (R_W1, w1), (R_W2, w2),
                   (R_RANK1, rank1), (R_RANK2, rank2)):
        rec = jnp.where(lane == k, val, rec)
    route_ref[rows, :] = rec
    route_t_ref[:, rows] = jnp.transpose(rec)[0:ROUTE_ROWS, :]


def _run_mixer(x2, lnin_g, lnin_b, w_in, conv_w, conv_b, pool_w, pool_scale, w_out, ln1_g, ln1_b,
               w_r, b_r, batch_lo, batch, seq):
    n_tok = batch * seq
    n_s = seq // SEQ_TILE
    tok_map = lambda b, s: (b * n_s + s, 0)
    in_map = lambda b, s: ((batch_lo + b) * n_s + s, 0)

    def const(shape):
        return pl.BlockSpec(shape, lambda b, s: (0,) * len(shape), pipeline_mode=pl.Buffered(1))

    in_specs = [
        pl.BlockSpec((SEQ_TILE, D_MODEL), in_map),
        const((1, D_MODEL)), const((1, D_MODEL)),
        const((D_MODEL, IN_PROJ)),
        const((CONV_K, CONV_WIDTH)), const((1, CONV_WIDTH)),
        const((len(POOL_WINDOWS), POOL_GW, POOL_GW)), const((1, POOL_WIDTH)),
        const((D_MODEL, D_MODEL)),
        const((1, D_MODEL)), const((1, D_MODEL)),
        const((3 * D_MODEL, LANES)), const((1, LANES)),
    ]
    out_specs = [
        pl.BlockSpec((SEQ_TILE, HALF), tok_map),
        pl.BlockSpec((SEQ_TILE, D_MODEL), tok_map),
        pl.BlockSpec((SEQ_TILE, LANES), tok_map),
        pl.BlockSpec((ROUTE_ROWS, SEQ_TILE), lambda b, s: (0, b * n_s + s)),
        pl.BlockSpec((1, LANES), lambda b, s: (0, 0)),
    ]
    out_shape = [
        jax.ShapeDtypeStruct((n_tok, HALF), jnp.uint32),
        jax.ShapeDtypeStruct((n_tok, D_MODEL), jnp.float32),
        jax.ShapeDtypeStruct((n_tok, LANES), jnp.float32),
        jax.ShapeDtypeStruct((ROUTE_ROWS, n_tok), jnp.float32),
        jax.ShapeDtypeStruct((1, LANES), jnp.float32),
    ]
    return pl.pallas_call(
        _mixer_kernel,
        grid=(batch, n_s),
        in_specs=in_specs,
        out_specs=out_specs,
        out_shape=out_shape,
        scratch_shapes=[
            pltpu.VMEM((SEQ_TILE + CONV_HALO, CONV_WIDTH), jnp.float32),
            pltpu.VMEM((SEQ_TILE + POOL_HALO, POOL_WIDTH), jnp.float32),
            pltpu.VMEM((1, LANES), jnp.float32),
        ],
        compiler_params=pltpu.CompilerParams(
            dimension_semantics=("arbitrary", "arbitrary"), vmem_limit_bytes=VMEM_LIMIT),
    )(x2, lnin_g, lnin_b, w_in, conv_w, conv_b, pool_w, pool_scale, w_out, ln1_g, ln1_b, w_r, b_r)


def _plan_kernel(rt_ref, counts_ref, pos_ref, tile_start_ref, tiles_ref, pad_ref, *, n_rows):
    lane = lax.broadcasted_iota(jnp.int32, (ROUTE_ROWS, LANES), 1)
    counts = jnp.broadcast_to(counts_ref[...], (ROUTE_ROWS, LANES))
    tiles = jnp.floor((counts + (ROW_TILE - 1)) * (1.0 / ROW_TILE))
    tile_end = tiles
    shift = 1
    while shift < N_EXPERTS:
        tile_end = tile_end + jnp.where(lane >= shift, pltpu.roll(tile_end, shift, axis=1), 0.0)
        shift *= 2
    row_start = (tile_end - tiles) * ROW_TILE
    tile_start_ref[...] = (tile_end - tiles)[0:1, :].astype(jnp.int32)
    tiles_ref[...] = tiles[0:1, :].astype(jnp.int32)

    rt = rt_ref[...]
    ids = rt[R_ID1:R_ID2 + 1, :]
    start = jnp.zeros_like(ids)
    for e in range(N_EXPERTS):
        start = jnp.where(ids == e, row_start[0:1, e:e + 1], start)
    pos_ref[...] = (start + rt[R_RANK1:R_RANK2 + 1, :]).astype(jnp.int32)

    sub = lax.broadcasted_iota(jnp.int32, (N_EXPERTS, LANES), 0)
    lane_e = lax.broadcasted_iota(jnp.int32, (N_EXPERTS, LANES), 1)
    diag = sub == lane_e
    pad_lo = jnp.sum(jnp.where(diag, (row_start + counts)[0:1, :], 0.0), axis=1, keepdims=True)
    pad_n = jnp.sum(jnp.where(diag, (tiles * ROW_TILE - counts)[0:1, :], 0.0), axis=1, keepdims=True)
    j = lax.broadcasted_iota(jnp.int32, (N_EXPERTS, ROW_TILE), 1).astype(jnp.float32)
    pad_ref[...] = jnp.where(j < pad_n, pad_lo + j, n_rows + j).astype(jnp.int32)


def _run_plan(route_t, counts, n_rows):
    n_tok = route_t.shape[1]
    full = lambda shape: pl.BlockSpec(shape, lambda i: (0, 0))
    return pl.pallas_call(
        functools.partial(_plan_kernel, n_rows=n_rows),
        grid=(1,),
        in_specs=[full((ROUTE_ROWS, n_tok)), full((1, LANES))],
        out_specs=[full((2, n_tok)), full((1, LANES)), full((1, LANES)), full((N_EXPERTS, ROW_TILE))],
        out_shape=[
            jax.ShapeDtypeStruct((2, n_tok), jnp.int32),
            jax.ShapeDtypeStruct((1, LANES), jnp.int32),
            jax.ShapeDtypeStruct((1, LANES), jnp.int32),
            jax.ShapeDtypeStruct((N_EXPERTS, ROW_TILE), jnp.int32),
        ],
        compiler_params=pltpu.CompilerParams(dimension_semantics=("arbitrary",)),
    )(route_t, counts)


def _sc_mesh():
    return plsc.VectorSubcoreMesh(core_axis_name="core", subcore_axis_name="subcore")


def _sc_worker_id():
    return lax.axis_index("core") * (SC_WORKERS // 2) + lax.axis_index("subcore")


def _dispatch_rows(xp, pos1, pos2, pad_pos, zero_rows, n_rows):
    n_tok, width = xp.shape
    n_win = n_tok // SC_WORKERS // SC_WINDOW
    n_pad = pad_pos.shape[0] // SC_WORKERS // SC_WINDOW
    as_windows = lambda v: v.reshape(-1, SC_WINDOW)

    @functools.partial(
        pl.kernel, out_type=jax.ShapeDtypeStruct((n_rows + ROW_TILE, width), xp.dtype), mesh=_sc_mesh(),
        scratch_types=[pltpu.VMEM((n_win, SC_WINDOW), jnp.int32), pltpu.VMEM((n_win, SC_WINDOW), jnp.int32),
                       pltpu.VMEM((n_pad, SC_WINDOW), jnp.int32),
                       pltpu.VMEM((2, SC_WINDOW, width), xp.dtype),
                       pltpu.SemaphoreType.DMA((2,)), pltpu.SemaphoreType.DMA((2,))])
    def dispatch(xp_hbm, pos1_hbm, pos2_hbm, pad_hbm, zero_hbm, out_hbm, idx1, idx2, idxp, buf, lsem, ssem):
        wid = _sc_worker_id()
        pltpu.sync_copy(pos1_hbm.at[pl.ds(wid * n_win, n_win)], idx1)
        pltpu.sync_copy(pos2_hbm.at[pl.ds(wid * n_win, n_win)], idx2)
        pltpu.sync_copy(pad_hbm.at[pl.ds(wid * n_pad, n_pad)], idxp)

        def load(j):
            rows = xp_hbm.at[pl.ds((wid * n_win + j) * SC_WINDOW, SC_WINDOW)]
            return pltpu.make_async_copy(rows, buf.at[j % 2], lsem.at[j % 2])

        def scatters(j):
            return [pltpu.make_async_copy(buf.at[j % 2], out_hbm.at[idx.at[j]], ssem.at[j % 2])
                    for idx in (idx1, idx2)]

        load(0).start()
        for j in range(n_win):
            load(j).wait()
            for cp in scatters(j):
                cp.start()
            if j >= 1:
                for cp in scatters(j - 1):
                    cp.wait()
            if j + 1 < n_win:
                load(j + 1).start()
        for cp in scatters(n_win - 1):
            cp.wait()

        pltpu.sync_copy(zero_hbm, buf.at[0])
        pads = [pltpu.make_async_copy(buf.at[0], out_hbm.at[idxp.at[j]], ssem.at[0]) for j in range(n_pad)]
        for cp in pads:
            cp.start()
        for cp in pads:
            cp.wait()

    return dispatch(xp, as_windows(pos1), as_windows(pos2), as_windows(pad_pos), zero_rows)


def _gather_rows(src, idx):
    n_out, width = idx.shape[0], src.shape[1]
    n_win = n_out // SC_WORKERS // SC_WINDOW

    @functools.partial(
        pl.kernel, out_type=jax.ShapeDtypeStruct((n_out, width), src.dtype), mesh=_sc_mesh(),
        scratch_types=[pltpu.VMEM((n_win, SC_WINDOW), jnp.int32), pltpu.VMEM((2, SC_WINDOW, width), src.dtype),
                       pltpu.SemaphoreType.DMA((2,)), pltpu.SemaphoreType.DMA((2,))])
    def gather(src_hbm, idx_hbm, dst_hbm, idx_v, buf, gsem, ssem):
        wid = _sc_worker_id()
        pltpu.sync_copy(idx_hbm.at[pl.ds(wid * n_win, n_win)], idx_v)

        def fetch(j):
            return pltpu.make_async_copy(src_hbm.at[idx_v.at[j]], buf.at[j % 2], gsem.at[j % 2])

        def store(j):
            rows = dst_hbm.at[pl.ds((wid * n_win + j) * SC_WINDOW, SC_WINDOW)]
            return pltpu.make_async_copy(buf.at[j % 2], rows, ssem.at[j % 2])

        fetch(0).start()
        for j in range(n_win):
            fetch(j).wait()
            store(j).start()
            if j >= 1:
                store(j - 1).wait()
            if j + 1 < n_win:
                fetch(j + 1).start()
        store(n_win - 1).wait()

    return gather(src, idx.reshape(-1, SC_WINDOW))


def _tile_copy(hbm, buf, sem, tile, slot, to_hbm):
    rows = hbm.at[pl.ds(pl.multiple_of(tile * ROW_TILE, ROW_TILE), ROW_TILE)]
    if to_hbm:
        return pltpu.make_async_copy(buf.at[slot], rows, sem.at[slot])
    return pltpu.make_async_copy(rows, buf.at[slot], sem.at[slot])


def _weight_copies(w_hbm, wbuf, wsem, expert):
    slot = expert % WEIGHT_BUFFERS
    return [pltpu.make_async_copy(w.at[expert], buf.at[slot], wsem.at[slot]) for w, buf in zip(w_hbm, wbuf)]


def _expert_kernel(ts_ref, nte_ref, x_hbm, wg_hbm, wu_hbm, wd_hbm, y_hbm,
                   xbuf, ybuf, xsem, ysem, wg_buf, wu_buf, wd_buf, wsem, wgu_bf, wd_bf, act_ref, *, max_tiles):
    e = pl.program_id(0)
    first = ts_ref[e]
    count = nte_ref[e]
    n_tiles = ts_ref[N_EXPERTS - 1] + nte_ref[N_EXPERTS - 1]
    ahead = TILE_BUFFERS - 1
    w_hbm = (wg_hbm, wu_hbm, wd_hbm)
    wbuf = (wg_buf, wu_buf, wd_buf)

    @pl.when(e == 0)
    def _():
        for g in range(ahead):
            @pl.when(g < n_tiles)
            def _():
                _tile_copy(x_hbm, xbuf, xsem, g, g, False).start()
        for k in range(WEIGHT_BUFFERS - 1):
            for cp in _weight_copies(w_hbm, wbuf, wsem, k):
                cp.start()

    @pl.when(e + WEIGHT_BUFFERS - 1 < N_EXPERTS)
    def _():
        for cp in _weight_copies(w_hbm, wbuf, wsem, e + WEIGHT_BUFFERS - 1):
            cp.start()

    for cp in _weight_copies(w_hbm, wbuf, wsem, e):
        cp.wait()

    def sync_x(g):
        _tile_copy(x_hbm, xbuf, xsem, g, g % TILE_BUFFERS, False).wait()

        @pl.when(g + ahead < n_tiles)
        def _():
            _tile_copy(x_hbm, xbuf, xsem, g + ahead, (g + ahead) % TILE_BUFFERS, False).start()

    def sync_y_slot(g):
        @pl.when(g >= TILE_BUFFERS)
        def _():
            _tile_copy(y_hbm, ybuf, ysem, g - TILE_BUFFERS, g % TILE_BUFFERS, True).wait()

    def up_proj(g):
        lo, hi = _unpack_bf16_pairs(xbuf[g % TILE_BUFFERS])
        return (jnp.dot(lo.astype(jnp.bfloat16), wgu_bf[0:HALF, :], preferred_element_type=jnp.float32)
                + jnp.dot(hi.astype(jnp.bfloat16), wgu_bf[HALF:, :], preferred_element_type=jnp.float32))

    def put_act(hgu):
        hg = hgu[:, 0:D_EXPERT]
        act_ref[...] = (hg * jax.nn.sigmoid(hg) * hgu[:, D_EXPERT:]).astype(jnp.bfloat16)

    def down_proj(g):
        y = jnp.dot(act_ref[...], wd_bf[...], preferred_element_type=jnp.float32)
        ybuf[g % TILE_BUFFERS] = _pack_bf16_pairs(y)

    @pl.when(count > 0)
    def _():
        slot = e % WEIGHT_BUFFERS
        wgu_bf[:, 0:D_EXPERT] = wg_buf[slot].astype(jnp.bfloat16)
        wgu_bf[:, D_EXPERT:] = wu_buf[slot].astype(jnp.bfloat16)
        wd_bf[...] = wd_buf[slot].astype(jnp.bfloat16)

        sync_x(first)
        put_act(up_proj(first))

        def tile_body(g, c):
            sync_x(g)
            sync_y_slot(g - 1)
            down_proj(g - 1)
            hgu = up_proj(g)
            put_act(hgu)
            _tile_copy(y_hbm, ybuf, ysem, g - 1, (g - 1) % TILE_BUFFERS, True).start()
            return c

        lax.fori_loop(first + 1, first + count, tile_body, 0)
        last = first + count - 1
        sync_y_slot(last)
        down_proj(last)
        _tile_copy(y_hbm, ybuf, ysem, last, last % TILE_BUFFERS, True).start()

    @pl.when(e == N_EXPERTS - 1)
    def _():
        for k in range(TILE_BUFFERS, 0, -1):
            @pl.when(n_tiles >= k)
            def _():
                _tile_copy(y_hbm, ybuf, ysem, n_tiles - k, (n_tiles - k) % TILE_BUFFERS, True).wait()

        ybuf[0] = jnp.zeros((ROW_TILE, HALF), jnp.uint32)

        def fill(g, c):
            cp = _tile_copy(y_hbm, ybuf, ysem, g, 0, True)
            cp.start()
            cp.wait()
            return c

        lax.fori_loop(n_tiles, max_tiles, fill, 0)


def _run_experts(tile_start, tiles_per_expert, x_sorted, w_gate, w_up, w_down):
    n_rows = x_sorted.shape[0]
    max_tiles = n_rows // ROW_TILE
    hbm = pl.BlockSpec(memory_space=pl.ANY)

    grid_spec = pltpu.PrefetchScalarGridSpec(
        num_scalar_prefetch=2,
        grid=(N_EXPERTS,),
        in_specs=[hbm, hbm, hbm, hbm],
        out_specs=hbm,
        scratch_shapes=[
            pltpu.VMEM((TILE_BUFFERS, ROW_TILE, HALF), jnp.uint32),
            pltpu.VMEM((TILE_BUFFERS, ROW_TILE, HALF), jnp.uint32),
            pltpu.SemaphoreType.DMA((TILE_BUFFERS,)),
            pltpu.SemaphoreType.DMA((TILE_BUFFERS,)),
            pltpu.VMEM((WEIGHT_BUFFERS, D_MODEL, D_EXPERT), jnp.float32),
            pltpu.VMEM((WEIGHT_BUFFERS, D_MODEL, D_EXPERT), jnp.float32),
            pltpu.VMEM((WEIGHT_BUFFERS, D_EXPERT, D_MODEL), jnp.float32),
            pltpu.SemaphoreType.DMA((WEIGHT_BUFFERS,)),
            pltpu.VMEM((D_MODEL, 2 * D_EXPERT), jnp.bfloat16),
            pltpu.VMEM((D_EXPERT, D_MODEL), jnp.bfloat16),
            pltpu.VMEM((ROW_TILE, D_EXPERT), jnp.bfloat16),
        ],
    )
    return pl.pallas_call(
        functools.partial(_expert_kernel, max_tiles=max_tiles),
        grid_spec=grid_spec,
        out_shape=jax.ShapeDtypeStruct((n_rows, HALF), jnp.uint32),
        compiler_params=pltpu.CompilerParams(
            dimension_semantics=("arbitrary",), vmem_limit_bytes=VMEM_LIMIT),
    )(tile_start, tiles_per_expert, x_sorted, w_gate, w_up, w_down)


def _combine_rows(r0, h1_ref, p_ref, route_ref, ya_ref, yb_ref, w_pg, b_pg, w_ple, g_ref, b_ref, o_ref):
    rows = pl.ds(r0, SUB_TILE)
    h1 = h1_ref[rows, :]
    h_hi = h1.astype(jnp.bfloat16)
    p_b = p_ref[rows, :].astype(jnp.bfloat16)
    yield
    gate_pre = jnp.dot(h_hi, w_pg[...], preferred_element_type=jnp.float32)
    ple_pre = jnp.dot(p_b, w_ple[...], preferred_element_type=jnp.float32)
    yield
    ple = ple_pre * jax.nn.sigmoid(gate_pre + b_pg[...])
    route = route_ref[rows, :]
    w1 = route[:, R_W1:R_W1 + 1]
    w2 = route[:, R_W2:R_W2 + 1]
    a_lo, a_hi = _unpack_bf16_pairs(ya_ref[rows, :])
    b_lo, b_hi = _unpack_bf16_pairs(yb_ref[rows, :])
    moe = jnp.concatenate([w1 * a_lo + w2 * b_lo, w1 * a_hi + w2 * b_hi], axis=-1)
    o_ref[rows, :] = _layernorm(DEEPNORM_ALPHA * h1 + ple + moe, g_ref[...], b_ref[...])


def _combine_kernel(h1_ref, p_ref, route_ref, ya_ref, yb_ref, w_pg, b_pg, w_ple, g_ref, b_ref, *rest):
    o_ref = rest[-1]
    chains = [_combine_rows(r0, h1_ref, p_ref, route_ref, ya_ref, yb_ref, w_pg, b_pg, w_ple, g_ref, b_ref, o_ref)
              for r0 in range(0, h1_ref.shape[0], SUB_TILE)]
    for t in range(COMBINE_PHASES + len(chains) - 1):
        for k, chain in enumerate(chains):
            if 0 <= t - k < COMBINE_PHASES:
                next(chain, None)


def _run_combine(h1, p2, tok_lo, route, y_tok, w_pg, b_pg, w_ple, ln2_g, ln2_b, out_prev, chunk, n_tok_total):
    n_tok = h1.shape[0]
    n_t = n_tok // TOKEN_TILE
    p_lo = tok_lo // TOKEN_TILE
    const = lambda shape: pl.BlockSpec(shape, lambda i: (0, 0), pipeline_mode=pl.Buffered(1))
    in_specs = [
        pl.BlockSpec((TOKEN_TILE, D_MODEL), lambda i: (i, 0)),
        pl.BlockSpec((TOKEN_TILE, PLE_DIM), lambda i: (p_lo + i, 0)),
        pl.BlockSpec((TOKEN_TILE, LANES), lambda i: (i, 0)),
        pl.BlockSpec((TOKEN_TILE, HALF), lambda i: (i, 0)),
        pl.BlockSpec((TOKEN_TILE, HALF), lambda i: (i + n_t, 0)),
        const((D_MODEL, D_MODEL)), const((1, D_MODEL)), const((PLE_DIM, D_MODEL)),
        const((1, D_MODEL)), const((1, D_MODEL)),
    ]
    args = [h1, p2, route, y_tok, y_tok, w_pg, b_pg, w_ple, ln2_g, ln2_b]
    aliases = {}
    if out_prev is not None:
        in_specs.append(pl.BlockSpec(memory_space=pl.ANY))
        args.append(out_prev)
        aliases = {len(args) - 1: 0}
    return pl.pallas_call(
        _combine_kernel,
        grid=(n_t,),
        in_specs=in_specs,
        out_specs=pl.BlockSpec((TOKEN_TILE, D_MODEL), lambda i: (chunk * n_t + i, 0)),
        out_shape=jax.ShapeDtypeStruct((n_tok_total, D_MODEL), jnp.float32),
        input_output_aliases=aliases,
        compiler_params=pltpu.CompilerParams(dimension_semantics=("arbitrary",), vmem_limit_bytes=VMEM_LIMIT),
    )(*args)


def _split_bf16(w):
    hi = w.astype(jnp.bfloat16)
    lo = (w - hi.astype(jnp.float32)).astype(jnp.bfloat16)
    return hi, lo


def kernel(x, p, ln_in_g, ln_in_b, w_in, conv_w, conv_b, pool_w, pool_scale, w_out, ln1_g, ln1_b,
           w_rg, b_rg, w_re, b_re, w_gate, w_up, w_down, w_pg, b_pg, w_ple, ln2_g, ln2_b):
    batch, seq, _ = x.shape
    n_tok = batch * seq
    bf = jnp.bfloat16
    row = lambda v: v.reshape(1, -1)

    w_r = jnp.concatenate([w_rg[0], jnp.transpose(w_re[0], (1, 0, 2)).reshape(D_MODEL, N_EXPERTS)], axis=1)
    w_r = jnp.pad(w_r, ((0, 0), (0, LANES - w_r.shape[1])))
    w_r_hi, w_r_lo = _split_bf16(w_r)
    w_r_cat = jnp.concatenate([w_r_hi, w_r_hi, w_r_lo], axis=0)
    b_r = jnp.pad(jnp.concatenate([b_rg[0], b_re[0].reshape(-1)]), (0, LANES - N_GROUPS - N_EXPERTS)).reshape(1, LANES)

    x2 = x.reshape(n_tok, D_MODEL)
    p2 = p[0].reshape(n_tok, PLE_DIM)
    mixer_weights = (row(ln_in_g), row(ln_in_b), w_in[0].astype(bf), conv_w[0], row(conv_b[0]),
                     pool_w[0].astype(bf), row(pool_scale[0]), w_out[0].astype(bf), row(ln1_g[0]), row(ln1_b[0]),
                     w_r_cat, b_r)
    combine_weights = (w_pg[0].astype(bf), row(b_pg[0]), w_ple[0].astype(bf), row(ln2_g[0]), row(ln2_b[0]))
    expert_weights = (w_gate[0].reshape(N_EXPERTS, D_MODEL, D_EXPERT),
                      w_up[0].reshape(N_EXPERTS, D_MODEL, D_EXPERT),
                      w_down[0].reshape(N_EXPERTS, D_EXPERT, D_MODEL))
    zero_rows = jnp.zeros((SC_WINDOW, HALF), jnp.uint32)

    n_chunks = N_CHUNKS if batch % N_CHUNKS == 0 else 1
    chunk_batch = batch // n_chunks
    chunk_tok = chunk_batch * seq
    n_rows = -(-(2 * chunk_tok + N_EXPERTS * (ROW_TILE - 1)) // ROW_TILE) * ROW_TILE
    out = None
    for c in range(n_chunks):
        xp, h1, route, route_t, counts = _run_mixer(x2, *mixer_weights, c * chunk_batch, chunk_batch, seq)
        pos, tile_start, tiles_per_expert, pad_pos = _run_plan(route_t, counts, n_rows)
        x_sorted = _dispatch_rows(xp, pos[0], pos[1], pad_pos.reshape(-1), zero_rows, n_rows)
        y_sorted = _run_experts(tile_start[0, :N_EXPERTS], tiles_per_expert[0, :N_EXPERTS], x_sorted,
                                *expert_weights)
        y_tok = _gather_rows(y_sorted, pos.reshape(-1))
        out = _run_combine(h1, p2, c * chunk_tok, route, y_tok, *combine_weights, out, c, n_tok)
    return out.reshape(batch, seq, D_MODEL)
```

```python
import functools

import jax
import jax.numpy as jnp
from jax import lax
from jax.experimental import pallas as pl
from jax.experimental.pallas import tpu as pltpu
from jax.experimental.pallas import tpu_sc as plsc

D_MODEL = 1024
CONV_WIDTH = 512
CONV_K = 3
POOL_WIDTH = 512
POOL_WINDOWS = (2, 4, 8, 16)
POOL_GW = 128
IN_PROJ = 3 * CONV_WIDTH + POOL_WIDTH
N_GROUPS = 4
EXPERTS_PER_GROUP = 8
N_EXPERTS = N_GROUPS * EXPERTS_PER_GROUP
D_EXPERT = 256
PLE_DIM = 256
LN_EPS = 1e-5
DEEPNORM_ALPHA = 2.0 ** 0.25

LANES = 128
HALF = D_MODEL // 2
CONV_HALO = 8
POOL_HALO = 16
SEQ_TILE = 1024
SUB_TILE = 256
MIXER_PHASES = 7
COMBINE_PHASES = 3
ROW_TILE = 256
TILE_BUFFERS = 8
N_CHUNKS = 1
SC_WORKERS = 32
SC_WINDOW = 64
WEIGHT_BUFFERS = 3
TOKEN_TILE = 1024
VMEM_LIMIT = 56 * 1024 * 1024

R_ID1, R_ID2, R_RANK1, R_RANK2, R_W1, R_W2 = range(6)
ROUTE_ROWS = 8


def _layernorm(x, g, b):
    mu = jnp.mean(x, axis=-1, keepdims=True)
    xc = x - mu
    var = jnp.mean(xc * xc, axis=-1, keepdims=True)
    return xc * lax.rsqrt(var + LN_EPS) * g + b


def _pack_bf16_pairs(v):
    bits = lax.bitcast_convert_type(v.astype(jnp.bfloat16).astype(jnp.float32), jnp.uint32)
    return bits[:, HALF:] | (bits[:, :HALF] >> 16)


def _unpack_bf16_pairs(w):
    lo = lax.bitcast_convert_type(w << 16, jnp.float32)
    hi = lax.bitcast_convert_type(w & jnp.uint32(0xFFFF0000), jnp.float32)
    return lo, hi


def _mixer_kernel(x_ref, lnin_g, lnin_b, w_in, conv_w, conv_b, pool_w, pool_scale, w_out,
                  ln1_g, ln1_b, w_r, b_r,
                  xp_ref, h1_ref, route_ref, route_t_ref, counts_ref,
                  zbuf, vbuf, carry):
    b = pl.program_id(0)
    s = pl.program_id(1)
    ts = x_ref.shape[0]

    @pl.when(s == 0)
    def _():
        zbuf[0:CONV_HALO, :] = jnp.zeros((CONV_HALO, CONV_WIDTH), jnp.float32)
        vbuf[0:POOL_HALO, :] = jnp.zeros((POOL_HALO, POOL_WIDTH), jnp.float32)

    @pl.when((b == 0) & (s == 0))
    def _():
        carry[...] = jnp.zeros_like(carry)

    chains = [_mixer_rows(r0, s * ts + r0, x_ref, lnin_g, lnin_b, w_in, conv_w, conv_b, pool_w, pool_scale,
                          w_out, ln1_g, ln1_b, w_r, b_r,
                          xp_ref, h1_ref, route_ref, route_t_ref, zbuf, vbuf, carry)
              for r0 in range(0, ts, SUB_TILE)]
    for t in range(MIXER_PHASES + len(chains) - 1):
        for k, chain in enumerate(chains):
            if 0 <= t - k < MIXER_PHASES:
                next(chain, None)
    zbuf[0:CONV_HALO, :] = zbuf[ts:ts + CONV_HALO, :]
    vbuf[0:POOL_HALO, :] = vbuf[ts:ts + POOL_HALO, :]
    counts_ref[...] = carry[...]


def _mixer_rows(r0, seq0, x_ref, lnin_g, lnin_b, w_in, conv_w, conv_b, pool_w, pool_scale, w_out,
                ln1_g, ln1_b, w_r, b_r,
                xp_ref, h1_ref, route_ref, route_t_ref, zbuf, vbuf, carry):
    n = SUB_TILE
    rows = pl.ds(r0, n)
    h0 = _layernorm(x_ref[rows, :], lnin_g[...], lnin_b[...])
    h0b = h0.astype(jnp.bfloat16)
    yield
    u = jnp.dot(h0b, w_in[...], preferred_element_type=jnp.float32)
    yield
    b_g = u[:, 0:CONV_WIDTH]
    c_g = u[:, CONV_WIDTH:2 * CONV_WIDTH]
    v_c = u[:, 2 * CONV_WIDTH:3 * CONV_WIDTH]
    v_p = u[:, 3 * CONV_WIDTH:]

    zbuf[pl.ds(CONV_HALO + r0, n), :] = c_g * v_c
    zext = zbuf[pl.ds(r0, n + CONV_HALO), :]
    z1 = pltpu.roll(zext, 1, axis=0)[CONV_HALO:, :]
    z2 = pltpu.roll(zext, 2, axis=0)[CONV_HALO:, :]
    conv = z2 * conv_w[0:1, :] + z1 * conv_w[1:2, :] + zext[CONV_HALO:, :] * conv_w[2:3, :] + conv_b[...]
    y_conv = b_g * conv

    vbuf[pl.ds(POOL_HALO + r0, n), :] = v_p
    vext = vbuf[pl.ds(r0, n + POOL_HALO), :]
    s2 = vext + pltpu.roll(vext, 1, axis=0)
    s4 = s2[:, POOL_GW:] + pltpu.roll(s2[:, POOL_GW:], 2, axis=0)
    s8 = s4[:, POOL_GW:] + pltpu.roll(s4[:, POOL_GW:], 4, axis=0)
    s16 = s8[:, POOL_GW:] + pltpu.roll(s8[:, POOL_GW:], 8, axis=0)
    wsums = (s2[POOL_HALO:, 0:POOL_GW], s4[POOL_HALO:, 0:POOL_GW],
             s8[POOL_HALO:, 0:POOL_GW], s16[POOL_HALO:, 0:POOL_GW])

    t_pos = seq0 + lax.broadcasted_iota(jnp.int32, (n, 1), 0)
    y_pool = []
    for j, w in enumerate(POOL_WINDOWS):
        inv_cnt = 1.0 / jnp.minimum(t_pos + 1, w).astype(jnp.float32)
        pooled = wsums[j] * inv_cnt - v_p[:, j * POOL_GW:(j + 1) * POOL_GW]
        y_pool.append(jnp.dot(pooled.astype(jnp.bfloat16), pool_w[j], preferred_element_type=jnp.float32))
    y_pool = jnp.concatenate(y_pool, axis=-1) * pool_scale[...]

    ycat = jnp.concatenate([y_conv, y_pool], axis=-1).astype(jnp.bfloat16)
    yield
    mix = jnp.dot(ycat, w_out[...], preferred_element_type=jnp.float32)
    yield
    h1 = _layernorm(DEEPNORM_ALPHA * h0 + mix, ln1_g[...], ln1_b[...])

    h_hi = h1.astype(jnp.bfloat16)
    h_lo = (h1 - h_hi.astype(jnp.float32)).astype(jnp.bfloat16)
    xp_ref[rows, :] = _pack_bf16_pairs(h1)
    h1_ref[rows, :] = h1
    hcat = jnp.concatenate([h_hi, h_lo, h_hi], axis=-1)
    yield
    logits = jnp.dot(hcat, w_r[...], preferred_element_type=jnp.float32) + b_r[...]
    yield
    lane = lax.broadcasted_iota(jnp.int32, (n, LANES), 1).astype(jnp.float32)
    neg = jnp.float32(-jnp.inf)

    def first_argmax(vals):
        m = jnp.max(vals, axis=-1, keepdims=True)
        idx = jnp.min(jnp.where(vals == m, lane, float(LANES)), axis=-1, keepdims=True)
        return m, idx

    g_mask = lane < N_GROUPS
    g_max, g_idx = first_argmax(jnp.where(g_mask, logits, neg))
    g_w = 1.0 / jnp.sum(jnp.where(g_mask, jnp.exp(logits - g_max), 0.0), axis=-1, keepdims=True)

    e_lo = N_GROUPS + EXPERTS_PER_GROUP * g_idx
    e_vals = jnp.where((lane >= e_lo) & (lane < e_lo + EXPERTS_PER_GROUP), logits, neg)
    m1, i1 = first_argmax(e_vals)
    m2, i2 = first_argmax(jnp.where(lane == i1, neg, e_vals))
    e21 = jnp.exp(m2 - m1)
    w1 = g_w / (1.0 + e21)
    w2 = g_w * e21 / (1.0 + e21)
    id1 = i1 - N_GROUPS
    id2 = i2 - N_GROUPS

    sel1 = lane == id1
    sel2 = lane == id2
    onehot = (sel1 | sel2).astype(jnp.float32)
    row = lax.broadcasted_iota(jnp.int32, (n, n), 0)
    col = lax.broadcasted_iota(jnp.int32, (n, n), 1)
    tri = (col < row).astype(jnp.bfloat16)
    before = jnp.dot(tri, onehot.astype(jnp.bfloat16), preferred_element_type=jnp.float32) + carry[...]
    rank1 = jnp.sum(jnp.where(sel1, before, 0.0), axis=-1, keepdims=True)
    rank2 = jnp.sum(jnp.where(sel2, before, 0.0), axis=-1, keepdims=True)
    carry[...] = carry[...] + jnp.sum(onehot, axis=0, keepdims=True)

    rec = jnp.zeros((n, LANES), jnp.float32)
    for k, val in ((R_ID1, id1), (R_ID2, id2), (R_W1, w1), (R_W2, w2), (R_RANK1, rank1), (R_RANK2, rank2)):
        rec = jnp.where(lane == k, val, rec)
    route_ref[rows, :] = rec
    route_t_ref[:, rows] = jnp.transpose(rec)[0:ROUTE_ROWS, :]


def _run_mixer(x2, lnin_g, lnin_b, w_in, conv_w, conv_b, pool_w, pool_scale, w_out, ln1_g, ln1_b,
               w_r, b_r, batch_lo, batch, seq):
    n_tok = batch * seq
    n_s = seq // SEQ_TILE
    tok_map = lambda b, s: (b * n_s + s, 0)
    in_map = lambda b, s: ((batch_lo + b) * n_s + s, 0)

    def const(shape):
        return pl.BlockSpec(shape, lambda b, s: (0,) * len(shape), pipeline_mode=pl.Buffered(1))

    in_specs = [
        pl.BlockSpec((SEQ_TILE, D_MODEL), in_map),
        const((1, D_MODEL)), const((1, D_MODEL)),
        const((D_MODEL, IN_PROJ)),
        const((CONV_K, CONV_WIDTH)), const((1, CONV_WIDTH)),
        const((len(POOL_WINDOWS), POOL_GW, POOL_GW)), const((1, POOL_WIDTH)),
        const((D_MODEL, D_MODEL)),
        const((1, D_MODEL)), const((1, D_MODEL)),
        const((3 * D_MODEL, LANES)), const((1, LANES)),
    ]
    out_specs = [
        pl.BlockSpec((SEQ_TILE, HALF), tok_map),
        pl.BlockSpec((SEQ_TILE, D_MODEL), tok_map),
        pl.BlockSpec((SEQ_TILE, LANES), tok_map),
        pl.BlockSpec((ROUTE_ROWS, SEQ_TILE), lambda b, s: (0, b * n_s + s)),
        pl.BlockSpec((1, LANES), lambda b, s: (0, 0)),
    ]
    out_shape = [
        jax.ShapeDtypeStruct((n_tok, HALF), jnp.uint32),
        jax.ShapeDtypeStruct((n_tok, D_MODEL), jnp.float32),
        jax.ShapeDtypeStruct((n_tok, LANES), jnp.float32),
        jax.ShapeDtypeStruct((ROUTE_ROWS, n_tok), jnp.float32),
        jax.ShapeDtypeStruct((1, LANES), jnp.float32),
    ]
    return pl.pallas_call(
        _mixer_kernel,
        grid=(batch, n_s),
        in_specs=in_specs,
        out_specs=out_specs,
        out_shape=out_shape,
        scratch_shapes=[
            pltpu.VMEM((SEQ_TILE + CONV_HALO, CONV_WIDTH), jnp.float32),
            pltpu.VMEM((SEQ_TILE + POOL_HALO, POOL_WIDTH), jnp.float32),
            pltpu.VMEM((1, LANES), jnp.float32),
        ],
        compiler_params=pltpu.CompilerParams(
            dimension_semantics=("arbitrary", "arbitrary"), vmem_limit_bytes=VMEM_LIMIT),
    )(x2, lnin_g, lnin_b, w_in, conv_w, conv_b, pool_w, pool_scale, w_out, ln1_g, ln1_b, w_r, b_r)


def _plan_kernel(rt_ref, counts_ref, pos_ref, tile_start_ref, tiles_ref, pad_ref, *, n_rows):
    lane = lax.broadcasted_iota(jnp.int32, (ROUTE_ROWS, LANES), 1)
    counts = jnp.broadcast_to(counts_ref[...], (ROUTE_ROWS, LANES))
    tiles = jnp.floor((counts + (ROW_TILE - 1)) * (1.0 / ROW_TILE))
    tile_end = tiles
    shift = 1
    while shift < N_EXPERTS:
        tile_end = tile_end + jnp.where(lane >= shift, pltpu.roll(tile_end, shift, axis=1), 0.0)
        shift *= 2
    row_start = (tile_end - tiles) * ROW_TILE
    tile_start_ref[...] = (tile_end - tiles)[0:1, :].astype(jnp.int32)
    tiles_ref[...] = tiles[0:1, :].astype(jnp.int32)

    rt = rt_ref[...]
    ids = rt[R_ID1:R_ID2 + 1, :]
    start = jnp.zeros_like(ids)
    for e in range(N_EXPERTS):
        start = jnp.where(ids == e, row_start[0:1, e:e + 1], start)
    pos_ref[...] = (start + rt[R_RANK1:R_RANK2 + 1, :]).astype(jnp.int32)

    sub = lax.broadcasted_iota(jnp.int32, (N_EXPERTS, LANES), 0)
    lane_e = lax.broadcasted_iota(jnp.int32, (N_EXPERTS, LANES), 1)
    diag = sub == lane_e
    pad_lo = jnp.sum(jnp.where(diag, (row_start + counts)[0:1, :], 0.0), axis=1, keepdims=True)
    pad_n = jnp.sum(jnp.where(diag, (tiles * ROW_TILE - counts)[0:1, :], 0.0), axis=1, keepdims=True)
    j = lax.broadcasted_iota(jnp.int32, (N_EXPERTS, ROW_TILE), 1).astype(jnp.float32)
    pad_ref[...] = jnp.where(j < pad_n, pad_lo + j, n_rows + j).astype(jnp.int32)


def _run_plan(route_t, counts, n_rows):
    n_tok = route_t.shape[1]
    full = lambda shape: pl.BlockSpec(shape, lambda i: (0, 0))
    return pl.pallas_call(
        functools.partial(_plan_kernel, n_rows=n_rows),
        grid=(1,),
        in_specs=[full((ROUTE_ROWS, n_tok)), full((1, LANES))],
        out_specs=[full((2, n_tok)), full((1, LANES)), full((1, LANES)), full((N_EXPERTS, ROW_TILE))],
        out_shape=[
            jax.ShapeDtypeStruct((2, n_tok), jnp.int32),
            jax.ShapeDtypeStruct((1, LANES), jnp.int32),
            jax.ShapeDtypeStruct((1, LANES), jnp.int32),
            jax.ShapeDtypeStruct((N_EXPERTS, ROW_TILE), jnp.int32),
        ],
        compiler_params=pltpu.CompilerParams(dimension_semantics=("arbitrary",)),
    )(route_t, counts)


def _sc_mesh():
    return plsc.VectorSubcoreMesh(core_axis_name="core", subcore_axis_name="subcore")


def _sc_worker_id():
    return lax.axis_index("core") * (SC_WORKERS // 2) + lax.axis_index("subcore")


def _dispatch_rows(xp, pos1, pos2, pad_pos, zero_rows, n_rows):
    n_tok, width = xp.shape
    n_win = n_tok // SC_WORKERS // SC_WINDOW
    n_pad = pad_pos.shape[0] // SC_WORKERS // SC_WINDOW
    as_windows = lambda v: v.reshape(-1, SC_WINDOW)

    @functools.partial(
        pl.kernel, out_type=jax.ShapeDtypeStruct((n_rows + ROW_TILE, width), xp.dtype), mesh=_sc_mesh(),
        scratch_types=[pltpu.VMEM((n_win, SC_WINDOW), jnp.int32), pltpu.VMEM((n_win, SC_WINDOW), jnp.int32),
                       pltpu.VMEM((n_pad, SC_WINDOW), jnp.int32),
                       pltpu.VMEM((2, SC_WINDOW, width), xp.dtype),
                       pltpu.SemaphoreType.DMA((2,)), pltpu.SemaphoreType.DMA((2,))])
    def dispatch(xp_hbm, pos1_hbm, pos2_hbm, pad_hbm, zero_hbm, out_hbm, idx1, idx2, idxp, buf, lsem, ssem):
        wid = _sc_worker_id()
        pltpu.sync_copy(pos1_hbm.at[pl.ds(wid * n_win, n_win)], idx1)
        pltpu.sync_copy(pos2_hbm.at[pl.ds(wid * n_win, n_win)], idx2)
        pltpu.sync_copy(pad_hbm.at[pl.ds(wid * n_pad, n_pad)], idxp)

        def load(j):
            rows = xp_hbm.at[pl.ds((wid * n_win + j) * SC_WINDOW, SC_WINDOW)]
            return pltpu.make_async_copy(rows, buf.at[j % 2], lsem.at[j % 2])

        def scatters(j):
            return [pltpu.make_async_copy(buf.at[j % 2], out_hbm.at[idx.at[j]], ssem.at[j % 2])
                    for idx in (idx1, idx2)]

        load(0).start()
        for j in range(n_win):
            load(j).wait()
            for cp in scatters(j):
                cp.start()
            if j >= 1:
                for cp in scatters(j - 1):
                    cp.wait()
            if j + 1 < n_win:
                load(j + 1).start()
        for cp in scatters(n_win - 1):
            cp.wait()

        pltpu.sync_copy(zero_hbm, buf.at[0])
        pads = [pltpu.make_async_copy(buf.at[0], out_hbm.at[idxp.at[j]], ssem.at[0]) for j in range(n_pad)]
        for cp in pads:
            cp.start()
        for cp in pads:
            cp.wait()

    return dispatch(xp, as_windows(pos1), as_windows(pos2), as_windows(pad_pos), zero_rows)


def _gather_rows(src, idx):
    n_out, width = idx.shape[0], src.shape[1]
    n_win = n_out // SC_WORKERS // SC_WINDOW

    @functools.partial(
        pl.kernel, out_type=jax.ShapeDtypeStruct((n_out, width), src.dtype), mesh=_sc_mesh(),
        scratch_types=[pltpu.VMEM((n_win, SC_WINDOW), jnp.int32), pltpu.VMEM((2, SC_WINDOW, width), src.dtype),
                       pltpu.SemaphoreType.DMA((2,)), pltpu.SemaphoreType.DMA((2,))])
    def gather(src_hbm, idx_hbm, dst_hbm, idx_v, buf, gsem, ssem):
        wid = _sc_worker_id()
        pltpu.sync_copy(idx_hbm.at[pl.ds(wid * n_win, n_win)], idx_v)

        def fetch(j):
            return pltpu.make_async_copy(src_hbm.at[idx_v.at[j]], buf.at[j % 2], gsem.at[j % 2])

        def store(j):
            rows = dst_hbm.at[pl.ds((wid * n_win + j) * SC_WINDOW, SC_WINDOW)]
            return pltpu.make_async_copy(buf.at[j % 2], rows, ssem.at[j % 2])

        fetch(0).start()
        for j in range(n_win):
            fetch(j).wait()
            store(j).start()
            if j >= 1:
                store(j - 1).wait()
            if j + 1 < n_win:
                fetch(j + 1).start()
        store(n_win - 1).wait()

    return gather(src, idx.reshape(-1, SC_WINDOW))


def _tile_copy(hbm, buf, sem, tile, slot, to_hbm):
    rows = hbm.at[pl.ds(pl.multiple_of(tile * ROW_TILE, ROW_TILE), ROW_TILE)]
    if to_hbm:
        return pltpu.make_async_copy(buf.at[slot], rows, sem.at[slot])
    return pltpu.make_async_copy(rows, buf.at[slot], sem.at[slot])


def _weight_copies(w_hbm, wbuf, wsem, expert):
    slot = expert % WEIGHT_BUFFERS
    return [pltpu.make_async_copy(w.at[expert], buf.at[slot], wsem.at[slot]) for w, buf in zip(w_hbm, wbuf)]


def _expert_kernel(ts_ref, nte_ref, x_hbm, wg_hbm, wu_hbm, wd_hbm, y_hbm,
                   xbuf, ybuf, xsem, ysem, wg_buf, wu_buf, wd_buf, wsem, wgu_bf, wd_bf, act_ref, *, max_tiles):
    e = pl.program_id(0)
    first = ts_ref[e]
    count = nte_ref[e]
    n_tiles = ts_ref[N_EXPERTS - 1] + nte_ref[N_EXPERTS - 1]
    ahead = TILE_BUFFERS - 1
    w_hbm = (wg_hbm, wu_hbm, wd_hbm)
    wbuf = (wg_buf, wu_buf, wd_buf)

    @pl.when(e == 0)
    def _():
        for g in range(ahead):
            @pl.when(g < n_tiles)
            def _():
                _tile_copy(x_hbm, xbuf, xsem, g, g, False).start()
        for k in range(WEIGHT_BUFFERS - 1):
            for cp in _weight_copies(w_hbm, wbuf, wsem, k):
                cp.start()

    @pl.when(e + WEIGHT_BUFFERS - 1 < N_EXPERTS)
    def _():
        for cp in _weight_copies(w_hbm, wbuf, wsem, e + WEIGHT_BUFFERS - 1):
            cp.start()

    for cp in _weight_copies(w_hbm, wbuf, wsem, e):
        cp.wait()

    def sync_x(g):
        _tile_copy(x_hbm, xbuf, xsem, g, g % TILE_BUFFERS, False).wait()

        @pl.when(g + ahead < n_tiles)
        def _():
            _tile_copy(x_hbm, xbuf, xsem, g + ahead, (g + ahead) % TILE_BUFFERS, False).start()

    def sync_y_slot(g):
        @pl.when(g >= TILE_BUFFERS)
        def _():
            _tile_copy(y_hbm, ybuf, ysem, g - TILE_BUFFERS, g % TILE_BUFFERS, True).wait()

    def up_proj(g):
        lo, hi = _unpack_bf16_pairs(xbuf[g % TILE_BUFFERS])
        return (jnp.dot(lo.astype(jnp.bfloat16), wgu_bf[0:HALF, :], preferred_element_type=jnp.float32)
                + jnp.dot(hi.astype(jnp.bfloat16), wgu_bf[HALF:, :], preferred_element_type=jnp.float32))

    def put_act(hgu):
        hg = hgu[:, 0:D_EXPERT]
        act_ref[...] = (hg * jax.nn.sigmoid(hg) * hgu[:, D_EXPERT:]).astype(jnp.bfloat16)

    def down_proj(g):
        y = jnp.dot(act_ref[...], wd_bf[...], preferred_element_type=jnp.float32)
        ybuf[g % TILE_BUFFERS] = _pack_bf16_pairs(y)

    @pl.when(count > 0)
    def _():
        slot = e % WEIGHT_BUFFERS
        wgu_bf[:, 0:D_EXPERT] = wg_buf[slot].astype(jnp.bfloat16)
        wgu_bf[:, D_EXPERT:] = wu_buf[slot].astype(jnp.bfloat16)
        wd_bf[...] = wd_buf[slot].astype(jnp.bfloat16)

        sync_x(first)
        put_act(up_proj(first))

        def tile_body(g, c):
            sync_x(g)
            sync_y_slot(g - 1)
            down_proj(g - 1)
            hgu = up_proj(g)
            put_act(hgu)
            _tile_copy(y_hbm, ybuf, ysem, g - 1, (g - 1) % TILE_BUFFERS, True).start()
            return c

        lax.fori_loop(first + 1, first + count, tile_body, 0)
        last = first + count - 1
        sync_y_slot(last)
        down_proj(last)
        _tile_copy(y_hbm, ybuf, ysem, last, last % TILE_BUFFERS, True).start()

    @pl.when(e == N_EXPERTS - 1)
    def _():
        for k in range(TILE_BUFFERS, 0, -1):
            @pl.when(n_tiles >= k)
            def _():
                _tile_copy(y_hbm, ybuf, ysem, n_tiles - k, (n_tiles - k) % TILE_BUFFERS, True).wait()

        ybuf[0] = jnp.zeros((ROW_TILE, HALF), jnp.uint32)

        def fill(g, c):
            cp = _tile_copy(y_hbm, ybuf, ysem, g, 0, True)
            cp.start()
            cp.wait()
            return c

        lax.fori_loop(n_tiles, max_tiles, fill, 0)


def _run_experts(tile_start, tiles_per_expert, x_sorted, w_gate, w_up, w_down):
    n_rows = x_sorted.shape[0]
    max_tiles = n_rows // ROW_TILE
    hbm = pl.BlockSpec(memory_space=pl.ANY)

    grid_spec = pltpu.PrefetchScalarGridSpec(
        num_scalar_prefetch=2,
        grid=(N_EXPERTS,),
        in_specs=[hbm, hbm, hbm, hbm],
        out_specs=hbm,
        scratch_shapes=[
            pltpu.VMEM((TILE_BUFFERS, ROW_TILE, HALF), jnp.uint32),
            pltpu.VMEM((TILE_BUFFERS, ROW_TILE, HALF), jnp.uint32),
            pltpu.SemaphoreType.DMA((TILE_BUFFERS,)),
            pltpu.SemaphoreType.DMA((TILE_BUFFERS,)),
            pltpu.VMEM((WEIGHT_BUFFERS, D_MODEL, D_EXPERT), jnp.float32),
            pltpu.VMEM((WEIGHT_BUFFERS, D_MODEL, D_EXPERT), jnp.float32),
            pltpu.VMEM((WEIGHT_BUFFERS, D_EXPERT, D_MODEL), jnp.float32),
            pltpu.SemaphoreType.DMA((WEIGHT_BUFFERS,)),
            pltpu.VMEM((D_MODEL, 2 * D_EXPERT), jnp.bfloat16),
            pltpu.VMEM((D_EXPERT, D_MODEL), jnp.bfloat16),
            pltpu.VMEM((ROW_TILE, D_EXPERT), jnp.bfloat16),
        ],
    )
    return pl.pallas_call(
        functools.partial(_expert_kernel, max_tiles=max_tiles),
        grid_spec=grid_spec,
        out_shape=jax.ShapeDtypeStruct((n_rows, HALF), jnp.uint32),
        compiler_params=pltpu.CompilerParams(
            dimension_semantics=("arbitrary",), vmem_limit_bytes=VMEM_LIMIT),
    )(tile_start, tiles_per_expert, x_sorted, w_gate, w_up, w_down)


def _combine_rows(r0, h1_ref, p_ref, route_ref, ya_ref, yb_ref, w_pg, b_pg, w_ple, g_ref, b_ref, o_ref):
    rows = pl.ds(r0, SUB_TILE)
    h1 = h1_ref[rows, :]
    h_hi = h1.astype(jnp.bfloat16)
    p_b = p_ref[rows, :].astype(jnp.bfloat16)
    yield
    gate_pre = jnp.dot(h_hi, w_pg[...], preferred_element_type=jnp.float32)
    ple_pre = jnp.dot(p_b, w_ple[...], preferred_element_type=jnp.float32)
    yield
    ple = ple_pre * jax.nn.sigmoid(gate_pre + b_pg[...])
    route = route_ref[rows, :]
    w1 = route[:, R_W1:R_W1 + 1]
    w2 = route[:, R_W2:R_W2 + 1]
    a_lo, a_hi = _unpack_bf16_pairs(ya_ref[rows, :])
    b_lo, b_hi = _unpack_bf16_pairs(yb_ref[rows, :])
    moe = jnp.concatenate([w1 * a_lo + w2 * b_lo, w1 * a_hi + w2 * b_hi], axis=-1)
    o_ref[rows, :] = _layernorm(DEEPNORM_ALPHA * h1 + ple + moe, g_ref[...], b_ref[...])


def _combine_kernel(h1_ref, p_ref, route_ref, ya_ref, yb_ref, w_pg, b_pg, w_ple, g_ref, b_ref, *rest):
    o_ref = rest[-1]
    chains = [_combine_rows(r0, h1_ref, p_ref, route_ref, ya_ref, yb_ref, w_pg, b_pg, w_ple, g_ref, b_ref, o_ref)
              for r0 in range(0, h1_ref.shape[0], SUB_TILE)]
    for t in range(COMBINE_PHASES + len(chains) - 1):
        for k, chain in enumerate(chains):
            if 0 <= t - k < COMBINE_PHASES:
                next(chain, None)


def _run_combine(h1, p2, tok_lo, route, y_tok, w_pg, b_pg, w_ple, ln2_g, ln2_b, out_prev, chunk, n_tok_total):
    n_tok = h1.shape[0]
    n_t = n_tok // TOKEN_TILE
    p_lo = tok_lo // TOKEN_TILE
    const = lambda shape: pl.BlockSpec(shape, lambda i: (0, 0), pipeline_mode=pl.Buffered(1))
    in_specs = [
        pl.BlockSpec((TOKEN_TILE, D_MODEL), lambda i: (i, 0)),
        pl.BlockSpec((TOKEN_TILE, PLE_DIM), lambda i: (p_lo + i, 0)),
        pl.BlockSpec((TOKEN_TILE, LANES), lambda i: (i, 0)),
        pl.BlockSpec((TOKEN_TILE, HALF), lambda i: (i, 0)),
        pl.BlockSpec((TOKEN_TILE, HALF), lambda i: (i + n_t, 0)),
        const((D_MODEL, D_MODEL)), const((1, D_MODEL)), const((PLE_DIM, D_MODEL)),
        const((1, D_MODEL)), const((1, D_MODEL)),
    ]
    args = [h1, p2, route, y_tok, y_tok, w_pg, b_pg, w_ple, ln2_g, ln2_b]
    aliases = {}
    if out_prev is not None:
        in_specs.append(pl.BlockSpec(memory_space=pl.ANY))
        args.append(out_prev)
        aliases = {len(args) - 1: 0}
    return pl.pallas_call(
        _combine_kernel,
        grid=(n_t,),
        in_specs=in_specs,
        out_specs=pl.BlockSpec((TOKEN_TILE, D_MODEL), lambda i: (chunk * n_t + i, 0)),
        out_shape=jax.ShapeDtypeStruct((n_tok_total, D_MODEL), jnp.float32),
        input_output_aliases=aliases,
        compiler_params=pltpu.CompilerParams(dimension_semantics=("arbitrary",), vmem_limit_bytes=VMEM_LIMIT),
    )(*args)


def _split_bf16(w):
    hi = w.astype(jnp.bfloat16)
    lo = (w - hi.astype(jnp.float32)).astype(jnp.bfloat16)
    return hi, lo


def kernel(x, p, ln_in_g, ln_in_b, w_in, conv_w, conv_b, pool_w, pool_scale, w_out, ln1_g, ln1_b,
           w_rg, b_rg, w_re, b_re, w_gate, w_up, w_down, w_pg, b_pg, w_ple, ln2_g, ln2_b):
    batch, seq, _ = x.shape
    n_tok = batch * seq
    bf = jnp.bfloat16
    row = lambda v: v.reshape(1, -1)

    w_r = jnp.concatenate([w_rg[0], jnp.transpose(w_re[0], (1, 0, 2)).reshape(D_MODEL, N_EXPERTS)], axis=1)
    w_r = jnp.pad(w_r, ((0, 0), (0, LANES - w_r.shape[1])))
    w_r_hi, w_r_lo = _split_bf16(w_r)
    w_r_cat = jnp.concatenate([w_r_hi, w_r_hi, w_r_lo], axis=0)
    b_r = jnp.pad(jnp.concatenate([b_rg[0], b_re[0].reshape(-1)]), (0, LANES - N_GROUPS - N_EXPERTS)).reshape(1, LANES)

    x2 = x.reshape(n_tok, D_MODEL)
    p2 = p[0].reshape(n_tok, PLE_DIM)
    mixer_weights = (row(ln_in_g), row(ln_in_b), w_in[0].astype(bf), conv_w[0], row(conv_b[0]),
                     pool_w[0].astype(bf), row(pool_scale[0]), w_out[0].astype(bf), row(ln1_g[0]), row(ln1_b[0]),
                     w_r_cat, b_r)
    combine_weights = (w_pg[0].astype(bf), row(b_pg[0]), w_ple[0].astype(bf), row(ln2_g[0]), row(ln2_b[0]))
    expert_weights = (w_gate[0].reshape(N_EXPERTS, D_MODEL, D_EXPERT),
                      w_up[0].reshape(N_EXPERTS, D_MODEL, D_EXPERT),
                      w_down[0].reshape(N_EXPERTS, D_EXPERT, D_MODEL))
    zero_rows = jnp.zeros((SC_WINDOW, HALF), jnp.uint32)

    n_chunks = N_CHUNKS if batch % N_CHUNKS == 0 else 1
    chunk_batch = batch // n_chunks
    chunk_tok = chunk_batch * seq
    n_rows = -(-(2 * chunk_tok + N_EXPERTS * (ROW_TILE - 1)) // ROW_TILE) * ROW_TILE
    out = None
    for c in range(n_chunks):
        xp, h1, route, route_t, counts = _run_mixer(x2, *mixer_weights, c * chunk_batch, chunk_batch, seq)
        pos, tile_start, tiles_per_expert, pad_pos = _run_plan(route_t, counts, n_rows)
        x_sorted = _dispatch_rows(xp, pos[0], pos[1], pad_pos.reshape(-1), zero_rows, n_rows)
        y_sorted = _run_experts(tile_start[0, :N_EXPERTS], tiles_per_expert[0, :N_EXPERTS], x_sorted,
                                *expert_weights)
        y_tok = _gather_rows(y_sorted, pos.reshape(-1))
        out = _run_combine(h1, p2, c * chunk_tok, route, y_tok, *combine_weights, out, c, n_tok)
    return out.reshape(batch, seq, D_MODEL)
```

```python
import functools

import jax
import jax.numpy as jnp
from jax import lax
from jax.experimental import pallas as pl
from jax.experimental.pallas import tpu as pltpu
from jax.experimental.pallas import tpu_sc as plsc

D_MODEL = 1024
CONV_WIDTH = 512
CONV_K = 3
POOL_WIDTH = 512
POOL_WINDOWS = (2, 4, 8, 16)
POOL_GW = 128
IN_PROJ = 3 * CONV_WIDTH + POOL_WIDTH
N_GROUPS = 4
EXPERTS_PER_GROUP = 8
N_EXPERTS = N_GROUPS * EXPERTS_PER_GROUP
D_EXPERT = 256
PLE_DIM = 256
LN_EPS = 1e-5
DEEPNORM_ALPHA = 2.0 ** 0.25

LANES = 128
HALF = D_MODEL // 2
CONV_HALO = 8
POOL_HALO = 16
SEQ_TILE = 1024
SUB_TILE = 256
MIXER_PHASES = 7
COMBINE_PHASES = 3
ROW_TILE = 256
TILE_BUFFERS = 8
N_CHUNKS = 1
SC_WORKERS = 32
SC_WINDOW = 64
WEIGHT_BUFFERS = 3
WEIGHT_DMA_PRIORITY = 1
TOKEN_TILE = 1024
VMEM_LIMIT = 56 * 1024 * 1024

R_ID1, R_ID2, R_RANK1, R_RANK2, R_W1, R_W2 = range(6)
ROUTE_ROWS = 8


def _layernorm(x, g, b):
    mu = jnp.mean(x, axis=-1, keepdims=True)
    xc = x - mu
    var = jnp.mean(xc * xc, axis=-1, keepdims=True)
    return xc * lax.rsqrt(var + LN_EPS) * g + b


def _pack_bf16_pairs(v):
    bits = lax.bitcast_convert_type(v.astype(jnp.bfloat16).astype(jnp.float32), jnp.uint32)
    return bits[:, HALF:] | (bits[:, :HALF] >> 16)


def _unpack_bf16_pairs(w):
    lo = lax.bitcast_convert_type(w << 16, jnp.float32)
    hi = lax.bitcast_convert_type(w & jnp.uint32(0xFFFF0000), jnp.float32)
    return lo, hi


def _mixer_kernel(x_ref, lnin_g, lnin_b, w_in, conv_w, conv_b, pool_w, pool_scale, w_out,
                  ln1_g, ln1_b, w_r, b_r,
                  xp_ref, h1_ref, route_ref, route_t_ref, counts_ref,
                  zbuf, vbuf, carry):
    b = pl.program_id(0)
    s = pl.program_id(1)
    ts = x_ref.shape[0]

    @pl.when(s == 0)
    def _():
        zbuf[0:CONV_HALO, :] = jnp.zeros((CONV_HALO, CONV_WIDTH), jnp.float32)
        vbuf[0:POOL_HALO, :] = jnp.zeros((POOL_HALO, POOL_WIDTH), jnp.float32)

    @pl.when((b == 0) & (s == 0))
    def _():
        carry[...] = jnp.zeros_like(carry)

    chains = [_mixer_rows(r0, s * ts + r0, x_ref, lnin_g, lnin_b, w_in, conv_w, conv_b, pool_w, pool_scale,
                          w_out, ln1_g, ln1_b, w_r, b_r,
                          xp_ref, h1_ref, route_ref, route_t_ref, zbuf, vbuf, carry)
              for r0 in range(0, ts, SUB_TILE)]
    for t in range(MIXER_PHASES + len(chains) - 1):
        for k, chain in enumerate(chains):
            if 0 <= t - k < MIXER_PHASES:
                next(chain, None)
    zbuf[0:CONV_HALO, :] = zbuf[ts:ts + CONV_HALO, :]
    vbuf[0:POOL_HALO, :] = vbuf[ts:ts + POOL_HALO, :]
    counts_ref[...] = carry[...]


def _mixer_rows(r0, seq0, x_ref, lnin_g, lnin_b, w_in, conv_w, conv_b, pool_w, pool_scale, w_out,
                ln1_g, ln1_b, w_r, b_r,
                xp_ref, h1_ref, route_ref, route_t_ref, zbuf, vbuf, carry):
    n = SUB_TILE
    rows = pl.ds(r0, n)
    h0 = _layernorm(x_ref[rows, :], lnin_g[...], lnin_b[...])
    h0b = h0.astype(jnp.bfloat16)
    yield
    u = jnp.dot(h0b, w_in[...], preferred_element_type=jnp.float32)
    yield
    b_g = u[:, 0:CONV_WIDTH]
    c_g = u[:, CONV_WIDTH:2 * CONV_WIDTH]
    v_c = u[:, 2 * CONV_WIDTH:3 * CONV_WIDTH]
    v_p = u[:, 3 * CONV_WIDTH:]

    zbuf[pl.ds(CONV_HALO + r0, n), :] = c_g * v_c
    zext = zbuf[pl.ds(r0, n + CONV_HALO), :]
    z1 = pltpu.roll(zext, 1, axis=0)[CONV_HALO:, :]
    z2 = pltpu.roll(zext, 2, axis=0)[CONV_HALO:, :]
    conv = z2 * conv_w[0:1, :] + z1 * conv_w[1:2, :] + zext[CONV_HALO:, :] * conv_w[2:3, :] + conv_b[...]
    y_conv = b_g * conv

    vbuf[pl.ds(POOL_HALO + r0, n), :] = v_p
    vext = vbuf[pl.ds(r0, n + POOL_HALO), :]
    s2 = vext + pltpu.roll(vext, 1, axis=0)
    s4 = s2[:, POOL_GW:] + pltpu.roll(s2[:, POOL_GW:], 2, axis=0)
    s8 = s4[:, POOL_GW:] + pltpu.roll(s4[:, POOL_GW:], 4, axis=0)
    s16 = s8[:, POOL_GW:] + pltpu.roll(s8[:, POOL_GW:], 8, axis=0)
    wsums = (s2[POOL_HALO:, 0:POOL_GW], s4[POOL_HALO:, 0:POOL_GW],
             s8[POOL_HALO:, 0:POOL_GW], s16[POOL_HALO:, 0:POOL_GW])

    t_pos = seq0 + lax.broadcasted_iota(jnp.int32, (n, 1), 0)
    y_pool = []
    for j, w in enumerate(POOL_WINDOWS):
        inv_cnt = 1.0 / jnp.minimum(t_pos + 1, w).astype(jnp.float32)
        pooled = wsums[j] * inv_cnt - v_p[:, j * POOL_GW:(j + 1) * POOL_GW]
        y_pool.append(jnp.dot(pooled.astype(jnp.bfloat16), pool_w[j], preferred_element_type=jnp.float32))
    y_pool = jnp.concatenate(y_pool, axis=-1) * pool_scale[...]

    ycat = jnp.concatenate([y_conv, y_pool], axis=-1).astype(jnp.bfloat16)
    yield
    mix = jnp.dot(ycat, w_out[...], preferred_element_type=jnp.float32)
    yield
    h1 = _layernorm(DEEPNORM_ALPHA * h0 + mix, ln1_g[...], ln1_b[...])

    h_hi = h1.astype(jnp.bfloat16)
    h_lo = (h1 - h_hi.astype(jnp.float32)).astype(jnp.bfloat16)
    xp_ref[rows, :] = _pack_bf16_pairs(h1)
    h1_ref[rows, :] = h1
    hcat = jnp.concatenate([h_hi, h_lo, h_hi], axis=-1)
    yield
    logits = jnp.dot(hcat, w_r[...], preferred_element_type=jnp.float32) + b_r[...]
    yield
    lane = lax.broadcasted_iota(jnp.int32, (n, LANES), 1).astype(jnp.float32)
    neg = jnp.float32(-jnp.inf)

    def first_argmax(vals):
        m = jnp.max(vals, axis=-1, keepdims=True)
        idx = jnp.min(jnp.where(vals == m, lane, float(LANES)), axis=-1, keepdims=True)
        return m, idx

    g_mask = lane < N_GROUPS
    g_max, g_idx = first_argmax(jnp.where(g_mask, logits, neg))
    g_w = 1.0 / jnp.sum(jnp.where(g_mask, jnp.exp(logits - g_max), 0.0), axis=-1, keepdims=True)

    e_lo = N_GROUPS + EXPERTS_PER_GROUP * g_idx
    e_vals = jnp.where((lane >= e_lo) & (lane < e_lo + EXPERTS_PER_GROUP), logits, neg)
    m1, i1 = first_argmax(e_vals)
    m2, i2 = first_argmax(jnp.where(lane == i1, neg, e_vals))
    e21 = jnp.exp(m2 - m1)
    w1 = g_w / (1.0 + e21)
    w2 = g_w * e21 / (1.0 + e21)
    id1 = i1 - N_GROUPS
    id2 = i2 - N_GROUPS

    sel1 = lane == id1
    sel2 = lane == id2
    onehot = (sel1 | sel2).astype(jnp.float32)
    row = lax.broadcasted_iota(jnp.int32, (n, n), 0)
    col = lax.broadcasted_iota(jnp.int32, (n, n), 1)
    tri = (col < row).astype(jnp.bfloat16)
    before = jnp.dot(tri, onehot.astype(jnp.bfloat16), preferred_element_type=jnp.float32) + carry[...]
    rank1 = jnp.sum(jnp.where(sel1, before, 0.0), axis=-1, keepdims=True)
    rank2 = jnp.sum(jnp.where(sel2, before, 0.0), axis=-1, keepdims=True)
    carry[...] = carry[...] + jnp.sum(onehot, axis=0, keepdims=True)

    rec = jnp.zeros((n, LANES), jnp.float32)
    for k, val in ((R_ID1, id1), (R_ID2, id2), (R_W1, w1), (R_W2, w2), (R_RANK1, rank1), (R_RANK2, rank2)):
        rec = jnp.where(lane == k, val, rec)
    route_ref[rows, :] = rec
    route_t_ref[:, rows] = jnp.transpose(rec)[0:ROUTE_ROWS, :]


def _run_mixer(x2, lnin_g, lnin_b, w_in, conv_w, conv_b, pool_w, pool_scale, w_out, ln1_g, ln1_b,
               w_r, b_r, batch_lo, batch, seq):
    n_tok = batch * seq
    n_s = seq // SEQ_TILE
    tok_map = lambda b, s: (b * n_s + s, 0)
    in_map = lambda b, s: ((batch_lo + b) * n_s + s, 0)

    def const(shape):
        return pl.BlockSpec(shape, lambda b, s: (0,) * len(shape), pipeline_mode=pl.Buffered(1))

    in_specs = [
        pl.BlockSpec((SEQ_TILE, D_MODEL), in_map),
        const((1, D_MODEL)), const((1, D_MODEL)),
        const((D_MODEL, IN_PROJ)),
        const((CONV_K, CONV_WIDTH)), const((1, CONV_WIDTH)),
        const((len(POOL_WINDOWS), POOL_GW, POOL_GW)), const((1, POOL_WIDTH)),
        const((D_MODEL, D_MODEL)),
        const((1, D_MODEL)), const((1, D_MODEL)),
        const((3 * D_MODEL, LANES)), const((1, LANES)),
    ]
    out_specs = [
        pl.BlockSpec((SEQ_TILE, HALF), tok_map),
        pl.BlockSpec((SEQ_TILE, D_MODEL), tok_map),
        pl.BlockSpec((SEQ_TILE, LANES), tok_map),
        pl.BlockSpec((ROUTE_ROWS, SEQ_TILE), lambda b, s: (0, b * n_s + s)),
        pl.BlockSpec((1, LANES), lambda b, s: (0, 0)),
    ]
    out_shape = [
        jax.ShapeDtypeStruct((n_tok, HALF), jnp.uint32),
        jax.ShapeDtypeStruct((n_tok, D_MODEL), jnp.float32),
        jax.ShapeDtypeStruct((n_tok, LANES), jnp.float32),
        jax.ShapeDtypeStruct((ROUTE_ROWS, n_tok), jnp.float32),
        jax.ShapeDtypeStruct((1, LANES), jnp.float32),
    ]
    return pl.pallas_call(
        _mixer_kernel,
        grid=(batch, n_s),
        in_specs=in_specs,
        out_specs=out_specs,
        out_shape=out_shape,
        scratch_shapes=[
            pltpu.VMEM((SEQ_TILE + CONV_HALO, CONV_WIDTH), jnp.float32),
            pltpu.VMEM((SEQ_TILE + POOL_HALO, POOL_WIDTH), jnp.float32),
            pltpu.VMEM((1, LANES), jnp.float32),
        ],
        compiler_params=pltpu.CompilerParams(
            dimension_semantics=("arbitrary", "arbitrary"), vmem_limit_bytes=VMEM_LIMIT),
    )(x2, lnin_g, lnin_b, w_in, conv_w, conv_b, pool_w, pool_scale, w_out, ln1_g, ln1_b, w_r, b_r)


def _plan_kernel(rt_ref, counts_ref, pos_ref, tile_start_ref, tiles_ref, pad_ref, *, n_rows):
    lane = lax.broadcasted_iota(jnp.int32, (ROUTE_ROWS, LANES), 1)
    counts = jnp.broadcast_to(counts_ref[...], (ROUTE_ROWS, LANES))
    tiles = jnp.floor((counts + (ROW_TILE - 1)) * (1.0 / ROW_TILE))
    tile_end = tiles
    shift = 1
    while shift < N_EXPERTS:
        tile_end = tile_end + jnp.where(lane >= shift, pltpu.roll(tile_end, shift, axis=1), 0.0)
        shift *= 2
    row_start = (tile_end - tiles) * ROW_TILE
    tile_start_ref[...] = (tile_end - tiles)[0:1, :].astype(jnp.int32)
    tiles_ref[...] = tiles[0:1, :].astype(jnp.int32)

    rt = rt_ref[...]
    ids = rt[R_ID1:R_ID2 + 1, :]
    start = jnp.zeros_like(ids)
    for e in range(N_EXPERTS):
        start = jnp.where(ids == e, row_start[0:1, e:e + 1], start)
    pos_ref[...] = (start + rt[R_RANK1:R_RANK2 + 1, :]).astype(jnp.int32)

    sub = lax.broadcasted_iota(jnp.int32, (N_EXPERTS, LANES), 0)
    lane_e = lax.broadcasted_iota(jnp.int32, (N_EXPERTS, LANES), 1)
    diag = sub == lane_e
    pad_lo = jnp.sum(jnp.where(diag, (row_start + counts)[0:1, :], 0.0), axis=1, keepdims=True)
    pad_n = jnp.sum(jnp.where(diag, (tiles * ROW_TILE - counts)[0:1, :], 0.0), axis=1, keepdims=True)
    j = lax.broadcasted_iota(jnp.int32, (N_EXPERTS, ROW_TILE), 1).astype(jnp.float32)
    pad_ref[...] = jnp.where(j < pad_n, pad_lo + j, n_rows + j).astype(jnp.int32)


def _run_plan(route_t, counts, n_rows):
    n_tok = route_t.shape[1]
    full = lambda shape: pl.BlockSpec(shape, lambda i: (0, 0))
    return pl.pallas_call(
        functools.partial(_plan_kernel, n_rows=n_rows),
        grid=(1,),
        in_specs=[full((ROUTE_ROWS, n_tok)), full((1, LANES))],
        out_specs=[full((2, n_tok)), full((1, LANES)), full((1, LANES)), full((N_EXPERTS, ROW_TILE))],
        out_shape=[
            jax.ShapeDtypeStruct((2, n_tok), jnp.int32),
            jax.ShapeDtypeStruct((1, LANES), jnp.int32),
            jax.ShapeDtypeStruct((1, LANES), jnp.int32),
            jax.ShapeDtypeStruct((N_EXPERTS, ROW_TILE), jnp.int32),
        ],
        compiler_params=pltpu.CompilerParams(dimension_semantics=("arbitrary",)),
    )(route_t, counts)


def _sc_mesh():
    return plsc.VectorSubcoreMesh(core_axis_name="core", subcore_axis_name="subcore")


def _sc_worker_id():
    return lax.axis_index("core") * (SC_WORKERS // 2) + lax.axis_index("subcore")


def _dispatch_rows(xp, pos1, pos2, pad_pos, zero_rows, n_rows):
    n_tok, width = xp.shape
    n_win = n_tok // SC_WORKERS // SC_WINDOW
    n_pad = pad_pos.shape[0] // SC_WORKERS // SC_WINDOW
    as_windows = lambda v: v.reshape(-1, SC_WINDOW)

    @functools.partial(
        pl.kernel, out_type=jax.ShapeDtypeStruct((n_rows + ROW_TILE, width), xp.dtype), mesh=_sc_mesh(),
        scratch_types=[pltpu.VMEM((n_win, SC_WINDOW), jnp.int32), pltpu.VMEM((n_win, SC_WINDOW), jnp.int32),
                       pltpu.VMEM((n_pad, SC_WINDOW), jnp.int32),
                       pltpu.VMEM((2, SC_WINDOW, width), xp.dtype),
                       pltpu.SemaphoreType.DMA((2,)), pltpu.SemaphoreType.DMA((2,))])
    def dispatch(xp_hbm, pos1_hbm, pos2_hbm, pad_hbm, zero_hbm, out_hbm, idx1, idx2, idxp, buf, lsem, ssem):
        wid = _sc_worker_id()
        pltpu.sync_copy(pos1_hbm.at[pl.ds(wid * n_win, n_win)], idx1)
        pltpu.sync_copy(pos2_hbm.at[pl.ds(wid * n_win, n_win)], idx2)
        pltpu.sync_copy(pad_hbm.at[pl.ds(wid * n_pad, n_pad)], idxp)

        def load(j):
            rows = xp_hbm.at[pl.ds((wid * n_win + j) * SC_WINDOW, SC_WINDOW)]
            return pltpu.make_async_copy(rows, buf.at[j % 2], lsem.at[j % 2])

        def scatters(j):
            return [pltpu.make_async_copy(buf.at[j % 2], out_hbm.at[idx.at[j]], ssem.at[j % 2])
                    for idx in (idx1, idx2)]

        load(0).start()
        for j in range(n_win):
            load(j).wait()
            for cp in scatters(j):
                cp.start()
            if j >= 1:
                for cp in scatters(j - 1):
                    cp.wait()
            if j + 1 < n_win:
                load(j + 1).start()
        for cp in scatters(n_win - 1):
            cp.wait()

        pltpu.sync_copy(zero_hbm, buf.at[0])
        pads = [pltpu.make_async_copy(buf.at[0], out_hbm.at[idxp.at[j]], ssem.at[0]) for j in range(n_pad)]
        for cp in pads:
            cp.start()
        for cp in pads:
            cp.wait()

    return dispatch(xp, as_windows(pos1), as_windows(pos2), as_windows(pad_pos), zero_rows)


def _gather_rows(src, idx):
    n_out, width = idx.shape[0], src.shape[1]
    n_win = n_out // SC_WORKERS // SC_WINDOW

    @functools.partial(
        pl.kernel, out_type=jax.ShapeDtypeStruct((n_out, width), src.dtype), mesh=_sc_mesh(),
        scratch_types=[pltpu.VMEM((n_win, SC_WINDOW), jnp.int32), pltpu.VMEM((2, SC_WINDOW, width), src.dtype),
                       pltpu.SemaphoreType.DMA((2,)), pltpu.SemaphoreType.DMA((2,))])
    def gather(src_hbm, idx_hbm, dst_hbm, idx_v, buf, gsem, ssem):
        wid = _sc_worker_id()
        pltpu.sync_copy(idx_hbm.at[pl.ds(wid * n_win, n_win)], idx_v)

        def fetch(j):
            return pltpu.make_async_copy(src_hbm.at[idx_v.at[j]], buf.at[j % 2], gsem.at[j % 2])

        def store(j):
            rows = dst_hbm.at[pl.ds((wid * n_win + j) * SC_WINDOW, SC_WINDOW)]
            return pltpu.make_async_copy(buf.at[j % 2], rows, ssem.at[j % 2])

        fetch(0).start()
        for j in range(n_win):
            fetch(j).wait()
            store(j).start()
            if j >= 1:
                store(j - 1).wait()
            if j + 1 < n_win:
                fetch(j + 1).start()
        store(n_win - 1).wait()

    return gather(src, idx.reshape(-1, SC_WINDOW))


def _tile_copy(hbm, buf, sem, tile, slot, to_hbm):
    rows = hbm.at[pl.ds(pl.multiple_of(tile * ROW_TILE, ROW_TILE), ROW_TILE)]
    if to_hbm:
        return pltpu.make_async_copy(buf.at[slot], rows, sem.at[slot])
    return pltpu.make_async_copy(rows, buf.at[slot], sem.at[slot])


def _weight_copies(w_hbm, wbuf, wsem, expert):
    slot = expert % WEIGHT_BUFFERS
    return [pltpu.make_async_copy(w.at[expert], buf.at[slot], wsem.at[slot]) for w, buf in zip(w_hbm, wbuf)]


def _expert_kernel(ts_ref, nte_ref, x_hbm, wg_hbm, wu_hbm, wd_hbm, y_hbm,
                   xbuf, ybuf, xsem, ysem, wg_buf, wu_buf, wd_buf, wsem, wgu_bf, wd_bf, act_ref, *, max_tiles):
    e = pl.program_id(0)
    first = ts_ref[e]
    count = nte_ref[e]
    n_tiles = ts_ref[N_EXPERTS - 1] + nte_ref[N_EXPERTS - 1]
    ahead = TILE_BUFFERS - 1
    w_hbm = (wg_hbm, wu_hbm, wd_hbm)
    wbuf = (wg_buf, wu_buf, wd_buf)

    @pl.when(e == 0)
    def _():
        for g in range(ahead):
            @pl.when(g < n_tiles)
            def _():
                _tile_copy(x_hbm, xbuf, xsem, g, g, False).start()
        for k in range(WEIGHT_BUFFERS - 1):
            for cp in _weight_copies(w_hbm, wbuf, wsem, k):
                cp.start(priority=WEIGHT_DMA_PRIORITY)

    @pl.when(e + WEIGHT_BUFFERS - 1 < N_EXPERTS)
    def _():
        for cp in _weight_copies(w_hbm, wbuf, wsem, e + WEIGHT_BUFFERS - 1):
            cp.start(priority=WEIGHT_DMA_PRIORITY)

    for cp in _weight_copies(w_hbm, wbuf, wsem, e):
        cp.wait()

    def sync_x(g):
        _tile_copy(x_hbm, xbuf, xsem, g, g % TILE_BUFFERS, False).wait()

        @pl.when(g + ahead < n_tiles)
        def _():
            _tile_copy(x_hbm, xbuf, xsem, g + ahead, (g + ahead) % TILE_BUFFERS, False).start()

    def sync_y_slot(g):
        @pl.when(g >= TILE_BUFFERS)
        def _():
            _tile_copy(y_hbm, ybuf, ysem, g - TILE_BUFFERS, g % TILE_BUFFERS, True).wait()

    def up_proj(g):
        lo, hi = _unpack_bf16_pairs(xbuf[g % TILE_BUFFERS])
        return (jnp.dot(lo.astype(jnp.bfloat16), wgu_bf[0:HALF, :], preferred_element_type=jnp.float32)
                + jnp.dot(hi.astype(jnp.bfloat16), wgu_bf[HALF:, :], preferred_element_type=jnp.float32))

    def put_act(hgu):
        hg = hgu[:, 0:D_EXPERT]
        act_ref[...] = (hg * jax.nn.sigmoid(hg) * hgu[:, D_EXPERT:]).astype(jnp.bfloat16)

    def down_proj(g):
        y = jnp.dot(act_ref[...], wd_bf[...], preferred_element_type=jnp.float32)
        ybuf[g % TILE_BUFFERS] = _pack_bf16_pairs(y)

    @pl.when(count > 0)
    def _():
        slot = e % WEIGHT_BUFFERS
        wgu_bf[:, 0:D_EXPERT] = wg_buf[slot].astype(jnp.bfloat16)
        wgu_bf[:, D_EXPERT:] = wu_buf[slot].astype(jnp.bfloat16)
        wd_bf[...] = wd_buf[slot].astype(jnp.bfloat16)

        sync_x(first)
        put_act(up_proj(first))

        def tile_body(g, c):
            sync_x(g)
            sync_y_slot(g - 1)
            down_proj(g - 1)
            hgu = up_proj(g)
            put_act(hgu)
            _tile_copy(y_hbm, ybuf, ysem, g - 1, (g - 1) % TILE_BUFFERS, True).start()
            return c

        lax.fori_loop(first + 1, first + count, tile_body, 0)
        last = first + count - 1
        sync_y_slot(last)
        down_proj(last)
        _tile_copy(y_hbm, ybuf, ysem, last, last % TILE_BUFFERS, True).start()

    @pl.when(e == N_EXPERTS - 1)
    def _():
        for k in range(TILE_BUFFERS, 0, -1):
            @pl.when(n_tiles >= k)
            def _():
                _tile_copy(y_hbm, ybuf, ysem, n_tiles - k, (n_tiles - k) % TILE_BUFFERS, True).wait()

        ybuf[0] = jnp.zeros((ROW_TILE, HALF), jnp.uint32)

        def fill(g, c):
            cp = _tile_copy(y_hbm, ybuf, ysem, g, 0, True)
            cp.start()
            cp.wait()
            return c

        lax.fori_loop(n_tiles, max_tiles, fill, 0)


def _run_experts(tile_start, tiles_per_expert, x_sorted, w_gate, w_up, w_down):
    n_rows = x_sorted.shape[0]
    max_tiles = n_rows // ROW_TILE
    hbm = pl.BlockSpec(memory_space=pl.ANY)

    grid_spec = pltpu.PrefetchScalarGridSpec(
        num_scalar_prefetch=2,
        grid=(N_EXPERTS,),
        in_specs=[hbm, hbm, hbm, hbm],
        out_specs=hbm,
        scratch_shapes=[
            pltpu.VMEM((TILE_BUFFERS, ROW_TILE, HALF), jnp.uint32),
            pltpu.VMEM((TILE_BUFFERS, ROW_TILE, HALF), jnp.uint32),
            pltpu.SemaphoreType.DMA((TILE_BUFFERS,)),
            pltpu.SemaphoreType.DMA((TILE_BUFFERS,)),
            pltpu.VMEM((WEIGHT_BUFFERS, D_MODEL, D_EXPERT), jnp.float32),
            pltpu.VMEM((WEIGHT_BUFFERS, D_MODEL, D_EXPERT), jnp.float32),
            pltpu.VMEM((WEIGHT_BUFFERS, D_EXPERT, D_MODEL), jnp.float32),
            pltpu.SemaphoreType.DMA((WEIGHT_BUFFERS,)),
            pltpu.VMEM((D_MODEL, 2 * D_EXPERT), jnp.bfloat16),
            pltpu.VMEM((D_EXPERT, D_MODEL), jnp.bfloat16),
            pltpu.VMEM((ROW_TILE, D_EXPERT), jnp.bfloat16),
        ],
    )
    return pl.pallas_call(
        functools.partial(_expert_kernel, max_tiles=max_tiles),
        grid_spec=grid_spec,
        out_shape=jax.ShapeDtypeStruct((n_rows, HALF), jnp.uint32),
        compiler_params=pltpu.CompilerParams(
            dimension_semantics=("arbitrary",), vmem_limit_bytes=VMEM_LIMIT),
    )(tile_start, tiles_per_expert, x_sorted, w_gate, w_up, w_down)


def _combine_rows(r0, h1_ref, p_ref, route_ref, ya_ref, yb_ref, w_pg, b_pg, w_ple, g_ref, b_ref, o_ref):
    rows = pl.ds(r0, SUB_TILE)
    h1 = h1_ref[rows, :]
    h_hi = h1.astype(jnp.bfloat16)
    p_b = p_ref[rows, :].astype(jnp.bfloat16)
    yield
    gate_pre = jnp.dot(h_hi, w_pg[...], preferred_element_type=jnp.float32)
    ple_pre = jnp.dot(p_b, w_ple[...], preferred_element_type=jnp.float32)
    yield
    ple = ple_pre * jax.nn.sigmoid(gate_pre + b_pg[...])
    route = route_ref[rows, :]
    w1 = route[:, R_W1:R_W1 + 1]
    w2 = route[:, R_W2:R_W2 + 1]
    a_lo, a_hi = _unpack_bf16_pairs(ya_ref[rows, :])
    b_lo, b_hi = _unpack_bf16_pairs(yb_ref[rows, :])
    moe = jnp.concatenate([w1 * a_lo + w2 * b_lo, w1 * a_hi + w2 * b_hi], axis=-1)
    o_ref[rows, :] = _layernorm(DEEPNORM_ALPHA * h1 + ple + moe, g_ref[...], b_ref[...])


def _combine_kernel(h1_ref, p_ref, route_ref, ya_ref, yb_ref, w_pg, b_pg, w_ple, g_ref, b_ref, *rest):
    o_ref = rest[-1]
    chains = [_combine_rows(r0, h1_ref, p_ref, route_ref, ya_ref, yb_ref, w_pg, b_pg, w_ple, g_ref, b_ref, o_ref)
              for r0 in range(0, h1_ref.shape[0], SUB_TILE)]
    for t in range(COMBINE_PHASES + len(chains) - 1):
        for k, chain in enumerate(chains):
            if 0 <= t - k < COMBINE_PHASES:
                next(chain, None)


def _run_combine(h1, p2, tok_lo, route, y_tok, w_pg, b_pg, w_ple, ln2_g, ln2_b, out_prev, chunk, n_tok_total):
    n_tok = h1.shape[0]
    n_t = n_tok // TOKEN_TILE
    p_lo = tok_lo // TOKEN_TILE
    const = lambda shape: pl.BlockSpec(shape, lambda i: (0, 0), pipeline_mode=pl.Buffered(1))
    in_specs = [
        pl.BlockSpec((TOKEN_TILE, D_MODEL), lambda i: (i, 0)),
        pl.BlockSpec((TOKEN_TILE, PLE_DIM), lambda i: (p_lo + i, 0)),
        pl.BlockSpec((TOKEN_TILE, LANES), lambda i: (i, 0)),
        pl.BlockSpec((TOKEN_TILE, HALF), lambda i: (i, 0)),
        pl.BlockSpec((TOKEN_TILE, HALF), lambda i: (i + n_t, 0)),
        const((D_MODEL, D_MODEL)), const((1, D_MODEL)), const((PLE_DIM, D_MODEL)),
        const((1, D_MODEL)), const((1, D_MODEL)),
    ]
    args = [h1, p2, route, y_tok, y_tok, w_pg, b_pg, w_ple, ln2_g, ln2_b]
    aliases = {}
    if out_prev is not None:
        in_specs.append(pl.BlockSpec(memory_space=pl.ANY))
        args.append(out_prev)
        aliases = {len(args) - 1: 0}
    return pl.pallas_call(
        _combine_kernel,
        grid=(n_t,),
        in_specs=in_specs,
        out_specs=pl.BlockSpec((TOKEN_TILE, D_MODEL), lambda i: (chunk * n_t + i, 0)),
        out_shape=jax.ShapeDtypeStruct((n_tok_total, D_MODEL), jnp.float32),
        input_output_aliases=aliases,
        compiler_params=pltpu.CompilerParams(dimension_semantics=("arbitrary",), vmem_limit_bytes=VMEM_LIMIT),
    )(*args)


def _split_bf16(w):
    hi = w.astype(jnp.bfloat16)
    lo = (w - hi.astype(jnp.float32)).astype(jnp.bfloat16)
    return hi, lo


def kernel(x, p, ln_in_g, ln_in_b, w_in, conv_w, conv_b, pool_w, pool_scale, w_out, ln1_g, ln1_b,
           w_rg, b_rg, w_re, b_re, w_gate, w_up, w_down, w_pg, b_pg, w_ple, ln2_g, ln2_b):
    batch, seq, _ = x.shape
    n_tok = batch * seq
    bf = jnp.bfloat16
    row = lambda v: v.reshape(1, -1)

    w_r = jnp.concatenate([w_rg[0], jnp.transpose(w_re[0], (1, 0, 2)).reshape(D_MODEL, N_EXPERTS)], axis=1)
    w_r = jnp.pad(w_r, ((0, 0), (0, LANES - w_r.shape[1])))
    w_r_hi, w_r_lo = _split_bf16(w_r)
    w_r_cat = jnp.concatenate([w_r_hi, w_r_hi, w_r_lo], axis=0)
    b_r = jnp.pad(jnp.concatenate([b_rg[0], b_re[0].reshape(-1)]), (0, LANES - N_GROUPS - N_EXPERTS)).reshape(1, LANES)

    x2 = x.reshape(n_tok, D_MODEL)
    p2 = p[0].reshape(n_tok, PLE_DIM)
    mixer_weights = (row(ln_in_g), row(ln_in_b), w_in[0].astype(bf), conv_w[0], row(conv_b[0]),
                     pool_w[0].astype(bf), row(pool_scale[0]), w_out[0].astype(bf), row(ln1_g[0]), row(ln1_b[0]),
                     w_r_cat, b_r)
    combine_weights = (w_pg[0].astype(bf), row(b_pg[0]), w_ple[0].astype(bf), row(ln2_g[0]), row(ln2_b[0]))
    expert_weights = (w_gate[0].reshape(N_EXPERTS, D_MODEL, D_EXPERT),
                      w_up[0].reshape(N_EXPERTS, D_MODEL, D_EXPERT),
                      w_down[0].reshape(N_EXPERTS, D_EXPERT, D_MODEL))
    zero_rows = jnp.zeros((SC_WINDOW, HALF), jnp.uint32)

    n_chunks = N_CHUNKS if batch % N_CHUNKS == 0 else 1
    chunk_batch = batch // n_chunks
    chunk_tok = chunk_batch * seq
    n_rows = -(-(2 * chunk_tok + N_EXPERTS * (ROW_TILE - 1)) // ROW_TILE) * ROW_TILE
    out = None
    for c in range(n_chunks):
        xp, h1, route, route_t, counts = _run_mixer(x2, *mixer_weights, c * chunk_batch, chunk_batch, seq)
        pos, tile_start, tiles_per_expert, pad_pos = _run_plan(route_t, counts, n_rows)
        x_sorted = _dispatch_rows(xp, pos[0], pos[1], pad_pos.reshape(-1), zero_rows, n_rows)
        y_sorted = _run_experts(tile_start[0, :N_EXPERTS], tiles_per_expert[0, :N_EXPERTS], x_sorted,
                                *expert_weights)
        y_tok = _gather_rows(y_sorted, pos.reshape(-1))
        out = _run_combine(h1, p2, c * chunk_tok, route, y_tok, *combine_weights, out, c, n_tok)
    return out.reshape(batch, seq, D_MODEL)
```

```python
import functools

import jax
import jax.numpy as jnp
from jax import lax
from jax.experimental import pallas as pl
from jax.experimental.pallas import tpu as pltpu
from jax.experimental.pallas import tpu_sc as plsc

D_MODEL = 1024
CONV_WIDTH = 512
CONV_K = 3
POOL_WIDTH = 512
POOL_WINDOWS = (2, 4, 8, 16)
POOL_GW = 128
IN_PROJ = 3 * CONV_WIDTH + POOL_WIDTH
N_GROUPS = 4
EXPERTS_PER_GROUP = 8
N_EXPERTS = N_GROUPS * EXPERTS_PER_GROUP
D_EXPERT = 256
PLE_DIM = 256
LN_EPS = 1e-5
DEEPNORM_ALPHA = 2.0 ** 0.25

LANES = 128
HALF = D_MODEL // 2
CONV_HALO = 8
POOL_HALO = 16
SEQ_TILE = 1024
SUB_TILE = 256
MIXER_PHASES = 7
COMBINE_PHASES = 3
ROW_TILE = 256
TILE_BUFFERS = 8
COMBINE_PARTS = 2
SC_WORKERS = 32
SC_WINDOW = 64
WEIGHT_BUFFERS = 3
TOKEN_TILE = 1024
VMEM_LIMIT = 56 * 1024 * 1024

R_ID1, R_ID2, R_RANK1, R_RANK2, R_W1, R_W2 = range(6)
ROUTE_ROWS = 8


def _layernorm(x, g, b):
    mu = jnp.mean(x, axis=-1, keepdims=True)
    xc = x - mu
    var = jnp.mean(xc * xc, axis=-1, keepdims=True)
    return xc * lax.rsqrt(var + LN_EPS) * g + b


def _pack_bf16_pairs(v):
    bits = lax.bitcast_convert_type(v.astype(jnp.bfloat16).astype(jnp.float32), jnp.uint32)
    return bits[:, HALF:] | (bits[:, :HALF] >> 16)


def _unpack_bf16_pairs(w):
    lo = lax.bitcast_convert_type(w << 16, jnp.float32)
    hi = lax.bitcast_convert_type(w & jnp.uint32(0xFFFF0000), jnp.float32)
    return lo, hi


def _mixer_kernel(x_ref, lnin_g, lnin_b, w_in, conv_w, conv_b, pool_w, pool_scale, w_out,
                  ln1_g, ln1_b, w_r, b_r,
                  xp_ref, h1_ref, route_ref, route_t_ref, counts_ref,
                  zbuf, vbuf, carry):
    b = pl.program_id(0)
    s = pl.program_id(1)
    ts = x_ref.shape[0]

    @pl.when(s == 0)
    def _():
        zbuf[0:CONV_HALO, :] = jnp.zeros((CONV_HALO, CONV_WIDTH), jnp.float32)
        vbuf[0:POOL_HALO, :] = jnp.zeros((POOL_HALO, POOL_WIDTH), jnp.float32)

    @pl.when((b == 0) & (s == 0))
    def _():
        carry[...] = jnp.zeros_like(carry)

    chains = [_mixer_rows(r0, s * ts + r0, x_ref, lnin_g, lnin_b, w_in, conv_w, conv_b, pool_w, pool_scale,
                          w_out, ln1_g, ln1_b, w_r, b_r,
                          xp_ref, h1_ref, route_ref, route_t_ref, zbuf, vbuf, carry)
              for r0 in range(0, ts, SUB_TILE)]
    for t in range(MIXER_PHASES + len(chains) - 1):
        for k, chain in enumerate(chains):
            if 0 <= t - k < MIXER_PHASES:
                next(chain, None)
    zbuf[0:CONV_HALO, :] = zbuf[ts:ts + CONV_HALO, :]
    vbuf[0:POOL_HALO, :] = vbuf[ts:ts + POOL_HALO, :]
    counts_ref[...] = carry[...]


def _mixer_rows(r0, seq0, x_ref, lnin_g, lnin_b, w_in, conv_w, conv_b, pool_w, pool_scale, w_out,
                ln1_g, ln1_b, w_r, b_r,
                xp_ref, h1_ref, route_ref, route_t_ref, zbuf, vbuf, carry):
    n = SUB_TILE
    rows = pl.ds(r0, n)
    h0 = _layernorm(x_ref[rows, :], lnin_g[...], lnin_b[...])
    h0b = h0.astype(jnp.bfloat16)
    yield
    u = jnp.dot(h0b, w_in[...], preferred_element_type=jnp.float32)
    yield
    b_g = u[:, 0:CONV_WIDTH]
    c_g = u[:, CONV_WIDTH:2 * CONV_WIDTH]
    v_c = u[:, 2 * CONV_WIDTH:3 * CONV_WIDTH]
    v_p = u[:, 3 * CONV_WIDTH:]

    zbuf[pl.ds(CONV_HALO + r0, n), :] = c_g * v_c
    zext = zbuf[pl.ds(r0, n + CONV_HALO), :]
    z1 = pltpu.roll(zext, 1, axis=0)[CONV_HALO:, :]
    z2 = pltpu.roll(zext, 2, axis=0)[CONV_HALO:, :]
    conv = z2 * conv_w[0:1, :] + z1 * conv_w[1:2, :] + zext[CONV_HALO:, :] * conv_w[2:3, :] + conv_b[...]
    y_conv = b_g * conv

    vbuf[pl.ds(POOL_HALO + r0, n), :] = v_p
    vext = vbuf[pl.ds(r0, n + POOL_HALO), :]
    s2 = vext + pltpu.roll(vext, 1, axis=0)
    s4 = s2[:, POOL_GW:] + pltpu.roll(s2[:, POOL_GW:], 2, axis=0)
    s8 = s4[:, POOL_GW:] + pltpu.roll(s4[:, POOL_GW:], 4, axis=0)
    s16 = s8[:, POOL_GW:] + pltpu.roll(s8[:, POOL_GW:], 8, axis=0)
    wsums = (s2[POOL_HALO:, 0:POOL_GW], s4[POOL_HALO:, 0:POOL_GW],
             s8[POOL_HALO:, 0:POOL_GW], s16[POOL_HALO:, 0:POOL_GW])

    t_pos = seq0 + lax.broadcasted_iota(jnp.int32, (n, 1), 0)
    y_pool = []
    for j, w in enumerate(POOL_WINDOWS):
        inv_cnt = 1.0 / jnp.minimum(t_pos + 1, w).astype(jnp.float32)
        pooled = wsums[j] * inv_cnt - v_p[:, j * POOL_GW:(j + 1) * POOL_GW]
        y_pool.append(jnp.dot(pooled.astype(jnp.bfloat16), pool_w[j], preferred_element_type=jnp.float32))
    y_pool = jnp.concatenate(y_pool, axis=-1) * pool_scale[...]

    ycat = jnp.concatenate([y_conv, y_pool], axis=-1).astype(jnp.bfloat16)
    yield
    mix = jnp.dot(ycat, w_out[...], preferred_element_type=jnp.float32)
    yield
    h1 = _layernorm(DEEPNORM_ALPHA * h0 + mix, ln1_g[...], ln1_b[...])

    h_hi = h1.astype(jnp.bfloat16)
    h_lo = (h1 - h_hi.astype(jnp.float32)).astype(jnp.bfloat16)
    xp_ref[rows, :] = _pack_bf16_pairs(h1)
    h1_ref[rows, :] = h1
    hcat = jnp.concatenate([h_hi, h_lo, h_hi], axis=-1)
    yield
    logits = jnp.dot(hcat, w_r[...], preferred_element_type=jnp.float32) + b_r[...]
    yield
    lane = lax.broadcasted_iota(jnp.int32, (n, LANES), 1).astype(jnp.float32)
    neg = jnp.float32(-jnp.inf)

    def first_argmax(vals):
        m = jnp.max(vals, axis=-1, keepdims=True)
        idx = jnp.min(jnp.where(vals == m, lane, float(LANES)), axis=-1, keepdims=True)
        return m, idx

    g_mask = lane < N_GROUPS
    g_max, g_idx = first_argmax(jnp.where(g_mask, logits, neg))
    g_w = 1.0 / jnp.sum(jnp.where(g_mask, jnp.exp(logits - g_max), 0.0), axis=-1, keepdims=True)

    e_lo = N_GROUPS + EXPERTS_PER_GROUP * g_idx
    e_vals = jnp.where((lane >= e_lo) & (lane < e_lo + EXPERTS_PER_GROUP), logits, neg)
    m1, i1 = first_argmax(e_vals)
    m2, i2 = first_argmax(jnp.where(lane == i1, neg, e_vals))
    e21 = jnp.exp(m2 - m1)
    w1 = g_w / (1.0 + e21)
    w2 = g_w * e21 / (1.0 + e21)
    id1 = i1 - N_GROUPS
    id2 = i2 - N_GROUPS

    sel1 = lane == id1
    sel2 = lane == id2
    onehot = (sel1 | sel2).astype(jnp.float32)
    row = lax.broadcasted_iota(jnp.int32, (n, n), 0)
    col = lax.broadcasted_iota(jnp.int32, (n, n), 1)
    tri = (col < row).astype(jnp.bfloat16)
    before = jnp.dot(tri, onehot.astype(jnp.bfloat16), preferred_element_type=jnp.float32) + carry[...]
    rank1 = jnp.sum(jnp.where(sel1, before, 0.0), axis=-1, keepdims=True)
    rank2 = jnp.sum(jnp.where(sel2, before, 0.0), axis=-1, keepdims=True)
    carry[...] = carry[...] + jnp.sum(onehot, axis=0, keepdims=True)

    rec = jnp.zeros((n, LANES), jnp.float32)
    for k, val in ((R_ID1, id1), (R_ID2, id2), (R_W1, w1), (R_W2, w2), (R_RANK1, rank1), (R_RANK2, rank2)):
        rec = jnp.where(lane == k, val, rec)
    route_ref[rows, :] = rec
    route_t_ref[:, rows] = jnp.transpose(rec)[0:ROUTE_ROWS, :]


def _run_mixer(x2, lnin_g, lnin_b, w_in, conv_w, conv_b, pool_w, pool_scale, w_out, ln1_g, ln1_b,
               w_r, b_r, batch, seq):
    n_tok = batch * seq
    n_s = seq // SEQ_TILE
    tok_map = lambda b, s: (b * n_s + s, 0)

    def const(shape):
        return pl.BlockSpec(shape, lambda b, s: (0,) * len(shape), pipeline_mode=pl.Buffered(1))

    in_specs = [
        pl.BlockSpec((SEQ_TILE, D_MODEL), tok_map),
        const((1, D_MODEL)), const((1, D_MODEL)),
        const((D_MODEL, IN_PROJ)),
        const((CONV_K, CONV_WIDTH)), const((1, CONV_WIDTH)),
        const((len(POOL_WINDOWS), POOL_GW, POOL_GW)), const((1, POOL_WIDTH)),
        const((D_MODEL, D_MODEL)),
        const((1, D_MODEL)), const((1, D_MODEL)),
        const((3 * D_MODEL, LANES)), const((1, LANES)),
    ]
    out_specs = [
        pl.BlockSpec((SEQ_TILE, HALF), tok_map),
        pl.BlockSpec((SEQ_TILE, D_MODEL), tok_map),
        pl.BlockSpec((SEQ_TILE, LANES), tok_map),
        pl.BlockSpec((ROUTE_ROWS, SEQ_TILE), lambda b, s: (0, b * n_s + s)),
        pl.BlockSpec((1, LANES), lambda b, s: (0, 0)),
    ]
    out_shape = [
        jax.ShapeDtypeStruct((n_tok, HALF), jnp.uint32),
        jax.ShapeDtypeStruct((n_tok, D_MODEL), jnp.float32),
        jax.ShapeDtypeStruct((n_tok, LANES), jnp.float32),
        jax.ShapeDtypeStruct((ROUTE_ROWS, n_tok), jnp.float32),
        jax.ShapeDtypeStruct((1, LANES), jnp.float32),
    ]
    return pl.pallas_call(
        _mixer_kernel,
        grid=(batch, n_s),
        in_specs=in_specs,
        out_specs=out_specs,
        out_shape=out_shape,
        scratch_shapes=[
            pltpu.VMEM((SEQ_TILE + CONV_HALO, CONV_WIDTH), jnp.float32),
            pltpu.VMEM((SEQ_TILE + POOL_HALO, POOL_WIDTH), jnp.float32),
            pltpu.VMEM((1, LANES), jnp.float32),
        ],
        compiler_params=pltpu.CompilerParams(
            dimension_semantics=("arbitrary", "arbitrary"), vmem_limit_bytes=VMEM_LIMIT),
    )(x2, lnin_g, lnin_b, w_in, conv_w, conv_b, pool_w, pool_scale, w_out, ln1_g, ln1_b, w_r, b_r)


def _plan_kernel(rt_ref, counts_ref, pos_ref, tile_start_ref, tiles_ref, pad_ref, *, n_rows):
    lane = lax.broadcasted_iota(jnp.int32, (ROUTE_ROWS, LANES), 1)
    counts = jnp.broadcast_to(counts_ref[...], (ROUTE_ROWS, LANES))
    tiles = jnp.floor((counts + (ROW_TILE - 1)) * (1.0 / ROW_TILE))
    tile_end = tiles
    shift = 1
    while shift < N_EXPERTS:
        tile_end = tile_end + jnp.where(lane >= shift, pltpu.roll(tile_end, shift, axis=1), 0.0)
        shift *= 2
    row_start = (tile_end - tiles) * ROW_TILE
    tile_start_ref[...] = (tile_end - tiles)[0:1, :].astype(jnp.int32)
    tiles_ref[...] = tiles[0:1, :].astype(jnp.int32)

    rt = rt_ref[...]
    ids = rt[R_ID1:R_ID2 + 1, :]
    start = jnp.zeros_like(ids)
    for e in range(N_EXPERTS):
        start = jnp.where(ids == e, row_start[0:1, e:e + 1], start)
    pos_ref[...] = (start + rt[R_RANK1:R_RANK2 + 1, :]).astype(jnp.int32)

    sub = lax.broadcasted_iota(jnp.int32, (N_EXPERTS, LANES), 0)
    lane_e = lax.broadcasted_iota(jnp.int32, (N_EXPERTS, LANES), 1)
    diag = sub == lane_e
    pad_lo = jnp.sum(jnp.where(diag, (row_start + counts)[0:1, :], 0.0), axis=1, keepdims=True)
    pad_n = jnp.sum(jnp.where(diag, (tiles * ROW_TILE - counts)[0:1, :], 0.0), axis=1, keepdims=True)
    j = lax.broadcasted_iota(jnp.int32, (N_EXPERTS, ROW_TILE), 1).astype(jnp.float32)
    pad_ref[...] = jnp.where(j < pad_n, pad_lo + j, n_rows + j).astype(jnp.int32)


def _run_plan(route_t, counts, n_rows):
    n_tok = route_t.shape[1]
    full = lambda shape: pl.BlockSpec(shape, lambda i: (0, 0))
    return pl.pallas_call(
        functools.partial(_plan_kernel, n_rows=n_rows),
        grid=(1,),
        in_specs=[full((ROUTE_ROWS, n_tok)), full((1, LANES))],
        out_specs=[full((2, n_tok)), full((1, LANES)), full((1, LANES)), full((N_EXPERTS, ROW_TILE))],
        out_shape=[
            jax.ShapeDtypeStruct((2, n_tok), jnp.int32),
            jax.ShapeDtypeStruct((1, LANES), jnp.int32),
            jax.ShapeDtypeStruct((1, LANES), jnp.int32),
            jax.ShapeDtypeStruct((N_EXPERTS, ROW_TILE), jnp.int32),
        ],
        compiler_params=pltpu.CompilerParams(dimension_semantics=("arbitrary",)),
    )(route_t, counts)


def _sc_mesh():
    return plsc.VectorSubcoreMesh(core_axis_name="core", subcore_axis_name="subcore")


def _sc_worker_id():
    return lax.axis_index("core") * (SC_WORKERS // 2) + lax.axis_index("subcore")


def _dispatch_rows(xp, pos1, pos2, pad_pos, zero_rows, n_rows):
    n_tok, width = xp.shape
    n_win = n_tok // SC_WORKERS // SC_WINDOW
    n_pad = pad_pos.shape[0] // SC_WORKERS // SC_WINDOW
    as_windows = lambda v: v.reshape(-1, SC_WINDOW)

    @functools.partial(
        pl.kernel, out_type=jax.ShapeDtypeStruct((n_rows + ROW_TILE, width), xp.dtype), mesh=_sc_mesh(),
        scratch_types=[pltpu.VMEM((n_win, SC_WINDOW), jnp.int32), pltpu.VMEM((n_win, SC_WINDOW), jnp.int32),
                       pltpu.VMEM((n_pad, SC_WINDOW), jnp.int32),
                       pltpu.VMEM((2, SC_WINDOW, width), xp.dtype),
                       pltpu.SemaphoreType.DMA((2,)), pltpu.SemaphoreType.DMA((2,))])
    def dispatch(xp_hbm, pos1_hbm, pos2_hbm, pad_hbm, zero_hbm, out_hbm, idx1, idx2, idxp, buf, lsem, ssem):
        wid = _sc_worker_id()
        pltpu.sync_copy(pos1_hbm.at[pl.ds(wid * n_win, n_win)], idx1)
        pltpu.sync_copy(pos2_hbm.at[pl.ds(wid * n_win, n_win)], idx2)
        pltpu.sync_copy(pad_hbm.at[pl.ds(wid * n_pad, n_pad)], idxp)

        def load(j):
            rows = xp_hbm.at[pl.ds((wid * n_win + j) * SC_WINDOW, SC_WINDOW)]
            return pltpu.make_async_copy(rows, buf.at[j % 2], lsem.at[j % 2])

        def scatters(j):
            return [pltpu.make_async_copy(buf.at[j % 2], out_hbm.at[idx.at[j]], ssem.at[j % 2])
                    for idx in (idx1, idx2)]

        load(0).start()
        for j in range(n_win):
            load(j).wait()
            for cp in scatters(j):
                cp.start()
            if j >= 1:
                for cp in scatters(j - 1):
                    cp.wait()
            if j + 1 < n_win:
                load(j + 1).start()
        for cp in scatters(n_win - 1):
            cp.wait()

        pltpu.sync_copy(zero_hbm, buf.at[0])
        pads = [pltpu.make_async_copy(buf.at[0], out_hbm.at[idxp.at[j]], ssem.at[0]) for j in range(n_pad)]
        for cp in pads:
            cp.start()
        for cp in pads:
            cp.wait()

    return dispatch(xp, as_windows(pos1), as_windows(pos2), as_windows(pad_pos), zero_rows)


def _gather_rows(src, idx):
    n_out, width = idx.shape[0], src.shape[1]
    n_win = n_out // SC_WORKERS // SC_WINDOW

    @functools.partial(
        pl.kernel, out_type=jax.ShapeDtypeStruct((n_out, width), src.dtype), mesh=_sc_mesh(),
        scratch_types=[pltpu.VMEM((n_win, SC_WINDOW), jnp.int32), pltpu.VMEM((2, SC_WINDOW, width), src.dtype),
                       pltpu.SemaphoreType.DMA((2,)), pltpu.SemaphoreType.DMA((2,))])
    def gather(src_hbm, idx_hbm, dst_hbm, idx_v, buf, gsem, ssem):
        wid = _sc_worker_id()
        pltpu.sync_copy(idx_hbm.at[pl.ds(wid * n_win, n_win)], idx_v)

        def fetch(j):
            return pltpu.make_async_copy(src_hbm.at[idx_v.at[j]], buf.at[j % 2], gsem.at[j % 2])

        def store(j):
            rows = dst_hbm.at[pl.ds((wid * n_win + j) * SC_WINDOW, SC_WINDOW)]
            return pltpu.make_async_copy(buf.at[j % 2], rows, ssem.at[j % 2])

        fetch(0).start()
        for j in range(n_win):
            fetch(j).wait()
            store(j).start()
            if j >= 1:
                store(j - 1).wait()
            if j + 1 < n_win:
                fetch(j + 1).start()
        store(n_win - 1).wait()

    return gather(src, idx.reshape(-1, SC_WINDOW))


def _tile_copy(hbm, buf, sem, tile, slot, to_hbm):
    rows = hbm.at[pl.ds(pl.multiple_of(tile * ROW_TILE, ROW_TILE), ROW_TILE)]
    if to_hbm:
        return pltpu.make_async_copy(buf.at[slot], rows, sem.at[slot])
    return pltpu.make_async_copy(rows, buf.at[slot], sem.at[slot])


def _weight_copies(w_hbm, wbuf, wsem, expert):
    slot = expert % WEIGHT_BUFFERS
    return [pltpu.make_async_copy(w.at[expert], buf.at[slot], wsem.at[slot]) for w, buf in zip(w_hbm, wbuf)]


def _expert_kernel(ts_ref, nte_ref, x_hbm, wg_hbm, wu_hbm, wd_hbm, y_hbm,
                   xbuf, ybuf, xsem, ysem, wg_buf, wu_buf, wd_buf, wsem, wgu_bf, wd_bf, act_ref, *, max_tiles):
    e = pl.program_id(0)
    first = ts_ref[e]
    count = nte_ref[e]
    n_tiles = ts_ref[N_EXPERTS - 1] + nte_ref[N_EXPERTS - 1]
    ahead = TILE_BUFFERS - 1
    w_hbm = (wg_hbm, wu_hbm, wd_hbm)
    wbuf = (wg_buf, wu_buf, wd_buf)

    @pl.when(e == 0)
    def _():
        for g in range(ahead):
            @pl.when(g < n_tiles)
            def _():
                _tile_copy(x_hbm, xbuf, xsem, g, g, False).start()
        for k in range(WEIGHT_BUFFERS - 1):
            for cp in _weight_copies(w_hbm, wbuf, wsem, k):
                cp.start()

    @pl.when(e + WEIGHT_BUFFERS - 1 < N_EXPERTS)
    def _():
        for cp in _weight_copies(w_hbm, wbuf, wsem, e + WEIGHT_BUFFERS - 1):
            cp.start()

    for cp in _weight_copies(w_hbm, wbuf, wsem, e):
        cp.wait()

    def sync_x(g):
        _tile_copy(x_hbm, xbuf, xsem, g, g % TILE_BUFFERS, False).wait()

        @pl.when(g + ahead < n_tiles)
        def _():
            _tile_copy(x_hbm, xbuf, xsem, g + ahead, (g + ahead) % TILE_BUFFERS, False).start()

    def sync_y_slot(g):
        @pl.when(g >= TILE_BUFFERS)
        def _():
            _tile_copy(y_hbm, ybuf, ysem, g - TILE_BUFFERS, g % TILE_BUFFERS, True).wait()

    def up_proj(g):
        lo, hi = _unpack_bf16_pairs(xbuf[g % TILE_BUFFERS])
        return (jnp.dot(lo.astype(jnp.bfloat16), wgu_bf[0:HALF, :], preferred_element_type=jnp.float32)
                + jnp.dot(hi.astype(jnp.bfloat16), wgu_bf[HALF:, :], preferred_element_type=jnp.float32))

    def put_act(hgu):
        hg = hgu[:, 0:D_EXPERT]
        act_ref[...] = (hg * jax.nn.sigmoid(hg) * hgu[:, D_EXPERT:]).astype(jnp.bfloat16)

    def down_proj(g):
        y = jnp.dot(act_ref[...], wd_bf[...], preferred_element_type=jnp.float32)
        ybuf[g % TILE_BUFFERS] = _pack_bf16_pairs(y)

    @pl.when(count > 0)
    def _():
        slot = e % WEIGHT_BUFFERS
        wgu_bf[:, 0:D_EXPERT] = wg_buf[slot].astype(jnp.bfloat16)
        wgu_bf[:, D_EXPERT:] = wu_buf[slot].astype(jnp.bfloat16)
        wd_bf[...] = wd_buf[slot].astype(jnp.bfloat16)

        sync_x(first)
        put_act(up_proj(first))

        def tile_body(g, c):
            sync_x(g)
            sync_y_slot(g - 1)
            down_proj(g - 1)
            hgu = up_proj(g)
            put_act(hgu)
            _tile_copy(y_hbm, ybuf, ysem, g - 1, (g - 1) % TILE_BUFFERS, True).start()
            return c

        lax.fori_loop(first + 1, first + count, tile_body, 0)
        last = first + count - 1
        sync_y_slot(last)
        down_proj(last)
        _tile_copy(y_hbm, ybuf, ysem, last, last % TILE_BUFFERS, True).start()

    @pl.when(e == N_EXPERTS - 1)
    def _():
        for k in range(TILE_BUFFERS, 0, -1):
            @pl.when(n_tiles >= k)
            def _():
                _tile_copy(y_hbm, ybuf, ysem, n_tiles - k, (n_tiles - k) % TILE_BUFFERS, True).wait()

        ybuf[0] = jnp.zeros((ROW_TILE, HALF), jnp.uint32)

        def fill(g, c):
            cp = _tile_copy(y_hbm, ybuf, ysem, g, 0, True)
            cp.start()
            cp.wait()
            return c

        lax.fori_loop(n_tiles, max_tiles, fill, 0)


def _run_experts(tile_start, tiles_per_expert, x_sorted, w_gate, w_up, w_down):
    n_rows = x_sorted.shape[0]
    max_tiles = n_rows // ROW_TILE
    hbm = pl.BlockSpec(memory_space=pl.ANY)

    grid_spec = pltpu.PrefetchScalarGridSpec(
        num_scalar_prefetch=2,
        grid=(N_EXPERTS,),
        in_specs=[hbm, hbm, hbm, hbm],
        out_specs=hbm,
        scratch_shapes=[
            pltpu.VMEM((TILE_BUFFERS, ROW_TILE, HALF), jnp.uint32),
            pltpu.VMEM((TILE_BUFFERS, ROW_TILE, HALF), jnp.uint32),
            pltpu.SemaphoreType.DMA((TILE_BUFFERS,)),
            pltpu.SemaphoreType.DMA((TILE_BUFFERS,)),
            pltpu.VMEM((WEIGHT_BUFFERS, D_MODEL, D_EXPERT), jnp.float32),
            pltpu.VMEM((WEIGHT_BUFFERS, D_MODEL, D_EXPERT), jnp.float32),
            pltpu.VMEM((WEIGHT_BUFFERS, D_EXPERT, D_MODEL), jnp.float32),
            pltpu.SemaphoreType.DMA((WEIGHT_BUFFERS,)),
            pltpu.VMEM((D_MODEL, 2 * D_EXPERT), jnp.bfloat16),
            pltpu.VMEM((D_EXPERT, D_MODEL), jnp.bfloat16),
            pltpu.VMEM((ROW_TILE, D_EXPERT), jnp.bfloat16),
        ],
    )
    return pl.pallas_call(
        functools.partial(_expert_kernel, max_tiles=max_tiles),
        grid_spec=grid_spec,
        out_shape=jax.ShapeDtypeStruct((n_rows, HALF), jnp.uint32),
        compiler_params=pltpu.CompilerParams(
            dimension_semantics=("arbitrary",), vmem_limit_bytes=VMEM_LIMIT),
    )(tile_start, tiles_per_expert, x_sorted, w_gate, w_up, w_down)


def _combine_rows(r0, h1_ref, p_ref, route_ref, ya_ref, yb_ref, w_pg, b_pg, w_ple, g_ref, b_ref, o_ref):
    rows = pl.ds(r0, SUB_TILE)
    h1 = h1_ref[rows, :]
    h_hi = h1.astype(jnp.bfloat16)
    p_b = p_ref[rows, :].astype(jnp.bfloat16)
    yield
    gate_pre = jnp.dot(h_hi, w_pg[...], preferred_element_type=jnp.float32)
    ple_pre = jnp.dot(p_b, w_ple[...], preferred_element_type=jnp.float32)
    yield
    ple = ple_pre * jax.nn.sigmoid(gate_pre + b_pg[...])
    route = route_ref[rows, :]
    w1 = route[:, R_W1:R_W1 + 1]
    w2 = route[:, R_W2:R_W2 + 1]
    a_lo, a_hi = _unpack_bf16_pairs(ya_ref[rows, :])
    b_lo, b_hi = _unpack_bf16_pairs(yb_ref[rows, :])
    moe = jnp.concatenate([w1 * a_lo + w2 * b_lo, w1 * a_hi + w2 * b_hi], axis=-1)
    o_ref[rows, :] = _layernorm(DEEPNORM_ALPHA * h1 + ple + moe, g_ref[...], b_ref[...])


def _combine_kernel(h1_ref, p_ref, route_ref, ya_ref, yb_ref, w_pg, b_pg, w_ple, g_ref, b_ref, *rest):
    o_ref = rest[-1]
    chains = [_combine_rows(r0, h1_ref, p_ref, route_ref, ya_ref, yb_ref, w_pg, b_pg, w_ple, g_ref, b_ref, o_ref)
              for r0 in range(0, h1_ref.shape[0], SUB_TILE)]
    for t in range(COMBINE_PHASES + len(chains) - 1):
        for k, chain in enumerate(chains):
            if 0 <= t - k < COMBINE_PHASES:
                next(chain, None)


def _run_combine(h1, p2, route, y_tok, w_pg, b_pg, w_ple, ln2_g, ln2_b, out_prev, part):
    n_tok_total = h1.shape[0]
    n_t = y_tok.shape[0] // 2 // TOKEN_TILE
    tok_map = lambda i: (part * n_t + i, 0)
    const = lambda shape: pl.BlockSpec(shape, lambda i: (0, 0), pipeline_mode=pl.Buffered(1))
    in_specs = [
        pl.BlockSpec((TOKEN_TILE, D_MODEL), tok_map),
        pl.BlockSpec((TOKEN_TILE, PLE_DIM), tok_map),
        pl.BlockSpec((TOKEN_TILE, LANES), tok_map),
        pl.BlockSpec((TOKEN_TILE, HALF), lambda i: (i, 0)),
        pl.BlockSpec((TOKEN_TILE, HALF), lambda i: (i + n_t, 0)),
        const((D_MODEL, D_MODEL)), const((1, D_MODEL)), const((PLE_DIM, D_MODEL)),
        const((1, D_MODEL)), const((1, D_MODEL)),
    ]
    args = [h1, p2, route, y_tok, y_tok, w_pg, b_pg, w_ple, ln2_g, ln2_b]
    aliases = {}
    if out_prev is not None:
        in_specs.append(pl.BlockSpec(memory_space=pl.ANY))
        args.append(out_prev)
        aliases = {len(args) - 1: 0}
    return pl.pallas_call(
        _combine_kernel,
        grid=(n_t,),
        in_specs=in_specs,
        out_specs=pl.BlockSpec((TOKEN_TILE, D_MODEL), tok_map),
        out_shape=jax.ShapeDtypeStruct((n_tok_total, D_MODEL), jnp.float32),
        input_output_aliases=aliases,
        compiler_params=pltpu.CompilerParams(dimension_semantics=("arbitrary",), vmem_limit_bytes=VMEM_LIMIT),
    )(*args)


def _split_bf16(w):
    hi = w.astype(jnp.bfloat16)
    lo = (w - hi.astype(jnp.float32)).astype(jnp.bfloat16)
    return hi, lo


def kernel(x, p, ln_in_g, ln_in_b, w_in, conv_w, conv_b, pool_w, pool_scale, w_out, ln1_g, ln1_b,
           w_rg, b_rg, w_re, b_re, w_gate, w_up, w_down, w_pg, b_pg, w_ple, ln2_g, ln2_b):
    batch, seq, _ = x.shape
    n_tok = batch * seq
    bf = jnp.bfloat16
    row = lambda v: v.reshape(1, -1)

    w_r = jnp.concatenate([w_rg[0], jnp.transpose(w_re[0], (1, 0, 2)).reshape(D_MODEL, N_EXPERTS)], axis=1)
    w_r = jnp.pad(w_r, ((0, 0), (0, LANES - w_r.shape[1])))
    w_r_hi, w_r_lo = _split_bf16(w_r)
    w_r_cat = jnp.concatenate([w_r_hi, w_r_hi, w_r_lo], axis=0)
    b_r = jnp.pad(jnp.concatenate([b_rg[0], b_re[0].reshape(-1)]), (0, LANES - N_GROUPS - N_EXPERTS)).reshape(1, LANES)

    x2 = x.reshape(n_tok, D_MODEL)
    p2 = p[0].reshape(n_tok, PLE_DIM)
    mixer_weights = (row(ln_in_g), row(ln_in_b), w_in[0].astype(bf), conv_w[0], row(conv_b[0]),
                     pool_w[0].astype(bf), row(pool_scale[0]), w_out[0].astype(bf), row(ln1_g[0]), row(ln1_b[0]),
                     w_r_cat, b_r)
    combine_weights = (w_pg[0].astype(bf), row(b_pg[0]), w_ple[0].astype(bf), row(ln2_g[0]), row(ln2_b[0]))
    expert_weights = (w_gate[0].reshape(N_EXPERTS, D_MODEL, D_EXPERT),
                      w_up[0].reshape(N_EXPERTS, D_MODEL, D_EXPERT),
                      w_down[0].reshape(N_EXPERTS, D_EXPERT, D_MODEL))
    zero_rows = jnp.zeros((SC_WINDOW, HALF), jnp.uint32)

    n_rows = -(-(2 * n_tok + N_EXPERTS * (ROW_TILE - 1)) // ROW_TILE) * ROW_TILE
    xp, h1, route, route_t, counts = _run_mixer(x2, *mixer_weights, batch, seq)
    pos, tile_start, tiles_per_expert, pad_pos = _run_plan(route_t, counts, n_rows)
    x_sorted = _dispatch_rows(xp, pos[0], pos[1], pad_pos.reshape(-1), zero_rows, n_rows)
    y_sorted = _run_experts(tile_start[0, :N_EXPERTS], tiles_per_expert[0, :N_EXPERTS], x_sorted, *expert_weights)

    part_tok = n_tok // COMBINE_PARTS
    out = None
    for part in range(COMBINE_PARTS):
        idx = pos[:, part * part_tok:(part + 1) * part_tok].reshape(-1)
        y_tok = _gather_rows(y_sorted, idx)
        out = _run_combine(h1, p2, route, y_tok, *combine_weights, out, part)
    return out.reshape(batch, seq, D_MODEL)
```

```python
import functools

import jax
import jax.numpy as jnp
from jax import lax
from jax.experimental import pallas as pl
from jax.experimental.pallas import tpu as pltpu
from jax.experimental.pallas import tpu_sc as plsc

D_MODEL = 1024
CONV_WIDTH = 512
CONV_K = 3
POOL_WIDTH = 512
POOL_WINDOWS = (2, 4, 8, 16)
POOL_GW = 128
IN_PROJ = 3 * CONV_WIDTH + POOL_WIDTH
N_GROUPS = 4
EXPERTS_PER_GROUP = 8
N_EXPERTS = N_GROUPS * EXPERTS_PER_GROUP
D_EXPERT = 256
PLE_DIM = 256
LN_EPS = 1e-5
DEEPNORM_ALPHA = 2.0 ** 0.25

LANES = 128
HALF = D_MODEL // 2
CONV_HALO = 8
POOL_HALO = 16
SEQ_TILE = 1024
SUB_TILE = 256
MIXER_PHASES = 7
COMBINE_PHASES = 3
ROW_TILE = 256
TILE_BUFFERS = 8
SC_WORKERS = 32
SC_WINDOW = 64
WEIGHT_BUFFERS = 3
TOKEN_TILE = 1024
VMEM_LIMIT = 56 * 1024 * 1024

R_ID1, R_ID2, R_RANK1, R_RANK2, R_W1, R_W2 = range(6)
ROUTE_ROWS = 8
LOGIT_ROWS = 40


def _layernorm(x, g, b):
    mu = jnp.mean(x, axis=-1, keepdims=True)
    xc = x - mu
    var = jnp.mean(xc * xc, axis=-1, keepdims=True)
    return xc * lax.rsqrt(var + LN_EPS) * g + b


def _pack_bf16_pairs(v):
    bits = lax.bitcast_convert_type(v.astype(jnp.bfloat16).astype(jnp.float32), jnp.uint32)
    return bits[:, HALF:] | (bits[:, :HALF] >> 16)


def _unpack_bf16_pairs(w):
    lo = lax.bitcast_convert_type(w << 16, jnp.float32)
    hi = lax.bitcast_convert_type(w & jnp.uint32(0xFFFF0000), jnp.float32)
    return lo, hi


def _mixer_kernel(x_ref, lnin_g, lnin_b, w_in, conv_w, conv_b, pool_w, pool_scale, w_out,
                  ln1_g, ln1_b, w_r, b_r,
                  xp_ref, h1_ref, route_ref, route_t_ref, counts_ref,
                  zbuf, vbuf, carry):
    b = pl.program_id(0)
    s = pl.program_id(1)
    ts = x_ref.shape[0]

    @pl.when(s == 0)
    def _():
        zbuf[0:CONV_HALO, :] = jnp.zeros((CONV_HALO, CONV_WIDTH), jnp.float32)
        vbuf[0:POOL_HALO, :] = jnp.zeros((POOL_HALO, POOL_WIDTH), jnp.float32)

    @pl.when((b == 0) & (s == 0))
    def _():
        carry[...] = jnp.zeros_like(carry)

    chains = [_mixer_rows(r0, s * ts + r0, x_ref, lnin_g, lnin_b, w_in, conv_w, conv_b, pool_w, pool_scale,
                          w_out, ln1_g, ln1_b, w_r, b_r,
                          xp_ref, h1_ref, route_ref, route_t_ref, zbuf, vbuf, carry)
              for r0 in range(0, ts, SUB_TILE)]
    for t in range(MIXER_PHASES + len(chains) - 1):
        for k, chain in enumerate(chains):
            if 0 <= t - k < MIXER_PHASES:
                next(chain, None)
    zbuf[0:CONV_HALO, :] = zbuf[ts:ts + CONV_HALO, :]
    vbuf[0:POOL_HALO, :] = vbuf[ts:ts + POOL_HALO, :]
    counts_ref[...] = jnp.transpose(carry[...])[0:1, :]


def _mixer_rows(r0, seq0, x_ref, lnin_g, lnin_b, w_in, conv_w, conv_b, pool_w, pool_scale, w_out,
                ln1_g, ln1_b, w_r, b_r,
                xp_ref, h1_ref, route_ref, route_t_ref, zbuf, vbuf, carry):
    n = SUB_TILE
    rows = pl.ds(r0, n)
    h0 = _layernorm(x_ref[rows, :], lnin_g[...], lnin_b[...])
    h0b = h0.astype(jnp.bfloat16)
    yield
    u = jnp.dot(h0b, w_in[...], preferred_element_type=jnp.float32)
    yield
    b_g = u[:, 0:CONV_WIDTH]
    c_g = u[:, CONV_WIDTH:2 * CONV_WIDTH]
    v_c = u[:, 2 * CONV_WIDTH:3 * CONV_WIDTH]
    v_p = u[:, 3 * CONV_WIDTH:]

    zbuf[pl.ds(CONV_HALO + r0, n), :] = c_g * v_c
    zext = zbuf[pl.ds(r0, n + CONV_HALO), :]
    z1 = pltpu.roll(zext, 1, axis=0)[CONV_HALO:, :]
    z2 = pltpu.roll(zext, 2, axis=0)[CONV_HALO:, :]
    conv = z2 * conv_w[0:1, :] + z1 * conv_w[1:2, :] + zext[CONV_HALO:, :] * conv_w[2:3, :] + conv_b[...]
    y_conv = b_g * conv

    vbuf[pl.ds(POOL_HALO + r0, n), :] = v_p
    vext = vbuf[pl.ds(r0, n + POOL_HALO), :]
    s2 = vext + pltpu.roll(vext, 1, axis=0)
    s4 = s2[:, POOL_GW:] + pltpu.roll(s2[:, POOL_GW:], 2, axis=0)
    s8 = s4[:, POOL_GW:] + pltpu.roll(s4[:, POOL_GW:], 4, axis=0)
    s16 = s8[:, POOL_GW:] + pltpu.roll(s8[:, POOL_GW:], 8, axis=0)
    wsums = (s2[POOL_HALO:, 0:POOL_GW], s4[POOL_HALO:, 0:POOL_GW],
             s8[POOL_HALO:, 0:POOL_GW], s16[POOL_HALO:, 0:POOL_GW])

    t_pos = seq0 + lax.broadcasted_iota(jnp.int32, (n, 1), 0)
    y_pool = []
    for j, w in enumerate(POOL_WINDOWS):
        inv_cnt = 1.0 / jnp.minimum(t_pos + 1, w).astype(jnp.float32)
        pooled = wsums[j] * inv_cnt - v_p[:, j * POOL_GW:(j + 1) * POOL_GW]
        y_pool.append(jnp.dot(pooled.astype(jnp.bfloat16), pool_w[j], preferred_element_type=jnp.float32))
    y_pool = jnp.concatenate(y_pool, axis=-1) * pool_scale[...]

    ycat = jnp.concatenate([y_conv, y_pool], axis=-1).astype(jnp.bfloat16)
    yield
    mix = jnp.dot(ycat, w_out[...], preferred_element_type=jnp.float32)
    yield
    h1 = _layernorm(DEEPNORM_ALPHA * h0 + mix, ln1_g[...], ln1_b[...])

    h_hi = h1.astype(jnp.bfloat16)
    h_lo = (h1 - h_hi.astype(jnp.float32)).astype(jnp.bfloat16)
    xp_ref[rows, :] = _pack_bf16_pairs(h1)
    h1_ref[rows, :] = h1
    hcat = jnp.concatenate([h_hi, h_lo, h_hi], axis=-1)
    yield
    logits = jnp.dot(hcat, w_r[...], preferred_element_type=jnp.float32) + b_r[...]
    yield
    lt = jnp.transpose(logits)[0:LOGIT_ROWS, :]
    rid = lax.broadcasted_iota(jnp.int32, (LOGIT_ROWS, n), 0).astype(jnp.float32)
    neg = jnp.float32(-jnp.inf)

    def first_argmax(vals):
        m = jnp.max(vals, axis=0, keepdims=True)
        idx = jnp.min(jnp.where(vals == m, rid, float(LOGIT_ROWS)), axis=0, keepdims=True)
        return m, idx

    g_mask = rid < N_GROUPS
    g_max, g_idx = first_argmax(jnp.where(g_mask, lt, neg))
    g_w = 1.0 / jnp.sum(jnp.where(g_mask, jnp.exp(lt - g_max), 0.0), axis=0, keepdims=True)

    e_lo = N_GROUPS + EXPERTS_PER_GROUP * g_idx
    e_vals = jnp.where((rid >= e_lo) & (rid < e_lo + EXPERTS_PER_GROUP), lt, neg)
    m1, i1 = first_argmax(e_vals)
    m2, i2 = first_argmax(jnp.where(rid == i1, neg, e_vals))
    e21 = jnp.exp(m2 - m1)
    w1 = g_w / (1.0 + e21)
    w2 = g_w * e21 / (1.0 + e21)
    id1 = i1 - N_GROUPS
    id2 = i2 - N_GROUPS

    eid = lax.broadcasted_iota(jnp.int32, (N_EXPERTS, n), 0).astype(jnp.float32)
    sel1 = eid == id1
    sel2 = eid == id2
    onehot = (sel1 | sel2).astype(jnp.float32)
    src = lax.broadcasted_iota(jnp.int32, (n, n), 0)
    dst = lax.broadcasted_iota(jnp.int32, (n, n), 1)
    earlier = (src < dst).astype(jnp.bfloat16)
    before = (jnp.dot(onehot.astype(jnp.bfloat16), earlier, preferred_element_type=jnp.float32)
              + carry[0:N_EXPERTS, 0:1])
    rank1 = jnp.sum(jnp.where(sel1, before, 0.0), axis=0, keepdims=True)
    rank2 = jnp.sum(jnp.where(sel2, before, 0.0), axis=0, keepdims=True)
    carry[0:N_EXPERTS, :] = carry[0:N_EXPERTS, :] + jnp.sum(onehot, axis=1, keepdims=True)

    rec_t = jnp.zeros((ROUTE_ROWS, n), jnp.float32)
    rec_row = lax.broadcasted_iota(jnp.int32, (ROUTE_ROWS, n), 0)
    for k, val in ((R_ID1, id1), (R_ID2, id2), (R_W1, w1), (R_W2, w2), (R_RANK1, rank1), (R_RANK2, rank2)):
        rec_t = jnp.where(rec_row == k, val, rec_t)
    route_t_ref[:, rows] = rec_t
    padded = jnp.concatenate([rec_t, jnp.zeros((LANES - ROUTE_ROWS, n), jnp.float32)], axis=0)
    route_ref[rows, :] = jnp.transpose(padded)


def _run_mixer(x2, lnin_g, lnin_b, w_in, conv_w, conv_b, pool_w, pool_scale, w_out, ln1_g, ln1_b,
               w_r, b_r, batch, seq):
    n_tok = batch * seq
    n_s = seq // SEQ_TILE
    tok_map = lambda b, s: (b * n_s + s, 0)

    def const(shape):
        return pl.BlockSpec(shape, lambda b, s: (0,) * len(shape), pipeline_mode=pl.Buffered(1))

    in_specs = [
        pl.BlockSpec((SEQ_TILE, D_MODEL), tok_map),
        const((1, D_MODEL)), const((1, D_MODEL)),
        const((D_MODEL, IN_PROJ)),
        const((CONV_K, CONV_WIDTH)), const((1, CONV_WIDTH)),
        const((len(POOL_WINDOWS), POOL_GW, POOL_GW)), const((1, POOL_WIDTH)),
        const((D_MODEL, D_MODEL)),
        const((1, D_MODEL)), const((1, D_MODEL)),
        const((3 * D_MODEL, LANES)), const((1, LANES)),
    ]
    out_specs = [
        pl.BlockSpec((SEQ_TILE, HALF), tok_map),
        pl.BlockSpec((SEQ_TILE, D_MODEL), tok_map),
        pl.BlockSpec((SEQ_TILE, LANES), tok_map),
        pl.BlockSpec((ROUTE_ROWS, SEQ_TILE), lambda b, s: (0, b * n_s + s)),
        pl.BlockSpec((1, LANES), lambda b, s: (0, 0)),
    ]
    out_shape = [
        jax.ShapeDtypeStruct((n_tok, HALF), jnp.uint32),
        jax.ShapeDtypeStruct((n_tok, D_MODEL), jnp.float32),
        jax.ShapeDtypeStruct((n_tok, LANES), jnp.float32),
        jax.ShapeDtypeStruct((ROUTE_ROWS, n_tok), jnp.float32),
        jax.ShapeDtypeStruct((1, LANES), jnp.float32),
    ]
    return pl.pallas_call(
        _mixer_kernel,
        grid=(batch, n_s),
        in_specs=in_specs,
        out_specs=out_specs,
        out_shape=out_shape,
        scratch_shapes=[
            pltpu.VMEM((SEQ_TILE + CONV_HALO, CONV_WIDTH), jnp.float32),
            pltpu.VMEM((SEQ_TILE + POOL_HALO, POOL_WIDTH), jnp.float32),
            pltpu.VMEM((LANES, LANES), jnp.float32),
        ],
        compiler_params=pltpu.CompilerParams(
            dimension_semantics=("arbitrary", "arbitrary"), vmem_limit_bytes=VMEM_LIMIT),
    )(x2, lnin_g, lnin_b, w_in, conv_w, conv_b, pool_w, pool_scale, w_out, ln1_g, ln1_b, w_r, b_r)


def _plan_kernel(rt_ref, counts_ref, pos_ref, tile_start_ref, tiles_ref, pad_ref, *, n_rows):
    lane = lax.broadcasted_iota(jnp.int32, (ROUTE_ROWS, LANES), 1)
    counts = jnp.broadcast_to(counts_ref[...], (ROUTE_ROWS, LANES))
    tiles = jnp.floor((counts + (ROW_TILE - 1)) * (1.0 / ROW_TILE))
    tile_end = tiles
    shift = 1
    while shift < N_EXPERTS:
        tile_end = tile_end + jnp.where(lane >= shift, pltpu.roll(tile_end, shift, axis=1), 0.0)
        shift *= 2
    row_start = (tile_end - tiles) * ROW_TILE
    tile_start_ref[...] = (tile_end - tiles)[0:1, :].astype(jnp.int32)
    tiles_ref[...] = tiles[0:1, :].astype(jnp.int32)

    rt = rt_ref[...]
    ids = rt[R_ID1:R_ID2 + 1, :]
    start = jnp.zeros_like(ids)
    for e in range(N_EXPERTS):
        start = jnp.where(ids == e, row_start[0:1, e:e + 1], start)
    pos_ref[...] = (start + rt[R_RANK1:R_RANK2 + 1, :]).astype(jnp.int32)

    sub = lax.broadcasted_iota(jnp.int32, (N_EXPERTS, LANES), 0)
    lane_e = lax.broadcasted_iota(jnp.int32, (N_EXPERTS, LANES), 1)
    diag = sub == lane_e
    pad_lo = jnp.sum(jnp.where(diag, (row_start + counts)[0:1, :], 0.0), axis=1, keepdims=True)
    pad_n = jnp.sum(jnp.where(diag, (tiles * ROW_TILE - counts)[0:1, :], 0.0), axis=1, keepdims=True)
    j = lax.broadcasted_iota(jnp.int32, (N_EXPERTS, ROW_TILE), 1).astype(jnp.float32)
    pad_ref[...] = jnp.where(j < pad_n, pad_lo + j, n_rows + j).astype(jnp.int32)


def _run_plan(route_t, counts, n_rows):
    n_tok = route_t.shape[1]
    full = lambda shape: pl.BlockSpec(shape, lambda i: (0, 0))
    return pl.pallas_call(
        functools.partial(_plan_kernel, n_rows=n_rows),
        grid=(1,),
        in_specs=[full((ROUTE_ROWS, n_tok)), full((1, LANES))],
        out_specs=[full((2, n_tok)), full((1, LANES)), full((1, LANES)), full((N_EXPERTS, ROW_TILE))],
        out_shape=[
            jax.ShapeDtypeStruct((2, n_tok), jnp.int32),
            jax.ShapeDtypeStruct((1, LANES), jnp.int32),
            jax.ShapeDtypeStruct((1, LANES), jnp.int32),
            jax.ShapeDtypeStruct((N_EXPERTS, ROW_TILE), jnp.int32),
        ],
        compiler_params=pltpu.CompilerParams(dimension_semantics=("arbitrary",)),
    )(route_t, counts)


def _sc_mesh():
    return plsc.VectorSubcoreMesh(core_axis_name="core", subcore_axis_name="subcore")


def _sc_worker_id():
    return lax.axis_index("core") * (SC_WORKERS // 2) + lax.axis_index("subcore")


def _dispatch_rows(xp, pos1, pos2, pad_pos, zero_rows, n_rows):
    n_tok, width = xp.shape
    n_win = n_tok // SC_WORKERS // SC_WINDOW
    n_pad = pad_pos.shape[0] // SC_WORKERS // SC_WINDOW
    as_windows = lambda v: v.reshape(-1, SC_WINDOW)

    @functools.partial(
        pl.kernel, out_type=jax.ShapeDtypeStruct((n_rows + ROW_TILE, width), xp.dtype), mesh=_sc_mesh(),
        scratch_types=[pltpu.VMEM((n_win, SC_WINDOW), jnp.int32), pltpu.VMEM((n_win, SC_WINDOW), jnp.int32),
                       pltpu.VMEM((n_pad, SC_WINDOW), jnp.int32),
                       pltpu.VMEM((2, SC_WINDOW, width), xp.dtype),
                       pltpu.SemaphoreType.DMA((2,)), pltpu.SemaphoreType.DMA((2,))])
    def dispatch(xp_hbm, pos1_hbm, pos2_hbm, pad_hbm, zero_hbm, out_hbm, idx1, idx2, idxp, buf, lsem, ssem):
        wid = _sc_worker_id()
        pltpu.sync_copy(pos1_hbm.at[pl.ds(wid * n_win, n_win)], idx1)
        pltpu.sync_copy(pos2_hbm.at[pl.ds(wid * n_win, n_win)], idx2)
        pltpu.sync_copy(pad_hbm.at[pl.ds(wid * n_pad, n_pad)], idxp)

        def load(j):
            rows = xp_hbm.at[pl.ds((wid * n_win + j) * SC_WINDOW, SC_WINDOW)]
            return pltpu.make_async_copy(rows, buf.at[j % 2], lsem.at[j % 2])

        def scatters(j):
            return [pltpu.make_async_copy(buf.at[j % 2], out_hbm.at[idx.at[j]], ssem.at[j % 2])
                    for idx in (idx1, idx2)]

        load(0).start()
        for j in range(n_win):
            load(j).wait()
            for cp in scatters(j):
                cp.start()
            if j >= 1:
                for cp in scatters(j - 1):
                    cp.wait()
            if j + 1 < n_win:
                load(j + 1).start()
        for cp in scatters(n_win - 1):
            cp.wait()

        pltpu.sync_copy(zero_hbm, buf.at[0])
        pads = [pltpu.make_async_copy(buf.at[0], out_hbm.at[idxp.at[j]], ssem.at[0]) for j in range(n_pad)]
        for cp in pads:
            cp.start()
        for cp in pads:
            cp.wait()

    return dispatch(xp, as_windows(pos1), as_windows(pos2), as_windows(pad_pos), zero_rows)


def _gather_rows(src, idx):
    n_out, width = idx.shape[0], src.shape[1]
    n_win = n_out // SC_WORKERS // SC_WINDOW

    @functools.partial(
        pl.kernel, out_type=jax.ShapeDtypeStruct((n_out, width), src.dtype), mesh=_sc_mesh(),
        scratch_types=[pltpu.VMEM((n_win, SC_WINDOW), jnp.int32), pltpu.VMEM((2, SC_WINDOW, width), src.dtype),
                       pltpu.SemaphoreType.DMA((2,)), pltpu.SemaphoreType.DMA((2,))])
    def gather(src_hbm, idx_hbm, dst_hbm, idx_v, buf, gsem, ssem):
        wid = _sc_worker_id()
        pltpu.sync_copy(idx_hbm.at[pl.ds(wid * n_win, n_win)], idx_v)

        def fetch(j):
            return pltpu.make_async_copy(src_hbm.at[idx_v.at[j]], buf.at[j % 2], gsem.at[j % 2])

        def store(j):
            rows = dst_hbm.at[pl.ds((wid * n_win + j) * SC_WINDOW, SC_WINDOW)]
            return pltpu.make_async_copy(buf.at[j % 2], rows, ssem.at[j % 2])

        fetch(0).start()
        for j in range(n_win):
            fetch(j).wait()
            store(j).start()
            if j >= 1:
                store(j - 1).wait()
            if j + 1 < n_win:
                fetch(j + 1).start()
        store(n_win - 1).wait()

    return gather(src, idx.reshape(-1, SC_WINDOW))


def _tile_copy(hbm, buf, sem, tile, slot, to_hbm):
    rows = hbm.at[pl.ds(pl.multiple_of(tile * ROW_TILE, ROW_TILE), ROW_TILE)]
    if to_hbm:
        return pltpu.make_async_copy(buf.at[slot], rows, sem.at[slot])
    return pltpu.make_async_copy(rows, buf.at[slot], sem.at[slot])


def _weight_copies(w_hbm, wbuf, wsem, expert):
    slot = expert % WEIGHT_BUFFERS
    return [pltpu.make_async_copy(w.at[expert], buf.at[slot], wsem.at[slot]) for w, buf in zip(w_hbm, wbuf)]


def _expert_kernel(ts_ref, nte_ref, x_hbm, wg_hbm, wu_hbm, wd_hbm, y_hbm,
                   xbuf, ybuf, xsem, ysem, wg_buf, wu_buf, wd_buf, wsem, wgu_bf, wd_bf, act_ref, *, max_tiles):
    e = pl.program_id(0)
    first = ts_ref[e]
    count = nte_ref[e]
    n_tiles = ts_ref[N_EXPERTS - 1] + nte_ref[N_EXPERTS - 1]
    ahead = TILE_BUFFERS - 1
    w_hbm = (wg_hbm, wu_hbm, wd_hbm)
    wbuf = (wg_buf, wu_buf, wd_buf)

    @pl.when(e == 0)
    def _():
        for g in range(ahead):
            @pl.when(g < n_tiles)
            def _():
                _tile_copy(x_hbm, xbuf, xsem, g, g, False).start()
        for k in range(WEIGHT_BUFFERS - 1):
            for cp in _weight_copies(w_hbm, wbuf, wsem, k):
                cp.start()

    @pl.when(e + WEIGHT_BUFFERS - 1 < N_EXPERTS)
    def _():
        for cp in _weight_copies(w_hbm, wbuf, wsem, e + WEIGHT_BUFFERS - 1):
            cp.start()

    for cp in _weight_copies(w_hbm, wbuf, wsem, e):
        cp.wait()

    def sync_x(g):
        _tile_copy(x_hbm, xbuf, xsem, g, g % TILE_BUFFERS, False).wait()

        @pl.when(g + ahead < n_tiles)
        def _():
            _tile_copy(x_hbm, xbuf, xsem, g + ahead, (g + ahead) % TILE_BUFFERS, False).start()

    def sync_y_slot(g):
        @pl.when(g >= TILE_BUFFERS)
        def _():
            _tile_copy(y_hbm, ybuf, ysem, g - TILE_BUFFERS, g % TILE_BUFFERS, True).wait()

    def up_proj(g):
        lo, hi = _unpack_bf16_pairs(xbuf[g % TILE_BUFFERS])
        return (jnp.dot(lo.astype(jnp.bfloat16), wgu_bf[0:HALF, :], preferred_element_type=jnp.float32)
                + jnp.dot(hi.astype(jnp.bfloat16), wgu_bf[HALF:, :], preferred_element_type=jnp.float32))

    def put_act(hgu):
        hg = hgu[:, 0:D_EXPERT]
        act_ref[...] = (hg * jax.nn.sigmoid(hg) * hgu[:, D_EXPERT:]).astype(jnp.bfloat16)

    def down_proj(g):
        y = jnp.dot(act_ref[...], wd_bf[...], preferred_element_type=jnp.float32)
        ybuf[g % TILE_BUFFERS] = _pack_bf16_pairs(y)

    @pl.when(count > 0)
    def _():
        slot = e % WEIGHT_BUFFERS
        wgu_bf[:, 0:D_EXPERT] = wg_buf[slot].astype(jnp.bfloat16)
        wgu_bf[:, D_EXPERT:] = wu_buf[slot].astype(jnp.bfloat16)
        wd_bf[...] = wd_buf[slot].astype(jnp.bfloat16)

        sync_x(first)
        put_act(up_proj(first))

        def tile_body(g, c):
            sync_x(g)
            sync_y_slot(g - 1)
            down_proj(g - 1)
            hgu = up_proj(g)
            put_act(hgu)
            _tile_copy(y_hbm, ybuf, ysem, g - 1, (g - 1) % TILE_BUFFERS, True).start()
            return c

        lax.fori_loop(first + 1, first + count, tile_body, 0)
        last = first + count - 1
        sync_y_slot(last)
        down_proj(last)
        _tile_copy(y_hbm, ybuf, ysem, last, last % TILE_BUFFERS, True).start()

    @pl.when(e == N_EXPERTS - 1)
    def _():
        for k in range(TILE_BUFFERS, 0, -1):
            @pl.when(n_tiles >= k)
            def _():
                _tile_copy(y_hbm, ybuf, ysem, n_tiles - k, (n_tiles - k) % TILE_BUFFERS, True).wait()

        ybuf[0] = jnp.zeros((ROW_TILE, HALF), jnp.uint32)

        def fill(g, c):
            cp = _tile_copy(y_hbm, ybuf, ysem, g, 0, True)
            cp.start()
            cp.wait()
            return c

        lax.fori_loop(n_tiles, max_tiles, fill, 0)


def _run_experts(tile_start, tiles_per_expert, x_sorted, w_gate, w_up, w_down):
    n_rows = x_sorted.shape[0]
    max_tiles = n_rows // ROW_TILE
    hbm = pl.BlockSpec(memory_space=pl.ANY)

    grid_spec = pltpu.PrefetchScalarGridSpec(
        num_scalar_prefetch=2,
        grid=(N_EXPERTS,),
        in_specs=[hbm, hbm, hbm, hbm],
        out_specs=hbm,
        scratch_shapes=[
            pltpu.VMEM((TILE_BUFFERS, ROW_TILE, HALF), jnp.uint32),
            pltpu.VMEM((TILE_BUFFERS, ROW_TILE, HALF), jnp.uint32),
            pltpu.SemaphoreType.DMA((TILE_BUFFERS,)),
            pltpu.SemaphoreType.DMA((TILE_BUFFERS,)),
            pltpu.VMEM((WEIGHT_BUFFERS, D_MODEL, D_EXPERT), jnp.float32),
            pltpu.VMEM((WEIGHT_BUFFERS, D_MODEL, D_EXPERT), jnp.float32),
            pltpu.VMEM((WEIGHT_BUFFERS, D_EXPERT, D_MODEL), jnp.float32),
            pltpu.SemaphoreType.DMA((WEIGHT_BUFFERS,)),
            pltpu.VMEM((D_MODEL, 2 * D_EXPERT), jnp.bfloat16),
            pltpu.VMEM((D_EXPERT, D_MODEL), jnp.bfloat16),
            pltpu.VMEM((ROW_TILE, D_EXPERT), jnp.bfloat16),
        ],
    )
    return pl.pallas_call(
        functools.partial(_expert_kernel, max_tiles=max_tiles),
        grid_spec=grid_spec,
        out_shape=jax.ShapeDtypeStruct((n_rows, HALF), jnp.uint32),
        compiler_params=pltpu.CompilerParams(
            dimension_semantics=("arbitrary",), vmem_limit_bytes=VMEM_LIMIT),
    )(tile_start, tiles_per_expert, x_sorted, w_gate, w_up, w_down)


def _combine_rows(r0, h1_ref, p_ref, route_ref, ya_ref, yb_ref, w_pg, b_pg, w_ple, g_ref, b_ref, o_ref):
    rows = pl.ds(r0, SUB_TILE)
    h1 = h1_ref[rows, :]
    h_hi = h1.astype(jnp.bfloat16)
    p_b = p_ref[rows, :].astype(jnp.bfloat16)
    yield
    gate_pre = jnp.dot(h_hi, w_pg[...], preferred_element_type=jnp.float32)
    ple_pre = jnp.dot(p_b, w_ple[...], preferred_element_type=jnp.float32)
    yield
    ple = ple_pre * jax.nn.sigmoid(gate_pre + b_pg[...])
    route = route_ref[rows, :]
    w1 = route[:, R_W1:R_W1 + 1]
    w2 = route[:, R_W2:R_W2 + 1]
    a_lo, a_hi = _unpack_bf16_pairs(ya_ref[rows, :])
    b_lo, b_hi = _unpack_bf16_pairs(yb_ref[rows, :])
    moe = jnp.concatenate([w1 * a_lo + w2 * b_lo, w1 * a_hi + w2 * b_hi], axis=-1)
    o_ref[rows, :] = _layernorm(DEEPNORM_ALPHA * h1 + ple + moe, g_ref[...], b_ref[...])


def _combine_kernel(h1_ref, p_ref, route_ref, ya_ref, yb_ref, w_pg, b_pg, w_ple, g_ref, b_ref, o_ref):
    chains = [_combine_rows(r0, h1_ref, p_ref, route_ref, ya_ref, yb_ref, w_pg, b_pg, w_ple, g_ref, b_ref, o_ref)
              for r0 in range(0, h1_ref.shape[0], SUB_TILE)]
    for t in range(COMBINE_PHASES + len(chains) - 1):
        for k, chain in enumerate(chains):
            if 0 <= t - k < COMBINE_PHASES:
                next(chain, None)


def _run_combine(h1, p2, route, y_tok, w_pg, b_pg, w_ple, ln2_g, ln2_b):
    n_tok = h1.shape[0]
    n_t = n_tok // TOKEN_TILE
    tok_map = lambda i: (i, 0)
    const = lambda shape: pl.BlockSpec(shape, lambda i: (0, 0), pipeline_mode=pl.Buffered(1))
    in_specs = [
        pl.BlockSpec((TOKEN_TILE, D_MODEL), tok_map),
        pl.BlockSpec((TOKEN_TILE, PLE_DIM), tok_map),
        pl.BlockSpec((TOKEN_TILE, LANES), tok_map),
        pl.BlockSpec((TOKEN_TILE, HALF), lambda i: (i, 0)),
        pl.BlockSpec((TOKEN_TILE, HALF), lambda i: (i + n_t, 0)),
        const((D_MODEL, D_MODEL)), const((1, D_MODEL)), const((PLE_DIM, D_MODEL)),
        const((1, D_MODEL)), const((1, D_MODEL)),
    ]
    return pl.pallas_call(
        _combine_kernel,
        grid=(n_t,),
        in_specs=in_specs,
        out_specs=pl.BlockSpec((TOKEN_TILE, D_MODEL), tok_map),
        out_shape=jax.ShapeDtypeStruct((n_tok, D_MODEL), jnp.float32),
        compiler_params=pltpu.CompilerParams(dimension_semantics=("arbitrary",), vmem_limit_bytes=VMEM_LIMIT),
    )(h1, p2, route, y_tok, y_tok, w_pg, b_pg, w_ple, ln2_g, ln2_b)


def _split_bf16(w):
    hi = w.astype(jnp.bfloat16)
    lo = (w - hi.astype(jnp.float32)).astype(jnp.bfloat16)
    return hi, lo


def kernel(x, p, ln_in_g, ln_in_b, w_in, conv_w, conv_b, pool_w, pool_scale, w_out, ln1_g, ln1_b,
           w_rg, b_rg, w_re, b_re, w_gate, w_up, w_down, w_pg, b_pg, w_ple, ln2_g, ln2_b):
    batch, seq, _ = x.shape
    n_tok = batch * seq
    bf = jnp.bfloat16
    row = lambda v: v.reshape(1, -1)

    w_r = jnp.concatenate([w_rg[0], jnp.transpose(w_re[0], (1, 0, 2)).reshape(D_MODEL, N_EXPERTS)], axis=1)
    w_r = jnp.pad(w_r, ((0, 0), (0, LANES - w_r.shape[1])))
    w_r_hi, w_r_lo = _split_bf16(w_r)
    w_r_cat = jnp.concatenate([w_r_hi, w_r_hi, w_r_lo], axis=0)
    b_r = jnp.pad(jnp.concatenate([b_rg[0], b_re[0].reshape(-1)]), (0, LANES - N_GROUPS - N_EXPERTS)).reshape(1, LANES)

    x2 = x.reshape(n_tok, D_MODEL)
    p2 = p[0].reshape(n_tok, PLE_DIM)
    mixer_weights = (row(ln_in_g), row(ln_in_b), w_in[0].astype(bf), conv_w[0], row(conv_b[0]),
                     pool_w[0].astype(bf), row(pool_scale[0]), w_out[0].astype(bf), row(ln1_g[0]), row(ln1_b[0]),
                     w_r_cat, b_r)
    combine_weights = (w_pg[0].astype(bf), row(b_pg[0]), w_ple[0].astype(bf), row(ln2_g[0]), row(ln2_b[0]))
    expert_weights = (w_gate[0].reshape(N_EXPERTS, D_MODEL, D_EXPERT),
                      w_up[0].reshape(N_EXPERTS, D_MODEL, D_EXPERT),
                      w_down[0].reshape(N_EXPERTS, D_EXPERT, D_MODEL))
    zero_rows = jnp.zeros((SC_WINDOW, HALF), jnp.uint32)

    n_rows = -(-(2 * n_tok + N_EXPERTS * (ROW_TILE - 1)) // ROW_TILE) * ROW_TILE
    xp, h1, route, route_t, counts = _run_mixer(x2, *mixer_weights, batch, seq)
    pos, tile_start, tiles_per_expert, pad_pos = _run_plan(route_t, counts, n_rows)
    x_sorted = _dispatch_rows(xp, pos[0], pos[1], pad_pos.reshape(-1), zero_rows, n_rows)
    y_sorted = _run_experts(tile_start[0, :N_EXPERTS], tiles_per_expert[0, :N_EXPERTS], x_sorted, *expert_weights)

    y_tok = _gather_rows(y_sorted, pos.reshape(-1))
    out = _run_combine(h1, p2, route, y_tok, *combine_weights)
    return out.reshape(batch, seq, D_MODEL)
```

```python
import functools

import jax
import jax.numpy as jnp
from jax import lax
from jax.experimental import pallas as pl
from jax.experimental.pallas import tpu as pltpu
from jax.experimental.pallas import tpu_sc as plsc

D_MODEL = 1024
CONV_WIDTH = 512
CONV_K = 3
POOL_WIDTH = 512
POOL_WINDOWS = (2, 4, 8, 16)
POOL_GW = 128
IN_PROJ = 3 * CONV_WIDTH + POOL_WIDTH
N_GROUPS = 4
EXPERTS_PER_GROUP = 8
N_EXPERTS = N_GROUPS * EXPERTS_PER_GROUP
D_EXPERT = 256
PLE_DIM = 256
LN_EPS = 1e-5
DEEPNORM_ALPHA = 2.0 ** 0.25

LANES = 128
HALF = D_MODEL // 2
CONV_HALO = 8
POOL_HALO = 16
SEQ_TILE = 1024
SUB_TILE = 256
MIXER_PHASES = 7
COMBINE_PHASES = 3
ROW_TILE = 256
TILE_BUFFERS = 12
SC_WORKERS = 32
SC_WINDOW = 64
WEIGHT_BUFFERS = 4
TOKEN_TILE = 1024
VMEM_LIMIT = 56 * 1024 * 1024

R_ID1, R_ID2, R_RANK1, R_RANK2, R_W1, R_W2 = range(6)
ROUTE_ROWS = 8
LOGIT_ROWS = 40


def _layernorm(x, g, b):
    mu = jnp.mean(x, axis=-1, keepdims=True)
    xc = x - mu
    var = jnp.mean(xc * xc, axis=-1, keepdims=True)
    return xc * lax.rsqrt(var + LN_EPS) * g + b


def _pack_bf16_pairs(v):
    bits = lax.bitcast_convert_type(v.astype(jnp.bfloat16).astype(jnp.float32), jnp.uint32)
    return bits[:, HALF:] | (bits[:, :HALF] >> 16)


def _unpack_bf16_pairs(w):
    lo = lax.bitcast_convert_type(w << 16, jnp.float32)
    hi = lax.bitcast_convert_type(w & jnp.uint32(0xFFFF0000), jnp.float32)
    return lo, hi


def _mixer_kernel(x_ref, lnin_g, lnin_b, w_in, conv_w, conv_b, pool_w, pool_scale, w_out,
                  ln1_g, ln1_b, w_r, b_r,
                  xp_ref, h1_ref, route_ref, route_t_ref, counts_ref,
                  zbuf, vbuf, carry):
    b = pl.program_id(0)
    s = pl.program_id(1)
    ts = x_ref.shape[0]

    @pl.when(s == 0)
    def _():
        zbuf[0:CONV_HALO, :] = jnp.zeros((CONV_HALO, CONV_WIDTH), jnp.float32)
        vbuf[0:POOL_HALO, :] = jnp.zeros((POOL_HALO, POOL_WIDTH), jnp.float32)

    @pl.when((b == 0) & (s == 0))
    def _():
        carry[...] = jnp.zeros_like(carry)

    chains = [_mixer_rows(r0, s * ts + r0, x_ref, lnin_g, lnin_b, w_in, conv_w, conv_b, pool_w, pool_scale,
                          w_out, ln1_g, ln1_b, w_r, b_r,
                          xp_ref, h1_ref, route_ref, route_t_ref, zbuf, vbuf, carry)
              for r0 in range(0, ts, SUB_TILE)]
    for t in range(MIXER_PHASES + len(chains) - 1):
        for k, chain in enumerate(chains):
            if 0 <= t - k < MIXER_PHASES:
                next(chain, None)
    zbuf[0:CONV_HALO, :] = zbuf[ts:ts + CONV_HALO, :]
    vbuf[0:POOL_HALO, :] = vbuf[ts:ts + POOL_HALO, :]
    counts_ref[...] = jnp.transpose(carry[...])[0:1, :]


def _mixer_rows(r0, seq0, x_ref, lnin_g, lnin_b, w_in, conv_w, conv_b, pool_w, pool_scale, w_out,
                ln1_g, ln1_b, w_r, b_r,
                xp_ref, h1_ref, route_ref, route_t_ref, zbuf, vbuf, carry):
    n = SUB_TILE
    rows = pl.ds(r0, n)
    h0 = _layernorm(x_ref[rows, :], lnin_g[...], lnin_b[...])
    h0b = h0.astype(jnp.bfloat16)
    yield
    u = jnp.dot(h0b, w_in[...], preferred_element_type=jnp.float32)
    yield
    b_g = u[:, 0:CONV_WIDTH]
    c_g = u[:, CONV_WIDTH:2 * CONV_WIDTH]
    v_c = u[:, 2 * CONV_WIDTH:3 * CONV_WIDTH]
    v_p = u[:, 3 * CONV_WIDTH:]

    zbuf[pl.ds(CONV_HALO + r0, n), :] = c_g * v_c
    zext = zbuf[pl.ds(r0, n + CONV_HALO), :]
    z1 = pltpu.roll(zext, 1, axis=0)[CONV_HALO:, :]
    z2 = pltpu.roll(zext, 2, axis=0)[CONV_HALO:, :]
    conv = z2 * conv_w[0:1, :] + z1 * conv_w[1:2, :] + zext[CONV_HALO:, :] * conv_w[2:3, :] + conv_b[...]
    y_conv = b_g * conv

    vbuf[pl.ds(POOL_HALO + r0, n), :] = v_p
    vext = vbuf[pl.ds(r0, n + POOL_HALO), :]
    s2 = vext + pltpu.roll(vext, 1, axis=0)
    s4 = s2[:, POOL_GW:] + pltpu.roll(s2[:, POOL_GW:], 2, axis=0)
    s8 = s4[:, POOL_GW:] + pltpu.roll(s4[:, POOL_GW:], 4, axis=0)
    s16 = s8[:, POOL_GW:] + pltpu.roll(s8[:, POOL_GW:], 8, axis=0)
    wsums = (s2[POOL_HALO:, 0:POOL_GW], s4[POOL_HALO:, 0:POOL_GW],
             s8[POOL_HALO:, 0:POOL_GW], s16[POOL_HALO:, 0:POOL_GW])

    t_pos = seq0 + lax.broadcasted_iota(jnp.int32, (n, 1), 0)
    y_pool = []
    for j, w in enumerate(POOL_WINDOWS):
        inv_cnt = 1.0 / jnp.minimum(t_pos + 1, w).astype(jnp.float32)
        pooled = wsums[j] * inv_cnt - v_p[:, j * POOL_GW:(j + 1) * POOL_GW]
        y_pool.append(jnp.dot(pooled.astype(jnp.bfloat16), pool_w[j], preferred_element_type=jnp.float32))
    y_pool = jnp.concatenate(y_pool, axis=-1) * pool_scale[...]

    ycat = jnp.concatenate([y_conv, y_pool], axis=-1).astype(jnp.bfloat16)
    yield
    mix = jnp.dot(ycat, w_out[...], preferred_element_type=jnp.float32)
    yield
    h1 = _layernorm(DEEPNORM_ALPHA * h0 + mix, ln1_g[...], ln1_b[...])

    h_hi = h1.astype(jnp.bfloat16)
    h_lo = (h1 - h_hi.astype(jnp.float32)).astype(jnp.bfloat16)
    xp_ref[rows, :] = _pack_bf16_pairs(h1)
    h1_ref[rows, :] = h1
    hcat = jnp.concatenate([h_hi, h_lo, h_hi], axis=-1)
    yield
    logits = jnp.dot(hcat, w_r[...], preferred_element_type=jnp.float32) + b_r[...]
    yield
    lt = jnp.transpose(logits)[0:LOGIT_ROWS, :]
    rid = lax.broadcasted_iota(jnp.int32, (LOGIT_ROWS, n), 0).astype(jnp.float32)
    neg = jnp.float32(-jnp.inf)

    def first_argmax(vals):
        m = jnp.max(vals, axis=0, keepdims=True)
        idx = jnp.min(jnp.where(vals == m, rid, float(LOGIT_ROWS)), axis=0, keepdims=True)
        return m, idx

    g_mask = rid < N_GROUPS
    g_max, g_idx = first_argmax(jnp.where(g_mask, lt, neg))
    g_w = 1.0 / jnp.sum(jnp.where(g_mask, jnp.exp(lt - g_max), 0.0), axis=0, keepdims=True)

    e_lo = N_GROUPS + EXPERTS_PER_GROUP * g_idx
    e_vals = jnp.where((rid >= e_lo) & (rid < e_lo + EXPERTS_PER_GROUP), lt, neg)
    m1, i1 = first_argmax(e_vals)
    m2, i2 = first_argmax(jnp.where(rid == i1, neg, e_vals))
    e21 = jnp.exp(m2 - m1)
    w1 = g_w / (1.0 + e21)
    w2 = g_w * e21 / (1.0 + e21)
    id1 = i1 - N_GROUPS
    id2 = i2 - N_GROUPS

    eid = lax.broadcasted_iota(jnp.int32, (N_EXPERTS, n), 0).astype(jnp.float32)
    sel1 = eid == id1
    sel2 = eid == id2
    onehot = (sel1 | sel2).astype(jnp.float32)
    src = lax.broadcasted_iota(jnp.int32, (n, n), 0)
    dst = lax.broadcasted_iota(jnp.int32, (n, n), 1)
    earlier = (src < dst).astype(jnp.bfloat16)
    before = (jnp.dot(onehot.astype(jnp.bfloat16), earlier, preferred_element_type=jnp.float32)
              + carry[0:N_EXPERTS, 0:1])
    rank1 = jnp.sum(jnp.where(sel1, before, 0.0), axis=0, keepdims=True)
    rank2 = jnp.sum(jnp.where(sel2, before, 0.0), axis=0, keepdims=True)
    carry[0:N_EXPERTS, :] = carry[0:N_EXPERTS, :] + jnp.sum(onehot, axis=1, keepdims=True)

    rec_t = jnp.zeros((ROUTE_ROWS, n), jnp.float32)
    rec_row = lax.broadcasted_iota(jnp.int32, (ROUTE_ROWS, n), 0)
    for k, val in ((R_ID1, id1), (R_ID2, id2), (R_W1, w1), (R_W2, w2), (R_RANK1, rank1), (R_RANK2, rank2)):
        rec_t = jnp.where(rec_row == k, val, rec_t)
    route_t_ref[:, rows] = rec_t
    padded = jnp.concatenate([rec_t, jnp.zeros((LANES - ROUTE_ROWS, n), jnp.float32)], axis=0)
    route_ref[rows, :] = jnp.transpose(padded)


def _run_mixer(x2, lnin_g, lnin_b, w_in, conv_w, conv_b, pool_w, pool_scale, w_out, ln1_g, ln1_b,
               w_r, b_r, batch, seq):
    n_tok = batch * seq
    n_s = seq // SEQ_TILE
    tok_map = lambda b, s: (b * n_s + s, 0)

    def const(shape):
        return pl.BlockSpec(shape, lambda b, s: (0,) * len(shape), pipeline_mode=pl.Buffered(1))

    in_specs = [
        pl.BlockSpec((SEQ_TILE, D_MODEL), tok_map),
        const((1, D_MODEL)), const((1, D_MODEL)),
        const((D_MODEL, IN_PROJ)),
        const((CONV_K, CONV_WIDTH)), const((1, CONV_WIDTH)),
        const((len(POOL_WINDOWS), POOL_GW, POOL_GW)), const((1, POOL_WIDTH)),
        const((D_MODEL, D_MODEL)),
        const((1, D_MODEL)), const((1, D_MODEL)),
        const((3 * D_MODEL, LANES)), const((1, LANES)),
    ]
    out_specs = [
        pl.BlockSpec((SEQ_TILE, HALF), tok_map),
        pl.BlockSpec((SEQ_TILE, D_MODEL), tok_map),
        pl.BlockSpec((SEQ_TILE, LANES), tok_map),
        pl.BlockSpec((ROUTE_ROWS, SEQ_TILE), lambda b, s: (0, b * n_s + s)),
        pl.BlockSpec((1, LANES), lambda b, s: (0, 0)),
    ]
    out_shape = [
        jax.ShapeDtypeStruct((n_tok, HALF), jnp.uint32),
        jax.ShapeDtypeStruct((n_tok, D_MODEL), jnp.float32),
        jax.ShapeDtypeStruct((n_tok, LANES), jnp.float32),
        jax.ShapeDtypeStruct((ROUTE_ROWS, n_tok), jnp.float32),
        jax.ShapeDtypeStruct((1, LANES), jnp.float32),
    ]
    return pl.pallas_call(
        _mixer_kernel,
        grid=(batch, n_s),
        in_specs=in_specs,
        out_specs=out_specs,
        out_shape=out_shape,
        scratch_shapes=[
            pltpu.VMEM((SEQ_TILE + CONV_HALO, CONV_WIDTH), jnp.float32),
            pltpu.VMEM((SEQ_TILE + POOL_HALO, POOL_WIDTH), jnp.float32),
            pltpu.VMEM((LANES, LANES), jnp.float32),
        ],
        compiler_params=pltpu.CompilerParams(
            dimension_semantics=("arbitrary", "arbitrary"), vmem_limit_bytes=VMEM_LIMIT),
    )(x2, lnin_g, lnin_b, w_in, conv_w, conv_b, pool_w, pool_scale, w_out, ln1_g, ln1_b, w_r, b_r)


def _plan_kernel(rt_ref, counts_ref, pos_ref, tile_start_ref, tiles_ref, pad_ref, *, n_rows):
    lane = lax.broadcasted_iota(jnp.int32, (ROUTE_ROWS, LANES), 1)
    counts = jnp.broadcast_to(counts_ref[...], (ROUTE_ROWS, LANES))
    tiles = jnp.floor((counts + (ROW_TILE - 1)) * (1.0 / ROW_TILE))
    tile_end = tiles
    shift = 1
    while shift < N_EXPERTS:
        tile_end = tile_end + jnp.where(lane >= shift, pltpu.roll(tile_end, shift, axis=1), 0.0)
        shift *= 2
    row_start = (tile_end - tiles) * ROW_TILE
    tile_start_ref[...] = (tile_end - tiles)[0:1, :].astype(jnp.int32)
    tiles_ref[...] = tiles[0:1, :].astype(jnp.int32)

    rt = rt_ref[...]
    ids = rt[R_ID1:R_ID2 + 1, :]
    start = jnp.zeros_like(ids)
    for e in range(N_EXPERTS):
        start = jnp.where(ids == e, row_start[0:1, e:e + 1], start)
    pos_ref[...] = (start + rt[R_RANK1:R_RANK2 + 1, :]).astype(jnp.int32)

    sub = lax.broadcasted_iota(jnp.int32, (N_EXPERTS, LANES), 0)
    lane_e = lax.broadcasted_iota(jnp.int32, (N_EXPERTS, LANES), 1)
    diag = sub == lane_e
    pad_lo = jnp.sum(jnp.where(diag, (row_start + counts)[0:1, :], 0.0), axis=1, keepdims=True)
    pad_n = jnp.sum(jnp.where(diag, (tiles * ROW_TILE - counts)[0:1, :], 0.0), axis=1, keepdims=True)
    j = lax.broadcasted_iota(jnp.int32, (N_EXPERTS, ROW_TILE), 1).astype(jnp.float32)
    pad_ref[...] = jnp.where(j < pad_n, pad_lo + j, n_rows + j).astype(jnp.int32)


def _run_plan(route_t, counts, n_rows):
    n_tok = route_t.shape[1]
    full = lambda shape: pl.BlockSpec(shape, lambda i: (0, 0))
    return pl.pallas_call(
        functools.partial(_plan_kernel, n_rows=n_rows),
        grid=(1,),
        in_specs=[full((ROUTE_ROWS, n_tok)), full((1, LANES))],
        out_specs=[full((2, n_tok)), full((1, LANES)), full((1, LANES)), full((N_EXPERTS, ROW_TILE))],
        out_shape=[
            jax.ShapeDtypeStruct((2, n_tok), jnp.int32),
            jax.ShapeDtypeStruct((1, LANES), jnp.int32),
            jax.ShapeDtypeStruct((1, LANES), jnp.int32),
            jax.ShapeDtypeStruct((N_EXPERTS, ROW_TILE), jnp.int32),
        ],
        compiler_params=pltpu.CompilerParams(dimension_semantics=("arbitrary",)),
    )(route_t, counts)


def _sc_mesh():
    return plsc.VectorSubcoreMesh(core_axis_name="core", subcore_axis_name="subcore")


def _sc_worker_id():
    return lax.axis_index("core") * (SC_WORKERS // 2) + lax.axis_index("subcore")


def _dispatch_rows(xp, pos1, pos2, pad_pos, zero_rows, n_rows):
    n_tok, width = xp.shape
    n_win = n_tok // SC_WORKERS // SC_WINDOW
    n_pad = pad_pos.shape[0] // SC_WORKERS // SC_WINDOW
    as_windows = lambda v: v.reshape(-1, SC_WINDOW)

    @functools.partial(
        pl.kernel, out_type=jax.ShapeDtypeStruct((n_rows + ROW_TILE, width), xp.dtype), mesh=_sc_mesh(),
        scratch_types=[pltpu.VMEM((n_win, SC_WINDOW), jnp.int32), pltpu.VMEM((n_win, SC_WINDOW), jnp.int32),
                       pltpu.VMEM((n_pad, SC_WINDOW), jnp.int32),
                       pltpu.VMEM((2, SC_WINDOW, width), xp.dtype),
                       pltpu.SemaphoreType.DMA((2,)), pltpu.SemaphoreType.DMA((2,))])
    def dispatch(xp_hbm, pos1_hbm, pos2_hbm, pad_hbm, zero_hbm, out_hbm, idx1, idx2, idxp, buf, lsem, ssem):
        wid = _sc_worker_id()
        pltpu.sync_copy(pos1_hbm.at[pl.ds(wid * n_win, n_win)], idx1)
        pltpu.sync_copy(pos2_hbm.at[pl.ds(wid * n_win, n_win)], idx2)
        pltpu.sync_copy(pad_hbm.at[pl.ds(wid * n_pad, n_pad)], idxp)

        def load(j):
            rows = xp_hbm.at[pl.ds((wid * n_win + j) * SC_WINDOW, SC_WINDOW)]
            return pltpu.make_async_copy(rows, buf.at[j % 2], lsem.at[j % 2])

        def scatters(j):
            return [pltpu.make_async_copy(buf.at[j % 2], out_hbm.at[idx.at[j]], ssem.at[j % 2])
                    for idx in (idx1, idx2)]

        load(0).start()
        for j in range(n_win):
            load(j).wait()
            for cp in scatters(j):
                cp.start()
            if j >= 1:
                for cp in scatters(j - 1):
                    cp.wait()
            if j + 1 < n_win:
                load(j + 1).start()
        for cp in scatters(n_win - 1):
            cp.wait()

        pltpu.sync_copy(zero_hbm, buf.at[0])
        pads = [pltpu.make_async_copy(buf.at[0], out_hbm.at[idxp.at[j]], ssem.at[0]) for j in range(n_pad)]
        for cp in pads:
            cp.start()
        for cp in pads:
            cp.wait()

    return dispatch(xp, as_windows(pos1), as_windows(pos2), as_windows(pad_pos), zero_rows)


def _gather_rows(src, idx):
    n_out, width = idx.shape[0], src.shape[1]
    n_win = n_out // SC_WORKERS // SC_WINDOW

    @functools.partial(
        pl.kernel, out_type=jax.ShapeDtypeStruct((n_out, width), src.dtype), mesh=_sc_mesh(),
        scratch_types=[pltpu.VMEM((n_win, SC_WINDOW), jnp.int32), pltpu.VMEM((2, SC_WINDOW, width), src.dtype),
                       pltpu.SemaphoreType.DMA((2,)), pltpu.SemaphoreType.DMA((2,))])
    def gather(src_hbm, idx_hbm, dst_hbm, idx_v, buf, gsem, ssem):
        wid = _sc_worker_id()
        pltpu.sync_copy(idx_hbm.at[pl.ds(wid * n_win, n_win)], idx_v)

        def fetch(j):
            return pltpu.make_async_copy(src_hbm.at[idx_v.at[j]], buf.at[j % 2], gsem.at[j % 2])

        def store(j):
            rows = dst_hbm.at[pl.ds((wid * n_win + j) * SC_WINDOW, SC_WINDOW)]
            return pltpu.make_async_copy(buf.at[j % 2], rows, ssem.at[j % 2])

        fetch(0).start()
        for j in range(n_win):
            fetch(j).wait()
            store(j).start()
            if j >= 1:
                store(j - 1).wait()
            if j + 1 < n_win:
                fetch(j + 1).start()
        store(n_win - 1).wait()

    return gather(src, idx.reshape(-1, SC_WINDOW))


def _tile_copy(hbm, buf, sem, tile, slot, to_hbm):
    rows = hbm.at[pl.ds(pl.multiple_of(tile * ROW_TILE, ROW_TILE), ROW_TILE)]
    if to_hbm:
        return pltpu.make_async_copy(buf.at[slot], rows, sem.at[slot])
    return pltpu.make_async_copy(rows, buf.at[slot], sem.at[slot])


def _weight_copies(w_hbm, wbuf, wsem, expert):
    slot = expert % WEIGHT_BUFFERS
    return [pltpu.make_async_copy(w.at[expert], buf.at[slot], wsem.at[slot]) for w, buf in zip(w_hbm, wbuf)]


def _expert_kernel(ts_ref, nte_ref, x_hbm, wg_hbm, wu_hbm, wd_hbm, y_hbm,
                   xbuf, ybuf, xsem, ysem, wg_buf, wu_buf, wd_buf, wsem, wgu_bf, wd_bf, act_ref, *, max_tiles):
    e = pl.program_id(0)
    first = ts_ref[e]
    count = nte_ref[e]
    n_tiles = ts_ref[N_EXPERTS - 1] + nte_ref[N_EXPERTS - 1]
    ahead = TILE_BUFFERS - 1
    w_hbm = (wg_hbm, wu_hbm, wd_hbm)
    wbuf = (wg_buf, wu_buf, wd_buf)

    @pl.when(e == 0)
    def _():
        for g in range(ahead):
            @pl.when(g < n_tiles)
            def _():
                _tile_copy(x_hbm, xbuf, xsem, g, g, False).start()
        for k in range(WEIGHT_BUFFERS - 1):
            for cp in _weight_copies(w_hbm, wbuf, wsem, k):
                cp.start()

    @pl.when(e + WEIGHT_BUFFERS - 1 < N_EXPERTS)
    def _():
        for cp in _weight_copies(w_hbm, wbuf, wsem, e + WEIGHT_BUFFERS - 1):
            cp.start()

    for cp in _weight_copies(w_hbm, wbuf, wsem, e):
        cp.wait()

    def sync_x(g):
        _tile_copy(x_hbm, xbuf, xsem, g, g % TILE_BUFFERS, False).wait()

        @pl.when(g + ahead < n_tiles)
        def _():
            _tile_copy(x_hbm, xbuf, xsem, g + ahead, (g + ahead) % TILE_BUFFERS, False).start()

    def sync_y_slot(g):
        @pl.when(g >= TILE_BUFFERS)
        def _():
            _tile_copy(y_hbm, ybuf, ysem, g - TILE_BUFFERS, g % TILE_BUFFERS, True).wait()

    def up_proj(g):
        lo, hi = _unpack_bf16_pairs(xbuf[g % TILE_BUFFERS])
        return (jnp.dot(lo.astype(jnp.bfloat16), wgu_bf[0:HALF, :], preferred_element_type=jnp.float32)
                + jnp.dot(hi.astype(jnp.bfloat16), wgu_bf[HALF:, :], preferred_element_type=jnp.float32))

    def put_act(hgu):
        hg = hgu[:, 0:D_EXPERT]
        act_ref[...] = (hg * jax.nn.sigmoid(hg) * hgu[:, D_EXPERT:]).astype(jnp.bfloat16)

    def down_proj(g):
        y = jnp.dot(act_ref[...], wd_bf[...], preferred_element_type=jnp.float32)
        ybuf[g % TILE_BUFFERS] = _pack_bf16_pairs(y)

    @pl.when(count > 0)
    def _():
        slot = e % WEIGHT_BUFFERS
        wgu_bf[:, 0:D_EXPERT] = wg_buf[slot].astype(jnp.bfloat16)
        wgu_bf[:, D_EXPERT:] = wu_buf[slot].astype(jnp.bfloat16)
        wd_bf[...] = wd_buf[slot].astype(jnp.bfloat16)

        sync_x(first)
        put_act(up_proj(first))

        def tile_body(g, c):
            sync_x(g)
            sync_y_slot(g - 1)
            down_proj(g - 1)
            hgu = up_proj(g)
            put_act(hgu)
            _tile_copy(y_hbm, ybuf, ysem, g - 1, (g - 1) % TILE_BUFFERS, True).start()
            return c

        lax.fori_loop(first + 1, first + count, tile_body, 0)
        last = first + count - 1
        sync_y_slot(last)
        down_proj(last)
        _tile_copy(y_hbm, ybuf, ysem, last, last % TILE_BUFFERS, True).start()

    @pl.when(e == N_EXPERTS - 1)
    def _():
        for k in range(TILE_BUFFERS, 0, -1):
            @pl.when(n_tiles >= k)
            def _():
                _tile_copy(y_hbm, ybuf, ysem, n_tiles - k, (n_tiles - k) % TILE_BUFFERS, True).wait()

        ybuf[0] = jnp.zeros((ROW_TILE, HALF), jnp.uint32)

        def fill(g, c):
            cp = _tile_copy(y_hbm, ybuf, ysem, g, 0, True)
            cp.start()
            cp.wait()
            return c

        lax.fori_loop(n_tiles, max_tiles, fill, 0)


def _run_experts(tile_start, tiles_per_expert, x_sorted, w_gate, w_up, w_down):
    n_rows = x_sorted.shape[0]
    max_tiles = n_rows // ROW_TILE
    hbm = pl.BlockSpec(memory_space=pl.ANY)

    grid_spec = pltpu.PrefetchScalarGridSpec(
        num_scalar_prefetch=2,
        grid=(N_EXPERTS,),
        in_specs=[hbm, hbm, hbm, hbm],
        out_specs=hbm,
        scratch_shapes=[
            pltpu.VMEM((TILE_BUFFERS, ROW_TILE, HALF), jnp.uint32),
            pltpu.VMEM((TILE_BUFFERS, ROW_TILE, HALF), jnp.uint32),
            pltpu.SemaphoreType.DMA((TILE_BUFFERS,)),
            pltpu.SemaphoreType.DMA((TILE_BUFFERS,)),
            pltpu.VMEM((WEIGHT_BUFFERS, D_MODEL, D_EXPERT), jnp.float32),
            pltpu.VMEM((WEIGHT_BUFFERS, D_MODEL, D_EXPERT), jnp.float32),
            pltpu.VMEM((WEIGHT_BUFFERS, D_EXPERT, D_MODEL), jnp.float32),
            pltpu.SemaphoreType.DMA((WEIGHT_BUFFERS,)),
            pltpu.VMEM((D_MODEL, 2 * D_EXPERT), jnp.bfloat16),
            pltpu.VMEM((D_EXPERT, D_MODEL), jnp.bfloat16),
            pltpu.VMEM((ROW_TILE, D_EXPERT), jnp.bfloat16),
        ],
    )
    return pl.pallas_call(
        functools.partial(_expert_kernel, max_tiles=max_tiles),
        grid_spec=grid_spec,
        out_shape=jax.ShapeDtypeStruct((n_rows, HALF), jnp.uint32),
        compiler_params=pltpu.CompilerParams(
            dimension_semantics=("arbitrary",), vmem_limit_bytes=VMEM_LIMIT),
    )(tile_start, tiles_per_expert, x_sorted, w_gate, w_up, w_down)


def _combine_rows(r0, h1_ref, p_ref, route_ref, ya_ref, yb_ref, w_pg, b_pg, w_ple, g_ref, b_ref, o_ref):
    rows = pl.ds(r0, SUB_TILE)
    h1 = h1_ref[rows, :]
    h_hi = h1.astype(jnp.bfloat16)
    p_b = p_ref[rows, :].astype(jnp.bfloat16)
    yield
    gate_pre = jnp.dot(h_hi, w_pg[...], preferred_element_type=jnp.float32)
    ple_pre = jnp.dot(p_b, w_ple[...], preferred_element_type=jnp.float32)
    yield
    ple = ple_pre * jax.nn.sigmoid(gate_pre + b_pg[...])
    route = route_ref[rows, :]
    w1 = route[:, R_W1:R_W1 + 1]
    w2 = route[:, R_W2:R_W2 + 1]
    a_lo, a_hi = _unpack_bf16_pairs(ya_ref[rows, :])
    b_lo, b_hi = _unpack_bf16_pairs(yb_ref[rows, :])
    moe = jnp.concatenate([w1 * a_lo + w2 * b_lo, w1 * a_hi + w2 * b_hi], axis=-1)
    o_ref[rows, :] = _layernorm(DEEPNORM_ALPHA * h1 + ple + moe, g_ref[...], b_ref[...])


def _combine_kernel(h1_ref, p_ref, route_ref, ya_ref, yb_ref, w_pg, b_pg, w_ple, g_ref, b_ref, o_ref):
    chains = [_combine_rows(r0, h1_ref, p_ref, route_ref, ya_ref, yb_ref, w_pg, b_pg, w_ple, g_ref, b_ref, o_ref)
              for r0 in range(0, h1_ref.shape[0], SUB_TILE)]
    for t in range(COMBINE_PHASES + len(chains) - 1):
        for k, chain in enumerate(chains):
            if 0 <= t - k < COMBINE_PHASES:
                next(chain, None)


def _run_combine(h1, p2, route, y_tok, w_pg, b_pg, w_ple, ln2_g, ln2_b):
    n_tok = h1.shape[0]
    n_t = n_tok // TOKEN_TILE
    tok_map = lambda i: (i, 0)
    const = lambda shape: pl.BlockSpec(shape, lambda i: (0, 0), pipeline_mode=pl.Buffered(1))
    in_specs = [
        pl.BlockSpec((TOKEN_TILE, D_MODEL), tok_map),
        pl.BlockSpec((TOKEN_TILE, PLE_DIM), tok_map),
        pl.BlockSpec((TOKEN_TILE, LANES), tok_map),
        pl.BlockSpec((TOKEN_TILE, HALF), lambda i: (i, 0)),
        pl.BlockSpec((TOKEN_TILE, HALF), lambda i: (i + n_t, 0)),
        const((D_MODEL, D_MODEL)), const((1, D_MODEL)), const((PLE_DIM, D_MODEL)),
        const((1, D_MODEL)), const((1, D_MODEL)),
    ]
    return pl.pallas_call(
        _combine_kernel,
        grid=(n_t,),
        in_specs=in_specs,
        out_specs=pl.BlockSpec((TOKEN_TILE, D_MODEL), tok_map),
        out_shape=jax.ShapeDtypeStruct((n_tok, D_MODEL), jnp.float32),
        compiler_params=pltpu.CompilerParams(dimension_semantics=("arbitrary",), vmem_limit_bytes=VMEM_LIMIT),
    )(h1, p2, route, y_tok, y_tok, w_pg, b_pg, w_ple, ln2_g, ln2_b)


def _split_bf16(w):
    hi = w.astype(jnp.bfloat16)
    lo = (w - hi.astype(jnp.float32)).astype(jnp.bfloat16)
    return hi, lo


def kernel(x, p, ln_in_g, ln_in_b, w_in, conv_w, conv_b, pool_w, pool_scale, w_out, ln1_g, ln1_b,
           w_rg, b_rg, w_re, b_re, w_gate, w_up, w_down, w_pg, b_pg, w_ple, ln2_g, ln2_b):
    batch, seq, _ = x.shape
    n_tok = batch * seq
    bf = jnp.bfloat16
    row = lambda v: v.reshape(1, -1)

    w_r = jnp.concatenate([w_rg[0], jnp.transpose(w_re[0], (1, 0, 2)).reshape(D_MODEL, N_EXPERTS)], axis=1)
    w_r = jnp.pad(w_r, ((0, 0), (0, LANES - w_r.shape[1])))
    w_r_hi, w_r_lo = _split_bf16(w_r)
    w_r_cat = jnp.concatenate([w_r_hi, w_r_hi, w_r_lo], axis=0)
    b_r = jnp.pad(jnp.concatenate([b_rg[0], b_re[0].reshape(-1)]), (0, LANES - N_GROUPS - N_EXPERTS)).reshape(1, LANES)

    x2 = x.reshape(n_tok, D_MODEL)
    p2 = p[0].reshape(n_tok, PLE_DIM)
    mixer_weights = (row(ln_in_g), row(ln_in_b), w_in[0].astype(bf), conv_w[0], row(conv_b[0]),
                     pool_w[0].astype(bf), row(pool_scale[0]), w_out[0].astype(bf), row(ln1_g[0]), row(ln1_b[0]),
                     w_r_cat, b_r)
    combine_weights = (w_pg[0].astype(bf), row(b_pg[0]), w_ple[0].astype(bf), row(ln2_g[0]), row(ln2_b[0]))
    expert_weights = (w_gate[0].reshape(N_EXPERTS, D_MODEL, D_EXPERT),
                      w_up[0].reshape(N_EXPERTS, D_MODEL, D_EXPERT),
                      w_down[0].reshape(N_EXPERTS, D_EXPERT, D_MODEL))
    zero_rows = jnp.zeros((SC_WINDOW, HALF), jnp.uint32)

    n_rows = -(-(2 * n_tok + N_EXPERTS * (ROW_TILE - 1)) // ROW_TILE) * ROW_TILE
    xp, h1, route, route_t, counts = _run_mixer(x2, *mixer_weights, batch, seq)
    pos, tile_start, tiles_per_expert, pad_pos = _run_plan(route_t, counts, n_rows)
    x_sorted = _dispatch_rows(xp, pos[0], pos[1], pad_pos.reshape(-1), zero_rows, n_rows)
    y_sorted = _run_experts(tile_start[0, :N_EXPERTS], tiles_per_expert[0, :N_EXPERTS], x_sorted, *expert_weights)

    y_tok = _gather_rows(y_sorted, pos.reshape(-1))
    out = _run_combine(h1, p2, route, y_tok, *combine_weights)
    return out.reshape(batch, seq, D_MODEL)
```

```python
import functools

import jax
import jax.numpy as jnp
from jax import lax
from jax.experimental import pallas as pl
from jax.experimental.pallas import tpu as pltpu
from jax.experimental.pallas import tpu_sc as plsc

D_MODEL = 1024
CONV_WIDTH = 512
CONV_K = 3
POOL_WIDTH = 512
POOL_WINDOWS = (2, 4, 8, 16)
POOL_GW = 128
IN_PROJ = 3 * CONV_WIDTH + POOL_WIDTH
N_GROUPS = 4
EXPERTS_PER_GROUP = 8
N_EXPERTS = N_GROUPS * EXPERTS_PER_GROUP
D_EXPERT = 256
PLE_DIM = 256
LN_EPS = 1e-5
DEEPNORM_ALPHA = 2.0 ** 0.25

LANES = 128
HALF = D_MODEL // 2
CONV_HALO = 8
POOL_HALO = 16
SEQ_TILE = 1024
SUB_TILE = 256
MIXER_PHASES = 8
COMBINE_PHASES = 3
ROW_TILE = 256
TILE_BUFFERS = 8
SC_WORKERS = 32
SC_WINDOW = 64
WEIGHT_BUFFERS = 3
TOKEN_TILE = 1024
VMEM_LIMIT = 56 * 1024 * 1024

R_ID1, R_ID2, R_RANK1, R_RANK2, R_W1, R_W2 = range(6)
ROUTE_ROWS = 8
LOGIT_ROWS = 40


def _layernorm(x, g, b):
    mu = jnp.mean(x, axis=-1, keepdims=True)
    xc = x - mu
    var = jnp.mean(xc * xc, axis=-1, keepdims=True)
    return xc * lax.rsqrt(var + LN_EPS) * g + b


def _pack_bf16_pairs(v):
    bits = lax.bitcast_convert_type(v.astype(jnp.bfloat16).astype(jnp.float32), jnp.uint32)
    return bits[:, HALF:] | (bits[:, :HALF] >> 16)


def _unpack_bf16_pairs(w):
    lo = lax.bitcast_convert_type(w << 16, jnp.float32)
    hi = lax.bitcast_convert_type(w & jnp.uint32(0xFFFF0000), jnp.float32)
    return lo, hi


def _mixer_kernel(x_ref, lnin_g, lnin_b, w_in, conv_w, conv_b, pool_w, pool_scale, w_out,
                  ln1_g, ln1_b, w_r, b_r,
                  xp_ref, h1_ref, route_ref, route_t_ref, counts_ref,
                  zbuf, vbuf, carry):
    b = pl.program_id(0)
    s = pl.program_id(1)
    ts = x_ref.shape[0]

    @pl.when(s == 0)
    def _():
        zbuf[0:CONV_HALO, :] = jnp.zeros((CONV_HALO, CONV_WIDTH), jnp.float32)
        vbuf[0:POOL_HALO, :] = jnp.zeros((POOL_HALO, POOL_WIDTH), jnp.float32)

    @pl.when((b == 0) & (s == 0))
    def _():
        carry[...] = jnp.zeros_like(carry)

    chains = [_mixer_rows(r0, s * ts + r0, x_ref, lnin_g, lnin_b, w_in, conv_w, conv_b, pool_w, pool_scale,
                          w_out, ln1_g, ln1_b, w_r, b_r,
                          xp_ref, h1_ref, route_ref, route_t_ref, zbuf, vbuf, carry)
              for r0 in range(0, ts, SUB_TILE)]
    for t in range(MIXER_PHASES + len(chains) - 1):
        for k, chain in reversed(list(enumerate(chains))):
            if 0 <= t - k < MIXER_PHASES:
                next(chain, None)
    zbuf[0:CONV_HALO, :] = zbuf[ts:ts + CONV_HALO, :]
    vbuf[0:POOL_HALO, :] = vbuf[ts:ts + POOL_HALO, :]
    counts_ref[...] = jnp.transpose(carry[...])[0:1, :]


def _mixer_rows(r0, seq0, x_ref, lnin_g, lnin_b, w_in, conv_w, conv_b, pool_w, pool_scale, w_out,
                ln1_g, ln1_b, w_r, b_r,
                xp_ref, h1_ref, route_ref, route_t_ref, zbuf, vbuf, carry):
    n = SUB_TILE
    rows = pl.ds(r0, n)
    h0 = _layernorm(x_ref[rows, :], lnin_g[...], lnin_b[...])
    h0b = h0.astype(jnp.bfloat16)
    yield
    u_a = jnp.dot(h0b, w_in[:, 0:IN_PROJ // 2], preferred_element_type=jnp.float32)
    yield
    u_b = jnp.dot(h0b, w_in[:, IN_PROJ // 2:], preferred_element_type=jnp.float32)
    yield
    b_g = u_a[:, 0:CONV_WIDTH]
    c_g = u_a[:, CONV_WIDTH:]
    v_c = u_b[:, 0:CONV_WIDTH]
    v_p = u_b[:, CONV_WIDTH:]

    zbuf[pl.ds(CONV_HALO + r0, n), :] = c_g * v_c
    zext = zbuf[pl.ds(r0, n + CONV_HALO), :]
    z1 = pltpu.roll(zext, 1, axis=0)[CONV_HALO:, :]
    z2 = pltpu.roll(zext, 2, axis=0)[CONV_HALO:, :]
    conv = z2 * conv_w[0:1, :] + z1 * conv_w[1:2, :] + zext[CONV_HALO:, :] * conv_w[2:3, :] + conv_b[...]
    y_conv = b_g * conv

    vbuf[pl.ds(POOL_HALO + r0, n), :] = v_p
    vext = vbuf[pl.ds(r0, n + POOL_HALO), :]
    s2 = vext + pltpu.roll(vext, 1, axis=0)
    s4 = s2[:, POOL_GW:] + pltpu.roll(s2[:, POOL_GW:], 2, axis=0)
    s8 = s4[:, POOL_GW:] + pltpu.roll(s4[:, POOL_GW:], 4, axis=0)
    s16 = s8[:, POOL_GW:] + pltpu.roll(s8[:, POOL_GW:], 8, axis=0)
    wsums = (s2[POOL_HALO:, 0:POOL_GW], s4[POOL_HALO:, 0:POOL_GW],
             s8[POOL_HALO:, 0:POOL_GW], s16[POOL_HALO:, 0:POOL_GW])

    t_pos = seq0 + lax.broadcasted_iota(jnp.int32, (n, 1), 0)
    y_pool = []
    for j, w in enumerate(POOL_WINDOWS):
        inv_cnt = 1.0 / jnp.minimum(t_pos + 1, w).astype(jnp.float32)
        pooled = wsums[j] * inv_cnt - v_p[:, j * POOL_GW:(j + 1) * POOL_GW]
        y_pool.append(jnp.dot(pooled.astype(jnp.bfloat16), pool_w[j], preferred_element_type=jnp.float32))
    y_pool = jnp.concatenate(y_pool, axis=-1) * pool_scale[...]

    ycat = jnp.concatenate([y_conv, y_pool], axis=-1).astype(jnp.bfloat16)
    yield
    mix = jnp.dot(ycat, w_out[...], preferred_element_type=jnp.float32)
    yield
    h1 = _layernorm(DEEPNORM_ALPHA * h0 + mix, ln1_g[...], ln1_b[...])

    h_hi = h1.astype(jnp.bfloat16)
    h_lo = (h1 - h_hi.astype(jnp.float32)).astype(jnp.bfloat16)
    xp_ref[rows, :] = _pack_bf16_pairs(h1)
    h1_ref[rows, :] = h1
    hcat = jnp.concatenate([h_hi, h_lo, h_hi], axis=-1)
    yield
    logits = jnp.dot(hcat, w_r[...], preferred_element_type=jnp.float32) + b_r[...]
    yield
    lt = jnp.transpose(logits)[0:LOGIT_ROWS, :]
    rid = lax.broadcasted_iota(jnp.int32, (LOGIT_ROWS, n), 0).astype(jnp.float32)
    neg = jnp.float32(-jnp.inf)

    def first_argmax(vals):
        m = jnp.max(vals, axis=0, keepdims=True)
        idx = jnp.min(jnp.where(vals == m, rid, float(LOGIT_ROWS)), axis=0, keepdims=True)
        return m, idx

    g_mask = rid < N_GROUPS
    g_max, g_idx = first_argmax(jnp.where(g_mask, lt, neg))
    g_w = 1.0 / jnp.sum(jnp.where(g_mask, jnp.exp(lt - g_max), 0.0), axis=0, keepdims=True)

    e_lo = N_GROUPS + EXPERTS_PER_GROUP * g_idx
    e_vals = jnp.where((rid >= e_lo) & (rid < e_lo + EXPERTS_PER_GROUP), lt, neg)
    m1, i1 = first_argmax(e_vals)
    m2, i2 = first_argmax(jnp.where(rid == i1, neg, e_vals))
    e21 = jnp.exp(m2 - m1)
    w1 = g_w / (1.0 + e21)
    w2 = g_w * e21 / (1.0 + e21)
    id1 = i1 - N_GROUPS
    id2 = i2 - N_GROUPS

    eid = lax.broadcasted_iota(jnp.int32, (N_EXPERTS, n), 0).astype(jnp.float32)
    sel1 = eid == id1
    sel2 = eid == id2
    onehot = (sel1 | sel2).astype(jnp.float32)
    src = lax.broadcasted_iota(jnp.int32, (n, n), 0)
    dst = lax.broadcasted_iota(jnp.int32, (n, n), 1)
    earlier = (src < dst).astype(jnp.bfloat16)
    before = (jnp.dot(onehot.astype(jnp.bfloat16), earlier, preferred_element_type=jnp.float32)
              + carry[0:N_EXPERTS, 0:1])
    rank1 = jnp.sum(jnp.where(sel1, before, 0.0), axis=0, keepdims=True)
    rank2 = jnp.sum(jnp.where(sel2, before, 0.0), axis=0, keepdims=True)
    carry[0:N_EXPERTS, :] = carry[0:N_EXPERTS, :] + jnp.sum(onehot, axis=1, keepdims=True)

    rec_t = jnp.zeros((ROUTE_ROWS, n), jnp.float32)
    rec_row = lax.broadcasted_iota(jnp.int32, (ROUTE_ROWS, n), 0)
    for k, val in ((R_ID1, id1), (R_ID2, id2), (R_W1, w1), (R_W2, w2), (R_RANK1, rank1), (R_RANK2, rank2)):
        rec_t = jnp.where(rec_row == k, val, rec_t)
    route_t_ref[:, rows] = rec_t
    padded = jnp.concatenate([rec_t, jnp.zeros((LANES - ROUTE_ROWS, n), jnp.float32)], axis=0)
    route_ref[rows, :] = jnp.transpose(padded)


def _run_mixer(x2, lnin_g, lnin_b, w_in, conv_w, conv_b, pool_w, pool_scale, w_out, ln1_g, ln1_b,
               w_r, b_r, batch, seq):
    n_tok = batch * seq
    n_s = seq // SEQ_TILE
    tok_map = lambda b, s: (b * n_s + s, 0)

    def const(shape):
        return pl.BlockSpec(shape, lambda b, s: (0,) * len(shape), pipeline_mode=pl.Buffered(1))

    in_specs = [
        pl.BlockSpec((SEQ_TILE, D_MODEL), tok_map),
        const((1, D_MODEL)), const((1, D_MODEL)),
        const((D_MODEL, IN_PROJ)),
        const((CONV_K, CONV_WIDTH)), const((1, CONV_WIDTH)),
        const((len(POOL_WINDOWS), POOL_GW, POOL_GW)), const((1, POOL_WIDTH)),
        const((D_MODEL, D_MODEL)),
        const((1, D_MODEL)), const((1, D_MODEL)),
        const((3 * D_MODEL, LANES)), const((1, LANES)),
    ]
    out_specs = [
        pl.BlockSpec((SEQ_TILE, HALF), tok_map),
        pl.BlockSpec((SEQ_TILE, D_MODEL), tok_map),
        pl.BlockSpec((SEQ_TILE, LANES), tok_map),
        pl.BlockSpec((ROUTE_ROWS, SEQ_TILE), lambda b, s: (0, b * n_s + s)),
        pl.BlockSpec((1, LANES), lambda b, s: (0, 0)),
    ]
    out_shape = [
        jax.ShapeDtypeStruct((n_tok, HALF), jnp.uint32),
        jax.ShapeDtypeStruct((n_tok, D_MODEL), jnp.float32),
        jax.ShapeDtypeStruct((n_tok, LANES), jnp.float32),
        jax.ShapeDtypeStruct((ROUTE_ROWS, n_tok), jnp.float32),
        jax.ShapeDtypeStruct((1, LANES), jnp.float32),
    ]
    return pl.pallas_call(
        _mixer_kernel,
        grid=(batch, n_s),
        in_specs=in_specs,
        out_specs=out_specs,
        out_shape=out_shape,
        scratch_shapes=[
            pltpu.VMEM((SEQ_TILE + CONV_HALO, CONV_WIDTH), jnp.float32),
            pltpu.VMEM((SEQ_TILE + POOL_HALO, POOL_WIDTH), jnp.float32),
            pltpu.VMEM((LANES, LANES), jnp.float32),
        ],
        compiler_params=pltpu.CompilerParams(
            dimension_semantics=("arbitrary", "arbitrary"), vmem_limit_bytes=VMEM_LIMIT),
    )(x2, lnin_g, lnin_b, w_in, conv_w, conv_b, pool_w, pool_scale, w_out, ln1_g, ln1_b, w_r, b_r)


def _plan_kernel(rt_ref, counts_ref, pos_ref, tile_start_ref, tiles_ref, pad_ref, *, n_rows):
    lane = lax.broadcasted_iota(jnp.int32, (ROUTE_ROWS, LANES), 1)
    counts = jnp.broadcast_to(counts_ref[...], (ROUTE_ROWS, LANES))
    tiles = jnp.floor((counts + (ROW_TILE - 1)) * (1.0 / ROW_TILE))
    tile_end = tiles
    shift = 1
    while shift < N_EXPERTS:
        tile_end = tile_end + jnp.where(lane >= shift, pltpu.roll(tile_end, shift, axis=1), 0.0)
        shift *= 2
    row_start = (tile_end - tiles) * ROW_TILE
    tile_start_ref[...] = (tile_end - tiles)[0:1, :].astype(jnp.int32)
    tiles_ref[...] = tiles[0:1, :].astype(jnp.int32)

    rt = rt_ref[...]
    ids = rt[R_ID1:R_ID2 + 1, :]
    start = jnp.zeros_like(ids)
    for e in range(N_EXPERTS):
        start = jnp.where(ids == e, row_start[0:1, e:e + 1], start)
    pos_ref[...] = (start + rt[R_RANK1:R_RANK2 + 1, :]).astype(jnp.int32)

    sub = lax.broadcasted_iota(jnp.int32, (N_EXPERTS, LANES), 0)
    lane_e = lax.broadcasted_iota(jnp.int32, (N_EXPERTS, LANES), 1)
    diag = sub == lane_e
    pad_lo = jnp.sum(jnp.where(diag, (row_start + counts)[0:1, :], 0.0), axis=1, keepdims=True)
    pad_n = jnp.sum(jnp.where(diag, (tiles * ROW_TILE - counts)[0:1, :], 0.0), axis=1, keepdims=True)
    j = lax.broadcasted_iota(jnp.int32, (N_EXPERTS, ROW_TILE), 1).astype(jnp.float32)
    pad_ref[...] = jnp.where(j < pad_n, pad_lo + j, n_rows + j).astype(jnp.int32)


def _run_plan(route_t, counts, n_rows):
    n_tok = route_t.shape[1]
    full = lambda shape: pl.BlockSpec(shape, lambda i: (0, 0))
    return pl.pallas_call(
        functools.partial(_plan_kernel, n_rows=n_rows),
        grid=(1,),
        in_specs=[full((ROUTE_ROWS, n_tok)), full((1, LANES))],
        out_specs=[full((2, n_tok)), full((1, LANES)), full((1, LANES)), full((N_EXPERTS, ROW_TILE))],
        out_shape=[
            jax.ShapeDtypeStruct((2, n_tok), jnp.int32),
            jax.ShapeDtypeStruct((1, LANES), jnp.int32),
            jax.ShapeDtypeStruct((1, LANES), jnp.int32),
            jax.ShapeDtypeStruct((N_EXPERTS, ROW_TILE), jnp.int32),
        ],
        compiler_params=pltpu.CompilerParams(dimension_semantics=("arbitrary",)),
    )(route_t, counts)


def _sc_mesh():
    return plsc.VectorSubcoreMesh(core_axis_name="core", subcore_axis_name="subcore")


def _sc_worker_id():
    return lax.axis_index("core") * (SC_WORKERS // 2) + lax.axis_index("subcore")


def _dispatch_rows(xp, pos1, pos2, pad_pos, zero_rows, n_rows):
    n_tok, width = xp.shape
    n_win = n_tok // SC_WORKERS // SC_WINDOW
    n_pad = pad_pos.shape[0] // SC_WORKERS // SC_WINDOW
    as_windows = lambda v: v.reshape(-1, SC_WINDOW)

    @functools.partial(
        pl.kernel, out_type=jax.ShapeDtypeStruct((n_rows + ROW_TILE, width), xp.dtype), mesh=_sc_mesh(),
        scratch_types=[pltpu.VMEM((n_win, SC_WINDOW), jnp.int32), pltpu.VMEM((n_win, SC_WINDOW), jnp.int32),
                       pltpu.VMEM((n_pad, SC_WINDOW), jnp.int32),
                       pltpu.VMEM((2, SC_WINDOW, width), xp.dtype),
                       pltpu.SemaphoreType.DMA((2,)), pltpu.SemaphoreType.DMA((2,))])
    def dispatch(xp_hbm, pos1_hbm, pos2_hbm, pad_hbm, zero_hbm, out_hbm, idx1, idx2, idxp, buf, lsem, ssem):
        wid = _sc_worker_id()
        pltpu.sync_copy(pos1_hbm.at[pl.ds(wid * n_win, n_win)], idx1)
        pltpu.sync_copy(pos2_hbm.at[pl.ds(wid * n_win, n_win)], idx2)
        pltpu.sync_copy(pad_hbm.at[pl.ds(wid * n_pad, n_pad)], idxp)

        def load(j):
            rows = xp_hbm.at[pl.ds((wid * n_win + j) * SC_WINDOW, SC_WINDOW)]
            return pltpu.make_async_copy(rows, buf.at[j % 2], lsem.at[j % 2])

        def scatters(j):
            return [pltpu.make_async_copy(buf.at[j % 2], out_hbm.at[idx.at[j]], ssem.at[j % 2])
                    for idx in (idx1, idx2)]

        load(0).start()
        for j in range(n_win):
            load(j).wait()
            for cp in scatters(j):
                cp.start()
            if j >= 1:
                for cp in scatters(j - 1):
                    cp.wait()
            if j + 1 < n_win:
                load(j + 1).start()
        for cp in scatters(n_win - 1):
            cp.wait()

        pltpu.sync_copy(zero_hbm, buf.at[0])
        pads = [pltpu.make_async_copy(buf.at[0], out_hbm.at[idxp.at[j]], ssem.at[0]) for j in range(n_pad)]
        for cp in pads:
            cp.start()
        for cp in pads:
            cp.wait()

    return dispatch(xp, as_windows(pos1), as_windows(pos2), as_windows(pad_pos), zero_rows)


def _gather_rows(src, idx):
    n_out, width = idx.shape[0], src.shape[1]
    n_win = n_out // SC_WORKERS // SC_WINDOW

    @functools.partial(
        pl.kernel, out_type=jax.ShapeDtypeStruct((n_out, width), src.dtype), mesh=_sc_mesh(),
        scratch_types=[pltpu.VMEM((n_win, SC_WINDOW), jnp.int32), pltpu.VMEM((2, SC_WINDOW, width), src.dtype),
                       pltpu.SemaphoreType.DMA((2,)), pltpu.SemaphoreType.DMA((2,))])
    def gather(src_hbm, idx_hbm, dst_hbm, idx_v, buf, gsem, ssem):
        wid = _sc_worker_id()
        pltpu.sync_copy(idx_hbm.at[pl.ds(wid * n_win, n_win)], idx_v)

        def fetch(j):
            return pltpu.make_async_copy(src_hbm.at[idx_v.at[j]], buf.at[j % 2], gsem.at[j % 2])

        def store(j):
            rows = dst_hbm.at[pl.ds((wid * n_win + j) * SC_WINDOW, SC_WINDOW)]
            return pltpu.make_async_copy(buf.at[j % 2], rows, ssem.at[j % 2])

        fetch(0).start()
        for j in range(n_win):
            fetch(j).wait()
            store(j).start()
            if j >= 1:
                store(j - 1).wait()
            if j + 1 < n_win:
                fetch(j + 1).start()
        store(n_win - 1).wait()

    return gather(src, idx.reshape(-1, SC_WINDOW))


def _tile_copy(hbm, buf, sem, tile, slot, to_hbm):
    rows = hbm.at[pl.ds(pl.multiple_of(tile * ROW_TILE, ROW_TILE), ROW_TILE)]
    if to_hbm:
        return pltpu.make_async_copy(buf.at[slot], rows, sem.at[slot])
    return pltpu.make_async_copy(rows, buf.at[slot], sem.at[slot])


def _weight_copies(w_hbm, wbuf, wsem, expert):
    slot = expert % WEIGHT_BUFFERS
    return [pltpu.make_async_copy(w.at[expert], buf.at[slot], wsem.at[slot]) for w, buf in zip(w_hbm, wbuf)]


def _expert_kernel(ts_ref, nte_ref, x_hbm, wg_hbm, wu_hbm, wd_hbm, y_hbm,
                   xbuf, ybuf, xsem, ysem, wg_buf, wu_buf, wd_buf, wsem, wgu_bf, wd_bf, act_ref, *, max_tiles):
    e = pl.program_id(0)
    first = ts_ref[e]
    count = nte_ref[e]
    n_tiles = ts_ref[N_EXPERTS - 1] + nte_ref[N_EXPERTS - 1]
    ahead = TILE_BUFFERS - 1
    w_hbm = (wg_hbm, wu_hbm, wd_hbm)
    wbuf = (wg_buf, wu_buf, wd_buf)

    @pl.when(e == 0)
    def _():
        for g in range(ahead):
            @pl.when(g < n_tiles)
            def _():
                _tile_copy(x_hbm, xbuf, xsem, g, g, False).start()
        for k in range(WEIGHT_BUFFERS - 1):
            for cp in _weight_copies(w_hbm, wbuf, wsem, k):
                cp.start()

    @pl.when(e + WEIGHT_BUFFERS - 1 < N_EXPERTS)
    def _():
        for cp in _weight_copies(w_hbm, wbuf, wsem, e + WEIGHT_BUFFERS - 1):
            cp.start()

    for cp in _weight_copies(w_hbm, wbuf, wsem, e):
        cp.wait()

    def sync_x(g):
        _tile_copy(x_hbm, xbuf, xsem, g, g % TILE_BUFFERS, False).wait()

        @pl.when(g + ahead < n_tiles)
        def _():
            _tile_copy(x_hbm, xbuf, xsem, g + ahead, (g + ahead) % TILE_BUFFERS, False).start()

    def sync_y_slot(g):
        @pl.when(g >= TILE_BUFFERS)
        def _():
            _tile_copy(y_hbm, ybuf, ysem, g - TILE_BUFFERS, g % TILE_BUFFERS, True).wait()

    def up_proj(g):
        lo, hi = _unpack_bf16_pairs(xbuf[g % TILE_BUFFERS])
        return (jnp.dot(lo.astype(jnp.bfloat16), wgu_bf[0:HALF, :], preferred_element_type=jnp.float32)
                + jnp.dot(hi.astype(jnp.bfloat16), wgu_bf[HALF:, :], preferred_element_type=jnp.float32))

    def put_act(hgu):
        hg = hgu[:, 0:D_EXPERT]
        act_ref[...] = (hg * jax.nn.sigmoid(hg) * hgu[:, D_EXPERT:]).astype(jnp.bfloat16)

    def down_proj(g):
        y = jnp.dot(act_ref[...], wd_bf[...], preferred_element_type=jnp.float32)
        ybuf[g % TILE_BUFFERS] = _pack_bf16_pairs(y)

    @pl.when(count > 0)
    def _():
        slot = e % WEIGHT_BUFFERS
        wgu_bf[:, 0:D_EXPERT] = wg_buf[slot].astype(jnp.bfloat16)
        wgu_bf[:, D_EXPERT:] = wu_buf[slot].astype(jnp.bfloat16)
        wd_bf[...] = wd_buf[slot].astype(jnp.bfloat16)

        sync_x(first)
        put_act(up_proj(first))

        def tile_body(g, c):
            sync_x(g)
            sync_y_slot(g - 1)
            down_proj(g - 1)
            hgu = up_proj(g)
            put_act(hgu)
            _tile_copy(y_hbm, ybuf, ysem, g - 1, (g - 1) % TILE_BUFFERS, True).start()
            return c

        lax.fori_loop(first + 1, first + count, tile_body, 0)
        last = first + count - 1
        sync_y_slot(last)
        down_proj(last)
        _tile_copy(y_hbm, ybuf, ysem, last, last % TILE_BUFFERS, True).start()

    @pl.when(e == N_EXPERTS - 1)
    def _():
        for k in range(TILE_BUFFERS, 0, -1):
            @pl.when(n_tiles >= k)
            def _():
                _tile_copy(y_hbm, ybuf, ysem, n_tiles - k, (n_tiles - k) % TILE_BUFFERS, True).wait()

        ybuf[0] = jnp.zeros((ROW_TILE, HALF), jnp.uint32)

        def fill(g, c):
            cp = _tile_copy(y_hbm, ybuf, ysem, g, 0, True)
            cp.start()
            cp.wait()
            return c

        lax.fori_loop(n_tiles, max_tiles, fill, 0)


def _run_experts(tile_start, tiles_per_expert, x_sorted, w_gate, w_up, w_down):
    n_rows = x_sorted.shape[0]
    max_tiles = n_rows // ROW_TILE
    hbm = pl.BlockSpec(memory_space=pl.ANY)

    grid_spec = pltpu.PrefetchScalarGridSpec(
        num_scalar_prefetch=2,
        grid=(N_EXPERTS,),
        in_specs=[hbm, hbm, hbm, hbm],
        out_specs=hbm,
        scratch_shapes=[
            pltpu.VMEM((TILE_BUFFERS, ROW_TILE, HALF), jnp.uint32),
            pltpu.VMEM((TILE_BUFFERS, ROW_TILE, HALF), jnp.uint32),
            pltpu.SemaphoreType.DMA((TILE_BUFFERS,)),
            pltpu.SemaphoreType.DMA((TILE_BUFFERS,)),
            pltpu.VMEM((WEIGHT_BUFFERS, D_MODEL, D_EXPERT), jnp.float32),
            pltpu.VMEM((WEIGHT_BUFFERS, D_MODEL, D_EXPERT), jnp.float32),
            pltpu.VMEM((WEIGHT_BUFFERS, D_EXPERT, D_MODEL), jnp.float32),
            pltpu.SemaphoreType.DMA((WEIGHT_BUFFERS,)),
            pltpu.VMEM((D_MODEL, 2 * D_EXPERT), jnp.bfloat16),
            pltpu.VMEM((D_EXPERT, D_MODEL), jnp.bfloat16),
            pltpu.VMEM((ROW_TILE, D_EXPERT), jnp.bfloat16),
        ],
    )
    return pl.pallas_call(
        functools.partial(_expert_kernel, max_tiles=max_tiles),
        grid_spec=grid_spec,
        out_shape=jax.ShapeDtypeStruct((n_rows, HALF), jnp.uint32),
        compiler_params=pltpu.CompilerParams(
            dimension_semantics=("arbitrary",), vmem_limit_bytes=VMEM_LIMIT),
    )(tile_start, tiles_per_expert, x_sorted, w_gate, w_up, w_down)


def _combine_rows(r0, h1_ref, p_ref, route_ref, ya_ref, yb_ref, w_pg, b_pg, w_ple, g_ref, b_ref, o_ref):
    rows = pl.ds(r0, SUB_TILE)
    h1 = h1_ref[rows, :]
    h_hi = h1.astype(jnp.bfloat16)
    p_b = p_ref[rows, :].astype(jnp.bfloat16)
    yield
    gate_pre = jnp.dot(h_hi, w_pg[...], preferred_element_type=jnp.float32)
    ple_pre = jnp.dot(p_b, w_ple[...], preferred_element_type=jnp.float32)
    yield
    ple = ple_pre * jax.nn.sigmoid(gate_pre + b_pg[...])
    route = route_ref[rows, :]
    w1 = route[:, R_W1:R_W1 + 1]
    w2 = route[:, R_W2:R_W2 + 1]
    a_lo, a_hi = _unpack_bf16_pairs(ya_ref[rows, :])
    b_lo, b_hi = _unpack_bf16_pairs(yb_ref[rows, :])
    moe = jnp.concatenate([w1 * a_lo + w2 * b_lo, w1 * a_hi + w2 * b_hi], axis=-1)
    o_ref[rows, :] = _layernorm(DEEPNORM_ALPHA * h1 + ple + moe, g_ref[...], b_ref[...])


def _combine_kernel(h1_ref, p_ref, route_ref, ya_ref, yb_ref, w_pg, b_pg, w_ple, g_ref, b_ref, o_ref):
    chains = [_combine_rows(r0, h1_ref, p_ref, route_ref, ya_ref, yb_ref, w_pg, b_pg, w_ple, g_ref, b_ref, o_ref)
              for r0 in range(0, h1_ref.shape[0], SUB_TILE)]
    for t in range(COMBINE_PHASES + len(chains) - 1):
        for k, chain in reversed(list(enumerate(chains))):
            if 0 <= t - k < COMBINE_PHASES:
                next(chain, None)


def _run_combine(h1, p2, route, y_tok, w_pg, b_pg, w_ple, ln2_g, ln2_b):
    n_tok = h1.shape[0]
    n_t = n_tok // TOKEN_TILE
    tok_map = lambda i: (i, 0)
    const = lambda shape: pl.BlockSpec(shape, lambda i: (0, 0), pipeline_mode=pl.Buffered(1))
    in_specs = [
        pl.BlockSpec((TOKEN_TILE, D_MODEL), tok_map),
        pl.BlockSpec((TOKEN_TILE, PLE_DIM), tok_map),
        pl.BlockSpec((TOKEN_TILE, LANES), tok_map),
        pl.BlockSpec((TOKEN_TILE, HALF), lambda i: (i, 0)),
        pl.BlockSpec((TOKEN_TILE, HALF), lambda i: (i + n_t, 0)),
        const((D_MODEL, D_MODEL)), const((1, D_MODEL)), const((PLE_DIM, D_MODEL)),
        const((1, D_MODEL)), const((1, D_MODEL)),
    ]
    return pl.pallas_call(
        _combine_kernel,
        grid=(n_t,),
        in_specs=in_specs,
        out_specs=pl.BlockSpec((TOKEN_TILE, D_MODEL), tok_map),
        out_shape=jax.ShapeDtypeStruct((n_tok, D_MODEL), jnp.float32),
        compiler_params=pltpu.CompilerParams(dimension_semantics=("arbitrary",), vmem_limit_bytes=VMEM_LIMIT),
    )(h1, p2, route, y_tok, y_tok, w_pg, b_pg, w_ple, ln2_g, ln2_b)


def _split_bf16(w):
    hi = w.astype(jnp.bfloat16)
    lo = (w - hi.astype(jnp.float32)).astype(jnp.bfloat16)
    return hi, lo


def kernel(x, p, ln_in_g, ln_in_b, w_in, conv_w, conv_b, pool_w, pool_scale, w_out, ln1_g, ln1_b,
           w_rg, b_rg, w_re, b_re, w_gate, w_up, w_down, w_pg, b_pg, w_ple, ln2_g, ln2_b):
    batch, seq, _ = x.shape
    n_tok = batch * seq
    bf = jnp.bfloat16
    row = lambda v: v.reshape(1, -1)

    w_r = jnp.concatenate([w_rg[0], jnp.transpose(w_re[0], (1, 0, 2)).reshape(D_MODEL, N_EXPERTS)], axis=1)
    w_r = jnp.pad(w_r, ((0, 0), (0, LANES - w_r.shape[1])))
    w_r_hi, w_r_lo = _split_bf16(w_r)
    w_r_cat = jnp.concatenate([w_r_hi, w_r_hi, w_r_lo], axis=0)
    b_r = jnp.pad(jnp.concatenate([b_rg[0], b_re[0].reshape(-1)]), (0, LANES - N_GROUPS - N_EXPERTS)).reshape(1, LANES)

    x2 = x.reshape(n_tok, D_MODEL)
    p2 = p[0].reshape(n_tok, PLE_DIM)
    mixer_weights = (row(ln_in_g), row(ln_in_b), w_in[0].astype(bf), conv_w[0], row(conv_b[0]),
                     pool_w[0].astype(bf), row(pool_scale[0]), w_out[0].astype(bf), row(ln1_g[0]), row(ln1_b[0]),
                     w_r_cat, b_r)
    combine_weights = (w_pg[0].astype(bf), row(b_pg[0]), w_ple[0].astype(bf), row(ln2_g[0]), row(ln2_b[0]))
    expert_weights = (w_gate[0].reshape(N_EXPERTS, D_MODEL, D_EXPERT),
                      w_up[0].reshape(N_EXPERTS, D_MODEL, D_EXPERT),
                      w_down[0].reshape(N_EXPERTS, D_EXPERT, D_MODEL))
    zero_rows = jnp.zeros((SC_WINDOW, HALF), jnp.uint32)

    n_rows = -(-(2 * n_tok + N_EXPERTS * (ROW_TILE - 1)) // ROW_TILE) * ROW_TILE
    xp, h1, route, route_t, counts = _run_mixer(x2, *mixer_weights, batch, seq)
    pos, tile_start, tiles_per_expert, pad_pos = _run_plan(route_t, counts, n_rows)
    x_sorted = _dispatch_rows(xp, pos[0], pos[1], pad_pos.reshape(-1), zero_rows, n_rows)
    y_sorted = _run_experts(tile_start[0, :N_EXPERTS], tiles_per_expert[0, :N_EXPERTS], x_sorted, *expert_weights)

    y_tok = _gather_rows(y_sorted, pos.reshape(-1))
    out = _run_combine(h1, p2, route, y_tok, *combine_weights)
    return out.reshape(batch, seq, D_MODEL)
```

```python
import functools

import jax
import jax.numpy as jnp
from jax import lax
from jax.experimental import pallas as pl
from jax.experimental.pallas import tpu as pltpu
from jax.experimental.pallas import tpu_sc as plsc

D_MODEL = 1024
CONV_WIDTH = 512
CONV_K = 3
POOL_WIDTH = 512
POOL_WINDOWS = (2, 4, 8, 16)
POOL_GW = 128
IN_PROJ = 3 * CONV_WIDTH + POOL_WIDTH
N_GROUPS = 4
EXPERTS_PER_GROUP = 8
N_EXPERTS = N_GROUPS * EXPERTS_PER_GROUP
D_EXPERT = 256
PLE_DIM = 256
LN_EPS = 1e-5
DEEPNORM_ALPHA = 2.0 ** 0.25

LANES = 128
HALF = D_MODEL // 2
CONV_HALO = 8
POOL_HALO = 16
SEQ_TILE = 1024
SUB_TILE = 256
MIXER_PHASES = 8
COMBINE_PHASES = 3
ROW_TILE = 256
TILE_BUFFERS = 8
SC_WORKERS = 32
SC_WINDOW = 64
WEIGHT_BUFFERS = 3
TOKEN_TILE = 1024
VMEM_LIMIT = 56 * 1024 * 1024

R_ID1, R_ID2, R_RANK1, R_RANK2, R_W1, R_W2 = range(6)
ROUTE_ROWS = 8
LOGIT_ROWS = 40


def _layernorm(x, g, b):
    mu = jnp.mean(x, axis=-1, keepdims=True)
    xc = x - mu
    var = jnp.mean(xc * xc, axis=-1, keepdims=True)
    return xc * lax.rsqrt(var + LN_EPS) * g + b


def _pack_bf16_pairs(v):
    bits = lax.bitcast_convert_type(v.astype(jnp.bfloat16).astype(jnp.float32), jnp.uint32)
    return bits[:, HALF:] | (bits[:, :HALF] >> 16)


def _unpack_bf16_pairs(w):
    lo = lax.bitcast_convert_type(w << 16, jnp.float32)
    hi = lax.bitcast_convert_type(w & jnp.uint32(0xFFFF0000), jnp.float32)
    return lo, hi


def _mixer_kernel(x_ref, lnin_g, lnin_b, w_in, conv_w, conv_b, pool_w, pool_scale, w_out,
                  ln1_g, ln1_b, w_r, b_r,
                  xp_ref, h1_ref, route_ref, route_t_ref, counts_ref,
                  zbuf, vbuf, carry):
    b = pl.program_id(0)
    s = pl.program_id(1)
    ts = x_ref.shape[0]

    @pl.when(s == 0)
    def _():
        zbuf[0:CONV_HALO, :] = jnp.zeros((CONV_HALO, CONV_WIDTH), jnp.float32)
        vbuf[0:POOL_HALO, :] = jnp.zeros((POOL_HALO, POOL_WIDTH), jnp.float32)

    @pl.when((b == 0) & (s == 0))
    def _():
        carry[...] = jnp.zeros_like(carry)

    chains = [_mixer_rows(r0, s * ts + r0, x_ref, lnin_g, lnin_b, w_in, conv_w, conv_b, pool_w, pool_scale,
                          w_out, ln1_g, ln1_b, w_r, b_r,
                          xp_ref, h1_ref, route_ref, route_t_ref, zbuf, vbuf, carry)
              for r0 in range(0, ts, SUB_TILE)]
    for t in range(MIXER_PHASES + len(chains) - 1):
        for k, chain in reversed(list(enumerate(chains))):
            if 0 <= t - k < MIXER_PHASES:
                next(chain, None)
    zbuf[0:CONV_HALO, :] = zbuf[ts:ts + CONV_HALO, :]
    vbuf[0:POOL_HALO, :] = vbuf[ts:ts + POOL_HALO, :]
    counts_ref[...] = jnp.transpose(carry[...])[0:1, :]


def _mixer_rows(r0, seq0, x_ref, lnin_g, lnin_b, w_in, conv_w, conv_b, pool_w, pool_scale, w_out,
                ln1_g, ln1_b, w_r, b_r,
                xp_ref, h1_ref, route_ref, route_t_ref, zbuf, vbuf, carry):
    n = SUB_TILE
    rows = pl.ds(r0, n)
    h0 = _layernorm(x_ref[rows, :], lnin_g[...], lnin_b[...])
    h0b = h0.astype(jnp.bfloat16)
    yield
    u_a = jnp.dot(h0b, w_in[:, 0:IN_PROJ // 2], preferred_element_type=jnp.float32)
    yield
    u_b = jnp.dot(h0b, w_in[:, IN_PROJ // 2:], preferred_element_type=jnp.float32)
    yield
    b_g = u_a[:, 0:CONV_WIDTH]
    c_g = u_a[:, CONV_WIDTH:]
    v_c = u_b[:, 0:CONV_WIDTH]
    v_p = u_b[:, CONV_WIDTH:]

    zbuf[pl.ds(CONV_HALO + r0, n), :] = c_g * v_c
    zext = zbuf[pl.ds(r0, n + CONV_HALO), :]
    z1 = pltpu.roll(zext, 1, axis=0)[CONV_HALO:, :]
    z2 = pltpu.roll(zext, 2, axis=0)[CONV_HALO:, :]
    conv = z2 * conv_w[0:1, :] + z1 * conv_w[1:2, :] + zext[CONV_HALO:, :] * conv_w[2:3, :] + conv_b[...]
    y_conv = b_g * conv

    vbuf[pl.ds(POOL_HALO + r0, n), :] = v_p
    vext = vbuf[pl.ds(r0, n + POOL_HALO), :]
    s2 = vext + pltpu.roll(vext, 1, axis=0)
    s4 = s2[:, POOL_GW:] + pltpu.roll(s2[:, POOL_GW:], 2, axis=0)
    s8 = s4[:, POOL_GW:] + pltpu.roll(s4[:, POOL_GW:], 4, axis=0)
    s16 = s8[:, POOL_GW:] + pltpu.roll(s8[:, POOL_GW:], 8, axis=0)
    wsums = (s2[POOL_HALO:, 0:POOL_GW], s4[POOL_HALO:, 0:POOL_GW],
             s8[POOL_HALO:, 0:POOL_GW], s16[POOL_HALO:, 0:POOL_GW])

    t_pos = seq0 + lax.broadcasted_iota(jnp.int32, (n, 1), 0)
    y_pool = []
    for j, w in enumerate(POOL_WINDOWS):
        inv_cnt = 1.0 / jnp.minimum(t_pos + 1, w).astype(jnp.float32)
        pooled = wsums[j] * inv_cnt - v_p[:, j * POOL_GW:(j + 1) * POOL_GW]
        y_pool.append(jnp.dot(pooled.astype(jnp.bfloat16), pool_w[j], preferred_element_type=jnp.float32))
    y_pool = jnp.concatenate(y_pool, axis=-1) * pool_scale[...]

    ycat = jnp.concatenate([y_conv, y_pool], axis=-1).astype(jnp.bfloat16)
    yield
    mix = jnp.dot(ycat, w_out[...], preferred_element_type=jnp.float32)
    yield
    h1 = _layernorm(DEEPNORM_ALPHA * h0 + mix, ln1_g[...], ln1_b[...])

    h_hi = h1.astype(jnp.bfloat16)
    h_lo = (h1 - h_hi.astype(jnp.float32)).astype(jnp.bfloat16)
    xp_ref[rows, :] = _pack_bf16_pairs(h1)
    h1_ref[rows, :] = h1
    hcat = jnp.concatenate([h_hi, h_lo, h_hi], axis=-1)
    yield
    logits = jnp.dot(hcat, w_r[...], preferred_element_type=jnp.float32) + b_r[...]
    yield
    lt = jnp.transpose(logits)[0:LOGIT_ROWS, :]
    rid = lax.broadcasted_iota(jnp.int32, (LOGIT_ROWS, n), 0).astype(jnp.float32)
    neg = jnp.float32(-jnp.inf)

    def first_argmax(vals):
        m = jnp.max(vals, axis=0, keepdims=True)
        idx = jnp.min(jnp.where(vals == m, rid, float(LOGIT_ROWS)), axis=0, keepdims=True)
        return m, idx

    g_mask = rid < N_GROUPS
    g_max, g_idx = first_argmax(jnp.where(g_mask, lt, neg))
    g_w = 1.0 / jnp.sum(jnp.where(g_mask, jnp.exp(lt - g_max), 0.0), axis=0, keepdims=True)

    e_lo = N_GROUPS + EXPERTS_PER_GROUP * g_idx
    e_vals = jnp.where((rid >= e_lo) & (rid < e_lo + EXPERTS_PER_GROUP), lt, neg)
    m1, i1 = first_argmax(e_vals)
    m2, i2 = first_argmax(jnp.where(rid == i1, neg, e_vals))
    e21 = jnp.exp(m2 - m1)
    w1 = g_w / (1.0 + e21)
    w2 = g_w * e21 / (1.0 + e21)
    id1 = i1 - N_GROUPS
    id2 = i2 - N_GROUPS

    eid = lax.broadcasted_iota(jnp.int32, (N_EXPERTS, n), 0).astype(jnp.float32)
    sel1 = eid == id1
    sel2 = eid == id2
    onehot = (sel1 | sel2).astype(jnp.float32)
    src = lax.broadcasted_iota(jnp.int32, (n, n), 0)
    dst = lax.broadcasted_iota(jnp.int32, (n, n), 1)
    earlier = (src < dst).astype(jnp.bfloat16)
    before = (jnp.dot(onehot.astype(jnp.bfloat16), earlier, preferred_element_type=jnp.float32)
              + carry[0:N_EXPERTS, 0:1])
    rank1 = jnp.sum(jnp.where(sel1, before, 0.0), axis=0, keepdims=True)
    rank2 = jnp.sum(jnp.where(sel2, before, 0.0), axis=0, keepdims=True)
    carry[0:N_EXPERTS, :] = carry[0:N_EXPERTS, :] + jnp.sum(onehot, axis=1, keepdims=True)

    rec_t = jnp.zeros((ROUTE_ROWS, n), jnp.float32)
    rec_row = lax.broadcasted_iota(jnp.int32, (ROUTE_ROWS, n), 0)
    for k, val in ((R_ID1, id1), (R_ID2, id2), (R_W1, w1), (R_W2, w2), (R_RANK1, rank1), (R_RANK2, rank2)):
        rec_t = jnp.where(rec_row == k, val, rec_t)
    route_t_ref[:, rows] = rec_t
    padded = jnp.concatenate([rec_t, jnp.zeros((LANES - ROUTE_ROWS, n), jnp.float32)], axis=0)
    route_ref[rows, :] = jnp.transpose(padded)


def _run_mixer(x2, lnin_g, lnin_b, w_in, conv_w, conv_b, pool_w, pool_scale, w_out, ln1_g, ln1_b,
               w_r, b_r, batch, seq):
    n_tok = batch * seq
    n_s = seq // SEQ_TILE
    tok_map = lambda b, s: (b * n_s + s, 0)

    def const(shape):
        return pl.BlockSpec(shape, lambda b, s: (0,) * len(shape), pipeline_mode=pl.Buffered(1))

    in_specs = [
        pl.BlockSpec((SEQ_TILE, D_MODEL), tok_map),
        const((1, D_MODEL)), const((1, D_MODEL)),
        const((D_MODEL, IN_PROJ)),
        const((CONV_K, CONV_WIDTH)), const((1, CONV_WIDTH)),
        const((len(POOL_WINDOWS), POOL_GW, POOL_GW)), const((1, POOL_WIDTH)),
        const((D_MODEL, D_MODEL)),
        const((1, D_MODEL)), const((1, D_MODEL)),
        const((3 * D_MODEL, LANES)), const((1, LANES)),
    ]
    out_specs = [
        pl.BlockSpec((SEQ_TILE, HALF), tok_map),
        pl.BlockSpec((SEQ_TILE, D_MODEL), tok_map),
        pl.BlockSpec((SEQ_TILE, LANES), tok_map),
        pl.BlockSpec((ROUTE_ROWS, SEQ_TILE), lambda b, s: (0, b * n_s + s)),
        pl.BlockSpec((1, LANES), lambda b, s: (0, 0)),
    ]
    out_shape = [
        jax.ShapeDtypeStruct((n_tok, HALF), jnp.uint32),
        jax.ShapeDtypeStruct((n_tok, D_MODEL), jnp.float32),
        jax.ShapeDtypeStruct((n_tok, LANES), jnp.float32),
        jax.ShapeDtypeStruct((ROUTE_ROWS, n_tok), jnp.float32),
        jax.ShapeDtypeStruct((1, LANES), jnp.float32),
    ]
    return pl.pallas_call(
        _mixer_kernel,
        grid=(batch, n_s),
        in_specs=in_specs,
        out_specs=out_specs,
        out_shape=out_shape,
        scratch_shapes=[
            pltpu.VMEM((SEQ_TILE + CONV_HALO, CONV_WIDTH), jnp.float32),
            pltpu.VMEM((SEQ_TILE + POOL_HALO, POOL_WIDTH), jnp.float32),
            pltpu.VMEM((LANES, LANES), jnp.float32),
        ],
        compiler_params=pltpu.CompilerParams(
            dimension_semantics=("arbitrary", "arbitrary"), vmem_limit_bytes=VMEM_LIMIT),
    )(x2, lnin_g, lnin_b, w_in, conv_w, conv_b, pool_w, pool_scale, w_out, ln1_g, ln1_b, w_r, b_r)


def _plan_kernel(rt_ref, counts_ref, pos_ref, tile_start_ref, tiles_ref, pad_ref, *, n_rows):
    lane = lax.broadcasted_iota(jnp.int32, (ROUTE_ROWS, LANES), 1)
    counts = jnp.broadcast_to(counts_ref[...], (ROUTE_ROWS, LANES))
    tiles = jnp.floor((counts + (ROW_TILE - 1)) * (1.0 / ROW_TILE))
    tile_end = tiles
    shift = 1
    while shift < N_EXPERTS:
        tile_end = tile_end + jnp.where(lane >= shift, pltpu.roll(tile_end, shift, axis=1), 0.0)
        shift *= 2
    row_start = (tile_end - tiles) * ROW_TILE
    tile_start_ref[...] = (tile_end - tiles)[0:1, :].astype(jnp.int32)
    tiles_ref[...] = tiles[0:1, :].astype(jnp.int32)

    rt = rt_ref[...]
    ids = rt[R_ID1:R_ID2 + 1, :]
    start = jnp.zeros_like(ids)
    for e in range(N_EXPERTS):
        start = jnp.where(ids == e, row_start[0:1, e:e + 1], start)
    pos_ref[...] = (start + rt[R_RANK1:R_RANK2 + 1, :]).astype(jnp.int32)

    sub = lax.broadcasted_iota(jnp.int32, (N_EXPERTS, LANES), 0)
    lane_e = lax.broadcasted_iota(jnp.int32, (N_EXPERTS, LANES), 1)
    diag = sub == lane_e
    pad_lo = jnp.sum(jnp.where(diag, (row_start + counts)[0:1, :], 0.0), axis=1, keepdims=True)
    pad_n = jnp.sum(jnp.where(diag, (tiles * ROW_TILE - counts)[0:1, :], 0.0), axis=1, keepdims=True)
    j = lax.broadcasted_iota(jnp.int32, (N_EXPERTS, ROW_TILE), 1).astype(jnp.float32)
    period = jnp.maximum(pad_n, 1.0)
    r = j - jnp.floor(j / period) * period
    r = jnp.where(r < 0.0, r + period, r)
    r = jnp.where(r >= period, r - period, r)
    pad_ref[...] = jnp.where(pad_n > 0.0, pad_lo + r, n_rows + j).astype(jnp.int32)


def _run_plan(route_t, counts, n_rows):
    n_tok = route_t.shape[1]
    full = lambda shape: pl.BlockSpec(shape, lambda i: (0, 0))
    return pl.pallas_call(
        functools.partial(_plan_kernel, n_rows=n_rows),
        grid=(1,),
        in_specs=[full((ROUTE_ROWS, n_tok)), full((1, LANES))],
        out_specs=[full((2, n_tok)), full((1, LANES)), full((1, LANES)), full((N_EXPERTS, ROW_TILE))],
        out_shape=[
            jax.ShapeDtypeStruct((2, n_tok), jnp.int32),
            jax.ShapeDtypeStruct((1, LANES), jnp.int32),
            jax.ShapeDtypeStruct((1, LANES), jnp.int32),
            jax.ShapeDtypeStruct((N_EXPERTS, ROW_TILE), jnp.int32),
        ],
        compiler_params=pltpu.CompilerParams(dimension_semantics=("arbitrary",)),
    )(route_t, counts)


def _sc_mesh():
    return plsc.VectorSubcoreMesh(core_axis_name="core", subcore_axis_name="subcore")


def _sc_worker_id():
    return lax.axis_index("core") * (SC_WORKERS // 2) + lax.axis_index("subcore")


def _dispatch_rows(xp, pos1, pos2, pad_pos, zero_rows, n_rows):
    n_tok, width = xp.shape
    n_win = n_tok // SC_WORKERS // SC_WINDOW
    n_pad = pad_pos.shape[0] // SC_WORKERS // SC_WINDOW
    as_windows = lambda v: v.reshape(-1, SC_WINDOW)

    @functools.partial(
        pl.kernel, out_type=jax.ShapeDtypeStruct((n_rows + ROW_TILE, width), xp.dtype), mesh=_sc_mesh(),
        scratch_types=[pltpu.VMEM((n_win, SC_WINDOW), jnp.int32), pltpu.VMEM((n_win, SC_WINDOW), jnp.int32),
                       pltpu.VMEM((n_pad, SC_WINDOW), jnp.int32),
                       pltpu.VMEM((2, SC_WINDOW, width), xp.dtype),
                       pltpu.SemaphoreType.DMA((2,)), pltpu.SemaphoreType.DMA((2,))])
    def dispatch(xp_hbm, pos1_hbm, pos2_hbm, pad_hbm, zero_hbm, out_hbm, idx1, idx2, idxp, buf, lsem, ssem):
        wid = _sc_worker_id()
        pltpu.sync_copy(pos1_hbm.at[pl.ds(wid * n_win, n_win)], idx1)
        pltpu.sync_copy(pos2_hbm.at[pl.ds(wid * n_win, n_win)], idx2)
        pltpu.sync_copy(pad_hbm.at[pl.ds(wid * n_pad, n_pad)], idxp)

        def load(j):
            rows = xp_hbm.at[pl.ds((wid * n_win + j) * SC_WINDOW, SC_WINDOW)]
            return pltpu.make_async_copy(rows, buf.at[j % 2], lsem.at[j % 2])

        def scatters(j):
            return [pltpu.make_async_copy(buf.at[j % 2], out_hbm.at[idx.at[j]], ssem.at[j % 2])
                    for idx in (idx1, idx2)]

        load(0).start()
        for j in range(n_win):
            load(j).wait()
            for cp in scatters(j):
                cp.start()
            if j >= 1:
                for cp in scatters(j - 1):
                    cp.wait()
            if j + 1 < n_win:
                load(j + 1).start()
        for cp in scatters(n_win - 1):
            cp.wait()

        pltpu.sync_copy(zero_hbm, buf.at[0])
        pads = [pltpu.make_async_copy(buf.at[0], out_hbm.at[idxp.at[j]], ssem.at[0]) for j in range(n_pad)]
        for cp in pads:
            cp.start()
        for cp in pads:
            cp.wait()

    return dispatch(xp, as_windows(pos1), as_windows(pos2), as_windows(pad_pos), zero_rows)


def _gather_rows(src, idx):
    n_out, width = idx.shape[0], src.shape[1]
    n_win = n_out // SC_WORKERS // SC_WINDOW

    @functools.partial(
        pl.kernel, out_type=jax.ShapeDtypeStruct((n_out, width), src.dtype), mesh=_sc_mesh(),
        scratch_types=[pltpu.VMEM((n_win, SC_WINDOW), jnp.int32), pltpu.VMEM((2, SC_WINDOW, width), src.dtype),
                       pltpu.SemaphoreType.DMA((2,)), pltpu.SemaphoreType.DMA((2,))])
    def gather(src_hbm, idx_hbm, dst_hbm, idx_v, buf, gsem, ssem):
        wid = _sc_worker_id()
        pltpu.sync_copy(idx_hbm.at[pl.ds(wid * n_win, n_win)], idx_v)

        def fetch(j):
            return pltpu.make_async_copy(src_hbm.at[idx_v.at[j]], buf.at[j % 2], gsem.at[j % 2])

        def store(j):
            rows = dst_hbm.at[pl.ds((wid * n_win + j) * SC_WINDOW, SC_WINDOW)]
            return pltpu.make_async_copy(buf.at[j % 2], rows, ssem.at[j % 2])

        fetch(0).start()
        for j in range(n_win):
            fetch(j).wait()
            store(j).start()
            if j >= 1:
                store(j - 1).wait()
            if j + 1 < n_win:
                fetch(j + 1).start()
        store(n_win - 1).wait()

    return gather(src, idx.reshape(-1, SC_WINDOW))


def _tile_copy(hbm, buf, sem, tile, slot, to_hbm):
    rows = hbm.at[pl.ds(pl.multiple_of(tile * ROW_TILE, ROW_TILE), ROW_TILE)]
    if to_hbm:
        return pltpu.make_async_copy(buf.at[slot], rows, sem.at[slot])
    return pltpu.make_async_copy(rows, buf.at[slot], sem.at[slot])


def _weight_copies(w_hbm, wbuf, wsem, expert):
    slot = expert % WEIGHT_BUFFERS
    return [pltpu.make_async_copy(w.at[expert], buf.at[slot], wsem.at[slot]) for w, buf in zip(w_hbm, wbuf)]


def _expert_kernel(ts_ref, nte_ref, x_hbm, wg_hbm, wu_hbm, wd_hbm, y_hbm,
                   xbuf, ybuf, xsem, ysem, wg_buf, wu_buf, wd_buf, wsem, wgu_bf, wd_bf, act_ref, *, max_tiles):
    e = pl.program_id(0)
    first = ts_ref[e]
    count = nte_ref[e]
    n_tiles = ts_ref[N_EXPERTS - 1] + nte_ref[N_EXPERTS - 1]
    ahead = TILE_BUFFERS - 1
    w_hbm = (wg_hbm, wu_hbm, wd_hbm)
    wbuf = (wg_buf, wu_buf, wd_buf)

    @pl.when(e == 0)
    def _():
        for g in range(ahead):
            @pl.when(g < n_tiles)
            def _():
                _tile_copy(x_hbm, xbuf, xsem, g, g, False).start()
        for k in range(WEIGHT_BUFFERS - 1):
            for cp in _weight_copies(w_hbm, wbuf, wsem, k):
                cp.start()

    @pl.when(e + WEIGHT_BUFFERS - 1 < N_EXPERTS)
    def _():
        for cp in _weight_copies(w_hbm, wbuf, wsem, e + WEIGHT_BUFFERS - 1):
            cp.start()

    for cp in _weight_copies(w_hbm, wbuf, wsem, e):
        cp.wait()

    def sync_x(g):
        _tile_copy(x_hbm, xbuf, xsem, g, g % TILE_BUFFERS, False).wait()

        @pl.when(g + ahead < n_tiles)
        def _():
            _tile_copy(x_hbm, xbuf, xsem, g + ahead, (g + ahead) % TILE_BUFFERS, False).start()

    def sync_y_slot(g):
        @pl.when(g >= TILE_BUFFERS)
        def _():
            _tile_copy(y_hbm, ybuf, ysem, g - TILE_BUFFERS, g % TILE_BUFFERS, True).wait()

    def up_proj(g):
        lo, hi = _unpack_bf16_pairs(xbuf[g % TILE_BUFFERS])
        return (jnp.dot(lo.astype(jnp.bfloat16), wgu_bf[0:HALF, :], preferred_element_type=jnp.float32)
                + jnp.dot(hi.astype(jnp.bfloat16), wgu_bf[HALF:, :], preferred_element_type=jnp.float32))

    def put_act(hgu):
        hg = hgu[:, 0:D_EXPERT]
        act_ref[...] = (hg * jax.nn.sigmoid(hg) * hgu[:, D_EXPERT:]).astype(jnp.bfloat16)

    def down_proj(g):
        y = jnp.dot(act_ref[...], wd_bf[...], preferred_element_type=jnp.float32)
        ybuf[g % TILE_BUFFERS] = _pack_bf16_pairs(y)

    @pl.when(count > 0)
    def _():
        slot = e % WEIGHT_BUFFERS
        wgu_bf[:, 0:D_EXPERT] = wg_buf[slot].astype(jnp.bfloat16)
        wgu_bf[:, D_EXPERT:] = wu_buf[slot].astype(jnp.bfloat16)
        wd_bf[...] = wd_buf[slot].astype(jnp.bfloat16)

        sync_x(first)
        put_act(up_proj(first))

        def tile_body(g, c):
            sync_x(g)
            sync_y_slot(g - 1)
            down_proj(g - 1)
            hgu = up_proj(g)
            put_act(hgu)
            _tile_copy(y_hbm, ybuf, ysem, g - 1, (g - 1) % TILE_BUFFERS, True).start()
            return c

        lax.fori_loop(first + 1, first + count, tile_body, 0)
        last = first + count - 1
        sync_y_slot(last)
        down_proj(last)
        _tile_copy(y_hbm, ybuf, ysem, last, last % TILE_BUFFERS, True).start()

    @pl.when(e == N_EXPERTS - 1)
    def _():
        for k in range(TILE_BUFFERS, 0, -1):
            @pl.when(n_tiles >= k)
            def _():
                _tile_copy(y_hbm, ybuf, ysem, n_tiles - k, (n_tiles - k) % TILE_BUFFERS, True).wait()

        ybuf[0] = jnp.zeros((ROW_TILE, HALF), jnp.uint32)

        def fill(g, c):
            cp = _tile_copy(y_hbm, ybuf, ysem, g, 0, True)
            cp.start()
            cp.wait()
            return c

        lax.fori_loop(n_tiles, max_tiles, fill, 0)


def _run_experts(tile_start, tiles_per_expert, x_sorted, w_gate, w_up, w_down):
    n_rows = x_sorted.shape[0]
    max_tiles = n_rows // ROW_TILE
    hbm = pl.BlockSpec(memory_space=pl.ANY)

    grid_spec = pltpu.PrefetchScalarGridSpec(
        num_scalar_prefetch=2,
        grid=(N_EXPERTS,),
        in_specs=[hbm, hbm, hbm, hbm],
        out_specs=hbm,
        scratch_shapes=[
            pltpu.VMEM((TILE_BUFFERS, ROW_TILE, HALF), jnp.uint32),
            pltpu.VMEM((TILE_BUFFERS, ROW_TILE, HALF), jnp.uint32),
            pltpu.SemaphoreType.DMA((TILE_BUFFERS,)),
            pltpu.SemaphoreType.DMA((TILE_BUFFERS,)),
            pltpu.VMEM((WEIGHT_BUFFERS, D_MODEL, D_EXPERT), jnp.float32),
            pltpu.VMEM((WEIGHT_BUFFERS, D_MODEL, D_EXPERT), jnp.float32),
            pltpu.VMEM((WEIGHT_BUFFERS, D_EXPERT, D_MODEL), jnp.float32),
            pltpu.SemaphoreType.DMA((WEIGHT_BUFFERS,)),
            pltpu.VMEM((D_MODEL, 2 * D_EXPERT), jnp.bfloat16),
            pltpu.VMEM((D_EXPERT, D_MODEL), jnp.bfloat16),
            pltpu.VMEM((ROW_TILE, D_EXPERT), jnp.bfloat16),
        ],
    )
    return pl.pallas_call(
        functools.partial(_expert_kernel, max_tiles=max_tiles),
        grid_spec=grid_spec,
        out_shape=jax.ShapeDtypeStruct((n_rows, HALF), jnp.uint32),
        compiler_params=pltpu.CompilerParams(
            dimension_semantics=("arbitrary",), vmem_limit_bytes=VMEM_LIMIT),
    )(tile_start, tiles_per_expert, x_sorted, w_gate, w_up, w_down)


def _combine_rows(r0, h1_ref, p_ref, route_ref, ya_ref, yb_ref, w_pg, b_pg, w_ple, g_ref, b_ref, o_ref):
    rows = pl.ds(r0, SUB_TILE)
    h1 = h1_ref[rows, :]
    h_hi = h1.astype(jnp.bfloat16)
    p_b = p_ref[rows, :].astype(jnp.bfloat16)
    yield
    gate_pre = jnp.dot(h_hi, w_pg[...], preferred_element_type=jnp.float32)
    ple_pre = jnp.dot(p_b, w_ple[...], preferred_element_type=jnp.float32)
    yield
    ple = ple_pre * jax.nn.sigmoid(gate_pre + b_pg[...])
    route = route_ref[rows, :]
    w1 = route[:, R_W1:R_W1 + 1]
    w2 = route[:, R_W2:R_W2 + 1]
    a_lo, a_hi = _unpack_bf16_pairs(ya_ref[rows, :])
    b_lo, b_hi = _unpack_bf16_pairs(yb_ref[rows, :])
    moe = jnp.concatenate([w1 * a_lo + w2 * b_lo, w1 * a_hi + w2 * b_hi], axis=-1)
    o_ref[rows, :] = _layernorm(DEEPNORM_ALPHA * h1 + ple + moe, g_ref[...], b_ref[...])


def _combine_kernel(h1_ref, p_ref, route_ref, ya_ref, yb_ref, w_pg, b_pg, w_ple, g_ref, b_ref, o_ref):
    chains = [_combine_rows(r0, h1_ref, p_ref, route_ref, ya_ref, yb_ref, w_pg, b_pg, w_ple, g_ref, b_ref, o_ref)
              for r0 in range(0, h1_ref.shape[0], SUB_TILE)]
    for t in range(COMBINE_PHASES + len(chains) - 1):
        for k, chain in reversed(list(enumerate(chains))):
            if 0 <= t - k < COMBINE_PHASES:
                next(chain, None)


def _run_combine(h1, p2, route, y_tok, w_pg, b_pg, w_ple, ln2_g, ln2_b):
    n_tok = h1.shape[0]
    n_t = n_tok // TOKEN_TILE
    tok_map = lambda i: (i, 0)
    const = lambda shape: pl.BlockSpec(shape, lambda i: (0, 0), pipeline_mode=pl.Buffered(1))
    in_specs = [
        pl.BlockSpec((TOKEN_TILE, D_MODEL), tok_map),
        pl.BlockSpec((TOKEN_TILE, PLE_DIM), tok_map),
        pl.BlockSpec((TOKEN_TILE, LANES), tok_map),
        pl.BlockSpec((TOKEN_TILE, HALF), lambda i: (i, 0)),
        pl.BlockSpec((TOKEN_TILE, HALF), lambda i: (i + n_t, 0)),
        const((D_MODEL, D_MODEL)), const((1, D_MODEL)), const((PLE_DIM, D_MODEL)),
        const((1, D_MODEL)), const((1, D_MODEL)),
    ]
    return pl.pallas_call(
        _combine_kernel,
        grid=(n_t,),
        in_specs=in_specs,
        out_specs=pl.BlockSpec((TOKEN_TILE, D_MODEL), tok_map),
        out_shape=jax.ShapeDtypeStruct((n_tok, D_MODEL), jnp.float32),
        compiler_params=pltpu.CompilerParams(dimension_semantics=("arbitrary",), vmem_limit_bytes=VMEM_LIMIT),
    )(h1, p2, route, y_tok, y_tok, w_pg, b_pg, w_ple, ln2_g, ln2_b)


def _split_bf16(w):
    hi = w.astype(jnp.bfloat16)
    lo = (w - hi.astype(jnp.float32)).astype(jnp.bfloat16)
    return hi, lo


def kernel(x, p, ln_in_g, ln_in_b, w_in, conv_w, conv_b, pool_w, pool_scale, w_out, ln1_g, ln1_b,
           w_rg, b_rg, w_re, b_re, w_gate, w_up, w_down, w_pg, b_pg, w_ple, ln2_g, ln2_b):
    batch, seq, _ = x.shape
    n_tok = batch * seq
    bf = jnp.bfloat16
    row = lambda v: v.reshape(1, -1)

    w_r = jnp.concatenate([w_rg[0], jnp.transpose(w_re[0], (1, 0, 2)).reshape(D_MODEL, N_EXPERTS)], axis=1)
    w_r = jnp.pad(w_r, ((0, 0), (0, LANES - w_r.shape[1])))
    w_r_hi, w_r_lo = _split_bf16(w_r)
    w_r_cat = jnp.concatenate([w_r_hi, w_r_hi, w_r_lo], axis=0)
    b_r = jnp.pad(jnp.concatenate([b_rg[0], b_re[0].reshape(-1)]), (0, LANES - N_GROUPS - N_EXPERTS)).reshape(1, LANES)

    x2 = x.reshape(n_tok, D_MODEL)
    p2 = p[0].reshape(n_tok, PLE_DIM)
    mixer_weights = (row(ln_in_g), row(ln_in_b), w_in[0].astype(bf), conv_w[0], row(conv_b[0]),
                     pool_w[0].astype(bf), row(pool_scale[0]), w_out[0].astype(bf), row(ln1_g[0]), row(ln1_b[0]),
                     w_r_cat, b_r)
    combine_weights = (w_pg[0].astype(bf), row(b_pg[0]), w_ple[0].astype(bf), row(ln2_g[0]), row(ln2_b[0]))
    expert_weights = (w_gate[0].reshape(N_EXPERTS, D_MODEL, D_EXPERT),
                      w_up[0].reshape(N_EXPERTS, D_MODEL, D_EXPERT),
                      w_down[0].reshape(N_EXPERTS, D_EXPERT, D_MODEL))
    zero_rows = jnp.zeros((SC_WINDOW, HALF), jnp.uint32)

    n_rows = -(-(2 * n_tok + N_EXPERTS * (ROW_TILE - 1)) // ROW_TILE) * ROW_TILE
    xp, h1, route, route_t, counts = _run_mixer(x2, *mixer_weights, batch, seq)
    pos, tile_start, tiles_per_expert, pad_pos = _run_plan(route_t, counts, n_rows)
    x_sorted = _dispatch_rows(xp, pos[0], pos[1], pad_pos.reshape(-1), zero_rows, n_rows)
    y_sorted = _run_experts(tile_start[0, :N_EXPERTS], tiles_per_expert[0, :N_EXPERTS], x_sorted, *expert_weights)

    y_tok = _gather_rows(y_sorted, pos.reshape(-1))
    out = _run_combine(h1, p2, route, y_tok, *combine_weights)
    return out.reshape(batch, seq, D_MODEL)
```

```python
import functools

import jax
import jax.numpy as jnp
from jax import lax
from jax.experimental import pallas as pl
from jax.experimental.pallas import tpu as pltpu
from jax.experimental.pallas import tpu_sc as plsc

D_MODEL = 1024
CONV_WIDTH = 512
CONV_K = 3
POOL_WIDTH = 512
POOL_WINDOWS = (2, 4, 8, 16)
POOL_GW = 128
IN_PROJ = 3 * CONV_WIDTH + POOL_WIDTH
N_GROUPS = 4
EXPERTS_PER_GROUP = 8
N_EXPERTS = N_GROUPS * EXPERTS_PER_GROUP
D_EXPERT = 256
PLE_DIM = 256
LN_EPS = 1e-5
DEEPNORM_ALPHA = 2.0 ** 0.25

LANES = 128
HALF = D_MODEL // 2
CONV_HALO = 8
POOL_HALO = 16
SEQ_TILE = 1024
SUB_TILE = 256
MIXER_PHASES = 8
COMBINE_PHASES = 3
ROW_TILE = 256
TILE_BUFFERS = 8
SC_WORKERS = 32
SC_WINDOW = 64
WEIGHT_BUFFERS = 3
TOKEN_TILE = 1024
VMEM_LIMIT = 56 * 1024 * 1024

R_ID1, R_ID2, R_RANK1, R_RANK2, R_W1, R_W2 = range(6)
ROUTE_ROWS = 8
LOGIT_ROWS = 40


def _layernorm(x, g, b):
    mu = jnp.mean(x, axis=-1, keepdims=True)
    xc = x - mu
    var = jnp.mean(xc * xc, axis=-1, keepdims=True)
    return xc * lax.rsqrt(var + LN_EPS) * g + b


def _pack_bf16_pairs(v):
    bits = lax.bitcast_convert_type(v.astype(jnp.bfloat16).astype(jnp.float32), jnp.uint32)
    return bits[:, HALF:] | (bits[:, :HALF] >> 16)


def _unpack_bf16_pairs(w):
    lo = lax.bitcast_convert_type(w << 16, jnp.float32)
    hi = lax.bitcast_convert_type(w & jnp.uint32(0xFFFF0000), jnp.float32)
    return lo, hi


def _mixer_kernel(x_ref, lnin_g, lnin_b, w_in_f32, conv_w, conv_b, pool_w, pool_scale, w_out,
                  ln1_g, ln1_b, w_r, b_r,
                  xp_ref, h1_ref, route_ref, route_t_ref, counts_ref,
                  zbuf, vbuf, carry, w_in):
    b = pl.program_id(0)
    s = pl.program_id(1)
    ts = x_ref.shape[0]

    @pl.when(s == 0)
    def _():
        zbuf[0:CONV_HALO, :] = jnp.zeros((CONV_HALO, CONV_WIDTH), jnp.float32)
        vbuf[0:POOL_HALO, :] = jnp.zeros((POOL_HALO, POOL_WIDTH), jnp.float32)

    @pl.when((b == 0) & (s == 0))
    def _():
        carry[...] = jnp.zeros_like(carry)
        w_in[...] = w_in_f32[...].astype(jnp.bfloat16)

    chains = [_mixer_rows(r0, s * ts + r0, x_ref, lnin_g, lnin_b, w_in, conv_w, conv_b, pool_w, pool_scale,
                          w_out, ln1_g, ln1_b, w_r, b_r,
                          xp_ref, h1_ref, route_ref, route_t_ref, zbuf, vbuf, carry)
              for r0 in range(0, ts, SUB_TILE)]
    for t in range(MIXER_PHASES + len(chains) - 1):
        for k, chain in reversed(list(enumerate(chains))):
            if 0 <= t - k < MIXER_PHASES:
                next(chain, None)
    zbuf[0:CONV_HALO, :] = zbuf[ts:ts + CONV_HALO, :]
    vbuf[0:POOL_HALO, :] = vbuf[ts:ts + POOL_HALO, :]
    counts_ref[...] = jnp.transpose(carry[...])[0:1, :]


def _mixer_rows(r0, seq0, x_ref, lnin_g, lnin_b, w_in, conv_w, conv_b, pool_w, pool_scale, w_out,
                ln1_g, ln1_b, w_r, b_r,
                xp_ref, h1_ref, route_ref, route_t_ref, zbuf, vbuf, carry):
    n = SUB_TILE
    rows = pl.ds(r0, n)
    h0 = _layernorm(x_ref[rows, :], lnin_g[...], lnin_b[...])
    h0b = h0.astype(jnp.bfloat16)
    yield
    u_a = jnp.dot(h0b, w_in[:, 0:IN_PROJ // 2], preferred_element_type=jnp.float32)
    yield
    u_b = jnp.dot(h0b, w_in[:, IN_PROJ // 2:], preferred_element_type=jnp.float32)
    yield
    b_g = u_a[:, 0:CONV_WIDTH]
    c_g = u_a[:, CONV_WIDTH:]
    v_c = u_b[:, 0:CONV_WIDTH]
    v_p = u_b[:, CONV_WIDTH:]

    zbuf[pl.ds(CONV_HALO + r0, n), :] = c_g * v_c
    zext = zbuf[pl.ds(r0, n + CONV_HALO), :]
    z1 = pltpu.roll(zext, 1, axis=0)[CONV_HALO:, :]
    z2 = pltpu.roll(zext, 2, axis=0)[CONV_HALO:, :]
    conv = z2 * conv_w[0:1, :] + z1 * conv_w[1:2, :] + zext[CONV_HALO:, :] * conv_w[2:3, :] + conv_b[...]
    y_conv = b_g * conv

    vbuf[pl.ds(POOL_HALO + r0, n), :] = v_p
    vext = vbuf[pl.ds(r0, n + POOL_HALO), :]
    s2 = vext + pltpu.roll(vext, 1, axis=0)
    s4 = s2[:, POOL_GW:] + pltpu.roll(s2[:, POOL_GW:], 2, axis=0)
    s8 = s4[:, POOL_GW:] + pltpu.roll(s4[:, POOL_GW:], 4, axis=0)
    s16 = s8[:, POOL_GW:] + pltpu.roll(s8[:, POOL_GW:], 8, axis=0)
    wsums = (s2[POOL_HALO:, 0:POOL_GW], s4[POOL_HALO:, 0:POOL_GW],
             s8[POOL_HALO:, 0:POOL_GW], s16[POOL_HALO:, 0:POOL_GW])

    t_pos = seq0 + lax.broadcasted_iota(jnp.int32, (n, 1), 0)
    y_pool = []
    for j, w in enumerate(POOL_WINDOWS):
        inv_cnt = 1.0 / jnp.minimum(t_pos + 1, w).astype(jnp.float32)
        pooled = wsums[j] * inv_cnt - v_p[:, j * POOL_GW:(j + 1) * POOL_GW]
        y_pool.append(jnp.dot(pooled.astype(jnp.bfloat16), pool_w[j], preferred_element_type=jnp.float32))
    y_pool = jnp.concatenate(y_pool, axis=-1) * pool_scale[...]

    ycat = jnp.concatenate([y_conv, y_pool], axis=-1).astype(jnp.bfloat16)
    yield
    mix = jnp.dot(ycat, w_out[...], preferred_element_type=jnp.float32)
    yield
    h1 = _layernorm(DEEPNORM_ALPHA * h0 + mix, ln1_g[...], ln1_b[...])

    h_hi = h1.astype(jnp.bfloat16)
    h_lo = (h1 - h_hi.astype(jnp.float32)).astype(jnp.bfloat16)
    xp_ref[rows, :] = _pack_bf16_pairs(h1)
    h1_ref[rows, :] = h1
    hcat = jnp.concatenate([h_hi, h_lo, h_hi], axis=-1)
    yield
    logits = jnp.dot(hcat, w_r[...], preferred_element_type=jnp.float32) + b_r[...]
    yield
    lt = jnp.transpose(logits)[0:LOGIT_ROWS, :]
    rid = lax.broadcasted_iota(jnp.int32, (LOGIT_ROWS, n), 0).astype(jnp.float32)
    neg = jnp.float32(-jnp.inf)

    def first_argmax(vals):
        m = jnp.max(vals, axis=0, keepdims=True)
        idx = jnp.min(jnp.where(vals == m, rid, float(LOGIT_ROWS)), axis=0, keepdims=True)
        return m, idx

    g_mask = rid < N_GROUPS
    g_max, g_idx = first_argmax(jnp.where(g_mask, lt, neg))
    g_w = 1.0 / jnp.sum(jnp.where(g_mask, jnp.exp(lt - g_max), 0.0), axis=0, keepdims=True)

    e_lo = N_GROUPS + EXPERTS_PER_GROUP * g_idx
    e_vals = jnp.where((rid >= e_lo) & (rid < e_lo + EXPERTS_PER_GROUP), lt, neg)
    m1, i1 = first_argmax(e_vals)
    m2, i2 = first_argmax(jnp.where(rid == i1, neg, e_vals))
    e21 = jnp.exp(m2 - m1)
    w1 = g_w / (1.0 + e21)
    w2 = g_w * e21 / (1.0 + e21)
    id1 = i1 - N_GROUPS
    id2 = i2 - N_GROUPS

    eid = lax.broadcasted_iota(jnp.int32, (N_EXPERTS, n), 0).astype(jnp.float32)
    sel1 = eid == id1
    sel2 = eid == id2
    onehot = (sel1 | sel2).astype(jnp.float32)
    src = lax.broadcasted_iota(jnp.int32, (n, n), 0)
    dst = lax.broadcasted_iota(jnp.int32, (n, n), 1)
    earlier = (src < dst).astype(jnp.bfloat16)
    before = (jnp.dot(onehot.astype(jnp.bfloat16), earlier, preferred_element_type=jnp.float32)
              + carry[0:N_EXPERTS, 0:1])
    rank1 = jnp.sum(jnp.where(sel1, before, 0.0), axis=0, keepdims=True)
    rank2 = jnp.sum(jnp.where(sel2, before, 0.0), axis=0, keepdims=True)
    carry[0:N_EXPERTS, :] = carry[0:N_EXPERTS, :] + jnp.sum(onehot, axis=1, keepdims=True)

    rec_t = jnp.zeros((ROUTE_ROWS, n), jnp.float32)
    rec_row = lax.broadcasted_iota(jnp.int32, (ROUTE_ROWS, n), 0)
    for k, val in ((R_ID1, id1), (R_ID2, id2), (R_W1, w1), (R_W2, w2), (R_RANK1, rank1), (R_RANK2, rank2)):
        rec_t = jnp.where(rec_row == k, val, rec_t)
    route_t_ref[:, rows] = rec_t
    padded = jnp.concatenate([rec_t, jnp.zeros((LANES - ROUTE_ROWS, n), jnp.float32)], axis=0)
    route_ref[rows, :] = jnp.transpose(padded)


def _run_mixer(x2, lnin_g, lnin_b, w_in, conv_w, conv_b, pool_w, pool_scale, w_out, ln1_g, ln1_b,
               w_r, b_r, batch, seq):
    n_tok = batch * seq
    n_s = seq // SEQ_TILE
    tok_map = lambda b, s: (b * n_s + s, 0)

    def const(shape):
        return pl.BlockSpec(shape, lambda b, s: (0,) * len(shape), pipeline_mode=pl.Buffered(1))

    in_specs = [
        pl.BlockSpec((SEQ_TILE, D_MODEL), tok_map),
        const((1, D_MODEL)), const((1, D_MODEL)),
        const((D_MODEL, IN_PROJ)),
        const((CONV_K, CONV_WIDTH)), const((1, CONV_WIDTH)),
        const((len(POOL_WINDOWS), POOL_GW, POOL_GW)), const((1, POOL_WIDTH)),
        const((D_MODEL, D_MODEL)),
        const((1, D_MODEL)), const((1, D_MODEL)),
        const((3 * D_MODEL, LANES)), const((1, LANES)),
    ]
    out_specs = [
        pl.BlockSpec((SEQ_TILE, HALF), tok_map),
        pl.BlockSpec((SEQ_TILE, D_MODEL), tok_map),
        pl.BlockSpec((SEQ_TILE, LANES), tok_map),
        pl.BlockSpec((ROUTE_ROWS, SEQ_TILE), lambda b, s: (0, b * n_s + s)),
        pl.BlockSpec((1, LANES), lambda b, s: (0, 0)),
    ]
    out_shape = [
        jax.ShapeDtypeStruct((n_tok, HALF), jnp.uint32),
        jax.ShapeDtypeStruct((n_tok, D_MODEL), jnp.float32),
        jax.ShapeDtypeStruct((n_tok, LANES), jnp.float32),
        jax.ShapeDtypeStruct((ROUTE_ROWS, n_tok), jnp.float32),
        jax.ShapeDtypeStruct((1, LANES), jnp.float32),
    ]
    return pl.pallas_call(
        _mixer_kernel,
        grid=(batch, n_s),
        in_specs=in_specs,
        out_specs=out_specs,
        out_shape=out_shape,
        scratch_shapes=[
            pltpu.VMEM((SEQ_TILE + CONV_HALO, CONV_WIDTH), jnp.float32),
            pltpu.VMEM((SEQ_TILE + POOL_HALO, POOL_WIDTH), jnp.float32),
            pltpu.VMEM((LANES, LANES), jnp.float32),
            pltpu.VMEM((D_MODEL, IN_PROJ), jnp.bfloat16),
        ],
        compiler_params=pltpu.CompilerParams(
            dimension_semantics=("arbitrary", "arbitrary"), vmem_limit_bytes=VMEM_LIMIT),
    )(x2, lnin_g, lnin_b, w_in, conv_w, conv_b, pool_w, pool_scale, w_out, ln1_g, ln1_b, w_r, b_r)


def _plan_kernel(rt_ref, counts_ref, pos_ref, tile_start_ref, tiles_ref, pad_ref, *, n_rows):
    lane = lax.broadcasted_iota(jnp.int32, (ROUTE_ROWS, LANES), 1)
    counts = jnp.broadcast_to(counts_ref[...], (ROUTE_ROWS, LANES))
    tiles = jnp.floor((counts + (ROW_TILE - 1)) * (1.0 / ROW_TILE))
    tile_end = tiles
    shift = 1
    while shift < N_EXPERTS:
        tile_end = tile_end + jnp.where(lane >= shift, pltpu.roll(tile_end, shift, axis=1), 0.0)
        shift *= 2
    row_start = (tile_end - tiles) * ROW_TILE
    tile_start_ref[...] = (tile_end - tiles)[0:1, :].astype(jnp.int32)
    tiles_ref[...] = tiles[0:1, :].astype(jnp.int32)

    rt = rt_ref[...]
    ids = rt[R_ID1:R_ID2 + 1, :]
    start = jnp.zeros_like(ids)
    for e in range(N_EXPERTS):
        start = jnp.where(ids == e, row_start[0:1, e:e + 1], start)
    pos_ref[...] = (start + rt[R_RANK1:R_RANK2 + 1, :]).astype(jnp.int32)

    sub = lax.broadcasted_iota(jnp.int32, (N_EXPERTS, LANES), 0)
    lane_e = lax.broadcasted_iota(jnp.int32, (N_EXPERTS, LANES), 1)
    diag = sub == lane_e
    pad_lo = jnp.sum(jnp.where(diag, (row_start + counts)[0:1, :], 0.0), axis=1, keepdims=True)
    pad_n = jnp.sum(jnp.where(diag, (tiles * ROW_TILE - counts)[0:1, :], 0.0), axis=1, keepdims=True)
    j = lax.broadcasted_iota(jnp.int32, (N_EXPERTS, ROW_TILE), 1).astype(jnp.float32)
    pad_ref[...] = jnp.where(j < pad_n, pad_lo + j, n_rows + j).astype(jnp.int32)


def _run_plan(route_t, counts, n_rows):
    n_tok = route_t.shape[1]
    full = lambda shape: pl.BlockSpec(shape, lambda i: (0, 0))
    return pl.pallas_call(
        functools.partial(_plan_kernel, n_rows=n_rows),
        grid=(1,),
        in_specs=[full((ROUTE_ROWS, n_tok)), full((1, LANES))],
        out_specs=[full((2, n_tok)), full((1, LANES)), full((1, LANES)), full((N_EXPERTS, ROW_TILE))],
        out_shape=[
            jax.ShapeDtypeStruct((2, n_tok), jnp.int32),
            jax.ShapeDtypeStruct((1, LANES), jnp.int32),
            jax.ShapeDtypeStruct((1, LANES), jnp.int32),
            jax.ShapeDtypeStruct((N_EXPERTS, ROW_TILE), jnp.int32),
        ],
        compiler_params=pltpu.CompilerParams(dimension_semantics=("arbitrary",)),
    )(route_t, counts)


def _sc_mesh():
    return plsc.VectorSubcoreMesh(core_axis_name="core", subcore_axis_name="subcore")


def _sc_worker_id():
    return lax.axis_index("core") * (SC_WORKERS // 2) + lax.axis_index("subcore")


def _dispatch_rows(xp, pos1, pos2, pad_pos, zero_rows, n_rows):
    n_tok, width = xp.shape
    n_win = n_tok // SC_WORKERS // SC_WINDOW
    n_pad = pad_pos.shape[0] // SC_WORKERS // SC_WINDOW
    as_windows = lambda v: v.reshape(-1, SC_WINDOW)

    @functools.partial(
        pl.kernel, out_type=jax.ShapeDtypeStruct((n_rows + ROW_TILE, width), xp.dtype), mesh=_sc_mesh(),
        scratch_types=[pltpu.VMEM((n_win, SC_WINDOW), jnp.int32), pltpu.VMEM((n_win, SC_WINDOW), jnp.int32),
                       pltpu.VMEM((n_pad, SC_WINDOW), jnp.int32),
                       pltpu.VMEM((2, SC_WINDOW, width), xp.dtype),
                       pltpu.SemaphoreType.DMA((2,)), pltpu.SemaphoreType.DMA((2,))])
    def dispatch(xp_hbm, pos1_hbm, pos2_hbm, pad_hbm, zero_hbm, out_hbm, idx1, idx2, idxp, buf, lsem, ssem):
        wid = _sc_worker_id()
        pltpu.sync_copy(pos1_hbm.at[pl.ds(wid * n_win, n_win)], idx1)
        pltpu.sync_copy(pos2_hbm.at[pl.ds(wid * n_win, n_win)], idx2)
        pltpu.sync_copy(pad_hbm.at[pl.ds(wid * n_pad, n_pad)], idxp)

        def load(j):
            rows = xp_hbm.at[pl.ds((wid * n_win + j) * SC_WINDOW, SC_WINDOW)]
            return pltpu.make_async_copy(rows, buf.at[j % 2], lsem.at[j % 2])

        def scatters(j):
            return [pltpu.make_async_copy(buf.at[j % 2], out_hbm.at[idx.at[j]], ssem.at[j % 2])
                    for idx in (idx1, idx2)]

        load(0).start()
        for j in range(n_win):
            load(j).wait()
            for cp in scatters(j):
                cp.start()
            if j >= 1:
                for cp in scatters(j - 1):
                    cp.wait()
            if j + 1 < n_win:
                load(j + 1).start()
        for cp in scatters(n_win - 1):
            cp.wait()

        pltpu.sync_copy(zero_hbm, buf.at[0])
        pads = [pltpu.make_async_copy(buf.at[0], out_hbm.at[idxp.at[j]], ssem.at[0]) for j in range(n_pad)]
        for cp in pads:
            cp.start()
        for cp in pads:
            cp.wait()

    return dispatch(xp, as_windows(pos1), as_windows(pos2), as_windows(pad_pos), zero_rows)


def _gather_rows(src, idx):
    n_out, width = idx.shape[0], src.shape[1]
    n_win = n_out // SC_WORKERS // SC_WINDOW

    @functools.partial(
        pl.kernel, out_type=jax.ShapeDtypeStruct((n_out, width), src.dtype), mesh=_sc_mesh(),
        scratch_types=[pltpu.VMEM((n_win, SC_WINDOW), jnp.int32), pltpu.VMEM((2, SC_WINDOW, width), src.dtype),
                       pltpu.SemaphoreType.DMA((2,)), pltpu.SemaphoreType.DMA((2,))])
    def gather(src_hbm, idx_hbm, dst_hbm, idx_v, buf, gsem, ssem):
        wid = _sc_worker_id()
        pltpu.sync_copy(idx_hbm.at[pl.ds(wid * n_win, n_win)], idx_v)

        def fetch(j):
            return pltpu.make_async_copy(src_hbm.at[idx_v.at[j]], buf.at[j % 2], gsem.at[j % 2])

        def store(j):
            rows = dst_hbm.at[pl.ds((wid * n_win + j) * SC_WINDOW, SC_WINDOW)]
            return pltpu.make_async_copy(buf.at[j % 2], rows, ssem.at[j % 2])

        fetch(0).start()
        for j in range(n_win):
            fetch(j).wait()
            store(j).start()
            if j >= 1:
                store(j - 1).wait()
            if j + 1 < n_win:
                fetch(j + 1).start()
        store(n_win - 1).wait()

    return gather(src, idx.reshape(-1, SC_WINDOW))


def _tile_copy(hbm, buf, sem, tile, slot, to_hbm):
    rows = hbm.at[pl.ds(pl.multiple_of(tile * ROW_TILE, ROW_TILE), ROW_TILE)]
    if to_hbm:
        return pltpu.make_async_copy(buf.at[slot], rows, sem.at[slot])
    return pltpu.make_async_copy(rows, buf.at[slot], sem.at[slot])


def _weight_copies(w_hbm, wbuf, wsem, expert):
    slot = expert % WEIGHT_BUFFERS
    return [pltpu.make_async_copy(w.at[expert], buf.at[slot], wsem.at[slot]) for w, buf in zip(w_hbm, wbuf)]


def _expert_kernel(ts_ref, nte_ref, x_hbm, wg_hbm, wu_hbm, wd_hbm, y_hbm,
                   xbuf, ybuf, xsem, ysem, wg_buf, wu_buf, wd_buf, wsem, wgu_bf, wd_bf, act_ref, *, max_tiles):
    e = pl.program_id(0)
    first = ts_ref[e]
    count = nte_ref[e]
    n_tiles = ts_ref[N_EXPERTS - 1] + nte_ref[N_EXPERTS - 1]
    ahead = TILE_BUFFERS - 1
    w_hbm = (wg_hbm, wu_hbm, wd_hbm)
    wbuf = (wg_buf, wu_buf, wd_buf)

    @pl.when(e == 0)
    def _():
        for g in range(ahead):
            @pl.when(g < n_tiles)
            def _():
                _tile_copy(x_hbm, xbuf, xsem, g, g, False).start()
        for k in range(WEIGHT_BUFFERS - 1):
            for cp in _weight_copies(w_hbm, wbuf, wsem, k):
                cp.start()

    @pl.when(e + WEIGHT_BUFFERS - 1 < N_EXPERTS)
    def _():
        for cp in _weight_copies(w_hbm, wbuf, wsem, e + WEIGHT_BUFFERS - 1):
            cp.start()

    for cp in _weight_copies(w_hbm, wbuf, wsem, e):
        cp.wait()

    def sync_x(g):
        _tile_copy(x_hbm, xbuf, xsem, g, g % TILE_BUFFERS, False).wait()

        @pl.when(g + ahead < n_tiles)
        def _():
            _tile_copy(x_hbm, xbuf, xsem, g + ahead, (g + ahead) % TILE_BUFFERS, False).start()

    def sync_y_slot(g):
        @pl.when(g >= TILE_BUFFERS)
        def _():
            _tile_copy(y_hbm, ybuf, ysem, g - TILE_BUFFERS, g % TILE_BUFFERS, True).wait()

    def up_proj(g):
        lo, hi = _unpack_bf16_pairs(xbuf[g % TILE_BUFFERS])
        return (jnp.dot(lo.astype(jnp.bfloat16), wgu_bf[0:HALF, :], preferred_element_type=jnp.float32)
                + jnp.dot(hi.astype(jnp.bfloat16), wgu_bf[HALF:, :], preferred_element_type=jnp.float32))

    def put_act(hgu):
        hg = hgu[:, 0:D_EXPERT]
        act_ref[...] = (hg * jax.nn.sigmoid(hg) * hgu[:, D_EXPERT:]).astype(jnp.bfloat16)

    def down_proj(g):
        y = jnp.dot(act_ref[...], wd_bf[...], preferred_element_type=jnp.float32)
        ybuf[g % TILE_BUFFERS] = _pack_bf16_pairs(y)

    @pl.when(count > 0)
    def _():
        slot = e % WEIGHT_BUFFERS
        wgu_bf[:, 0:D_EXPERT] = wg_buf[slot].astype(jnp.bfloat16)
        wgu_bf[:, D_EXPERT:] = wu_buf[slot].astype(jnp.bfloat16)
        wd_bf[...] = wd_buf[slot].astype(jnp.bfloat16)

        sync_x(first)
        put_act(up_proj(first))

        def tile_body(g, c):
            sync_x(g)
            sync_y_slot(g - 1)
            down_proj(g - 1)
            hgu = up_proj(g)
            put_act(hgu)
            _tile_copy(y_hbm, ybuf, ysem, g - 1, (g - 1) % TILE_BUFFERS, True).start()
            return c

        lax.fori_loop(first + 1, first + count, tile_body, 0)
        last = first + count - 1
        sync_y_slot(last)
        down_proj(last)
        _tile_copy(y_hbm, ybuf, ysem, last, last % TILE_BUFFERS, True).start()

    @pl.when(e == N_EXPERTS - 1)
    def _():
        for k in range(TILE_BUFFERS, 0, -1):
            @pl.when(n_tiles >= k)
            def _():
                _tile_copy(y_hbm, ybuf, ysem, n_tiles - k, (n_tiles - k) % TILE_BUFFERS, True).wait()

        ybuf[0] = jnp.zeros((ROW_TILE, HALF), jnp.uint32)

        def fill(g, c):
            cp = _tile_copy(y_hbm, ybuf, ysem, g, 0, True)
            cp.start()
            cp.wait()
            return c

        lax.fori_loop(n_tiles, max_tiles, fill, 0)


def _run_experts(tile_start, tiles_per_expert, x_sorted, w_gate, w_up, w_down):
    n_rows = x_sorted.shape[0]
    max_tiles = n_rows // ROW_TILE
    hbm = pl.BlockSpec(memory_space=pl.ANY)

    grid_spec = pltpu.PrefetchScalarGridSpec(
        num_scalar_prefetch=2,
        grid=(N_EXPERTS,),
        in_specs=[hbm, hbm, hbm, hbm],
        out_specs=hbm,
        scratch_shapes=[
            pltpu.VMEM((TILE_BUFFERS, ROW_TILE, HALF), jnp.uint32),
            pltpu.VMEM((TILE_BUFFERS, ROW_TILE, HALF), jnp.uint32),
            pltpu.SemaphoreType.DMA((TILE_BUFFERS,)),
            pltpu.SemaphoreType.DMA((TILE_BUFFERS,)),
            pltpu.VMEM((WEIGHT_BUFFERS, D_MODEL, D_EXPERT), jnp.float32),
            pltpu.VMEM((WEIGHT_BUFFERS, D_MODEL, D_EXPERT), jnp.float32),
            pltpu.VMEM((WEIGHT_BUFFERS, D_EXPERT, D_MODEL), jnp.float32),
            pltpu.SemaphoreType.DMA((WEIGHT_BUFFERS,)),
            pltpu.VMEM((D_MODEL, 2 * D_EXPERT), jnp.bfloat16),
            pltpu.VMEM((D_EXPERT, D_MODEL), jnp.bfloat16),
            pltpu.VMEM((ROW_TILE, D_EXPERT), jnp.bfloat16),
        ],
    )
    return pl.pallas_call(
        functools.partial(_expert_kernel, max_tiles=max_tiles),
        grid_spec=grid_spec,
        out_shape=jax.ShapeDtypeStruct((n_rows, HALF), jnp.uint32),
        compiler_params=pltpu.CompilerParams(
            dimension_semantics=("arbitrary",), vmem_limit_bytes=VMEM_LIMIT),
    )(tile_start, tiles_per_expert, x_sorted, w_gate, w_up, w_down)


def _combine_rows(r0, h1_ref, p_ref, route_ref, ya_ref, yb_ref, w_pg, b_pg, w_ple, g_ref, b_ref, o_ref):
    rows = pl.ds(r0, SUB_TILE)
    h1 = h1_ref[rows, :]
    h_hi = h1.astype(jnp.bfloat16)
    p_b = p_ref[rows, :].astype(jnp.bfloat16)
    yield
    gate_pre = jnp.dot(h_hi, w_pg[...], preferred_element_type=jnp.float32)
    ple_pre = jnp.dot(p_b, w_ple[...], preferred_element_type=jnp.float32)
    yield
    ple = ple_pre * jax.nn.sigmoid(gate_pre + b_pg[...])
    route = route_ref[rows, :]
    w1 = route[:, R_W1:R_W1 + 1]
    w2 = route[:, R_W2:R_W2 + 1]
    a_lo, a_hi = _unpack_bf16_pairs(ya_ref[rows, :])
    b_lo, b_hi = _unpack_bf16_pairs(yb_ref[rows, :])
    moe = jnp.concatenate([w1 * a_lo + w2 * b_lo, w1 * a_hi + w2 * b_hi], axis=-1)
    o_ref[rows, :] = _layernorm(DEEPNORM_ALPHA * h1 + ple + moe, g_ref[...], b_ref[...])


def _combine_kernel(h1_ref, p_ref, route_ref, ya_ref, yb_ref, w_pg, b_pg, w_ple, g_ref, b_ref, o_ref):
    chains = [_combine_rows(r0, h1_ref, p_ref, route_ref, ya_ref, yb_ref, w_pg, b_pg, w_ple, g_ref, b_ref, o_ref)
              for r0 in range(0, h1_ref.shape[0], SUB_TILE)]
    for t in range(COMBINE_PHASES + len(chains) - 1):
        for k, chain in reversed(list(enumerate(chains))):
            if 0 <= t - k < COMBINE_PHASES:
                next(chain, None)


def _run_combine(h1, p2, route, y_tok, w_pg, b_pg, w_ple, ln2_g, ln2_b):
    n_tok = h1.shape[0]
    n_t = n_tok // TOKEN_TILE
    tok_map = lambda i: (i, 0)
    const = lambda shape: pl.BlockSpec(shape, lambda i: (0, 0), pipeline_mode=pl.Buffered(1))
    in_specs = [
        pl.BlockSpec((TOKEN_TILE, D_MODEL), tok_map),
        pl.BlockSpec((TOKEN_TILE, PLE_DIM), tok_map),
        pl.BlockSpec((TOKEN_TILE, LANES), tok_map),
        pl.BlockSpec((TOKEN_TILE, HALF), lambda i: (i, 0)),
        pl.BlockSpec((TOKEN_TILE, HALF), lambda i: (i + n_t, 0)),
        const((D_MODEL, D_MODEL)), const((1, D_MODEL)), const((PLE_DIM, D_MODEL)),
        const((1, D_MODEL)), const((1, D_MODEL)),
    ]
    return pl.pallas_call(
        _combine_kernel,
        grid=(n_t,),
        in_specs=in_specs,
        out_specs=pl.BlockSpec((TOKEN_TILE, D_MODEL), tok_map),
        out_shape=jax.ShapeDtypeStruct((n_tok, D_MODEL), jnp.float32),
        compiler_params=pltpu.CompilerParams(dimension_semantics=("arbitrary",), vmem_limit_bytes=VMEM_LIMIT),
    )(h1, p2, route, y_tok, y_tok, w_pg, b_pg, w_ple, ln2_g, ln2_b)


def _split_bf16(w):
    hi = w.astype(jnp.bfloat16)
    lo = (w - hi.astype(jnp.float32)).astype(jnp.bfloat16)
    return hi, lo


def kernel(x, p, ln_in_g, ln_in_b, w_in, conv_w, conv_b, pool_w, pool_scale, w_out, ln1_g, ln1_b,
           w_rg, b_rg, w_re, b_re, w_gate, w_up, w_down, w_pg, b_pg, w_ple, ln2_g, ln2_b):
    batch, seq, _ = x.shape
    n_tok = batch * seq
    bf = jnp.bfloat16
    row = lambda v: v.reshape(1, -1)

    w_r = jnp.concatenate([w_rg[0], jnp.transpose(w_re[0], (1, 0, 2)).reshape(D_MODEL, N_EXPERTS)], axis=1)
    w_r = jnp.pad(w_r, ((0, 0), (0, LANES - w_r.shape[1])))
    w_r_hi, w_r_lo = _split_bf16(w_r)
    w_r_cat = jnp.concatenate([w_r_hi, w_r_hi, w_r_lo], axis=0)
    b_r = jnp.pad(jnp.concatenate([b_rg[0], b_re[0].reshape(-1)]), (0, LANES - N_GROUPS - N_EXPERTS)).reshape(1, LANES)

    x2 = x.reshape(n_tok, D_MODEL)
    p2 = p[0].reshape(n_tok, PLE_DIM)
    mixer_weights = (row(ln_in_g), row(ln_in_b), w_in[0], conv_w[0], row(conv_b[0]),
                     pool_w[0].astype(bf), row(pool_scale[0]), w_out[0].astype(bf), row(ln1_g[0]), row(ln1_b[0]),
                     w_r_cat, b_r)
    combine_weights = (w_pg[0].astype(bf), row(b_pg[0]), w_ple[0].astype(bf), row(ln2_g[0]), row(ln2_b[0]))
    expert_weights = (w_gate[0].reshape(N_EXPERTS, D_MODEL, D_EXPERT),
                      w_up[0].reshape(N_EXPERTS, D_MODEL, D_EXPERT),
                      w_down[0].reshape(N_EXPERTS, D_EXPERT, D_MODEL))
    zero_rows = jnp.zeros((SC_WINDOW, HALF), jnp.uint32)

    n_rows = -(-(2 * n_tok + N_EXPERTS * (ROW_TILE - 1)) // ROW_TILE) * ROW_TILE
    xp, h1, route, route_t, counts = _run_mixer(x2, *mixer_weights, batch, seq)
    pos, tile_start, tiles_per_expert, pad_pos = _run_plan(route_t, counts, n_rows)
    x_sorted = _dispatch_rows(xp, pos[0], pos[1], pad_pos.reshape(-1), zero_rows, n_rows)
    y_sorted = _run_experts(tile_start[0, :N_EXPERTS], tiles_per_expert[0, :N_EXPERTS], x_sorted, *expert_weights)

    y_tok = _gather_rows(y_sorted, pos.reshape(-1))
    out = _run_combine(h1, p2, route, y_tok, *combine_weights)
    return out.reshape(batch, seq, D_MODEL)
```

```python
import functools

import jax
import jax.numpy as jnp
from jax import lax
from jax.experimental import pallas as pl
from jax.experimental.pallas import tpu as pltpu
from jax.experimental.pallas import tpu_sc as plsc

D_MODEL = 1024
CONV_WIDTH = 512
CONV_K = 3
POOL_WIDTH = 512
POOL_WINDOWS = (2, 4, 8, 16)
POOL_GW = 128
IN_PROJ = 3 * CONV_WIDTH + POOL_WIDTH
N_GROUPS = 4
EXPERTS_PER_GROUP = 8
N_EXPERTS = N_GROUPS * EXPERTS_PER_GROUP
D_EXPERT = 256
PLE_DIM = 256
LN_EPS = 1e-5
DEEPNORM_ALPHA = 2.0 ** 0.25

LANES = 128
HALF = D_MODEL // 2
CONV_HALO = 8
POOL_HALO = 16
SEQ_TILE = 1024
SUB_TILE = 256
MIXER_PHASES = 8
COMBINE_PHASES = 3
ROW_TILE = 256
TILE_BUFFERS = 8
SC_WORKERS = 32
SC_WINDOW = 64
WEIGHT_BUFFERS = 3
TOKEN_TILE = 1024
VMEM_LIMIT = 56 * 1024 * 1024

R_ID1, R_ID2, R_RANK1, R_RANK2, R_W1, R_W2 = range(6)
ROUTE_ROWS = 8
LOGIT_ROWS = 40


def _layernorm(x, g, b):
    mu = jnp.mean(x, axis=-1, keepdims=True)
    xc = x - mu
    var = jnp.mean(xc * xc, axis=-1, keepdims=True)
    return xc * lax.rsqrt(var + LN_EPS) * g + b


def _pack_bf16_pairs(v):
    bits = lax.bitcast_convert_type(v.astype(jnp.bfloat16).astype(jnp.float32), jnp.uint32)
    return bits[:, HALF:] | (bits[:, :HALF] >> 16)


def _unpack_bf16_pairs(w):
    lo = lax.bitcast_convert_type(w << 16, jnp.float32)
    hi = lax.bitcast_convert_type(w & jnp.uint32(0xFFFF0000), jnp.float32)
    return lo, hi


def _mixer_kernel(x_ref, lnin_g, lnin_b, w_in_f32, conv_w, conv_b, pool_w, pool_scale, w_out,
                  ln1_g, ln1_b, w_r, b_r,
                  xp_ref, h1_ref, route_ref, route_t_ref, counts_ref,
                  zbuf, vbuf, carry, w_in):
    b = pl.program_id(0)
    s = pl.program_id(1)
    ts = x_ref.shape[0]

    @pl.when(s == 0)
    def _():
        zbuf[0:CONV_HALO, :] = jnp.zeros((CONV_HALO, CONV_WIDTH), jnp.float32)
        vbuf[0:POOL_HALO, :] = jnp.zeros((POOL_HALO, POOL_WIDTH), jnp.float32)

    @pl.when((b == 0) & (s == 0))
    def _():
        carry[...] = jnp.zeros_like(carry)
        w_in[...] = w_in_f32[...].astype(jnp.bfloat16)

    chains = [_mixer_rows(r0, s * ts + r0, x_ref, lnin_g, lnin_b, w_in, conv_w, conv_b, pool_w, pool_scale,
                          w_out, ln1_g, ln1_b, w_r, b_r,
                          xp_ref, h1_ref, route_ref, route_t_ref, zbuf, vbuf, carry)
              for r0 in range(0, ts, SUB_TILE)]
    for t in range(MIXER_PHASES + len(chains) - 1):
        for k, chain in reversed(list(enumerate(chains))):
            if 0 <= t - k < MIXER_PHASES:
                next(chain, None)
    zbuf[0:CONV_HALO, :] = zbuf[ts:ts + CONV_HALO, :]
    vbuf[0:POOL_HALO, :] = vbuf[ts:ts + POOL_HALO, :]
    counts_ref[...] = jnp.transpose(carry[...])[0:1, :]


def _mixer_rows(r0, seq0, x_ref, lnin_g, lnin_b, w_in, conv_w, conv_b, pool_w, pool_scale, w_out,
                ln1_g, ln1_b, w_r, b_r,
                xp_ref, h1_ref, route_ref, route_t_ref, zbuf, vbuf, carry):
    n = SUB_TILE
    rows = pl.ds(r0, n)
    h0 = _layernorm(x_ref[rows, :], lnin_g[...], lnin_b[...])
    h0b = h0.astype(jnp.bfloat16)
    yield
    u_a = jnp.dot(h0b, w_in[:, 0:IN_PROJ // 2], preferred_element_type=jnp.float32)
    yield
    u_b = jnp.dot(h0b, w_in[:, IN_PROJ // 2:], preferred_element_type=jnp.float32)
    yield
    b_g = u_a[:, 0:CONV_WIDTH]
    c_g = u_a[:, CONV_WIDTH:]
    v_c = u_b[:, 0:CONV_WIDTH]
    v_p = u_b[:, CONV_WIDTH:]

    zbuf[pl.ds(CONV_HALO + r0, n), :] = c_g * v_c
    zext = zbuf[pl.ds(r0, n + CONV_HALO), :]
    z1 = pltpu.roll(zext, 1, axis=0)[CONV_HALO:, :]
    z2 = pltpu.roll(zext, 2, axis=0)[CONV_HALO:, :]
    conv = z2 * conv_w[0:1, :] + z1 * conv_w[1:2, :] + zext[CONV_HALO:, :] * conv_w[2:3, :] + conv_b[...]
    y_conv = b_g * conv

    vbuf[pl.ds(POOL_HALO + r0, n), :] = v_p
    vext = vbuf[pl.ds(r0, n + POOL_HALO), :]
    s2 = vext + pltpu.roll(vext, 1, axis=0)
    s4 = s2[:, POOL_GW:] + pltpu.roll(s2[:, POOL_GW:], 2, axis=0)
    s8 = s4[:, POOL_GW:] + pltpu.roll(s4[:, POOL_GW:], 4, axis=0)
    s16 = s8[:, POOL_GW:] + pltpu.roll(s8[:, POOL_GW:], 8, axis=0)
    wsums = (s2[POOL_HALO:, 0:POOL_GW], s4[POOL_HALO:, 0:POOL_GW],
             s8[POOL_HALO:, 0:POOL_GW], s16[POOL_HALO:, 0:POOL_GW])

    t_pos = seq0 + lax.broadcasted_iota(jnp.int32, (n, 1), 0)
    y_pool = []
    for j, w in enumerate(POOL_WINDOWS):
        inv_cnt = 1.0 / jnp.minimum(t_pos + 1, w).astype(jnp.float32)
        pooled = wsums[j] * inv_cnt - v_p[:, j * POOL_GW:(j + 1) * POOL_GW]
        y_pool.append(jnp.dot(pooled.astype(jnp.bfloat16), pool_w[j], preferred_element_type=jnp.float32))
    y_pool = jnp.concatenate(y_pool, axis=-1) * pool_scale[...]

    ycat = jnp.concatenate([y_conv, y_pool], axis=-1).astype(jnp.bfloat16)
    yield
    mix = jnp.dot(ycat, w_out[...], preferred_element_type=jnp.float32)
    yield
    h1 = _layernorm(DEEPNORM_ALPHA * h0 + mix, ln1_g[...], ln1_b[...])

    h_hi = h1.astype(jnp.bfloat16)
    h_lo = (h1 - h_hi.astype(jnp.float32)).astype(jnp.bfloat16)
    xp_ref[rows, :] = _pack_bf16_pairs(h1)
    h1_ref[rows, :] = h1
    hcat = jnp.concatenate([h_hi, h_lo, h_hi], axis=-1)
    yield
    logits = jnp.dot(hcat, w_r[...], preferred_element_type=jnp.float32) + b_r[...]
    yield
    lt = jnp.transpose(logits)[0:LOGIT_ROWS, :]
    rid = lax.broadcasted_iota(jnp.int32, (LOGIT_ROWS, n), 0).astype(jnp.float32)
    neg = jnp.float32(-jnp.inf)

    def first_argmax(vals):
        m = jnp.max(vals, axis=0, keepdims=True)
        idx = jnp.min(jnp.where(vals == m, rid, float(LOGIT_ROWS)), axis=0, keepdims=True)
        return m, idx

    g_mask = rid < N_GROUPS
    g_max, g_idx = first_argmax(jnp.where(g_mask, lt, neg))
    g_w = 1.0 / jnp.sum(jnp.where(g_mask, jnp.exp(lt - g_max), 0.0), axis=0, keepdims=True)

    e_lo = N_GROUPS + EXPERTS_PER_GROUP * g_idx
    e_vals = jnp.where((rid >= e_lo) & (rid < e_lo + EXPERTS_PER_GROUP), lt, neg)
    m1, i1 = first_argmax(e_vals)
    m2, i2 = first_argmax(jnp.where(rid == i1, neg, e_vals))
    e21 = jnp.exp(m2 - m1)
    w1 = g_w / (1.0 + e21)
    w2 = g_w * e21 / (1.0 + e21)
    id1 = i1 - N_GROUPS
    id2 = i2 - N_GROUPS

    eid = lax.broadcasted_iota(jnp.int32, (N_EXPERTS, n), 0).astype(jnp.float32)
    sel1 = eid == id1
    sel2 = eid == id2
    onehot = (sel1 | sel2).astype(jnp.float32)
    src = lax.broadcasted_iota(jnp.int32, (n, n), 0)
    dst = lax.broadcasted_iota(jnp.int32, (n, n), 1)
    earlier = (src < dst).astype(jnp.bfloat16)
    before = (jnp.dot(onehot.astype(jnp.bfloat16), earlier, preferred_element_type=jnp.float32)
              + carry[0:N_EXPERTS, 0:1])
    rank1 = jnp.sum(jnp.where(sel1, before, 0.0), axis=0, keepdims=True)
    rank2 = jnp.sum(jnp.where(sel2, before, 0.0), axis=0, keepdims=True)
    carry[0:N_EXPERTS, :] = carry[0:N_EXPERTS, :] + jnp.sum(onehot, axis=1, keepdims=True)

    rec_t = jnp.zeros((ROUTE_ROWS, n), jnp.float32)
    rec_row = lax.broadcasted_iota(jnp.int32, (ROUTE_ROWS, n), 0)
    for k, val in ((R_ID1, id1), (R_ID2, id2), (R_W1, w1), (R_W2, w2), (R_RANK1, rank1), (R_RANK2, rank2)):
        rec_t = jnp.where(rec_row == k, val, rec_t)
    route_t_ref[:, rows] = rec_t
    padded = jnp.concatenate([rec_t, jnp.zeros((LANES - ROUTE_ROWS, n), jnp.float32)], axis=0)
    route_ref[rows, :] = jnp.transpose(padded)


def _run_mixer(x2, lnin_g, lnin_b, w_in, conv_w, conv_b, pool_w, pool_scale, w_out, ln1_g, ln1_b,
               w_r, b_r, batch, seq):
    n_tok = batch * seq
    n_s = seq // SEQ_TILE
    tok_map = lambda b, s: (b * n_s + s, 0)

    def const(shape):
        return pl.BlockSpec(shape, lambda b, s: (0,) * len(shape), pipeline_mode=pl.Buffered(1))

    in_specs = [
        pl.BlockSpec((SEQ_TILE, D_MODEL), tok_map),
        const((1, D_MODEL)), const((1, D_MODEL)),
        const((D_MODEL, IN_PROJ)),
        const((CONV_K, CONV_WIDTH)), const((1, CONV_WIDTH)),
        const((len(POOL_WINDOWS), POOL_GW, POOL_GW)), const((1, POOL_WIDTH)),
        const((D_MODEL, D_MODEL)),
        const((1, D_MODEL)), const((1, D_MODEL)),
        const((3 * D_MODEL, LANES)), const((1, LANES)),
    ]
    out_specs = [
        pl.BlockSpec((SEQ_TILE, HALF), tok_map),
        pl.BlockSpec((SEQ_TILE, D_MODEL), tok_map),
        pl.BlockSpec((SEQ_TILE, LANES), tok_map),
        pl.BlockSpec((ROUTE_ROWS, SEQ_TILE), lambda b, s: (0, b * n_s + s)),
        pl.BlockSpec((1, LANES), lambda b, s: (0, 0)),
    ]
    out_shape = [
        jax.ShapeDtypeStruct((n_tok, HALF), jnp.uint32),
        jax.ShapeDtypeStruct((n_tok, D_MODEL), jnp.float32),
        jax.ShapeDtypeStruct((n_tok, LANES), jnp.float32),
        jax.ShapeDtypeStruct((ROUTE_ROWS, n_tok), jnp.float32),
        jax.ShapeDtypeStruct((1, LANES), jnp.float32),
    ]
    return pl.pallas_call(
        _mixer_kernel,
        grid=(batch, n_s),
        in_specs=in_specs,
        out_specs=out_specs,
        out_shape=out_shape,
        scratch_shapes=[
            pltpu.VMEM((SEQ_TILE + CONV_HALO, CONV_WIDTH), jnp.float32),
            pltpu.VMEM((SEQ_TILE + POOL_HALO, POOL_WIDTH), jnp.float32),
            pltpu.VMEM((LANES, LANES), jnp.float32),
            pltpu.VMEM((D_MODEL, IN_PROJ), jnp.bfloat16),
        ],
        compiler_params=pltpu.CompilerParams(
            dimension_semantics=("arbitrary", "arbitrary"), vmem_limit_bytes=VMEM_LIMIT),
    )(x2, lnin_g, lnin_b, w_in, conv_w, conv_b, pool_w, pool_scale, w_out, ln1_g, ln1_b, w_r, b_r)


def _plan_kernel(rt_ref, counts_ref, pos_ref, tile_start_ref, tiles_ref, pad_ref, *, n_rows):
    lane = lax.broadcasted_iota(jnp.int32, (ROUTE_ROWS, LANES), 1)
    counts = jnp.broadcast_to(counts_ref[...], (ROUTE_ROWS, LANES))
    tiles = jnp.floor((counts + (ROW_TILE - 1)) * (1.0 / ROW_TILE))
    tile_end = tiles
    shift = 1
    while shift < N_EXPERTS:
        tile_end = tile_end + jnp.where(lane >= shift, pltpu.roll(tile_end, shift, axis=1), 0.0)
        shift *= 2
    row_start = (tile_end - tiles) * ROW_TILE
    tile_start_ref[...] = (tile_end - tiles)[0:1, :].astype(jnp.int32)
    tiles_ref[...] = tiles[0:1, :].astype(jnp.int32)

    rt = rt_ref[...]
    ids = rt[R_ID1:R_ID2 + 1, :]
    start = jnp.zeros_like(ids)
    for e in range(N_EXPERTS):
        start = jnp.where(ids == e, row_start[0:1, e:e + 1], start)
    pos_ref[...] = (start + rt[R_RANK1:R_RANK2 + 1, :]).astype(jnp.int32)

    sub = lax.broadcasted_iota(jnp.int32, (N_EXPERTS, LANES), 0)
    lane_e = lax.broadcasted_iota(jnp.int32, (N_EXPERTS, LANES), 1)
    diag = sub == lane_e
    pad_lo = jnp.sum(jnp.where(diag, (row_start + counts)[0:1, :], 0.0), axis=1, keepdims=True)
    pad_n = jnp.sum(jnp.where(diag, (tiles * ROW_TILE - counts)[0:1, :], 0.0), axis=1, keepdims=True)
    j = lax.broadcasted_iota(jnp.int32, (N_EXPERTS, ROW_TILE), 1).astype(jnp.float32)
    pad_ref[...] = jnp.where(j < pad_n, pad_lo + j, n_rows + j).astype(jnp.int32)


def _run_plan(route_t, counts, n_rows):
    n_tok = route_t.shape[1]
    full = lambda shape: pl.BlockSpec(shape, lambda i: (0, 0))
    return pl.pallas_call(
        functools.partial(_plan_kernel, n_rows=n_rows),
        grid=(1,),
        in_specs=[full((ROUTE_ROWS, n_tok)), full((1, LANES))],
        out_specs=[full((2, n_tok)), full((1, LANES)), full((1, LANES)), full((N_EXPERTS, ROW_TILE))],
        out_shape=[
            jax.ShapeDtypeStruct((2, n_tok), jnp.int32),
            jax.ShapeDtypeStruct((1, LANES), jnp.int32),
            jax.ShapeDtypeStruct((1, LANES), jnp.int32),
            jax.ShapeDtypeStruct((N_EXPERTS, ROW_TILE), jnp.int32),
        ],
        compiler_params=pltpu.CompilerParams(dimension_semantics=("arbitrary",)),
    )(route_t, counts)


def _sc_mesh():
    return plsc.VectorSubcoreMesh(core_axis_name="core", subcore_axis_name="subcore")


def _sc_worker_id():
    return lax.axis_index("core") * (SC_WORKERS // 2) + lax.axis_index("subcore")


def _dispatch_rows(xp, pos1, pos2, pad_pos, zero_rows, n_rows):
    n_tok, width = xp.shape
    n_win = n_tok // SC_WORKERS // SC_WINDOW
    n_pad = pad_pos.shape[0] // SC_WORKERS // SC_WINDOW
    as_windows = lambda v: v.reshape(-1, SC_WINDOW)

    @functools.partial(
        pl.kernel, out_type=jax.ShapeDtypeStruct((n_rows + ROW_TILE, width), xp.dtype), mesh=_sc_mesh(),
        scratch_types=[pltpu.VMEM((n_win, SC_WINDOW), jnp.int32), pltpu.VMEM((n_win, SC_WINDOW), jnp.int32),
                       pltpu.VMEM((n_pad, SC_WINDOW), jnp.int32),
                       pltpu.VMEM((2, SC_WINDOW, width), xp.dtype), pltpu.VMEM((SC_WINDOW, width), xp.dtype),
                       pltpu.SemaphoreType.DMA((2,)), pltpu.SemaphoreType.DMA((2,)), pltpu.SemaphoreType.DMA])
    def dispatch(xp_hbm, pos1_hbm, pos2_hbm, pad_hbm, zero_hbm, out_hbm,
                 idx1, idx2, idxp, buf, zbuf, lsem, ssem, psem):
        wid = _sc_worker_id()
        pltpu.sync_copy(pad_hbm.at[pl.ds(wid * n_pad, n_pad)], idxp)
        pltpu.sync_copy(zero_hbm, zbuf)
        pads = [pltpu.make_async_copy(zbuf, out_hbm.at[idxp.at[j]], psem) for j in range(n_pad)]
        for cp in pads:
            cp.start()
        pltpu.sync_copy(pos1_hbm.at[pl.ds(wid * n_win, n_win)], idx1)
        pltpu.sync_copy(pos2_hbm.at[pl.ds(wid * n_win, n_win)], idx2)

        def load(j):
            rows = xp_hbm.at[pl.ds((wid * n_win + j) * SC_WINDOW, SC_WINDOW)]
            return pltpu.make_async_copy(rows, buf.at[j % 2], lsem.at[j % 2])

        def scatters(j):
            return [pltpu.make_async_copy(buf.at[j % 2], out_hbm.at[idx.at[j]], ssem.at[j % 2])
                    for idx in (idx1, idx2)]

        load(0).start()
        for j in range(n_win):
            load(j).wait()
            for cp in scatters(j):
                cp.start()
            if j >= 1:
                for cp in scatters(j - 1):
                    cp.wait()
            if j + 1 < n_win:
                load(j + 1).start()
        for cp in scatters(n_win - 1):
            cp.wait()
        for cp in pads:
            cp.wait()

    return dispatch(xp, as_windows(pos1), as_windows(pos2), as_windows(pad_pos), zero_rows)


def _gather_rows(src, idx):
    n_out, width = idx.shape[0], src.shape[1]
    n_win = n_out // SC_WORKERS // SC_WINDOW

    @functools.partial(
        pl.kernel, out_type=jax.ShapeDtypeStruct((n_out, width), src.dtype), mesh=_sc_mesh(),
        scratch_types=[pltpu.VMEM((n_win, SC_WINDOW), jnp.int32), pltpu.VMEM((2, SC_WINDOW, width), src.dtype),
                       pltpu.SemaphoreType.DMA((2,)), pltpu.SemaphoreType.DMA((2,))])
    def gather(src_hbm, idx_hbm, dst_hbm, idx_v, buf, gsem, ssem):
        wid = _sc_worker_id()
        pltpu.sync_copy(idx_hbm.at[pl.ds(wid * n_win, n_win)], idx_v)

        def fetch(j):
            return pltpu.make_async_copy(src_hbm.at[idx_v.at[j]], buf.at[j % 2], gsem.at[j % 2])

        def store(j):
            rows = dst_hbm.at[pl.ds((wid * n_win + j) * SC_WINDOW, SC_WINDOW)]
            return pltpu.make_async_copy(buf.at[j % 2], rows, ssem.at[j % 2])

        fetch(0).start()
        for j in range(n_win):
            fetch(j).wait()
            store(j).start()
            if j >= 1:
                store(j - 1).wait()
            if j + 1 < n_win:
                fetch(j + 1).start()
        store(n_win - 1).wait()

    return gather(src, idx.reshape(-1, SC_WINDOW))


def _tile_copy(hbm, buf, sem, tile, slot, to_hbm):
    rows = hbm.at[pl.ds(pl.multiple_of(tile * ROW_TILE, ROW_TILE), ROW_TILE)]
    if to_hbm:
        return pltpu.make_async_copy(buf.at[slot], rows, sem.at[slot])
    return pltpu.make_async_copy(rows, buf.at[slot], sem.at[slot])


def _weight_copies(w_hbm, wbuf, wsem, expert):
    slot = expert % WEIGHT_BUFFERS
    return [pltpu.make_async_copy(w.at[expert], buf.at[slot], wsem.at[slot]) for w, buf in zip(w_hbm, wbuf)]


def _expert_kernel(ts_ref, nte_ref, x_hbm, wg_hbm, wu_hbm, wd_hbm, y_hbm,
                   xbuf, ybuf, xsem, ysem, wg_buf, wu_buf, wd_buf, wsem, wgu_bf, wd_bf, act_ref, *, max_tiles):
    e = pl.program_id(0)
    first = ts_ref[e]
    count = nte_ref[e]
    n_tiles = ts_ref[N_EXPERTS - 1] + nte_ref[N_EXPERTS - 1]
    ahead = TILE_BUFFERS - 1
    w_hbm = (wg_hbm, wu_hbm, wd_hbm)
    wbuf = (wg_buf, wu_buf, wd_buf)

    @pl.when(e == 0)
    def _():
        for g in range(ahead):
            @pl.when(g < n_tiles)
            def _():
                _tile_copy(x_hbm, xbuf, xsem, g, g, False).start()
        for k in range(WEIGHT_BUFFERS - 1):
            for cp in _weight_copies(w_hbm, wbuf, wsem, k):
                cp.start()

    @pl.when(e + WEIGHT_BUFFERS - 1 < N_EXPERTS)
    def _():
        for cp in _weight_copies(w_hbm, wbuf, wsem, e + WEIGHT_BUFFERS - 1):
            cp.start()

    for cp in _weight_copies(w_hbm, wbuf, wsem, e):
        cp.wait()

    def sync_x(g):
        _tile_copy(x_hbm, xbuf, xsem, g, g % TILE_BUFFERS, False).wait()

        @pl.when(g + ahead < n_tiles)
        def _():
            _tile_copy(x_hbm, xbuf, xsem, g + ahead, (g + ahead) % TILE_BUFFERS, False).start()

    def sync_y_slot(g):
        @pl.when(g >= TILE_BUFFERS)
        def _():
            _tile_copy(y_hbm, ybuf, ysem, g - TILE_BUFFERS, g % TILE_BUFFERS, True).wait()

    def up_proj(g):
        lo, hi = _unpack_bf16_pairs(xbuf[g % TILE_BUFFERS])
        return (jnp.dot(lo.astype(jnp.bfloat16), wgu_bf[0:HALF, :], preferred_element_type=jnp.float32)
                + jnp.dot(hi.astype(jnp.bfloat16), wgu_bf[HALF:, :], preferred_element_type=jnp.float32))

    def put_act(hgu):
        hg = hgu[:, 0:D_EXPERT]
        act_ref[...] = (hg * jax.nn.sigmoid(hg) * hgu[:, D_EXPERT:]).astype(jnp.bfloat16)

    def down_proj(g):
        y = jnp.dot(act_ref[...], wd_bf[...], preferred_element_type=jnp.float32)
        ybuf[g % TILE_BUFFERS] = _pack_bf16_pairs(y)

    @pl.when(count > 0)
    def _():
        slot = e % WEIGHT_BUFFERS
        wgu_bf[:, 0:D_EXPERT] = wg_buf[slot].astype(jnp.bfloat16)
        wgu_bf[:, D_EXPERT:] = wu_buf[slot].astype(jnp.bfloat16)
        wd_bf[...] = wd_buf[slot].astype(jnp.bfloat16)

        sync_x(first)
        put_act(up_proj(first))

        def tile_body(g, c):
            sync_x(g)
            sync_y_slot(g - 1)
            down_proj(g - 1)
            hgu = up_proj(g)
            put_act(hgu)
            _tile_copy(y_hbm, ybuf, ysem, g - 1, (g - 1) % TILE_BUFFERS, True).start()
            return c

        lax.fori_loop(first + 1, first + count, tile_body, 0)
        last = first + count - 1
        sync_y_slot(last)
        down_proj(last)
        _tile_copy(y_hbm, ybuf, ysem, last, last % TILE_BUFFERS, True).start()

    @pl.when(e == N_EXPERTS - 1)
    def _():
        for k in range(TILE_BUFFERS, 0, -1):
            @pl.when(n_tiles >= k)
            def _():
                _tile_copy(y_hbm, ybuf, ysem, n_tiles - k, (n_tiles - k) % TILE_BUFFERS, True).wait()

        ybuf[0] = jnp.zeros((ROW_TILE, HALF), jnp.uint32)

        def fill(g, c):
            cp = _tile_copy(y_hbm, ybuf, ysem, g, 0, True)
            cp.start()
            cp.wait()
            return c

        lax.fori_loop(n_tiles, max_tiles, fill, 0)


def _run_experts(tile_start, tiles_per_expert, x_sorted, w_gate, w_up, w_down):
    n_rows = x_sorted.shape[0]
    max_tiles = n_rows // ROW_TILE
    hbm = pl.BlockSpec(memory_space=pl.ANY)

    grid_spec = pltpu.PrefetchScalarGridSpec(
        num_scalar_prefetch=2,
        grid=(N_EXPERTS,),
        in_specs=[hbm, hbm, hbm, hbm],
        out_specs=hbm,
        scratch_shapes=[
            pltpu.VMEM((TILE_BUFFERS, ROW_TILE, HALF), jnp.uint32),
            pltpu.VMEM((TILE_BUFFERS, ROW_TILE, HALF), jnp.uint32),
            pltpu.SemaphoreType.DMA((TILE_BUFFERS,)),
            pltpu.SemaphoreType.DMA((TILE_BUFFERS,)),
            pltpu.VMEM((WEIGHT_BUFFERS, D_MODEL, D_EXPERT), jnp.float32),
            pltpu.VMEM((WEIGHT_BUFFERS, D_MODEL, D_EXPERT), jnp.float32),
            pltpu.VMEM((WEIGHT_BUFFERS, D_EXPERT, D_MODEL), jnp.float32),
            pltpu.SemaphoreType.DMA((WEIGHT_BUFFERS,)),
            pltpu.VMEM((D_MODEL, 2 * D_EXPERT), jnp.bfloat16),
            pltpu.VMEM((D_EXPERT, D_MODEL), jnp.bfloat16),
            pltpu.VMEM((ROW_TILE, D_EXPERT), jnp.bfloat16),
        ],
    )
    return pl.pallas_call(
        functools.partial(_expert_kernel, max_tiles=max_tiles),
        grid_spec=grid_spec,
        out_shape=jax.ShapeDtypeStruct((n_rows, HALF), jnp.uint32),
        compiler_params=pltpu.CompilerParams(
            dimension_semantics=("arbitrary",), vmem_limit_bytes=VMEM_LIMIT),
    )(tile_start, tiles_per_expert, x_sorted, w_gate, w_up, w_down)


def _combine_rows(r0, h1_ref, p_ref, route_ref, ya_ref, yb_ref, w_pg, b_pg, w_ple, g_ref, b_ref, o_ref):
    rows = pl.ds(r0, SUB_TILE)
    h1 = h1_ref[rows, :]
    h_hi = h1.astype(jnp.bfloat16)
    p_b = p_ref[rows, :].astype(jnp.bfloat16)
    yield
    gate_pre = jnp.dot(h_hi, w_pg[...], preferred_element_type=jnp.float32)
    ple_pre = jnp.dot(p_b, w_ple[...], preferred_element_type=jnp.float32)
    yield
    ple = ple_pre * jax.nn.sigmoid(gate_pre + b_pg[...])
    route = route_ref[rows, :]
    w1 = route[:, R_W1:R_W1 + 1]
    w2 = route[:, R_W2:R_W2 + 1]
    a_lo, a_hi = _unpack_bf16_pairs(ya_ref[rows, :])
    b_lo, b_hi = _unpack_bf16_pairs(yb_ref[rows, :])
    moe = jnp.concatenate([w1 * a_lo + w2 * b_lo, w1 * a_hi + w2 * b_hi], axis=-1)
    o_ref[rows, :] = _layernorm(DEEPNORM_ALPHA * h1 + ple + moe, g_ref[...], b_ref[...])


def _combine_kernel(h1_ref, p_ref, route_ref, ya_ref, yb_ref, w_pg, b_pg, w_ple, g_ref, b_ref, o_ref):
    chains = [_combine_rows(r0, h1_ref, p_ref, route_ref, ya_ref, yb_ref, w_pg, b_pg, w_ple, g_ref, b_ref, o_ref)
              for r0 in range(0, h1_ref.shape[0], SUB_TILE)]
    for t in range(COMBINE_PHASES + len(chains) - 1):
        for k, chain in reversed(list(enumerate(chains))):
            if 0 <= t - k < COMBINE_PHASES:
                next(chain, None)


def _run_combine(h1, p2, route, y_tok, w_pg, b_pg, w_ple, ln2_g, ln2_b):
    n_tok = h1.shape[0]
    n_t = n_tok // TOKEN_TILE
    tok_map = lambda i: (i, 0)
    const = lambda shape: pl.BlockSpec(shape, lambda i: (0, 0), pipeline_mode=pl.Buffered(1))
    in_specs = [
        pl.BlockSpec((TOKEN_TILE, D_MODEL), tok_map),
        pl.BlockSpec((TOKEN_TILE, PLE_DIM), tok_map),
        pl.BlockSpec((TOKEN_TILE, LANES), tok_map),
        pl.BlockSpec((TOKEN_TILE, HALF), lambda i: (i, 0)),
        pl.BlockSpec((TOKEN_TILE, HALF), lambda i: (i + n_t, 0)),
        const((D_MODEL, D_MODEL)), const((1, D_MODEL)), const((PLE_DIM, D_MODEL)),
        const((1, D_MODEL)), const((1, D_MODEL)),
    ]
    return pl.pallas_call(
        _combine_kernel,
        grid=(n_t,),
        in_specs=in_specs,
        out_specs=pl.BlockSpec((TOKEN_TILE, D_MODEL), tok_map),
        out_shape=jax.ShapeDtypeStruct((n_tok, D_MODEL), jnp.float32),
        compiler_params=pltpu.CompilerParams(dimension_semantics=("arbitrary",), vmem_limit_bytes=VMEM_LIMIT),
    )(h1, p2, route, y_tok, y_tok, w_pg, b_pg, w_ple, ln2_g, ln2_b)


def _split_bf16(w):
    hi = w.astype(jnp.bfloat16)
    lo = (w - hi.astype(jnp.float32)).astype(jnp.bfloat16)
    return hi, lo


def kernel(x, p, ln_in_g, ln_in_b, w_in, conv_w, conv_b, pool_w, pool_scale, w_out, ln1_g, ln1_b,
           w_rg, b_rg, w_re, b_re, w_gate, w_up, w_down, w_pg, b_pg, w_ple, ln2_g, ln2_b):
    batch, seq, _ = x.shape
    n_tok = batch * seq
    bf = jnp.bfloat16
    row = lambda v: v.reshape(1, -1)

    w_r = jnp.concatenate([w_rg[0], jnp.transpose(w_re[0], (1, 0, 2)).reshape(D_MODEL, N_EXPERTS)], axis=1)
    w_r = jnp.pad(w_r, ((0, 0), (0, LANES - w_r.shape[1])))
    w_r_hi, w_r_lo = _split_bf16(w_r)
    w_r_cat = jnp.concatenate([w_r_hi, w_r_hi, w_r_lo], axis=0)
    b_r = jnp.pad(jnp.concatenate([b_rg[0], b_re[0].reshape(-1)]), (0, LANES - N_GROUPS - N_EXPERTS)).reshape(1, LANES)

    x2 = x.reshape(n_tok, D_MODEL)
    p2 = p[0].reshape(n_tok, PLE_DIM)
    mixer_weights = (row(ln_in_g), row(ln_in_b), w_in[0], conv_w[0], row(conv_b[0]),
                     pool_w[0].astype(bf), row(pool_scale[0]), w_out[0].astype(bf), row(ln1_g[0]), row(ln1_b[0]),
                     w_r_cat, b_r)
    combine_weights = (w_pg[0].astype(bf), row(b_pg[0]), w_ple[0].astype(bf), row(ln2_g[0]), row(ln2_b[0]))
    expert_weights = (w_gate[0].reshape(N_EXPERTS, D_MODEL, D_EXPERT),
                      w_up[0].reshape(N_EXPERTS, D_MODEL, D_EXPERT),
                      w_down[0].reshape(N_EXPERTS, D_EXPERT, D_MODEL))
    zero_rows = jnp.zeros((SC_WINDOW, HALF), jnp.uint32)

    n_rows = -(-(2 * n_tok + N_EXPERTS * (ROW_TILE - 1)) // ROW_TILE) * ROW_TILE
    xp, h1, route, route_t, counts = _run_mixer(x2, *mixer_weights, batch, seq)
    pos, tile_start, tiles_per_expert, pad_pos = _run_plan(route_t, counts, n_rows)
    x_sorted = _dispatch_rows(xp, pos[0], pos[1], pad_pos.reshape(-1), zero_rows, n_rows)
    y_sorted = _run_experts(tile_start[0, :N_EXPERTS], tiles_per_expert[0, :N_EXPERTS], x_sorted, *expert_weights)

    y_tok = _gather_rows(y_sorted, pos.reshape(-1))
    out = _run_combine(h1, p2, route, y_tok, *combine_weights)
    return out.reshape(batch, seq, D_MODEL)
```

```python
import functools

import jax
import jax.numpy as jnp
from jax import lax
from jax.experimental import pallas as pl
from jax.experimental.pallas import tpu as pltpu
from jax.experimental.pallas import tpu_sc as plsc

D_MODEL = 1024
CONV_WIDTH = 512
CONV_K = 3
POOL_WIDTH = 512
POOL_WINDOWS = (2, 4, 8, 16)
POOL_GW = 128
IN_PROJ = 3 * CONV_WIDTH + POOL_WIDTH
N_GROUPS = 4
EXPERTS_PER_GROUP = 8
N_EXPERTS = N_GROUPS * EXPERTS_PER_GROUP
D_EXPERT = 256
PLE_DIM = 256
LN_EPS = 1e-5
DEEPNORM_ALPHA = 2.0 ** 0.25

LANES = 128
HALF = D_MODEL // 2
CONV_HALO = 8
POOL_HALO = 16
SEQ_TILE = 1024
SUB_TILE = 256
MIXER_PHASES = 8
COMBINE_PHASES = 3
ROW_TILE = 256
TILE_BUFFERS = 8
SC_WORKERS = 32
SC_WINDOW = 64
WEIGHT_BUFFERS = 3
TOKEN_TILE = 1024
VMEM_LIMIT = 56 * 1024 * 1024

R_ID1, R_ID2, R_RANK1, R_RANK2, R_W1, R_W2 = range(6)
ROUTE_ROWS = 8
LOGIT_ROWS = 40


def _layernorm(x, g, b):
    mu = jnp.mean(x, axis=-1, keepdims=True)
    xc = x - mu
    var = jnp.mean(xc * xc, axis=-1, keepdims=True)
    return xc * lax.rsqrt(var + LN_EPS) * g + b


def _pack_bf16_pairs(v):
    bits = lax.bitcast_convert_type(v.astype(jnp.bfloat16).astype(jnp.float32), jnp.uint32)
    return bits[:, HALF:] | (bits[:, :HALF] >> 16)


def _unpack_bf16_pairs(w):
    lo = lax.bitcast_convert_type(w << 16, jnp.float32)
    hi = lax.bitcast_convert_type(w & jnp.uint32(0xFFFF0000), jnp.float32)
    return lo, hi


def _mixer_kernel(x_ref, lnin_g, lnin_b, w_in_f32, conv_w, conv_b, pool_w, pool_scale, w_out,
                  ln1_g, ln1_b, w_r, b_r,
                  xp_ref, h1_ref, route_ref, route_t_ref, counts_ref,
                  zbuf, vbuf, carry, w_in):
    b = pl.program_id(0)
    s = pl.program_id(1)
    ts = x_ref.shape[0]

    @pl.when(s == 0)
    def _():
        zbuf[0:CONV_HALO, :] = jnp.zeros((CONV_HALO, CONV_WIDTH), jnp.float32)
        vbuf[0:POOL_HALO, :] = jnp.zeros((POOL_HALO, POOL_WIDTH), jnp.float32)

    @pl.when((b == 0) & (s == 0))
    def _():
        carry[...] = jnp.zeros_like(carry)
        w_in[...] = w_in_f32[...].astype(jnp.bfloat16)

    chains = [_mixer_rows(r0, s * ts + r0, x_ref, lnin_g, lnin_b, w_in, conv_w, conv_b, pool_w, pool_scale,
                          w_out, ln1_g, ln1_b, w_r, b_r,
                          xp_ref, h1_ref, route_ref, route_t_ref, zbuf, vbuf, carry)
              for r0 in range(0, ts, SUB_TILE)]
    for t in range(MIXER_PHASES + len(chains) - 1):
        for k, chain in reversed(list(enumerate(chains))):
            if 0 <= t - k < MIXER_PHASES:
                next(chain, None)
    zbuf[0:CONV_HALO, :] = zbuf[ts:ts + CONV_HALO, :]
    vbuf[0:POOL_HALO, :] = vbuf[ts:ts + POOL_HALO, :]
    counts_ref[...] = jnp.transpose(carry[...])[0:1, :]


def _mixer_rows(r0, seq0, x_ref, lnin_g, lnin_b, w_in, conv_w, conv_b, pool_w, pool_scale, w_out,
                ln1_g, ln1_b, w_r, b_r,
                xp_ref, h1_ref, route_ref, route_t_ref, zbuf, vbuf, carry):
    n = SUB_TILE
    rows = pl.ds(r0, n)
    h0 = _layernorm(x_ref[rows, :], lnin_g[...], lnin_b[...])
    h0b = h0.astype(jnp.bfloat16)
    yield
    u_a = jnp.dot(h0b, w_in[:, 0:IN_PROJ // 2], preferred_element_type=jnp.float32)
    yield
    u_b = jnp.dot(h0b, w_in[:, IN_PROJ // 2:], preferred_element_type=jnp.float32)
    yield
    b_g = u_a[:, 0:CONV_WIDTH]
    c_g = u_a[:, CONV_WIDTH:]
    v_c = u_b[:, 0:CONV_WIDTH]
    v_p = u_b[:, CONV_WIDTH:]

    zbuf[pl.ds(CONV_HALO + r0, n), :] = c_g * v_c
    zext = zbuf[pl.ds(r0, n + CONV_HALO), :]
    z1 = pltpu.roll(zext, 1, axis=0)[CONV_HALO:, :]
    z2 = pltpu.roll(zext, 2, axis=0)[CONV_HALO:, :]
    conv = z2 * conv_w[0:1, :] + z1 * conv_w[1:2, :] + zext[CONV_HALO:, :] * conv_w[2:3, :] + conv_b[...]
    y_conv = b_g * conv

    vbuf[pl.ds(POOL_HALO + r0, n), :] = v_p
    vext = vbuf[pl.ds(r0, n + POOL_HALO), :]
    s2 = vext + pltpu.roll(vext, 1, axis=0)
    s4 = s2[:, POOL_GW:] + pltpu.roll(s2[:, POOL_GW:], 2, axis=0)
    s8 = s4[:, POOL_GW:] + pltpu.roll(s4[:, POOL_GW:], 4, axis=0)
    s16 = s8[:, POOL_GW:] + pltpu.roll(s8[:, POOL_GW:], 8, axis=0)
    wsums = (s2[POOL_HALO:, 0:POOL_GW], s4[POOL_HALO:, 0:POOL_GW],
             s8[POOL_HALO:, 0:POOL_GW], s16[POOL_HALO:, 0:POOL_GW])

    t_pos = seq0 + lax.broadcasted_iota(jnp.int32, (n, 1), 0)
    y_pool = []
    for j, w in enumerate(POOL_WINDOWS):
        inv_cnt = 1.0 / jnp.minimum(t_pos + 1, w).astype(jnp.float32)
        pooled = wsums[j] * inv_cnt - v_p[:, j * POOL_GW:(j + 1) * POOL_GW]
        y_pool.append(jnp.dot(pooled.astype(jnp.bfloat16), pool_w[j], preferred_element_type=jnp.float32))
    y_pool = jnp.concatenate(y_pool, axis=-1) * pool_scale[...]

    ycat = jnp.concatenate([y_conv, y_pool], axis=-1).astype(jnp.bfloat16)
    yield
    mix = jnp.dot(ycat, w_out[...], preferred_element_type=jnp.float32)
    yield
    h1 = _layernorm(DEEPNORM_ALPHA * h0 + mix, ln1_g[...], ln1_b[...])

    h_hi = h1.astype(jnp.bfloat16)
    h_lo = (h1 - h_hi.astype(jnp.float32)).astype(jnp.bfloat16)
    xp_ref[rows, :] = _pack_bf16_pairs(h1)
    h1_ref[rows, :] = h1
    hcat = jnp.concatenate([h_hi, h_lo, h_hi], axis=-1)
    yield
    logits = jnp.dot(hcat, w_r[...], preferred_element_type=jnp.float32) + b_r[...]
    yield
    lt = jnp.transpose(logits)[0:LOGIT_ROWS, :]
    rid = lax.broadcasted_iota(jnp.int32, (LOGIT_ROWS, n), 0).astype(jnp.float32)
    neg = jnp.float32(-jnp.inf)

    def first_argmax(vals):
        m = jnp.max(vals, axis=0, keepdims=True)
        idx = jnp.min(jnp.where(vals == m, rid, float(LOGIT_ROWS)), axis=0, keepdims=True)
        return m, idx

    g_mask = rid < N_GROUPS
    g_max, g_idx = first_argmax(jnp.where(g_mask, lt, neg))
    g_w = 1.0 / jnp.sum(jnp.where(g_mask, jnp.exp(lt - g_max), 0.0), axis=0, keepdims=True)

    e_lo = N_GROUPS + EXPERTS_PER_GROUP * g_idx
    e_vals = jnp.where((rid >= e_lo) & (rid < e_lo + EXPERTS_PER_GROUP), lt, neg)
    m1, i1 = first_argmax(e_vals)
    m2, i2 = first_argmax(jnp.where(rid == i1, neg, e_vals))
    e21 = jnp.exp(m2 - m1)
    w1 = g_w / (1.0 + e21)
    w2 = g_w * e21 / (1.0 + e21)
    id1 = i1 - N_GROUPS
    id2 = i2 - N_GROUPS

    eid = lax.broadcasted_iota(jnp.int32, (N_EXPERTS, n), 0).astype(jnp.float32)
    sel1 = eid == id1
    sel2 = eid == id2
    onehot = (sel1 | sel2).astype(jnp.float32)
    src = lax.broadcasted_iota(jnp.int32, (n, n), 0)
    dst = lax.broadcasted_iota(jnp.int32, (n, n), 1)
    earlier = (src < dst).astype(jnp.bfloat16)
    before = (jnp.dot(onehot.astype(jnp.bfloat16), earlier, preferred_element_type=jnp.float32)
              + carry[0:N_EXPERTS, 0:1])
    rank1 = jnp.sum(jnp.where(sel1, before, 0.0), axis=0, keepdims=True)
    rank2 = jnp.sum(jnp.where(sel2, before, 0.0), axis=0, keepdims=True)
    carry[0:N_EXPERTS, :] = carry[0:N_EXPERTS, :] + jnp.sum(onehot, axis=1, keepdims=True)

    rec_t = jnp.zeros((ROUTE_ROWS, n), jnp.float32)
    rec_row = lax.broadcasted_iota(jnp.int32, (ROUTE_ROWS, n), 0)
    for k, val in ((R_ID1, id1), (R_ID2, id2), (R_W1, w1), (R_W2, w2), (R_RANK1, rank1), (R_RANK2, rank2)):
        rec_t = jnp.where(rec_row == k, val, rec_t)
    route_t_ref[:, rows] = rec_t
    padded = jnp.concatenate([rec_t, jnp.zeros((LANES - ROUTE_ROWS, n), jnp.float32)], axis=0)
    route_ref[rows, :] = jnp.transpose(padded)


def _run_mixer(x2, lnin_g, lnin_b, w_in, conv_w, conv_b, pool_w, pool_scale, w_out, ln1_g, ln1_b,
               w_r, b_r, batch, seq):
    n_tok = batch * seq
    n_s = seq // SEQ_TILE
    tok_map = lambda b, s: (b * n_s + s, 0)

    def const(shape):
        return pl.BlockSpec(shape, lambda b, s: (0,) * len(shape), pipeline_mode=pl.Buffered(1))

    in_specs = [
        pl.BlockSpec((SEQ_TILE, D_MODEL), tok_map),
        const((1, D_MODEL)), const((1, D_MODEL)),
        const((D_MODEL, IN_PROJ)),
        const((CONV_K, CONV_WIDTH)), const((1, CONV_WIDTH)),
        const((len(POOL_WINDOWS), POOL_GW, POOL_GW)), const((1, POOL_WIDTH)),
        const((D_MODEL, D_MODEL)),
        const((1, D_MODEL)), const((1, D_MODEL)),
        const((3 * D_MODEL, LANES)), const((1, LANES)),
    ]
    out_specs = [
        pl.BlockSpec((SEQ_TILE, HALF), tok_map),
        pl.BlockSpec((SEQ_TILE, D_MODEL), tok_map),
        pl.BlockSpec((SEQ_TILE, LANES), tok_map),
        pl.BlockSpec((ROUTE_ROWS, SEQ_TILE), lambda b, s: (0, b * n_s + s)),
        pl.BlockSpec((1, LANES), lambda b, s: (0, 0)),
    ]
    out_shape = [
        jax.ShapeDtypeStruct((n_tok, HALF), jnp.uint32),
        jax.ShapeDtypeStruct((n_tok, D_MODEL), jnp.float32),
        jax.ShapeDtypeStruct((n_tok, LANES), jnp.float32),
        jax.ShapeDtypeStruct((ROUTE_ROWS, n_tok), jnp.float32),
        jax.ShapeDtypeStruct((1, LANES), jnp.float32),
    ]
    return pl.pallas_call(
        _mixer_kernel,
        grid=(batch, n_s),
        in_specs=in_specs,
        out_specs=out_specs,
        out_shape=out_shape,
        scratch_shapes=[
            pltpu.VMEM((SEQ_TILE + CONV_HALO, CONV_WIDTH), jnp.float32),
            pltpu.VMEM((SEQ_TILE + POOL_HALO, POOL_WIDTH), jnp.float32),
            pltpu.VMEM((LANES, LANES), jnp.float32),
            pltpu.VMEM((D_MODEL, IN_PROJ), jnp.bfloat16),
        ],
        compiler_params=pltpu.CompilerParams(
            dimension_semantics=("arbitrary", "arbitrary"), vmem_limit_bytes=VMEM_LIMIT),
    )(x2, lnin_g, lnin_b, w_in, conv_w, conv_b, pool_w, pool_scale, w_out, ln1_g, ln1_b, w_r, b_r)


def _plan_kernel(rt_ref, counts_ref, pos_ref, tile_start_ref, tiles_ref, pad_ref, *, n_rows):
    lane = lax.broadcasted_iota(jnp.int32, (ROUTE_ROWS, LANES), 1)
    counts = jnp.broadcast_to(counts_ref[...], (ROUTE_ROWS, LANES))
    tiles = jnp.floor((counts + (ROW_TILE - 1)) * (1.0 / ROW_TILE))
    tile_end = tiles
    shift = 1
    while shift < N_EXPERTS:
        tile_end = tile_end + jnp.where(lane >= shift, pltpu.roll(tile_end, shift, axis=1), 0.0)
        shift *= 2
    row_start = (tile_end - tiles) * ROW_TILE
    tile_start_ref[...] = (tile_end - tiles)[0:1, :].astype(jnp.int32)
    tiles_ref[...] = tiles[0:1, :].astype(jnp.int32)

    rt = rt_ref[...]
    ids = rt[R_ID1:R_ID2 + 1, :]
    start = jnp.zeros_like(ids)
    for e in range(N_EXPERTS):
        start = jnp.where(ids == e, row_start[0:1, e:e + 1], start)
    pos_ref[...] = (start + rt[R_RANK1:R_RANK2 + 1, :]).astype(jnp.int32)

    sub = lax.broadcasted_iota(jnp.int32, (N_EXPERTS, LANES), 0)
    lane_e = lax.broadcasted_iota(jnp.int32, (N_EXPERTS, LANES), 1)
    diag = sub == lane_e
    pad_lo = jnp.sum(jnp.where(diag, (row_start + counts)[0:1, :], 0.0), axis=1, keepdims=True)
    pad_n = jnp.sum(jnp.where(diag, (tiles * ROW_TILE - counts)[0:1, :], 0.0), axis=1, keepdims=True)
    j = lax.broadcasted_iota(jnp.int32, (N_EXPERTS, ROW_TILE), 1).astype(jnp.float32)
    pad_ref[...] = jnp.where(j < pad_n, pad_lo + j, n_rows + j).astype(jnp.int32)


def _run_plan(route_t, counts, n_rows):
    n_tok = route_t.shape[1]
    full = lambda shape: pl.BlockSpec(shape, lambda i: (0, 0))
    return pl.pallas_call(
        functools.partial(_plan_kernel, n_rows=n_rows),
        grid=(1,),
        in_specs=[full((ROUTE_ROWS, n_tok)), full((1, LANES))],
        out_specs=[full((2, n_tok)), full((1, LANES)), full((1, LANES)), full((N_EXPERTS, ROW_TILE))],
        out_shape=[
            jax.ShapeDtypeStruct((2, n_tok), jnp.int32),
            jax.ShapeDtypeStruct((1, LANES), jnp.int32),
            jax.ShapeDtypeStruct((1, LANES), jnp.int32),
            jax.ShapeDtypeStruct((N_EXPERTS, ROW_TILE), jnp.int32),
        ],
        compiler_params=pltpu.CompilerParams(dimension_semantics=("arbitrary",)),
    )(route_t, counts)


def _sc_mesh():
    return plsc.VectorSubcoreMesh(core_axis_name="core", subcore_axis_name="subcore")


def _sc_worker_id():
    return lax.axis_index("core") * (SC_WORKERS // 2) + lax.axis_index("subcore")


def _dispatch_rows(xp, pos1, pos2, pad_pos, zero_rows, n_rows):
    n_tok, width = xp.shape
    n_win = n_tok // SC_WORKERS // SC_WINDOW
    n_pad = pad_pos.shape[0] // SC_WORKERS // SC_WINDOW
    as_windows = lambda v: v.reshape(-1, SC_WINDOW)

    @functools.partial(
        pl.kernel, out_type=jax.ShapeDtypeStruct((n_rows + ROW_TILE, width), xp.dtype), mesh=_sc_mesh(),
        scratch_types=[pltpu.VMEM((n_win, SC_WINDOW), jnp.int32), pltpu.VMEM((n_win, SC_WINDOW), jnp.int32),
                       pltpu.VMEM((n_pad, SC_WINDOW), jnp.int32),
                       pltpu.VMEM((2, SC_WINDOW, width), xp.dtype), pltpu.VMEM((SC_WINDOW, width), xp.dtype),
                       pltpu.SemaphoreType.DMA((2,)), pltpu.SemaphoreType.DMA((2,)), pltpu.SemaphoreType.DMA])
    def dispatch(xp_hbm, pos1_hbm, pos2_hbm, pad_hbm, zero_hbm, out_hbm,
                 idx1, idx2, idxp, buf, zbuf, lsem, ssem, psem):
        wid = _sc_worker_id()
        pltpu.sync_copy(pad_hbm.at[pl.ds(wid * n_pad, n_pad)], idxp)
        pltpu.sync_copy(zero_hbm, zbuf)
        pads = [pltpu.make_async_copy(zbuf, out_hbm.at[idxp.at[j]], psem) for j in range(n_pad)]
        for cp in pads:
            cp.start()
        pltpu.sync_copy(pos1_hbm.at[pl.ds(wid * n_win, n_win)], idx1)
        pltpu.sync_copy(pos2_hbm.at[pl.ds(wid * n_win, n_win)], idx2)

        def load(j):
            rows = xp_hbm.at[pl.ds((wid * n_win + j) * SC_WINDOW, SC_WINDOW)]
            return pltpu.make_async_copy(rows, buf.at[j % 2], lsem.at[j % 2])

        def scatters(j):
            return [pltpu.make_async_copy(buf.at[j % 2], out_hbm.at[idx.at[j]], ssem.at[j % 2])
                    for idx in (idx1, idx2)]

        load(0).start()
        for j in range(n_win):
            load(j).wait()
            for cp in scatters(j):
                cp.start()
            if j >= 1:
                for cp in scatters(j - 1):
                    cp.wait()
            if j + 1 < n_win:
                load(j + 1).start()
        for cp in scatters(n_win - 1):
            cp.wait()
        for cp in pads:
            cp.wait()

    return dispatch(xp, as_windows(pos1), as_windows(pos2), as_windows(pad_pos), zero_rows)


def _gather_rows(src, idx):
    n_out, width = idx.shape[0], src.shape[1]
    n_win = n_out // SC_WORKERS // SC_WINDOW

    @functools.partial(
        pl.kernel, out_type=jax.ShapeDtypeStruct((n_out, width), src.dtype), mesh=_sc_mesh(),
        scratch_types=[pltpu.VMEM((n_win, SC_WINDOW), jnp.int32), pltpu.VMEM((2, SC_WINDOW, width), src.dtype),
                       pltpu.SemaphoreType.DMA((2,)), pltpu.SemaphoreType.DMA((2,))])
    def gather(src_hbm, idx_hbm, dst_hbm, idx_v, buf, gsem, ssem):
        wid = _sc_worker_id()
        pltpu.sync_copy(idx_hbm.at[pl.ds(wid * n_win, n_win)], idx_v)

        def fetch(j):
            return pltpu.make_async_copy(src_hbm.at[idx_v.at[j]], buf.at[j % 2], gsem.at[j % 2])

        def store(j):
            rows = dst_hbm.at[pl.ds((wid * n_win + j) * SC_WINDOW, SC_WINDOW)]
            return pltpu.make_async_copy(buf.at[j % 2], rows, ssem.at[j % 2])

        fetch(0).start()
        for j in range(n_win):
            fetch(j).wait()
            store(j).start()
            if j >= 1:
                store(j - 1).wait()
            if j + 1 < n_win:
                fetch(j + 1).start()
        store(n_win - 1).wait()

    return gather(src, idx.reshape(-1, SC_WINDOW))


def _tile_copy(hbm, buf, sem, tile, slot, to_hbm):
    rows = hbm.at[pl.ds(pl.multiple_of(tile * ROW_TILE, ROW_TILE), ROW_TILE)]
    if to_hbm:
        return pltpu.make_async_copy(buf.at[slot], rows, sem.at[slot])
    return pltpu.make_async_copy(rows, buf.at[slot], sem.at[slot])


def _weight_copies(w_hbm, wbuf, wsem, expert):
    slot = expert % WEIGHT_BUFFERS
    return [pltpu.make_async_copy(w.at[expert], buf.at[slot], wsem.at[slot]) for w, buf in zip(w_hbm, wbuf)]


def _expert_kernel(ts_ref, nte_ref, x_hbm, wg_hbm, wu_hbm, wd_hbm, y_hbm,
                   xbuf, ybuf, xsem, ysem, wg_buf, wu_buf, wd_buf, wsem, wgu_bf, wd_bf, act_ref, *, max_tiles):
    e = pl.program_id(0)
    first = ts_ref[e]
    count = nte_ref[e]
    n_tiles = ts_ref[N_EXPERTS - 1] + nte_ref[N_EXPERTS - 1]
    ahead = TILE_BUFFERS - 1
    w_hbm = (wg_hbm, wu_hbm, wd_hbm)
    wbuf = (wg_buf, wu_buf, wd_buf)

    @pl.when(e == 0)
    def _():
        for g in range(ahead):
            @pl.when(g < n_tiles)
            def _():
                _tile_copy(x_hbm, xbuf, xsem, g, g, False).start()
        for k in range(WEIGHT_BUFFERS - 1):
            for cp in _weight_copies(w_hbm, wbuf, wsem, k):
                cp.start()

    @pl.when(e + WEIGHT_BUFFERS - 1 < N_EXPERTS)
    def _():
        for cp in _weight_copies(w_hbm, wbuf, wsem, e + WEIGHT_BUFFERS - 1):
            cp.start()

    for cp in _weight_copies(w_hbm, wbuf, wsem, e):
        cp.wait()

    def sync_x(g):
        _tile_copy(x_hbm, xbuf, xsem, g, g % TILE_BUFFERS, False).wait()

        @pl.when(g + ahead < n_tiles)
        def _():
            _tile_copy(x_hbm, xbuf, xsem, g + ahead, (g + ahead) % TILE_BUFFERS, False).start()

    def sync_y_slot(g):
        @pl.when(g >= TILE_BUFFERS)
        def _():
            _tile_copy(y_hbm, ybuf, ysem, g - TILE_BUFFERS, g % TILE_BUFFERS, True).wait()

    def up_proj(g):
        lo, hi = _unpack_bf16_pairs(xbuf[g % TILE_BUFFERS])
        return (jnp.dot(lo.astype(jnp.bfloat16), wgu_bf[0:HALF, :], preferred_element_type=jnp.float32)
                + jnp.dot(hi.astype(jnp.bfloat16), wgu_bf[HALF:, :], preferred_element_type=jnp.float32))

    def put_act(hgu):
        half_g = hgu[:, 0:D_EXPERT]
        act_ref[...] = (half_g * (1.0 + jnp.tanh(half_g)) * hgu[:, D_EXPERT:]).astype(jnp.bfloat16)

    def down_proj(g):
        y = jnp.dot(act_ref[...], wd_bf[...], preferred_element_type=jnp.float32)
        ybuf[g % TILE_BUFFERS] = _pack_bf16_pairs(y)

    @pl.when(count > 0)
    def _():
        slot = e % WEIGHT_BUFFERS
        wgu_bf[:, 0:D_EXPERT] = (0.5 * wg_buf[slot]).astype(jnp.bfloat16)
        wgu_bf[:, D_EXPERT:] = wu_buf[slot].astype(jnp.bfloat16)
        wd_bf[...] = wd_buf[slot].astype(jnp.bfloat16)

        sync_x(first)
        put_act(up_proj(first))

        def tile_body(g, c):
            sync_x(g)
            sync_y_slot(g - 1)
            down_proj(g - 1)
            hgu = up_proj(g)
            put_act(hgu)
            _tile_copy(y_hbm, ybuf, ysem, g - 1, (g - 1) % TILE_BUFFERS, True).start()
            return c

        lax.fori_loop(first + 1, first + count, tile_body, 0)
        last = first + count - 1
        sync_y_slot(last)
        down_proj(last)
        _tile_copy(y_hbm, ybuf, ysem, last, last % TILE_BUFFERS, True).start()

    @pl.when(e == N_EXPERTS - 1)
    def _():
        for k in range(TILE_BUFFERS, 0, -1):
            @pl.when(n_tiles >= k)
            def _():
                _tile_copy(y_hbm, ybuf, ysem, n_tiles - k, (n_tiles - k) % TILE_BUFFERS, True).wait()

        ybuf[0] = jnp.zeros((ROW_TILE, HALF), jnp.uint32)

        def fill(g, c):
            cp = _tile_copy(y_hbm, ybuf, ysem, g, 0, True)
            cp.start()
            cp.wait()
            return c

        lax.fori_loop(n_tiles, max_tiles, fill, 0)


def _run_experts(tile_start, tiles_per_expert, x_sorted, w_gate, w_up, w_down):
    n_rows = x_sorted.shape[0]
    max_tiles = n_rows // ROW_TILE
    hbm = pl.BlockSpec(memory_space=pl.ANY)

    grid_spec = pltpu.PrefetchScalarGridSpec(
        num_scalar_prefetch=2,
        grid=(N_EXPERTS,),
        in_specs=[hbm, hbm, hbm, hbm],
        out_specs=hbm,
        scratch_shapes=[
            pltpu.VMEM((TILE_BUFFERS, ROW_TILE, HALF), jnp.uint32),
            pltpu.VMEM((TILE_BUFFERS, ROW_TILE, HALF), jnp.uint32),
            pltpu.SemaphoreType.DMA((TILE_BUFFERS,)),
            pltpu.SemaphoreType.DMA((TILE_BUFFERS,)),
            pltpu.VMEM((WEIGHT_BUFFERS, D_MODEL, D_EXPERT), jnp.float32),
            pltpu.VMEM((WEIGHT_BUFFERS, D_MODEL, D_EXPERT), jnp.float32),
            pltpu.VMEM((WEIGHT_BUFFERS, D_EXPERT, D_MODEL), jnp.float32),
            pltpu.SemaphoreType.DMA((WEIGHT_BUFFERS,)),
            pltpu.VMEM((D_MODEL, 2 * D_EXPERT), jnp.bfloat16),
            pltpu.VMEM((D_EXPERT, D_MODEL), jnp.bfloat16),
            pltpu.VMEM((ROW_TILE, D_EXPERT), jnp.bfloat16),
        ],
    )
    return pl.pallas_call(
        functools.partial(_expert_kernel, max_tiles=max_tiles),
        grid_spec=grid_spec,
        out_shape=jax.ShapeDtypeStruct((n_rows, HALF), jnp.uint32),
        compiler_params=pltpu.CompilerParams(
            dimension_semantics=("arbitrary",), vmem_limit_bytes=VMEM_LIMIT),
    )(tile_start, tiles_per_expert, x_sorted, w_gate, w_up, w_down)


def _combine_rows(r0, h1_ref, p_ref, route_ref, ya_ref, yb_ref, w_pg, b_pg, w_ple, g_ref, b_ref, o_ref):
    rows = pl.ds(r0, SUB_TILE)
    h1 = h1_ref[rows, :]
    h_hi = h1.astype(jnp.bfloat16)
    p_b = p_ref[rows, :].astype(jnp.bfloat16)
    yield
    gate_pre = jnp.dot(h_hi, w_pg[...], preferred_element_type=jnp.float32)
    ple_pre = jnp.dot(p_b, w_ple[...], preferred_element_type=jnp.float32)
    yield
    ple = ple_pre * (1.0 + jnp.tanh(gate_pre + b_pg[...]))
    route = route_ref[rows, :]
    w1 = route[:, R_W1:R_W1 + 1]
    w2 = route[:, R_W2:R_W2 + 1]
    a_lo, a_hi = _unpack_bf16_pairs(ya_ref[rows, :])
    b_lo, b_hi = _unpack_bf16_pairs(yb_ref[rows, :])
    moe = jnp.concatenate([w1 * a_lo + w2 * b_lo, w1 * a_hi + w2 * b_hi], axis=-1)
    o_ref[rows, :] = _layernorm(DEEPNORM_ALPHA * h1 + ple + moe, g_ref[...], b_ref[...])


def _combine_kernel(h1_ref, p_ref, route_ref, ya_ref, yb_ref, w_pg, b_pg, w_ple, g_ref, b_ref, o_ref):
    chains = [_combine_rows(r0, h1_ref, p_ref, route_ref, ya_ref, yb_ref, w_pg, b_pg, w_ple, g_ref, b_ref, o_ref)
              for r0 in range(0, h1_ref.shape[0], SUB_TILE)]
    for t in range(COMBINE_PHASES + len(chains) - 1):
        for k, chain in reversed(list(enumerate(chains))):
            if 0 <= t - k < COMBINE_PHASES:
                next(chain, None)


def _run_combine(h1, p2, route, y_tok, w_pg, b_pg, w_ple, ln2_g, ln2_b):
    n_tok = h1.shape[0]
    n_t = n_tok // TOKEN_TILE
    tok_map = lambda i: (i, 0)
    const = lambda shape: pl.BlockSpec(shape, lambda i: (0, 0), pipeline_mode=pl.Buffered(1))
    in_specs = [
        pl.BlockSpec((TOKEN_TILE, D_MODEL), tok_map),
        pl.BlockSpec((TOKEN_TILE, PLE_DIM), tok_map),
        pl.BlockSpec((TOKEN_TILE, LANES), tok_map),
        pl.BlockSpec((TOKEN_TILE, HALF), lambda i: (i, 0)),
        pl.BlockSpec((TOKEN_TILE, HALF), lambda i: (i + n_t, 0)),
        const((D_MODEL, D_MODEL)), const((1, D_MODEL)), const((PLE_DIM, D_MODEL)),
        const((1, D_MODEL)), const((1, D_MODEL)),
    ]
    return pl.pallas_call(
        _combine_kernel,
        grid=(n_t,),
        in_specs=in_specs,
        out_specs=pl.BlockSpec((TOKEN_TILE, D_MODEL), tok_map),
        out_shape=jax.ShapeDtypeStruct((n_tok, D_MODEL), jnp.float32),
        compiler_params=pltpu.CompilerParams(dimension_semantics=("arbitrary",), vmem_limit_bytes=VMEM_LIMIT),
    )(h1, p2, route, y_tok, y_tok, w_pg, b_pg, w_ple, ln2_g, ln2_b)


def _split_bf16(w):
    hi = w.astype(jnp.bfloat16)
    lo = (w - hi.astype(jnp.float32)).astype(jnp.bfloat16)
    return hi, lo


def kernel(x, p, ln_in_g, ln_in_b, w_in, conv_w, conv_b, pool_w, pool_scale, w_out, ln1_g, ln1_b,
           w_rg, b_rg, w_re, b_re, w_gate, w_up, w_down, w_pg, b_pg, w_ple, ln2_g, ln2_b):
    batch, seq, _ = x.shape
    n_tok = batch * seq
    bf = jnp.bfloat16
    row = lambda v: v.reshape(1, -1)

    w_r = jnp.concatenate([w_rg[0], jnp.transpose(w_re[0], (1, 0, 2)).reshape(D_MODEL, N_EXPERTS)], axis=1)
    w_r = jnp.pad(w_r, ((0, 0), (0, LANES - w_r.shape[1])))
    w_r_hi, w_r_lo = _split_bf16(w_r)
    w_r_cat = jnp.concatenate([w_r_hi, w_r_hi, w_r_lo], axis=0)
    b_r = jnp.pad(jnp.concatenate([b_rg[0], b_re[0].reshape(-1)]), (0, LANES - N_GROUPS - N_EXPERTS)).reshape(1, LANES)

    x2 = x.reshape(n_tok, D_MODEL)
    p2 = p[0].reshape(n_tok, PLE_DIM)
    mixer_weights = (row(ln_in_g), row(ln_in_b), w_in[0], conv_w[0], row(conv_b[0]),
                     pool_w[0].astype(bf), row(pool_scale[0]), w_out[0].astype(bf), row(ln1_g[0]), row(ln1_b[0]),
                     w_r_cat, b_r)
    combine_weights = ((0.5 * w_pg[0]).astype(bf), row(0.5 * b_pg[0]), (0.5 * w_ple[0]).astype(bf),
                       row(ln2_g[0]), row(ln2_b[0]))
    expert_weights = (w_gate[0].reshape(N_EXPERTS, D_MODEL, D_EXPERT),
                      w_up[0].reshape(N_EXPERTS, D_MODEL, D_EXPERT),
                      w_down[0].reshape(N_EXPERTS, D_EXPERT, D_MODEL))
    zero_rows = jnp.zeros((SC_WINDOW, HALF), jnp.uint32)

    n_rows = -(-(2 * n_tok + N_EXPERTS * (ROW_TILE - 1)) // ROW_TILE) * ROW_TILE
    xp, h1, route, route_t, counts = _run_mixer(x2, *mixer_weights, batch, seq)
    pos, tile_start, tiles_per_expert, pad_pos = _run_plan(route_t, counts, n_rows)
    x_sorted = _dispatch_rows(xp, pos[0], pos[1], pad_pos.reshape(-1), zero_rows, n_rows)
    y_sorted = _run_experts(tile_start[0, :N_EXPERTS], tiles_per_expert[0, :N_EXPERTS], x_sorted, *expert_weights)

    y_tok = _gather_rows(y_sorted, pos.reshape(-1))
    out = _run_combine(h1, p2, route, y_tok, *combine_weights)
    return out.reshape(batch, seq, D_MODEL)
```

```python
import functools

import jax
import jax.numpy as jnp
from jax import lax
from jax.experimental import pallas as pl
from jax.experimental.pallas import tpu as pltpu
from jax.experimental.pallas import tpu_sc as plsc

D_MODEL = 1024
CONV_WIDTH = 512
CONV_K = 3
POOL_WIDTH = 512
POOL_WINDOWS = (2, 4, 8, 16)
POOL_GW = 128
IN_PROJ = 3 * CONV_WIDTH + POOL_WIDTH
N_GROUPS = 4
EXPERTS_PER_GROUP = 8
N_EXPERTS = N_GROUPS * EXPERTS_PER_GROUP
D_EXPERT = 256
PLE_DIM = 256
LN_EPS = 1e-5
DEEPNORM_ALPHA = 2.0 ** 0.25

LANES = 128
HALF = D_MODEL // 2
CONV_HALO = 8
POOL_HALO = 16
SEQ_TILE = 1024
SUB_TILE = 256
MIXER_PHASES = 8
COMBINE_PHASES = 3
ROW_TILE = 256
TILE_BUFFERS = 8
SC_WORKERS = 32
SC_WINDOW = 64
WEIGHT_BUFFERS = 3
TOKEN_TILE = 1024
VMEM_LIMIT = 56 * 1024 * 1024

R_ID1, R_ID2, R_RANK1, R_RANK2, R_W1, R_W2 = range(6)
ROUTE_ROWS = 8
LOGIT_ROWS = 40


def _layernorm(x, g, b):
    mu = jnp.mean(x, axis=-1, keepdims=True)
    xc = x - mu
    var = jnp.mean(xc * xc, axis=-1, keepdims=True)
    return xc * lax.rsqrt(var + LN_EPS) * g + b


def _pack_bf16_pairs(v):
    bits = lax.bitcast_convert_type(v.astype(jnp.bfloat16).astype(jnp.float32), jnp.uint32)
    return bits[:, HALF:] | (bits[:, :HALF] >> 16)


def _unpack_bf16_pairs(w):
    lo = lax.bitcast_convert_type(w << 16, jnp.float32)
    hi = lax.bitcast_convert_type(w & jnp.uint32(0xFFFF0000), jnp.float32)
    return lo, hi


def _mixer_kernel(x_ref, lnin_g, lnin_b, w_in_f32, conv_w, conv_b, pool_w, pool_scale, w_out,
                  ln1_g, ln1_b, w_r, b_r,
                  xp_ref, h1_ref, route_ref, route_t_ref, counts_ref,
                  zbuf, vbuf, carry, w_in):
    b = pl.program_id(0)
    s = pl.program_id(1)
    ts = x_ref.shape[0]

    @pl.when(s == 0)
    def _():
        zbuf[0:CONV_HALO, :] = jnp.zeros((CONV_HALO, CONV_WIDTH), jnp.float32)
        vbuf[0:POOL_HALO, :] = jnp.zeros((POOL_HALO, POOL_WIDTH), jnp.float32)

    @pl.when((b == 0) & (s == 0))
    def _():
        carry[...] = jnp.zeros_like(carry)
        w_in[...] = w_in_f32[...].astype(jnp.bfloat16)

    chains = [_mixer_rows(r0, s * ts + r0, x_ref, lnin_g, lnin_b, w_in, conv_w, conv_b, pool_w, pool_scale,
                          w_out, ln1_g, ln1_b, w_r, b_r,
                          xp_ref, h1_ref, route_ref, route_t_ref, zbuf, vbuf, carry)
              for r0 in range(0, ts, SUB_TILE)]
    for t in range(MIXER_PHASES + len(chains) - 1):
        for k, chain in reversed(list(enumerate(chains))):
            if 0 <= t - k < MIXER_PHASES:
                next(chain, None)
    zbuf[0:CONV_HALO, :] = zbuf[ts:ts + CONV_HALO, :]
    vbuf[0:POOL_HALO, :] = vbuf[ts:ts + POOL_HALO, :]
    counts_ref[...] = jnp.transpose(carry[...])[0:1, :]


def _mixer_rows(r0, seq0, x_ref, lnin_g, lnin_b, w_in, conv_w, conv_b, pool_w, pool_scale, w_out,
                ln1_g, ln1_b, w_r, b_r,
                xp_ref, h1_ref, route_ref, route_t_ref, zbuf, vbuf, carry):
    n = SUB_TILE
    rows = pl.ds(r0, n)
    h0 = _layernorm(x_ref[rows, :], lnin_g[...], lnin_b[...])
    h0b = h0.astype(jnp.bfloat16)
    yield
    u_a = jnp.dot(h0b, w_in[:, 0:IN_PROJ // 2], preferred_element_type=jnp.float32)
    yield
    u_b = jnp.dot(h0b, w_in[:, IN_PROJ // 2:], preferred_element_type=jnp.float32)
    yield
    b_g = u_a[:, 0:CONV_WIDTH]
    c_g = u_a[:, CONV_WIDTH:]
    v_c = u_b[:, 0:CONV_WIDTH]
    v_p = u_b[:, CONV_WIDTH:]

    zbuf[pl.ds(CONV_HALO + r0, n), :] = c_g * v_c
    zext = zbuf[pl.ds(r0, n + CONV_HALO), :]
    z1 = pltpu.roll(zext, 1, axis=0)[CONV_HALO:, :]
    z2 = pltpu.roll(zext, 2, axis=0)[CONV_HALO:, :]
    conv = z2 * conv_w[0:1, :] + z1 * conv_w[1:2, :] + zext[CONV_HALO:, :] * conv_w[2:3, :] + conv_b[...]
    y_conv = b_g * conv

    vbuf[pl.ds(POOL_HALO + r0, n), :] = v_p
    vext = vbuf[pl.ds(r0, n + POOL_HALO), :]
    s2 = vext + pltpu.roll(vext, 1, axis=0)
    s4 = s2[:, POOL_GW:] + pltpu.roll(s2[:, POOL_GW:], 2, axis=0)
    s8 = s4[:, POOL_GW:] + pltpu.roll(s4[:, POOL_GW:], 4, axis=0)
    s16 = s8[:, POOL_GW:] + pltpu.roll(s8[:, POOL_GW:], 8, axis=0)
    wsums = (s2[POOL_HALO:, 0:POOL_GW], s4[POOL_HALO:, 0:POOL_GW],
             s8[POOL_HALO:, 0:POOL_GW], s16[POOL_HALO:, 0:POOL_GW])

    t_pos = seq0 + lax.broadcasted_iota(jnp.int32, (n, 1), 0)
    y_pool = []
    for j, w in enumerate(POOL_WINDOWS):
        inv_cnt = 1.0 / jnp.minimum(t_pos + 1, w).astype(jnp.float32)
        pooled = wsums[j] * inv_cnt - v_p[:, j * POOL_GW:(j + 1) * POOL_GW]
        y_pool.append(jnp.dot(pooled.astype(jnp.bfloat16), pool_w[j], preferred_element_type=jnp.float32))
    y_pool = jnp.concatenate(y_pool, axis=-1) * pool_scale[...]

    ycat = jnp.concatenate([y_conv, y_pool], axis=-1).astype(jnp.bfloat16)
    yield
    mix = jnp.dot(ycat, w_out[...], preferred_element_type=jnp.float32)
    yield
    h1 = _layernorm(DEEPNORM_ALPHA * h0 + mix, ln1_g[...], ln1_b[...])

    h_hi = h1.astype(jnp.bfloat16)
    h_lo = (h1 - h_hi.astype(jnp.float32)).astype(jnp.bfloat16)
    xp_ref[rows, :] = _pack_bf16_pairs(h1)
    h1_ref[rows, :] = h1
    hcat = jnp.concatenate([h_hi, h_lo, h_hi], axis=-1)
    yield
    logits = jnp.dot(hcat, w_r[...], preferred_element_type=jnp.float32) + b_r[...]
    yield
    lt = jnp.transpose(logits)[0:LOGIT_ROWS, :]
    rid = lax.broadcasted_iota(jnp.int32, (LOGIT_ROWS, n), 0).astype(jnp.float32)
    neg = jnp.float32(-jnp.inf)

    def first_argmax(vals):
        m = jnp.max(vals, axis=0, keepdims=True)
        idx = jnp.min(jnp.where(vals == m, rid, float(LOGIT_ROWS)), axis=0, keepdims=True)
        return m, idx

    g_mask = rid < N_GROUPS
    g_max, g_idx = first_argmax(jnp.where(g_mask, lt, neg))
    g_w = 1.0 / jnp.sum(jnp.where(g_mask, jnp.exp(lt - g_max), 0.0), axis=0, keepdims=True)

    e_lo = N_GROUPS + EXPERTS_PER_GROUP * g_idx
    e_vals = jnp.where((rid >= e_lo) & (rid < e_lo + EXPERTS_PER_GROUP), lt, neg)
    m1, i1 = first_argmax(e_vals)
    m2, i2 = first_argmax(jnp.where(rid == i1, neg, e_vals))
    e21 = jnp.exp(m2 - m1)
    w1 = g_w / (1.0 + e21)
    w2 = g_w * e21 / (1.0 + e21)
    id1 = i1 - N_GROUPS
    id2 = i2 - N_GROUPS

    eid = lax.broadcasted_iota(jnp.int32, (N_EXPERTS, n), 0).astype(jnp.float32)
    sel1 = eid == id1
    sel2 = eid == id2
    onehot = (sel1 | sel2).astype(jnp.float32)
    src = lax.broadcasted_iota(jnp.int32, (n, n), 0)
    dst = lax.broadcasted_iota(jnp.int32, (n, n), 1)
    earlier = (src < dst).astype(jnp.bfloat16)
    before = (jnp.dot(onehot.astype(jnp.bfloat16), earlier, preferred_element_type=jnp.float32)
              + carry[0:N_EXPERTS, 0:1])
    rank1 = jnp.sum(jnp.where(sel1, before, 0.0), axis=0, keepdims=True)
    rank2 = jnp.sum(jnp.where(sel2, before, 0.0), axis=0, keepdims=True)
    carry[0:N_EXPERTS, :] = carry[0:N_EXPERTS, :] + jnp.sum(onehot, axis=1, keepdims=True)

    rec_t = jnp.zeros((ROUTE_ROWS, n), jnp.float32)
    rec_row = lax.broadcasted_iota(jnp.int32, (ROUTE_ROWS, n), 0)
    for k, val in ((R_ID1, id1), (R_ID2, id2), (R_W1, w1), (R_W2, w2), (R_RANK1, rank1), (R_RANK2, rank2)):
        rec_t = jnp.where(rec_row == k, val, rec_t)
    route_t_ref[:, rows] = rec_t
    padded = jnp.concatenate([rec_t, jnp.zeros((LANES - ROUTE_ROWS, n), jnp.float32)], axis=0)
    route_ref[rows, :] = jnp.transpose(padded)


def _run_mixer(x2, lnin_g, lnin_b, w_in, conv_w, conv_b, pool_w, pool_scale, w_out, ln1_g, ln1_b,
               w_r, b_r, batch, seq):
    n_tok = batch * seq
    n_s = seq // SEQ_TILE
    tok_map = lambda b, s: (b * n_s + s, 0)

    def const(shape):
        return pl.BlockSpec(shape, lambda b, s: (0,) * len(shape), pipeline_mode=pl.Buffered(1))

    in_specs = [
        pl.BlockSpec((SEQ_TILE, D_MODEL), tok_map),
        const((1, D_MODEL)), const((1, D_MODEL)),
        const((D_MODEL, IN_PROJ)),
        const((CONV_K, CONV_WIDTH)), const((1, CONV_WIDTH)),
        const((len(POOL_WINDOWS), POOL_GW, POOL_GW)), const((1, POOL_WIDTH)),
        const((D_MODEL, D_MODEL)),
        const((1, D_MODEL)), const((1, D_MODEL)),
        const((3 * D_MODEL, LANES)), const((1, LANES)),
    ]
    out_specs = [
        pl.BlockSpec((SEQ_TILE, HALF), tok_map),
        pl.BlockSpec((SEQ_TILE, D_MODEL), tok_map),
        pl.BlockSpec((SEQ_TILE, LANES), tok_map),
        pl.BlockSpec((ROUTE_ROWS, SEQ_TILE), lambda b, s: (0, b * n_s + s)),
        pl.BlockSpec((1, LANES), lambda b, s: (0, 0)),
    ]
    out_shape = [
        jax.ShapeDtypeStruct((n_tok, HALF), jnp.uint32),
        jax.ShapeDtypeStruct((n_tok, D_MODEL), jnp.float32),
        jax.ShapeDtypeStruct((n_tok, LANES), jnp.float32),
        jax.ShapeDtypeStruct((ROUTE_ROWS, n_tok), jnp.float32),
        jax.ShapeDtypeStruct((1, LANES), jnp.float32),
    ]
    return pl.pallas_call(
        _mixer_kernel,
        grid=(batch, n_s),
        in_specs=in_specs,
        out_specs=out_specs,
        out_shape=out_shape,
        scratch_shapes=[
            pltpu.VMEM((SEQ_TILE + CONV_HALO, CONV_WIDTH), jnp.float32),
            pltpu.VMEM((SEQ_TILE + POOL_HALO, POOL_WIDTH), jnp.float32),
            pltpu.VMEM((LANES, LANES), jnp.float32),
            pltpu.VMEM((D_MODEL, IN_PROJ), jnp.bfloat16),
        ],
        compiler_params=pltpu.CompilerParams(
            dimension_semantics=("arbitrary", "arbitrary"), vmem_limit_bytes=VMEM_LIMIT),
    )(x2, lnin_g, lnin_b, w_in, conv_w, conv_b, pool_w, pool_scale, w_out, ln1_g, ln1_b, w_r, b_r)


def _plan_kernel(rt_ref, counts_ref, pos_ref, tile_start_ref, tiles_ref, pad_ref, *, n_rows):
    lane = lax.broadcasted_iota(jnp.int32, (ROUTE_ROWS, LANES), 1)
    counts = jnp.broadcast_to(counts_ref[...], (ROUTE_ROWS, LANES))
    tiles = jnp.floor((counts + (ROW_TILE - 1)) * (1.0 / ROW_TILE))
    tile_end = tiles
    shift = 1
    while shift < N_EXPERTS:
        tile_end = tile_end + jnp.where(lane >= shift, pltpu.roll(tile_end, shift, axis=1), 0.0)
        shift *= 2
    row_start = (tile_end - tiles) * ROW_TILE
    tile_start_ref[...] = (tile_end - tiles)[0:1, :].astype(jnp.int32)
    tiles_ref[...] = tiles[0:1, :].astype(jnp.int32)

    rt = rt_ref[...]
    ids = rt[R_ID1:R_ID2 + 1, :]
    start = jnp.zeros_like(ids)
    for e in range(N_EXPERTS):
        start = jnp.where(ids == e, row_start[0:1, e:e + 1], start)
    pos_ref[...] = (start + rt[R_RANK1:R_RANK2 + 1, :]).astype(jnp.int32)

    sub = lax.broadcasted_iota(jnp.int32, (N_EXPERTS, LANES), 0)
    lane_e = lax.broadcasted_iota(jnp.int32, (N_EXPERTS, LANES), 1)
    diag = sub == lane_e
    pad_lo = jnp.sum(jnp.where(diag, (row_start + counts)[0:1, :], 0.0), axis=1, keepdims=True)
    pad_n = jnp.sum(jnp.where(diag, (tiles * ROW_TILE - counts)[0:1, :], 0.0), axis=1, keepdims=True)
    j = lax.broadcasted_iota(jnp.int32, (N_EXPERTS, ROW_TILE), 1).astype(jnp.float32)
    pad_ref[...] = jnp.where(j < pad_n, pad_lo + j, n_rows + j).astype(jnp.int32)


def _run_plan(route_t, counts, n_rows):
    n_tok = route_t.shape[1]
    full = lambda shape: pl.BlockSpec(shape, lambda i: (0, 0))
    return pl.pallas_call(
        functools.partial(_plan_kernel, n_rows=n_rows),
        grid=(1,),
        in_specs=[full((ROUTE_ROWS, n_tok)), full((1, LANES))],
        out_specs=[full((2, n_tok)), full((1, LANES)), full((1, LANES)), full((N_EXPERTS, ROW_TILE))],
        out_shape=[
            jax.ShapeDtypeStruct((2, n_tok), jnp.int32),
            jax.ShapeDtypeStruct((1, LANES), jnp.int32),
            jax.ShapeDtypeStruct((1, LANES), jnp.int32),
            jax.ShapeDtypeStruct((N_EXPERTS, ROW_TILE), jnp.int32),
        ],
        compiler_params=pltpu.CompilerParams(dimension_semantics=("arbitrary",)),
    )(route_t, counts)


def _sc_mesh():
    return plsc.VectorSubcoreMesh(core_axis_name="core", subcore_axis_name="subcore")


def _sc_worker_id():
    return lax.axis_index("core") * (SC_WORKERS // 2) + lax.axis_index("subcore")


def _dispatch_rows(xp, pos1, pos2, pad_pos, zero_rows, n_rows):
    n_tok, width = xp.shape
    n_win = n_tok // SC_WORKERS // SC_WINDOW
    n_pad = pad_pos.shape[0] // SC_WORKERS // SC_WINDOW
    as_windows = lambda v: v.reshape(-1, SC_WINDOW)

    @functools.partial(
        pl.kernel, out_type=jax.ShapeDtypeStruct((n_rows + ROW_TILE, width), xp.dtype), mesh=_sc_mesh(),
        scratch_types=[pltpu.VMEM((n_win, SC_WINDOW), jnp.int32), pltpu.VMEM((n_win, SC_WINDOW), jnp.int32),
                       pltpu.VMEM((n_pad, SC_WINDOW), jnp.int32),
                       pltpu.VMEM((2, SC_WINDOW, width), xp.dtype), pltpu.VMEM((SC_WINDOW, width), xp.dtype),
                       pltpu.SemaphoreType.DMA((2,)), pltpu.SemaphoreType.DMA((2,)), pltpu.SemaphoreType.DMA])
    def dispatch(xp_hbm, pos1_hbm, pos2_hbm, pad_hbm, zero_hbm, out_hbm,
                 idx1, idx2, idxp, buf, zbuf, lsem, ssem, psem):
        wid = _sc_worker_id()
        pltpu.sync_copy(pad_hbm.at[pl.ds(wid * n_pad, n_pad)], idxp)
        pltpu.sync_copy(zero_hbm, zbuf)
        pads = [pltpu.make_async_copy(zbuf, out_hbm.at[idxp.at[j]], psem) for j in range(n_pad)]
        for cp in pads:
            cp.start()
        pltpu.sync_copy(pos1_hbm.at[pl.ds(wid * n_win, n_win)], idx1)
        pltpu.sync_copy(pos2_hbm.at[pl.ds(wid * n_win, n_win)], idx2)

        def load(j):
            rows = xp_hbm.at[pl.ds((wid * n_win + j) * SC_WINDOW, SC_WINDOW)]
            return pltpu.make_async_copy(rows, buf.at[j % 2], lsem.at[j % 2])

        def scatters(j):
            return [pltpu.make_async_copy(buf.at[j % 2], out_hbm.at[idx.at[j]], ssem.at[j % 2])
                    for idx in (idx1, idx2)]

        load(0).start()
        for j in range(n_win):
            load(j).wait()
            for cp in scatters(j):
                cp.start()
            if j >= 1:
                for cp in scatters(j - 1):
                    cp.wait()
            if j + 1 < n_win:
                load(j + 1).start()
        for cp in scatters(n_win - 1):
            cp.wait()
        for cp in pads:
            cp.wait()

    return dispatch(xp, as_windows(pos1), as_windows(pos2), as_windows(pad_pos), zero_rows)


def _gather_rows(src, idx):
    n_out, width = idx.shape[0], src.shape[1]
    n_win = n_out // SC_WORKERS // SC_WINDOW

    @functools.partial(
        pl.kernel, out_type=jax.ShapeDtypeStruct((n_out, width), src.dtype), mesh=_sc_mesh(),
        scratch_types=[pltpu.VMEM((n_win, SC_WINDOW), jnp.int32), pltpu.VMEM((2, SC_WINDOW, width), src.dtype),
                       pltpu.SemaphoreType.DMA((2,)), pltpu.SemaphoreType.DMA((2,))])
    def gather(src_hbm, idx_hbm, dst_hbm, idx_v, buf, gsem, ssem):
        wid = _sc_worker_id()
        pltpu.sync_copy(idx_hbm.at[pl.ds(wid * n_win, n_win)], idx_v)

        def fetch(j):
            return pltpu.make_async_copy(src_hbm.at[idx_v.at[j]], buf.at[j % 2], gsem.at[j % 2])

        def store(j):
            rows = dst_hbm.at[pl.ds((wid * n_win + j) * SC_WINDOW, SC_WINDOW)]
            return pltpu.make_async_copy(buf.at[j % 2], rows, ssem.at[j % 2])

        fetch(0).start()
        for j in range(n_win):
            fetch(j).wait()
            store(j).start()
            if j >= 1:
                store(j - 1).wait()
            if j + 1 < n_win:
                fetch(j + 1).start()
        store(n_win - 1).wait()

    return gather(src, idx.reshape(-1, SC_WINDOW))


def _tile_copy(hbm, buf, sem, tile, slot, to_hbm):
    rows = hbm.at[pl.ds(pl.multiple_of(tile * ROW_TILE, ROW_TILE), ROW_TILE)]
    if to_hbm:
        return pltpu.make_async_copy(buf.at[slot], rows, sem.at[slot])
    return pltpu.make_async_copy(rows, buf.at[slot], sem.at[slot])


def _weight_copies(w_hbm, wbuf, wsem, expert):
    slot = expert % WEIGHT_BUFFERS
    return [pltpu.make_async_copy(w.at[expert], buf.at[slot], wsem.at[slot]) for w, buf in zip(w_hbm, wbuf)]


def _expert_kernel(ts_ref, nte_ref, x_hbm, wg_hbm, wu_hbm, wd_hbm, y_hbm,
                   xbuf, ybuf, xsem, ysem, wg_buf, wu_buf, wd_buf, wsem, wgu_bf, wd_bf, act_ref, *, max_tiles):
    e = pl.program_id(0)
    first = ts_ref[e]
    count = nte_ref[e]
    n_tiles = ts_ref[N_EXPERTS - 1] + nte_ref[N_EXPERTS - 1]
    ahead = TILE_BUFFERS - 2
    w_hbm = (wg_hbm, wu_hbm, wd_hbm)
    wbuf = (wg_buf, wu_buf, wd_buf)

    @pl.when(e == 0)
    def _():
        for g in range(ahead):
            @pl.when(g < n_tiles)
            def _():
                _tile_copy(x_hbm, xbuf, xsem, g, g, False).start()
        for k in range(WEIGHT_BUFFERS - 1):
            for cp in _weight_copies(w_hbm, wbuf, wsem, k):
                cp.start()

    @pl.when(e + WEIGHT_BUFFERS - 1 < N_EXPERTS)
    def _():
        for cp in _weight_copies(w_hbm, wbuf, wsem, e + WEIGHT_BUFFERS - 1):
            cp.start()

    for cp in _weight_copies(w_hbm, wbuf, wsem, e):
        cp.wait()

    def sync_x(g):
        _tile_copy(x_hbm, xbuf, xsem, g, g % TILE_BUFFERS, False).wait()

        @pl.when(g + ahead < n_tiles)
        def _():
            _tile_copy(x_hbm, xbuf, xsem, g + ahead, (g + ahead) % TILE_BUFFERS, False).start()

    def sync_y_slot(g):
        @pl.when(g >= TILE_BUFFERS)
        def _():
            _tile_copy(y_hbm, ybuf, ysem, g - TILE_BUFFERS, g % TILE_BUFFERS, True).wait()

    def up_proj(g):
        lo, hi = _unpack_bf16_pairs(xbuf[g % TILE_BUFFERS])
        return (jnp.dot(lo.astype(jnp.bfloat16), wgu_bf[0:HALF, :], preferred_element_type=jnp.float32)
                + jnp.dot(hi.astype(jnp.bfloat16), wgu_bf[HALF:, :], preferred_element_type=jnp.float32))

    def put_act(hgu, which):
        hg = hgu[:, 0:D_EXPERT]
        act_ref[which] = (hg * jax.nn.sigmoid(hg) * hgu[:, D_EXPERT:]).astype(jnp.bfloat16)

    def down_proj(g, which):
        y = jnp.dot(act_ref[which], wd_bf[...], preferred_element_type=jnp.float32)
        ybuf[g % TILE_BUFFERS] = _pack_bf16_pairs(y)

    def start_y(g):
        _tile_copy(y_hbm, ybuf, ysem, g, g % TILE_BUFFERS, True).start()

    @pl.when(count > 0)
    def _():
        slot = e % WEIGHT_BUFFERS
        wgu_bf[:, 0:D_EXPERT] = wg_buf[slot].astype(jnp.bfloat16)
        wgu_bf[:, D_EXPERT:] = wu_buf[slot].astype(jnp.bfloat16)
        wd_bf[...] = wd_buf[slot].astype(jnp.bfloat16)

        sync_x(first)
        put_act(up_proj(first), 0)

        def pair_body(i, c):
            g = first + 1 + 2 * i
            sync_x(g)
            sync_x(g + 1)
            sync_y_slot(g - 1)
            sync_y_slot(g)
            down_proj(g - 1, 0)
            hgu0 = up_proj(g)
            put_act(hgu0, 1)
            hgu1 = up_proj(g + 1)
            down_proj(g, 1)
            put_act(hgu1, 0)
            start_y(g - 1)
            start_y(g)
            return c

        n_pairs = (count - 1) // 2
        lax.fori_loop(0, n_pairs, pair_body, 0)

        @pl.when((count - 1) % 2 == 1)
        def _():
            g = first + count - 1
            sync_x(g)
            sync_y_slot(g - 1)
            down_proj(g - 1, 0)
            put_act(up_proj(g), 0)
            start_y(g - 1)

        last = first + count - 1
        sync_y_slot(last)
        down_proj(last, 0)
        start_y(last)

    @pl.when(e == N_EXPERTS - 1)
    def _():
        for k in range(TILE_BUFFERS, 0, -1):
            @pl.when(n_tiles >= k)
            def _():
                _tile_copy(y_hbm, ybuf, ysem, n_tiles - k, (n_tiles - k) % TILE_BUFFERS, True).wait()

        ybuf[0] = jnp.zeros((ROW_TILE, HALF), jnp.uint32)

        def fill(g, c):
            cp = _tile_copy(y_hbm, ybuf, ysem, g, 0, True)
            cp.start()
            cp.wait()
            return c

        lax.fori_loop(n_tiles, max_tiles, fill, 0)


def _run_experts(tile_start, tiles_per_expert, x_sorted, w_gate, w_up, w_down):
    n_rows = x_sorted.shape[0]
    max_tiles = n_rows // ROW_TILE
    hbm = pl.BlockSpec(memory_space=pl.ANY)

    grid_spec = pltpu.PrefetchScalarGridSpec(
        num_scalar_prefetch=2,
        grid=(N_EXPERTS,),
        in_specs=[hbm, hbm, hbm, hbm],
        out_specs=hbm,
        scratch_shapes=[
            pltpu.VMEM((TILE_BUFFERS, ROW_TILE, HALF), jnp.uint32),
            pltpu.VMEM((TILE_BUFFERS, ROW_TILE, HALF), jnp.uint32),
            pltpu.SemaphoreType.DMA((TILE_BUFFERS,)),
            pltpu.SemaphoreType.DMA((TILE_BUFFERS,)),
            pltpu.VMEM((WEIGHT_BUFFERS, D_MODEL, D_EXPERT), jnp.float32),
            pltpu.VMEM((WEIGHT_BUFFERS, D_MODEL, D_EXPERT), jnp.float32),
            pltpu.VMEM((WEIGHT_BUFFERS, D_EXPERT, D_MODEL), jnp.float32),
            pltpu.SemaphoreType.DMA((WEIGHT_BUFFERS,)),
            pltpu.VMEM((D_MODEL, 2 * D_EXPERT), jnp.bfloat16),
            pltpu.VMEM((D_EXPERT, D_MODEL), jnp.bfloat16),
            pltpu.VMEM((2, ROW_TILE, D_EXPERT), jnp.bfloat16),
        ],
    )
    return pl.pallas_call(
        functools.partial(_expert_kernel, max_tiles=max_tiles),
        grid_spec=grid_spec,
        out_shape=jax.ShapeDtypeStruct((n_rows, HALF), jnp.uint32),
        compiler_params=pltpu.CompilerParams(
            dimension_semantics=("arbitrary",), vmem_limit_bytes=VMEM_LIMIT),
    )(tile_start, tiles_per_expert, x_sorted, w_gate, w_up, w_down)


def _combine_rows(r0, h1_ref, p_ref, route_ref, ya_ref, yb_ref, w_pg, b_pg, w_ple, g_ref, b_ref, o_ref):
    rows = pl.ds(r0, SUB_TILE)
    h1 = h1_ref[rows, :]
    h_hi = h1.astype(jnp.bfloat16)
    p_b = p_ref[rows, :].astype(jnp.bfloat16)
    yield
    gate_pre = jnp.dot(h_hi, w_pg[...], preferred_element_type=jnp.float32)
    ple_pre = jnp.dot(p_b, w_ple[...], preferred_element_type=jnp.float32)
    yield
    ple = ple_pre * jax.nn.sigmoid(gate_pre + b_pg[...])
    route = route_ref[rows, :]
    w1 = route[:, R_W1:R_W1 + 1]
    w2 = route[:, R_W2:R_W2 + 1]
    a_lo, a_hi = _unpack_bf16_pairs(ya_ref[rows, :])
    b_lo, b_hi = _unpack_bf16_pairs(yb_ref[rows, :])
    moe = jnp.concatenate([w1 * a_lo + w2 * b_lo, w1 * a_hi + w2 * b_hi], axis=-1)
    o_ref[rows, :] = _layernorm(DEEPNORM_ALPHA * h1 + ple + moe, g_ref[...], b_ref[...])


def _combine_kernel(h1_ref, p_ref, route_ref, ya_ref, yb_ref, w_pg, b_pg, w_ple, g_ref, b_ref, o_ref):
    chains = [_combine_rows(r0, h1_ref, p_ref, route_ref, ya_ref, yb_ref, w_pg, b_pg, w_ple, g_ref, b_ref, o_ref)
              for r0 in range(0, h1_ref.shape[0], SUB_TILE)]
    for t in range(COMBINE_PHASES + len(chains) - 1):
        for k, chain in reversed(list(enumerate(chains))):
            if 0 <= t - k < COMBINE_PHASES:
                next(chain, None)


def _run_combine(h1, p2, route, y_tok, w_pg, b_pg, w_ple, ln2_g, ln2_b):
    n_tok = h1.shape[0]
    n_t = n_tok // TOKEN_TILE
    tok_map = lambda i: (i, 0)
    const = lambda shape: pl.BlockSpec(shape, lambda i: (0, 0), pipeline_mode=pl.Buffered(1))
    in_specs = [
        pl.BlockSpec((TOKEN_TILE, D_MODEL), tok_map),
        pl.BlockSpec((TOKEN_TILE, PLE_DIM), tok_map),
        pl.BlockSpec((TOKEN_TILE, LANES), tok_map),
        pl.BlockSpec((TOKEN_TILE, HALF), lambda i: (i, 0)),
        pl.BlockSpec((TOKEN_TILE, HALF), lambda i: (i + n_t, 0)),
        const((D_MODEL, D_MODEL)), const((1, D_MODEL)), const((PLE_DIM, D_MODEL)),
        const((1, D_MODEL)), const((1, D_MODEL)),
    ]
    return pl.pallas_call(
        _combine_kernel,
        grid=(n_t,),
        in_specs=in_specs,
        out_specs=pl.BlockSpec((TOKEN_TILE, D_MODEL), tok_map),
        out_shape=jax.ShapeDtypeStruct((n_tok, D_MODEL), jnp.float32),
        compiler_params=pltpu.CompilerParams(dimension_semantics=("arbitrary",), vmem_limit_bytes=VMEM_LIMIT),
    )(h1, p2, route, y_tok, y_tok, w_pg, b_pg, w_ple, ln2_g, ln2_b)


def _split_bf16(w):
    hi = w.astype(jnp.bfloat16)
    lo = (w - hi.astype(jnp.float32)).astype(jnp.bfloat16)
    return hi, lo


def kernel(x, p, ln_in_g, ln_in_b, w_in, conv_w, conv_b, pool_w, pool_scale, w_out, ln1_g, ln1_b,
           w_rg, b_rg, w_re, b_re, w_gate, w_up, w_down, w_pg, b_pg, w_ple, ln2_g, ln2_b):
    batch, seq, _ = x.shape
    n_tok = batch * seq
    bf = jnp.bfloat16
    row = lambda v: v.reshape(1, -1)

    w_r = jnp.concatenate([w_rg[0], jnp.transpose(w_re[0], (1, 0, 2)).reshape(D_MODEL, N_EXPERTS)], axis=1)
    w_r = jnp.pad(w_r, ((0, 0), (0, LANES - w_r.shape[1])))
    w_r_hi, w_r_lo = _split_bf16(w_r)
    w_r_cat = jnp.concatenate([w_r_hi, w_r_hi, w_r_lo], axis=0)
    b_r = jnp.pad(jnp.concatenate([b_rg[0], b_re[0].reshape(-1)]), (0, LANES - N_GROUPS - N_EXPERTS)).reshape(1, LANES)

    x2 = x.reshape(n_tok, D_MODEL)
    p2 = p[0].reshape(n_tok, PLE_DIM)
    mixer_weights = (row(ln_in_g), row(ln_in_b), w_in[0], conv_w[0], row(conv_b[0]),
                     pool_w[0].astype(bf), row(pool_scale[0]), w_out[0].astype(bf), row(ln1_g[0]), row(ln1_b[0]),
                     w_r_cat, b_r)
    combine_weights = (w_pg[0].astype(bf), row(b_pg[0]), w_ple[0].astype(bf), row(ln2_g[0]), row(ln2_b[0]))
    expert_weights = (w_gate[0].reshape(N_EXPERTS, D_MODEL, D_EXPERT),
                      w_up[0].reshape(N_EXPERTS, D_MODEL, D_EXPERT),
                      w_down[0].reshape(N_EXPERTS, D_EXPERT, D_MODEL))
    zero_rows = jnp.zeros((SC_WINDOW, HALF), jnp.uint32)

    n_rows = -(-(2 * n_tok + N_EXPERTS * (ROW_TILE - 1)) // ROW_TILE) * ROW_TILE
    xp, h1, route, route_t, counts = _run_mixer(x2, *mixer_weights, batch, seq)
    pos, tile_start, tiles_per_expert, pad_pos = _run_plan(route_t, counts, n_rows)
    x_sorted = _dispatch_rows(xp, pos[0], pos[1], pad_pos.reshape(-1), zero_rows, n_rows)
    y_sorted = _run_experts(tile_start[0, :N_EXPERTS], tiles_per_expert[0, :N_EXPERTS], x_sorted, *expert_weights)

    y_tok = _gather_rows(y_sorted, pos.reshape(-1))
    out = _run_combine(h1, p2, route, y_tok, *combine_weights)
    return out.reshape(batch, seq, D_MODEL)
```

```python
import functools

import jax
import jax.numpy as jnp
from jax import lax
from jax.experimental import pallas as pl
from jax.experimental.pallas import tpu as pltpu
from jax.experimental.pallas import tpu_sc as plsc

D_MODEL = 1024
CONV_WIDTH = 512
CONV_K = 3
POOL_WIDTH = 512
POOL_WINDOWS = (2, 4, 8, 16)
POOL_GW = 128
IN_PROJ = 3 * CONV_WIDTH + POOL_WIDTH
N_GROUPS = 4
EXPERTS_PER_GROUP = 8
N_EXPERTS = N_GROUPS * EXPERTS_PER_GROUP
D_EXPERT = 256
PLE_DIM = 256
LN_EPS = 1e-5
DEEPNORM_ALPHA = 2.0 ** 0.25

LANES = 128
HALF = D_MODEL // 2
CONV_HALO = 8
POOL_HALO = 16
SEQ_TILE = 1024
SUB_TILE = 256
MIXER_PHASES = 8
COMBINE_PHASES = 3
ROW_TILE = 256
TILE_BUFFERS = 8
SC_WORKERS = 32
SC_WINDOW = 64
GATHER_WINDOW = 32
GATHER_BUFFERS = 4
WEIGHT_BUFFERS = 3
TOKEN_TILE = 1024
VMEM_LIMIT = 56 * 1024 * 1024

R_ID1, R_ID2, R_RANK1, R_RANK2, R_W1, R_W2 = range(6)
ROUTE_ROWS = 8
LOGIT_ROWS = 40


def _layernorm(x, g, b):
    mu = jnp.mean(x, axis=-1, keepdims=True)
    xc = x - mu
    var = jnp.mean(xc * xc, axis=-1, keepdims=True)
    return xc * lax.rsqrt(var + LN_EPS) * g + b


def _pack_bf16_pairs(v):
    bits = lax.bitcast_convert_type(v.astype(jnp.bfloat16).astype(jnp.float32), jnp.uint32)
    return bits[:, HALF:] | (bits[:, :HALF] >> 16)


def _unpack_bf16_pairs(w):
    lo = lax.bitcast_convert_type(w << 16, jnp.float32)
    hi = lax.bitcast_convert_type(w & jnp.uint32(0xFFFF0000), jnp.float32)
    return lo, hi


def _mixer_kernel(x_ref, lnin_g, lnin_b, w_in_f32, conv_w, conv_b, pool_w, pool_scale, w_out,
                  ln1_g, ln1_b, w_r, b_r,
                  xp_ref, h1_ref, route_ref, route_t_ref, counts_ref,
                  zbuf, vbuf, carry, w_in):
    b = pl.program_id(0)
    s = pl.program_id(1)
    ts = x_ref.shape[0]

    @pl.when(s == 0)
    def _():
        zbuf[0:CONV_HALO, :] = jnp.zeros((CONV_HALO, CONV_WIDTH), jnp.float32)
        vbuf[0:POOL_HALO, :] = jnp.zeros((POOL_HALO, POOL_WIDTH), jnp.float32)

    @pl.when((b == 0) & (s == 0))
    def _():
        carry[...] = jnp.zeros_like(carry)
        w_in[...] = w_in_f32[...].astype(jnp.bfloat16)

    chains = [_mixer_rows(r0, s * ts + r0, x_ref, lnin_g, lnin_b, w_in, conv_w, conv_b, pool_w, pool_scale,
                          w_out, ln1_g, ln1_b, w_r, b_r,
                          xp_ref, h1_ref, route_ref, route_t_ref, zbuf, vbuf, carry)
              for r0 in range(0, ts, SUB_TILE)]
    for t in range(MIXER_PHASES + len(chains) - 1):
        for k, chain in reversed(list(enumerate(chains))):
            if 0 <= t - k < MIXER_PHASES:
                next(chain, None)
    zbuf[0:CONV_HALO, :] = zbuf[ts:ts + CONV_HALO, :]
    vbuf[0:POOL_HALO, :] = vbuf[ts:ts + POOL_HALO, :]
    counts_ref[...] = jnp.transpose(carry[...])[0:1, :]


def _mixer_rows(r0, seq0, x_ref, lnin_g, lnin_b, w_in, conv_w, conv_b, pool_w, pool_scale, w_out,
                ln1_g, ln1_b, w_r, b_r,
                xp_ref, h1_ref, route_ref, route_t_ref, zbuf, vbuf, carry):
    n = SUB_TILE
    rows = pl.ds(r0, n)
    h0 = _layernorm(x_ref[rows, :], lnin_g[...], lnin_b[...])
    h0b = h0.astype(jnp.bfloat16)
    yield
    u_a = jnp.dot(h0b, w_in[:, 0:IN_PROJ // 2], preferred_element_type=jnp.float32)
    yield
    u_b = jnp.dot(h0b, w_in[:, IN_PROJ // 2:], preferred_element_type=jnp.float32)
    yield
    b_g = u_a[:, 0:CONV_WIDTH]
    c_g = u_a[:, CONV_WIDTH:]
    v_c = u_b[:, 0:CONV_WIDTH]
    v_p = u_b[:, CONV_WIDTH:]

    zbuf[pl.ds(CONV_HALO + r0, n), :] = c_g * v_c
    zext = zbuf[pl.ds(r0, n + CONV_HALO), :]
    z1 = pltpu.roll(zext, 1, axis=0)[CONV_HALO:, :]
    z2 = pltpu.roll(zext, 2, axis=0)[CONV_HALO:, :]
    conv = z2 * conv_w[0:1, :] + z1 * conv_w[1:2, :] + zext[CONV_HALO:, :] * conv_w[2:3, :] + conv_b[...]
    y_conv = b_g * conv

    vbuf[pl.ds(POOL_HALO + r0, n), :] = v_p
    vext = vbuf[pl.ds(r0, n + POOL_HALO), :]
    s2 = vext + pltpu.roll(vext, 1, axis=0)
    s4 = s2[:, POOL_GW:] + pltpu.roll(s2[:, POOL_GW:], 2, axis=0)
    s8 = s4[:, POOL_GW:] + pltpu.roll(s4[:, POOL_GW:], 4, axis=0)
    s16 = s8[:, POOL_GW:] + pltpu.roll(s8[:, POOL_GW:], 8, axis=0)
    wsums = (s2[POOL_HALO:, 0:POOL_GW], s4[POOL_HALO:, 0:POOL_GW],
             s8[POOL_HALO:, 0:POOL_GW], s16[POOL_HALO:, 0:POOL_GW])

    t_pos = seq0 + lax.broadcasted_iota(jnp.int32, (n, 1), 0)
    y_pool = []
    for j, w in enumerate(POOL_WINDOWS):
        inv_cnt = 1.0 / jnp.minimum(t_pos + 1, w).astype(jnp.float32)
        pooled = wsums[j] * inv_cnt - v_p[:, j * POOL_GW:(j + 1) * POOL_GW]
        y_pool.append(jnp.dot(pooled.astype(jnp.bfloat16), pool_w[j], preferred_element_type=jnp.float32))
    y_pool = jnp.concatenate(y_pool, axis=-1) * pool_scale[...]

    ycat = jnp.concatenate([y_conv, y_pool], axis=-1).astype(jnp.bfloat16)
    yield
    mix = jnp.dot(ycat, w_out[...], preferred_element_type=jnp.float32)
    yield
    h1 = _layernorm(DEEPNORM_ALPHA * h0 + mix, ln1_g[...], ln1_b[...])

    h_hi = h1.astype(jnp.bfloat16)
    h_lo = (h1 - h_hi.astype(jnp.float32)).astype(jnp.bfloat16)
    xp_ref[rows, :] = _pack_bf16_pairs(h1)
    h1_ref[rows, :] = h1
    hcat = jnp.concatenate([h_hi, h_lo, h_hi], axis=-1)
    yield
    logits = jnp.dot(hcat, w_r[...], preferred_element_type=jnp.float32) + b_r[...]
    yield
    lt = jnp.transpose(logits)[0:LOGIT_ROWS, :]
    rid = lax.broadcasted_iota(jnp.int32, (LOGIT_ROWS, n), 0).astype(jnp.float32)
    neg = jnp.float32(-jnp.inf)

    def first_argmax(vals):
        m = jnp.max(vals, axis=0, keepdims=True)
        idx = jnp.min(jnp.where(vals == m, rid, float(LOGIT_ROWS)), axis=0, keepdims=True)
        return m, idx

    g_mask = rid < N_GROUPS
    g_max, g_idx = first_argmax(jnp.where(g_mask, lt, neg))
    g_w = 1.0 / jnp.sum(jnp.where(g_mask, jnp.exp(lt - g_max), 0.0), axis=0, keepdims=True)

    e_lo = N_GROUPS + EXPERTS_PER_GROUP * g_idx
    e_vals = jnp.where((rid >= e_lo) & (rid < e_lo + EXPERTS_PER_GROUP), lt, neg)
    m1, i1 = first_argmax(e_vals)
    m2, i2 = first_argmax(jnp.where(rid == i1, neg, e_vals))
    e21 = jnp.exp(m2 - m1)
    w1 = g_w / (1.0 + e21)
    w2 = g_w * e21 / (1.0 + e21)
    id1 = i1 - N_GROUPS
    id2 = i2 - N_GROUPS

    eid = lax.broadcasted_iota(jnp.int32, (N_EXPERTS, n), 0).astype(jnp.float32)
    sel1 = eid == id1
    sel2 = eid == id2
    onehot = (sel1 | sel2).astype(jnp.float32)
    src = lax.broadcasted_iota(jnp.int32, (n, n), 0)
    dst = lax.broadcasted_iota(jnp.int32, (n, n), 1)
    earlier = (src < dst).astype(jnp.bfloat16)
    before = (jnp.dot(onehot.astype(jnp.bfloat16), earlier, preferred_element_type=jnp.float32)
              + carry[0:N_EXPERTS, 0:1])
    rank1 = jnp.sum(jnp.where(sel1, before, 0.0), axis=0, keepdims=True)
    rank2 = jnp.sum(jnp.where(sel2, before, 0.0), axis=0, keepdims=True)
    carry[0:N_EXPERTS, :] = carry[0:N_EXPERTS, :] + jnp.sum(onehot, axis=1, keepdims=True)

    rec_t = jnp.zeros((ROUTE_ROWS, n), jnp.float32)
    rec_row = lax.broadcasted_iota(jnp.int32, (ROUTE_ROWS, n), 0)
    for k, val in ((R_ID1, id1), (R_ID2, id2), (R_W1, w1), (R_W2, w2), (R_RANK1, rank1), (R_RANK2, rank2)):
        rec_t = jnp.where(rec_row == k, val, rec_t)
    route_t_ref[:, rows] = rec_t
    padded = jnp.concatenate([rec_t, jnp.zeros((LANES - ROUTE_ROWS, n), jnp.float32)], axis=0)
    route_ref[rows, :] = jnp.transpose(padded)


def _run_mixer(x2, lnin_g, lnin_b, w_in, conv_w, conv_b, pool_w, pool_scale, w_out, ln1_g, ln1_b,
               w_r, b_r, batch, seq):
    n_tok = batch * seq
    n_s = seq // SEQ_TILE
    tok_map = lambda b, s: (b * n_s + s, 0)

    def const(shape):
        return pl.BlockSpec(shape, lambda b, s: (0,) * len(shape), pipeline_mode=pl.Buffered(1))

    in_specs = [
        pl.BlockSpec((SEQ_TILE, D_MODEL), tok_map),
        const((1, D_MODEL)), const((1, D_MODEL)),
        const((D_MODEL, IN_PROJ)),
        const((CONV_K, CONV_WIDTH)), const((1, CONV_WIDTH)),
        const((len(POOL_WINDOWS), POOL_GW, POOL_GW)), const((1, POOL_WIDTH)),
        const((D_MODEL, D_MODEL)),
        const((1, D_MODEL)), const((1, D_MODEL)),
        const((3 * D_MODEL, LANES)), const((1, LANES)),
    ]
    out_specs = [
        pl.BlockSpec((SEQ_TILE, HALF), tok_map),
        pl.BlockSpec((SEQ_TILE, D_MODEL), tok_map),
        pl.BlockSpec((SEQ_TILE, LANES), tok_map),
        pl.BlockSpec((ROUTE_ROWS, SEQ_TILE), lambda b, s: (0, b * n_s + s)),
        pl.BlockSpec((1, LANES), lambda b, s: (0, 0)),
    ]
    out_shape = [
        jax.ShapeDtypeStruct((n_tok, HALF), jnp.uint32),
        jax.ShapeDtypeStruct((n_tok, D_MODEL), jnp.float32),
        jax.ShapeDtypeStruct((n_tok, LANES), jnp.float32),
        jax.ShapeDtypeStruct((ROUTE_ROWS, n_tok), jnp.float32),
        jax.ShapeDtypeStruct((1, LANES), jnp.float32),
    ]
    return pl.pallas_call(
        _mixer_kernel,
        grid=(batch, n_s),
        in_specs=in_specs,
        out_specs=out_specs,
        out_shape=out_shape,
        scratch_shapes=[
            pltpu.VMEM((SEQ_TILE + CONV_HALO, CONV_WIDTH), jnp.float32),
            pltpu.VMEM((SEQ_TILE + POOL_HALO, POOL_WIDTH), jnp.float32),
            pltpu.VMEM((LANES, LANES), jnp.float32),
            pltpu.VMEM((D_MODEL, IN_PROJ), jnp.bfloat16),
        ],
        compiler_params=pltpu.CompilerParams(
            dimension_semantics=("arbitrary", "arbitrary"), vmem_limit_bytes=VMEM_LIMIT),
    )(x2, lnin_g, lnin_b, w_in, conv_w, conv_b, pool_w, pool_scale, w_out, ln1_g, ln1_b, w_r, b_r)


def _plan_kernel(rt_ref, counts_ref, pos_ref, tile_start_ref, tiles_ref, pad_ref, *, n_rows):
    lane = lax.broadcasted_iota(jnp.int32, (ROUTE_ROWS, LANES), 1)
    counts = jnp.broadcast_to(counts_ref[...], (ROUTE_ROWS, LANES))
    tiles = jnp.floor((counts + (ROW_TILE - 1)) * (1.0 / ROW_TILE))
    tile_end = tiles
    shift = 1
    while shift < N_EXPERTS:
        tile_end = tile_end + jnp.where(lane >= shift, pltpu.roll(tile_end, shift, axis=1), 0.0)
        shift *= 2
    row_start = (tile_end - tiles) * ROW_TILE
    tile_start_ref[...] = (tile_end - tiles)[0:1, :].astype(jnp.int32)
    tiles_ref[...] = tiles[0:1, :].astype(jnp.int32)

    rt = rt_ref[...]
    ids = rt[R_ID1:R_ID2 + 1, :]
    start = jnp.zeros_like(ids)
    for e in range(N_EXPERTS):
        start = jnp.where(ids == e, row_start[0:1, e:e + 1], start)
    pos_ref[...] = (start + rt[R_RANK1:R_RANK2 + 1, :]).astype(jnp.int32)

    sub = lax.broadcasted_iota(jnp.int32, (N_EXPERTS, LANES), 0)
    lane_e = lax.broadcasted_iota(jnp.int32, (N_EXPERTS, LANES), 1)
    diag = sub == lane_e
    pad_lo = jnp.sum(jnp.where(diag, (row_start + counts)[0:1, :], 0.0), axis=1, keepdims=True)
    pad_n = jnp.sum(jnp.where(diag, (tiles * ROW_TILE - counts)[0:1, :], 0.0), axis=1, keepdims=True)
    j = lax.broadcasted_iota(jnp.int32, (N_EXPERTS, ROW_TILE), 1).astype(jnp.float32)
    pad_ref[...] = jnp.where(j < pad_n, pad_lo + j, n_rows + j).astype(jnp.int32)


def _run_plan(route_t, counts, n_rows):
    n_tok = route_t.shape[1]
    full = lambda shape: pl.BlockSpec(shape, lambda i: (0, 0))
    return pl.pallas_call(
        functools.partial(_plan_kernel, n_rows=n_rows),
        grid=(1,),
        in_specs=[full((ROUTE_ROWS, n_tok)), full((1, LANES))],
        out_specs=[full((2, n_tok)), full((1, LANES)), full((1, LANES)), full((N_EXPERTS, ROW_TILE))],
        out_shape=[
            jax.ShapeDtypeStruct((2, n_tok), jnp.int32),
            jax.ShapeDtypeStruct((1, LANES), jnp.int32),
            jax.ShapeDtypeStruct((1, LANES), jnp.int32),
            jax.ShapeDtypeStruct((N_EXPERTS, ROW_TILE), jnp.int32),
        ],
        compiler_params=pltpu.CompilerParams(dimension_semantics=("arbitrary",)),
    )(route_t, counts)


def _sc_mesh():
    return plsc.VectorSubcoreMesh(core_axis_name="core", subcore_axis_name="subcore")


def _sc_worker_id():
    return lax.axis_index("core") * (SC_WORKERS // 2) + lax.axis_index("subcore")


def _dispatch_rows(xp, pos1, pos2, pad_pos, zero_rows, n_rows):
    n_tok, width = xp.shape
    n_win = n_tok // SC_WORKERS // SC_WINDOW
    n_pad = pad_pos.shape[0] // SC_WORKERS // SC_WINDOW
    as_windows = lambda v: v.reshape(-1, SC_WINDOW)

    @functools.partial(
        pl.kernel, out_type=jax.ShapeDtypeStruct((n_rows + ROW_TILE, width), xp.dtype), mesh=_sc_mesh(),
        scratch_types=[pltpu.VMEM((n_win, SC_WINDOW), jnp.int32), pltpu.VMEM((n_win, SC_WINDOW), jnp.int32),
                       pltpu.VMEM((n_pad, SC_WINDOW), jnp.int32),
                       pltpu.VMEM((2, SC_WINDOW, width), xp.dtype), pltpu.VMEM((SC_WINDOW, width), xp.dtype),
                       pltpu.SemaphoreType.DMA((2,)), pltpu.SemaphoreType.DMA((2,)), pltpu.SemaphoreType.DMA])
    def dispatch(xp_hbm, pos1_hbm, pos2_hbm, pad_hbm, zero_hbm, out_hbm,
                 idx1, idx2, idxp, buf, zbuf, lsem, ssem, psem):
        wid = _sc_worker_id()
        pltpu.sync_copy(pad_hbm.at[pl.ds(wid * n_pad, n_pad)], idxp)
        pltpu.sync_copy(zero_hbm, zbuf)
        pads = [pltpu.make_async_copy(zbuf, out_hbm.at[idxp.at[j]], psem) for j in range(n_pad)]
        for cp in pads:
            cp.start()
        pltpu.sync_copy(pos1_hbm.at[pl.ds(wid * n_win, n_win)], idx1)
        pltpu.sync_copy(pos2_hbm.at[pl.ds(wid * n_win, n_win)], idx2)

        def load(j):
            rows = xp_hbm.at[pl.ds((wid * n_win + j) * SC_WINDOW, SC_WINDOW)]
            return pltpu.make_async_copy(rows, buf.at[j % 2], lsem.at[j % 2])

        def scatters(j):
            return [pltpu.make_async_copy(buf.at[j % 2], out_hbm.at[idx.at[j]], ssem.at[j % 2])
                    for idx in (idx1, idx2)]

        load(0).start()
        for j in range(n_win):
            load(j).wait()
            for cp in scatters(j):
                cp.start()
            if j >= 1:
                for cp in scatters(j - 1):
                    cp.wait()
            if j + 1 < n_win:
                load(j + 1).start()
        for cp in scatters(n_win - 1):
            cp.wait()
        for cp in pads:
            cp.wait()

    return dispatch(xp, as_windows(pos1), as_windows(pos2), as_windows(pad_pos), zero_rows)


def _gather_rows(src, idx):
    n_out, width = idx.shape[0], src.shape[1]
    n_win = n_out // SC_WORKERS // GATHER_WINDOW
    nbuf = GATHER_BUFFERS

    @functools.partial(
        pl.kernel, out_type=jax.ShapeDtypeStruct((n_out, width), src.dtype), mesh=_sc_mesh(),
        scratch_types=[pltpu.VMEM((n_win, GATHER_WINDOW), jnp.int32),
                       pltpu.VMEM((nbuf, GATHER_WINDOW, width), src.dtype),
                       pltpu.SemaphoreType.DMA((nbuf,)), pltpu.SemaphoreType.DMA((nbuf,))])
    def gather(src_hbm, idx_hbm, dst_hbm, idx_v, buf, gsem, ssem):
        wid = _sc_worker_id()
        pltpu.sync_copy(idx_hbm.at[pl.ds(wid * n_win, n_win)], idx_v)

        def fetch(j):
            return pltpu.make_async_copy(src_hbm.at[idx_v.at[j]], buf.at[j % nbuf], gsem.at[j % nbuf])

        def store(j):
            rows = dst_hbm.at[pl.ds((wid * n_win + j) * GATHER_WINDOW, GATHER_WINDOW)]
            return pltpu.make_async_copy(buf.at[j % nbuf], rows, ssem.at[j % nbuf])

        for j in range(min(nbuf - 1, n_win)):
            fetch(j).start()
        for j in range(n_win):
            fetch(j).wait()
            store(j).start()
            if j + nbuf - 1 < n_win:
                if j >= 1:
                    store(j - 1).wait()
                fetch(j + nbuf - 1).start()
        for j in range(max(0, n_win - nbuf), n_win):
            store(j).wait()

    return gather(src, idx.reshape(-1, GATHER_WINDOW))


def _tile_copy(hbm, buf, sem, tile, slot, to_hbm):
    rows = hbm.at[pl.ds(pl.multiple_of(tile * ROW_TILE, ROW_TILE), ROW_TILE)]
    if to_hbm:
        return pltpu.make_async_copy(buf.at[slot], rows, sem.at[slot])
    return pltpu.make_async_copy(rows, buf.at[slot], sem.at[slot])


def _weight_copies(w_hbm, wbuf, wsem, expert):
    slot = expert % WEIGHT_BUFFERS
    return [pltpu.make_async_copy(w.at[expert], buf.at[slot], wsem.at[slot]) for w, buf in zip(w_hbm, wbuf)]


def _expert_kernel(ts_ref, nte_ref, x_hbm, wg_hbm, wu_hbm, wd_hbm, y_hbm,
                   xbuf, ybuf, xsem, ysem, wg_buf, wu_buf, wd_buf, wsem, wgu_bf, wd_bf, act_ref, *, max_tiles):
    e = pl.program_id(0)
    first = ts_ref[e]
    count = nte_ref[e]
    n_tiles = ts_ref[N_EXPERTS - 1] + nte_ref[N_EXPERTS - 1]
    ahead = TILE_BUFFERS - 2
    w_hbm = (wg_hbm, wu_hbm, wd_hbm)
    wbuf = (wg_buf, wu_buf, wd_buf)

    @pl.when(e == 0)
    def _():
        for g in range(ahead):
            @pl.when(g < n_tiles)
            def _():
                _tile_copy(x_hbm, xbuf, xsem, g, g, False).start()
        for k in range(WEIGHT_BUFFERS - 1):
            for cp in _weight_copies(w_hbm, wbuf, wsem, k):
                cp.start()

    @pl.when(e + WEIGHT_BUFFERS - 1 < N_EXPERTS)
    def _():
        for cp in _weight_copies(w_hbm, wbuf, wsem, e + WEIGHT_BUFFERS - 1):
            cp.start()

    for cp in _weight_copies(w_hbm, wbuf, wsem, e):
        cp.wait()

    def sync_x(g):
        _tile_copy(x_hbm, xbuf, xsem, g, g % TILE_BUFFERS, False).wait()

        @pl.when(g + ahead < n_tiles)
        def _():
            _tile_copy(x_hbm, xbuf, xsem, g + ahead, (g + ahead) % TILE_BUFFERS, False).start()

    def sync_y_slot(g):
        @pl.when(g >= TILE_BUFFERS)
        def _():
            _tile_copy(y_hbm, ybuf, ysem, g - TILE_BUFFERS, g % TILE_BUFFERS, True).wait()

    def up_proj(g):
        lo, hi = _unpack_bf16_pairs(xbuf[g % TILE_BUFFERS])
        return (jnp.dot(lo.astype(jnp.bfloat16), wgu_bf[0:HALF, :], preferred_element_type=jnp.float32)
                + jnp.dot(hi.astype(jnp.bfloat16), wgu_bf[HALF:, :], preferred_element_type=jnp.float32))

    def put_act(hgu, which):
        hg = hgu[:, 0:D_EXPERT]
        act_ref[which] = (hg * jax.nn.sigmoid(hg) * hgu[:, D_EXPERT:]).astype(jnp.bfloat16)

    def down_proj(g, which):
        y = jnp.dot(act_ref[which], wd_bf[...], preferred_element_type=jnp.float32)
        ybuf[g % TILE_BUFFERS] = _pack_bf16_pairs(y)

    def start_y(g):
        _tile_copy(y_hbm, ybuf, ysem, g, g % TILE_BUFFERS, True).start()

    @pl.when(count > 0)
    def _():
        slot = e % WEIGHT_BUFFERS
        wgu_bf[:, 0:D_EXPERT] = wg_buf[slot].astype(jnp.bfloat16)
        wgu_bf[:, D_EXPERT:] = wu_buf[slot].astype(jnp.bfloat16)
        wd_bf[...] = wd_buf[slot].astype(jnp.bfloat16)

        sync_x(first)
        put_act(up_proj(first), 0)

        def pair_body(i, c):
            g = first + 1 + 2 * i
            sync_x(g)
            sync_x(g + 1)
            sync_y_slot(g - 1)
            sync_y_slot(g)
            down_proj(g - 1, 0)
            hgu0 = up_proj(g)
            put_act(hgu0, 1)
            hgu1 = up_proj(g + 1)
            down_proj(g, 1)
            put_act(hgu1, 0)
            start_y(g - 1)
            start_y(g)
            return c

        n_pairs = (count - 1) // 2
        lax.fori_loop(0, n_pairs, pair_body, 0)

        @pl.when((count - 1) % 2 == 1)
        def _():
            g = first + count - 1
            sync_x(g)
            sync_y_slot(g - 1)
            down_proj(g - 1, 0)
            put_act(up_proj(g), 0)
            start_y(g - 1)

        last = first + count - 1
        sync_y_slot(last)
        down_proj(last, 0)
        start_y(last)

    @pl.when(e == N_EXPERTS - 1)
    def _():
        for k in range(TILE_BUFFERS, 0, -1):
            @pl.when(n_tiles >= k)
            def _():
                _tile_copy(y_hbm, ybuf, ysem, n_tiles - k, (n_tiles - k) % TILE_BUFFERS, True).wait()

        ybuf[0] = jnp.zeros((ROW_TILE, HALF), jnp.uint32)

        def fill(g, c):
            cp = _tile_copy(y_hbm, ybuf, ysem, g, 0, True)
            cp.start()
            cp.wait()
            return c

        lax.fori_loop(n_tiles, max_tiles, fill, 0)


def _run_experts(tile_start, tiles_per_expert, x_sorted, w_gate, w_up, w_down):
    n_rows = x_sorted.shape[0]
    max_tiles = n_rows // ROW_TILE
    hbm = pl.BlockSpec(memory_space=pl.ANY)

    grid_spec = pltpu.PrefetchScalarGridSpec(
        num_scalar_prefetch=2,
        grid=(N_EXPERTS,),
        in_specs=[hbm, hbm, hbm, hbm],
        out_specs=hbm,
        scratch_shapes=[
            pltpu.VMEM((TILE_BUFFERS, ROW_TILE, HALF), jnp.uint32),
            pltpu.VMEM((TILE_BUFFERS, ROW_TILE, HALF), jnp.uint32),
            pltpu.SemaphoreType.DMA((TILE_BUFFERS,)),
            pltpu.SemaphoreType.DMA((TILE_BUFFERS,)),
            pltpu.VMEM((WEIGHT_BUFFERS, D_MODEL, D_EXPERT), jnp.float32),
            pltpu.VMEM((WEIGHT_BUFFERS, D_MODEL, D_EXPERT), jnp.float32),
            pltpu.VMEM((WEIGHT_BUFFERS, D_EXPERT, D_MODEL), jnp.float32),
            pltpu.SemaphoreType.DMA((WEIGHT_BUFFERS,)),
            pltpu.VMEM((D_MODEL, 2 * D_EXPERT), jnp.bfloat16),
            pltpu.VMEM((D_EXPERT, D_MODEL), jnp.bfloat16),
            pltpu.VMEM((2, ROW_TILE, D_EXPERT), jnp.bfloat16),
        ],
    )
    return pl.pallas_call(
        functools.partial(_expert_kernel, max_tiles=max_tiles),
        grid_spec=grid_spec,
        out_shape=jax.ShapeDtypeStruct((n_rows, HALF), jnp.uint32),
        compiler_params=pltpu.CompilerParams(
            dimension_semantics=("arbitrary",), vmem_limit_bytes=VMEM_LIMIT),
    )(tile_start, tiles_per_expert, x_sorted, w_gate, w_up, w_down)


def _combine_rows(r0, h1_ref, p_ref, route_ref, ya_ref, yb_ref, w_pg, b_pg, w_ple, g_ref, b_ref, o_ref):
    rows = pl.ds(r0, SUB_TILE)
    h1 = h1_ref[rows, :]
    h_hi = h1.astype(jnp.bfloat16)
    p_b = p_ref[rows, :].astype(jnp.bfloat16)
    yield
    gate_pre = jnp.dot(h_hi, w_pg[...], preferred_element_type=jnp.float32)
    ple_pre = jnp.dot(p_b, w_ple[...], preferred_element_type=jnp.float32)
    yield
    ple = ple_pre * jax.nn.sigmoid(gate_pre + b_pg[...])
    route = route_ref[rows, :]
    w1 = route[:, R_W1:R_W1 + 1]
    w2 = route[:, R_W2:R_W2 + 1]
    a_lo, a_hi = _unpack_bf16_pairs(ya_ref[rows, :])
    b_lo, b_hi = _unpack_bf16_pairs(yb_ref[rows, :])
    moe = jnp.concatenate([w1 * a_lo + w2 * b_lo, w1 * a_hi + w2 * b_hi], axis=-1)
    o_ref[rows, :] = _layernorm(DEEPNORM_ALPHA * h1 + ple + moe, g_ref[...], b_ref[...])


def _combine_kernel(h1_ref, p_ref, route_ref, ya_ref, yb_ref, w_pg, b_pg, w_ple, g_ref, b_ref, o_ref):
    chains = [_combine_rows(r0, h1_ref, p_ref, route_ref, ya_ref, yb_ref, w_pg, b_pg, w_ple, g_ref, b_ref, o_ref)
              for r0 in range(0, h1_ref.shape[0], SUB_TILE)]
    for t in range(COMBINE_PHASES + len(chains) - 1):
        for k, chain in reversed(list(enumerate(chains))):
            if 0 <= t - k < COMBINE_PHASES:
                next(chain, None)


def _run_combine(h1, p2, route, y_tok, w_pg, b_pg, w_ple, ln2_g, ln2_b):
    n_tok = h1.shape[0]
    n_t = n_tok // TOKEN_TILE
    tok_map = lambda i: (i, 0)
    const = lambda shape: pl.BlockSpec(shape, lambda i: (0, 0), pipeline_mode=pl.Buffered(1))
    in_specs = [
        pl.BlockSpec((TOKEN_TILE, D_MODEL), tok_map),
        pl.BlockSpec((TOKEN_TILE, PLE_DIM), tok_map),
        pl.BlockSpec((TOKEN_TILE, LANES), tok_map),
        pl.BlockSpec((TOKEN_TILE, HALF), lambda i: (i, 0)),
        pl.BlockSpec((TOKEN_TILE, HALF), lambda i: (i + n_t, 0)),
        const((D_MODEL, D_MODEL)), const((1, D_MODEL)), const((PLE_DIM, D_MODEL)),
        const((1, D_MODEL)), const((1, D_MODEL)),
    ]
    return pl.pallas_call(
        _combine_kernel,
        grid=(n_t,),
        in_specs=in_specs,
        out_specs=pl.BlockSpec((TOKEN_TILE, D_MODEL), tok_map),
        out_shape=jax.ShapeDtypeStruct((n_tok, D_MODEL), jnp.float32),
        compiler_params=pltpu.CompilerParams(dimension_semantics=("arbitrary",), vmem_limit_bytes=VMEM_LIMIT),
    )(h1, p2, route, y_tok, y_tok, w_pg, b_pg, w_ple, ln2_g, ln2_b)


def _split_bf16(w):
    hi = w.astype(jnp.bfloat16)
    lo = (w - hi.astype(jnp.float32)).astype(jnp.bfloat16)
    return hi, lo


def kernel(x, p, ln_in_g, ln_in_b, w_in, conv_w, conv_b, pool_w, pool_scale, w_out, ln1_g, ln1_b,
           w_rg, b_rg, w_re, b_re, w_gate, w_up, w_down, w_pg, b_pg, w_ple, ln2_g, ln2_b):
    batch, seq, _ = x.shape
    n_tok = batch * seq
    bf = jnp.bfloat16
    row = lambda v: v.reshape(1, -1)

    w_r = jnp.concatenate([w_rg[0], jnp.transpose(w_re[0], (1, 0, 2)).reshape(D_MODEL, N_EXPERTS)], axis=1)
    w_r = jnp.pad(w_r, ((0, 0), (0, LANES - w_r.shape[1])))
    w_r_hi, w_r_lo = _split_bf16(w_r)
    w_r_cat = jnp.concatenate([w_r_hi, w_r_hi, w_r_lo], axis=0)
    b_r = jnp.pad(jnp.concatenate([b_rg[0], b_re[0].reshape(-1)]), (0, LANES - N_GROUPS - N_EXPERTS)).reshape(1, LANES)

    x2 = x.reshape(n_tok, D_MODEL)
    p2 = p[0].reshape(n_tok, PLE_DIM)
    mixer_weights = (row(ln_in_g), row(ln_in_b), w_in[0], conv_w[0], row(conv_b[0]),
                     pool_w[0].astype(bf), row(pool_scale[0]), w_out[0].astype(bf), row(ln1_g[0]), row(ln1_b[0]),
                     w_r_cat, b_r)
    combine_weights = (w_pg[0].astype(bf), row(b_pg[0]), w_ple[0].astype(bf), row(ln2_g[0]), row(ln2_b[0]))
    expert_weights = (w_gate[0].reshape(N_EXPERTS, D_MODEL, D_EXPERT),
                      w_up[0].reshape(N_EXPERTS, D_MODEL, D_EXPERT),
                      w_down[0].reshape(N_EXPERTS, D_EXPERT, D_MODEL))
    zero_rows = jnp.zeros((SC_WINDOW, HALF), jnp.uint32)

    n_rows = -(-(2 * n_tok + N_EXPERTS * (ROW_TILE - 1)) // ROW_TILE) * ROW_TILE
    xp, h1, route, route_t, counts = _run_mixer(x2, *mixer_weights, batch, seq)
    pos, tile_start, tiles_per_expert, pad_pos = _run_plan(route_t, counts, n_rows)
    x_sorted = _dispatch_rows(xp, pos[0], pos[1], pad_pos.reshape(-1), zero_rows, n_rows)
    y_sorted = _run_experts(tile_start[0, :N_EXPERTS], tiles_per_expert[0, :N_EXPERTS], x_sorted, *expert_weights)

    y_tok = _gather_rows(y_sorted, pos.reshape(-1))
    out = _run_combine(h1, p2, route, y_tok, *combine_weights)
    return out.reshape(batch, seq, D_MODEL)
```

```python
import functools

import jax
import jax.numpy as jnp
from jax import lax
from jax.experimental import pallas as pl
from jax.experimental.pallas import tpu as pltpu
from jax.experimental.pallas import tpu_sc as plsc

D_MODEL = 1024
CONV_WIDTH = 512
CONV_K = 3
POOL_WIDTH = 512
POOL_WINDOWS = (2, 4, 8, 16)
POOL_GW = 128
IN_PROJ = 3 * CONV_WIDTH + POOL_WIDTH
N_GROUPS = 4
EXPERTS_PER_GROUP = 8
N_EXPERTS = N_GROUPS * EXPERTS_PER_GROUP
D_EXPERT = 256
PLE_DIM = 256
LN_EPS = 1e-5
DEEPNORM_ALPHA = 2.0 ** 0.25

LANES = 128
HALF = D_MODEL // 2
CONV_HALO = 8
POOL_HALO = 16
SEQ_TILE = 1024
SUB_TILE = 256
MIXER_PHASES = 8
COMBINE_PHASES = 3
ROW_TILE = 256
TILE_BUFFERS = 8
SC_WORKERS = 32
PAD_WINDOW = 64
MOVE_WINDOW = 32
MOVE_BUFFERS = 4
WEIGHT_BUFFERS = 3
TOKEN_TILE = 1024
VMEM_LIMIT = 56 * 1024 * 1024

R_ID1, R_ID2, R_RANK1, R_RANK2, R_W1, R_W2 = range(6)
ROUTE_ROWS = 8
LOGIT_ROWS = 40


def _layernorm(x, g, b):
    mu = jnp.mean(x, axis=-1, keepdims=True)
    xc = x - mu
    var = jnp.mean(xc * xc, axis=-1, keepdims=True)
    return xc * lax.rsqrt(var + LN_EPS) * g + b


def _pack_bf16_pairs(v):
    bits = lax.bitcast_convert_type(v.astype(jnp.bfloat16).astype(jnp.float32), jnp.uint32)
    return bits[:, HALF:] | (bits[:, :HALF] >> 16)


def _unpack_bf16_pairs(w):
    lo = lax.bitcast_convert_type(w << 16, jnp.float32)
    hi = lax.bitcast_convert_type(w & jnp.uint32(0xFFFF0000), jnp.float32)
    return lo, hi


def _mixer_kernel(x_ref, lnin_g, lnin_b, w_in_f32, conv_w, conv_b, pool_w, pool_scale, w_out,
                  ln1_g, ln1_b, w_r, b_r,
                  xp_ref, h1_ref, route_ref, route_t_ref, counts_ref,
                  zbuf, vbuf, carry, w_in):
    b = pl.program_id(0)
    s = pl.program_id(1)
    ts = x_ref.shape[0]

    @pl.when(s == 0)
    def _():
        zbuf[0:CONV_HALO, :] = jnp.zeros((CONV_HALO, CONV_WIDTH), jnp.float32)
        vbuf[0:POOL_HALO, :] = jnp.zeros((POOL_HALO, POOL_WIDTH), jnp.float32)

    @pl.when((b == 0) & (s == 0))
    def _():
        carry[...] = jnp.zeros_like(carry)
        w_in[...] = w_in_f32[...].astype(jnp.bfloat16)

    chains = [_mixer_rows(r0, s * ts + r0, x_ref, lnin_g, lnin_b, w_in, conv_w, conv_b, pool_w, pool_scale,
                          w_out, ln1_g, ln1_b, w_r, b_r,
                          xp_ref, h1_ref, route_ref, route_t_ref, zbuf, vbuf, carry)
              for r0 in range(0, ts, SUB_TILE)]
    for t in range(MIXER_PHASES + len(chains) - 1):
        for k, chain in reversed(list(enumerate(chains))):
            if 0 <= t - k < MIXER_PHASES:
                next(chain, None)
    zbuf[0:CONV_HALO, :] = zbuf[ts:ts + CONV_HALO, :]
    vbuf[0:POOL_HALO, :] = vbuf[ts:ts + POOL_HALO, :]
    counts_ref[...] = jnp.transpose(carry[...])[0:1, :]


def _mixer_rows(r0, seq0, x_ref, lnin_g, lnin_b, w_in, conv_w, conv_b, pool_w, pool_scale, w_out,
                ln1_g, ln1_b, w_r, b_r,
                xp_ref, h1_ref, route_ref, route_t_ref, zbuf, vbuf, carry):
    n = SUB_TILE
    rows = pl.ds(r0, n)
    h0 = _layernorm(x_ref[rows, :], lnin_g[...], lnin_b[...])
    h0b = h0.astype(jnp.bfloat16)
    yield
    u_a = jnp.dot(h0b, w_in[:, 0:IN_PROJ // 2], preferred_element_type=jnp.float32)
    yield
    u_b = jnp.dot(h0b, w_in[:, IN_PROJ // 2:], preferred_element_type=jnp.float32)
    yield
    b_g = u_a[:, 0:CONV_WIDTH]
    c_g = u_a[:, CONV_WIDTH:]
    v_c = u_b[:, 0:CONV_WIDTH]
    v_p = u_b[:, CONV_WIDTH:]

    zbuf[pl.ds(CONV_HALO + r0, n), :] = c_g * v_c
    zext = zbuf[pl.ds(r0, n + CONV_HALO), :]
    z1 = pltpu.roll(zext, 1, axis=0)[CONV_HALO:, :]
    z2 = pltpu.roll(zext, 2, axis=0)[CONV_HALO:, :]
    conv = z2 * conv_w[0:1, :] + z1 * conv_w[1:2, :] + zext[CONV_HALO:, :] * conv_w[2:3, :] + conv_b[...]
    y_conv = b_g * conv

    vbuf[pl.ds(POOL_HALO + r0, n), :] = v_p
    vext = vbuf[pl.ds(r0, n + POOL_HALO), :]
    s2 = vext + pltpu.roll(vext, 1, axis=0)
    s4 = s2[:, POOL_GW:] + pltpu.roll(s2[:, POOL_GW:], 2, axis=0)
    s8 = s4[:, POOL_GW:] + pltpu.roll(s4[:, POOL_GW:], 4, axis=0)
    s16 = s8[:, POOL_GW:] + pltpu.roll(s8[:, POOL_GW:], 8, axis=0)
    wsums = (s2[POOL_HALO:, 0:POOL_GW], s4[POOL_HALO:, 0:POOL_GW],
             s8[POOL_HALO:, 0:POOL_GW], s16[POOL_HALO:, 0:POOL_GW])

    t_pos = seq0 + lax.broadcasted_iota(jnp.int32, (n, 1), 0)
    y_pool = []
    for j, w in enumerate(POOL_WINDOWS):
        inv_cnt = 1.0 / jnp.minimum(t_pos + 1, w).astype(jnp.float32)
        pooled = wsums[j] * inv_cnt - v_p[:, j * POOL_GW:(j + 1) * POOL_GW]
        y_pool.append(jnp.dot(pooled.astype(jnp.bfloat16), pool_w[j], preferred_element_type=jnp.float32))
    y_pool = jnp.concatenate(y_pool, axis=-1) * pool_scale[...]

    ycat = jnp.concatenate([y_conv, y_pool], axis=-1).astype(jnp.bfloat16)
    yield
    mix = jnp.dot(ycat, w_out[...], preferred_element_type=jnp.float32)
    yield
    h1 = _layernorm(DEEPNORM_ALPHA * h0 + mix, ln1_g[...], ln1_b[...])

    h_hi = h1.astype(jnp.bfloat16)
    h_lo = (h1 - h_hi.astype(jnp.float32)).astype(jnp.bfloat16)
    xp_ref[rows, :] = _pack_bf16_pairs(h1)
    h1_ref[rows, :] = h1
    hcat = jnp.concatenate([h_hi, h_lo, h_hi], axis=-1)
    yield
    logits = jnp.dot(hcat, w_r[...], preferred_element_type=jnp.float32) + b_r[...]
    yield
    lt = jnp.transpose(logits)[0:LOGIT_ROWS, :]
    rid = lax.broadcasted_iota(jnp.int32, (LOGIT_ROWS, n), 0).astype(jnp.float32)
    neg = jnp.float32(-jnp.inf)

    def first_argmax(vals):
        m = jnp.max(vals, axis=0, keepdims=True)
        idx = jnp.min(jnp.where(vals == m, rid, float(LOGIT_ROWS)), axis=0, keepdims=True)
        return m, idx

    g_mask = rid < N_GROUPS
    g_max, g_idx = first_argmax(jnp.where(g_mask, lt, neg))
    g_w = 1.0 / jnp.sum(jnp.where(g_mask, jnp.exp(lt - g_max), 0.0), axis=0, keepdims=True)

    e_lo = N_GROUPS + EXPERTS_PER_GROUP * g_idx
    e_vals = jnp.where((rid >= e_lo) & (rid < e_lo + EXPERTS_PER_GROUP), lt, neg)
    m1, i1 = first_argmax(e_vals)
    m2, i2 = first_argmax(jnp.where(rid == i1, neg, e_vals))
    e21 = jnp.exp(m2 - m1)
    w1 = g_w / (1.0 + e21)
    w2 = g_w * e21 / (1.0 + e21)
    id1 = i1 - N_GROUPS
    id2 = i2 - N_GROUPS

    eid = lax.broadcasted_iota(jnp.int32, (N_EXPERTS, n), 0).astype(jnp.float32)
    sel1 = eid == id1
    sel2 = eid == id2
    onehot = (sel1 | sel2).astype(jnp.float32)
    src = lax.broadcasted_iota(jnp.int32, (n, n), 0)
    dst = lax.broadcasted_iota(jnp.int32, (n, n), 1)
    earlier = (src < dst).astype(jnp.bfloat16)
    before = (jnp.dot(onehot.astype(jnp.bfloat16), earlier, preferred_element_type=jnp.float32)
              + carry[0:N_EXPERTS, 0:1])
    rank1 = jnp.sum(jnp.where(sel1, before, 0.0), axis=0, keepdims=True)
    rank2 = jnp.sum(jnp.where(sel2, before, 0.0), axis=0, keepdims=True)
    carry[0:N_EXPERTS, :] = carry[0:N_EXPERTS, :] + jnp.sum(onehot, axis=1, keepdims=True)

    rec_t = jnp.zeros((ROUTE_ROWS, n), jnp.float32)
    rec_row = lax.broadcasted_iota(jnp.int32, (ROUTE_ROWS, n), 0)
    for k, val in ((R_ID1, id1), (R_ID2, id2), (R_W1, w1), (R_W2, w2), (R_RANK1, rank1), (R_RANK2, rank2)):
        rec_t = jnp.where(rec_row == k, val, rec_t)
    route_t_ref[:, rows] = rec_t
    padded = jnp.concatenate([rec_t, jnp.zeros((LANES - ROUTE_ROWS, n), jnp.float32)], axis=0)
    route_ref[rows, :] = jnp.transpose(padded)


def _run_mixer(x2, lnin_g, lnin_b, w_in, conv_w, conv_b, pool_w, pool_scale, w_out, ln1_g, ln1_b,
               w_r, b_r, batch, seq):
    n_tok = batch * seq
    n_s = seq // SEQ_TILE
    tok_map = lambda b, s: (b * n_s + s, 0)

    def const(shape):
        return pl.BlockSpec(shape, lambda b, s: (0,) * len(shape), pipeline_mode=pl.Buffered(1))

    in_specs = [
        pl.BlockSpec((SEQ_TILE, D_MODEL), tok_map),
        const((1, D_MODEL)), const((1, D_MODEL)),
        const((D_MODEL, IN_PROJ)),
        const((CONV_K, CONV_WIDTH)), const((1, CONV_WIDTH)),
        const((len(POOL_WINDOWS), POOL_GW, POOL_GW)), const((1, POOL_WIDTH)),
        const((D_MODEL, D_MODEL)),
        const((1, D_MODEL)), const((1, D_MODEL)),
        const((3 * D_MODEL, LANES)), const((1, LANES)),
    ]
    out_specs = [
        pl.BlockSpec((SEQ_TILE, HALF), tok_map),
        pl.BlockSpec((SEQ_TILE, D_MODEL), tok_map),
        pl.BlockSpec((SEQ_TILE, LANES), tok_map),
        pl.BlockSpec((ROUTE_ROWS, SEQ_TILE), lambda b, s: (0, b * n_s + s)),
        pl.BlockSpec((1, LANES), lambda b, s: (0, 0)),
    ]
    out_shape = [
        jax.ShapeDtypeStruct((n_tok, HALF), jnp.uint32),
        jax.ShapeDtypeStruct((n_tok, D_MODEL), jnp.float32),
        jax.ShapeDtypeStruct((n_tok, LANES), jnp.float32),
        jax.ShapeDtypeStruct((ROUTE_ROWS, n_tok), jnp.float32),
        jax.ShapeDtypeStruct((1, LANES), jnp.float32),
    ]
    return pl.pallas_call(
        _mixer_kernel,
        grid=(batch, n_s),
        in_specs=in_specs,
        out_specs=out_specs,
        out_shape=out_shape,
        scratch_shapes=[
            pltpu.VMEM((SEQ_TILE + CONV_HALO, CONV_WIDTH), jnp.float32),
            pltpu.VMEM((SEQ_TILE + POOL_HALO, POOL_WIDTH), jnp.float32),
            pltpu.VMEM((LANES, LANES), jnp.float32),
            pltpu.VMEM((D_MODEL, IN_PROJ), jnp.bfloat16),
        ],
        compiler_params=pltpu.CompilerParams(
            dimension_semantics=("arbitrary", "arbitrary"), vmem_limit_bytes=VMEM_LIMIT),
    )(x2, lnin_g, lnin_b, w_in, conv_w, conv_b, pool_w, pool_scale, w_out, ln1_g, ln1_b, w_r, b_r)


def _plan_kernel(rt_ref, counts_ref, pos_ref, tile_start_ref, tiles_ref, pad_ref, *, n_rows):
    lane = lax.broadcasted_iota(jnp.int32, (ROUTE_ROWS, LANES), 1)
    counts = jnp.broadcast_to(counts_ref[...], (ROUTE_ROWS, LANES))
    tiles = jnp.floor((counts + (ROW_TILE - 1)) * (1.0 / ROW_TILE))
    tile_end = tiles
    shift = 1
    while shift < N_EXPERTS:
        tile_end = tile_end + jnp.where(lane >= shift, pltpu.roll(tile_end, shift, axis=1), 0.0)
        shift *= 2
    row_start = (tile_end - tiles) * ROW_TILE
    tile_start_ref[...] = (tile_end - tiles)[0:1, :].astype(jnp.int32)
    tiles_ref[...] = tiles[0:1, :].astype(jnp.int32)

    rt = rt_ref[...]
    ids = rt[R_ID1:R_ID2 + 1, :]
    start = jnp.zeros_like(ids)
    for e in range(N_EXPERTS):
        start = jnp.where(ids == e, row_start[0:1, e:e + 1], start)
    pos_ref[...] = (start + rt[R_RANK1:R_RANK2 + 1, :]).astype(jnp.int32)

    sub = lax.broadcasted_iota(jnp.int32, (N_EXPERTS, LANES), 0)
    lane_e = lax.broadcasted_iota(jnp.int32, (N_EXPERTS, LANES), 1)
    diag = sub == lane_e
    pad_lo = jnp.sum(jnp.where(diag, (row_start + counts)[0:1, :], 0.0), axis=1, keepdims=True)
    pad_n = jnp.sum(jnp.where(diag, (tiles * ROW_TILE - counts)[0:1, :], 0.0), axis=1, keepdims=True)
    j = lax.broadcasted_iota(jnp.int32, (N_EXPERTS, ROW_TILE), 1).astype(jnp.float32)
    pad_ref[...] = jnp.where(j < pad_n, pad_lo + j, n_rows + j).astype(jnp.int32)


def _run_plan(route_t, counts, n_rows):
    n_tok = route_t.shape[1]
    full = lambda shape: pl.BlockSpec(shape, lambda i: (0, 0))
    return pl.pallas_call(
        functools.partial(_plan_kernel, n_rows=n_rows),
        grid=(1,),
        in_specs=[full((ROUTE_ROWS, n_tok)), full((1, LANES))],
        out_specs=[full((2, n_tok)), full((1, LANES)), full((1, LANES)), full((N_EXPERTS, ROW_TILE))],
        out_shape=[
            jax.ShapeDtypeStruct((2, n_tok), jnp.int32),
            jax.ShapeDtypeStruct((1, LANES), jnp.int32),
            jax.ShapeDtypeStruct((1, LANES), jnp.int32),
            jax.ShapeDtypeStruct((N_EXPERTS, ROW_TILE), jnp.int32),
        ],
        compiler_params=pltpu.CompilerParams(dimension_semantics=("arbitrary",)),
    )(route_t, counts)


def _sc_mesh():
    return plsc.VectorSubcoreMesh(core_axis_name="core", subcore_axis_name="subcore")


def _sc_worker_id():
    return lax.axis_index("core") * (SC_WORKERS // 2) + lax.axis_index("subcore")


def _dispatch_rows(xp, pos1, pos2, pad_pos, zero_rows, n_rows):
    n_tok, width = xp.shape
    n_win = n_tok // SC_WORKERS // MOVE_WINDOW
    n_pad = pad_pos.shape[0] // SC_WORKERS // PAD_WINDOW
    nbuf = MOVE_BUFFERS

    @functools.partial(
        pl.kernel, out_type=jax.ShapeDtypeStruct((n_rows + ROW_TILE, width), xp.dtype), mesh=_sc_mesh(),
        scratch_types=[pltpu.VMEM((n_win, MOVE_WINDOW), jnp.int32), pltpu.VMEM((n_win, MOVE_WINDOW), jnp.int32),
                       pltpu.VMEM((n_pad, PAD_WINDOW), jnp.int32),
                       pltpu.VMEM((nbuf, MOVE_WINDOW, width), xp.dtype), pltpu.VMEM((PAD_WINDOW, width), xp.dtype),
                       pltpu.SemaphoreType.DMA((nbuf,)), pltpu.SemaphoreType.DMA((nbuf,)), pltpu.SemaphoreType.DMA])
    def dispatch(xp_hbm, pos1_hbm, pos2_hbm, pad_hbm, zero_hbm, out_hbm,
                 idx1, idx2, idxp, buf, zbuf, lsem, ssem, psem):
        wid = _sc_worker_id()
        pltpu.sync_copy(pad_hbm.at[pl.ds(wid * n_pad, n_pad)], idxp)
        pltpu.sync_copy(zero_hbm, zbuf)
        pads = [pltpu.make_async_copy(zbuf, out_hbm.at[idxp.at[j]], psem) for j in range(n_pad)]
        for cp in pads:
            cp.start()
        pltpu.sync_copy(pos1_hbm.at[pl.ds(wid * n_win, n_win)], idx1)
        pltpu.sync_copy(pos2_hbm.at[pl.ds(wid * n_win, n_win)], idx2)

        def load(j):
            rows = xp_hbm.at[pl.ds((wid * n_win + j) * MOVE_WINDOW, MOVE_WINDOW)]
            return pltpu.make_async_copy(rows, buf.at[j % nbuf], lsem.at[j % nbuf])

        def scatters(j):
            return [pltpu.make_async_copy(buf.at[j % nbuf], out_hbm.at[idx.at[j]], ssem.at[j % nbuf])
                    for idx in (idx1, idx2)]

        for j in range(min(nbuf - 1, n_win)):
            load(j).start()
        for j in range(n_win):
            load(j).wait()
            for cp in scatters(j):
                cp.start()
            if j + nbuf - 1 < n_win:
                if j >= 1:
                    for cp in scatters(j - 1):
                        cp.wait()
                load(j + nbuf - 1).start()
        for j in range(max(0, n_win - nbuf), n_win):
            for cp in scatters(j):
                cp.wait()
        for cp in pads:
            cp.wait()

    return dispatch(xp, pos1.reshape(-1, MOVE_WINDOW), pos2.reshape(-1, MOVE_WINDOW),
                    pad_pos.reshape(-1, PAD_WINDOW), zero_rows)


def _gather_rows(src, idx):
    n_out, width = idx.shape[0], src.shape[1]
    n_win = n_out // SC_WORKERS // MOVE_WINDOW
    nbuf = MOVE_BUFFERS

    @functools.partial(
        pl.kernel, out_type=jax.ShapeDtypeStruct((n_out, width), src.dtype), mesh=_sc_mesh(),
        scratch_types=[pltpu.VMEM((n_win, MOVE_WINDOW), jnp.int32),
                       pltpu.VMEM((nbuf, MOVE_WINDOW, width), src.dtype),
                       pltpu.SemaphoreType.DMA((nbuf,)), pltpu.SemaphoreType.DMA((nbuf,))])
    def gather(src_hbm, idx_hbm, dst_hbm, idx_v, buf, gsem, ssem):
        wid = _sc_worker_id()
        pltpu.sync_copy(idx_hbm.at[pl.ds(wid * n_win, n_win)], idx_v)

        def fetch(j):
            return pltpu.make_async_copy(src_hbm.at[idx_v.at[j]], buf.at[j % nbuf], gsem.at[j % nbuf])

        def store(j):
            rows = dst_hbm.at[pl.ds((wid * n_win + j) * MOVE_WINDOW, MOVE_WINDOW)]
            return pltpu.make_async_copy(buf.at[j % nbuf], rows, ssem.at[j % nbuf])

        for j in range(min(nbuf - 1, n_win)):
            fetch(j).start()
        for j in range(n_win):
            fetch(j).wait()
            store(j).start()
            if j + nbuf - 1 < n_win:
                if j >= 1:
                    store(j - 1).wait()
                fetch(j + nbuf - 1).start()
        for j in range(max(0, n_win - nbuf), n_win):
            store(j).wait()

    return gather(src, idx.reshape(-1, MOVE_WINDOW))


def _tile_copy(hbm, buf, sem, tile, slot, to_hbm):
    rows = hbm.at[pl.ds(pl.multiple_of(tile * ROW_TILE, ROW_TILE), ROW_TILE)]
    if to_hbm:
        return pltpu.make_async_copy(buf.at[slot], rows, sem.at[slot])
    return pltpu.make_async_copy(rows, buf.at[slot], sem.at[slot])


def _weight_copies(w_hbm, wbuf, wsem, expert):
    slot = expert % WEIGHT_BUFFERS
    return [pltpu.make_async_copy(w.at[expert], buf.at[slot], wsem.at[slot]) for w, buf in zip(w_hbm, wbuf)]


def _expert_kernel(ts_ref, nte_ref, x_hbm, wg_hbm, wu_hbm, wd_hbm, y_hbm,
                   xbuf, ybuf, xsem, ysem, wg_buf, wu_buf, wd_buf, wsem, wgu_bf, wd_bf, act_ref, *, max_tiles):
    e = pl.program_id(0)
    first = ts_ref[e]
    count = nte_ref[e]
    n_tiles = ts_ref[N_EXPERTS - 1] + nte_ref[N_EXPERTS - 1]
    ahead = TILE_BUFFERS - 2
    w_hbm = (wg_hbm, wu_hbm, wd_hbm)
    wbuf = (wg_buf, wu_buf, wd_buf)

    @pl.when(e == 0)
    def _():
        for g in range(ahead):
            @pl.when(g < n_tiles)
            def _():
                _tile_copy(x_hbm, xbuf, xsem, g, g, False).start()
        for k in range(WEIGHT_BUFFERS - 1):
            for cp in _weight_copies(w_hbm, wbuf, wsem, k):
                cp.start()

    @pl.when(e + WEIGHT_BUFFERS - 1 < N_EXPERTS)
    def _():
        for cp in _weight_copies(w_hbm, wbuf, wsem, e + WEIGHT_BUFFERS - 1):
            cp.start()

    for cp in _weight_copies(w_hbm, wbuf, wsem, e):
        cp.wait()

    def sync_x(g):
        _tile_copy(x_hbm, xbuf, xsem, g, g % TILE_BUFFERS, False).wait()

        @pl.when(g + ahead < n_tiles)
        def _():
            _tile_copy(x_hbm, xbuf, xsem, g + ahead, (g + ahead) % TILE_BUFFERS, False).start()

    def sync_y_slot(g):
        @pl.when(g >= TILE_BUFFERS)
        def _():
            _tile_copy(y_hbm, ybuf, ysem, g - TILE_BUFFERS, g % TILE_BUFFERS, True).wait()

    def up_proj(g):
        lo, hi = _unpack_bf16_pairs(xbuf[g % TILE_BUFFERS])
        return (jnp.dot(lo.astype(jnp.bfloat16), wgu_bf[0:HALF, :], preferred_element_type=jnp.float32)
                + jnp.dot(hi.astype(jnp.bfloat16), wgu_bf[HALF:, :], preferred_element_type=jnp.float32))

    def put_act(hgu, which):
        hg = hgu[:, 0:D_EXPERT]
        act_ref[which] = (hg * jax.nn.sigmoid(hg) * hgu[:, D_EXPERT:]).astype(jnp.bfloat16)

    def down_proj(g, which):
        y = jnp.dot(act_ref[which], wd_bf[...], preferred_element_type=jnp.float32)
        ybuf[g % TILE_BUFFERS] = _pack_bf16_pairs(y)

    def start_y(g):
        _tile_copy(y_hbm, ybuf, ysem, g, g % TILE_BUFFERS, True).start()

    @pl.when(count > 0)
    def _():
        slot = e % WEIGHT_BUFFERS
        wgu_bf[:, 0:D_EXPERT] = wg_buf[slot].astype(jnp.bfloat16)
        wgu_bf[:, D_EXPERT:] = wu_buf[slot].astype(jnp.bfloat16)
        wd_bf[...] = wd_buf[slot].astype(jnp.bfloat16)

        sync_x(first)
        put_act(up_proj(first), 0)

        def pair_body(i, c):
            g = first + 1 + 2 * i
            sync_x(g)
            sync_x(g + 1)
            sync_y_slot(g - 1)
            sync_y_slot(g)
            down_proj(g - 1, 0)
            hgu0 = up_proj(g)
            put_act(hgu0, 1)
            hgu1 = up_proj(g + 1)
            down_proj(g, 1)
            put_act(hgu1, 0)
            start_y(g - 1)
            start_y(g)
            return c

        n_pairs = (count - 1) // 2
        lax.fori_loop(0, n_pairs, pair_body, 0)

        @pl.when((count - 1) % 2 == 1)
        def _():
            g = first + count - 1
            sync_x(g)
            sync_y_slot(g - 1)
            down_proj(g - 1, 0)
            put_act(up_proj(g), 0)
            start_y(g - 1)

        last = first + count - 1
        sync_y_slot(last)
        down_proj(last, 0)
        start_y(last)

    @pl.when(e == N_EXPERTS - 1)
    def _():
        for k in range(TILE_BUFFERS, 0, -1):
            @pl.when(n_tiles >= k)
            def _():
                _tile_copy(y_hbm, ybuf, ysem, n_tiles - k, (n_tiles - k) % TILE_BUFFERS, True).wait()

        ybuf[0] = jnp.zeros((ROW_TILE, HALF), jnp.uint32)

        def fill(g, c):
            cp = _tile_copy(y_hbm, ybuf, ysem, g, 0, True)
            cp.start()
            cp.wait()
            return c

        lax.fori_loop(n_tiles, max_tiles, fill, 0)


def _run_experts(tile_start, tiles_per_expert, x_sorted, w_gate, w_up, w_down):
    n_rows = x_sorted.shape[0]
    max_tiles = n_rows // ROW_TILE
    hbm = pl.BlockSpec(memory_space=pl.ANY)

    grid_spec = pltpu.PrefetchScalarGridSpec(
        num_scalar_prefetch=2,
        grid=(N_EXPERTS,),
        in_specs=[hbm, hbm, hbm, hbm],
        out_specs=hbm,
        scratch_shapes=[
            pltpu.VMEM((TILE_BUFFERS, ROW_TILE, HALF), jnp.uint32),
            pltpu.VMEM((TILE_BUFFERS, ROW_TILE, HALF), jnp.uint32),
            pltpu.SemaphoreType.DMA((TILE_BUFFERS,)),
            pltpu.SemaphoreType.DMA((TILE_BUFFERS,)),
            pltpu.VMEM((WEIGHT_BUFFERS, D_MODEL, D_EXPERT), jnp.float32),
            pltpu.VMEM((WEIGHT_BUFFERS, D_MODEL, D_EXPERT), jnp.float32),
            pltpu.VMEM((WEIGHT_BUFFERS, D_EXPERT, D_MODEL), jnp.float32),
            pltpu.SemaphoreType.DMA((WEIGHT_BUFFERS,)),
            pltpu.VMEM((D_MODEL, 2 * D_EXPERT), jnp.bfloat16),
            pltpu.VMEM((D_EXPERT, D_MODEL), jnp.bfloat16),
            pltpu.VMEM((2, ROW_TILE, D_EXPERT), jnp.bfloat16),
        ],
    )
    return pl.pallas_call(
        functools.partial(_expert_kernel, max_tiles=max_tiles),
        grid_spec=grid_spec,
        out_shape=jax.ShapeDtypeStruct((n_rows, HALF), jnp.uint32),
        compiler_params=pltpu.CompilerParams(
            dimension_semantics=("arbitrary",), vmem_limit_bytes=VMEM_LIMIT),
    )(tile_start, tiles_per_expert, x_sorted, w_gate, w_up, w_down)


def _combine_rows(r0, h1_ref, p_ref, route_ref, ya_ref, yb_ref, w_pg, b_pg, w_ple, g_ref, b_ref, o_ref):
    rows = pl.ds(r0, SUB_TILE)
    h1 = h1_ref[rows, :]
    h_hi = h1.astype(jnp.bfloat16)
    p_b = p_ref[rows, :].astype(jnp.bfloat16)
    yield
    gate_pre = jnp.dot(h_hi, w_pg[...], preferred_element_type=jnp.float32)
    ple_pre = jnp.dot(p_b, w_ple[...], preferred_element_type=jnp.float32)
    yield
    ple = ple_pre * jax.nn.sigmoid(gate_pre + b_pg[...])
    route = route_ref[rows, :]
    w1 = route[:, R_W1:R_W1 + 1]
    w2 = route[:, R_W2:R_W2 + 1]
    a_lo, a_hi = _unpack_bf16_pairs(ya_ref[rows, :])
    b_lo, b_hi = _unpack_bf16_pairs(yb_ref[rows, :])
    moe = jnp.concatenate([w1 * a_lo + w2 * b_lo, w1 * a_hi + w2 * b_hi], axis=-1)
    o_ref[rows, :] = _layernorm(DEEPNORM_ALPHA * h1 + ple + moe, g_ref[...], b_ref[...])


def _combine_kernel(h1_ref, p_ref, route_ref, ya_ref, yb_ref, w_pg, b_pg, w_ple, g_ref, b_ref, o_ref):
    chains = [_combine_rows(r0, h1_ref, p_ref, route_ref, ya_ref, yb_ref, w_pg, b_pg, w_ple, g_ref, b_ref, o_ref)
              for r0 in range(0, h1_ref.shape[0], SUB_TILE)]
    for t in range(COMBINE_PHASES + len(chains) - 1):
        for k, chain in reversed(list(enumerate(chains))):
            if 0 <= t - k < COMBINE_PHASES:
                next(chain, None)


def _run_combine(h1, p2, route, y_tok, w_pg, b_pg, w_ple, ln2_g, ln2_b):
    n_tok = h1.shape[0]
    n_t = n_tok // TOKEN_TILE
    tok_map = lambda i: (i, 0)
    const = lambda shape: pl.BlockSpec(shape, lambda i: (0, 0), pipeline_mode=pl.Buffered(1))
    in_specs = [
        pl.BlockSpec((TOKEN_TILE, D_MODEL), tok_map),
        pl.BlockSpec((TOKEN_TILE, PLE_DIM), tok_map),
        pl.BlockSpec((TOKEN_TILE, LANES), tok_map),
        pl.BlockSpec((TOKEN_TILE, HALF), lambda i: (i, 0)),
        pl.BlockSpec((TOKEN_TILE, HALF), lambda i: (i + n_t, 0)),
        const((D_MODEL, D_MODEL)), const((1, D_MODEL)), const((PLE_DIM, D_MODEL)),
        const((1, D_MODEL)), const((1, D_MODEL)),
    ]
    return pl.pallas_call(
        _combine_kernel,
        grid=(n_t,),
        in_specs=in_specs,
        out_specs=pl.BlockSpec((TOKEN_TILE, D_MODEL), tok_map),
        out_shape=jax.ShapeDtypeStruct((n_tok, D_MODEL), jnp.float32),
        compiler_params=pltpu.CompilerParams(dimension_semantics=("arbitrary",), vmem_limit_bytes=VMEM_LIMIT),
    )(h1, p2, route, y_tok, y_tok, w_pg, b_pg, w_ple, ln2_g, ln2_b)


def _split_bf16(w):
    hi = w.astype(jnp.bfloat16)
    lo = (w - hi.astype(jnp.float32)).astype(jnp.bfloat16)
    return hi, lo


def kernel(x, p, ln_in_g, ln_in_b, w_in, conv_w, conv_b, pool_w, pool_scale, w_out, ln1_g, ln1_b,
           w_rg, b_rg, w_re, b_re, w_gate, w_up, w_down, w_pg, b_pg, w_ple, ln2_g, ln2_b):
    batch, seq, _ = x.shape
    n_tok = batch * seq
    bf = jnp.bfloat16
    row = lambda v: v.reshape(1, -1)

    w_r = jnp.concatenate([w_rg[0], jnp.transpose(w_re[0], (1, 0, 2)).reshape(D_MODEL, N_EXPERTS)], axis=1)
    w_r = jnp.pad(w_r, ((0, 0), (0, LANES - w_r.shape[1])))
    w_r_hi, w_r_lo = _split_bf16(w_r)
    w_r_cat = jnp.concatenate([w_r_hi, w_r_hi, w_r_lo], axis=0)
    b_r = jnp.pad(jnp.concatenate([b_rg[0], b_re[0].reshape(-1)]), (0, LANES - N_GROUPS - N_EXPERTS)).reshape(1, LANES)

    x2 = x.reshape(n_tok, D_MODEL)
    p2 = p[0].reshape(n_tok, PLE_DIM)
    mixer_weights = (row(ln_in_g), row(ln_in_b), w_in[0], conv_w[0], row(conv_b[0]),
                     pool_w[0].astype(bf), row(pool_scale[0]), w_out[0].astype(bf), row(ln1_g[0]), row(ln1_b[0]),
                     w_r_cat, b_r)
    combine_weights = (w_pg[0].astype(bf), row(b_pg[0]), w_ple[0].astype(bf), row(ln2_g[0]), row(ln2_b[0]))
    expert_weights = (w_gate[0].reshape(N_EXPERTS, D_MODEL, D_EXPERT),
                      w_up[0].reshape(N_EXPERTS, D_MODEL, D_EXPERT),
                      w_down[0].reshape(N_EXPERTS, D_EXPERT, D_MODEL))
    zero_rows = jnp.zeros((PAD_WINDOW, HALF), jnp.uint32)

    n_rows = -(-(2 * n_tok + N_EXPERTS * (ROW_TILE - 1)) // ROW_TILE) * ROW_TILE
    xp, h1, route, route_t, counts = _run_mixer(x2, *mixer_weights, batch, seq)
    pos, tile_start, tiles_per_expert, pad_pos = _run_plan(route_t, counts, n_rows)
    x_sorted = _dispatch_rows(xp, pos[0], pos[1], pad_pos.reshape(-1), zero_rows, n_rows)
    y_sorted = _run_experts(tile_start[0, :N_EXPERTS], tiles_per_expert[0, :N_EXPERTS], x_sorted, *expert_weights)

    y_tok = _gather_rows(y_sorted, pos.reshape(-1))
    out = _run_combine(h1, p2, route, y_tok, *combine_weights)
    return out.reshape(batch, seq, D_MODEL)
```

```python
import functools

import jax
import jax.numpy as jnp
from jax import lax
from jax.experimental import pallas as pl
from jax.experimental.pallas import tpu as pltpu
from jax.experimental.pallas import tpu_sc as plsc

D_MODEL = 1024
CONV_WIDTH = 512
CONV_K = 3
POOL_WIDTH = 512
POOL_WINDOWS = (2, 4, 8, 16)
POOL_GW = 128
IN_PROJ = 3 * CONV_WIDTH + POOL_WIDTH
N_GROUPS = 4
EXPERTS_PER_GROUP = 8
N_EXPERTS = N_GROUPS * EXPERTS_PER_GROUP
D_EXPERT = 256
PLE_DIM = 256
LN_EPS = 1e-5
DEEPNORM_ALPHA = 2.0 ** 0.25

LANES = 128
HALF = D_MODEL // 2
CONV_HALO = 8
POOL_HALO = 16
SEQ_TILE = 1024
SUB_TILE = 256
MIXER_PHASES = 8
COMBINE_PHASES = 3
ROW_TILE = 256
TILE_BUFFERS = 8
SC_WORKERS = 32
PAD_WINDOW = 64
MOVE_WINDOW = 16
MOVE_BUFFERS = 8
WEIGHT_BUFFERS = 3
TOKEN_TILE = 1024
VMEM_LIMIT = 56 * 1024 * 1024

R_ID1, R_ID2, R_RANK1, R_RANK2, R_W1, R_W2 = range(6)
ROUTE_ROWS = 8
LOGIT_ROWS = 40


def _layernorm(x, g, b):
    mu = jnp.mean(x, axis=-1, keepdims=True)
    xc = x - mu
    var = jnp.mean(xc * xc, axis=-1, keepdims=True)
    return xc * lax.rsqrt(var + LN_EPS) * g + b


def _pack_bf16_pairs(v):
    bits = lax.bitcast_convert_type(v.astype(jnp.bfloat16).astype(jnp.float32), jnp.uint32)
    return bits[:, HALF:] | (bits[:, :HALF] >> 16)


def _unpack_bf16_pairs(w):
    lo = lax.bitcast_convert_type(w << 16, jnp.float32)
    hi = lax.bitcast_convert_type(w & jnp.uint32(0xFFFF0000), jnp.float32)
    return lo, hi


def _mixer_kernel(x_ref, lnin_g, lnin_b, w_in_f32, conv_w, conv_b, pool_w, pool_scale, w_out,
                  ln1_g, ln1_b, w_r, b_r,
                  xp_ref, h1_ref, route_ref, route_t_ref, counts_ref,
                  zbuf, vbuf, carry, w_in):
    b = pl.program_id(0)
    s = pl.program_id(1)
    ts = x_ref.shape[0]

    @pl.when(s == 0)
    def _():
        zbuf[0:CONV_HALO, :] = jnp.zeros((CONV_HALO, CONV_WIDTH), jnp.float32)
        vbuf[0:POOL_HALO, :] = jnp.zeros((POOL_HALO, POOL_WIDTH), jnp.float32)

    @pl.when((b == 0) & (s == 0))
    def _():
        carry[...] = jnp.zeros_like(carry)
        w_in[...] = w_in_f32[...].astype(jnp.bfloat16)

    chains = [_mixer_rows(r0, s * ts + r0, x_ref, lnin_g, lnin_b, w_in, conv_w, conv_b, pool_w, pool_scale,
                          w_out, ln1_g, ln1_b, w_r, b_r,
                          xp_ref, h1_ref, route_ref, route_t_ref, zbuf, vbuf, carry)
              for r0 in range(0, ts, SUB_TILE)]
    for t in range(MIXER_PHASES + len(chains) - 1):
        for k, chain in reversed(list(enumerate(chains))):
            if 0 <= t - k < MIXER_PHASES:
                next(chain, None)
    zbuf[0:CONV_HALO, :] = zbuf[ts:ts + CONV_HALO, :]
    vbuf[0:POOL_HALO, :] = vbuf[ts:ts + POOL_HALO, :]
    counts_ref[...] = jnp.transpose(carry[...])[0:1, :]


def _mixer_rows(r0, seq0, x_ref, lnin_g, lnin_b, w_in, conv_w, conv_b, pool_w, pool_scale, w_out,
                ln1_g, ln1_b, w_r, b_r,
                xp_ref, h1_ref, route_ref, route_t_ref, zbuf, vbuf, carry):
    n = SUB_TILE
    rows = pl.ds(r0, n)
    h0 = _layernorm(x_ref[rows, :], lnin_g[...], lnin_b[...])
    h0b = h0.astype(jnp.bfloat16)
    yield
    u_a = jnp.dot(h0b, w_in[:, 0:IN_PROJ // 2], preferred_element_type=jnp.float32)
    yield
    u_b = jnp.dot(h0b, w_in[:, IN_PROJ // 2:], preferred_element_type=jnp.float32)
    yield
    b_g = u_a[:, 0:CONV_WIDTH]
    c_g = u_a[:, CONV_WIDTH:]
    v_c = u_b[:, 0:CONV_WIDTH]
    v_p = u_b[:, CONV_WIDTH:]

    zbuf[pl.ds(CONV_HALO + r0, n), :] = c_g * v_c
    zext = zbuf[pl.ds(r0, n + CONV_HALO), :]
    z1 = pltpu.roll(zext, 1, axis=0)[CONV_HALO:, :]
    z2 = pltpu.roll(zext, 2, axis=0)[CONV_HALO:, :]
    conv = z2 * conv_w[0:1, :] + z1 * conv_w[1:2, :] + zext[CONV_HALO:, :] * conv_w[2:3, :] + conv_b[...]
    y_conv = b_g * conv

    vbuf[pl.ds(POOL_HALO + r0, n), :] = v_p
    vext = vbuf[pl.ds(r0, n + POOL_HALO), :]
    s2 = vext + pltpu.roll(vext, 1, axis=0)
    s4 = s2[:, POOL_GW:] + pltpu.roll(s2[:, POOL_GW:], 2, axis=0)
    s8 = s4[:, POOL_GW:] + pltpu.roll(s4[:, POOL_GW:], 4, axis=0)
    s16 = s8[:, POOL_GW:] + pltpu.roll(s8[:, POOL_GW:], 8, axis=0)
    wsums = (s2[POOL_HALO:, 0:POOL_GW], s4[POOL_HALO:, 0:POOL_GW],
             s8[POOL_HALO:, 0:POOL_GW], s16[POOL_HALO:, 0:POOL_GW])

    t_pos = seq0 + lax.broadcasted_iota(jnp.int32, (n, 1), 0)
    y_pool = []
    for j, w in enumerate(POOL_WINDOWS):
        inv_cnt = 1.0 / jnp.minimum(t_pos + 1, w).astype(jnp.float32)
        pooled = wsums[j] * inv_cnt - v_p[:, j * POOL_GW:(j + 1) * POOL_GW]
        y_pool.append(jnp.dot(pooled.astype(jnp.bfloat16), pool_w[j], preferred_element_type=jnp.float32))
    y_pool = jnp.concatenate(y_pool, axis=-1) * pool_scale[...]

    ycat = jnp.concatenate([y_conv, y_pool], axis=-1).astype(jnp.bfloat16)
    yield
    mix = jnp.dot(ycat, w_out[...], preferred_element_type=jnp.float32)
    yield
    h1 = _layernorm(DEEPNORM_ALPHA * h0 + mix, ln1_g[...], ln1_b[...])

    h_hi = h1.astype(jnp.bfloat16)
    h_lo = (h1 - h_hi.astype(jnp.float32)).astype(jnp.bfloat16)
    xp_ref[rows, :] = _pack_bf16_pairs(h1)
    h1_ref[rows, :] = h1
    hcat = jnp.concatenate([h_hi, h_lo, h_hi], axis=-1)
    yield
    logits = jnp.dot(hcat, w_r[...], preferred_element_type=jnp.float32) + b_r[...]
    yield
    lt = jnp.transpose(logits)[0:LOGIT_ROWS, :]
    rid = lax.broadcasted_iota(jnp.int32, (LOGIT_ROWS, n), 0).astype(jnp.float32)
    neg = jnp.float32(-jnp.inf)

    def first_argmax(vals):
        m = jnp.max(vals, axis=0, keepdims=True)
        idx = jnp.min(jnp.where(vals == m, rid, float(LOGIT_ROWS)), axis=0, keepdims=True)
        return m, idx

    g_mask = rid < N_GROUPS
    g_max, g_idx = first_argmax(jnp.where(g_mask, lt, neg))
    g_w = 1.0 / jnp.sum(jnp.where(g_mask, jnp.exp(lt - g_max), 0.0), axis=0, keepdims=True)

    e_lo = N_GROUPS + EXPERTS_PER_GROUP * g_idx
    e_vals = jnp.where((rid >= e_lo) & (rid < e_lo + EXPERTS_PER_GROUP), lt, neg)
    m1, i1 = first_argmax(e_vals)
    m2, i2 = first_argmax(jnp.where(rid == i1, neg, e_vals))
    e21 = jnp.exp(m2 - m1)
    w1 = g_w / (1.0 + e21)
    w2 = g_w * e21 / (1.0 + e21)
    id1 = i1 - N_GROUPS
    id2 = i2 - N_GROUPS

    eid = lax.broadcasted_iota(jnp.int32, (N_EXPERTS, n), 0).astype(jnp.float32)
    sel1 = eid == id1
    sel2 = eid == id2
    onehot = (sel1 | sel2).astype(jnp.float32)
    src = lax.broadcasted_iota(jnp.int32, (n, n), 0)
    dst = lax.broadcasted_iota(jnp.int32, (n, n), 1)
    earlier = (src < dst).astype(jnp.bfloat16)
    before = (jnp.dot(onehot.astype(jnp.bfloat16), earlier, preferred_element_type=jnp.float32)
              + carry[0:N_EXPERTS, 0:1])
    rank1 = jnp.sum(jnp.where(sel1, before, 0.0), axis=0, keepdims=True)
    rank2 = jnp.sum(jnp.where(sel2, before, 0.0), axis=0, keepdims=True)
    carry[0:N_EXPERTS, :] = carry[0:N_EXPERTS, :] + jnp.sum(onehot, axis=1, keepdims=True)

    rec_t = jnp.zeros((ROUTE_ROWS, n), jnp.float32)
    rec_row = lax.broadcasted_iota(jnp.int32, (ROUTE_ROWS, n), 0)
    for k, val in ((R_ID1, id1), (R_ID2, id2), (R_W1, w1), (R_W2, w2), (R_RANK1, rank1), (R_RANK2, rank2)):
        rec_t = jnp.where(rec_row == k, val, rec_t)
    route_t_ref[:, rows] = rec_t
    padded = jnp.concatenate([rec_t, jnp.zeros((LANES - ROUTE_ROWS, n), jnp.float32)], axis=0)
    route_ref[rows, :] = jnp.transpose(padded)


def _run_mixer(x2, lnin_g, lnin_b, w_in, conv_w, conv_b, pool_w, pool_scale, w_out, ln1_g, ln1_b,
               w_r, b_r, batch, seq):
    n_tok = batch * seq
    n_s = seq // SEQ_TILE
    tok_map = lambda b, s: (b * n_s + s, 0)

    def const(shape):
        return pl.BlockSpec(shape, lambda b, s: (0,) * len(shape), pipeline_mode=pl.Buffered(1))

    in_specs = [
        pl.BlockSpec((SEQ_TILE, D_MODEL), tok_map),
        const((1, D_MODEL)), const((1, D_MODEL)),
        const((D_MODEL, IN_PROJ)),
        const((CONV_K, CONV_WIDTH)), const((1, CONV_WIDTH)),
        const((len(POOL_WINDOWS), POOL_GW, POOL_GW)), const((1, POOL_WIDTH)),
        const((D_MODEL, D_MODEL)),
        const((1, D_MODEL)), const((1, D_MODEL)),
        const((3 * D_MODEL, LANES)), const((1, LANES)),
    ]
    out_specs = [
        pl.BlockSpec((SEQ_TILE, HALF), tok_map),
        pl.BlockSpec((SEQ_TILE, D_MODEL), tok_map),
        pl.BlockSpec((SEQ_TILE, LANES), tok_map),
        pl.BlockSpec((ROUTE_ROWS, SEQ_TILE), lambda b, s: (0, b * n_s + s)),
        pl.BlockSpec((1, LANES), lambda b, s: (0, 0)),
    ]
    out_shape = [
        jax.ShapeDtypeStruct((n_tok, HALF), jnp.uint32),
        jax.ShapeDtypeStruct((n_tok, D_MODEL), jnp.float32),
        jax.ShapeDtypeStruct((n_tok, LANES), jnp.float32),
        jax.ShapeDtypeStruct((ROUTE_ROWS, n_tok), jnp.float32),
        jax.ShapeDtypeStruct((1, LANES), jnp.float32),
    ]
    return pl.pallas_call(
        _mixer_kernel,
        grid=(batch, n_s),
        in_specs=in_specs,
        out_specs=out_specs,
        out_shape=out_shape,
        scratch_shapes=[
            pltpu.VMEM((SEQ_TILE + CONV_HALO, CONV_WIDTH), jnp.float32),
            pltpu.VMEM((SEQ_TILE + POOL_HALO, POOL_WIDTH), jnp.float32),
            pltpu.VMEM((LANES, LANES), jnp.float32),
            pltpu.VMEM((D_MODEL, IN_PROJ), jnp.bfloat16),
        ],
        compiler_params=pltpu.CompilerParams(
            dimension_semantics=("arbitrary", "arbitrary"), vmem_limit_bytes=VMEM_LIMIT),
    )(x2, lnin_g, lnin_b, w_in, conv_w, conv_b, pool_w, pool_scale, w_out, ln1_g, ln1_b, w_r, b_r)


def _plan_kernel(rt_ref, counts_ref, pos_ref, tile_start_ref, tiles_ref, pad_ref, *, n_rows):
    lane = lax.broadcasted_iota(jnp.int32, (ROUTE_ROWS, LANES), 1)
    counts = jnp.broadcast_to(counts_ref[...], (ROUTE_ROWS, LANES))
    tiles = jnp.floor((counts + (ROW_TILE - 1)) * (1.0 / ROW_TILE))
    tile_end = tiles
    shift = 1
    while shift < N_EXPERTS:
        tile_end = tile_end + jnp.where(lane >= shift, pltpu.roll(tile_end, shift, axis=1), 0.0)
        shift *= 2
    row_start = (tile_end - tiles) * ROW_TILE
    tile_start_ref[...] = (tile_end - tiles)[0:1, :].astype(jnp.int32)
    tiles_ref[...] = tiles[0:1, :].astype(jnp.int32)

    rt = rt_ref[...]
    ids = rt[R_ID1:R_ID2 + 1, :]
    start = jnp.zeros_like(ids)
    for e in range(N_EXPERTS):
        start = jnp.where(ids == e, row_start[0:1, e:e + 1], start)
    pos_ref[...] = (start + rt[R_RANK1:R_RANK2 + 1, :]).astype(jnp.int32)

    sub = lax.broadcasted_iota(jnp.int32, (N_EXPERTS, LANES), 0)
    lane_e = lax.broadcasted_iota(jnp.int32, (N_EXPERTS, LANES), 1)
    diag = sub == lane_e
    pad_lo = jnp.sum(jnp.where(diag, (row_start + counts)[0:1, :], 0.0), axis=1, keepdims=True)
    pad_n = jnp.sum(jnp.where(diag, (tiles * ROW_TILE - counts)[0:1, :], 0.0), axis=1, keepdims=True)
    j = lax.broadcasted_iota(jnp.int32, (N_EXPERTS, ROW_TILE), 1).astype(jnp.float32)
    pad_ref[...] = jnp.where(j < pad_n, pad_lo + j, n_rows + j).astype(jnp.int32)


def _run_plan(route_t, counts, n_rows):
    n_tok = route_t.shape[1]
    full = lambda shape: pl.BlockSpec(shape, lambda i: (0, 0))
    return pl.pallas_call(
        functools.partial(_plan_kernel, n_rows=n_rows),
        grid=(1,),
        in_specs=[full((ROUTE_ROWS, n_tok)), full((1, LANES))],
        out_specs=[full((2, n_tok)), full((1, LANES)), full((1, LANES)), full((N_EXPERTS, ROW_TILE))],
        out_shape=[
            jax.ShapeDtypeStruct((2, n_tok), jnp.int32),
            jax.ShapeDtypeStruct((1, LANES), jnp.int32),
            jax.ShapeDtypeStruct((1, LANES), jnp.int32),
            jax.ShapeDtypeStruct((N_EXPERTS, ROW_TILE), jnp.int32),
        ],
        compiler_params=pltpu.CompilerParams(dimension_semantics=("arbitrary",)),
    )(route_t, counts)


def _sc_mesh():
    return plsc.VectorSubcoreMesh(core_axis_name="core", subcore_axis_name="subcore")


def _sc_worker_id():
    return lax.axis_index("core") * (SC_WORKERS // 2) + lax.axis_index("subcore")


def _dispatch_rows(xp, pos1, pos2, pad_pos, zero_rows, n_rows):
    n_tok, width = xp.shape
    n_win = n_tok // SC_WORKERS // MOVE_WINDOW
    n_pad = pad_pos.shape[0] // SC_WORKERS // PAD_WINDOW
    nbuf = MOVE_BUFFERS

    @functools.partial(
        pl.kernel, out_type=jax.ShapeDtypeStruct((n_rows + ROW_TILE, width), xp.dtype), mesh=_sc_mesh(),
        scratch_types=[pltpu.VMEM((n_win, MOVE_WINDOW), jnp.int32), pltpu.VMEM((n_win, MOVE_WINDOW), jnp.int32),
                       pltpu.VMEM((n_pad, PAD_WINDOW), jnp.int32),
                       pltpu.VMEM((nbuf, MOVE_WINDOW, width), xp.dtype), pltpu.VMEM((PAD_WINDOW, width), xp.dtype),
                       pltpu.SemaphoreType.DMA((nbuf,)), pltpu.SemaphoreType.DMA((nbuf,)), pltpu.SemaphoreType.DMA])
    def dispatch(xp_hbm, pos1_hbm, pos2_hbm, pad_hbm, zero_hbm, out_hbm,
                 idx1, idx2, idxp, buf, zbuf, lsem, ssem, psem):
        wid = _sc_worker_id()
        pltpu.sync_copy(pad_hbm.at[pl.ds(wid * n_pad, n_pad)], idxp)
        pltpu.sync_copy(zero_hbm, zbuf)
        pads = [pltpu.make_async_copy(zbuf, out_hbm.at[idxp.at[j]], psem) for j in range(n_pad)]
        for cp in pads:
            cp.start()
        pltpu.sync_copy(pos1_hbm.at[pl.ds(wid * n_win, n_win)], idx1)
        pltpu.sync_copy(pos2_hbm.at[pl.ds(wid * n_win, n_win)], idx2)

        def load(j):
            rows = xp_hbm.at[pl.ds((wid * n_win + j) * MOVE_WINDOW, MOVE_WINDOW)]
            return pltpu.make_async_copy(rows, buf.at[j % nbuf], lsem.at[j % nbuf])

        def scatters(j):
            return [pltpu.make_async_copy(buf.at[j % nbuf], out_hbm.at[idx.at[j]], ssem.at[j % nbuf])
                    for idx in (idx1, idx2)]

        for j in range(min(nbuf - 1, n_win)):
            load(j).start()
        for j in range(n_win):
            load(j).wait()
            for cp in scatters(j):
                cp.start()
            if j + nbuf - 1 < n_win:
                if j >= 1:
                    for cp in scatters(j - 1):
                        cp.wait()
                load(j + nbuf - 1).start()
        for j in range(max(0, n_win - nbuf), n_win):
            for cp in scatters(j):
                cp.wait()
        for cp in pads:
            cp.wait()

    return dispatch(xp, pos1.reshape(-1, MOVE_WINDOW), pos2.reshape(-1, MOVE_WINDOW),
                    pad_pos.reshape(-1, PAD_WINDOW), zero_rows)


def _gather_rows(src, idx):
    n_out, width = idx.shape[0], src.shape[1]
    n_win = n_out // SC_WORKERS // MOVE_WINDOW
    nbuf = MOVE_BUFFERS

    @functools.partial(
        pl.kernel, out_type=jax.ShapeDtypeStruct((n_out, width), src.dtype), mesh=_sc_mesh(),
        scratch_types=[pltpu.VMEM((n_win, MOVE_WINDOW), jnp.int32),
                       pltpu.VMEM((nbuf, MOVE_WINDOW, width), src.dtype),
                       pltpu.SemaphoreType.DMA((nbuf,)), pltpu.SemaphoreType.DMA((nbuf,))])
    def gather(src_hbm, idx_hbm, dst_hbm, idx_v, buf, gsem, ssem):
        wid = _sc_worker_id()
        pltpu.sync_copy(idx_hbm.at[pl.ds(wid * n_win, n_win)], idx_v)

        def fetch(j):
            return pltpu.make_async_copy(src_hbm.at[idx_v.at[j]], buf.at[j % nbuf], gsem.at[j % nbuf])

        def store(j):
            rows = dst_hbm.at[pl.ds((wid * n_win + j) * MOVE_WINDOW, MOVE_WINDOW)]
            return pltpu.make_async_copy(buf.at[j % nbuf], rows, ssem.at[j % nbuf])

        for j in range(min(nbuf - 1, n_win)):
            fetch(j).start()
        for j in range(n_win):
            fetch(j).wait()
            store(j).start()
            if j + nbuf - 1 < n_win:
                if j >= 1:
                    store(j - 1).wait()
                fetch(j + nbuf - 1).start()
        for j in range(max(0, n_win - nbuf), n_win):
            store(j).wait()

    return gather(src, idx.reshape(-1, MOVE_WINDOW))


def _tile_copy(hbm, buf, sem, tile, slot, to_hbm):
    rows = hbm.at[pl.ds(pl.multiple_of(tile * ROW_TILE, ROW_TILE), ROW_TILE)]
    if to_hbm:
        return pltpu.make_async_copy(buf.at[slot], rows, sem.at[slot])
    return pltpu.make_async_copy(rows, buf.at[slot], sem.at[slot])


def _weight_copies(w_hbm, wbuf, wsem, expert):
    slot = expert % WEIGHT_BUFFERS
    return [pltpu.make_async_copy(w.at[expert], buf.at[slot], wsem.at[slot]) for w, buf in zip(w_hbm, wbuf)]


def _expert_kernel(ts_ref, nte_ref, x_hbm, wg_hbm, wu_hbm, wd_hbm, y_hbm,
                   xbuf, ybuf, xsem, ysem, wg_buf, wu_buf, wd_buf, wsem, wgu_bf, wd_bf, act_ref, *, max_tiles):
    e = pl.program_id(0)
    first = ts_ref[e]
    count = nte_ref[e]
    n_tiles = ts_ref[N_EXPERTS - 1] + nte_ref[N_EXPERTS - 1]
    ahead = TILE_BUFFERS - 2
    w_hbm = (wg_hbm, wu_hbm, wd_hbm)
    wbuf = (wg_buf, wu_buf, wd_buf)

    @pl.when(e == 0)
    def _():
        for g in range(ahead):
            @pl.when(g < n_tiles)
            def _():
                _tile_copy(x_hbm, xbuf, xsem, g, g, False).start()
        for k in range(WEIGHT_BUFFERS - 1):
            for cp in _weight_copies(w_hbm, wbuf, wsem, k):
                cp.start()

    @pl.when(e + WEIGHT_BUFFERS - 1 < N_EXPERTS)
    def _():
        for cp in _weight_copies(w_hbm, wbuf, wsem, e + WEIGHT_BUFFERS - 1):
            cp.start()

    for cp in _weight_copies(w_hbm, wbuf, wsem, e):
        cp.wait()

    def sync_x(g):
        _tile_copy(x_hbm, xbuf, xsem, g, g % TILE_BUFFERS, False).wait()

        @pl.when(g + ahead < n_tiles)
        def _():
            _tile_copy(x_hbm, xbuf, xsem, g + ahead, (g + ahead) % TILE_BUFFERS, False).start()

    def sync_y_slot(g):
        @pl.when(g >= TILE_BUFFERS)
        def _():
            _tile_copy(y_hbm, ybuf, ysem, g - TILE_BUFFERS, g % TILE_BUFFERS, True).wait()

    def up_proj(g):
        lo, hi = _unpack_bf16_pairs(xbuf[g % TILE_BUFFERS])
        return (jnp.dot(lo.astype(jnp.bfloat16), wgu_bf[0:HALF, :], preferred_element_type=jnp.float32)
                + jnp.dot(hi.astype(jnp.bfloat16), wgu_bf[HALF:, :], preferred_element_type=jnp.float32))

    def put_act(hgu, which):
        hg = hgu[:, 0:D_EXPERT]
        act_ref[which] = (hg * jax.nn.sigmoid(hg) * hgu[:, D_EXPERT:]).astype(jnp.bfloat16)

    def down_proj(g, which):
        y = jnp.dot(act_ref[which], wd_bf[...], preferred_element_type=jnp.float32)
        ybuf[g % TILE_BUFFERS] = _pack_bf16_pairs(y)

    def start_y(g):
        _tile_copy(y_hbm, ybuf, ysem, g, g % TILE_BUFFERS, True).start()

    @pl.when(count > 0)
    def _():
        slot = e % WEIGHT_BUFFERS
        wgu_bf[:, 0:D_EXPERT] = wg_buf[slot].astype(jnp.bfloat16)
        wgu_bf[:, D_EXPERT:] = wu_buf[slot].astype(jnp.bfloat16)
        wd_bf[...] = wd_buf[slot].astype(jnp.bfloat16)

        sync_x(first)
        put_act(up_proj(first), 0)

        def pair_body(i, c):
            g = first + 1 + 2 * i
            sync_x(g)
            sync_x(g + 1)
            sync_y_slot(g - 1)
            sync_y_slot(g)
            down_proj(g - 1, 0)
            hgu0 = up_proj(g)
            put_act(hgu0, 1)
            hgu1 = up_proj(g + 1)
            down_proj(g, 1)
            put_act(hgu1, 0)
            start_y(g - 1)
            start_y(g)
            return c

        n_pairs = (count - 1) // 2
        lax.fori_loop(0, n_pairs, pair_body, 0)

        @pl.when((count - 1) % 2 == 1)
        def _():
            g = first + count - 1
            sync_x(g)
            sync_y_slot(g - 1)
            down_proj(g - 1, 0)
            put_act(up_proj(g), 0)
            start_y(g - 1)

        last = first + count - 1
        sync_y_slot(last)
        down_proj(last, 0)
        start_y(last)

    @pl.when(e == N_EXPERTS - 1)
    def _():
        for k in range(TILE_BUFFERS, 0, -1):
            @pl.when(n_tiles >= k)
            def _():
                _tile_copy(y_hbm, ybuf, ysem, n_tiles - k, (n_tiles - k) % TILE_BUFFERS, True).wait()

        ybuf[0] = jnp.zeros((ROW_TILE, HALF), jnp.uint32)

        def fill(g, c):
            cp = _tile_copy(y_hbm, ybuf, ysem, g, 0, True)
            cp.start()
            cp.wait()
            return c

        lax.fori_loop(n_tiles, max_tiles, fill, 0)


def _run_experts(tile_start, tiles_per_expert, x_sorted, w_gate, w_up, w_down):
    n_rows = x_sorted.shape[0]
    max_tiles = n_rows // ROW_TILE
    hbm = pl.BlockSpec(memory_space=pl.ANY)

    grid_spec = pltpu.PrefetchScalarGridSpec(
        num_scalar_prefetch=2,
        grid=(N_EXPERTS,),
        in_specs=[hbm, hbm, hbm, hbm],
        out_specs=hbm,
        scratch_shapes=[
            pltpu.VMEM((TILE_BUFFERS, ROW_TILE, HALF), jnp.uint32),
            pltpu.VMEM((TILE_BUFFERS, ROW_TILE, HALF), jnp.uint32),
            pltpu.SemaphoreType.DMA((TILE_BUFFERS,)),
            pltpu.SemaphoreType.DMA((TILE_BUFFERS,)),
            pltpu.VMEM((WEIGHT_BUFFERS, D_MODEL, D_EXPERT), jnp.float32),
            pltpu.VMEM((WEIGHT_BUFFERS, D_MODEL, D_EXPERT), jnp.float32),
            pltpu.VMEM((WEIGHT_BUFFERS, D_EXPERT, D_MODEL), jnp.float32),
            pltpu.SemaphoreType.DMA((WEIGHT_BUFFERS,)),
            pltpu.VMEM((D_MODEL, 2 * D_EXPERT), jnp.bfloat16),
            pltpu.VMEM((D_EXPERT, D_MODEL), jnp.bfloat16),
            pltpu.VMEM((2, ROW_TILE, D_EXPERT), jnp.bfloat16),
        ],
    )
    return pl.pallas_call(
        functools.partial(_expert_kernel, max_tiles=max_tiles),
        grid_spec=grid_spec,
        out_shape=jax.ShapeDtypeStruct((n_rows, HALF), jnp.uint32),
        compiler_params=pltpu.CompilerParams(
            dimension_semantics=("arbitrary",), vmem_limit_bytes=VMEM_LIMIT),
    )(tile_start, tiles_per_expert, x_sorted, w_gate, w_up, w_down)


def _combine_rows(r0, h1_ref, p_ref, route_ref, ya_ref, yb_ref, w_pg, b_pg, w_ple, g_ref, b_ref, o_ref):
    rows = pl.ds(r0, SUB_TILE)
    h1 = h1_ref[rows, :]
    h_hi = h1.astype(jnp.bfloat16)
    p_b = p_ref[rows, :].astype(jnp.bfloat16)
    yield
    gate_pre = jnp.dot(h_hi, w_pg[...], preferred_element_type=jnp.float32)
    ple_pre = jnp.dot(p_b, w_ple[...], preferred_element_type=jnp.float32)
    yield
    ple = ple_pre * jax.nn.sigmoid(gate_pre + b_pg[...])
    route = route_ref[rows, :]
    w1 = route[:, R_W1:R_W1 + 1]
    w2 = route[:, R_W2:R_W2 + 1]
    a_lo, a_hi = _unpack_bf16_pairs(ya_ref[rows, :])
    b_lo, b_hi = _unpack_bf16_pairs(yb_ref[rows, :])
    moe = jnp.concatenate([w1 * a_lo + w2 * b_lo, w1 * a_hi + w2 * b_hi], axis=-1)
    o_ref[rows, :] = _layernorm(DEEPNORM_ALPHA * h1 + ple + moe, g_ref[...], b_ref[...])


def _combine_kernel(h1_ref, p_ref, route_ref, ya_ref, yb_ref, w_pg, b_pg, w_ple, g_ref, b_ref, o_ref):
    chains = [_combine_rows(r0, h1_ref, p_ref, route_ref, ya_ref, yb_ref, w_pg, b_pg, w_ple, g_ref, b_ref, o_ref)
              for r0 in range(0, h1_ref.shape[0], SUB_TILE)]
    for t in range(COMBINE_PHASES + len(chains) - 1):
        for k, chain in reversed(list(enumerate(chains))):
            if 0 <= t - k < COMBINE_PHASES:
                next(chain, None)


def _run_combine(h1, p2, route, y_tok, w_pg, b_pg, w_ple, ln2_g, ln2_b):
    n_tok = h1.shape[0]
    n_t = n_tok // TOKEN_TILE
    tok_map = lambda i: (i, 0)
    const = lambda shape: pl.BlockSpec(shape, lambda i: (0, 0), pipeline_mode=pl.Buffered(1))
    in_specs = [
        pl.BlockSpec((TOKEN_TILE, D_MODEL), tok_map),
        pl.BlockSpec((TOKEN_TILE, PLE_DIM), tok_map),
        pl.BlockSpec((TOKEN_TILE, LANES), tok_map),
        pl.BlockSpec((TOKEN_TILE, HALF), lambda i: (i, 0)),
        pl.BlockSpec((TOKEN_TILE, HALF), lambda i: (i + n_t, 0)),
        const((D_MODEL, D_MODEL)), const((1, D_MODEL)), const((PLE_DIM, D_MODEL)),
        const((1, D_MODEL)), const((1, D_MODEL)),
    ]
    return pl.pallas_call(
        _combine_kernel,
        grid=(n_t,),
        in_specs=in_specs,
        out_specs=pl.BlockSpec((TOKEN_TILE, D_MODEL), tok_map),
        out_shape=jax.ShapeDtypeStruct((n_tok, D_MODEL), jnp.float32),
        compiler_params=pltpu.CompilerParams(dimension_semantics=("arbitrary",), vmem_limit_bytes=VMEM_LIMIT),
    )(h1, p2, route, y_tok, y_tok, w_pg, b_pg, w_ple, ln2_g, ln2_b)


def _split_bf16(w):
    hi = w.astype(jnp.bfloat16)
    lo = (w - hi.astype(jnp.float32)).astype(jnp.bfloat16)
    return hi, lo


def kernel(x, p, ln_in_g, ln_in_b, w_in, conv_w, conv_b, pool_w, pool_scale, w_out, ln1_g, ln1_b,
           w_rg, b_rg, w_re, b_re, w_gate, w_up, w_down, w_pg, b_pg, w_ple, ln2_g, ln2_b):
    batch, seq, _ = x.shape
    n_tok = batch * seq
    bf = jnp.bfloat16
    row = lambda v: v.reshape(1, -1)

    w_r = jnp.concatenate([w_rg[0], jnp.transpose(w_re[0], (1, 0, 2)).reshape(D_MODEL, N_EXPERTS)], axis=1)
    w_r = jnp.pad(w_r, ((0, 0), (0, LANES - w_r.shape[1])))
    w_r_hi, w_r_lo = _split_bf16(w_r)
    w_r_cat = jnp.concatenate([w_r_hi, w_r_hi, w_r_lo], axis=0)
    b_r = jnp.pad(jnp.concatenate([b_rg[0], b_re[0].reshape(-1)]), (0, LANES - N_GROUPS - N_EXPERTS)).reshape(1, LANES)

    x2 = x.reshape(n_tok, D_MODEL)
    p2 = p[0].reshape(n_tok, PLE_DIM)
    mixer_weights = (row(ln_in_g), row(ln_in_b), w_in[0], conv_w[0], row(conv_b[0]),
                     pool_w[0].astype(bf), row(pool_scale[0]), w_out[0].astype(bf), row(ln1_g[0]), row(ln1_b[0]),
                     w_r_cat, b_r)
    combine_weights = (w_pg[0].astype(bf), row(b_pg[0]), w_ple[0].astype(bf), row(ln2_g[0]), row(ln2_b[0]))
    expert_weights = (w_gate[0].reshape(N_EXPERTS, D_MODEL, D_EXPERT),
                      w_up[0].reshape(N_EXPERTS, D_MODEL, D_EXPERT),
                      w_down[0].reshape(N_EXPERTS, D_EXPERT, D_MODEL))
    zero_rows = jnp.zeros((PAD_WINDOW, HALF), jnp.uint32)

    n_rows = -(-(2 * n_tok + N_EXPERTS * (ROW_TILE - 1)) // ROW_TILE) * ROW_TILE
    xp, h1, route, route_t, counts = _run_mixer(x2, *mixer_weights, batch, seq)
    pos, tile_start, tiles_per_expert, pad_pos = _run_plan(route_t, counts, n_rows)
    x_sorted = _dispatch_rows(xp, pos[0], pos[1], pad_pos.reshape(-1), zero_rows, n_rows)
    y_sorted = _run_experts(tile_start[0, :N_EXPERTS], tiles_per_expert[0, :N_EXPERTS], x_sorted, *expert_weights)

    y_tok = _gather_rows(y_sorted, pos.reshape(-1))
    out = _run_combine(h1, p2, route, y_tok, *combine_weights)
    return out.reshape(batch, seq, D_MODEL)
```

```python
import functools

import jax
import jax.numpy as jnp
from jax import lax
from jax.experimental import pallas as pl
from jax.experimental.pallas import tpu as pltpu
from jax.experimental.pallas import tpu_sc as plsc

D_MODEL = 1024
CONV_WIDTH = 512
CONV_K = 3
POOL_WIDTH = 512
POOL_WINDOWS = (2, 4, 8, 16)
POOL_GW = 128
IN_PROJ = 3 * CONV_WIDTH + POOL_WIDTH
N_GROUPS = 4
EXPERTS_PER_GROUP = 8
N_EXPERTS = N_GROUPS * EXPERTS_PER_GROUP
D_EXPERT = 256
PLE_DIM = 256
LN_EPS = 1e-5
DEEPNORM_ALPHA = 2.0 ** 0.25

LANES = 128
HALF = D_MODEL // 2
CONV_HALO = 8
POOL_HALO = 16
SEQ_TILE = 1024
SUB_TILE = 256
MIXER_PHASES = 8
COMBINE_PHASES = 3
ROW_TILE = 256
TILE_BUFFERS = 8
SC_WORKERS = 32
SC_WINDOW = 64
GATHER_WINDOW = 32
GATHER_BUFFERS = 4
WEIGHT_BUFFERS = 3
TOKEN_TILE = 1024
VMEM_LIMIT = 56 * 1024 * 1024

R_ID1, R_ID2, R_RANK1, R_RANK2, R_W1, R_W2 = range(6)
ROUTE_ROWS = 8
LOGIT_ROWS = 40


def _layernorm(x, g, b):
    mu = jnp.mean(x, axis=-1, keepdims=True)
    xc = x - mu
    var = jnp.mean(xc * xc, axis=-1, keepdims=True)
    return xc * lax.rsqrt(var + LN_EPS) * g + b


def _pack_bf16_pairs(v):
    bits = lax.bitcast_convert_type(v.astype(jnp.bfloat16).astype(jnp.float32), jnp.uint32)
    return bits[:, HALF:] | (bits[:, :HALF] >> 16)


def _unpack_bf16_pairs(w):
    lo = lax.bitcast_convert_type(w << 16, jnp.float32)
    hi = lax.bitcast_convert_type(w & jnp.uint32(0xFFFF0000), jnp.float32)
    return lo, hi


def _mixer_kernel(x_ref, lnin_g, lnin_b, w_in_f32, conv_w, conv_b, pool_w, pool_scale, w_out,
                  ln1_g, ln1_b, w_r, b_r,
                  xp_ref, h1_ref, route_ref, route_t_ref, counts_ref,
                  zbuf, vbuf, carry, w_in):
    b = pl.program_id(0)
    s = pl.program_id(1)
    ts = x_ref.shape[0]

    @pl.when(s == 0)
    def _():
        zbuf[0:CONV_HALO, :] = jnp.zeros((CONV_HALO, CONV_WIDTH), jnp.float32)
        vbuf[0:POOL_HALO, :] = jnp.zeros((POOL_HALO, POOL_WIDTH), jnp.float32)

    @pl.when((b == 0) & (s == 0))
    def _():
        carry[...] = jnp.zeros_like(carry)
        w_in[...] = w_in_f32[...].astype(jnp.bfloat16)

    chains = [_mixer_rows(r0, s * ts + r0, x_ref, lnin_g, lnin_b, w_in, conv_w, conv_b, pool_w, pool_scale,
                          w_out, ln1_g, ln1_b, w_r, b_r,
                          xp_ref, h1_ref, route_ref, route_t_ref, zbuf, vbuf, carry)
              for r0 in range(0, ts, SUB_TILE)]
    for t in range(MIXER_PHASES + len(chains) - 1):
        for k, chain in reversed(list(enumerate(chains))):
            if 0 <= t - k < MIXER_PHASES:
                next(chain, None)
    zbuf[0:CONV_HALO, :] = zbuf[ts:ts + CONV_HALO, :]
    vbuf[0:POOL_HALO, :] = vbuf[ts:ts + POOL_HALO, :]
    counts_ref[...] = jnp.transpose(carry[...])[0:1, :]


def _mixer_rows(r0, seq0, x_ref, lnin_g, lnin_b, w_in, conv_w, conv_b, pool_w, pool_scale, w_out,
                ln1_g, ln1_b, w_r, b_r,
                xp_ref, h1_ref, route_ref, route_t_ref, zbuf, vbuf, carry):
    n = SUB_TILE
    rows = pl.ds(r0, n)
    h0 = _layernorm(x_ref[rows, :], lnin_g[...], lnin_b[...])
    h0b = h0.astype(jnp.bfloat16)
    yield
    u_a = jnp.dot(h0b, w_in[:, 0:IN_PROJ // 2], preferred_element_type=jnp.float32)
    yield
    u_b = jnp.dot(h0b, w_in[:, IN_PROJ // 2:], preferred_element_type=jnp.float32)
    yield
    b_g = u_a[:, 0:CONV_WIDTH]
    c_g = u_a[:, CONV_WIDTH:]
    v_c = u_b[:, 0:CONV_WIDTH]
    v_p = u_b[:, CONV_WIDTH:]

    zbuf[pl.ds(CONV_HALO + r0, n), :] = c_g * v_c
    zext = zbuf[pl.ds(r0, n + CONV_HALO), :]
    z1 = pltpu.roll(zext, 1, axis=0)[CONV_HALO:, :]
    z2 = pltpu.roll(zext, 2, axis=0)[CONV_HALO:, :]
    conv = z2 * conv_w[0:1, :] + z1 * conv_w[1:2, :] + zext[CONV_HALO:, :] * conv_w[2:3, :] + conv_b[...]
    y_conv = b_g * conv

    vbuf[pl.ds(POOL_HALO + r0, n), :] = v_p
    vext = vbuf[pl.ds(r0, n + POOL_HALO), :]
    s2 = vext + pltpu.roll(vext, 1, axis=0)
    s4 = s2[:, POOL_GW:] + pltpu.roll(s2[:, POOL_GW:], 2, axis=0)
    s8 = s4[:, POOL_GW:] + pltpu.roll(s4[:, POOL_GW:], 4, axis=0)
    s16 = s8[:, POOL_GW:] + pltpu.roll(s8[:, POOL_GW:], 8, axis=0)
    wsums = (s2[POOL_HALO:, 0:POOL_GW], s4[POOL_HALO:, 0:POOL_GW],
             s8[POOL_HALO:, 0:POOL_GW], s16[POOL_HALO:, 0:POOL_GW])

    t_pos = seq0 + lax.broadcasted_iota(jnp.int32, (n, 1), 0)
    y_pool = []
    for j, w in enumerate(POOL_WINDOWS):
        inv_cnt = 1.0 / jnp.minimum(t_pos + 1, w).astype(jnp.float32)
        pooled = wsums[j] * inv_cnt - v_p[:, j * POOL_GW:(j + 1) * POOL_GW]
        y_pool.append(jnp.dot(pooled.astype(jnp.bfloat16), pool_w[j], preferred_element_type=jnp.float32))
    y_pool = jnp.concatenate(y_pool, axis=-1) * pool_scale[...]

    ycat = jnp.concatenate([y_conv, y_pool], axis=-1).astype(jnp.bfloat16)
    yield
    mix = jnp.dot(ycat, w_out[...], preferred_element_type=jnp.float32)
    yield
    h1 = _layernorm(DEEPNORM_ALPHA * h0 + mix, ln1_g[...], ln1_b[...])

    h_hi = h1.astype(jnp.bfloat16)
    h_lo = (h1 - h_hi.astype(jnp.float32)).astype(jnp.bfloat16)
    xp_ref[rows, :] = _pack_bf16_pairs(h1)
    h1_ref[rows, :] = h1
    hcat = jnp.concatenate([h_hi, h_lo, h_hi], axis=-1)
    yield
    logits = jnp.dot(hcat, w_r[...], preferred_element_type=jnp.float32) + b_r[...]
    yield
    lt = jnp.transpose(logits)[0:LOGIT_ROWS, :]
    rid = lax.broadcasted_iota(jnp.int32, (LOGIT_ROWS, n), 0).astype(jnp.float32)
    neg = jnp.float32(-jnp.inf)

    def first_argmax(vals):
        m = jnp.max(vals, axis=0, keepdims=True)
        idx = jnp.min(jnp.where(vals == m, rid, float(LOGIT_ROWS)), axis=0, keepdims=True)
        return m, idx

    g_mask = rid < N_GROUPS
    g_max, g_idx = first_argmax(jnp.where(g_mask, lt, neg))
    g_w = 1.0 / jnp.sum(jnp.where(g_mask, jnp.exp(lt - g_max), 0.0), axis=0, keepdims=True)

    e_lo = N_GROUPS + EXPERTS_PER_GROUP * g_idx
    e_vals = jnp.where((rid >= e_lo) & (rid < e_lo + EXPERTS_PER_GROUP), lt, neg)
    m1, i1 = first_argmax(e_vals)
    m2, i2 = first_argmax(jnp.where(rid == i1, neg, e_vals))
    e21 = jnp.exp(m2 - m1)
    w1 = g_w / (1.0 + e21)
    w2 = g_w * e21 / (1.0 + e21)
    id1 = i1 - N_GROUPS
    id2 = i2 - N_GROUPS

    eid = lax.broadcasted_iota(jnp.int32, (N_EXPERTS, n), 0).astype(jnp.float32)
    sel1 = eid == id1
    sel2 = eid == id2
    onehot = (sel1 | sel2).astype(jnp.float32)
    src = lax.broadcasted_iota(jnp.int32, (n, n), 0)
    dst = lax.broadcasted_iota(jnp.int32, (n, n), 1)
    earlier = (src < dst).astype(jnp.bfloat16)
    before = (jnp.dot(onehot.astype(jnp.bfloat16), earlier, preferred_element_type=jnp.float32)
              + carry[0:N_EXPERTS, 0:1])
    rank1 = jnp.sum(jnp.where(sel1, before, 0.0), axis=0, keepdims=True)
    rank2 = jnp.sum(jnp.where(sel2, before, 0.0), axis=0, keepdims=True)
    carry[0:N_EXPERTS, :] = carry[0:N_EXPERTS, :] + jnp.sum(onehot, axis=1, keepdims=True)

    rec_t = jnp.zeros((ROUTE_ROWS, n), jnp.float32)
    rec_row = lax.broadcasted_iota(jnp.int32, (ROUTE_ROWS, n), 0)
    for k, val in ((R_ID1, id1), (R_ID2, id2), (R_W1, w1), (R_W2, w2), (R_RANK1, rank1), (R_RANK2, rank2)):
        rec_t = jnp.where(rec_row == k, val, rec_t)
    route_t_ref[:, rows] = rec_t
    padded = jnp.concatenate([rec_t, jnp.zeros((LANES - ROUTE_ROWS, n), jnp.float32)], axis=0)
    route_ref[rows, :] = jnp.transpose(padded)


def _run_mixer(x2, lnin_g, lnin_b, w_in, conv_w, conv_b, pool_w, pool_scale, w_out, ln1_g, ln1_b,
               w_r, b_r, batch, seq):
    n_tok = batch * seq
    n_s = seq // SEQ_TILE
    tok_map = lambda b, s: (b * n_s + s, 0)

    def const(shape):
        return pl.BlockSpec(shape, lambda b, s: (0,) * len(shape), pipeline_mode=pl.Buffered(1))

    in_specs = [
        pl.BlockSpec((SEQ_TILE, D_MODEL), tok_map),
        const((1, D_MODEL)), const((1, D_MODEL)),
        const((D_MODEL, IN_PROJ)),
        const((CONV_K, CONV_WIDTH)), const((1, CONV_WIDTH)),
        const((len(POOL_WINDOWS), POOL_GW, POOL_GW)), const((1, POOL_WIDTH)),
        const((D_MODEL, D_MODEL)),
        const((1, D_MODEL)), const((1, D_MODEL)),
        const((3 * D_MODEL, LANES)), const((1, LANES)),
    ]
    out_specs = [
        pl.BlockSpec((SEQ_TILE, HALF), tok_map),
        pl.BlockSpec((SEQ_TILE, D_MODEL), tok_map),
        pl.BlockSpec((SEQ_TILE, LANES), tok_map),
        pl.BlockSpec((ROUTE_ROWS, SEQ_TILE), lambda b, s: (0, b * n_s + s)),
        pl.BlockSpec((1, LANES), lambda b, s: (0, 0)),
    ]
    out_shape = [
        jax.ShapeDtypeStruct((n_tok, HALF), jnp.uint32),
        jax.ShapeDtypeStruct((n_tok, D_MODEL), jnp.float32),
        jax.ShapeDtypeStruct((n_tok, LANES), jnp.float32),
        jax.ShapeDtypeStruct((ROUTE_ROWS, n_tok), jnp.float32),
        jax.ShapeDtypeStruct((1, LANES), jnp.float32),
    ]
    return pl.pallas_call(
        _mixer_kernel,
        grid=(batch, n_s),
        in_specs=in_specs,
        out_specs=out_specs,
        out_shape=out_shape,
        scratch_shapes=[
            pltpu.VMEM((SEQ_TILE + CONV_HALO, CONV_WIDTH), jnp.float32),
            pltpu.VMEM((SEQ_TILE + POOL_HALO, POOL_WIDTH), jnp.float32),
            pltpu.VMEM((LANES, LANES), jnp.float32),
            pltpu.VMEM((D_MODEL, IN_PROJ), jnp.bfloat16),
        ],
        compiler_params=pltpu.CompilerParams(
            dimension_semantics=("arbitrary", "arbitrary"), vmem_limit_bytes=VMEM_LIMIT),
    )(x2, lnin_g, lnin_b, w_in, conv_w, conv_b, pool_w, pool_scale, w_out, ln1_g, ln1_b, w_r, b_r)


def _plan_kernel(rt_ref, counts_ref, pos_ref, tile_start_ref, tiles_ref, pad_ref, *, n_rows):
    lane = lax.broadcasted_iota(jnp.int32, (ROUTE_ROWS, LANES), 1)
    counts = jnp.broadcast_to(counts_ref[...], (ROUTE_ROWS, LANES))
    tiles = jnp.floor((counts + (ROW_TILE - 1)) * (1.0 / ROW_TILE))
    tile_end = tiles
    shift = 1
    while shift < N_EXPERTS:
        tile_end = tile_end + jnp.where(lane >= shift, pltpu.roll(tile_end, shift, axis=1), 0.0)
        shift *= 2
    row_start = (tile_end - tiles) * ROW_TILE
    tile_start_ref[...] = (tile_end - tiles)[0:1, :].astype(jnp.int32)
    tiles_ref[...] = tiles[0:1, :].astype(jnp.int32)

    rt = rt_ref[...]
    ids = rt[R_ID1:R_ID2 + 1, :]
    start = jnp.zeros_like(ids)
    for e in range(N_EXPERTS):
        start = jnp.where(ids == e, row_start[0:1, e:e + 1], start)
    pos_ref[...] = (start + rt[R_RANK1:R_RANK2 + 1, :]).astype(jnp.int32)

    sub = lax.broadcasted_iota(jnp.int32, (N_EXPERTS, LANES), 0)
    lane_e = lax.broadcasted_iota(jnp.int32, (N_EXPERTS, LANES), 1)
    diag = sub == lane_e
    pad_lo = jnp.sum(jnp.where(diag, (row_start + counts)[0:1, :], 0.0), axis=1, keepdims=True)
    pad_n = jnp.sum(jnp.where(diag, (tiles * ROW_TILE - counts)[0:1, :], 0.0), axis=1, keepdims=True)
    j = lax.broadcasted_iota(jnp.int32, (N_EXPERTS, ROW_TILE), 1).astype(jnp.float32)
    pad_ref[...] = jnp.where(j < pad_n, pad_lo + j, n_rows + j).astype(jnp.int32)


def _run_plan(route_t, counts, n_rows):
    n_tok = route_t.shape[1]
    full = lambda shape: pl.BlockSpec(shape, lambda i: (0, 0))
    return pl.pallas_call(
        functools.partial(_plan_kernel, n_rows=n_rows),
        grid=(1,),
        in_specs=[full((ROUTE_ROWS, n_tok)), full((1, LANES))],
        out_specs=[full((2, n_tok)), full((1, LANES)), full((1, LANES)), full((N_EXPERTS, ROW_TILE))],
        out_shape=[
            jax.ShapeDtypeStruct((2, n_tok), jnp.int32),
            jax.ShapeDtypeStruct((1, LANES), jnp.int32),
            jax.ShapeDtypeStruct((1, LANES), jnp.int32),
            jax.ShapeDtypeStruct((N_EXPERTS, ROW_TILE), jnp.int32),
        ],
        compiler_params=pltpu.CompilerParams(dimension_semantics=("arbitrary",)),
    )(route_t, counts)


def _sc_mesh():
    return plsc.VectorSubcoreMesh(core_axis_name="core", subcore_axis_name="subcore")


def _sc_worker_id():
    return lax.axis_index("core") * (SC_WORKERS // 2) + lax.axis_index("subcore")


def _dispatch_rows(xp, pos1, pos2, pad_pos, zero_rows, n_rows):
    n_tok, width = xp.shape
    n_win = n_tok // SC_WORKERS // SC_WINDOW
    n_pad = pad_pos.shape[0] // SC_WORKERS // SC_WINDOW
    as_windows = lambda v: v.reshape(-1, SC_WINDOW)

    @functools.partial(
        pl.kernel, out_type=jax.ShapeDtypeStruct((n_rows + ROW_TILE, width), xp.dtype), mesh=_sc_mesh(),
        scratch_types=[pltpu.VMEM((n_win, SC_WINDOW), jnp.int32), pltpu.VMEM((n_win, SC_WINDOW), jnp.int32),
                       pltpu.VMEM((n_pad, SC_WINDOW), jnp.int32),
                       pltpu.VMEM((2, SC_WINDOW, width), xp.dtype), pltpu.VMEM((SC_WINDOW, width), xp.dtype),
                       pltpu.SemaphoreType.DMA((2,)), pltpu.SemaphoreType.DMA((2,)), pltpu.SemaphoreType.DMA])
    def dispatch(xp_hbm, pos1_hbm, pos2_hbm, pad_hbm, zero_hbm, out_hbm,
                 idx1, idx2, idxp, buf, zbuf, lsem, ssem, psem):
        wid = _sc_worker_id()
        pltpu.sync_copy(pad_hbm.at[pl.ds(wid * n_pad, n_pad)], idxp)
        pltpu.sync_copy(zero_hbm, zbuf)
        pads = [pltpu.make_async_copy(zbuf, out_hbm.at[idxp.at[j]], psem) for j in range(n_pad)]
        for cp in pads:
            cp.start()
        pltpu.sync_copy(pos1_hbm.at[pl.ds(wid * n_win, n_win)], idx1)
        pltpu.sync_copy(pos2_hbm.at[pl.ds(wid * n_win, n_win)], idx2)

        def load(j):
            rows = xp_hbm.at[pl.ds((wid * n_win + j) * SC_WINDOW, SC_WINDOW)]
            return pltpu.make_async_copy(rows, buf.at[j % 2], lsem.at[j % 2])

        def scatters(j):
            return [pltpu.make_async_copy(buf.at[j % 2], out_hbm.at[idx.at[j]], ssem.at[j % 2])
                    for idx in (idx1, idx2)]

        load(0).start()
        for j in range(n_win):
            load(j).wait()
            for cp in scatters(j):
                cp.start()
            if j >= 1:
                for cp in scatters(j - 1):
                    cp.wait()
            if j + 1 < n_win:
                load(j + 1).start()
        for cp in scatters(n_win - 1):
            cp.wait()
        for cp in pads:
            cp.wait()

    return dispatch(xp, as_windows(pos1), as_windows(pos2), as_windows(pad_pos), zero_rows)


def _gather_rows(src, idx):
    n_out, width = idx.shape[0], src.shape[1]
    n_win = n_out // SC_WORKERS // GATHER_WINDOW
    nbuf = GATHER_BUFFERS

    @functools.partial(
        pl.kernel, out_type=jax.ShapeDtypeStruct((n_out, width), src.dtype), mesh=_sc_mesh(),
        scratch_types=[pltpu.VMEM((n_win, GATHER_WINDOW), jnp.int32),
                       pltpu.VMEM((nbuf, GATHER_WINDOW, width), src.dtype),
                       pltpu.SemaphoreType.DMA((nbuf,)), pltpu.SemaphoreType.DMA((nbuf,))])
    def gather(src_hbm, idx_hbm, dst_hbm, idx_v, buf, gsem, ssem):
        wid = _sc_worker_id()
        pltpu.sync_copy(idx_hbm.at[pl.ds(wid * n_win, n_win)], idx_v)

        def fetch(j):
            return pltpu.make_async_copy(src_hbm.at[idx_v.at[j]], buf.at[j % nbuf], gsem.at[j % nbuf])

        def store(j):
            rows = dst_hbm.at[pl.ds((wid * n_win + j) * GATHER_WINDOW, GATHER_WINDOW)]
            return pltpu.make_async_copy(buf.at[j % nbuf], rows, ssem.at[j % nbuf])

        for j in range(min(nbuf - 1, n_win)):
            fetch(j).start()
        for j in range(n_win):
            fetch(j).wait()
            store(j).start()
            if j + nbuf - 1 < n_win:
                if j >= 1:
                    store(j - 1).wait()
                fetch(j + nbuf - 1).start()
        for j in range(max(0, n_win - nbuf), n_win):
            store(j).wait()

    return gather(src, idx.reshape(-1, GATHER_WINDOW))


def _tile_copy(hbm, buf, sem, tile, slot, to_hbm):
    rows = hbm.at[pl.ds(pl.multiple_of(tile * ROW_TILE, ROW_TILE), ROW_TILE)]
    if to_hbm:
        return pltpu.make_async_copy(buf.at[slot], rows, sem.at[slot])
    return pltpu.make_async_copy(rows, buf.at[slot], sem.at[slot])


def _weight_copies(w_hbm, wbuf, wsem, expert):
    slot = expert % WEIGHT_BUFFERS
    return [pltpu.make_async_copy(w.at[expert], buf.at[slot], wsem.at[slot]) for w, buf in zip(w_hbm, wbuf)]


def _expert_kernel(ts_ref, nte_ref, x_hbm, wg_hbm, wu_hbm, wd_hbm, y_hbm,
                   xbuf, ybuf, xsem, ysem, wg_buf, wu_buf, wd_buf, wsem, wgu_bf, wd_bf, act_ref, *, max_tiles):
    e = pl.program_id(0)
    first = ts_ref[e]
    count = nte_ref[e]
    n_tiles = ts_ref[N_EXPERTS - 1] + nte_ref[N_EXPERTS - 1]
    ahead = TILE_BUFFERS - 2
    w_hbm = (wg_hbm, wu_hbm, wd_hbm)
    wbuf = (wg_buf, wu_buf, wd_buf)

    @pl.when(e == 0)
    def _():
        for g in range(ahead):
            @pl.when(g < n_tiles)
            def _():
                _tile_copy(x_hbm, xbuf, xsem, g, g, False).start()
        for k in range(WEIGHT_BUFFERS - 1):
            for cp in _weight_copies(w_hbm, wbuf, wsem, k):
                cp.start()

    @pl.when(e + WEIGHT_BUFFERS - 1 < N_EXPERTS)
    def _():
        for cp in _weight_copies(w_hbm, wbuf, wsem, e + WEIGHT_BUFFERS - 1):
            cp.start()

    for cp in _weight_copies(w_hbm, wbuf, wsem, e):
        cp.wait()

    def sync_x(g):
        _tile_copy(x_hbm, xbuf, xsem, g, g % TILE_BUFFERS, False).wait()

        @pl.when(g + ahead < n_tiles)
        def _():
            _tile_copy(x_hbm, xbuf, xsem, g + ahead, (g + ahead) % TILE_BUFFERS, False).start()

    def sync_y_slot(g):
        @pl.when(g >= TILE_BUFFERS)
        def _():
            _tile_copy(y_hbm, ybuf, ysem, g - TILE_BUFFERS, g % TILE_BUFFERS, True).wait()

    def up_proj(g):
        lo, hi = _unpack_bf16_pairs(xbuf[g % TILE_BUFFERS])
        return (jnp.dot(lo.astype(jnp.bfloat16), wgu_bf[0:HALF, :], preferred_element_type=jnp.float32)
                + jnp.dot(hi.astype(jnp.bfloat16), wgu_bf[HALF:, :], preferred_element_type=jnp.float32))

    def put_act(hgu, which):
        hg = hgu[:, 0:D_EXPERT]
        act_ref[which] = (hg * jax.nn.sigmoid(hg) * hgu[:, D_EXPERT:]).astype(jnp.bfloat16)

    def down_proj(g, which):
        y = jnp.dot(act_ref[which], wd_bf[...], preferred_element_type=jnp.float32)
        ybuf[g % TILE_BUFFERS] = _pack_bf16_pairs(y)

    def start_y(g):
        _tile_copy(y_hbm, ybuf, ysem, g, g % TILE_BUFFERS, True).start()

    @pl.when(count > 0)
    def _():
        slot = e % WEIGHT_BUFFERS
        wgu_bf[:, 0:D_EXPERT] = wg_buf[slot].astype(jnp.bfloat16)
        wgu_bf[:, D_EXPERT:] = wu_buf[slot].astype(jnp.bfloat16)
        wd_bf[...] = wd_buf[slot].astype(jnp.bfloat16)

        sync_x(first)
        put_act(up_proj(first), 0)

        def pair_body(i, c):
            g = first + 1 + 2 * i
            sync_x(g)
            sync_x(g + 1)
            sync_y_slot(g - 1)
            sync_y_slot(g)
            down_proj(g - 1, 0)
            hgu0 = up_proj(g)
            put_act(hgu0, 1)
            hgu1 = up_proj(g + 1)
            down_proj(g, 1)
            put_act(hgu1, 0)
            start_y(g - 1)
            start_y(g)
            return c

        n_pairs = (count - 1) // 2
        lax.fori_loop(0, n_pairs, pair_body, 0)

        @pl.when((count - 1) % 2 == 1)
        def _():
            g = first + count - 1
            sync_x(g)
            sync_y_slot(g - 1)
            down_proj(g - 1, 0)
            put_act(up_proj(g), 0)
            start_y(g - 1)

        last = first + count - 1
        sync_y_slot(last)
        down_proj(last, 0)
        start_y(last)

    @pl.when(e == N_EXPERTS - 1)
    def _():
        for k in range(TILE_BUFFERS, 0, -1):
            @pl.when(n_tiles >= k)
            def _():
                _tile_copy(y_hbm, ybuf, ysem, n_tiles - k, (n_tiles - k) % TILE_BUFFERS, True).wait()

        ybuf[0] = jnp.zeros((ROW_TILE, HALF), jnp.uint32)

        def fill(g, c):
            cp = _tile_copy(y_hbm, ybuf, ysem, g, 0, True)
            cp.start()
            cp.wait()
            return c

        lax.fori_loop(n_tiles, max_tiles, fill, 0)


def _run_experts(tile_start, tiles_per_expert, x_sorted, w_gate, w_up, w_down):
    n_rows = x_sorted.shape[0]
    max_tiles = n_rows // ROW_TILE
    hbm = pl.BlockSpec(memory_space=pl.ANY)

    grid_spec = pltpu.PrefetchScalarGridSpec(
        num_scalar_prefetch=2,
        grid=(N_EXPERTS,),
        in_specs=[hbm, hbm, hbm, hbm],
        out_specs=hbm,
        scratch_shapes=[
            pltpu.VMEM((TILE_BUFFERS, ROW_TILE, HALF), jnp.uint32),
            pltpu.VMEM((TILE_BUFFERS, ROW_TILE, HALF), jnp.uint32),
            pltpu.SemaphoreType.DMA((TILE_BUFFERS,)),
            pltpu.SemaphoreType.DMA((TILE_BUFFERS,)),
            pltpu.VMEM((WEIGHT_BUFFERS, D_MODEL, D_EXPERT), jnp.float32),
            pltpu.VMEM((WEIGHT_BUFFERS, D_MODEL, D_EXPERT), jnp.float32),
            pltpu.VMEM((WEIGHT_BUFFERS, D_EXPERT, D_MODEL), jnp.float32),
            pltpu.SemaphoreType.DMA((WEIGHT_BUFFERS,)),
            pltpu.VMEM((D_MODEL, 2 * D_EXPERT), jnp.bfloat16),
            pltpu.VMEM((D_EXPERT, D_MODEL), jnp.bfloat16),
            pltpu.VMEM((2, ROW_TILE, D_EXPERT), jnp.bfloat16),
        ],
    )
    return pl.pallas_call(
        functools.partial(_expert_kernel, max_tiles=max_tiles),
        grid_spec=grid_spec,
        out_shape=jax.ShapeDtypeStruct((n_rows, HALF), jnp.uint32),
        compiler_params=pltpu.CompilerParams(
            dimension_semantics=("arbitrary",), vmem_limit_bytes=VMEM_LIMIT),
    )(tile_start, tiles_per_expert, x_sorted, w_gate, w_up, w_down)


def _combine_rows(r0, h1_ref, p_ref, route_ref, ya_ref, yb_ref, w_pg, b_pg, w_ple, g_ref, b_ref, o_ref):
    rows = pl.ds(r0, SUB_TILE)
    h1 = h1_ref[rows, :]
    h_hi = h1.astype(jnp.bfloat16)
    p_b = p_ref[rows, :].astype(jnp.bfloat16)
    yield
    gate_pre = jnp.dot(h_hi, w_pg[...], preferred_element_type=jnp.float32)
    ple_pre = jnp.dot(p_b, w_ple[...], preferred_element_type=jnp.float32)
    yield
    ple = ple_pre * jax.nn.sigmoid(gate_pre + b_pg[...])
    route = route_ref[rows, :]
    w1 = route[:, R_W1:R_W1 + 1]
    w2 = route[:, R_W2:R_W2 + 1]
    a_lo, a_hi = _unpack_bf16_pairs(ya_ref[rows, :])
    b_lo, b_hi = _unpack_bf16_pairs(yb_ref[rows, :])
    moe = jnp.concatenate([w1 * a_lo + w2 * b_lo, w1 * a_hi + w2 * b_hi], axis=-1)
    o_ref[rows, :] = _layernorm(DEEPNORM_ALPHA * h1 + ple + moe, g_ref[...], b_ref[...])


def _combine_kernel(h1_ref, p_ref, route_ref, ya_ref, yb_ref, w_pg, b_pg, w_ple, g_ref, b_ref, o_ref):
    chains = [_combine_rows(r0, h1_ref, p_ref, route_ref, ya_ref, yb_ref, w_pg, b_pg, w_ple, g_ref, b_ref, o_ref)
              for r0 in range(0, h1_ref.shape[0], SUB_TILE)]
    for t in range(COMBINE_PHASES + len(chains) - 1):
        for k, chain in reversed(list(enumerate(chains))):
            if 0 <= t - k < COMBINE_PHASES:
                next(chain, None)


def _run_combine(h1, p2, route, y_tok, w_pg, b_pg, w_ple, ln2_g, ln2_b):
    n_tok = h1.shape[0]
    n_t = n_tok // TOKEN_TILE
    tok_map = lambda i: (i, 0)
    const = lambda shape: pl.BlockSpec(shape, lambda i: (0, 0), pipeline_mode=pl.Buffered(1))
    in_specs = [
        pl.BlockSpec((TOKEN_TILE, D_MODEL), tok_map),
        pl.BlockSpec((TOKEN_TILE, PLE_DIM), tok_map),
        pl.BlockSpec((TOKEN_TILE, LANES), tok_map),
        pl.BlockSpec((TOKEN_TILE, HALF), lambda i: (i, 0)),
        pl.BlockSpec((TOKEN_TILE, HALF), lambda i: (i + n_t, 0)),
        const((D_MODEL, D_MODEL)), const((1, D_MODEL)), const((PLE_DIM, D_MODEL)),
        const((1, D_MODEL)), const((1, D_MODEL)),
    ]
    return pl.pallas_call(
        _combine_kernel,
        grid=(n_t,),
        in_specs=in_specs,
        out_specs=pl.BlockSpec((TOKEN_TILE, D_MODEL), tok_map),
        out_shape=jax.ShapeDtypeStruct((n_tok, D_MODEL), jnp.float32),
        compiler_params=pltpu.CompilerParams(dimension_semantics=("arbitrary",), vmem_limit_bytes=VMEM_LIMIT),
    )(h1, p2, route, y_tok, y_tok, w_pg, b_pg, w_ple, ln2_g, ln2_b)


def _split_bf16(w):
    hi = w.astype(jnp.bfloat16)
    lo = (w - hi.astype(jnp.float32)).astype(jnp.bfloat16)
    return hi, lo


def kernel(x, p, ln_in_g, ln_in_b, w_in, conv_w, conv_b, pool_w, pool_scale, w_out, ln1_g, ln1_b,
           w_rg, b_rg, w_re, b_re, w_gate, w_up, w_down, w_pg, b_pg, w_ple, ln2_g, ln2_b):
    batch, seq, _ = x.shape
    n_tok = batch * seq
    bf = jnp.bfloat16
    row = lambda v: v.reshape(1, -1)

    w_r = jnp.concatenate([w_rg[0], jnp.transpose(w_re[0], (1, 0, 2)).reshape(D_MODEL, N_EXPERTS)], axis=1)
    w_r = jnp.pad(w_r, ((0, 0), (0, LANES - w_r.shape[1])))
    w_r_hi, w_r_lo = _split_bf16(w_r)
    w_r_cat = jnp.concatenate([w_r_hi, w_r_hi, w_r_lo], axis=0)
    b_r = jnp.pad(jnp.concatenate([b_rg[0], b_re[0].reshape(-1)]), (0, LANES - N_GROUPS - N_EXPERTS)).reshape(1, LANES)

    x2 = x.reshape(n_tok, D_MODEL)
    p2 = p[0].reshape(n_tok, PLE_DIM)
    mixer_weights = (row(ln_in_g), row(ln_in_b), w_in[0], conv_w[0], row(conv_b[0]),
                     pool_w[0].astype(bf), row(pool_scale[0]), w_out[0].astype(bf), row(ln1_g[0]), row(ln1_b[0]),
                     w_r_cat, b_r)
    expert_weights =(w_gate[0].reshape(N_EXPERTS, D_MODEL, D_EXPERT),
                      w_up[0].reshape(N_EXPERTS, D_MODEL, D_EXPERT),
                      w_down[0].reshape(N_EXPERTS, D_EXPERT, D_MODEL))
    zero_rows = jnp.zeros((SC_WINDOW, HALF), jnp.uint32)

    n_rows = -(-(2 * n_tok + N_EXPERTS * (ROW_TILE - 1)) // ROW_TILE) * ROW_TILE
    xp, h1, route, route_t, counts = _run_mixer(x2, *mixer_weights, batch, seq)
    w_pg_late, w_ple_late, xp = lax.optimization_barrier((w_pg[0], w_ple[0], xp))
    combine_weights = (w_pg_late.astype(bf), row(b_pg[0]), w_ple_late.astype(bf), row(ln2_g[0]), row(ln2_b[0]))
    pos, tile_start, tiles_per_expert, pad_pos = _run_plan(route_t, counts, n_rows)
    x_sorted = _dispatch_rows(xp, pos[0], pos[1], pad_pos.reshape(-1), zero_rows, n_rows)
    y_sorted = _run_experts(tile_start[0, :N_EXPERTS], tiles_per_expert[0, :N_EXPERTS], x_sorted, *expert_weights)

    y_tok = _gather_rows(y_sorted, pos.reshape(-1))
    out = _run_combine(h1, p2, route, y_tok, *combine_weights)
    return out.reshape(batch, seq, D_MODEL)
```

```python
import functools

import jax
import jax.numpy as jnp
from jax import lax
from jax.experimental import pallas as pl
from jax.experimental.pallas import tpu as pltpu
from jax.experimental.pallas import tpu_sc as plsc

D_MODEL = 1024
CONV_WIDTH = 512
CONV_K = 3
POOL_WIDTH = 512
POOL_WINDOWS = (2, 4, 8, 16)
POOL_GW = 128
IN_PROJ = 3 * CONV_WIDTH + POOL_WIDTH
N_GROUPS = 4
EXPERTS_PER_GROUP = 8
N_EXPERTS = N_GROUPS * EXPERTS_PER_GROUP
D_EXPERT = 256
PLE_DIM = 256
LN_EPS = 1e-5
DEEPNORM_ALPHA = 2.0 ** 0.25

LANES = 128
HALF = D_MODEL // 2
CONV_HALO = 8
POOL_HALO = 16
SEQ_TILE = 1024
SUB_TILE = 256
MIXER_PHASES = 8
COMBINE_PHASES = 3
ROW_TILE = 256
TILE_BUFFERS = 8
SC_WORKERS = 32
SC_WINDOW = 64
GATHER_WINDOW = 32
GATHER_BUFFERS = 4
WEIGHT_BUFFERS = 3
TOKEN_TILE = 1024
VMEM_LIMIT = 56 * 1024 * 1024

R_ID1, R_ID2, R_RANK1, R_RANK2, R_W1, R_W2 = range(6)
ROUTE_ROWS = 8
LOGIT_ROWS = 40


def _layernorm(x, g, b):
    mu = jnp.mean(x, axis=-1, keepdims=True)
    xc = x - mu
    var = jnp.mean(xc * xc, axis=-1, keepdims=True)
    return xc * lax.rsqrt(var + LN_EPS) * g + b


def _pack_bf16_pairs(v):
    bits = lax.bitcast_convert_type(v.astype(jnp.bfloat16).astype(jnp.float32), jnp.uint32)
    return bits[:, HALF:] | (bits[:, :HALF] >> 16)


def _unpack_bf16_pairs(w):
    lo = lax.bitcast_convert_type(w << 16, jnp.float32)
    hi = lax.bitcast_convert_type(w & jnp.uint32(0xFFFF0000), jnp.float32)
    return lo, hi


def _mixer_kernel(x_ref, lnin_g, lnin_b, w_in_f32, conv_w, conv_b, pool_w_f32, pool_scale, w_out_f32,
                  ln1_g, ln1_b, w_r, b_r,
                  xp_ref, h1_ref, route_ref, route_t_ref, counts_ref,
                  zbuf, vbuf, carry, w_in, pool_w, w_out):
    b = pl.program_id(0)
    s = pl.program_id(1)
    ts = x_ref.shape[0]

    @pl.when(s == 0)
    def _():
        zbuf[0:CONV_HALO, :] = jnp.zeros((CONV_HALO, CONV_WIDTH), jnp.float32)
        vbuf[0:POOL_HALO, :] = jnp.zeros((POOL_HALO, POOL_WIDTH), jnp.float32)

    @pl.when((b == 0) & (s == 0))
    def _():
        carry[...] = jnp.zeros_like(carry)
        w_in[...] = w_in_f32[...].astype(jnp.bfloat16)
        pool_w[...] = pool_w_f32[...].astype(jnp.bfloat16)
        w_out[...] = w_out_f32[...].astype(jnp.bfloat16)

    chains = [_mixer_rows(r0, s * ts + r0, x_ref, lnin_g, lnin_b, w_in, conv_w, conv_b, pool_w, pool_scale,
                          w_out, ln1_g, ln1_b, w_r, b_r,
                          xp_ref, h1_ref, route_ref, route_t_ref, zbuf, vbuf, carry)
              for r0 in range(0, ts, SUB_TILE)]
    for t in range(MIXER_PHASES + len(chains) - 1):
        for k, chain in reversed(list(enumerate(chains))):
            if 0 <= t - k < MIXER_PHASES:
                next(chain, None)
    zbuf[0:CONV_HALO, :] = zbuf[ts:ts + CONV_HALO, :]
    vbuf[0:POOL_HALO, :] = vbuf[ts:ts + POOL_HALO, :]
    counts_ref[...] = jnp.transpose(carry[...])[0:1, :]


def _mixer_rows(r0, seq0, x_ref, lnin_g, lnin_b, w_in, conv_w, conv_b, pool_w, pool_scale, w_out,
                ln1_g, ln1_b, w_r, b_r,
                xp_ref, h1_ref, route_ref, route_t_ref, zbuf, vbuf, carry):
    n = SUB_TILE
    rows = pl.ds(r0, n)
    h0 = _layernorm(x_ref[rows, :], lnin_g[...], lnin_b[...])
    h0b = h0.astype(jnp.bfloat16)
    yield
    u_a = jnp.dot(h0b, w_in[:, 0:IN_PROJ // 2], preferred_element_type=jnp.float32)
    yield
    u_b = jnp.dot(h0b, w_in[:, IN_PROJ // 2:], preferred_element_type=jnp.float32)
    yield
    b_g = u_a[:, 0:CONV_WIDTH]
    c_g = u_a[:, CONV_WIDTH:]
    v_c = u_b[:, 0:CONV_WIDTH]
    v_p = u_b[:, CONV_WIDTH:]

    zbuf[pl.ds(CONV_HALO + r0, n), :] = c_g * v_c
    zext = zbuf[pl.ds(r0, n + CONV_HALO), :]
    z1 = pltpu.roll(zext, 1, axis=0)[CONV_HALO:, :]
    z2 = pltpu.roll(zext, 2, axis=0)[CONV_HALO:, :]
    conv = z2 * conv_w[0:1, :] + z1 * conv_w[1:2, :] + zext[CONV_HALO:, :] * conv_w[2:3, :] + conv_b[...]
    y_conv = b_g * conv

    vbuf[pl.ds(POOL_HALO + r0, n), :] = v_p
    vext = vbuf[pl.ds(r0, n + POOL_HALO), :]
    s2 = vext + pltpu.roll(vext, 1, axis=0)
    s4 = s2[:, POOL_GW:] + pltpu.roll(s2[:, POOL_GW:], 2, axis=0)
    s8 = s4[:, POOL_GW:] + pltpu.roll(s4[:, POOL_GW:], 4, axis=0)
    s16 = s8[:, POOL_GW:] + pltpu.roll(s8[:, POOL_GW:], 8, axis=0)
    wsums = (s2[POOL_HALO:, 0:POOL_GW], s4[POOL_HALO:, 0:POOL_GW],
             s8[POOL_HALO:, 0:POOL_GW], s16[POOL_HALO:, 0:POOL_GW])

    t_pos = seq0 + lax.broadcasted_iota(jnp.int32, (n, 1), 0)
    y_pool = []
    for j, w in enumerate(POOL_WINDOWS):
        inv_cnt = 1.0 / jnp.minimum(t_pos + 1, w).astype(jnp.float32)
        pooled = wsums[j] * inv_cnt - v_p[:, j * POOL_GW:(j + 1) * POOL_GW]
        y_pool.append(jnp.dot(pooled.astype(jnp.bfloat16), pool_w[j], preferred_element_type=jnp.float32))
    y_pool = jnp.concatenate(y_pool, axis=-1) * pool_scale[...]

    ycat = jnp.concatenate([y_conv, y_pool], axis=-1).astype(jnp.bfloat16)
    yield
    mix = jnp.dot(ycat, w_out[...], preferred_element_type=jnp.float32)
    yield
    h1 = _layernorm(DEEPNORM_ALPHA * h0 + mix, ln1_g[...], ln1_b[...])

    h_hi = h1.astype(jnp.bfloat16)
    h_lo = (h1 - h_hi.astype(jnp.float32)).astype(jnp.bfloat16)
    xp_ref[rows, :] = _pack_bf16_pairs(h1)
    h1_ref[rows, :] = h1
    hcat = jnp.concatenate([h_hi, h_lo, h_hi], axis=-1)
    yield
    logits = jnp.dot(hcat, w_r[...], preferred_element_type=jnp.float32) + b_r[...]
    yield
    lt = jnp.transpose(logits)[0:LOGIT_ROWS, :]
    rid = lax.broadcasted_iota(jnp.int32, (LOGIT_ROWS, n), 0).astype(jnp.float32)
    neg = jnp.float32(-jnp.inf)

    def first_argmax(vals):
        m = jnp.max(vals, axis=0, keepdims=True)
        idx = jnp.min(jnp.where(vals == m, rid, float(LOGIT_ROWS)), axis=0, keepdims=True)
        return m, idx

    g_mask = rid < N_GROUPS
    g_max, g_idx = first_argmax(jnp.where(g_mask, lt, neg))
    g_w = 1.0 / jnp.sum(jnp.where(g_mask, jnp.exp(lt - g_max), 0.0), axis=0, keepdims=True)

    e_lo = N_GROUPS + EXPERTS_PER_GROUP * g_idx
    e_vals = jnp.where((rid >= e_lo) & (rid < e_lo + EXPERTS_PER_GROUP), lt, neg)
    m1, i1 = first_argmax(e_vals)
    m2, i2 = first_argmax(jnp.where(rid == i1, neg, e_vals))
    e21 = jnp.exp(m2 - m1)
    w1 = g_w / (1.0 + e21)
    w2 = g_w * e21 / (1.0 + e21)
    id1 = i1 - N_GROUPS
    id2 = i2 - N_GROUPS

    eid = lax.broadcasted_iota(jnp.int32, (N_EXPERTS, n), 0).astype(jnp.float32)
    sel1 = eid == id1
    sel2 = eid == id2
    onehot = (sel1 | sel2).astype(jnp.float32)
    src = lax.broadcasted_iota(jnp.int32, (n, n), 0)
    dst = lax.broadcasted_iota(jnp.int32, (n, n), 1)
    earlier = (src < dst).astype(jnp.bfloat16)
    before = (jnp.dot(onehot.astype(jnp.bfloat16), earlier, preferred_element_type=jnp.float32)
              + carry[0:N_EXPERTS, 0:1])
    rank1 = jnp.sum(jnp.where(sel1, before, 0.0), axis=0, keepdims=True)
    rank2 = jnp.sum(jnp.where(sel2, before, 0.0), axis=0, keepdims=True)
    carry[0:N_EXPERTS, :] = carry[0:N_EXPERTS, :] + jnp.sum(onehot, axis=1, keepdims=True)

    rec_t = jnp.zeros((ROUTE_ROWS, n), jnp.float32)
    rec_row = lax.broadcasted_iota(jnp.int32, (ROUTE_ROWS, n), 0)
    for k, val in ((R_ID1, id1), (R_ID2, id2), (R_W1, w1), (R_W2, w2), (R_RANK1, rank1), (R_RANK2, rank2)):
        rec_t = jnp.where(rec_row == k, val, rec_t)
    route_t_ref[:, rows] = rec_t
    padded = jnp.concatenate([rec_t, jnp.zeros((LANES - ROUTE_ROWS, n), jnp.float32)], axis=0)
    route_ref[rows, :] = jnp.transpose(padded)


def _run_mixer(x2, lnin_g, lnin_b, w_in, conv_w, conv_b, pool_w, pool_scale, w_out, ln1_g, ln1_b,
               w_r, b_r, batch, seq):
    n_tok = batch * seq
    n_s = seq // SEQ_TILE
    tok_map = lambda b, s: (b * n_s + s, 0)

    def const(shape):
        return pl.BlockSpec(shape, lambda b, s: (0,) * len(shape), pipeline_mode=pl.Buffered(1))

    in_specs = [
        pl.BlockSpec((SEQ_TILE, D_MODEL), tok_map),
        const((1, D_MODEL)), const((1, D_MODEL)),
        const((D_MODEL, IN_PROJ)),
        const((CONV_K, CONV_WIDTH)), const((1, CONV_WIDTH)),
        const((len(POOL_WINDOWS), POOL_GW, POOL_GW)), const((1, POOL_WIDTH)),
        const((D_MODEL, D_MODEL)),
        const((1, D_MODEL)), const((1, D_MODEL)),
        const((3 * D_MODEL, LANES)), const((1, LANES)),
    ]
    out_specs = [
        pl.BlockSpec((SEQ_TILE, HALF), tok_map),
        pl.BlockSpec((SEQ_TILE, D_MODEL), tok_map),
        pl.BlockSpec((SEQ_TILE, LANES), tok_map),
        pl.BlockSpec((ROUTE_ROWS, SEQ_TILE), lambda b, s: (0, b * n_s + s)),
        pl.BlockSpec((1, LANES), lambda b, s: (0, 0)),
    ]
    out_shape = [
        jax.ShapeDtypeStruct((n_tok, HALF), jnp.uint32),
        jax.ShapeDtypeStruct((n_tok, D_MODEL), jnp.float32),
        jax.ShapeDtypeStruct((n_tok, LANES), jnp.float32),
        jax.ShapeDtypeStruct((ROUTE_ROWS, n_tok), jnp.float32),
        jax.ShapeDtypeStruct((1, LANES), jnp.float32),
    ]
    return pl.pallas_call(
        _mixer_kernel,
        grid=(batch, n_s),
        in_specs=in_specs,
        out_specs=out_specs,
        out_shape=out_shape,
        scratch_shapes=[
            pltpu.VMEM((SEQ_TILE + CONV_HALO, CONV_WIDTH), jnp.float32),
            pltpu.VMEM((SEQ_TILE + POOL_HALO, POOL_WIDTH), jnp.float32),
            pltpu.VMEM((LANES, LANES), jnp.float32),
            pltpu.VMEM((D_MODEL, IN_PROJ), jnp.bfloat16),
            pltpu.VMEM((len(POOL_WINDOWS), POOL_GW, POOL_GW), jnp.bfloat16),
            pltpu.VMEM((D_MODEL, D_MODEL), jnp.bfloat16),
        ],
        compiler_params=pltpu.CompilerParams(
            dimension_semantics=("arbitrary", "arbitrary"), vmem_limit_bytes=VMEM_LIMIT),
    )(x2, lnin_g, lnin_b, w_in, conv_w, conv_b, pool_w, pool_scale, w_out, ln1_g, ln1_b, w_r, b_r)


def _plan_kernel(rt_ref, counts_ref, pos_ref, tile_start_ref, tiles_ref, pad_ref, *, n_rows):
    lane = lax.broadcasted_iota(jnp.int32, (ROUTE_ROWS, LANES), 1)
    counts = jnp.broadcast_to(counts_ref[...], (ROUTE_ROWS, LANES))
    tiles = jnp.floor((counts + (ROW_TILE - 1)) * (1.0 / ROW_TILE))
    tile_end = tiles
    shift = 1
    while shift < N_EXPERTS:
        tile_end = tile_end + jnp.where(lane >= shift, pltpu.roll(tile_end, shift, axis=1), 0.0)
        shift *= 2
    row_start = (tile_end - tiles) * ROW_TILE
    tile_start_ref[...] = (tile_end - tiles)[0:1, :].astype(jnp.int32)
    tiles_ref[...] = tiles[0:1, :].astype(jnp.int32)

    rt = rt_ref[...]
    ids = rt[R_ID1:R_ID2 + 1, :]
    start = jnp.zeros_like(ids)
    for e in range(N_EXPERTS):
        start = jnp.where(ids == e, row_start[0:1, e:e + 1], start)
    pos_ref[...] = (start + rt[R_RANK1:R_RANK2 + 1, :]).astype(jnp.int32)

    sub = lax.broadcasted_iota(jnp.int32, (N_EXPERTS, LANES), 0)
    lane_e = lax.broadcasted_iota(jnp.int32, (N_EXPERTS, LANES), 1)
    diag = sub == lane_e
    pad_lo = jnp.sum(jnp.where(diag, (row_start + counts)[0:1, :], 0.0), axis=1, keepdims=True)
    pad_n = jnp.sum(jnp.where(diag, (tiles * ROW_TILE - counts)[0:1, :], 0.0), axis=1, keepdims=True)
    j = lax.broadcasted_iota(jnp.int32, (N_EXPERTS, ROW_TILE), 1).astype(jnp.float32)
    pad_ref[...] = jnp.where(j < pad_n, pad_lo + j, n_rows + j).astype(jnp.int32)


def _run_plan(route_t, counts, n_rows):
    n_tok = route_t.shape[1]
    full = lambda shape: pl.BlockSpec(shape, lambda i: (0, 0))
    return pl.pallas_call(
        functools.partial(_plan_kernel, n_rows=n_rows),
        grid=(1,),
        in_specs=[full((ROUTE_ROWS, n_tok)), full((1, LANES))],
        out_specs=[full((2, n_tok)), full((1, LANES)), full((1, LANES)), full((N_EXPERTS, ROW_TILE))],
        out_shape=[
            jax.ShapeDtypeStruct((2, n_tok), jnp.int32),
            jax.ShapeDtypeStruct((1, LANES), jnp.int32),
            jax.ShapeDtypeStruct((1, LANES), jnp.int32),
            jax.ShapeDtypeStruct((N_EXPERTS, ROW_TILE), jnp.int32),
        ],
        compiler_params=pltpu.CompilerParams(dimension_semantics=("arbitrary",)),
    )(route_t, counts)


def _sc_mesh():
    return plsc.VectorSubcoreMesh(core_axis_name="core", subcore_axis_name="subcore")


def _sc_worker_id():
    return lax.axis_index("core") * (SC_WORKERS // 2) + lax.axis_index("subcore")


def _dispatch_rows(xp, pos1, pos2, pad_pos, zero_rows, n_rows):
    n_tok, width = xp.shape
    n_win = n_tok // SC_WORKERS // SC_WINDOW
    n_pad = pad_pos.shape[0] // SC_WORKERS // SC_WINDOW
    as_windows = lambda v: v.reshape(-1, SC_WINDOW)

    @functools.partial(
        pl.kernel, out_type=jax.ShapeDtypeStruct((n_rows + ROW_TILE, width), xp.dtype), mesh=_sc_mesh(),
        scratch_types=[pltpu.VMEM((n_win, SC_WINDOW), jnp.int32), pltpu.VMEM((n_win, SC_WINDOW), jnp.int32),
                       pltpu.VMEM((n_pad, SC_WINDOW), jnp.int32),
                       pltpu.VMEM((2, SC_WINDOW, width), xp.dtype), pltpu.VMEM((SC_WINDOW, width), xp.dtype),
                       pltpu.SemaphoreType.DMA((2,)), pltpu.SemaphoreType.DMA((2,)), pltpu.SemaphoreType.DMA])
    def dispatch(xp_hbm, pos1_hbm, pos2_hbm, pad_hbm, zero_hbm, out_hbm,
                 idx1, idx2, idxp, buf, zbuf, lsem, ssem, psem):
        wid = _sc_worker_id()
        pltpu.sync_copy(pad_hbm.at[pl.ds(wid * n_pad, n_pad)], idxp)
        pltpu.sync_copy(zero_hbm, zbuf)
        pads = [pltpu.make_async_copy(zbuf, out_hbm.at[idxp.at[j]], psem) for j in range(n_pad)]
        for cp in pads:
            cp.start()
        pltpu.sync_copy(pos1_hbm.at[pl.ds(wid * n_win, n_win)], idx1)
        pltpu.sync_copy(pos2_hbm.at[pl.ds(wid * n_win, n_win)], idx2)

        def load(j):
            rows = xp_hbm.at[pl.ds((wid * n_win + j) * SC_WINDOW, SC_WINDOW)]
            return pltpu.make_async_copy(rows, buf.at[j % 2], lsem.at[j % 2])

        def scatters(j):
            return [pltpu.make_async_copy(buf.at[j % 2], out_hbm.at[idx.at[j]], ssem.at[j % 2])
                    for idx in (idx1, idx2)]

        load(0).start()
        for j in range(n_win):
            load(j).wait()
            for cp in scatters(j):
                cp.start()
            if j >= 1:
                for cp in scatters(j - 1):
                    cp.wait()
            if j + 1 < n_win:
                load(j + 1).start()
        for cp in scatters(n_win - 1):
            cp.wait()
        for cp in pads:
            cp.wait()

    return dispatch(xp, as_windows(pos1), as_windows(pos2), as_windows(pad_pos), zero_rows)


def _gather_rows(src, idx):
    n_out, width = idx.shape[0], src.shape[1]
    n_win = n_out // SC_WORKERS // GATHER_WINDOW
    nbuf = GATHER_BUFFERS

    @functools.partial(
        pl.kernel, out_type=jax.ShapeDtypeStruct((n_out, width), src.dtype), mesh=_sc_mesh(),
        scratch_types=[pltpu.VMEM((n_win, GATHER_WINDOW), jnp.int32),
                       pltpu.VMEM((nbuf, GATHER_WINDOW, width), src.dtype),
                       pltpu.SemaphoreType.DMA((nbuf,)), pltpu.SemaphoreType.DMA((nbuf,))])
    def gather(src_hbm, idx_hbm, dst_hbm, idx_v, buf, gsem, ssem):
        wid = _sc_worker_id()
        pltpu.sync_copy(idx_hbm.at[pl.ds(wid * n_win, n_win)], idx_v)

        def fetch(j):
            return pltpu.make_async_copy(src_hbm.at[idx_v.at[j]], buf.at[j % nbuf], gsem.at[j % nbuf])

        def store(j):
            rows = dst_hbm.at[pl.ds((wid * n_win + j) * GATHER_WINDOW, GATHER_WINDOW)]
            return pltpu.make_async_copy(buf.at[j % nbuf], rows, ssem.at[j % nbuf])

        for j in range(min(nbuf - 1, n_win)):
            fetch(j).start()
        for j in range(n_win):
            fetch(j).wait()
            store(j).start()
            if j + nbuf - 1 < n_win:
                if j >= 1:
                    store(j - 1).wait()
                fetch(j + nbuf - 1).start()
        for j in range(max(0, n_win - nbuf), n_win):
            store(j).wait()

    return gather(src, idx.reshape(-1, GATHER_WINDOW))


def _tile_copy(hbm, buf, sem, tile, slot, to_hbm):
    rows = hbm.at[pl.ds(pl.multiple_of(tile * ROW_TILE, ROW_TILE), ROW_TILE)]
    if to_hbm:
        return pltpu.make_async_copy(buf.at[slot], rows, sem.at[slot])
    return pltpu.make_async_copy(rows, buf.at[slot], sem.at[slot])


def _weight_copies(w_hbm, wbuf, wsem, expert):
    slot = expert % WEIGHT_BUFFERS
    return [pltpu.make_async_copy(w.at[expert], buf.at[slot], wsem.at[slot]) for w, buf in zip(w_hbm, wbuf)]


def _expert_kernel(ts_ref, nte_ref, x_hbm, wg_hbm, wu_hbm, wd_hbm, y_hbm,
                   xbuf, ybuf, xsem, ysem, wg_buf, wu_buf, wd_buf, wsem, wgu_bf, wd_bf, act_ref, *, max_tiles):
    e = pl.program_id(0)
    first = ts_ref[e]
    count = nte_ref[e]
    n_tiles = ts_ref[N_EXPERTS - 1] + nte_ref[N_EXPERTS - 1]
    ahead = TILE_BUFFERS - 2
    w_hbm = (wg_hbm, wu_hbm, wd_hbm)
    wbuf = (wg_buf, wu_buf, wd_buf)

    @pl.when(e == 0)
    def _():
        for g in range(ahead):
            @pl.when(g < n_tiles)
            def _():
                _tile_copy(x_hbm, xbuf, xsem, g, g, False).start()
        for k in range(WEIGHT_BUFFERS - 1):
            for cp in _weight_copies(w_hbm, wbuf, wsem, k):
                cp.start()

    @pl.when(e + WEIGHT_BUFFERS - 1 < N_EXPERTS)
    def _():
        for cp in _weight_copies(w_hbm, wbuf, wsem, e + WEIGHT_BUFFERS - 1):
            cp.start()

    for cp in _weight_copies(w_hbm, wbuf, wsem, e):
        cp.wait()

    def sync_x(g):
        _tile_copy(x_hbm, xbuf, xsem, g, g % TILE_BUFFERS, False).wait()

        @pl.when(g + ahead < n_tiles)
        def _():
            _tile_copy(x_hbm, xbuf, xsem, g + ahead, (g + ahead) % TILE_BUFFERS, False).start()

    def sync_y_slot(g):
        @pl.when(g >= TILE_BUFFERS)
        def _():
            _tile_copy(y_hbm, ybuf, ysem, g - TILE_BUFFERS, g % TILE_BUFFERS, True).wait()

    def up_proj(g):
        lo, hi = _unpack_bf16_pairs(xbuf[g % TILE_BUFFERS])
        return (jnp.dot(lo.astype(jnp.bfloat16), wgu_bf[0:HALF, :], preferred_element_type=jnp.float32)
                + jnp.dot(hi.astype(jnp.bfloat16), wgu_bf[HALF:, :], preferred_element_type=jnp.float32))

    def put_act(hgu, which):
        hg = hgu[:, 0:D_EXPERT]
        act_ref[which] = (hg * jax.nn.sigmoid(hg) * hgu[:, D_EXPERT:]).astype(jnp.bfloat16)

    def down_proj(g, which):
        y = jnp.dot(act_ref[which], wd_bf[...], preferred_element_type=jnp.float32)
        ybuf[g % TILE_BUFFERS] = _pack_bf16_pairs(y)

    def start_y(g):
        _tile_copy(y_hbm, ybuf, ysem, g, g % TILE_BUFFERS, True).start()

    @pl.when(count > 0)
    def _():
        slot = e % WEIGHT_BUFFERS
        wgu_bf[:, 0:D_EXPERT] = wg_buf[slot].astype(jnp.bfloat16)
        wgu_bf[:, D_EXPERT:] = wu_buf[slot].astype(jnp.bfloat16)
        wd_bf[...] = wd_buf[slot].astype(jnp.bfloat16)

        sync_x(first)
        put_act(up_proj(first), 0)

        def pair_body(i, c):
            g = first + 1 + 2 * i
            sync_x(g)
            sync_x(g + 1)
            sync_y_slot(g - 1)
            sync_y_slot(g)
            down_proj(g - 1, 0)
            hgu0 = up_proj(g)
            put_act(hgu0, 1)
            hgu1 = up_proj(g + 1)
            down_proj(g, 1)
            put_act(hgu1, 0)
            start_y(g - 1)
            start_y(g)
            return c

        n_pairs = (count - 1) // 2
        lax.fori_loop(0, n_pairs, pair_body, 0)

        @pl.when((count - 1) % 2 == 1)
        def _():
            g = first + count - 1
            sync_x(g)
            sync_y_slot(g - 1)
            down_proj(g - 1, 0)
            put_act(up_proj(g), 0)
            start_y(g - 1)

        last = first + count - 1
        sync_y_slot(last)
        down_proj(last, 0)
        start_y(last)

    @pl.when(e == N_EXPERTS - 1)
    def _():
        for k in range(TILE_BUFFERS, 0, -1):
            @pl.when(n_tiles >= k)
            def _():
                _tile_copy(y_hbm, ybuf, ysem, n_tiles - k, (n_tiles - k) % TILE_BUFFERS, True).wait()

        ybuf[0] = jnp.zeros((ROW_TILE, HALF), jnp.uint32)

        def fill(g, c):
            cp = _tile_copy(y_hbm, ybuf, ysem, g, 0, True)
            cp.start()
            cp.wait()
            return c

        lax.fori_loop(n_tiles, max_tiles, fill, 0)


def _run_experts(tile_start, tiles_per_expert, x_sorted, w_gate, w_up, w_down):
    n_rows = x_sorted.shape[0]
    max_tiles = n_rows // ROW_TILE
    hbm = pl.BlockSpec(memory_space=pl.ANY)

    grid_spec = pltpu.PrefetchScalarGridSpec(
        num_scalar_prefetch=2,
        grid=(N_EXPERTS,),
        in_specs=[hbm, hbm, hbm, hbm],
        out_specs=hbm,
        scratch_shapes=[
            pltpu.VMEM((TILE_BUFFERS, ROW_TILE, HALF), jnp.uint32),
            pltpu.VMEM((TILE_BUFFERS, ROW_TILE, HALF), jnp.uint32),
            pltpu.SemaphoreType.DMA((TILE_BUFFERS,)),
            pltpu.SemaphoreType.DMA((TILE_BUFFERS,)),
            pltpu.VMEM((WEIGHT_BUFFERS, D_MODEL, D_EXPERT), jnp.float32),
            pltpu.VMEM((WEIGHT_BUFFERS, D_MODEL, D_EXPERT), jnp.float32),
            pltpu.VMEM((WEIGHT_BUFFERS, D_EXPERT, D_MODEL), jnp.float32),
            pltpu.SemaphoreType.DMA((WEIGHT_BUFFERS,)),
            pltpu.VMEM((D_MODEL, 2 * D_EXPERT), jnp.bfloat16),
            pltpu.VMEM((D_EXPERT, D_MODEL), jnp.bfloat16),
            pltpu.VMEM((2, ROW_TILE, D_EXPERT), jnp.bfloat16),
        ],
    )
    return pl.pallas_call(
        functools.partial(_expert_kernel, max_tiles=max_tiles),
        grid_spec=grid_spec,
        out_shape=jax.ShapeDtypeStruct((n_rows, HALF), jnp.uint32),
        compiler_params=pltpu.CompilerParams(
            dimension_semantics=("arbitrary",), vmem_limit_bytes=VMEM_LIMIT),
    )(tile_start, tiles_per_expert, x_sorted, w_gate, w_up, w_down)


def _combine_rows(r0, h1_ref, p_ref, route_ref, ya_ref, yb_ref, w_pg, b_pg, w_ple, g_ref, b_ref, o_ref):
    rows = pl.ds(r0, SUB_TILE)
    h1 = h1_ref[rows, :]
    h_hi = h1.astype(jnp.bfloat16)
    p_b = p_ref[rows, :].astype(jnp.bfloat16)
    yield
    gate_pre = jnp.dot(h_hi, w_pg[...], preferred_element_type=jnp.float32)
    ple_pre = jnp.dot(p_b, w_ple[...], preferred_element_type=jnp.float32)
    yield
    ple = ple_pre * jax.nn.sigmoid(gate_pre + b_pg[...])
    route = route_ref[rows, :]
    w1 = route[:, R_W1:R_W1 + 1]
    w2 = route[:, R_W2:R_W2 + 1]
    a_lo, a_hi = _unpack_bf16_pairs(ya_ref[rows, :])
    b_lo, b_hi = _unpack_bf16_pairs(yb_ref[rows, :])
    moe = jnp.concatenate([w1 * a_lo + w2 * b_lo, w1 * a_hi + w2 * b_hi], axis=-1)
    o_ref[rows, :] = _layernorm(DEEPNORM_ALPHA * h1 + ple + moe, g_ref[...], b_ref[...])


def _combine_kernel(h1_ref, p_ref, route_ref, ya_ref, yb_ref, w_pg, b_pg, w_ple, g_ref, b_ref, o_ref):
    chains = [_combine_rows(r0, h1_ref, p_ref, route_ref, ya_ref, yb_ref, w_pg, b_pg, w_ple, g_ref, b_ref, o_ref)
              for r0 in range(0, h1_ref.shape[0], SUB_TILE)]
    for t in range(COMBINE_PHASES + len(chains) - 1):
        for k, chain in reversed(list(enumerate(chains))):
            if 0 <= t - k < COMBINE_PHASES:
                next(chain, None)


def _run_combine(h1, p2, route, y_tok, w_pg, b_pg, w_ple, ln2_g, ln2_b):
    n_tok = h1.shape[0]
    n_t = n_tok // TOKEN_TILE
    tok_map = lambda i: (i, 0)
    const = lambda shape: pl.BlockSpec(shape, lambda i: (0, 0), pipeline_mode=pl.Buffered(1))
    in_specs = [
        pl.BlockSpec((TOKEN_TILE, D_MODEL), tok_map),
        pl.BlockSpec((TOKEN_TILE, PLE_DIM), tok_map),
        pl.BlockSpec((TOKEN_TILE, LANES), tok_map),
        pl.BlockSpec((TOKEN_TILE, HALF), lambda i: (i, 0)),
        pl.BlockSpec((TOKEN_TILE, HALF), lambda i: (i + n_t, 0)),
        const((D_MODEL, D_MODEL)), const((1, D_MODEL)), const((PLE_DIM, D_MODEL)),
        const((1, D_MODEL)), const((1, D_MODEL)),
    ]
    return pl.pallas_call(
        _combine_kernel,
        grid=(n_t,),
        in_specs=in_specs,
        out_specs=pl.BlockSpec((TOKEN_TILE, D_MODEL), tok_map),
        out_shape=jax.ShapeDtypeStruct((n_tok, D_MODEL), jnp.float32),
        compiler_params=pltpu.CompilerParams(dimension_semantics=("arbitrary",), vmem_limit_bytes=VMEM_LIMIT),
    )(h1, p2, route, y_tok, y_tok, w_pg, b_pg, w_ple, ln2_g, ln2_b)


def _split_bf16(w):
    hi = w.astype(jnp.bfloat16)
    lo = (w - hi.astype(jnp.float32)).astype(jnp.bfloat16)
    return hi, lo


def kernel(x, p, ln_in_g, ln_in_b, w_in, conv_w, conv_b, pool_w, pool_scale, w_out, ln1_g, ln1_b,
           w_rg, b_rg, w_re, b_re, w_gate, w_up, w_down, w_pg, b_pg, w_ple, ln2_g, ln2_b):
    batch, seq, _ = x.shape
    n_tok = batch * seq
    bf = jnp.bfloat16
    row = lambda v: v.reshape(1, -1)

    w_r = jnp.concatenate([w_rg[0], jnp.transpose(w_re[0], (1, 0, 2)).reshape(D_MODEL, N_EXPERTS)], axis=1)
    w_r = jnp.pad(w_r, ((0, 0), (0, LANES - w_r.shape[1])))
    w_r_hi, w_r_lo = _split_bf16(w_r)
    w_r_cat = jnp.concatenate([w_r_hi, w_r_hi, w_r_lo], axis=0)
    b_r = jnp.pad(jnp.concatenate([b_rg[0], b_re[0].reshape(-1)]), (0, LANES - N_GROUPS - N_EXPERTS)).reshape(1, LANES)

    x2 = x.reshape(n_tok, D_MODEL)
    p2 = p[0].reshape(n_tok, PLE_DIM)
    mixer_weights = (row(ln_in_g), row(ln_in_b), w_in[0], conv_w[0], row(conv_b[0]),
                     pool_w[0], row(pool_scale[0]), w_out[0], row(ln1_g[0]), row(ln1_b[0]),
                     w_r_cat, b_r)
    combine_weights = (w_pg[0].astype(bf), row(b_pg[0]), w_ple[0].astype(bf), row(ln2_g[0]), row(ln2_b[0]))
    expert_weights = (w_gate[0].reshape(N_EXPERTS, D_MODEL, D_EXPERT),
                      w_up[0].reshape(N_EXPERTS, D_MODEL, D_EXPERT),
                      w_down[0].reshape(N_EXPERTS, D_EXPERT, D_MODEL))
    zero_rows = jnp.zeros((SC_WINDOW, HALF), jnp.uint32)

    n_rows = -(-(2 * n_tok + N_EXPERTS * (ROW_TILE - 1)) // ROW_TILE) * ROW_TILE
    xp, h1, route, route_t, counts = _run_mixer(x2, *mixer_weights, batch, seq)
    pos, tile_start, tiles_per_expert, pad_pos = _run_plan(route_t, counts, n_rows)
    x_sorted = _dispatch_rows(xp, pos[0], pos[1], pad_pos.reshape(-1), zero_rows, n_rows)
    y_sorted = _run_experts(tile_start[0, :N_EXPERTS], tiles_per_expert[0, :N_EXPERTS], x_sorted, *expert_weights)

    y_tok = _gather_rows(y_sorted, pos.reshape(-1))
    out = _run_combine(h1, p2, route, y_tok, *combine_weights)
    return out.reshape(batch, seq, D_MODEL)
```

```python
import functools

import jax
import jax.numpy as jnp
from jax import lax
from jax.experimental import pallas as pl
from jax.experimental.pallas import tpu as pltpu
from jax.experimental.pallas import tpu_sc as plsc

D_MODEL = 1024
CONV_WIDTH = 512
CONV_K = 3
POOL_WIDTH = 512
POOL_WINDOWS = (2, 4, 8, 16)
POOL_GW = 128
IN_PROJ = 3 * CONV_WIDTH + POOL_WIDTH
N_GROUPS = 4
EXPERTS_PER_GROUP = 8
N_EXPERTS = N_GROUPS * EXPERTS_PER_GROUP
D_EXPERT = 256
PLE_DIM = 256
LN_EPS = 1e-5
DEEPNORM_ALPHA = 2.0 ** 0.25

LANES = 128
HALF = D_MODEL // 2
CONV_HALO = 8
POOL_HALO = 16
SEQ_TILE = 1024
SUB_TILE = 256
MIXER_PHASES = 8
COMBINE_PHASES = 3
ROW_TILE = 256
TILE_BUFFERS = 8
SC_WORKERS = 32
SC_WINDOW = 64
GATHER_WINDOW = 32
GATHER_BUFFERS = 4
WEIGHT_BUFFERS = 3
TOKEN_TILE = 1024
VMEM_LIMIT = 56 * 1024 * 1024

R_ID1, R_ID2, R_RANK1, R_RANK2, R_W1, R_W2 = range(6)
ROUTE_ROWS = 8
LOGIT_ROWS = 40


def _layernorm(x, g, b):
    mu = jnp.mean(x, axis=-1, keepdims=True)
    xc = x - mu
    var = jnp.mean(xc * xc, axis=-1, keepdims=True)
    return xc * lax.rsqrt(var + LN_EPS) * g + b


def _pack_bf16_pairs(v):
    bits = lax.bitcast_convert_type(v.astype(jnp.bfloat16).astype(jnp.float32), jnp.uint32)
    return bits[:, HALF:] | (bits[:, :HALF] >> 16)


def _unpack_bf16_pairs(w):
    lo = lax.bitcast_convert_type(w << 16, jnp.float32)
    hi = lax.bitcast_convert_type(w & jnp.uint32(0xFFFF0000), jnp.float32)
    return lo, hi


def _mixer_kernel(x_ref, lnin_g, lnin_b, w_in_f32, conv_w, conv_b, pool_w_f32, pool_scale, w_out_f32,
                  ln1_g, ln1_b, w_r, b_r,
                  xp_ref, h1_ref, route_t_ref, counts_ref,
                  zbuf, vbuf, carry, w_in, pool_w, w_out):
    b = pl.program_id(0)
    s = pl.program_id(1)
    ts = x_ref.shape[0]

    @pl.when(s == 0)
    def _():
        zbuf[0:CONV_HALO, :] = jnp.zeros((CONV_HALO, CONV_WIDTH), jnp.float32)
        vbuf[0:POOL_HALO, :] = jnp.zeros((POOL_HALO, POOL_WIDTH), jnp.float32)

    @pl.when((b == 0) & (s == 0))
    def _():
        carry[...] = jnp.zeros_like(carry)
        w_in[...] = w_in_f32[...].astype(jnp.bfloat16)
        pool_w[...] = pool_w_f32[...].astype(jnp.bfloat16)
        w_out[...] = w_out_f32[...].astype(jnp.bfloat16)

    chains = [_mixer_rows(r0, s * ts + r0, x_ref, lnin_g, lnin_b, w_in, conv_w, conv_b, pool_w, pool_scale,
                          w_out, ln1_g, ln1_b, w_r, b_r,
                          xp_ref, h1_ref, route_t_ref, zbuf, vbuf, carry)
              for r0 in range(0, ts, SUB_TILE)]
    for t in range(MIXER_PHASES + len(chains) - 1):
        for k, chain in reversed(list(enumerate(chains))):
            if 0 <= t - k < MIXER_PHASES:
                next(chain, None)
    zbuf[0:CONV_HALO, :] = zbuf[ts:ts + CONV_HALO, :]
    vbuf[0:POOL_HALO, :] = vbuf[ts:ts + POOL_HALO, :]
    counts_ref[...] = jnp.transpose(carry[...])[0:1, :]


def _mixer_rows(r0, seq0, x_ref, lnin_g, lnin_b, w_in, conv_w, conv_b, pool_w, pool_scale, w_out,
                ln1_g, ln1_b, w_r, b_r,
                xp_ref, h1_ref, route_t_ref, zbuf, vbuf, carry):
    n = SUB_TILE
    rows = pl.ds(r0, n)
    h0 = _layernorm(x_ref[rows, :], lnin_g[...], lnin_b[...])
    h0b = h0.astype(jnp.bfloat16)
    yield
    u_a = jnp.dot(h0b, w_in[:, 0:IN_PROJ // 2], preferred_element_type=jnp.float32)
    yield
    u_b = jnp.dot(h0b, w_in[:, IN_PROJ // 2:], preferred_element_type=jnp.float32)
    yield
    b_g = u_a[:, 0:CONV_WIDTH]
    c_g = u_a[:, CONV_WIDTH:]
    v_c = u_b[:, 0:CONV_WIDTH]
    v_p = u_b[:, CONV_WIDTH:]

    zbuf[pl.ds(CONV_HALO + r0, n), :] = c_g * v_c
    zext = zbuf[pl.ds(r0, n + CONV_HALO), :]
    z1 = pltpu.roll(zext, 1, axis=0)[CONV_HALO:, :]
    z2 = pltpu.roll(zext, 2, axis=0)[CONV_HALO:, :]
    conv = z2 * conv_w[0:1, :] + z1 * conv_w[1:2, :] + zext[CONV_HALO:, :] * conv_w[2:3, :] + conv_b[...]
    y_conv = b_g * conv

    vbuf[pl.ds(POOL_HALO + r0, n), :] = v_p
    vext = vbuf[pl.ds(r0, n + POOL_HALO), :]
    s2 = vext + pltpu.roll(vext, 1, axis=0)
    s4 = s2[:, POOL_GW:] + pltpu.roll(s2[:, POOL_GW:], 2, axis=0)
    s8 = s4[:, POOL_GW:] + pltpu.roll(s4[:, POOL_GW:], 4, axis=0)
    s16 = s8[:, POOL_GW:] + pltpu.roll(s8[:, POOL_GW:], 8, axis=0)
    wsums = (s2[POOL_HALO:, 0:POOL_GW], s4[POOL_HALO:, 0:POOL_GW],
             s8[POOL_HALO:, 0:POOL_GW], s16[POOL_HALO:, 0:POOL_GW])

    t_pos = seq0 + lax.broadcasted_iota(jnp.int32, (n, 1), 0)
    y_pool = []
    for j, w in enumerate(POOL_WINDOWS):
        inv_cnt = 1.0 / jnp.minimum(t_pos + 1, w).astype(jnp.float32)
        pooled = wsums[j] * inv_cnt - v_p[:, j * POOL_GW:(j + 1) * POOL_GW]
        y_pool.append(jnp.dot(pooled.astype(jnp.bfloat16), pool_w[j], preferred_element_type=jnp.float32))
    y_pool = jnp.concatenate(y_pool, axis=-1) * pool_scale[...]

    ycat = jnp.concatenate([y_conv, y_pool], axis=-1).astype(jnp.bfloat16)
    yield
    mix = jnp.dot(ycat, w_out[...], preferred_element_type=jnp.float32)
    yield
    h1 = _layernorm(DEEPNORM_ALPHA * h0 + mix, ln1_g[...], ln1_b[...])

    h_hi = h1.astype(jnp.bfloat16)
    h_lo = (h1 - h_hi.astype(jnp.float32)).astype(jnp.bfloat16)
    xp_ref[rows, :] = _pack_bf16_pairs(h1)
    h1_ref[rows, :] = h1
    hcat = jnp.concatenate([h_hi, h_lo, h_hi], axis=-1)
    yield
    logits = jnp.dot(hcat, w_r[...], preferred_element_type=jnp.float32) + b_r[...]
    yield
    lt = jnp.transpose(logits)[0:LOGIT_ROWS, :]
    rid = lax.broadcasted_iota(jnp.int32, (LOGIT_ROWS, n), 0).astype(jnp.float32)
    neg = jnp.float32(-jnp.inf)

    def first_argmax(vals):
        m = jnp.max(vals, axis=0, keepdims=True)
        idx = jnp.min(jnp.where(vals == m, rid, float(LOGIT_ROWS)), axis=0, keepdims=True)
        return m, idx

    g_mask = rid < N_GROUPS
    g_max, g_idx = first_argmax(jnp.where(g_mask, lt, neg))
    g_w = 1.0 / jnp.sum(jnp.where(g_mask, jnp.exp(lt - g_max), 0.0), axis=0, keepdims=True)

    e_lo = N_GROUPS + EXPERTS_PER_GROUP * g_idx
    e_vals = jnp.where((rid >= e_lo) & (rid < e_lo + EXPERTS_PER_GROUP), lt, neg)
    m1, i1 = first_argmax(e_vals)
    m2, i2 = first_argmax(jnp.where(rid == i1, neg, e_vals))
    e21 = jnp.exp(m2 - m1)
    w1 = g_w / (1.0 + e21)
    w2 = g_w * e21 / (1.0 + e21)
    id1 = i1 - N_GROUPS
    id2 = i2 - N_GROUPS

    eid = lax.broadcasted_iota(jnp.int32, (N_EXPERTS, n), 0).astype(jnp.float32)
    sel1 = eid == id1
    sel2 = eid == id2
    onehot = (sel1 | sel2).astype(jnp.float32)
    src = lax.broadcasted_iota(jnp.int32, (n, n), 0)
    dst = lax.broadcasted_iota(jnp.int32, (n, n), 1)
    earlier = (src < dst).astype(jnp.bfloat16)
    before = (jnp.dot(onehot.astype(jnp.bfloat16), earlier, preferred_element_type=jnp.float32)
              + carry[0:N_EXPERTS, 0:1])
    rank1 = jnp.sum(jnp.where(sel1, before, 0.0), axis=0, keepdims=True)
    rank2 = jnp.sum(jnp.where(sel2, before, 0.0), axis=0, keepdims=True)
    carry[0:N_EXPERTS, :] = carry[0:N_EXPERTS, :] + jnp.sum(onehot, axis=1, keepdims=True)

    rec_t = jnp.zeros((ROUTE_ROWS, n), jnp.float32)
    rec_row = lax.broadcasted_iota(jnp.int32, (ROUTE_ROWS, n), 0)
    for k, val in ((R_ID1, id1), (R_ID2, id2), (R_W1, w1), (R_W2, w2), (R_RANK1, rank1), (R_RANK2, rank2)):
        rec_t = jnp.where(rec_row == k, val, rec_t)
    route_t_ref[:, rows] = rec_t


def _run_mixer(x2, lnin_g, lnin_b, w_in, conv_w, conv_b, pool_w, pool_scale, w_out, ln1_g, ln1_b,
               w_r, b_r, batch, seq):
    n_tok = batch * seq
    n_s = seq // SEQ_TILE
    tok_map = lambda b, s: (b * n_s + s, 0)

    def const(shape):
        return pl.BlockSpec(shape, lambda b, s: (0,) * len(shape), pipeline_mode=pl.Buffered(1))

    in_specs = [
        pl.BlockSpec((SEQ_TILE, D_MODEL), tok_map),
        const((1, D_MODEL)), const((1, D_MODEL)),
        const((D_MODEL, IN_PROJ)),
        const((CONV_K, CONV_WIDTH)), const((1, CONV_WIDTH)),
        const((len(POOL_WINDOWS), POOL_GW, POOL_GW)), const((1, POOL_WIDTH)),
        const((D_MODEL, D_MODEL)),
        const((1, D_MODEL)), const((1, D_MODEL)),
        const((3 * D_MODEL, LANES)), const((1, LANES)),
    ]
    out_specs = [
        pl.BlockSpec((SEQ_TILE, HALF), tok_map),
        pl.BlockSpec((SEQ_TILE, D_MODEL), tok_map),
        pl.BlockSpec((ROUTE_ROWS, SEQ_TILE), lambda b, s: (0, b * n_s + s)),
        pl.BlockSpec((1, LANES), lambda b, s: (0, 0)),
    ]
    out_shape = [
        jax.ShapeDtypeStruct((n_tok, HALF), jnp.uint32),
        jax.ShapeDtypeStruct((n_tok, D_MODEL), jnp.float32),
        jax.ShapeDtypeStruct((ROUTE_ROWS, n_tok), jnp.float32),
        jax.ShapeDtypeStruct((1, LANES), jnp.float32),
    ]
    return pl.pallas_call(
        _mixer_kernel,
        grid=(batch, n_s),
        in_specs=in_specs,
        out_specs=out_specs,
        out_shape=out_shape,
        scratch_shapes=[
            pltpu.VMEM((SEQ_TILE + CONV_HALO, CONV_WIDTH), jnp.float32),
            pltpu.VMEM((SEQ_TILE + POOL_HALO, POOL_WIDTH), jnp.float32),
            pltpu.VMEM((LANES, LANES), jnp.float32),
            pltpu.VMEM((D_MODEL, IN_PROJ), jnp.bfloat16),
            pltpu.VMEM((len(POOL_WINDOWS), POOL_GW, POOL_GW), jnp.bfloat16),
            pltpu.VMEM((D_MODEL, D_MODEL), jnp.bfloat16),
        ],
        compiler_params=pltpu.CompilerParams(
            dimension_semantics=("arbitrary", "arbitrary"), vmem_limit_bytes=VMEM_LIMIT),
    )(x2, lnin_g, lnin_b, w_in, conv_w, conv_b, pool_w, pool_scale, w_out, ln1_g, ln1_b, w_r, b_r)


def _plan_kernel(rt_ref, counts_ref, pos_ref, tile_start_ref, tiles_ref, pad_ref, *, n_rows):
    lane = lax.broadcasted_iota(jnp.int32, (ROUTE_ROWS, LANES), 1)
    counts = jnp.broadcast_to(counts_ref[...], (ROUTE_ROWS, LANES))
    tiles = jnp.floor((counts + (ROW_TILE - 1)) * (1.0 / ROW_TILE))
    tile_end = tiles
    shift = 1
    while shift < N_EXPERTS:
        tile_end = tile_end + jnp.where(lane >= shift, pltpu.roll(tile_end, shift, axis=1), 0.0)
        shift *= 2
    row_start = (tile_end - tiles) * ROW_TILE
    tile_start_ref[...] = (tile_end - tiles)[0:1, :].astype(jnp.int32)
    tiles_ref[...] = tiles[0:1, :].astype(jnp.int32)

    rt = rt_ref[...]
    ids = rt[R_ID1:R_ID2 + 1, :]
    start = jnp.zeros_like(ids)
    for e in range(N_EXPERTS):
        start = jnp.where(ids == e, row_start[0:1, e:e + 1], start)
    pos_ref[...] = (start + rt[R_RANK1:R_RANK2 + 1, :]).astype(jnp.int32)

    sub = lax.broadcasted_iota(jnp.int32, (N_EXPERTS, LANES), 0)
    lane_e = lax.broadcasted_iota(jnp.int32, (N_EXPERTS, LANES), 1)
    diag = sub == lane_e
    pad_lo = jnp.sum(jnp.where(diag, (row_start + counts)[0:1, :], 0.0), axis=1, keepdims=True)
    pad_n = jnp.sum(jnp.where(diag, (tiles * ROW_TILE - counts)[0:1, :], 0.0), axis=1, keepdims=True)
    j = lax.broadcasted_iota(jnp.int32, (N_EXPERTS, ROW_TILE), 1).astype(jnp.float32)
    pad_ref[...] = jnp.where(j < pad_n, pad_lo + j, n_rows + j).astype(jnp.int32)


def _run_plan(route_t, counts, n_rows):
    n_tok = route_t.shape[1]
    full = lambda shape: pl.BlockSpec(shape, lambda i: (0, 0))
    return pl.pallas_call(
        functools.partial(_plan_kernel, n_rows=n_rows),
        grid=(1,),
        in_specs=[full((ROUTE_ROWS, n_tok)), full((1, LANES))],
        out_specs=[full((2, n_tok)), full((1, LANES)), full((1, LANES)), full((N_EXPERTS, ROW_TILE))],
        out_shape=[
            jax.ShapeDtypeStruct((2, n_tok), jnp.int32),
            jax.ShapeDtypeStruct((1, LANES), jnp.int32),
            jax.ShapeDtypeStruct((1, LANES), jnp.int32),
            jax.ShapeDtypeStruct((N_EXPERTS, ROW_TILE), jnp.int32),
        ],
        compiler_params=pltpu.CompilerParams(dimension_semantics=("arbitrary",)),
    )(route_t, counts)


def _sc_mesh():
    return plsc.VectorSubcoreMesh(core_axis_name="core", subcore_axis_name="subcore")


def _sc_worker_id():
    return lax.axis_index("core") * (SC_WORKERS // 2) + lax.axis_index("subcore")


def _dispatch_rows(xp, pos1, pos2, pad_pos, zero_rows, n_rows):
    n_tok, width = xp.shape
    n_win = n_tok // SC_WORKERS // SC_WINDOW
    n_pad = pad_pos.shape[0] // SC_WORKERS // SC_WINDOW
    as_windows = lambda v: v.reshape(-1, SC_WINDOW)

    @functools.partial(
        pl.kernel, out_type=jax.ShapeDtypeStruct((n_rows + ROW_TILE, width), xp.dtype), mesh=_sc_mesh(),
        scratch_types=[pltpu.VMEM((n_win, SC_WINDOW), jnp.int32), pltpu.VMEM((n_win, SC_WINDOW), jnp.int32),
                       pltpu.VMEM((n_pad, SC_WINDOW), jnp.int32),
                       pltpu.VMEM((2, SC_WINDOW, width), xp.dtype), pltpu.VMEM((SC_WINDOW, width), xp.dtype),
                       pltpu.SemaphoreType.DMA((2,)), pltpu.SemaphoreType.DMA((2,)), pltpu.SemaphoreType.DMA])
    def dispatch(xp_hbm, pos1_hbm, pos2_hbm, pad_hbm, zero_hbm, out_hbm,
                 idx1, idx2, idxp, buf, zbuf, lsem, ssem, psem):
        wid = _sc_worker_id()
        pltpu.sync_copy(pad_hbm.at[pl.ds(wid * n_pad, n_pad)], idxp)
        pltpu.sync_copy(zero_hbm, zbuf)
        pads = [pltpu.make_async_copy(zbuf, out_hbm.at[idxp.at[j]], psem) for j in range(n_pad)]
        for cp in pads:
            cp.start()
        pltpu.sync_copy(pos1_hbm.at[pl.ds(wid * n_win, n_win)], idx1)
        pltpu.sync_copy(pos2_hbm.at[pl.ds(wid * n_win, n_win)], idx2)

        def load(j):
            rows = xp_hbm.at[pl.ds((wid * n_win + j) * SC_WINDOW, SC_WINDOW)]
            return pltpu.make_async_copy(rows, buf.at[j % 2], lsem.at[j % 2])

        def scatters(j):
            return [pltpu.make_async_copy(buf.at[j % 2], out_hbm.at[idx.at[j]], ssem.at[j % 2])
                    for idx in (idx1, idx2)]

        load(0).start()
        for j in range(n_win):
            load(j).wait()
            for cp in scatters(j):
                cp.start()
            if j >= 1:
                for cp in scatters(j - 1):
                    cp.wait()
            if j + 1 < n_win:
                load(j + 1).start()
        for cp in scatters(n_win - 1):
            cp.wait()
        for cp in pads:
            cp.wait()

    return dispatch(xp, as_windows(pos1), as_windows(pos2), as_windows(pad_pos), zero_rows)


def _gather_rows(src, idx):
    n_out, width = idx.shape[0], src.shape[1]
    n_win = n_out // SC_WORKERS // GATHER_WINDOW
    nbuf = GATHER_BUFFERS

    @functools.partial(
        pl.kernel, out_type=jax.ShapeDtypeStruct((n_out, width), src.dtype), mesh=_sc_mesh(),
        scratch_types=[pltpu.VMEM((n_win, GATHER_WINDOW), jnp.int32),
                       pltpu.VMEM((nbuf, GATHER_WINDOW, width), src.dtype),
                       pltpu.SemaphoreType.DMA((nbuf,)), pltpu.SemaphoreType.DMA((nbuf,))])
    def gather(src_hbm, idx_hbm, dst_hbm, idx_v, buf, gsem, ssem):
        wid = _sc_worker_id()
        pltpu.sync_copy(idx_hbm.at[pl.ds(wid * n_win, n_win)], idx_v)

        def fetch(j):
            return pltpu.make_async_copy(src_hbm.at[idx_v.at[j]], buf.at[j % nbuf], gsem.at[j % nbuf])

        def store(j):
            rows = dst_hbm.at[pl.ds((wid * n_win + j) * GATHER_WINDOW, GATHER_WINDOW)]
            return pltpu.make_async_copy(buf.at[j % nbuf], rows, ssem.at[j % nbuf])

        for j in range(min(nbuf - 1, n_win)):
            fetch(j).start()
        for j in range(n_win):
            fetch(j).wait()
            store(j).start()
            if j + nbuf - 1 < n_win:
                if j >= 1:
                    store(j - 1).wait()
                fetch(j + nbuf - 1).start()
        for j in range(max(0, n_win - nbuf), n_win):
            store(j).wait()

    return gather(src, idx.reshape(-1, GATHER_WINDOW))


def _tile_copy(hbm, buf, sem, tile, slot, to_hbm):
    rows = hbm.at[pl.ds(pl.multiple_of(tile * ROW_TILE, ROW_TILE), ROW_TILE)]
    if to_hbm:
        return pltpu.make_async_copy(buf.at[slot], rows, sem.at[slot])
    return pltpu.make_async_copy(rows, buf.at[slot], sem.at[slot])


def _weight_copies(w_hbm, wbuf, wsem, expert):
    slot = expert % WEIGHT_BUFFERS
    return [pltpu.make_async_copy(w.at[expert], buf.at[slot], wsem.at[slot]) for w, buf in zip(w_hbm, wbuf)]


def _expert_kernel(ts_ref, nte_ref, x_hbm, wg_hbm, wu_hbm, wd_hbm, y_hbm,
                   xbuf, ybuf, xsem, ysem, wg_buf, wu_buf, wd_buf, wsem, wgu_bf, wd_bf, act_ref, *, max_tiles):
    e = pl.program_id(0)
    first = ts_ref[e]
    count = nte_ref[e]
    n_tiles = ts_ref[N_EXPERTS - 1] + nte_ref[N_EXPERTS - 1]
    ahead = TILE_BUFFERS - 2
    w_hbm = (wg_hbm, wu_hbm, wd_hbm)
    wbuf = (wg_buf, wu_buf, wd_buf)

    @pl.when(e == 0)
    def _():
        for g in range(ahead):
            @pl.when(g < n_tiles)
            def _():
                _tile_copy(x_hbm, xbuf, xsem, g, g, False).start()
        for k in range(WEIGHT_BUFFERS - 1):
            for cp in _weight_copies(w_hbm, wbuf, wsem, k):
                cp.start()

    @pl.when(e + WEIGHT_BUFFERS - 1 < N_EXPERTS)
    def _():
        for cp in _weight_copies(w_hbm, wbuf, wsem, e + WEIGHT_BUFFERS - 1):
            cp.start()

    for cp in _weight_copies(w_hbm, wbuf, wsem, e):
        cp.wait()

    def sync_x(g):
        _tile_copy(x_hbm, xbuf, xsem, g, g % TILE_BUFFERS, False).wait()

        @pl.when(g + ahead < n_tiles)
        def _():
            _tile_copy(x_hbm, xbuf, xsem, g + ahead, (g + ahead) % TILE_BUFFERS, False).start()

    def sync_y_slot(g):
        @pl.when(g >= TILE_BUFFERS)
        def _():
            _tile_copy(y_hbm, ybuf, ysem, g - TILE_BUFFERS, g % TILE_BUFFERS, True).wait()

    def up_proj(g):
        lo, hi = _unpack_bf16_pairs(xbuf[g % TILE_BUFFERS])
        return (jnp.dot(lo.astype(jnp.bfloat16), wgu_bf[0:HALF, :], preferred_element_type=jnp.float32)
                + jnp.dot(hi.astype(jnp.bfloat16), wgu_bf[HALF:, :], preferred_element_type=jnp.float32))

    def put_act(hgu, which):
        hg = hgu[:, 0:D_EXPERT]
        act_ref[which] = (hg * jax.nn.sigmoid(hg) * hgu[:, D_EXPERT:]).astype(jnp.bfloat16)

    def down_proj(g, which):
        y = jnp.dot(act_ref[which], wd_bf[...], preferred_element_type=jnp.float32)
        ybuf[g % TILE_BUFFERS] = _pack_bf16_pairs(y)

    def start_y(g):
        _tile_copy(y_hbm, ybuf, ysem, g, g % TILE_BUFFERS, True).start()

    @pl.when(count > 0)
    def _():
        slot = e % WEIGHT_BUFFERS
        wgu_bf[:, 0:D_EXPERT] = wg_buf[slot].astype(jnp.bfloat16)
        wgu_bf[:, D_EXPERT:] = wu_buf[slot].astype(jnp.bfloat16)
        wd_bf[...] = wd_buf[slot].astype(jnp.bfloat16)

        sync_x(first)
        put_act(up_proj(first), 0)

        def pair_body(i, c):
            g = first + 1 + 2 * i
            sync_x(g)
            sync_x(g + 1)
            sync_y_slot(g - 1)
            sync_y_slot(g)
            down_proj(g - 1, 0)
            hgu0 = up_proj(g)
            put_act(hgu0, 1)
            hgu1 = up_proj(g + 1)
            down_proj(g, 1)
            put_act(hgu1, 0)
            start_y(g - 1)
            start_y(g)
            return c

        n_pairs = (count - 1) // 2
        lax.fori_loop(0, n_pairs, pair_body, 0)

        @pl.when((count - 1) % 2 == 1)
        def _():
            g = first + count - 1
            sync_x(g)
            sync_y_slot(g - 1)
            down_proj(g - 1, 0)
            put_act(up_proj(g), 0)
            start_y(g - 1)

        last = first + count - 1
        sync_y_slot(last)
        down_proj(last, 0)
        start_y(last)

    @pl.when(e == N_EXPERTS - 1)
    def _():
        for k in range(TILE_BUFFERS, 0, -1):
            @pl.when(n_tiles >= k)
            def _():
                _tile_copy(y_hbm, ybuf, ysem, n_tiles - k, (n_tiles - k) % TILE_BUFFERS, True).wait()

        ybuf[0] = jnp.zeros((ROW_TILE, HALF), jnp.uint32)

        def fill(g, c):
            cp = _tile_copy(y_hbm, ybuf, ysem, g, 0, True)
            cp.start()
            cp.wait()
            return c

        lax.fori_loop(n_tiles, max_tiles, fill, 0)


def _run_experts(tile_start, tiles_per_expert, x_sorted, w_gate, w_up, w_down):
    n_rows = x_sorted.shape[0]
    max_tiles = n_rows // ROW_TILE
    hbm = pl.BlockSpec(memory_space=pl.ANY)

    grid_spec = pltpu.PrefetchScalarGridSpec(
        num_scalar_prefetch=2,
        grid=(N_EXPERTS,),
        in_specs=[hbm, hbm, hbm, hbm],
        out_specs=hbm,
        scratch_shapes=[
            pltpu.VMEM((TILE_BUFFERS, ROW_TILE, HALF), jnp.uint32),
            pltpu.VMEM((TILE_BUFFERS, ROW_TILE, HALF), jnp.uint32),
            pltpu.SemaphoreType.DMA((TILE_BUFFERS,)),
            pltpu.SemaphoreType.DMA((TILE_BUFFERS,)),
            pltpu.VMEM((WEIGHT_BUFFERS, D_MODEL, D_EXPERT), jnp.float32),
            pltpu.VMEM((WEIGHT_BUFFERS, D_MODEL, D_EXPERT), jnp.float32),
            pltpu.VMEM((WEIGHT_BUFFERS, D_EXPERT, D_MODEL), jnp.float32),
            pltpu.SemaphoreType.DMA((WEIGHT_BUFFERS,)),
            pltpu.VMEM((D_MODEL, 2 * D_EXPERT), jnp.bfloat16),
            pltpu.VMEM((D_EXPERT, D_MODEL), jnp.bfloat16),
            pltpu.VMEM((2, ROW_TILE, D_EXPERT), jnp.bfloat16),
        ],
    )
    return pl.pallas_call(
        functools.partial(_expert_kernel, max_tiles=max_tiles),
        grid_spec=grid_spec,
        out_shape=jax.ShapeDtypeStruct((n_rows, HALF), jnp.uint32),
        compiler_params=pltpu.CompilerParams(
            dimension_semantics=("arbitrary",), vmem_limit_bytes=VMEM_LIMIT),
    )(tile_start, tiles_per_expert, x_sorted, w_gate, w_up, w_down)


def _combine_rows(r0, h1_ref, p_ref, route_t_ref, ya_ref, yb_ref, w_pg, b_pg, w_ple, g_ref, b_ref, o_ref):
    rows = pl.ds(r0, SUB_TILE)
    h1 = h1_ref[rows, :]
    h_hi = h1.astype(jnp.bfloat16)
    p_b = p_ref[rows, :].astype(jnp.bfloat16)
    rec_t = route_t_ref[:, rows]
    padded = jnp.concatenate([rec_t, jnp.zeros((LANES - ROUTE_ROWS, SUB_TILE), jnp.float32)], axis=0)
    route = jnp.transpose(padded)
    yield
    gate_pre = jnp.dot(h_hi, w_pg[...], preferred_element_type=jnp.float32)
    ple_pre = jnp.dot(p_b, w_ple[...], preferred_element_type=jnp.float32)
    yield
    ple = ple_pre * jax.nn.sigmoid(gate_pre + b_pg[...])
    w1 = route[:, R_W1:R_W1 + 1]
    w2 = route[:, R_W2:R_W2 + 1]
    a_lo, a_hi = _unpack_bf16_pairs(ya_ref[rows, :])
    b_lo, b_hi = _unpack_bf16_pairs(yb_ref[rows, :])
    moe = jnp.concatenate([w1 * a_lo + w2 * b_lo, w1 * a_hi + w2 * b_hi], axis=-1)
    o_ref[rows, :] = _layernorm(DEEPNORM_ALPHA * h1 + ple + moe, g_ref[...], b_ref[...])


def _combine_kernel(h1_ref, p_ref, route_t_ref, ya_ref, yb_ref, w_pg, b_pg, w_ple, g_ref, b_ref, o_ref):
    chains = [_combine_rows(r0, h1_ref, p_ref, route_t_ref, ya_ref, yb_ref, w_pg, b_pg, w_ple, g_ref, b_ref, o_ref)
              for r0 in range(0, h1_ref.shape[0], SUB_TILE)]
    for t in range(COMBINE_PHASES + len(chains) - 1):
        for k, chain in reversed(list(enumerate(chains))):
            if 0 <= t - k < COMBINE_PHASES:
                next(chain, None)


def _run_combine(h1, p2, route_t, y_tok, w_pg, b_pg, w_ple, ln2_g, ln2_b):
    n_tok = h1.shape[0]
    n_t = n_tok // TOKEN_TILE
    tok_map = lambda i: (i, 0)
    const = lambda shape: pl.BlockSpec(shape, lambda i: (0, 0), pipeline_mode=pl.Buffered(1))
    in_specs = [
        pl.BlockSpec((TOKEN_TILE, D_MODEL), tok_map),
        pl.BlockSpec((TOKEN_TILE, PLE_DIM), tok_map),
        pl.BlockSpec((ROUTE_ROWS, TOKEN_TILE), lambda i: (0, i)),
        pl.BlockSpec((TOKEN_TILE, HALF), lambda i: (i, 0)),
        pl.BlockSpec((TOKEN_TILE, HALF), lambda i: (i + n_t, 0)),
        const((D_MODEL, D_MODEL)), const((1, D_MODEL)), const((PLE_DIM, D_MODEL)),
        const((1, D_MODEL)), const((1, D_MODEL)),
    ]
    return pl.pallas_call(
        _combine_kernel,
        grid=(n_t,),
        in_specs=in_specs,
        out_specs=pl.BlockSpec((TOKEN_TILE, D_MODEL), tok_map),
        out_shape=jax.ShapeDtypeStruct((n_tok, D_MODEL), jnp.float32),
        compiler_params=pltpu.CompilerParams(dimension_semantics=("arbitrary",), vmem_limit_bytes=VMEM_LIMIT),
    )(h1, p2, route_t, y_tok, y_tok, w_pg, b_pg, w_ple, ln2_g, ln2_b)


def _split_bf16(w):
    hi = w.astype(jnp.bfloat16)
    lo = (w - hi.astype(jnp.float32)).astype(jnp.bfloat16)
    return hi, lo


def kernel(x, p, ln_in_g, ln_in_b, w_in, conv_w, conv_b, pool_w, pool_scale, w_out, ln1_g, ln1_b,
           w_rg, b_rg, w_re, b_re, w_gate, w_up, w_down, w_pg, b_pg, w_ple, ln2_g, ln2_b):
    batch, seq, _ = x.shape
    n_tok = batch * seq
    bf = jnp.bfloat16
    row = lambda v: v.reshape(1, -1)

    w_r = jnp.concatenate([w_rg[0], jnp.transpose(w_re[0], (1, 0, 2)).reshape(D_MODEL, N_EXPERTS)], axis=1)
    w_r = jnp.pad(w_r, ((0, 0), (0, LANES - w_r.shape[1])))
    w_r_hi, w_r_lo = _split_bf16(w_r)
    w_r_cat = jnp.concatenate([w_r_hi, w_r_hi, w_r_lo], axis=0)
    b_r = jnp.pad(jnp.concatenate([b_rg[0], b_re[0].reshape(-1)]), (0, LANES - N_GROUPS - N_EXPERTS)).reshape(1, LANES)

    x2 = x.reshape(n_tok, D_MODEL)
    p2 = p[0].reshape(n_tok, PLE_DIM)
    mixer_weights = (row(ln_in_g), row(ln_in_b), w_in[0], conv_w[0], row(conv_b[0]),
                     pool_w[0], row(pool_scale[0]), w_out[0], row(ln1_g[0]), row(ln1_b[0]),
                     w_r_cat, b_r)
    combine_weights = (w_pg[0].astype(bf), row(b_pg[0]), w_ple[0].astype(bf), row(ln2_g[0]), row(ln2_b[0]))
    expert_weights = (w_gate[0].reshape(N_EXPERTS, D_MODEL, D_EXPERT),
                      w_up[0].reshape(N_EXPERTS, D_MODEL, D_EXPERT),
                      w_down[0].reshape(N_EXPERTS, D_EXPERT, D_MODEL))
    zero_rows = jnp.zeros((SC_WINDOW, HALF), jnp.uint32)

    n_rows = -(-(2 * n_tok + N_EXPERTS * (ROW_TILE - 1)) // ROW_TILE) * ROW_TILE
    xp, h1, route_t, counts = _run_mixer(x2, *mixer_weights, batch, seq)
    pos, tile_start, tiles_per_expert, pad_pos = _run_plan(route_t, counts, n_rows)
    x_sorted = _dispatch_rows(xp, pos[0], pos[1], pad_pos.reshape(-1), zero_rows, n_rows)
    y_sorted = _run_experts(tile_start[0, :N_EXPERTS], tiles_per_expert[0, :N_EXPERTS], x_sorted, *expert_weights)

    y_tok = _gather_rows(y_sorted, pos.reshape(-1))
    out = _run_combine(h1, p2, route_t, y_tok, *combine_weights)
    return out.reshape(batch, seq, D_MODEL)
```

```python
import functools

import jax
import jax.numpy as jnp
from jax import lax
from jax.experimental import pallas as pl
from jax.experimental.pallas import tpu as pltpu
from jax.experimental.pallas import tpu_sc as plsc

D_MODEL = 1024
CONV_WIDTH = 512
CONV_K = 3
POOL_WIDTH = 512
POOL_WINDOWS = (2, 4, 8, 16)
POOL_GW = 128
IN_PROJ = 3 * CONV_WIDTH + POOL_WIDTH
N_GROUPS = 4
EXPERTS_PER_GROUP = 8
N_EXPERTS = N_GROUPS * EXPERTS_PER_GROUP
D_EXPERT = 256
PLE_DIM = 256
LN_EPS = 1e-5
DEEPNORM_ALPHA = 2.0 ** 0.25

LANES = 128
HALF = D_MODEL // 2
CONV_HALO = 8
POOL_HALO = 16
SEQ_TILE = 1024
SUB_TILE = 256
MIXER_PHASES = 8
COMBINE_PHASES = 3
ROW_TILE = 256
TILE_BUFFERS = 8
SC_WORKERS = 32
SC_WINDOW = 64
GATHER_WINDOW = 32
GATHER_BUFFERS = 4
WEIGHT_BUFFERS = 3
TOKEN_TILE = 1024
VMEM_LIMIT = 56 * 1024 * 1024

R_ID1, R_ID2, R_RANK1, R_RANK2, R_W1, R_W2 = range(6)
ROUTE_ROWS = 8
LOGIT_ROWS = 40


def _layernorm(x, g, b):
    mu = jnp.mean(x, axis=-1, keepdims=True)
    xc = x - mu
    var = jnp.mean(xc * xc, axis=-1, keepdims=True)
    return xc * lax.rsqrt(var + LN_EPS) * g + b


def _pack_bf16_pairs(v):
    bits = lax.bitcast_convert_type(v.astype(jnp.bfloat16).astype(jnp.float32), jnp.uint32)
    return bits[:, HALF:] | (bits[:, :HALF] >> 16)


def _unpack_bf16_pairs(w):
    lo = lax.bitcast_convert_type(w << 16, jnp.float32)
    hi = lax.bitcast_convert_type(w & jnp.uint32(0xFFFF0000), jnp.float32)
    return lo, hi


def _mixer_kernel(x_ref, lnin_g, lnin_b, w_in_f32, conv_w, conv_b, pool_w_f32, pool_scale, w_out_f32,
                  ln1_g, ln1_b, w_r, b_r,
                  xp_ref, h1_ref, route_t_ref, counts_ref,
                  zbuf, vbuf, carry, w_in, pool_w, w_out):
    b = pl.program_id(0)
    s = pl.program_id(1)
    ts = x_ref.shape[0]

    @pl.when(s == 0)
    def _():
        zbuf[0:CONV_HALO, :] = jnp.zeros((CONV_HALO, CONV_WIDTH), jnp.float32)
        vbuf[0:POOL_HALO, :] = jnp.zeros((POOL_HALO, POOL_WIDTH), jnp.float32)

    @pl.when((b == 0) & (s == 0))
    def _():
        carry[...] = jnp.zeros_like(carry)
        w_in[...] = w_in_f32[...].astype(jnp.bfloat16)
        pool_w[...] = pool_w_f32[...].astype(jnp.bfloat16)
        w_out[...] = w_out_f32[...].astype(jnp.bfloat16)

    chains = [_mixer_rows(r0, s * ts + r0, x_ref, lnin_g, lnin_b, w_in, conv_w, conv_b, pool_w, pool_scale,
                          w_out, ln1_g, ln1_b, w_r, b_r,
                          xp_ref, h1_ref, route_t_ref, zbuf, vbuf, carry)
              for r0 in range(0, ts, SUB_TILE)]
    for t in range(MIXER_PHASES + len(chains) - 1):
        for k, chain in reversed(list(enumerate(chains))):
            if 0 <= t - k < MIXER_PHASES:
                next(chain, None)
    zbuf[0:CONV_HALO, :] = zbuf[ts:ts + CONV_HALO, :]
    vbuf[0:POOL_HALO, :] = vbuf[ts:ts + POOL_HALO, :]
    counts_ref[...] = jnp.transpose(carry[...])[0:1, :]


def _mixer_rows(r0, seq0, x_ref, lnin_g, lnin_b, w_in, conv_w, conv_b, pool_w, pool_scale, w_out,
                ln1_g, ln1_b, w_r, b_r,
                xp_ref, h1_ref, route_t_ref, zbuf, vbuf, carry):
    n = SUB_TILE
    rows = pl.ds(r0, n)
    h0 = _layernorm(x_ref[rows, :], lnin_g[...], lnin_b[...])
    h0b = h0.astype(jnp.bfloat16)
    yield
    u_a = jnp.dot(h0b, w_in[:, 0:IN_PROJ // 2], preferred_element_type=jnp.float32)
    yield
    u_b = jnp.dot(h0b, w_in[:, IN_PROJ // 2:], preferred_element_type=jnp.float32)
    yield
    b_g = u_a[:, 0:CONV_WIDTH]
    c_g = u_a[:, CONV_WIDTH:]
    v_c = u_b[:, 0:CONV_WIDTH]
    v_p = u_b[:, CONV_WIDTH:]

    zbuf[pl.ds(CONV_HALO + r0, n), :] = c_g * v_c
    zext = zbuf[pl.ds(r0, n + CONV_HALO), :]
    z1 = pltpu.roll(zext, 1, axis=0)[CONV_HALO:, :]
    z2 = pltpu.roll(zext, 2, axis=0)[CONV_HALO:, :]
    conv = z2 * conv_w[0, 0:1, :] + z1 * conv_w[0, 1:2, :] + zext[CONV_HALO:, :] * conv_w[0, 2:3, :] + conv_b[...]
    y_conv = b_g * conv

    vbuf[pl.ds(POOL_HALO + r0, n), :] = v_p
    vext = vbuf[pl.ds(r0, n + POOL_HALO), :]
    s2 = vext + pltpu.roll(vext, 1, axis=0)
    s4 = s2[:, POOL_GW:] + pltpu.roll(s2[:, POOL_GW:], 2, axis=0)
    s8 = s4[:, POOL_GW:] + pltpu.roll(s4[:, POOL_GW:], 4, axis=0)
    s16 = s8[:, POOL_GW:] + pltpu.roll(s8[:, POOL_GW:], 8, axis=0)
    wsums = (s2[POOL_HALO:, 0:POOL_GW], s4[POOL_HALO:, 0:POOL_GW],
             s8[POOL_HALO:, 0:POOL_GW], s16[POOL_HALO:, 0:POOL_GW])

    t_pos = seq0 + lax.broadcasted_iota(jnp.int32, (n, 1), 0)
    y_pool = []
    for j, w in enumerate(POOL_WINDOWS):
        inv_cnt = 1.0 / jnp.minimum(t_pos + 1, w).astype(jnp.float32)
        pooled = wsums[j] * inv_cnt - v_p[:, j * POOL_GW:(j + 1) * POOL_GW]
        y_pool.append(jnp.dot(pooled.astype(jnp.bfloat16), pool_w[j], preferred_element_type=jnp.float32))
    y_pool = jnp.concatenate(y_pool, axis=-1) * pool_scale[...]

    ycat = jnp.concatenate([y_conv, y_pool], axis=-1).astype(jnp.bfloat16)
    yield
    mix = jnp.dot(ycat, w_out[...], preferred_element_type=jnp.float32)
    yield
    h1 = _layernorm(DEEPNORM_ALPHA * h0 + mix, ln1_g[...], ln1_b[...])

    h_hi = h1.astype(jnp.bfloat16)
    h_lo = (h1 - h_hi.astype(jnp.float32)).astype(jnp.bfloat16)
    xp_ref[rows, :] = _pack_bf16_pairs(h1)
    h1_ref[rows, :] = h1
    hcat = jnp.concatenate([h_hi, h_lo, h_hi], axis=-1)
    yield
    logits = jnp.dot(hcat, w_r[...], preferred_element_type=jnp.float32) + b_r[...]
    yield
    lt = jnp.transpose(logits)[0:LOGIT_ROWS, :]
    rid = lax.broadcasted_iota(jnp.int32, (LOGIT_ROWS, n), 0).astype(jnp.float32)
    neg = jnp.float32(-jnp.inf)

    def first_argmax(vals):
        m = jnp.max(vals, axis=0, keepdims=True)
        idx = jnp.min(jnp.where(vals == m, rid, float(LOGIT_ROWS)), axis=0, keepdims=True)
        return m, idx

    g_mask = rid < N_GROUPS
    g_max, g_idx = first_argmax(jnp.where(g_mask, lt, neg))
    g_w = 1.0 / jnp.sum(jnp.where(g_mask, jnp.exp(lt - g_max), 0.0), axis=0, keepdims=True)

    e_lo = N_GROUPS + EXPERTS_PER_GROUP * g_idx
    e_vals = jnp.where((rid >= e_lo) & (rid < e_lo + EXPERTS_PER_GROUP), lt, neg)
    m1, i1 = first_argmax(e_vals)
    m2, i2 = first_argmax(jnp.where(rid == i1, neg, e_vals))
    e21 = jnp.exp(m2 - m1)
    w1 = g_w / (1.0 + e21)
    w2 = g_w * e21 / (1.0 + e21)
    id1 = i1 - N_GROUPS
    id2 = i2 - N_GROUPS

    eid = lax.broadcasted_iota(jnp.int32, (N_EXPERTS, n), 0).astype(jnp.float32)
    sel1 = eid == id1
    sel2 = eid == id2
    onehot = (sel1 | sel2).astype(jnp.float32)
    src = lax.broadcasted_iota(jnp.int32, (n, n), 0)
    dst = lax.broadcasted_iota(jnp.int32, (n, n), 1)
    earlier = (src < dst).astype(jnp.bfloat16)
    before = (jnp.dot(onehot.astype(jnp.bfloat16), earlier, preferred_element_type=jnp.float32)
              + carry[0:N_EXPERTS, 0:1])
    rank1 = jnp.sum(jnp.where(sel1, before, 0.0), axis=0, keepdims=True)
    rank2 = jnp.sum(jnp.where(sel2, before, 0.0), axis=0, keepdims=True)
    carry[0:N_EXPERTS, :] = carry[0:N_EXPERTS, :] + jnp.sum(onehot, axis=1, keepdims=True)

    rec_t = jnp.zeros((ROUTE_ROWS, n), jnp.float32)
    rec_row = lax.broadcasted_iota(jnp.int32, (ROUTE_ROWS, n), 0)
    for k, val in ((R_ID1, id1), (R_ID2, id2), (R_W1, w1), (R_W2, w2), (R_RANK1, rank1), (R_RANK2, rank2)):
        rec_t = jnp.where(rec_row == k, val, rec_t)
    route_t_ref[:, rows] = rec_t


def _run_mixer(x2, lnin_g, lnin_b, w_in, conv_w, conv_b, pool_w, pool_scale, w_out, ln1_g, ln1_b,
               w_r, b_r, batch, seq):
    n_tok = batch * seq
    n_s = seq // SEQ_TILE
    tok_map = lambda b, s: (b * n_s + s, 0)

    def const(shape):
        return pl.BlockSpec(shape, lambda b, s: (0,) * len(shape), pipeline_mode=pl.Buffered(1))

    in_specs = [
        pl.BlockSpec((SEQ_TILE, D_MODEL), tok_map),
        const((1, D_MODEL)), const((1, D_MODEL)),
        const((D_MODEL, IN_PROJ)),
        const((1, CONV_K, CONV_WIDTH)), const((1, CONV_WIDTH)),
        const((len(POOL_WINDOWS), POOL_GW, POOL_GW)), const((1, POOL_WIDTH)),
        const((D_MODEL, D_MODEL)),
        const((1, D_MODEL)), const((1, D_MODEL)),
        const((3 * D_MODEL, LANES)), const((1, LANES)),
    ]
    out_specs = [
        pl.BlockSpec((SEQ_TILE, HALF), tok_map),
        pl.BlockSpec((SEQ_TILE, D_MODEL), tok_map),
        pl.BlockSpec((ROUTE_ROWS, SEQ_TILE), lambda b, s: (0, b * n_s + s)),
        pl.BlockSpec((1, LANES), lambda b, s: (0, 0)),
    ]
    out_shape = [
        jax.ShapeDtypeStruct((n_tok, HALF), jnp.uint32),
        jax.ShapeDtypeStruct((n_tok, D_MODEL), jnp.float32),
        jax.ShapeDtypeStruct((ROUTE_ROWS, n_tok), jnp.float32),
        jax.ShapeDtypeStruct((1, LANES), jnp.float32),
    ]
    return pl.pallas_call(
        _mixer_kernel,
        grid=(batch, n_s),
        in_specs=in_specs,
        out_specs=out_specs,
        out_shape=out_shape,
        scratch_shapes=[
            pltpu.VMEM((SEQ_TILE + CONV_HALO, CONV_WIDTH), jnp.float32),
            pltpu.VMEM((SEQ_TILE + POOL_HALO, POOL_WIDTH), jnp.float32),
            pltpu.VMEM((LANES, LANES), jnp.float32),
            pltpu.VMEM((D_MODEL, IN_PROJ), jnp.bfloat16),
            pltpu.VMEM((len(POOL_WINDOWS), POOL_GW, POOL_GW), jnp.bfloat16),
            pltpu.VMEM((D_MODEL, D_MODEL), jnp.bfloat16),
        ],
        compiler_params=pltpu.CompilerParams(
            dimension_semantics=("arbitrary", "arbitrary"), vmem_limit_bytes=VMEM_LIMIT),
    )(x2, lnin_g, lnin_b, w_in, conv_w, conv_b, pool_w, pool_scale, w_out, ln1_g, ln1_b, w_r, b_r)


def _plan_kernel(rt_ref, counts_ref, pos_ref, tile_start_ref, tiles_ref, pad_ref, *, n_rows):
    lane = lax.broadcasted_iota(jnp.int32, (ROUTE_ROWS, LANES), 1)
    counts = jnp.broadcast_to(counts_ref[...], (ROUTE_ROWS, LANES))
    tiles = jnp.floor((counts + (ROW_TILE - 1)) * (1.0 / ROW_TILE))
    tile_end = tiles
    shift = 1
    while shift < N_EXPERTS:
        tile_end = tile_end + jnp.where(lane >= shift, pltpu.roll(tile_end, shift, axis=1), 0.0)
        shift *= 2
    row_start = (tile_end - tiles) * ROW_TILE
    tile_start_ref[...] = (tile_end - tiles)[0:1, :].astype(jnp.int32)
    tiles_ref[...] = tiles[0:1, :].astype(jnp.int32)

    rt = rt_ref[...]
    ids = rt[R_ID1:R_ID2 + 1, :]
    start = jnp.zeros_like(ids)
    for e in range(N_EXPERTS):
        start = jnp.where(ids == e, row_start[0:1, e:e + 1], start)
    pos_ref[...] = (start + rt[R_RANK1:R_RANK2 + 1, :]).astype(jnp.int32)

    sub = lax.broadcasted_iota(jnp.int32, (N_EXPERTS, LANES), 0)
    lane_e = lax.broadcasted_iota(jnp.int32, (N_EXPERTS, LANES), 1)
    diag = sub == lane_e
    pad_lo = jnp.sum(jnp.where(diag, (row_start + counts)[0:1, :], 0.0), axis=1, keepdims=True)
    pad_n = jnp.sum(jnp.where(diag, (tiles * ROW_TILE - counts)[0:1, :], 0.0), axis=1, keepdims=True)
    j = lax.broadcasted_iota(jnp.int32, (N_EXPERTS, ROW_TILE), 1).astype(jnp.float32)
    pad_ref[...] = jnp.where(j < pad_n, pad_lo + j, n_rows + j).astype(jnp.int32)


def _run_plan(route_t, counts, n_rows):
    n_tok = route_t.shape[1]
    full = lambda shape: pl.BlockSpec(shape, lambda i: (0, 0))
    return pl.pallas_call(
        functools.partial(_plan_kernel, n_rows=n_rows),
        grid=(1,),
        in_specs=[full((ROUTE_ROWS, n_tok)), full((1, LANES))],
        out_specs=[full((2, n_tok)), full((1, LANES)), full((1, LANES)), full((N_EXPERTS, ROW_TILE))],
        out_shape=[
            jax.ShapeDtypeStruct((2, n_tok), jnp.int32),
            jax.ShapeDtypeStruct((1, LANES), jnp.int32),
            jax.ShapeDtypeStruct((1, LANES), jnp.int32),
            jax.ShapeDtypeStruct((N_EXPERTS, ROW_TILE), jnp.int32),
        ],
        compiler_params=pltpu.CompilerParams(dimension_semantics=("arbitrary",)),
    )(route_t, counts)


def _sc_mesh():
    return plsc.VectorSubcoreMesh(core_axis_name="core", subcore_axis_name="subcore")


def _sc_worker_id():
    return lax.axis_index("core") * (SC_WORKERS // 2) + lax.axis_index("subcore")


def _dispatch_rows(xp, pos, pad_pos, zero_rows, n_rows):
    n_tok, width = xp.shape
    second = n_tok // SC_WINDOW
    n_win = n_tok // SC_WORKERS // SC_WINDOW
    n_pad = pad_pos.size // SC_WORKERS // SC_WINDOW
    as_windows = lambda v: v.reshape(-1, SC_WINDOW)

    @functools.partial(
        pl.kernel, out_type=jax.ShapeDtypeStruct((n_rows + ROW_TILE, width), xp.dtype), mesh=_sc_mesh(),
        scratch_types=[pltpu.VMEM((n_win, SC_WINDOW), jnp.int32), pltpu.VMEM((n_win, SC_WINDOW), jnp.int32),
                       pltpu.VMEM((n_pad, SC_WINDOW), jnp.int32),
                       pltpu.VMEM((2, SC_WINDOW, width), xp.dtype), pltpu.VMEM((SC_WINDOW, width), xp.dtype),
                       pltpu.SemaphoreType.DMA((2,)), pltpu.SemaphoreType.DMA((2,)), pltpu.SemaphoreType.DMA])
    def dispatch(xp_hbm, pos_hbm, pad_hbm, zero_hbm, out_hbm,
                 idx1, idx2, idxp, buf, zbuf, lsem, ssem, psem):
        wid = _sc_worker_id()
        pltpu.sync_copy(pad_hbm.at[pl.ds(wid * n_pad, n_pad)], idxp)
        pltpu.sync_copy(zero_hbm, zbuf)
        pads = [pltpu.make_async_copy(zbuf, out_hbm.at[idxp.at[j]], psem) for j in range(n_pad)]
        for cp in pads:
            cp.start()
        pltpu.sync_copy(pos_hbm.at[pl.ds(wid * n_win, n_win)], idx1)
        pltpu.sync_copy(pos_hbm.at[pl.ds(second + wid * n_win, n_win)], idx2)

        def load(j):
            rows = xp_hbm.at[pl.ds((wid * n_win + j) * SC_WINDOW, SC_WINDOW)]
            return pltpu.make_async_copy(rows, buf.at[j % 2], lsem.at[j % 2])

        def scatters(j):
            return [pltpu.make_async_copy(buf.at[j % 2], out_hbm.at[idx.at[j]], ssem.at[j % 2])
                    for idx in (idx1, idx2)]

        load(0).start()
        for j in range(n_win):
            load(j).wait()
            for cp in scatters(j):
                cp.start()
            if j >= 1:
                for cp in scatters(j - 1):
                    cp.wait()
            if j + 1 < n_win:
                load(j + 1).start()
        for cp in scatters(n_win - 1):
            cp.wait()
        for cp in pads:
            cp.wait()

    return dispatch(xp, as_windows(pos), as_windows(pad_pos), zero_rows)


def _gather_rows(src, idx):
    n_out, width = idx.shape[0], src.shape[1]
    n_win = n_out // SC_WORKERS // GATHER_WINDOW
    nbuf = GATHER_BUFFERS

    @functools.partial(
        pl.kernel, out_type=jax.ShapeDtypeStruct((n_out, width), src.dtype), mesh=_sc_mesh(),
        scratch_types=[pltpu.VMEM((n_win, GATHER_WINDOW), jnp.int32),
                       pltpu.VMEM((nbuf, GATHER_WINDOW, width), src.dtype),
                       pltpu.SemaphoreType.DMA((nbuf,)), pltpu.SemaphoreType.DMA((nbuf,))])
    def gather(src_hbm, idx_hbm, dst_hbm, idx_v, buf, gsem, ssem):
        wid = _sc_worker_id()
        pltpu.sync_copy(idx_hbm.at[pl.ds(wid * n_win, n_win)], idx_v)

        def fetch(j):
            return pltpu.make_async_copy(src_hbm.at[idx_v.at[j]], buf.at[j % nbuf], gsem.at[j % nbuf])

        def store(j):
            rows = dst_hbm.at[pl.ds((wid * n_win + j) * GATHER_WINDOW, GATHER_WINDOW)]
            return pltpu.make_async_copy(buf.at[j % nbuf], rows, ssem.at[j % nbuf])

        for j in range(min(nbuf - 1, n_win)):
            fetch(j).start()
        for j in range(n_win):
            fetch(j).wait()
            store(j).start()
            if j + nbuf - 1 < n_win:
                if j >= 1:
                    store(j - 1).wait()
                fetch(j + nbuf - 1).start()
        for j in range(max(0, n_win - nbuf), n_win):
            store(j).wait()

    return gather(src, idx.reshape(-1, GATHER_WINDOW))


def _tile_copy(hbm, buf, sem, tile, slot, to_hbm):
    rows = hbm.at[pl.ds(pl.multiple_of(tile * ROW_TILE, ROW_TILE), ROW_TILE)]
    if to_hbm:
        return pltpu.make_async_copy(buf.at[slot], rows, sem.at[slot])
    return pltpu.make_async_copy(rows, buf.at[slot], sem.at[slot])


def _weight_copies(w_hbm, wbuf, wsem, expert):
    slot = expert % WEIGHT_BUFFERS
    return [pltpu.make_async_copy(w.at[expert], buf.at[slot], wsem.at[slot]) for w, buf in zip(w_hbm, wbuf)]


def _expert_kernel(ts_ref, nte_ref, x_hbm, wg_hbm, wu_hbm, wd_hbm, y_hbm,
                   xbuf, ybuf, xsem, ysem, wg_buf, wu_buf, wd_buf, wsem, wgu_bf, wd_bf, act_ref, *, max_tiles):
    e = pl.program_id(0)
    first = ts_ref[e]
    count = nte_ref[e]
    n_tiles = ts_ref[N_EXPERTS - 1] + nte_ref[N_EXPERTS - 1]
    ahead = TILE_BUFFERS - 2
    w_hbm = (wg_hbm, wu_hbm, wd_hbm)
    wbuf = (wg_buf, wu_buf, wd_buf)

    @pl.when(e == 0)
    def _():
        for g in range(ahead):
            @pl.when(g < n_tiles)
            def _():
                _tile_copy(x_hbm, xbuf, xsem, g, g, False).start()
        for k in range(WEIGHT_BUFFERS - 1):
            for cp in _weight_copies(w_hbm, wbuf, wsem, k):
                cp.start()

    @pl.when(e + WEIGHT_BUFFERS - 1 < N_EXPERTS)
    def _():
        for cp in _weight_copies(w_hbm, wbuf, wsem, e + WEIGHT_BUFFERS - 1):
            cp.start()

    for cp in _weight_copies(w_hbm, wbuf, wsem, e):
        cp.wait()

    def sync_x(g):
        _tile_copy(x_hbm, xbuf, xsem, g, g % TILE_BUFFERS, False).wait()

        @pl.when(g + ahead < n_tiles)
        def _():
            _tile_copy(x_hbm, xbuf, xsem, g + ahead, (g + ahead) % TILE_BUFFERS, False).start()

    def sync_y_slot(g):
        @pl.when(g >= TILE_BUFFERS)
        def _():
            _tile_copy(y_hbm, ybuf, ysem, g - TILE_BUFFERS, g % TILE_BUFFERS, True).wait()

    def up_proj(g):
        lo, hi = _unpack_bf16_pairs(xbuf[g % TILE_BUFFERS])
        return (jnp.dot(lo.astype(jnp.bfloat16), wgu_bf[0:HALF, :], preferred_element_type=jnp.float32)
                + jnp.dot(hi.astype(jnp.bfloat16), wgu_bf[HALF:, :], preferred_element_type=jnp.float32))

    def put_act(hgu, which):
        hg = hgu[:, 0:D_EXPERT]
        act_ref[which] = (hg * jax.nn.sigmoid(hg) * hgu[:, D_EXPERT:]).astype(jnp.bfloat16)

    def down_proj(g, which):
        y = jnp.dot(act_ref[which], wd_bf[...], preferred_element_type=jnp.float32)
        ybuf[g % TILE_BUFFERS] = _pack_bf16_pairs(y)

    def start_y(g):
        _tile_copy(y_hbm, ybuf, ysem, g, g % TILE_BUFFERS, True).start()

    @pl.when(count > 0)
    def _():
        slot = e % WEIGHT_BUFFERS
        wgu_bf[:, 0:D_EXPERT] = wg_buf[slot].astype(jnp.bfloat16)
        wgu_bf[:, D_EXPERT:] = wu_buf[slot].astype(jnp.bfloat16)
        wd_bf[...] = wd_buf[slot].astype(jnp.bfloat16)

        sync_x(first)
        put_act(up_proj(first), 0)

        def pair_body(i, c):
            g = first + 1 + 2 * i
            sync_x(g)
            sync_x(g + 1)
            sync_y_slot(g - 1)
            sync_y_slot(g)
            down_proj(g - 1, 0)
            hgu0 = up_proj(g)
            put_act(hgu0, 1)
            hgu1 = up_proj(g + 1)
            down_proj(g, 1)
            put_act(hgu1, 0)
            start_y(g - 1)
            start_y(g)
            return c

        n_pairs = (count - 1) // 2
        lax.fori_loop(0, n_pairs, pair_body, 0)

        @pl.when((count - 1) % 2 == 1)
        def _():
            g = first + count - 1
            sync_x(g)
            sync_y_slot(g - 1)
            down_proj(g - 1, 0)
            put_act(up_proj(g), 0)
            start_y(g - 1)

        last = first + count - 1
        sync_y_slot(last)
        down_proj(last, 0)
        start_y(last)

    @pl.when(e == N_EXPERTS - 1)
    def _():
        for k in range(TILE_BUFFERS, 0, -1):
            @pl.when(n_tiles >= k)
            def _():
                _tile_copy(y_hbm, ybuf, ysem, n_tiles - k, (n_tiles - k) % TILE_BUFFERS, True).wait()

        ybuf[0] = jnp.zeros((ROW_TILE, HALF), jnp.uint32)

        def fill(g, c):
            cp = _tile_copy(y_hbm, ybuf, ysem, g, 0, True)
            cp.start()
            cp.wait()
            return c

        lax.fori_loop(n_tiles, max_tiles, fill, 0)


def _run_experts(tile_start, tiles_per_expert, x_sorted, w_gate, w_up, w_down):
    n_rows = x_sorted.shape[0]
    max_tiles = n_rows // ROW_TILE
    hbm = pl.BlockSpec(memory_space=pl.ANY)

    grid_spec = pltpu.PrefetchScalarGridSpec(
        num_scalar_prefetch=2,
        grid=(N_EXPERTS,),
        in_specs=[hbm, hbm, hbm, hbm],
        out_specs=hbm,
        scratch_shapes=[
            pltpu.VMEM((TILE_BUFFERS, ROW_TILE, HALF), jnp.uint32),
            pltpu.VMEM((TILE_BUFFERS, ROW_TILE, HALF), jnp.uint32),
            pltpu.SemaphoreType.DMA((TILE_BUFFERS,)),
            pltpu.SemaphoreType.DMA((TILE_BUFFERS,)),
            pltpu.VMEM((WEIGHT_BUFFERS, D_MODEL, D_EXPERT), jnp.float32),
            pltpu.VMEM((WEIGHT_BUFFERS, D_MODEL, D_EXPERT), jnp.float32),
            pltpu.VMEM((WEIGHT_BUFFERS, D_EXPERT, D_MODEL), jnp.float32),
            pltpu.SemaphoreType.DMA((WEIGHT_BUFFERS,)),
            pltpu.VMEM((D_MODEL, 2 * D_EXPERT), jnp.bfloat16),
            pltpu.VMEM((D_EXPERT, D_MODEL), jnp.bfloat16),
            pltpu.VMEM((2, ROW_TILE, D_EXPERT), jnp.bfloat16),
        ],
    )
    return pl.pallas_call(
        functools.partial(_expert_kernel, max_tiles=max_tiles),
        grid_spec=grid_spec,
        out_shape=jax.ShapeDtypeStruct((n_rows, HALF), jnp.uint32),
        compiler_params=pltpu.CompilerParams(
            dimension_semantics=("arbitrary",), vmem_limit_bytes=VMEM_LIMIT),
    )(tile_start, tiles_per_expert, x_sorted, w_gate, w_up, w_down)


def _combine_rows(r0, h1_ref, p_ref, route_t_ref, ya_ref, yb_ref, w_pg, b_pg, w_ple, g_ref, b_ref, o_ref):
    rows = pl.ds(r0, SUB_TILE)
    h1 = h1_ref[rows, :]
    h_hi = h1.astype(jnp.bfloat16)
    p_b = p_ref[rows, :].astype(jnp.bfloat16)
    rec_t = route_t_ref[:, rows]
    padded = jnp.concatenate([rec_t, jnp.zeros((LANES - ROUTE_ROWS, SUB_TILE), jnp.float32)], axis=0)
    route = jnp.transpose(padded)
    yield
    gate_pre = jnp.dot(h_hi, w_pg[...], preferred_element_type=jnp.float32)
    ple_pre = jnp.dot(p_b, w_ple[...], preferred_element_type=jnp.float32)
    yield
    ple = ple_pre * jax.nn.sigmoid(gate_pre + b_pg[...])
    w1 = route[:, R_W1:R_W1 + 1]
    w2 = route[:, R_W2:R_W2 + 1]
    a_lo, a_hi = _unpack_bf16_pairs(ya_ref[rows, :])
    b_lo, b_hi = _unpack_bf16_pairs(yb_ref[rows, :])
    moe = jnp.concatenate([w1 * a_lo + w2 * b_lo, w1 * a_hi + w2 * b_hi], axis=-1)
    o_ref[rows, :] = _layernorm(DEEPNORM_ALPHA * h1 + ple + moe, g_ref[...], b_ref[...])


def _combine_kernel(h1_ref, p_ref, route_t_ref, ya_ref, yb_ref, w_pg, b_pg, w_ple, g_ref, b_ref, o_ref):
    chains = [_combine_rows(r0, h1_ref, p_ref, route_t_ref, ya_ref, yb_ref, w_pg, b_pg, w_ple, g_ref, b_ref, o_ref)
              for r0 in range(0, h1_ref.shape[0], SUB_TILE)]
    for t in range(COMBINE_PHASES + len(chains) - 1):
        for k, chain in reversed(list(enumerate(chains))):
            if 0 <= t - k < COMBINE_PHASES:
                next(chain, None)


def _run_combine(h1, p2, route_t, y_tok, w_pg, b_pg, w_ple, ln2_g, ln2_b):
    n_tok = h1.shape[0]
    n_t = n_tok // TOKEN_TILE
    tok_map = lambda i: (i, 0)
    const = lambda shape: pl.BlockSpec(shape, lambda i: (0, 0), pipeline_mode=pl.Buffered(1))
    in_specs = [
        pl.BlockSpec((TOKEN_TILE, D_MODEL), tok_map),
        pl.BlockSpec((TOKEN_TILE, PLE_DIM), tok_map),
        pl.BlockSpec((ROUTE_ROWS, TOKEN_TILE), lambda i: (0, i)),
        pl.BlockSpec((TOKEN_TILE, HALF), lambda i: (i, 0)),
        pl.BlockSpec((TOKEN_TILE, HALF), lambda i: (i + n_t, 0)),
        const((D_MODEL, D_MODEL)), const((1, D_MODEL)), const((PLE_DIM, D_MODEL)),
        const((1, D_MODEL)), const((1, D_MODEL)),
    ]
    return pl.pallas_call(
        _combine_kernel,
        grid=(n_t,),
        in_specs=in_specs,
        out_specs=pl.BlockSpec((TOKEN_TILE, D_MODEL), tok_map),
        out_shape=jax.ShapeDtypeStruct((n_tok, D_MODEL), jnp.float32),
        compiler_params=pltpu.CompilerParams(dimension_semantics=("arbitrary",), vmem_limit_bytes=VMEM_LIMIT),
    )(h1, p2, route_t, y_tok, y_tok, w_pg, b_pg, w_ple, ln2_g, ln2_b)


def _split_bf16(w):
    hi = w.astype(jnp.bfloat16)
    lo = (w - hi.astype(jnp.float32)).astype(jnp.bfloat16)
    return hi, lo


def kernel(x, p, ln_in_g, ln_in_b, w_in, conv_w, conv_b, pool_w, pool_scale, w_out, ln1_g, ln1_b,
           w_rg, b_rg, w_re, b_re, w_gate, w_up, w_down, w_pg, b_pg, w_ple, ln2_g, ln2_b):
    batch, seq, _ = x.shape
    n_tok = batch * seq
    bf = jnp.bfloat16
    row = lambda v: v.reshape(1, -1)

    w_r = jnp.concatenate([w_rg[0], jnp.transpose(w_re[0], (1, 0, 2)).reshape(D_MODEL, N_EXPERTS)], axis=1)
    w_r = jnp.pad(w_r, ((0, 0), (0, LANES - w_r.shape[1])))
    w_r_hi, w_r_lo = _split_bf16(w_r)
    w_r_cat = jnp.concatenate([w_r_hi, w_r_hi, w_r_lo], axis=0)
    b_r = jnp.pad(jnp.concatenate([b_rg[0], b_re[0].reshape(-1)]), (0, LANES - N_GROUPS - N_EXPERTS)).reshape(1, LANES)

    x2 = x.reshape(n_tok, D_MODEL)
    p2 = p[0].reshape(n_tok, PLE_DIM)
    mixer_weights = (row(ln_in_g), row(ln_in_b), w_in[0], conv_w, row(conv_b[0]),
                     pool_w[0], row(pool_scale[0]), w_out[0], row(ln1_g[0]), row(ln1_b[0]),
                     w_r_cat, b_r)
    combine_weights = (w_pg[0].astype(bf), row(b_pg[0]), w_ple[0].astype(bf), row(ln2_g[0]), row(ln2_b[0]))
    expert_weights = (w_gate[0].reshape(N_EXPERTS, D_MODEL, D_EXPERT),
                      w_up[0].reshape(N_EXPERTS, D_MODEL, D_EXPERT),
                      w_down[0].reshape(N_EXPERTS, D_EXPERT, D_MODEL))
    zero_rows = jnp.zeros((SC_WINDOW, HALF), jnp.uint32)

    n_rows = -(-(2 * n_tok + N_EXPERTS * (ROW_TILE - 1)) // ROW_TILE) * ROW_TILE
    xp, h1, route_t, counts = _run_mixer(x2, *mixer_weights, batch, seq)
    pos, tile_start, tiles_per_expert, pad_pos = _run_plan(route_t, counts, n_rows)
    x_sorted = _dispatch_rows(xp, pos, pad_pos, zero_rows, n_rows)
    y_sorted = _run_experts(tile_start[0, :N_EXPERTS], tiles_per_expert[0, :N_EXPERTS], x_sorted, *expert_weights)

    y_tok = _gather_rows(y_sorted, pos.reshape(-1))
    out = _run_combine(h1, p2, route_t, y_tok, *combine_weights)
    return out.reshape(batch, seq, D_MODEL)
```

```python
import functools

import jax
import jax.numpy as jnp
from jax import lax
from jax.experimental import pallas as pl
from jax.experimental.pallas import tpu as pltpu
from jax.experimental.pallas import tpu_sc as plsc

D_MODEL = 1024
CONV_WIDTH = 512
CONV_K = 3
POOL_WIDTH = 512
POOL_WINDOWS = (2, 4, 8, 16)
POOL_GW = 128
IN_PROJ = 3 * CONV_WIDTH + POOL_WIDTH
N_GROUPS = 4
EXPERTS_PER_GROUP = 8
N_EXPERTS = N_GROUPS * EXPERTS_PER_GROUP
D_EXPERT = 256
PLE_DIM = 256
LN_EPS = 1e-5
DEEPNORM_ALPHA = 2.0 ** 0.25

LANES = 128
HALF = D_MODEL // 2
CONV_HALO = 8
POOL_HALO = 16
SEQ_TILE = 1024
SUB_TILE = 256
MIXER_PHASES = 8
COMBINE_PHASES = 3
ROW_TILE = 256
TILE_BUFFERS = 8
SC_WORKERS = 32
SC_WINDOW = 64
GATHER_WINDOW = 64
GATHER_BUFFERS = 3
WEIGHT_BUFFERS = 3
TOKEN_TILE = 1024
VMEM_LIMIT = 56 * 1024 * 1024

R_ID1, R_ID2, R_RANK1, R_RANK2, R_W1, R_W2 = range(6)
ROUTE_ROWS = 8
LOGIT_ROWS = 40


def _layernorm(x, g, b):
    mu = jnp.mean(x, axis=-1, keepdims=True)
    xc = x - mu
    var = jnp.mean(xc * xc, axis=-1, keepdims=True)
    return xc * lax.rsqrt(var + LN_EPS) * g + b


def _pack_bf16_pairs(v):
    bits = lax.bitcast_convert_type(v.astype(jnp.bfloat16).astype(jnp.float32), jnp.uint32)
    return bits[:, HALF:] | (bits[:, :HALF] >> 16)


def _unpack_bf16_pairs(w):
    lo = lax.bitcast_convert_type(w << 16, jnp.float32)
    hi = lax.bitcast_convert_type(w & jnp.uint32(0xFFFF0000), jnp.float32)
    return lo, hi


def _mixer_kernel(x_ref, lnin_g, lnin_b, w_in_f32, conv_w, conv_b, pool_w_f32, pool_scale, w_out_f32,
                  ln1_g, ln1_b, w_r, b_r,
                  xp_ref, h1_ref, route_t_ref, counts_ref,
                  zbuf, vbuf, carry, w_in, pool_w, w_out):
    b = pl.program_id(0)
    s = pl.program_id(1)
    ts = x_ref.shape[0]

    @pl.when(s == 0)
    def _():
        zbuf[0:CONV_HALO, :] = jnp.zeros((CONV_HALO, CONV_WIDTH), jnp.float32)
        vbuf[0:POOL_HALO, :] = jnp.zeros((POOL_HALO, POOL_WIDTH), jnp.float32)

    @pl.when((b == 0) & (s == 0))
    def _():
        carry[...] = jnp.zeros_like(carry)
        w_in[...] = w_in_f32[...].astype(jnp.bfloat16)
        pool_w[...] = pool_w_f32[...].astype(jnp.bfloat16)
        w_out[...] = w_out_f32[...].astype(jnp.bfloat16)

    chains = [_mixer_rows(r0, s * ts + r0, x_ref, lnin_g, lnin_b, w_in, conv_w, conv_b, pool_w, pool_scale,
                          w_out, ln1_g, ln1_b, w_r, b_r,
                          xp_ref, h1_ref, route_t_ref, zbuf, vbuf, carry)
              for r0 in range(0, ts, SUB_TILE)]
    for t in range(MIXER_PHASES + len(chains) - 1):
        for k, chain in reversed(list(enumerate(chains))):
            if 0 <= t - k < MIXER_PHASES:
                next(chain, None)
    zbuf[0:CONV_HALO, :] = zbuf[ts:ts + CONV_HALO, :]
    vbuf[0:POOL_HALO, :] = vbuf[ts:ts + POOL_HALO, :]
    counts_ref[...] = jnp.transpose(carry[...])[0:1, :]


def _mixer_rows(r0, seq0, x_ref, lnin_g, lnin_b, w_in, conv_w, conv_b, pool_w, pool_scale, w_out,
                ln1_g, ln1_b, w_r, b_r,
                xp_ref, h1_ref, route_t_ref, zbuf, vbuf, carry):
    n = SUB_TILE
    rows = pl.ds(r0, n)
    h0 = _layernorm(x_ref[rows, :], lnin_g[...], lnin_b[...])
    h0b = h0.astype(jnp.bfloat16)
    yield
    u_a = jnp.dot(h0b, w_in[:, 0:IN_PROJ // 2], preferred_element_type=jnp.float32)
    yield
    u_b = jnp.dot(h0b, w_in[:, IN_PROJ // 2:], preferred_element_type=jnp.float32)
    yield
    b_g = u_a[:, 0:CONV_WIDTH]
    c_g = u_a[:, CONV_WIDTH:]
    v_c = u_b[:, 0:CONV_WIDTH]
    v_p = u_b[:, CONV_WIDTH:]

    zbuf[pl.ds(CONV_HALO + r0, n), :] = c_g * v_c
    zext = zbuf[pl.ds(r0, n + CONV_HALO), :]
    z1 = pltpu.roll(zext, 1, axis=0)[CONV_HALO:, :]
    z2 = pltpu.roll(zext, 2, axis=0)[CONV_HALO:, :]
    conv = z2 * conv_w[0, 0:1, :] + z1 * conv_w[0, 1:2, :] + zext[CONV_HALO:, :] * conv_w[0, 2:3, :] + conv_b[...]
    y_conv = b_g * conv

    vbuf[pl.ds(POOL_HALO + r0, n), :] = v_p
    vext = vbuf[pl.ds(r0, n + POOL_HALO), :]
    s2 = vext + pltpu.roll(vext, 1, axis=0)
    s4 = s2[:, POOL_GW:] + pltpu.roll(s2[:, POOL_GW:], 2, axis=0)
    s8 = s4[:, POOL_GW:] + pltpu.roll(s4[:, POOL_GW:], 4, axis=0)
    s16 = s8[:, POOL_GW:] + pltpu.roll(s8[:, POOL_GW:], 8, axis=0)
    wsums = (s2[POOL_HALO:, 0:POOL_GW], s4[POOL_HALO:, 0:POOL_GW],
             s8[POOL_HALO:, 0:POOL_GW], s16[POOL_HALO:, 0:POOL_GW])

    t_pos = seq0 + lax.broadcasted_iota(jnp.int32, (n, 1), 0)
    y_pool = []
    for j, w in enumerate(POOL_WINDOWS):
        inv_cnt = 1.0 / jnp.minimum(t_pos + 1, w).astype(jnp.float32)
        pooled = wsums[j] * inv_cnt - v_p[:, j * POOL_GW:(j + 1) * POOL_GW]
        y_pool.append(jnp.dot(pooled.astype(jnp.bfloat16), pool_w[j], preferred_element_type=jnp.float32))
    y_pool = jnp.concatenate(y_pool, axis=-1) * pool_scale[...]

    ycat = jnp.concatenate([y_conv, y_pool], axis=-1).astype(jnp.bfloat16)
    yield
    mix = jnp.dot(ycat, w_out[...], preferred_element_type=jnp.float32)
    yield
    h1 = _layernorm(DEEPNORM_ALPHA * h0 + mix, ln1_g[...], ln1_b[...])

    h_hi = h1.astype(jnp.bfloat16)
    h_lo = (h1 - h_hi.astype(jnp.float32)).astype(jnp.bfloat16)
    xp_ref[rows, :] = _pack_bf16_pairs(h1)
    h1_ref[rows, :] = h1
    hcat = jnp.concatenate([h_hi, h_lo, h_hi], axis=-1)
    yield
    logits = jnp.dot(hcat, w_r[...], preferred_element_type=jnp.float32) + b_r[...]
    yield
    lt = jnp.transpose(logits)[0:LOGIT_ROWS, :]
    rid = lax.broadcasted_iota(jnp.int32, (LOGIT_ROWS, n), 0).astype(jnp.float32)
    neg = jnp.float32(-jnp.inf)

    def first_argmax(vals):
        m = jnp.max(vals, axis=0, keepdims=True)
        idx = jnp.min(jnp.where(vals == m, rid, float(LOGIT_ROWS)), axis=0, keepdims=True)
        return m, idx

    g_mask = rid < N_GROUPS
    g_max, g_idx = first_argmax(jnp.where(g_mask, lt, neg))
    g_w = 1.0 / jnp.sum(jnp.where(g_mask, jnp.exp(lt - g_max), 0.0), axis=0, keepdims=True)

    e_lo = N_GROUPS + EXPERTS_PER_GROUP * g_idx
    e_vals = jnp.where((rid >= e_lo) & (rid < e_lo + EXPERTS_PER_GROUP), lt, neg)
    m1, i1 = first_argmax(e_vals)
    m2, i2 = first_argmax(jnp.where(rid == i1, neg, e_vals))
    e21 = jnp.exp(m2 - m1)
    w1 = g_w / (1.0 + e21)
    w2 = g_w * e21 / (1.0 + e21)
    id1 = i1 - N_GROUPS
    id2 = i2 - N_GROUPS

    eid = lax.broadcasted_iota(jnp.int32, (N_EXPERTS, n), 0).astype(jnp.float32)
    sel1 = eid == id1
    sel2 = eid == id2
    onehot = (sel1 | sel2).astype(jnp.float32)
    src = lax.broadcasted_iota(jnp.int32, (n, n), 0)
    dst = lax.broadcasted_iota(jnp.int32, (n, n), 1)
    earlier = (src < dst).astype(jnp.bfloat16)
    before = (jnp.dot(onehot.astype(jnp.bfloat16), earlier, preferred_element_type=jnp.float32)
              + carry[0:N_EXPERTS, 0:1])
    rank1 = jnp.sum(jnp.where(sel1, before, 0.0), axis=0, keepdims=True)
    rank2 = jnp.sum(jnp.where(sel2, before, 0.0), axis=0, keepdims=True)
    carry[0:N_EXPERTS, :] = carry[0:N_EXPERTS, :] + jnp.sum(onehot, axis=1, keepdims=True)

    rec_t = jnp.zeros((ROUTE_ROWS, n), jnp.float32)
    rec_row = lax.broadcasted_iota(jnp.int32, (ROUTE_ROWS, n), 0)
    for k, val in ((R_ID1, id1), (R_ID2, id2), (R_W1, w1), (R_W2, w2), (R_RANK1, rank1), (R_RANK2, rank2)):
        rec_t = jnp.where(rec_row == k, val, rec_t)
    route_t_ref[:, rows] = rec_t


def _run_mixer(x2, lnin_g, lnin_b, w_in, conv_w, conv_b, pool_w, pool_scale, w_out, ln1_g, ln1_b,
               w_r, b_r, batch, seq):
    n_tok = batch * seq
    n_s = seq // SEQ_TILE
    tok_map = lambda b, s: (b * n_s + s, 0)

    def const(shape):
        return pl.BlockSpec(shape, lambda b, s: (0,) * len(shape), pipeline_mode=pl.Buffered(1))

    in_specs = [
        pl.BlockSpec((SEQ_TILE, D_MODEL), tok_map),
        const((1, D_MODEL)), const((1, D_MODEL)),
        const((D_MODEL, IN_PROJ)),
        const((1, CONV_K, CONV_WIDTH)), const((1, CONV_WIDTH)),
        const((len(POOL_WINDOWS), POOL_GW, POOL_GW)), const((1, POOL_WIDTH)),
        const((D_MODEL, D_MODEL)),
        const((1, D_MODEL)), const((1, D_MODEL)),
        const((3 * D_MODEL, LANES)), const((1, LANES)),
    ]
    out_specs = [
        pl.BlockSpec((SEQ_TILE, HALF), tok_map),
        pl.BlockSpec((SEQ_TILE, D_MODEL), tok_map),
        pl.BlockSpec((ROUTE_ROWS, SEQ_TILE), lambda b, s: (0, b * n_s + s)),
        pl.BlockSpec((1, LANES), lambda b, s: (0, 0)),
    ]
    out_shape = [
        jax.ShapeDtypeStruct((n_tok, HALF), jnp.uint32),
        jax.ShapeDtypeStruct((n_tok, D_MODEL), jnp.float32),
        jax.ShapeDtypeStruct((ROUTE_ROWS, n_tok), jnp.float32),
        jax.ShapeDtypeStruct((1, LANES), jnp.float32),
    ]
    return pl.pallas_call(
        _mixer_kernel,
        grid=(batch, n_s),
        in_specs=in_specs,
        out_specs=out_specs,
        out_shape=out_shape,
        scratch_shapes=[
            pltpu.VMEM((SEQ_TILE + CONV_HALO, CONV_WIDTH), jnp.float32),
            pltpu.VMEM((SEQ_TILE + POOL_HALO, POOL_WIDTH), jnp.float32),
            pltpu.VMEM((LANES, LANES), jnp.float32),
            pltpu.VMEM((D_MODEL, IN_PROJ), jnp.bfloat16),
            pltpu.VMEM((len(POOL_WINDOWS), POOL_GW, POOL_GW), jnp.bfloat16),
            pltpu.VMEM((D_MODEL, D_MODEL), jnp.bfloat16),
        ],
        compiler_params=pltpu.CompilerParams(
            dimension_semantics=("arbitrary", "arbitrary"), vmem_limit_bytes=VMEM_LIMIT),
    )(x2, lnin_g, lnin_b, w_in, conv_w, conv_b, pool_w, pool_scale, w_out, ln1_g, ln1_b, w_r, b_r)


def _plan_kernel(rt_ref, counts_ref, pos_ref, tile_start_ref, tiles_ref, pad_ref, *, n_rows):
    lane = lax.broadcasted_iota(jnp.int32, (ROUTE_ROWS, LANES), 1)
    counts = jnp.broadcast_to(counts_ref[...], (ROUTE_ROWS, LANES))
    tiles = jnp.floor((counts + (ROW_TILE - 1)) * (1.0 / ROW_TILE))
    tile_end = tiles
    shift = 1
    while shift < N_EXPERTS:
        tile_end = tile_end + jnp.where(lane >= shift, pltpu.roll(tile_end, shift, axis=1), 0.0)
        shift *= 2
    row_start = (tile_end - tiles) * ROW_TILE
    tile_start_ref[...] = (tile_end - tiles)[0:1, :].astype(jnp.int32)
    tiles_ref[...] = tiles[0:1, :].astype(jnp.int32)

    rt = rt_ref[...]
    ids = rt[R_ID1:R_ID2 + 1, :]
    start = jnp.zeros_like(ids)
    for e in range(N_EXPERTS):
        start = jnp.where(ids == e, row_start[0:1, e:e + 1], start)
    pos_ref[...] = (start + rt[R_RANK1:R_RANK2 + 1, :]).astype(jnp.int32)

    sub = lax.broadcasted_iota(jnp.int32, (N_EXPERTS, LANES), 0)
    lane_e = lax.broadcasted_iota(jnp.int32, (N_EXPERTS, LANES), 1)
    diag = sub == lane_e
    pad_lo = jnp.sum(jnp.where(diag, (row_start + counts)[0:1, :], 0.0), axis=1, keepdims=True)
    pad_n = jnp.sum(jnp.where(diag, (tiles * ROW_TILE - counts)[0:1, :], 0.0), axis=1, keepdims=True)
    j = lax.broadcasted_iota(jnp.int32, (N_EXPERTS, ROW_TILE), 1).astype(jnp.float32)
    pad_ref[...] = jnp.where(j < pad_n, pad_lo + j, n_rows + j).astype(jnp.int32)


def _run_plan(route_t, counts, n_rows):
    n_tok = route_t.shape[1]
    full = lambda shape: pl.BlockSpec(shape, lambda i: (0, 0))
    return pl.pallas_call(
        functools.partial(_plan_kernel, n_rows=n_rows),
        grid=(1,),
        in_specs=[full((ROUTE_ROWS, n_tok)), full((1, LANES))],
        out_specs=[full((2, n_tok)), full((1, LANES)), full((1, LANES)), full((N_EXPERTS, ROW_TILE))],
        out_shape=[
            jax.ShapeDtypeStruct((2, n_tok), jnp.int32),
            jax.ShapeDtypeStruct((1, LANES), jnp.int32),
            jax.ShapeDtypeStruct((1, LANES), jnp.int32),
            jax.ShapeDtypeStruct((N_EXPERTS, ROW_TILE), jnp.int32),
        ],
        compiler_params=pltpu.CompilerParams(dimension_semantics=("arbitrary",)),
    )(route_t, counts)


def _sc_mesh():
    return plsc.VectorSubcoreMesh(core_axis_name="core", subcore_axis_name="subcore")


def _sc_worker_id():
    return lax.axis_index("core") * (SC_WORKERS // 2) + lax.axis_index("subcore")


def _dispatch_rows(xp, pos, pad_pos, zero_rows, n_rows):
    n_tok, width = xp.shape
    second = n_tok // SC_WINDOW
    n_win = n_tok // SC_WORKERS // SC_WINDOW
    n_pad = pad_pos.size // SC_WORKERS // SC_WINDOW
    as_windows = lambda v: v.reshape(-1, SC_WINDOW)

    @functools.partial(
        pl.kernel, out_type=jax.ShapeDtypeStruct((n_rows + ROW_TILE, width), xp.dtype), mesh=_sc_mesh(),
        scratch_types=[pltpu.VMEM((n_win, SC_WINDOW), jnp.int32), pltpu.VMEM((n_win, SC_WINDOW), jnp.int32),
                       pltpu.VMEM((n_pad, SC_WINDOW), jnp.int32),
                       pltpu.VMEM((2, SC_WINDOW, width), xp.dtype), pltpu.VMEM((SC_WINDOW, width), xp.dtype),
                       pltpu.SemaphoreType.DMA((2,)), pltpu.SemaphoreType.DMA((2,)), pltpu.SemaphoreType.DMA])
    def dispatch(xp_hbm, pos_hbm, pad_hbm, zero_hbm, out_hbm,
                 idx1, idx2, idxp, buf, zbuf, lsem, ssem, psem):
        wid = _sc_worker_id()
        pltpu.sync_copy(pad_hbm.at[pl.ds(wid * n_pad, n_pad)], idxp)
        pltpu.sync_copy(zero_hbm, zbuf)
        pads = [pltpu.make_async_copy(zbuf, out_hbm.at[idxp.at[j]], psem) for j in range(n_pad)]
        for cp in pads:
            cp.start()
        pltpu.sync_copy(pos_hbm.at[pl.ds(wid * n_win, n_win)], idx1)
        pltpu.sync_copy(pos_hbm.at[pl.ds(second + wid * n_win, n_win)], idx2)

        def load(j):
            rows = xp_hbm.at[pl.ds((wid * n_win + j) * SC_WINDOW, SC_WINDOW)]
            return pltpu.make_async_copy(rows, buf.at[j % 2], lsem.at[j % 2])

        def scatters(j):
            return [pltpu.make_async_copy(buf.at[j % 2], out_hbm.at[idx.at[j]], ssem.at[j % 2])
                    for idx in (idx1, idx2)]

        load(0).start()
        for j in range(n_win):
            load(j).wait()
            for cp in scatters(j):
                cp.start()
            if j >= 1:
                for cp in scatters(j - 1):
                    cp.wait()
            if j + 1 < n_win:
                load(j + 1).start()
        for cp in scatters(n_win - 1):
            cp.wait()
        for cp in pads:
            cp.wait()

    return dispatch(xp, as_windows(pos), as_windows(pad_pos), zero_rows)


def _gather_rows(src, idx):
    n_out, width = idx.shape[0], src.shape[1]
    n_win = n_out // SC_WORKERS // GATHER_WINDOW
    nbuf = GATHER_BUFFERS

    @functools.partial(
        pl.kernel, out_type=jax.ShapeDtypeStruct((n_out, width), src.dtype), mesh=_sc_mesh(),
        scratch_types=[pltpu.VMEM((n_win, GATHER_WINDOW), jnp.int32),
                       pltpu.VMEM((nbuf, GATHER_WINDOW, width), src.dtype),
                       pltpu.SemaphoreType.DMA((nbuf,)), pltpu.SemaphoreType.DMA((nbuf,))])
    def gather(src_hbm, idx_hbm, dst_hbm, idx_v, buf, gsem, ssem):
        wid = _sc_worker_id()
        pltpu.sync_copy(idx_hbm.at[pl.ds(wid * n_win, n_win)], idx_v)

        def fetch(j):
            return pltpu.make_async_copy(src_hbm.at[idx_v.at[j]], buf.at[j % nbuf], gsem.at[j % nbuf])

        def store(j):
            rows = dst_hbm.at[pl.ds((wid * n_win + j) * GATHER_WINDOW, GATHER_WINDOW)]
            return pltpu.make_async_copy(buf.at[j % nbuf], rows, ssem.at[j % nbuf])

        for j in range(min(nbuf - 1, n_win)):
            fetch(j).start()
        for j in range(n_win):
            fetch(j).wait()
            store(j).start()
            if j + nbuf - 1 < n_win:
                if j >= 1:
                    store(j - 1).wait()
                fetch(j + nbuf - 1).start()
        for j in range(max(0, n_win - nbuf), n_win):
            store(j).wait()

    return gather(src, idx.reshape(-1, GATHER_WINDOW))


def _tile_copy(hbm, buf, sem, tile, slot, to_hbm):
    rows = hbm.at[pl.ds(pl.multiple_of(tile * ROW_TILE, ROW_TILE), ROW_TILE)]
    if to_hbm:
        return pltpu.make_async_copy(buf.at[slot], rows, sem.at[slot])
    return pltpu.make_async_copy(rows, buf.at[slot], sem.at[slot])


def _weight_copies(w_hbm, wbuf, wsem, expert):
    slot = expert % WEIGHT_BUFFERS
    return [pltpu.make_async_copy(w.at[expert], buf.at[slot], wsem.at[slot]) for w, buf in zip(w_hbm, wbuf)]


def _expert_kernel(ts_ref, nte_ref, x_hbm, wg_hbm, wu_hbm, wd_hbm, y_hbm,
                   xbuf, ybuf, xsem, ysem, wg_buf, wu_buf, wd_buf, wsem, wgu_bf, wd_bf, act_ref, *, max_tiles):
    e = pl.program_id(0)
    first = ts_ref[e]
    count = nte_ref[e]
    n_tiles = ts_ref[N_EXPERTS - 1] + nte_ref[N_EXPERTS - 1]
    ahead = TILE_BUFFERS - 2
    w_hbm = (wg_hbm, wu_hbm, wd_hbm)
    wbuf = (wg_buf, wu_buf, wd_buf)

    @pl.when(e == 0)
    def _():
        for g in range(ahead):
            @pl.when(g < n_tiles)
            def _():
                _tile_copy(x_hbm, xbuf, xsem, g, g, False).start()
        for k in range(WEIGHT_BUFFERS - 1):
            for cp in _weight_copies(w_hbm, wbuf, wsem, k):
                cp.start()

    @pl.when(e + WEIGHT_BUFFERS - 1 < N_EXPERTS)
    def _():
        for cp in _weight_copies(w_hbm, wbuf, wsem, e + WEIGHT_BUFFERS - 1):
            cp.start()

    for cp in _weight_copies(w_hbm, wbuf, wsem, e):
        cp.wait()

    def sync_x(g):
        _tile_copy(x_hbm, xbuf, xsem, g, g % TILE_BUFFERS, False).wait()

        @pl.when(g + ahead < n_tiles)
        def _():
            _tile_copy(x_hbm, xbuf, xsem, g + ahead, (g + ahead) % TILE_BUFFERS, False).start()

    def sync_y_slot(g):
        @pl.when(g >= TILE_BUFFERS)
        def _():
            _tile_copy(y_hbm, ybuf, ysem, g - TILE_BUFFERS, g % TILE_BUFFERS, True).wait()

    def up_proj(g):
        lo, hi = _unpack_bf16_pairs(xbuf[g % TILE_BUFFERS])
        return (jnp.dot(lo.astype(jnp.bfloat16), wgu_bf[0:HALF, :], preferred_element_type=jnp.float32)
                + jnp.dot(hi.astype(jnp.bfloat16), wgu_bf[HALF:, :], preferred_element_type=jnp.float32))

    def put_act(hgu, which):
        hg = hgu[:, 0:D_EXPERT]
        act_ref[which] = (hg * jax.nn.sigmoid(hg) * hgu[:, D_EXPERT:]).astype(jnp.bfloat16)

    def down_proj(g, which):
        y = jnp.dot(act_ref[which], wd_bf[...], preferred_element_type=jnp.float32)
        ybuf[g % TILE_BUFFERS] = _pack_bf16_pairs(y)

    def start_y(g):
        _tile_copy(y_hbm, ybuf, ysem, g, g % TILE_BUFFERS, True).start()

    @pl.when(count > 0)
    def _():
        slot = e % WEIGHT_BUFFERS
        wgu_bf[:, 0:D_EXPERT] = wg_buf[slot].astype(jnp.bfloat16)
        wgu_bf[:, D_EXPERT:] = wu_buf[slot].astype(jnp.bfloat16)
        wd_bf[...] = wd_buf[slot].astype(jnp.bfloat16)

        sync_x(first)
        put_act(up_proj(first), 0)

        def pair_body(i, c):
            g = first + 1 + 2 * i
            sync_x(g)
            sync_x(g + 1)
            sync_y_slot(g - 1)
            sync_y_slot(g)
            down_proj(g - 1, 0)
            hgu0 = up_proj(g)
            put_act(hgu0, 1)
            hgu1 = up_proj(g + 1)
            down_proj(g, 1)
            put_act(hgu1, 0)
            start_y(g - 1)
            start_y(g)
            return c

        n_pairs = (count - 1) // 2
        lax.fori_loop(0, n_pairs, pair_body, 0)

        @pl.when((count - 1) % 2 == 1)
        def _():
            g = first + count - 1
            sync_x(g)
            sync_y_slot(g - 1)
            down_proj(g - 1, 0)
            put_act(up_proj(g), 0)
            start_y(g - 1)

        last = first + count - 1
        sync_y_slot(last)
        down_proj(last, 0)
        start_y(last)

    @pl.when(e == N_EXPERTS - 1)
    def _():
        for k in range(TILE_BUFFERS, 0, -1):
            @pl.when(n_tiles >= k)
            def _():
                _tile_copy(y_hbm, ybuf, ysem, n_tiles - k, (n_tiles - k) % TILE_BUFFERS, True).wait()

        ybuf[0] = jnp.zeros((ROW_TILE, HALF), jnp.uint32)

        def fill(g, c):
            cp = _tile_copy(y_hbm, ybuf, ysem, g, 0, True)
            cp.start()
            cp.wait()
            return c

        lax.fori_loop(n_tiles, max_tiles, fill, 0)


def _run_experts(tile_start, tiles_per_expert, x_sorted, w_gate, w_up, w_down):
    n_rows = x_sorted.shape[0]
    max_tiles = n_rows // ROW_TILE
    hbm = pl.BlockSpec(memory_space=pl.ANY)

    grid_spec = pltpu.PrefetchScalarGridSpec(
        num_scalar_prefetch=2,
        grid=(N_EXPERTS,),
        in_specs=[hbm, hbm, hbm, hbm],
        out_specs=hbm,
        scratch_shapes=[
            pltpu.VMEM((TILE_BUFFERS, ROW_TILE, HALF), jnp.uint32),
            pltpu.VMEM((TILE_BUFFERS, ROW_TILE, HALF), jnp.uint32),
            pltpu.SemaphoreType.DMA((TILE_BUFFERS,)),
            pltpu.SemaphoreType.DMA((TILE_BUFFERS,)),
            pltpu.VMEM((WEIGHT_BUFFERS, D_MODEL, D_EXPERT), jnp.float32),
            pltpu.VMEM((WEIGHT_BUFFERS, D_MODEL, D_EXPERT), jnp.float32),
            pltpu.VMEM((WEIGHT_BUFFERS, D_EXPERT, D_MODEL), jnp.float32),
            pltpu.SemaphoreType.DMA((WEIGHT_BUFFERS,)),
            pltpu.VMEM((D_MODEL, 2 * D_EXPERT), jnp.bfloat16),
            pltpu.VMEM((D_EXPERT, D_MODEL), jnp.bfloat16),
            pltpu.VMEM((2, ROW_TILE, D_EXPERT), jnp.bfloat16),
        ],
    )
    return pl.pallas_call(
        functools.partial(_expert_kernel, max_tiles=max_tiles),
        grid_spec=grid_spec,
        out_shape=jax.ShapeDtypeStruct((n_rows, HALF), jnp.uint32),
        compiler_params=pltpu.CompilerParams(
            dimension_semantics=("arbitrary",), vmem_limit_bytes=VMEM_LIMIT),
    )(tile_start, tiles_per_expert, x_sorted, w_gate, w_up, w_down)


def _combine_rows(r0, h1_ref, p_ref, route_t_ref, ya_ref, yb_ref, w_pg, b_pg, w_ple, g_ref, b_ref, o_ref):
    rows = pl.ds(r0, SUB_TILE)
    h1 = h1_ref[rows, :]
    h_hi = h1.astype(jnp.bfloat16)
    p_b = p_ref[rows, :].astype(jnp.bfloat16)
    rec_t = route_t_ref[:, rows]
    padded = jnp.concatenate([rec_t, jnp.zeros((LANES - ROUTE_ROWS, SUB_TILE), jnp.float32)], axis=0)
    route = jnp.transpose(padded)
    yield
    gate_pre = jnp.dot(h_hi, w_pg[...], preferred_element_type=jnp.float32)
    ple_pre = jnp.dot(p_b, w_ple[...], preferred_element_type=jnp.float32)
    yield
    ple = ple_pre * jax.nn.sigmoid(gate_pre + b_pg[...])
    w1 = route[:, R_W1:R_W1 + 1]
    w2 = route[:, R_W2:R_W2 + 1]
    a_lo, a_hi = _unpack_bf16_pairs(ya_ref[rows, :])
    b_lo, b_hi = _unpack_bf16_pairs(yb_ref[rows, :])
    moe = jnp.concatenate([w1 * a_lo + w2 * b_lo, w1 * a_hi + w2 * b_hi], axis=-1)
    o_ref[rows, :] = _layernorm(DEEPNORM_ALPHA * h1 + ple + moe, g_ref[...], b_ref[...])


def _combine_kernel(h1_ref, p_ref, route_t_ref, ya_ref, yb_ref, w_pg, b_pg, w_ple, g_ref, b_ref, o_ref):
    chains = [_combine_rows(r0, h1_ref, p_ref, route_t_ref, ya_ref, yb_ref, w_pg, b_pg, w_ple, g_ref, b_ref, o_ref)
              for r0 in range(0, h1_ref.shape[0], SUB_TILE)]
    for t in range(COMBINE_PHASES + len(chains) - 1):
        for k, chain in reversed(list(enumerate(chains))):
            if 0 <= t - k < COMBINE_PHASES:
                next(chain, None)


def _run_combine(h1, p2, route_t, y_tok, w_pg, b_pg, w_ple, ln2_g, ln2_b):
    n_tok = h1.shape[0]
    n_t = n_tok // TOKEN_TILE
    tok_map = lambda i: (i, 0)
    const = lambda shape: pl.BlockSpec(shape, lambda i: (0, 0), pipeline_mode=pl.Buffered(1))
    in_specs = [
        pl.BlockSpec((TOKEN_TILE, D_MODEL), tok_map),
        pl.BlockSpec((TOKEN_TILE, PLE_DIM), tok_map),
        pl.BlockSpec((ROUTE_ROWS, TOKEN_TILE), lambda i: (0, i)),
        pl.BlockSpec((TOKEN_TILE, HALF), lambda i: (i, 0)),
        pl.BlockSpec((TOKEN_TILE, HALF), lambda i: (i + n_t, 0)),
        const((D_MODEL, D_MODEL)), const((1, D_MODEL)), const((PLE_DIM, D_MODEL)),
        const((1, D_MODEL)), const((1, D_MODEL)),
    ]
    return pl.pallas_call(
        _combine_kernel,
        grid=(n_t,),
        in_specs=in_specs,
        out_specs=pl.BlockSpec((TOKEN_TILE, D_MODEL), tok_map),
        out_shape=jax.ShapeDtypeStruct((n_tok, D_MODEL), jnp.float32),
        compiler_params=pltpu.CompilerParams(dimension_semantics=("arbitrary",), vmem_limit_bytes=VMEM_LIMIT),
    )(h1, p2, route_t, y_tok, y_tok, w_pg, b_pg, w_ple, ln2_g, ln2_b)


def _split_bf16(w):
    hi = w.astype(jnp.bfloat16)
    lo = (w - hi.astype(jnp.float32)).astype(jnp.bfloat16)
    return hi, lo


def kernel(x, p, ln_in_g, ln_in_b, w_in, conv_w, conv_b, pool_w, pool_scale, w_out, ln1_g, ln1_b,
           w_rg, b_rg, w_re, b_re, w_gate, w_up, w_down, w_pg, b_pg, w_ple, ln2_g, ln2_b):
    batch, seq, _ = x.shape
    n_tok = batch * seq
    bf = jnp.bfloat16
    row = lambda v: v.reshape(1, -1)

    w_r = jnp.concatenate([w_rg[0], jnp.transpose(w_re[0], (1, 0, 2)).reshape(D_MODEL, N_EXPERTS)], axis=1)
    w_r = jnp.pad(w_r, ((0, 0), (0, LANES - w_r.shape[1])))
    w_r_hi, w_r_lo = _split_bf16(w_r)
    w_r_cat = jnp.concatenate([w_r_hi, w_r_hi, w_r_lo], axis=0)
    b_r = jnp.pad(jnp.concatenate([b_rg[0], b_re[0].reshape(-1)]), (0, LANES - N_GROUPS - N_EXPERTS)).reshape(1, LANES)

    x2 = x.reshape(n_tok, D_MODEL)
    p2 = p[0].reshape(n_tok, PLE_DIM)
    mixer_weights = (row(ln_in_g), row(ln_in_b), w_in[0], conv_w, row(conv_b[0]),
                     pool_w[0], row(pool_scale[0]), w_out[0], row(ln1_g[0]), row(ln1_b[0]),
                     w_r_cat, b_r)
    combine_weights = (w_pg[0].astype(bf), row(b_pg[0]), w_ple[0].astype(bf), row(ln2_g[0]), row(ln2_b[0]))
    expert_weights = (w_gate[0].reshape(N_EXPERTS, D_MODEL, D_EXPERT),
                      w_up[0].reshape(N_EXPERTS, D_MODEL, D_EXPERT),
                      w_down[0].reshape(N_EXPERTS, D_EXPERT, D_MODEL))
    zero_rows = jnp.zeros((SC_WINDOW, HALF), jnp.uint32)

    n_rows = -(-(2 * n_tok + N_EXPERTS * (ROW_TILE - 1)) // ROW_TILE) * ROW_TILE
    xp, h1, route_t, counts = _run_mixer(x2, *mixer_weights, batch, seq)
    pos, tile_start, tiles_per_expert, pad_pos = _run_plan(route_t, counts, n_rows)
    x_sorted = _dispatch_rows(xp, pos, pad_pos, zero_rows, n_rows)
    y_sorted = _run_experts(tile_start[0, :N_EXPERTS], tiles_per_expert[0, :N_EXPERTS], x_sorted, *expert_weights)

    y_tok = _gather_rows(y_sorted, pos.reshape(-1))
    out = _run_combine(h1, p2, route_t, y_tok, *combine_weights)
    return out.reshape(batch, seq, D_MODEL)
```

```python
import functools

import jax
import jax.numpy as jnp
from jax import lax
from jax.experimental import pallas as pl
from jax.experimental.pallas import tpu as pltpu
from jax.experimental.pallas import tpu_sc as plsc

D_MODEL = 1024
CONV_WIDTH = 512
CONV_K = 3
POOL_WIDTH = 512
POOL_WINDOWS = (2, 4, 8, 16)
POOL_GW = 128
IN_PROJ = 3 * CONV_WIDTH + POOL_WIDTH
N_GROUPS = 4
EXPERTS_PER_GROUP = 8
N_EXPERTS = N_GROUPS * EXPERTS_PER_GROUP
D_EXPERT = 256
PLE_DIM = 256
LN_EPS = 1e-5
DEEPNORM_ALPHA = 2.0 ** 0.25

LANES = 128
HALF = D_MODEL // 2
CONV_HALO = 8
POOL_HALO = 16
SEQ_TILE = 1024
SUB_TILE = 256
MIXER_PHASES = 8
COMBINE_PHASES = 3
ROW_TILE = 256
TILE_BUFFERS = 8
SC_WORKERS = 32
SC_WINDOW = 64
GATHER_WINDOW = 64
GATHER_BUFFERS = 3
WEIGHT_BUFFERS = 3
TOKEN_TILE = 1024
VMEM_LIMIT = 56 * 1024 * 1024

R_ID1, R_ID2, R_RANK1, R_RANK2, R_W1, R_W2 = range(6)
ROUTE_ROWS = 8
LOGIT_ROWS = 40


def _layernorm(x, g, b):
    mu = jnp.mean(x, axis=-1, keepdims=True)
    xc = x - mu
    var = jnp.mean(xc * xc, axis=-1, keepdims=True)
    return xc * lax.rsqrt(var + LN_EPS) * g + b


def _pack_bf16_pairs(v):
    bits = lax.bitcast_convert_type(v.astype(jnp.bfloat16).astype(jnp.float32), jnp.uint32)
    return bits[:, HALF:] | (bits[:, :HALF] >> 16)


def _unpack_bf16_pairs(w):
    lo = lax.bitcast_convert_type(w << 16, jnp.float32)
    hi = lax.bitcast_convert_type(w & jnp.uint32(0xFFFF0000), jnp.float32)
    return lo, hi


def _mixer_kernel(x_ref, lnin_g, lnin_b, w_in_f32, conv_w, conv_b, pool_w_f32, pool_scale, w_out_f32,
                  ln1_g, ln1_b, w_r, b_r,
                  xp_ref, h1_ref, route_t_ref, counts_ref,
                  zbuf, vbuf, carry, w_in, pool_w, w_out):
    b = pl.program_id(0)
    s = pl.program_id(1)
    ts = x_ref.shape[0]

    @pl.when(s == 0)
    def _():
        zbuf[0:CONV_HALO, :] = jnp.zeros((CONV_HALO, CONV_WIDTH), jnp.float32)
        vbuf[0:POOL_HALO, :] = jnp.zeros((POOL_HALO, POOL_WIDTH), jnp.float32)

    @pl.when((b == 0) & (s == 0))
    def _():
        carry[...] = jnp.zeros_like(carry)
        w_in[...] = w_in_f32[...].astype(jnp.bfloat16)
        pool_w[...] = pool_w_f32[...].astype(jnp.bfloat16)
        w_out[...] = w_out_f32[...].astype(jnp.bfloat16)

    chains = [_mixer_rows(r0, s * ts + r0, x_ref, lnin_g, lnin_b, w_in, conv_w, conv_b, pool_w, pool_scale,
                          w_out, ln1_g, ln1_b, w_r, b_r,
                          xp_ref, h1_ref, route_t_ref, zbuf, vbuf, carry)
              for r0 in range(0, ts, SUB_TILE)]
    for t in range(MIXER_PHASES + len(chains) - 1):
        for k, chain in reversed(list(enumerate(chains))):
            if 0 <= t - k < MIXER_PHASES:
                next(chain, None)
    zbuf[0:CONV_HALO, :] = zbuf[ts:ts + CONV_HALO, :]
    vbuf[0:POOL_HALO, :] = vbuf[ts:ts + POOL_HALO, :]
    counts_ref[...] = jnp.transpose(carry[...])[0:1, :]


def _mixer_rows(r0, seq0, x_ref, lnin_g, lnin_b, w_in, conv_w, conv_b, pool_w, pool_scale, w_out,
                ln1_g, ln1_b, w_r, b_r,
                xp_ref, h1_ref, route_t_ref, zbuf, vbuf, carry):
    n = SUB_TILE
    rows = pl.ds(r0, n)
    h0 = _layernorm(x_ref[rows, :], lnin_g[...], lnin_b[...])
    h0b = h0.astype(jnp.bfloat16)
    yield
    u_a = jnp.dot(h0b, w_in[:, 0:IN_PROJ // 2], preferred_element_type=jnp.float32)
    yield
    u_b = jnp.dot(h0b, w_in[:, IN_PROJ // 2:], preferred_element_type=jnp.float32)
    yield
    b_g = u_a[:, 0:CONV_WIDTH]
    c_g = u_a[:, CONV_WIDTH:]
    v_c = u_b[:, 0:CONV_WIDTH]
    v_p = u_b[:, CONV_WIDTH:]

    zbuf[pl.ds(CONV_HALO + r0, n), :] = c_g * v_c
    zext = zbuf[pl.ds(r0, n + CONV_HALO), :]
    z1 = pltpu.roll(zext, 1, axis=0)[CONV_HALO:, :]
    z2 = pltpu.roll(zext, 2, axis=0)[CONV_HALO:, :]
    conv = z2 * conv_w[0, 0:1, :] + z1 * conv_w[0, 1:2, :] + zext[CONV_HALO:, :] * conv_w[0, 2:3, :] + conv_b[...]
    y_conv = b_g * conv

    vbuf[pl.ds(POOL_HALO + r0, n), :] = v_p
    vext = vbuf[pl.ds(r0, n + POOL_HALO), :]
    s2 = vext + pltpu.roll(vext, 1, axis=0)
    s4 = s2[:, POOL_GW:] + pltpu.roll(s2[:, POOL_GW:], 2, axis=0)
    s8 = s4[:, POOL_GW:] + pltpu.roll(s4[:, POOL_GW:], 4, axis=0)
    s16 = s8[:, POOL_GW:] + pltpu.roll(s8[:, POOL_GW:], 8, axis=0)
    wsums = (s2[POOL_HALO:, 0:POOL_GW], s4[POOL_HALO:, 0:POOL_GW],
             s8[POOL_HALO:, 0:POOL_GW], s16[POOL_HALO:, 0:POOL_GW])

    t_pos = seq0 + lax.broadcasted_iota(jnp.int32, (n, 1), 0)
    y_pool = []
    for j, w in enumerate(POOL_WINDOWS):
        inv_cnt = 1.0 / jnp.minimum(t_pos + 1, w).astype(jnp.float32)
        pooled = wsums[j] * inv_cnt - v_p[:, j * POOL_GW:(j + 1) * POOL_GW]
        y_pool.append(jnp.dot(pooled.astype(jnp.bfloat16), pool_w[j], preferred_element_type=jnp.float32))
    y_pool = jnp.concatenate(y_pool, axis=-1) * pool_scale[...]

    ycat = jnp.concatenate([y_conv, y_pool], axis=-1).astype(jnp.bfloat16)
    yield
    mix = jnp.dot(ycat, w_out[...], preferred_element_type=jnp.float32)
    yield
    h1 = _layernorm(DEEPNORM_ALPHA * h0 + mix, ln1_g[...], ln1_b[...])

    h_hi = h1.astype(jnp.bfloat16)
    h_lo = (h1 - h_hi.astype(jnp.float32)).astype(jnp.bfloat16)
    xp_ref[rows, :] = _pack_bf16_pairs(h1)
    h1_ref[rows, :] = h1
    hcat = jnp.concatenate([h_hi, h_lo, h_hi], axis=-1)
    yield
    logits = jnp.dot(hcat, w_r[...], preferred_element_type=jnp.float32) + b_r[...]
    yield
    lt = jnp.transpose(logits)[0:LOGIT_ROWS, :]
    rid = lax.broadcasted_iota(jnp.int32, (LOGIT_ROWS, n), 0).astype(jnp.float32)
    neg = jnp.float32(-jnp.inf)

    def first_argmax(vals):
        m = jnp.max(vals, axis=0, keepdims=True)
        idx = jnp.min(jnp.where(vals == m, rid, float(LOGIT_ROWS)), axis=0, keepdims=True)
        return m, idx

    g_mask = rid < N_GROUPS
    g_max, g_idx = first_argmax(jnp.where(g_mask, lt, neg))
    g_w = 1.0 / jnp.sum(jnp.where(g_mask, jnp.exp(lt - g_max), 0.0), axis=0, keepdims=True)

    e_lo = N_GROUPS + EXPERTS_PER_GROUP * g_idx
    e_vals = jnp.where((rid >= e_lo) & (rid < e_lo + EXPERTS_PER_GROUP), lt, neg)
    m1, i1 = first_argmax(e_vals)
    m2, i2 = first_argmax(jnp.where(rid == i1, neg, e_vals))
    e21 = jnp.exp(m2 - m1)
    w1 = g_w / (1.0 + e21)
    w2 = g_w * e21 / (1.0 + e21)
    id1 = i1 - N_GROUPS
    id2 = i2 - N_GROUPS

    eid = lax.broadcasted_iota(jnp.int32, (N_EXPERTS, n), 0).astype(jnp.float32)
    sel1 = eid == id1
    sel2 = eid == id2
    onehot = (sel1 | sel2).astype(jnp.float32)
    src = lax.broadcasted_iota(jnp.int32, (n, n), 0)
    dst = lax.broadcasted_iota(jnp.int32, (n, n), 1)
    earlier = (src < dst).astype(jnp.bfloat16)
    before = (jnp.dot(onehot.astype(jnp.bfloat16), earlier, preferred_element_type=jnp.float32)
              + carry[0:N_EXPERTS, 0:1])
    rank1 = jnp.sum(jnp.where(sel1, before, 0.0), axis=0, keepdims=True)
    rank2 = jnp.sum(jnp.where(sel2, before, 0.0), axis=0, keepdims=True)
    carry[0:N_EXPERTS, :] = carry[0:N_EXPERTS, :] + jnp.sum(onehot, axis=1, keepdims=True)

    rec_t = jnp.zeros((ROUTE_ROWS, n), jnp.float32)
    rec_row = lax.broadcasted_iota(jnp.int32, (ROUTE_ROWS, n), 0)
    for k, val in ((R_ID1, id1), (R_ID2, id2), (R_W1, w1), (R_W2, w2), (R_RANK1, rank1), (R_RANK2, rank2)):
        rec_t = jnp.where(rec_row == k, val, rec_t)
    route_t_ref[:, rows] = rec_t


def _run_mixer(x2, lnin_g, lnin_b, w_in, conv_w, conv_b, pool_w, pool_scale, w_out, ln1_g, ln1_b,
               w_r, b_r, batch, seq):
    n_tok = batch * seq
    n_s = seq // SEQ_TILE
    tok_map = lambda b, s: (b * n_s + s, 0)

    def const(shape):
        return pl.BlockSpec(shape, lambda b, s: (0,) * len(shape), pipeline_mode=pl.Buffered(1))

    in_specs = [
        pl.BlockSpec((SEQ_TILE, D_MODEL), tok_map),
        const((1, D_MODEL)), const((1, D_MODEL)),
        const((D_MODEL, IN_PROJ)),
        const((1, CONV_K, CONV_WIDTH)), const((1, CONV_WIDTH)),
        const((len(POOL_WINDOWS), POOL_GW, POOL_GW)), const((1, POOL_WIDTH)),
        const((D_MODEL, D_MODEL)),
        const((1, D_MODEL)), const((1, D_MODEL)),
        const((3 * D_MODEL, LANES)), const((1, LANES)),
    ]
    out_specs = [
        pl.BlockSpec((SEQ_TILE, HALF), tok_map),
        pl.BlockSpec((SEQ_TILE, D_MODEL), tok_map),
        pl.BlockSpec((ROUTE_ROWS, SEQ_TILE), lambda b, s: (0, b * n_s + s)),
        pl.BlockSpec((1, LANES), lambda b, s: (0, 0)),
    ]
    out_shape = [
        jax.ShapeDtypeStruct((n_tok, HALF), jnp.uint32),
        jax.ShapeDtypeStruct((n_tok, D_MODEL), jnp.float32),
        jax.ShapeDtypeStruct((ROUTE_ROWS, n_tok), jnp.float32),
        jax.ShapeDtypeStruct((1, LANES), jnp.float32),
    ]
    return pl.pallas_call(
        _mixer_kernel,
        grid=(batch, n_s),
        in_specs=in_specs,
        out_specs=out_specs,
        out_shape=out_shape,
        scratch_shapes=[
            pltpu.VMEM((SEQ_TILE + CONV_HALO, CONV_WIDTH), jnp.float32),
            pltpu.VMEM((SEQ_TILE + POOL_HALO, POOL_WIDTH), jnp.float32),
            pltpu.VMEM((LANES, LANES), jnp.float32),
            pltpu.VMEM((D_MODEL, IN_PROJ), jnp.bfloat16),
            pltpu.VMEM((len(POOL_WINDOWS), POOL_GW, POOL_GW), jnp.bfloat16),
            pltpu.VMEM((D_MODEL, D_MODEL), jnp.bfloat16),
        ],
        compiler_params=pltpu.CompilerParams(
            dimension_semantics=("arbitrary", "arbitrary"), vmem_limit_bytes=VMEM_LIMIT),
    )(x2, lnin_g, lnin_b, w_in, conv_w, conv_b, pool_w, pool_scale, w_out, ln1_g, ln1_b, w_r, b_r)


def _plan_kernel(rt_ref, counts_ref, pos_ref, tile_start_ref, tiles_ref, pad_ref, *, n_rows):
    lane = lax.broadcasted_iota(jnp.int32, (ROUTE_ROWS, LANES), 1)
    counts = jnp.broadcast_to(counts_ref[...], (ROUTE_ROWS, LANES))
    tiles = jnp.floor((counts + (ROW_TILE - 1)) * (1.0 / ROW_TILE))
    tile_end = tiles
    shift = 1
    while shift < N_EXPERTS:
        tile_end = tile_end + jnp.where(lane >= shift, pltpu.roll(tile_end, shift, axis=1), 0.0)
        shift *= 2
    row_start = (tile_end - tiles) * ROW_TILE
    tile_start_ref[...] = (tile_end - tiles)[0:1, :].astype(jnp.int32)
    tiles_ref[...] = tiles[0:1, :].astype(jnp.int32)

    rt = rt_ref[...]
    ids = rt[R_ID1:R_ID2 + 1, :]
    start = jnp.zeros_like(ids)
    for e in range(N_EXPERTS):
        start = jnp.where(ids == e, row_start[0:1, e:e + 1], start)
    pos_ref[...] = (start + rt[R_RANK1:R_RANK2 + 1, :]).astype(jnp.int32)

    sub = lax.broadcasted_iota(jnp.int32, (N_EXPERTS, LANES), 0)
    lane_e = lax.broadcasted_iota(jnp.int32, (N_EXPERTS, LANES), 1)
    diag = sub == lane_e
    pad_lo = jnp.sum(jnp.where(diag, (row_start + counts)[0:1, :], 0.0), axis=1, keepdims=True)
    pad_n = jnp.sum(jnp.where(diag, (tiles * ROW_TILE - counts)[0:1, :], 0.0), axis=1, keepdims=True)
    j = lax.broadcasted_iota(jnp.int32, (N_EXPERTS, ROW_TILE), 1).astype(jnp.float32)
    pad_ref[...] = jnp.where(j < pad_n, pad_lo + j, n_rows + j).astype(jnp.int32)


def _run_plan(route_t, counts, n_rows):
    n_tok = route_t.shape[1]
    full = lambda shape: pl.BlockSpec(shape, lambda i: (0, 0))
    return pl.pallas_call(
        functools.partial(_plan_kernel, n_rows=n_rows),
        grid=(1,),
        in_specs=[full((ROUTE_ROWS, n_tok)), full((1, LANES))],
        out_specs=[full((2, n_tok)), full((1, LANES)), full((1, LANES)), full((N_EXPERTS, ROW_TILE))],
        out_shape=[
            jax.ShapeDtypeStruct((2, n_tok), jnp.int32),
            jax.ShapeDtypeStruct((1, LANES), jnp.int32),
            jax.ShapeDtypeStruct((1, LANES), jnp.int32),
            jax.ShapeDtypeStruct((N_EXPERTS, ROW_TILE), jnp.int32),
        ],
        compiler_params=pltpu.CompilerParams(dimension_semantics=("arbitrary",)),
    )(route_t, counts)


def _sc_mesh():
    return plsc.VectorSubcoreMesh(core_axis_name="core", subcore_axis_name="subcore")


def _sc_worker_id():
    return lax.axis_index("core") * (SC_WORKERS // 2) + lax.axis_index("subcore")


def _dispatch_rows(xp, pos, pad_pos, zero_rows, n_rows):
    n_tok, width = xp.shape
    second = n_tok // SC_WINDOW
    n_win = n_tok // SC_WORKERS // SC_WINDOW
    n_pad = pad_pos.size // SC_WORKERS // SC_WINDOW
    as_windows = lambda v: v.reshape(-1, SC_WINDOW)

    @functools.partial(
        pl.kernel, out_type=jax.ShapeDtypeStruct((n_rows + ROW_TILE, width), xp.dtype), mesh=_sc_mesh(),
        scratch_types=[pltpu.VMEM((n_win, SC_WINDOW), jnp.int32), pltpu.VMEM((n_win, SC_WINDOW), jnp.int32),
                       pltpu.VMEM((n_pad, SC_WINDOW), jnp.int32),
                       pltpu.VMEM((2, SC_WINDOW, width), xp.dtype), pltpu.VMEM((SC_WINDOW, width), xp.dtype),
                       pltpu.SemaphoreType.DMA((2,)), pltpu.SemaphoreType.DMA((2,)), pltpu.SemaphoreType.DMA])
    def dispatch(xp_hbm, pos_hbm, pad_hbm, zero_hbm, out_hbm,
                 idx1, idx2, idxp, buf, zbuf, lsem, ssem, psem):
        wid = _sc_worker_id()
        pltpu.sync_copy(pad_hbm.at[pl.ds(wid * n_pad, n_pad)], idxp)
        pltpu.sync_copy(zero_hbm, zbuf)
        pads = [pltpu.make_async_copy(zbuf, out_hbm.at[idxp.at[j]], psem) for j in range(n_pad)]
        for cp in pads:
            cp.start()
        pltpu.sync_copy(pos_hbm.at[pl.ds(wid * n_win, n_win)], idx1)
        pltpu.sync_copy(pos_hbm.at[pl.ds(second + wid * n_win, n_win)], idx2)

        def load(j):
            rows = xp_hbm.at[pl.ds((wid * n_win + j) * SC_WINDOW, SC_WINDOW)]
            return pltpu.make_async_copy(rows, buf.at[j % 2], lsem.at[j % 2])

        def scatters(j):
            return [pltpu.make_async_copy(buf.at[j % 2], out_hbm.at[idx.at[j]], ssem.at[j % 2])
                    for idx in (idx1, idx2)]

        load(0).start()
        for j in range(n_win):
            load(j).wait()
            for cp in scatters(j):
                cp.start()
            if j >= 1:
                for cp in scatters(j - 1):
                    cp.wait()
            if j + 1 < n_win:
                load(j + 1).start()
        for cp in scatters(n_win - 1):
            cp.wait()
        for cp in pads:
            cp.wait()

    return dispatch(xp, as_windows(pos), as_windows(pad_pos), zero_rows)


def _gather_rows(src, idx):
    n_out, width = idx.shape[0], src.shape[1]
    n_win = n_out // SC_WORKERS // GATHER_WINDOW
    nbuf = GATHER_BUFFERS

    @functools.partial(
        pl.kernel, out_type=jax.ShapeDtypeStruct((n_out, width), src.dtype), mesh=_sc_mesh(),
        scratch_types=[pltpu.VMEM((n_win, GATHER_WINDOW), jnp.int32),
                       pltpu.VMEM((nbuf, GATHER_WINDOW, width), src.dtype),
                       pltpu.SemaphoreType.DMA((nbuf,)), pltpu.SemaphoreType.DMA((nbuf,))])
    def gather(src_hbm, idx_hbm, dst_hbm, idx_v, buf, gsem, ssem):
        wid = _sc_worker_id()
        pltpu.sync_copy(idx_hbm.at[pl.ds(wid * n_win, n_win)], idx_v)

        def fetch(j):
            return pltpu.make_async_copy(src_hbm.at[idx_v.at[j]], buf.at[j % nbuf], gsem.at[j % nbuf])

        def store(j):
            rows = dst_hbm.at[pl.ds((wid * n_win + j) * GATHER_WINDOW, GATHER_WINDOW)]
            return pltpu.make_async_copy(buf.at[j % nbuf], rows, ssem.at[j % nbuf])

        for j in range(min(nbuf - 1, n_win)):
            fetch(j).start()
        for j in range(n_win):
            fetch(j).wait()
            store(j).start()
            if j + nbuf - 1 < n_win:
                if j >= 1:
                    store(j - 1).wait()
                fetch(j + nbuf - 1).start()
        for j in range(max(0, n_win - nbuf), n_win):
            store(j).wait()

    return gather(src, idx.reshape(-1, GATHER_WINDOW))


def _tile_copy(hbm, buf, sem, tile, slot, to_hbm):
    rows = hbm.at[pl.ds(pl.multiple_of(tile * ROW_TILE, ROW_TILE), ROW_TILE)]
    if to_hbm:
        return pltpu.make_async_copy(buf.at[slot], rows, sem.at[slot])
    return pltpu.make_async_copy(rows, buf.at[slot], sem.at[slot])


def _weight_copies(w_hbm, wbuf, wsem, expert):
    slot = expert % WEIGHT_BUFFERS
    return [pltpu.make_async_copy(w.at[expert], buf.at[slot], wsem.at[slot]) for w, buf in zip(w_hbm, wbuf)]


def _expert_kernel(ts_ref, nte_ref, x_hbm, wgu_hbm, wd_hbm, y_hbm,
                   xbuf, ybuf, xsem, ysem, wgu_buf, wd_buf, wsem, act_ref, *, max_tiles):
    e = pl.program_id(0)
    first = ts_ref[e]
    count = nte_ref[e]
    n_tiles = ts_ref[N_EXPERTS - 1] + nte_ref[N_EXPERTS - 1]
    ahead = TILE_BUFFERS - 2
    w_hbm = (wgu_hbm, wd_hbm)
    wbuf = (wgu_buf, wd_buf)
    wgu_bf = wgu_buf.at[e % WEIGHT_BUFFERS]
    wd_bf = wd_buf.at[e % WEIGHT_BUFFERS]

    @pl.when(e == 0)
    def _():
        for g in range(ahead):
            @pl.when(g < n_tiles)
            def _():
                _tile_copy(x_hbm, xbuf, xsem, g, g, False).start()
        for k in range(WEIGHT_BUFFERS - 1):
            for cp in _weight_copies(w_hbm, wbuf, wsem, k):
                cp.start()

    @pl.when(e + WEIGHT_BUFFERS - 1 < N_EXPERTS)
    def _():
        for cp in _weight_copies(w_hbm, wbuf, wsem, e + WEIGHT_BUFFERS - 1):
            cp.start()

    for cp in _weight_copies(w_hbm, wbuf, wsem, e):
        cp.wait()

    def sync_x(g):
        _tile_copy(x_hbm, xbuf, xsem, g, g % TILE_BUFFERS, False).wait()

        @pl.when(g + ahead < n_tiles)
        def _():
            _tile_copy(x_hbm, xbuf, xsem, g + ahead, (g + ahead) % TILE_BUFFERS, False).start()

    def sync_y_slot(g):
        @pl.when(g >= TILE_BUFFERS)
        def _():
            _tile_copy(y_hbm, ybuf, ysem, g - TILE_BUFFERS, g % TILE_BUFFERS, True).wait()

    def up_proj(g):
        lo, hi = _unpack_bf16_pairs(xbuf[g % TILE_BUFFERS])
        return (jnp.dot(lo.astype(jnp.bfloat16), wgu_bf[0:HALF, :], preferred_element_type=jnp.float32)
                + jnp.dot(hi.astype(jnp.bfloat16), wgu_bf[HALF:, :], preferred_element_type=jnp.float32))

    def put_act(hgu, which):
        hg = hgu[:, 0:D_EXPERT]
        act_ref[which] = (hg * jax.nn.sigmoid(hg) * hgu[:, D_EXPERT:]).astype(jnp.bfloat16)

    def down_proj(g, which):
        y = jnp.dot(act_ref[which], wd_bf[...], preferred_element_type=jnp.float32)
        ybuf[g % TILE_BUFFERS] = _pack_bf16_pairs(y)

    def start_y(g):
        _tile_copy(y_hbm, ybuf, ysem, g, g % TILE_BUFFERS, True).start()

    @pl.when(count > 0)
    def _():
        sync_x(first)
        put_act(up_proj(first), 0)

        def pair_body(i, c):
            g = first + 1 + 2 * i
            sync_x(g)
            sync_x(g + 1)
            sync_y_slot(g - 1)
            sync_y_slot(g)
            down_proj(g - 1, 0)
            hgu0 = up_proj(g)
            put_act(hgu0, 1)
            hgu1 = up_proj(g + 1)
            down_proj(g, 1)
            put_act(hgu1, 0)
            start_y(g - 1)
            start_y(g)
            return c

        n_pairs = (count - 1) // 2
        lax.fori_loop(0, n_pairs, pair_body, 0)

        @pl.when((count - 1) % 2 == 1)
        def _():
            g = first + count - 1
            sync_x(g)
            sync_y_slot(g - 1)
            down_proj(g - 1, 0)
            put_act(up_proj(g), 0)
            start_y(g - 1)

        last = first + count - 1
        sync_y_slot(last)
        down_proj(last, 0)
        start_y(last)

    @pl.when(e == N_EXPERTS - 1)
    def _():
        for k in range(TILE_BUFFERS, 0, -1):
            @pl.when(n_tiles >= k)
            def _():
                _tile_copy(y_hbm, ybuf, ysem, n_tiles - k, (n_tiles - k) % TILE_BUFFERS, True).wait()

        ybuf[0] = jnp.zeros((ROW_TILE, HALF), jnp.uint32)

        def fill(g, c):
            cp = _tile_copy(y_hbm, ybuf, ysem, g, 0, True)
            cp.start()
            cp.wait()
            return c

        lax.fori_loop(n_tiles, max_tiles, fill, 0)


def _run_experts(tile_start, tiles_per_expert, x_sorted, w_gate_up, w_down):
    n_rows = x_sorted.shape[0]
    max_tiles = n_rows // ROW_TILE
    hbm = pl.BlockSpec(memory_space=pl.ANY)

    grid_spec = pltpu.PrefetchScalarGridSpec(
        num_scalar_prefetch=2,
        grid=(N_EXPERTS,),
        in_specs=[hbm, hbm, hbm],
        out_specs=hbm,
        scratch_shapes=[
            pltpu.VMEM((TILE_BUFFERS, ROW_TILE, HALF), jnp.uint32),
            pltpu.VMEM((TILE_BUFFERS, ROW_TILE, HALF), jnp.uint32),
            pltpu.SemaphoreType.DMA((TILE_BUFFERS,)),
            pltpu.SemaphoreType.DMA((TILE_BUFFERS,)),
            pltpu.VMEM((WEIGHT_BUFFERS, D_MODEL, 2 * D_EXPERT), jnp.bfloat16),
            pltpu.VMEM((WEIGHT_BUFFERS, D_EXPERT, D_MODEL), jnp.bfloat16),
            pltpu.SemaphoreType.DMA((WEIGHT_BUFFERS,)),
            pltpu.VMEM((2, ROW_TILE, D_EXPERT), jnp.bfloat16),
        ],
    )
    return pl.pallas_call(
        functools.partial(_expert_kernel, max_tiles=max_tiles),
        grid_spec=grid_spec,
        out_shape=jax.ShapeDtypeStruct((n_rows, HALF), jnp.uint32),
        compiler_params=pltpu.CompilerParams(
            dimension_semantics=("arbitrary",), vmem_limit_bytes=VMEM_LIMIT),
    )(tile_start, tiles_per_expert, x_sorted, w_gate_up, w_down)


def _combine_rows(r0, h1_ref, p_ref, route_t_ref, ya_ref, yb_ref, w_pg, b_pg, w_ple, g_ref, b_ref, o_ref):
    rows = pl.ds(r0, SUB_TILE)
    h1 = h1_ref[rows, :]
    h_hi = h1.astype(jnp.bfloat16)
    p_b = p_ref[rows, :].astype(jnp.bfloat16)
    rec_t = route_t_ref[:, rows]
    padded = jnp.concatenate([rec_t, jnp.zeros((LANES - ROUTE_ROWS, SUB_TILE), jnp.float32)], axis=0)
    route = jnp.transpose(padded)
    yield
    gate_pre = jnp.dot(h_hi, w_pg[...], preferred_element_type=jnp.float32)
    ple_pre = jnp.dot(p_b, w_ple[...], preferred_element_type=jnp.float32)
    yield
    ple = ple_pre * jax.nn.sigmoid(gate_pre + b_pg[...])
    w1 = route[:, R_W1:R_W1 + 1]
    w2 = route[:, R_W2:R_W2 + 1]
    a_lo, a_hi = _unpack_bf16_pairs(ya_ref[rows, :])
    b_lo, b_hi = _unpack_bf16_pairs(yb_ref[rows, :])
    moe = jnp.concatenate([w1 * a_lo + w2 * b_lo, w1 * a_hi + w2 * b_hi], axis=-1)
    o_ref[rows, :] = _layernorm(DEEPNORM_ALPHA * h1 + ple + moe, g_ref[...], b_ref[...])


def _combine_kernel(h1_ref, p_ref, route_t_ref, ya_ref, yb_ref, w_pg, b_pg, w_ple, g_ref, b_ref, o_ref):
    chains = [_combine_rows(r0, h1_ref, p_ref, route_t_ref, ya_ref, yb_ref, w_pg, b_pg, w_ple, g_ref, b_ref, o_ref)
              for r0 in range(0, h1_ref.shape[0], SUB_TILE)]
    for t in range(COMBINE_PHASES + len(chains) - 1):
        for k, chain in reversed(list(enumerate(chains))):
            if 0 <= t - k < COMBINE_PHASES:
                next(chain, None)


def _run_combine(h1, p2, route_t, y_tok, w_pg, b_pg, w_ple, ln2_g, ln2_b):
    n_tok = h1.shape[0]
    n_t = n_tok // TOKEN_TILE
    tok_map = lambda i: (i, 0)
    const = lambda shape: pl.BlockSpec(shape, lambda i: (0, 0), pipeline_mode=pl.Buffered(1))
    in_specs = [
        pl.BlockSpec((TOKEN_TILE, D_MODEL), tok_map),
        pl.BlockSpec((TOKEN_TILE, PLE_DIM), tok_map),
        pl.BlockSpec((ROUTE_ROWS, TOKEN_TILE), lambda i: (0, i)),
        pl.BlockSpec((TOKEN_TILE, HALF), lambda i: (i, 0)),
        pl.BlockSpec((TOKEN_TILE, HALF), lambda i: (i + n_t, 0)),
        const((D_MODEL, D_MODEL)), const((1, D_MODEL)), const((PLE_DIM, D_MODEL)),
        const((1, D_MODEL)), const((1, D_MODEL)),
    ]
    return pl.pallas_call(
        _combine_kernel,
        grid=(n_t,),
        in_specs=in_specs,
        out_specs=pl.BlockSpec((TOKEN_TILE, D_MODEL), tok_map),
        out_shape=jax.ShapeDtypeStruct((n_tok, D_MODEL), jnp.float32),
        compiler_params=pltpu.CompilerParams(dimension_semantics=("arbitrary",), vmem_limit_bytes=VMEM_LIMIT),
    )(h1, p2, route_t, y_tok, y_tok, w_pg, b_pg, w_ple, ln2_g, ln2_b)


def _split_bf16(w):
    hi = w.astype(jnp.bfloat16)
    lo = (w - hi.astype(jnp.float32)).astype(jnp.bfloat16)
    return hi, lo


def kernel(x, p, ln_in_g, ln_in_b, w_in, conv_w, conv_b, pool_w, pool_scale, w_out, ln1_g, ln1_b,
           w_rg, b_rg, w_re, b_re, w_gate, w_up, w_down, w_pg, b_pg, w_ple, ln2_g, ln2_b):
    batch, seq, _ = x.shape
    n_tok = batch * seq
    bf = jnp.bfloat16
    row = lambda v: v.reshape(1, -1)

    w_r = jnp.concatenate([w_rg[0], jnp.transpose(w_re[0], (1, 0, 2)).reshape(D_MODEL, N_EXPERTS)], axis=1)
    w_r = jnp.pad(w_r, ((0, 0), (0, LANES - w_r.shape[1])))
    w_r_hi, w_r_lo = _split_bf16(w_r)
    w_r_cat = jnp.concatenate([w_r_hi, w_r_hi, w_r_lo], axis=0)
    b_r = jnp.pad(jnp.concatenate([b_rg[0], b_re[0].reshape(-1)]), (0, LANES - N_GROUPS - N_EXPERTS)).reshape(1, LANES)

    x2 = x.reshape(n_tok, D_MODEL)
    p2 = p[0].reshape(n_tok, PLE_DIM)
    mixer_weights = (row(ln_in_g), row(ln_in_b), w_in[0], conv_w, row(conv_b[0]),
                     pool_w[0], row(pool_scale[0]), w_out[0], row(ln1_g[0]), row(ln1_b[0]),
                     w_r_cat, b_r)
    combine_weights = (w_pg[0].astype(bf), row(b_pg[0]), w_ple[0].astype(bf), row(ln2_g[0]), row(ln2_b[0]))
    expert_weights = (jnp.concatenate([w_gate[0].reshape(N_EXPERTS, D_MODEL, D_EXPERT).astype(bf),
                                       w_up[0].reshape(N_EXPERTS, D_MODEL, D_EXPERT).astype(bf)], axis=-1),
                      w_down[0].reshape(N_EXPERTS, D_EXPERT, D_MODEL).astype(bf))
    zero_rows = jnp.zeros((SC_WINDOW, HALF), jnp.uint32)

    n_rows = -(-(2 * n_tok + N_EXPERTS * (ROW_TILE - 1)) // ROW_TILE) * ROW_TILE
    xp, h1, route_t, counts = _run_mixer(x2, *mixer_weights, batch, seq)
    pos, tile_start, tiles_per_expert, pad_pos = _run_plan(route_t, counts, n_rows)
    x_sorted = _dispatch_rows(xp, pos, pad_pos, zero_rows, n_rows)
    y_sorted = _run_experts(tile_start[0, :N_EXPERTS], tiles_per_expert[0, :N_EXPERTS], x_sorted, *expert_weights)

    y_tok = _gather_rows(y_sorted, pos.reshape(-1))
    out = _run_combine(h1, p2, route_t, y_tok, *combine_weights)
    return out.reshape(batch, seq, D_MODEL)
```

```python
import functools

import jax
import jax.numpy as jnp
from jax import lax
from jax.experimental import pallas as pl
from jax.experimental.pallas import tpu as pltpu
from jax.experimental.pallas import tpu_sc as plsc

D_MODEL = 1024
CONV_WIDTH = 512
CONV_K = 3
POOL_WIDTH = 512
POOL_WINDOWS = (2, 4, 8, 16)
POOL_GW = 128
IN_PROJ = 3 * CONV_WIDTH + POOL_WIDTH
N_GROUPS = 4
EXPERTS_PER_GROUP = 8
N_EXPERTS = N_GROUPS * EXPERTS_PER_GROUP
D_EXPERT = 256
PLE_DIM = 256
LN_EPS = 1e-5
DEEPNORM_ALPHA = 2.0 ** 0.25

LANES = 128
HALF = D_MODEL // 2
CONV_HALO = 8
POOL_HALO = 16
SEQ_TILE = 1024
SUB_TILE = 256
MIXER_PHASES = 8
COMBINE_PHASES = 3
ROW_TILE = 256
TILE_BUFFERS = 8
SC_WORKERS = 32
SC_WINDOW = 64
GATHER_WINDOW = 64
GATHER_BUFFERS = 3
WEIGHT_BUFFERS = 3
TOKEN_TILE = 1024
COMBINE_BUFFERS = 3
VMEM_LIMIT = 56 * 1024 * 1024

R_ID1, R_ID2, R_RANK1, R_RANK2, R_W1, R_W2 = range(6)
ROUTE_ROWS = 8
LOGIT_ROWS = 40


def _layernorm(x, g, b):
    mu = jnp.mean(x, axis=-1, keepdims=True)
    xc = x - mu
    var = jnp.mean(xc * xc, axis=-1, keepdims=True)
    return xc * lax.rsqrt(var + LN_EPS) * g + b


def _pack_bf16_pairs(v):
    bits = lax.bitcast_convert_type(v.astype(jnp.bfloat16).astype(jnp.float32), jnp.uint32)
    return bits[:, HALF:] | (bits[:, :HALF] >> 16)


def _unpack_bf16_pairs(w):
    lo = lax.bitcast_convert_type(w << 16, jnp.float32)
    hi = lax.bitcast_convert_type(w & jnp.uint32(0xFFFF0000), jnp.float32)
    return lo, hi


def _mixer_kernel(x_ref, lnin_g, lnin_b, w_in_f32, conv_w, conv_b, pool_w_f32, pool_scale, w_out_f32,
                  ln1_g, ln1_b, w_r, b_r,
                  xp_ref, h1_ref, route_t_ref, counts_ref,
                  zbuf, vbuf, carry, w_in, pool_w, w_out):
    b = pl.program_id(0)
    s = pl.program_id(1)
    ts = x_ref.shape[0]

    @pl.when(s == 0)
    def _():
        zbuf[0:CONV_HALO, :] = jnp.zeros((CONV_HALO, CONV_WIDTH), jnp.float32)
        vbuf[0:POOL_HALO, :] = jnp.zeros((POOL_HALO, POOL_WIDTH), jnp.float32)

    @pl.when((b == 0) & (s == 0))
    def _():
        carry[...] = jnp.zeros_like(carry)
        w_in[...] = w_in_f32[...].astype(jnp.bfloat16)
        pool_w[...] = pool_w_f32[...].astype(jnp.bfloat16)
        w_out[...] = w_out_f32[...].astype(jnp.bfloat16)

    chains = [_mixer_rows(r0, s * ts + r0, x_ref, lnin_g, lnin_b, w_in, conv_w, conv_b, pool_w, pool_scale,
                          w_out, ln1_g, ln1_b, w_r, b_r,
                          xp_ref, h1_ref, route_t_ref, zbuf, vbuf, carry)
              for r0 in range(0, ts, SUB_TILE)]
    for t in range(MIXER_PHASES + len(chains) - 1):
        for k, chain in reversed(list(enumerate(chains))):
            if 0 <= t - k < MIXER_PHASES:
                next(chain, None)
    zbuf[0:CONV_HALO, :] = zbuf[ts:ts + CONV_HALO, :]
    vbuf[0:POOL_HALO, :] = vbuf[ts:ts + POOL_HALO, :]
    counts_ref[...] = jnp.transpose(carry[...])[0:1, :]


def _mixer_rows(r0, seq0, x_ref, lnin_g, lnin_b, w_in, conv_w, conv_b, pool_w, pool_scale, w_out,
                ln1_g, ln1_b, w_r, b_r,
                xp_ref, h1_ref, route_t_ref, zbuf, vbuf, carry):
    n = SUB_TILE
    rows = pl.ds(r0, n)
    h0 = _layernorm(x_ref[rows, :], lnin_g[...], lnin_b[...])
    h0b = h0.astype(jnp.bfloat16)
    yield
    u_a = jnp.dot(h0b, w_in[:, 0:IN_PROJ // 2], preferred_element_type=jnp.float32)
    yield
    u_b = jnp.dot(h0b, w_in[:, IN_PROJ // 2:], preferred_element_type=jnp.float32)
    yield
    b_g = u_a[:, 0:CONV_WIDTH]
    c_g = u_a[:, CONV_WIDTH:]
    v_c = u_b[:, 0:CONV_WIDTH]
    v_p = u_b[:, CONV_WIDTH:]

    zbuf[pl.ds(CONV_HALO + r0, n), :] = c_g * v_c
    zext = zbuf[pl.ds(r0, n + CONV_HALO), :]
    z1 = pltpu.roll(zext, 1, axis=0)[CONV_HALO:, :]
    z2 = pltpu.roll(zext, 2, axis=0)[CONV_HALO:, :]
    conv = z2 * conv_w[0, 0:1, :] + z1 * conv_w[0, 1:2, :] + zext[CONV_HALO:, :] * conv_w[0, 2:3, :] + conv_b[...]
    y_conv = b_g * conv

    vbuf[pl.ds(POOL_HALO + r0, n), :] = v_p
    vext = vbuf[pl.ds(r0, n + POOL_HALO), :]
    s2 = vext + pltpu.roll(vext, 1, axis=0)
    s4 = s2[:, POOL_GW:] + pltpu.roll(s2[:, POOL_GW:], 2, axis=0)
    s8 = s4[:, POOL_GW:] + pltpu.roll(s4[:, POOL_GW:], 4, axis=0)
    s16 = s8[:, POOL_GW:] + pltpu.roll(s8[:, POOL_GW:], 8, axis=0)
    wsums = (s2[POOL_HALO:, 0:POOL_GW], s4[POOL_HALO:, 0:POOL_GW],
             s8[POOL_HALO:, 0:POOL_GW], s16[POOL_HALO:, 0:POOL_GW])

    t_pos = seq0 + lax.broadcasted_iota(jnp.int32, (n, 1), 0)
    y_pool = []
    for j, w in enumerate(POOL_WINDOWS):
        inv_cnt = 1.0 / jnp.minimum(t_pos + 1, w).astype(jnp.float32)
        pooled = wsums[j] * inv_cnt - v_p[:, j * POOL_GW:(j + 1) * POOL_GW]
        y_pool.append(jnp.dot(pooled.astype(jnp.bfloat16), pool_w[j], preferred_element_type=jnp.float32))
    y_pool = jnp.concatenate(y_pool, axis=-1) * pool_scale[...]

    ycat = jnp.concatenate([y_conv, y_pool], axis=-1).astype(jnp.bfloat16)
    yield
    mix = jnp.dot(ycat, w_out[...], preferred_element_type=jnp.float32)
    yield
    h1 = _layernorm(DEEPNORM_ALPHA * h0 + mix, ln1_g[...], ln1_b[...])

    h_hi = h1.astype(jnp.bfloat16)
    h_lo = (h1 - h_hi.astype(jnp.float32)).astype(jnp.bfloat16)
    xp_ref[rows, :] = _pack_bf16_pairs(h1)
    h1_ref[rows, :] = h1
    hcat = jnp.concatenate([h_hi, h_lo, h_hi], axis=-1)
    yield
    logits = jnp.dot(hcat, w_r[...], preferred_element_type=jnp.float32) + b_r[...]
    yield
    lt = jnp.transpose(logits)[0:LOGIT_ROWS, :]
    rid = lax.broadcasted_iota(jnp.int32, (LOGIT_ROWS, n), 0).astype(jnp.float32)
    neg = jnp.float32(-jnp.inf)

    def first_argmax(vals):
        m = jnp.max(vals, axis=0, keepdims=True)
        idx = jnp.min(jnp.where(vals == m, rid, float(LOGIT_ROWS)), axis=0, keepdims=True)
        return m, idx

    g_mask = rid < N_GROUPS
    g_max, g_idx = first_argmax(jnp.where(g_mask, lt, neg))
    g_w = 1.0 / jnp.sum(jnp.where(g_mask, jnp.exp(lt - g_max), 0.0), axis=0, keepdims=True)

    e_lo = N_GROUPS + EXPERTS_PER_GROUP * g_idx
    e_vals = jnp.where((rid >= e_lo) & (rid < e_lo + EXPERTS_PER_GROUP), lt, neg)
    m1, i1 = first_argmax(e_vals)
    m2, i2 = first_argmax(jnp.where(rid == i1, neg, e_vals))
    e21 = jnp.exp(m2 - m1)
    w1 = g_w / (1.0 + e21)
    w2 = g_w * e21 / (1.0 + e21)
    id1 = i1 - N_GROUPS
    id2 = i2 - N_GROUPS

    eid = lax.broadcasted_iota(jnp.int32, (N_EXPERTS, n), 0).astype(jnp.float32)
    sel1 = eid == id1
    sel2 = eid == id2
    onehot = (sel1 | sel2).astype(jnp.float32)
    src = lax.broadcasted_iota(jnp.int32, (n, n), 0)
    dst = lax.broadcasted_iota(jnp.int32, (n, n), 1)
    earlier = (src < dst).astype(jnp.bfloat16)
    before = (jnp.dot(onehot.astype(jnp.bfloat16), earlier, preferred_element_type=jnp.float32)
              + carry[0:N_EXPERTS, 0:1])
    rank1 = jnp.sum(jnp.where(sel1, before, 0.0), axis=0, keepdims=True)
    rank2 = jnp.sum(jnp.where(sel2, before, 0.0), axis=0, keepdims=True)
    carry[0:N_EXPERTS, :] = carry[0:N_EXPERTS, :] + jnp.sum(onehot, axis=1, keepdims=True)

    rec_t = jnp.zeros((ROUTE_ROWS, n), jnp.float32)
    rec_row = lax.broadcasted_iota(jnp.int32, (ROUTE_ROWS, n), 0)
    for k, val in ((R_ID1, id1), (R_ID2, id2), (R_W1, w1), (R_W2, w2), (R_RANK1, rank1), (R_RANK2, rank2)):
        rec_t = jnp.where(rec_row == k, val, rec_t)
    route_t_ref[:, rows] = rec_t


def _run_mixer(x2, lnin_g, lnin_b, w_in, conv_w, conv_b, pool_w, pool_scale, w_out, ln1_g, ln1_b,
               w_r, b_r, batch, seq):
    n_tok = batch * seq
    n_s = seq // SEQ_TILE
    tok_map = lambda b, s: (b * n_s + s, 0)

    def const(shape):
        return pl.BlockSpec(shape, lambda b, s: (0,) * len(shape), pipeline_mode=pl.Buffered(1))

    in_specs = [
        pl.BlockSpec((SEQ_TILE, D_MODEL), tok_map),
        const((1, D_MODEL)), const((1, D_MODEL)),
        const((D_MODEL, IN_PROJ)),
        const((1, CONV_K, CONV_WIDTH)), const((1, CONV_WIDTH)),
        const((len(POOL_WINDOWS), POOL_GW, POOL_GW)), const((1, POOL_WIDTH)),
        const((D_MODEL, D_MODEL)),
        const((1, D_MODEL)), const((1, D_MODEL)),
        const((3 * D_MODEL, LANES)), const((1, LANES)),
    ]
    out_specs = [
        pl.BlockSpec((SEQ_TILE, HALF), tok_map),
        pl.BlockSpec((SEQ_TILE, D_MODEL), tok_map),
        pl.BlockSpec((ROUTE_ROWS, SEQ_TILE), lambda b, s: (0, b * n_s + s)),
        pl.BlockSpec((1, LANES), lambda b, s: (0, 0)),
    ]
    out_shape = [
        jax.ShapeDtypeStruct((n_tok, HALF), jnp.uint32),
        jax.ShapeDtypeStruct((n_tok, D_MODEL), jnp.float32),
        jax.ShapeDtypeStruct((ROUTE_ROWS, n_tok), jnp.float32),
        jax.ShapeDtypeStruct((1, LANES), jnp.float32),
    ]
    return pl.pallas_call(
        _mixer_kernel,
        grid=(batch, n_s),
        in_specs=in_specs,
        out_specs=out_specs,
        out_shape=out_shape,
        scratch_shapes=[
            pltpu.VMEM((SEQ_TILE + CONV_HALO, CONV_WIDTH), jnp.float32),
            pltpu.VMEM((SEQ_TILE + POOL_HALO, POOL_WIDTH), jnp.float32),
            pltpu.VMEM((LANES, LANES), jnp.float32),
            pltpu.VMEM((D_MODEL, IN_PROJ), jnp.bfloat16),
            pltpu.VMEM((len(POOL_WINDOWS), POOL_GW, POOL_GW), jnp.bfloat16),
            pltpu.VMEM((D_MODEL, D_MODEL), jnp.bfloat16),
        ],
        compiler_params=pltpu.CompilerParams(
            dimension_semantics=("arbitrary", "arbitrary"), vmem_limit_bytes=VMEM_LIMIT),
    )(x2, lnin_g, lnin_b, w_in, conv_w, conv_b, pool_w, pool_scale, w_out, ln1_g, ln1_b, w_r, b_r)


def _plan_kernel(rt_ref, counts_ref, pos_ref, tile_start_ref, tiles_ref, pad_ref, *, n_rows):
    lane = lax.broadcasted_iota(jnp.int32, (ROUTE_ROWS, LANES), 1)
    counts = jnp.broadcast_to(counts_ref[...], (ROUTE_ROWS, LANES))
    tiles = jnp.floor((counts + (ROW_TILE - 1)) * (1.0 / ROW_TILE))
    tile_end = tiles
    shift = 1
    while shift < N_EXPERTS:
        tile_end = tile_end + jnp.where(lane >= shift, pltpu.roll(tile_end, shift, axis=1), 0.0)
        shift *= 2
    row_start = (tile_end - tiles) * ROW_TILE
    tile_start_ref[...] = (tile_end - tiles)[0:1, :].astype(jnp.int32)
    tiles_ref[...] = tiles[0:1, :].astype(jnp.int32)

    rt = rt_ref[...]
    ids = rt[R_ID1:R_ID2 + 1, :]
    start = jnp.zeros_like(ids)
    for e in range(N_EXPERTS):
        start = jnp.where(ids == e, row_start[0:1, e:e + 1], start)
    pos_ref[...] = (start + rt[R_RANK1:R_RANK2 + 1, :]).astype(jnp.int32)

    sub = lax.broadcasted_iota(jnp.int32, (N_EXPERTS, LANES), 0)
    lane_e = lax.broadcasted_iota(jnp.int32, (N_EXPERTS, LANES), 1)
    diag = sub == lane_e
    pad_lo = jnp.sum(jnp.where(diag, (row_start + counts)[0:1, :], 0.0), axis=1, keepdims=True)
    pad_n = jnp.sum(jnp.where(diag, (tiles * ROW_TILE - counts)[0:1, :], 0.0), axis=1, keepdims=True)
    j = lax.broadcasted_iota(jnp.int32, (N_EXPERTS, ROW_TILE), 1).astype(jnp.float32)
    pad_ref[...] = jnp.where(j < pad_n, pad_lo + j, n_rows + j).astype(jnp.int32)


def _run_plan(route_t, counts, n_rows):
    n_tok = route_t.shape[1]
    full = lambda shape: pl.BlockSpec(shape, lambda i: (0, 0))
    return pl.pallas_call(
        functools.partial(_plan_kernel, n_rows=n_rows),
        grid=(1,),
        in_specs=[full((ROUTE_ROWS, n_tok)), full((1, LANES))],
        out_specs=[full((2, n_tok)), full((1, LANES)), full((1, LANES)), full((N_EXPERTS, ROW_TILE))],
        out_shape=[
            jax.ShapeDtypeStruct((2, n_tok), jnp.int32),
            jax.ShapeDtypeStruct((1, LANES), jnp.int32),
            jax.ShapeDtypeStruct((1, LANES), jnp.int32),
            jax.ShapeDtypeStruct((N_EXPERTS, ROW_TILE), jnp.int32),
        ],
        compiler_params=pltpu.CompilerParams(dimension_semantics=("arbitrary",)),
    )(route_t, counts)


def _sc_mesh():
    return plsc.VectorSubcoreMesh(core_axis_name="core", subcore_axis_name="subcore")


def _sc_worker_id():
    return lax.axis_index("core") * (SC_WORKERS // 2) + lax.axis_index("subcore")


def _dispatch_rows(xp, pos, pad_pos, zero_rows, n_rows):
    n_tok, width = xp.shape
    second = n_tok // SC_WINDOW
    n_win = n_tok // SC_WORKERS // SC_WINDOW
    n_pad = pad_pos.size // SC_WORKERS // SC_WINDOW
    as_windows = lambda v: v.reshape(-1, SC_WINDOW)

    @functools.partial(
        pl.kernel, out_type=jax.ShapeDtypeStruct((n_rows + ROW_TILE, width), xp.dtype), mesh=_sc_mesh(),
        scratch_types=[pltpu.VMEM((n_win, SC_WINDOW), jnp.int32), pltpu.VMEM((n_win, SC_WINDOW), jnp.int32),
                       pltpu.VMEM((n_pad, SC_WINDOW), jnp.int32),
                       pltpu.VMEM((2, SC_WINDOW, width), xp.dtype), pltpu.VMEM((SC_WINDOW, width), xp.dtype),
                       pltpu.SemaphoreType.DMA((2,)), pltpu.SemaphoreType.DMA((2,)), pltpu.SemaphoreType.DMA])
    def dispatch(xp_hbm, pos_hbm, pad_hbm, zero_hbm, out_hbm,
                 idx1, idx2, idxp, buf, zbuf, lsem, ssem, psem):
        wid = _sc_worker_id()
        pltpu.sync_copy(pad_hbm.at[pl.ds(wid * n_pad, n_pad)], idxp)
        pltpu.sync_copy(zero_hbm, zbuf)
        pads = [pltpu.make_async_copy(zbuf, out_hbm.at[idxp.at[j]], psem) for j in range(n_pad)]
        for cp in pads:
            cp.start()
        pltpu.sync_copy(pos_hbm.at[pl.ds(wid * n_win, n_win)], idx1)
        pltpu.sync_copy(pos_hbm.at[pl.ds(second + wid * n_win, n_win)], idx2)

        def load(j):
            rows = xp_hbm.at[pl.ds((wid * n_win + j) * SC_WINDOW, SC_WINDOW)]
            return pltpu.make_async_copy(rows, buf.at[j % 2], lsem.at[j % 2])

        def scatters(j):
            return [pltpu.make_async_copy(buf.at[j % 2], out_hbm.at[idx.at[j]], ssem.at[j % 2])
                    for idx in (idx1, idx2)]

        load(0).start()
        for j in range(n_win):
            load(j).wait()
            for cp in scatters(j):
                cp.start()
            if j >= 1:
                for cp in scatters(j - 1):
                    cp.wait()
            if j + 1 < n_win:
                load(j + 1).start()
        for cp in scatters(n_win - 1):
            cp.wait()
        for cp in pads:
            cp.wait()

    return dispatch(xp, as_windows(pos), as_windows(pad_pos), zero_rows)


def _gather_rows(src, idx):
    n_out, width = idx.shape[0], src.shape[1]
    n_win = n_out // SC_WORKERS // GATHER_WINDOW
    nbuf = GATHER_BUFFERS

    @functools.partial(
        pl.kernel, out_type=jax.ShapeDtypeStruct((n_out, width), src.dtype), mesh=_sc_mesh(),
        scratch_types=[pltpu.VMEM((n_win, GATHER_WINDOW), jnp.int32),
                       pltpu.VMEM((nbuf, GATHER_WINDOW, width), src.dtype),
                       pltpu.SemaphoreType.DMA((nbuf,)), pltpu.SemaphoreType.DMA((nbuf,))])
    def gather(src_hbm, idx_hbm, dst_hbm, idx_v, buf, gsem, ssem):
        wid = _sc_worker_id()
        pltpu.sync_copy(idx_hbm.at[pl.ds(wid * n_win, n_win)], idx_v)

        def fetch(j):
            return pltpu.make_async_copy(src_hbm.at[idx_v.at[j]], buf.at[j % nbuf], gsem.at[j % nbuf])

        def store(j):
            rows = dst_hbm.at[pl.ds((wid * n_win + j) * GATHER_WINDOW, GATHER_WINDOW)]
            return pltpu.make_async_copy(buf.at[j % nbuf], rows, ssem.at[j % nbuf])

        for j in range(min(nbuf - 1, n_win)):
            fetch(j).start()
        for j in range(n_win):
            fetch(j).wait()
            store(j).start()
            if j + nbuf - 1 < n_win:
                if j >= 1:
                    store(j - 1).wait()
                fetch(j + nbuf - 1).start()
        for j in range(max(0, n_win - nbuf), n_win):
            store(j).wait()

    return gather(src, idx.reshape(-1, GATHER_WINDOW))


def _tile_copy(hbm, buf, sem, tile, slot, to_hbm):
    rows = hbm.at[pl.ds(pl.multiple_of(tile * ROW_TILE, ROW_TILE), ROW_TILE)]
    if to_hbm:
        return pltpu.make_async_copy(buf.at[slot], rows, sem.at[slot])
    return pltpu.make_async_copy(rows, buf.at[slot], sem.at[slot])


def _weight_copies(w_hbm, wbuf, wsem, expert):
    slot = expert % WEIGHT_BUFFERS
    return [pltpu.make_async_copy(w.at[expert], buf.at[slot], wsem.at[slot]) for w, buf in zip(w_hbm, wbuf)]


def _expert_kernel(ts_ref, nte_ref, x_hbm, wg_hbm, wu_hbm, wd_hbm, y_hbm,
                   xbuf, ybuf, xsem, ysem, wg_buf, wu_buf, wd_buf, wsem, wgu_bf, wd_bf, act_ref, *, max_tiles):
    e = pl.program_id(0)
    first = ts_ref[e]
    count = nte_ref[e]
    n_tiles = ts_ref[N_EXPERTS - 1] + nte_ref[N_EXPERTS - 1]
    ahead = TILE_BUFFERS - 2
    w_hbm = (wg_hbm, wu_hbm, wd_hbm)
    wbuf = (wg_buf, wu_buf, wd_buf)

    @pl.when(e == 0)
    def _():
        for g in range(ahead):
            @pl.when(g < n_tiles)
            def _():
                _tile_copy(x_hbm, xbuf, xsem, g, g, False).start()
        for k in range(WEIGHT_BUFFERS - 1):
            for cp in _weight_copies(w_hbm, wbuf, wsem, k):
                cp.start()

    @pl.when(e + WEIGHT_BUFFERS - 1 < N_EXPERTS)
    def _():
        for cp in _weight_copies(w_hbm, wbuf, wsem, e + WEIGHT_BUFFERS - 1):
            cp.start()

    for cp in _weight_copies(w_hbm, wbuf, wsem, e):
        cp.wait()

    def sync_x(g):
        _tile_copy(x_hbm, xbuf, xsem, g, g % TILE_BUFFERS, False).wait()

        @pl.when(g + ahead < n_tiles)
        def _():
            _tile_copy(x_hbm, xbuf, xsem, g + ahead, (g + ahead) % TILE_BUFFERS, False).start()

    def sync_y_slot(g):
        @pl.when(g >= TILE_BUFFERS)
        def _():
            _tile_copy(y_hbm, ybuf, ysem, g - TILE_BUFFERS, g % TILE_BUFFERS, True).wait()

    def up_proj(g):
        lo, hi = _unpack_bf16_pairs(xbuf[g % TILE_BUFFERS])
        return (jnp.dot(lo.astype(jnp.bfloat16), wgu_bf[0:HALF, :], preferred_element_type=jnp.float32)
                + jnp.dot(hi.astype(jnp.bfloat16), wgu_bf[HALF:, :], preferred_element_type=jnp.float32))

    def put_act(hgu, which):
        hg = hgu[:, 0:D_EXPERT]
        act_ref[which] = (hg * jax.nn.sigmoid(hg) * hgu[:, D_EXPERT:]).astype(jnp.bfloat16)

    def down_proj(g, which):
        y = jnp.dot(act_ref[which], wd_bf[...], preferred_element_type=jnp.float32)
        ybuf[g % TILE_BUFFERS] = _pack_bf16_pairs(y)

    def start_y(g):
        _tile_copy(y_hbm, ybuf, ysem, g, g % TILE_BUFFERS, True).start()

    @pl.when(count > 0)
    def _():
        slot = e % WEIGHT_BUFFERS
        wgu_bf[:, 0:D_EXPERT] = wg_buf[slot].astype(jnp.bfloat16)
        wgu_bf[:, D_EXPERT:] = wu_buf[slot].astype(jnp.bfloat16)
        wd_bf[...] = wd_buf[slot].astype(jnp.bfloat16)

        sync_x(first)
        put_act(up_proj(first), 0)

        def pair_body(i, c):
            g = first + 1 + 2 * i
            sync_x(g)
            sync_x(g + 1)
            sync_y_slot(g - 1)
            sync_y_slot(g)
            down_proj(g - 1, 0)
            hgu0 = up_proj(g)
            put_act(hgu0, 1)
            hgu1 = up_proj(g + 1)
            down_proj(g, 1)
            put_act(hgu1, 0)
            start_y(g - 1)
            start_y(g)
            return c

        n_pairs = (count - 1) // 2
        lax.fori_loop(0, n_pairs, pair_body, 0)

        @pl.when((count - 1) % 2 == 1)
        def _():
            g = first + count - 1
            sync_x(g)
            sync_y_slot(g - 1)
            down_proj(g - 1, 0)
            put_act(up_proj(g), 0)
            start_y(g - 1)

        last = first + count - 1
        sync_y_slot(last)
        down_proj(last, 0)
        start_y(last)

    @pl.when(e == N_EXPERTS - 1)
    def _():
        for k in range(TILE_BUFFERS, 0, -1):
            @pl.when(n_tiles >= k)
            def _():
                _tile_copy(y_hbm, ybuf, ysem, n_tiles - k, (n_tiles - k) % TILE_BUFFERS, True).wait()

        ybuf[0] = jnp.zeros((ROW_TILE, HALF), jnp.uint32)

        def fill(g, c):
            cp = _tile_copy(y_hbm, ybuf, ysem, g, 0, True)
            cp.start()
            cp.wait()
            return c

        lax.fori_loop(n_tiles, max_tiles, fill, 0)


def _run_experts(tile_start, tiles_per_expert, x_sorted, w_gate, w_up, w_down):
    n_rows = x_sorted.shape[0]
    max_tiles = n_rows // ROW_TILE
    hbm = pl.BlockSpec(memory_space=pl.ANY)

    grid_spec = pltpu.PrefetchScalarGridSpec(
        num_scalar_prefetch=2,
        grid=(N_EXPERTS,),
        in_specs=[hbm, hbm, hbm, hbm],
        out_specs=hbm,
        scratch_shapes=[
            pltpu.VMEM((TILE_BUFFERS, ROW_TILE, HALF), jnp.uint32),
            pltpu.VMEM((TILE_BUFFERS, ROW_TILE, HALF), jnp.uint32),
            pltpu.SemaphoreType.DMA((TILE_BUFFERS,)),
            pltpu.SemaphoreType.DMA((TILE_BUFFERS,)),
            pltpu.VMEM((WEIGHT_BUFFERS, D_MODEL, D_EXPERT), jnp.float32),
            pltpu.VMEM((WEIGHT_BUFFERS, D_MODEL, D_EXPERT), jnp.float32),
            pltpu.VMEM((WEIGHT_BUFFERS, D_EXPERT, D_MODEL), jnp.float32),
            pltpu.SemaphoreType.DMA((WEIGHT_BUFFERS,)),
            pltpu.VMEM((D_MODEL, 2 * D_EXPERT), jnp.bfloat16),
            pltpu.VMEM((D_EXPERT, D_MODEL), jnp.bfloat16),
            pltpu.VMEM((2, ROW_TILE, D_EXPERT), jnp.bfloat16),
        ],
    )
    return pl.pallas_call(
        functools.partial(_expert_kernel, max_tiles=max_tiles),
        grid_spec=grid_spec,
        out_shape=jax.ShapeDtypeStruct((n_rows, HALF), jnp.uint32),
        compiler_params=pltpu.CompilerParams(
            dimension_semantics=("arbitrary",), vmem_limit_bytes=VMEM_LIMIT),
    )(tile_start, tiles_per_expert, x_sorted, w_gate, w_up, w_down)


def _combine_rows(r0, h1_ref, p_ref, route_t_ref, ya_ref, yb_ref, w_pg, b_pg, w_ple, g_ref, b_ref, o_ref):
    rows = pl.ds(r0, SUB_TILE)
    h1 = h1_ref[rows, :]
    h_hi = h1.astype(jnp.bfloat16)
    p_b = p_ref[rows, :].astype(jnp.bfloat16)
    rec_t = route_t_ref[:, rows]
    padded = jnp.concatenate([rec_t, jnp.zeros((LANES - ROUTE_ROWS, SUB_TILE), jnp.float32)], axis=0)
    route = jnp.transpose(padded)
    yield
    gate_pre = jnp.dot(h_hi, w_pg[...], preferred_element_type=jnp.float32)
    ple_pre = jnp.dot(p_b, w_ple[...], preferred_element_type=jnp.float32)
    yield
    ple = ple_pre * jax.nn.sigmoid(gate_pre + b_pg[...])
    w1 = route[:, R_W1:R_W1 + 1]
    w2 = route[:, R_W2:R_W2 + 1]
    a_lo, a_hi = _unpack_bf16_pairs(ya_ref[rows, :])
    b_lo, b_hi = _unpack_bf16_pairs(yb_ref[rows, :])
    moe = jnp.concatenate([w1 * a_lo + w2 * b_lo, w1 * a_hi + w2 * b_hi], axis=-1)
    o_ref[rows, :] = _layernorm(DEEPNORM_ALPHA * h1 + ple + moe, g_ref[...], b_ref[...])


def _combine_copies(step, n_steps, h1_hbm, p_hbm, y_hbm, h1buf, pbuf, yabuf, ybbuf, sem):
    slot = step % COMBINE_BUFFERS
    rows = pl.ds(pl.multiple_of(step * TOKEN_TILE, TOKEN_TILE), TOKEN_TILE)
    rows_b = pl.ds(pl.multiple_of((step + n_steps) * TOKEN_TILE, TOKEN_TILE), TOKEN_TILE)
    pairs = ((h1_hbm.at[rows], h1buf), (p_hbm.at[rows], pbuf), (y_hbm.at[rows], yabuf), (y_hbm.at[rows_b], ybbuf))
    return [pltpu.make_async_copy(src, buf.at[slot], sem.at[slot]) for src, buf in pairs]


def _combine_kernel(h1_hbm, p_hbm, route_t_ref, y_hbm, w_pg, b_pg, w_ple, g_ref, b_ref, o_ref,
                    h1buf, pbuf, yabuf, ybbuf, sem, *, n_steps):
    s = pl.program_id(0)
    ring = (h1_hbm, p_hbm, y_hbm, h1buf, pbuf, yabuf, ybbuf, sem)
    ahead = COMBINE_BUFFERS - 1

    @pl.when(s == 0)
    def _():
        for k in range(min(ahead, n_steps)):
            for cp in _combine_copies(k, n_steps, *ring):
                cp.start()

    @pl.when(s + ahead < n_steps)
    def _():
        for cp in _combine_copies(s + ahead, n_steps, *ring):
            cp.start()

    for cp in _combine_copies(s, n_steps, *ring):
        cp.wait()
    slot = s % COMBINE_BUFFERS
    h1_ref, p_ref, ya_ref, yb_ref = h1buf.at[slot], pbuf.at[slot], yabuf.at[slot], ybbuf.at[slot]

    chains = [_combine_rows(r0, h1_ref, p_ref, route_t_ref, ya_ref, yb_ref, w_pg, b_pg, w_ple, g_ref, b_ref, o_ref)
              for r0 in range(0, TOKEN_TILE, SUB_TILE)]
    for t in range(COMBINE_PHASES + len(chains) - 1):
        for k, chain in reversed(list(enumerate(chains))):
            if 0 <= t - k < COMBINE_PHASES:
                next(chain, None)


def _run_combine(h1, p2, route_t, y_tok, w_pg, b_pg, w_ple, ln2_g, ln2_b):
    n_tok = h1.shape[0]
    n_t = n_tok // TOKEN_TILE
    tok_map = lambda i: (i, 0)
    const = lambda shape: pl.BlockSpec(shape, lambda i: (0, 0), pipeline_mode=pl.Buffered(1))
    hbm = pl.BlockSpec(memory_space=pl.ANY)
    in_specs = [
        hbm, hbm,
        pl.BlockSpec((ROUTE_ROWS, TOKEN_TILE), lambda i: (0, i)),
        hbm,
        const((D_MODEL, D_MODEL)), const((1, D_MODEL)), const((PLE_DIM, D_MODEL)),
        const((1, D_MODEL)), const((1, D_MODEL)),
    ]
    return pl.pallas_call(
        functools.partial(_combine_kernel, n_steps=n_t),
        grid=(n_t,),
        in_specs=in_specs,
        out_specs=pl.BlockSpec((TOKEN_TILE, D_MODEL), tok_map),
        out_shape=jax.ShapeDtypeStruct((n_tok, D_MODEL), jnp.float32),
        scratch_shapes=[
            pltpu.VMEM((COMBINE_BUFFERS, TOKEN_TILE, D_MODEL), jnp.float32),
            pltpu.VMEM((COMBINE_BUFFERS, TOKEN_TILE, PLE_DIM), jnp.float32),
            pltpu.VMEM((COMBINE_BUFFERS, TOKEN_TILE, HALF), jnp.uint32),
            pltpu.VMEM((COMBINE_BUFFERS, TOKEN_TILE, HALF), jnp.uint32),
            pltpu.SemaphoreType.DMA((COMBINE_BUFFERS,)),
        ],
        compiler_params=pltpu.CompilerParams(dimension_semantics=("arbitrary",), vmem_limit_bytes=VMEM_LIMIT),
    )(h1, p2, route_t, y_tok, w_pg, b_pg, w_ple, ln2_g, ln2_b)


def _split_bf16(w):
    hi = w.astype(jnp.bfloat16)
    lo = (w - hi.astype(jnp.float32)).astype(jnp.bfloat16)
    return hi, lo


def kernel(x, p, ln_in_g, ln_in_b, w_in, conv_w, conv_b, pool_w, pool_scale, w_out, ln1_g, ln1_b,
           w_rg, b_rg, w_re, b_re, w_gate, w_up, w_down, w_pg, b_pg, w_ple, ln2_g, ln2_b):
    batch, seq, _ = x.shape
    n_tok = batch * seq
    bf = jnp.bfloat16
    row = lambda v: v.reshape(1, -1)

    w_r = jnp.concatenate([w_rg[0], jnp.transpose(w_re[0], (1, 0, 2)).reshape(D_MODEL, N_EXPERTS)], axis=1)
    w_r = jnp.pad(w_r, ((0, 0), (0, LANES - w_r.shape[1])))
    w_r_hi, w_r_lo = _split_bf16(w_r)
    w_r_cat = jnp.concatenate([w_r_hi, w_r_hi, w_r_lo], axis=0)
    b_r = jnp.pad(jnp.concatenate([b_rg[0], b_re[0].reshape(-1)]), (0, LANES - N_GROUPS - N_EXPERTS)).reshape(1, LANES)

    x2 = x.reshape(n_tok, D_MODEL)
    p2 = p[0].reshape(n_tok, PLE_DIM)
    mixer_weights = (row(ln_in_g), row(ln_in_b), w_in[0], conv_w, row(conv_b[0]),
                     pool_w[0], row(pool_scale[0]), w_out[0], row(ln1_g[0]), row(ln1_b[0]),
                     w_r_cat, b_r)
    combine_weights = (w_pg[0].astype(bf), row(b_pg[0]), w_ple[0].astype(bf), row(ln2_g[0]), row(ln2_b[0]))
    expert_weights = (w_gate[0].reshape(N_EXPERTS, D_MODEL, D_EXPERT),
                      w_up[0].reshape(N_EXPERTS, D_MODEL, D_EXPERT),
                      w_down[0].reshape(N_EXPERTS, D_EXPERT, D_MODEL))
    zero_rows = jnp.zeros((SC_WINDOW, HALF), jnp.uint32)

    n_rows = -(-(2 * n_tok + N_EXPERTS * (ROW_TILE - 1)) // ROW_TILE) * ROW_TILE
    xp, h1, route_t, counts = _run_mixer(x2, *mixer_weights, batch, seq)
    pos, tile_start, tiles_per_expert, pad_pos = _run_plan(route_t, counts, n_rows)
    x_sorted = _dispatch_rows(xp, pos, pad_pos, zero_rows, n_rows)
    y_sorted = _run_experts(tile_start[0, :N_EXPERTS], tiles_per_expert[0, :N_EXPERTS], x_sorted, *expert_weights)

    y_tok = _gather_rows(y_sorted, pos.reshape(-1))
    out = _run_combine(h1, p2, route_t, y_tok, *combine_weights)
    return out.reshape(batch, seq, D_MODEL)
```

```python
import functools

import jax
import jax.numpy as jnp
from jax import lax
from jax.experimental import pallas as pl
from jax.experimental.pallas import tpu as pltpu
from jax.experimental.pallas import tpu_sc as plsc

D_MODEL = 1024
CONV_WIDTH = 512
CONV_K = 3
POOL_WIDTH = 512
POOL_WINDOWS = (2, 4, 8, 16)
POOL_GW = 128
IN_PROJ = 3 * CONV_WIDTH + POOL_WIDTH
N_GROUPS = 4
EXPERTS_PER_GROUP = 8
N_EXPERTS = N_GROUPS * EXPERTS_PER_GROUP
D_EXPERT = 256
PLE_DIM = 256
LN_EPS = 1e-5
DEEPNORM_ALPHA = 2.0 ** 0.25

LANES = 128
HALF = D_MODEL // 2
CONV_HALO = 8
POOL_HALO = 16
SEQ_TILE = 1024
SUB_TILE = 256
MIXER_PHASES = 8
COMBINE_PHASES = 3
ROW_TILE = 256
TILE_BUFFERS = 8
SC_WORKERS = 32
SC_WINDOW = 64
GATHER_WINDOW = 64
GATHER_BUFFERS = 3
WEIGHT_BUFFERS = 3
TOKEN_TILE = 1024
VMEM_LIMIT = 56 * 1024 * 1024

R_ID1, R_ID2, R_RANK1, R_RANK2, R_W1, R_W2 = range(6)
ROUTE_ROWS = 8
LOGIT_ROWS = 40


def _layernorm(x, g, b):
    mu = jnp.mean(x, axis=-1, keepdims=True)
    xc = x - mu
    var = jnp.mean(xc * xc, axis=-1, keepdims=True)
    return xc * lax.rsqrt(var + LN_EPS) * g + b


def _pack_bf16_pairs(v):
    bits = lax.bitcast_convert_type(v.astype(jnp.bfloat16).astype(jnp.float32), jnp.uint32)
    return bits[:, HALF:] | (bits[:, :HALF] >> 16)


def _unpack_bf16_pairs(w):
    lo = lax.bitcast_convert_type(w << 16, jnp.float32)
    hi = lax.bitcast_convert_type(w & jnp.uint32(0xFFFF0000), jnp.float32)
    return lo, hi


def _mixer_kernel(x_ref, lnin_g, lnin_b, w_in_f32, conv_w, conv_b, pool_w_f32, pool_scale, w_out_f32,
                  ln1_g, ln1_b, w_r, b_r,
                  xp_ref, h1_ref, route_t_ref, counts_ref,
                  zbuf, vbuf, carry, w_in, pool_w, w_out):
    b = pl.program_id(0)
    s = pl.program_id(1)
    ts = x_ref.shape[0]

    @pl.when(s == 0)
    def _():
        zbuf[0:CONV_HALO, :] = jnp.zeros((CONV_HALO, CONV_WIDTH), jnp.float32)
        vbuf[0:POOL_HALO, :] = jnp.zeros((POOL_HALO, POOL_WIDTH), jnp.float32)

    @pl.when((b == 0) & (s == 0))
    def _():
        carry[...] = jnp.zeros_like(carry)
        w_in[...] = w_in_f32[...].astype(jnp.bfloat16)
        pool_w[...] = pool_w_f32[...].astype(jnp.bfloat16)
        w_out[...] = w_out_f32[...].astype(jnp.bfloat16)

    chains = [_mixer_rows(r0, s * ts + r0, x_ref, lnin_g, lnin_b, w_in, conv_w, conv_b, pool_w, pool_scale,
                          w_out, ln1_g, ln1_b, w_r, b_r,
                          xp_ref, h1_ref, route_t_ref, zbuf, vbuf, carry)
              for r0 in range(0, ts, SUB_TILE)]
    for t in range(MIXER_PHASES + len(chains) - 1):
        for k, chain in reversed(list(enumerate(chains))):
            if 0 <= t - k < MIXER_PHASES:
                next(chain, None)
    zbuf[0:CONV_HALO, :] = zbuf[ts:ts + CONV_HALO, :]
    vbuf[0:POOL_HALO, :] = vbuf[ts:ts + POOL_HALO, :]
    counts_ref[...] = jnp.transpose(carry[...])[0:1, :]


def _mixer_rows(r0, seq0, x_ref, lnin_g, lnin_b, w_in, conv_w, conv_b, pool_w, pool_scale, w_out,
                ln1_g, ln1_b, w_r, b_r,
                xp_ref, h1_ref, route_t_ref, zbuf, vbuf, carry):
    n = SUB_TILE
    rows = pl.ds(r0, n)
    h0 = _layernorm(x_ref[rows, :], lnin_g[...], lnin_b[...])
    h0b = h0.astype(jnp.bfloat16)
    yield
    u_a = jnp.dot(h0b, w_in[:, 0:IN_PROJ // 2], preferred_element_type=jnp.float32)
    yield
    u_b = jnp.dot(h0b, w_in[:, IN_PROJ // 2:], preferred_element_type=jnp.float32)
    yield
    b_g = u_a[:, 0:CONV_WIDTH]
    c_g = u_a[:, CONV_WIDTH:]
    v_c = u_b[:, 0:CONV_WIDTH]
    v_p = u_b[:, CONV_WIDTH:]

    zbuf[pl.ds(CONV_HALO + r0, n), :] = c_g * v_c
    zext = zbuf[pl.ds(r0, n + CONV_HALO), :]
    z1 = pltpu.roll(zext, 1, axis=0)[CONV_HALO:, :]
    z2 = pltpu.roll(zext, 2, axis=0)[CONV_HALO:, :]
    conv = z2 * conv_w[0, 0:1, :] + z1 * conv_w[0, 1:2, :] + zext[CONV_HALO:, :] * conv_w[0, 2:3, :] + conv_b[...]
    y_conv = b_g * conv

    vbuf[pl.ds(POOL_HALO + r0, n), :] = v_p
    vext = vbuf[pl.ds(r0, n + POOL_HALO), :]
    s2 = vext + pltpu.roll(vext, 1, axis=0)
    s4 = s2[:, POOL_GW:] + pltpu.roll(s2[:, POOL_GW:], 2, axis=0)
    s8 = s4[:, POOL_GW:] + pltpu.roll(s4[:, POOL_GW:], 4, axis=0)
    s16 = s8[:, POOL_GW:] + pltpu.roll(s8[:, POOL_GW:], 8, axis=0)
    wsums = (s2[POOL_HALO:, 0:POOL_GW], s4[POOL_HALO:, 0:POOL_GW],
             s8[POOL_HALO:, 0:POOL_GW], s16[POOL_HALO:, 0:POOL_GW])

    t_pos = seq0 + lax.broadcasted_iota(jnp.int32, (n, 1), 0)
    y_pool = []
    for j, w in enumerate(POOL_WINDOWS):
        inv_cnt = 1.0 / jnp.minimum(t_pos + 1, w).astype(jnp.float32)
        pooled = wsums[j] * inv_cnt - v_p[:, j * POOL_GW:(j + 1) * POOL_GW]
        y_pool.append(jnp.dot(pooled.astype(jnp.bfloat16), pool_w[j], preferred_element_type=jnp.float32))
    y_pool = jnp.concatenate(y_pool, axis=-1) * pool_scale[...]

    ycat = jnp.concatenate([y_conv, y_pool], axis=-1).astype(jnp.bfloat16)
    yield
    mix = jnp.dot(ycat, w_out[...], preferred_element_type=jnp.float32)
    yield
    h1 = _layernorm(DEEPNORM_ALPHA * h0 + mix, ln1_g[...], ln1_b[...])

    h_hi = h1.astype(jnp.bfloat16)
    h_lo = (h1 - h_hi.astype(jnp.float32)).astype(jnp.bfloat16)
    xp_ref[rows, :] = _pack_bf16_pairs(h1)
    h1_ref[rows, :] = h1
    hcat = jnp.concatenate([h_hi, h_lo, h_hi], axis=-1)
    yield
    logits = jnp.dot(hcat, w_r[...], preferred_element_type=jnp.float32) + b_r[...]
    yield
    lt = jnp.transpose(logits)[0:LOGIT_ROWS, :]
    rid = lax.broadcasted_iota(jnp.int32, (LOGIT_ROWS, n), 0).astype(jnp.float32)
    neg = jnp.float32(-jnp.inf)

    def first_argmax(vals):
        m = jnp.max(vals, axis=0, keepdims=True)
        idx = jnp.min(jnp.where(vals == m, rid, float(LOGIT_ROWS)), axis=0, keepdims=True)
        return m, idx

    g_mask = rid < N_GROUPS
    g_max, g_idx = first_argmax(jnp.where(g_mask, lt, neg))
    g_w = 1.0 / jnp.sum(jnp.where(g_mask, jnp.exp(lt - g_max), 0.0), axis=0, keepdims=True)

    e_lo = N_GROUPS + EXPERTS_PER_GROUP * g_idx
    e_vals = jnp.where((rid >= e_lo) & (rid < e_lo + EXPERTS_PER_GROUP), lt, neg)
    m1, i1 = first_argmax(e_vals)
    m2, i2 = first_argmax(jnp.where(rid == i1, neg, e_vals))
    e21 = jnp.exp(m2 - m1)
    w1 = g_w / (1.0 + e21)
    w2 = g_w * e21 / (1.0 + e21)
    id1 = i1 - N_GROUPS
    id2 = i2 - N_GROUPS

    eid = lax.broadcasted_iota(jnp.int32, (N_EXPERTS, n), 0).astype(jnp.float32)
    sel1 = eid == id1
    sel2 = eid == id2
    onehot = (sel1 | sel2).astype(jnp.float32)
    src = lax.broadcasted_iota(jnp.int32, (n, n), 0)
    dst = lax.broadcasted_iota(jnp.int32, (n, n), 1)
    earlier = (src < dst).astype(jnp.bfloat16)
    before = (jnp.dot(onehot.astype(jnp.bfloat16), earlier, preferred_element_type=jnp.float32)
              + carry[0:N_EXPERTS, 0:1])
    rank1 = jnp.sum(jnp.where(sel1, before, 0.0), axis=0, keepdims=True)
    rank2 = jnp.sum(jnp.where(sel2, before, 0.0), axis=0, keepdims=True)
    carry[0:N_EXPERTS, :] = carry[0:N_EXPERTS, :] + jnp.sum(onehot, axis=1, keepdims=True)

    rec_t = jnp.zeros((ROUTE_ROWS, n), jnp.float32)
    rec_row = lax.broadcasted_iota(jnp.int32, (ROUTE_ROWS, n), 0)
    for k, val in ((R_ID1, id1), (R_ID2, id2), (R_W1, w1), (R_W2, w2), (R_RANK1, rank1), (R_RANK2, rank2)):
        rec_t = jnp.where(rec_row == k, val, rec_t)
    route_t_ref[:, rows] = rec_t


def _run_mixer(x2, lnin_g, lnin_b, w_in, conv_w, conv_b, pool_w, pool_scale, w_out, ln1_g, ln1_b,
               w_r, b_r, batch, seq):
    n_tok = batch * seq
    n_s = seq // SEQ_TILE
    tok_map = lambda b, s: (b * n_s + s, 0)

    def const(shape):
        return pl.BlockSpec(shape, lambda b, s: (0,) * len(shape), pipeline_mode=pl.Buffered(1))

    in_specs = [
        pl.BlockSpec((SEQ_TILE, D_MODEL), tok_map),
        const((1, D_MODEL)), const((1, D_MODEL)),
        const((D_MODEL, IN_PROJ)),
        const((1, CONV_K, CONV_WIDTH)), const((1, CONV_WIDTH)),
        const((len(POOL_WINDOWS), POOL_GW, POOL_GW)), const((1, POOL_WIDTH)),
        const((D_MODEL, D_MODEL)),
        const((1, D_MODEL)), const((1, D_MODEL)),
        const((3 * D_MODEL, LANES)), const((1, LANES)),
    ]
    out_specs = [
        pl.BlockSpec((SEQ_TILE, HALF), tok_map),
        pl.BlockSpec((SEQ_TILE, D_MODEL), tok_map),
        pl.BlockSpec((ROUTE_ROWS, SEQ_TILE), lambda b, s: (0, b * n_s + s)),
        pl.BlockSpec((1, LANES), lambda b, s: (0, 0)),
    ]
    out_shape = [
        jax.ShapeDtypeStruct((n_tok, HALF), jnp.uint32),
        jax.ShapeDtypeStruct((n_tok, D_MODEL), jnp.float32),
        jax.ShapeDtypeStruct((ROUTE_ROWS, n_tok), jnp.float32),
        jax.ShapeDtypeStruct((1, LANES), jnp.float32),
    ]
    return pl.pallas_call(
        _mixer_kernel,
        grid=(batch, n_s),
        in_specs=in_specs,
        out_specs=out_specs,
        out_shape=out_shape,
        scratch_shapes=[
            pltpu.VMEM((SEQ_TILE + CONV_HALO, CONV_WIDTH), jnp.float32),
            pltpu.VMEM((SEQ_TILE + POOL_HALO, POOL_WIDTH), jnp.float32),
            pltpu.VMEM((LANES, LANES), jnp.float32),
            pltpu.VMEM((D_MODEL, IN_PROJ), jnp.bfloat16),
            pltpu.VMEM((len(POOL_WINDOWS), POOL_GW, POOL_GW), jnp.bfloat16),
            pltpu.VMEM((D_MODEL, D_MODEL), jnp.bfloat16),
        ],
        compiler_params=pltpu.CompilerParams(
            dimension_semantics=("arbitrary", "arbitrary"), vmem_limit_bytes=VMEM_LIMIT),
    )(x2, lnin_g, lnin_b, w_in, conv_w, conv_b, pool_w, pool_scale, w_out, ln1_g, ln1_b, w_r, b_r)


def _plan_kernel(rt_ref, counts_ref, pos_ref, tile_start_ref, tiles_ref, pad_ref, *, n_rows):
    lane = lax.broadcasted_iota(jnp.int32, (ROUTE_ROWS, LANES), 1)
    counts = jnp.broadcast_to(counts_ref[...], (ROUTE_ROWS, LANES))
    tiles = jnp.floor((counts + (ROW_TILE - 1)) * (1.0 / ROW_TILE))
    tile_end = tiles
    shift = 1
    while shift < N_EXPERTS:
        tile_end = tile_end + jnp.where(lane >= shift, pltpu.roll(tile_end, shift, axis=1), 0.0)
        shift *= 2
    row_start = (tile_end - tiles) * ROW_TILE
    tile_start_ref[...] = (tile_end - tiles)[0:1, :].astype(jnp.int32)
    tiles_ref[...] = tiles[0:1, :].astype(jnp.int32)

    rt = rt_ref[...]
    ids = rt[R_ID1:R_ID2 + 1, :]
    start = jnp.zeros_like(ids)
    for e in range(N_EXPERTS):
        start = jnp.where(ids == e, row_start[0:1, e:e + 1], start)
    pos_ref[...] = (start + rt[R_RANK1:R_RANK2 + 1, :]).astype(jnp.int32)

    sub = lax.broadcasted_iota(jnp.int32, (N_EXPERTS, LANES), 0)
    lane_e = lax.broadcasted_iota(jnp.int32, (N_EXPERTS, LANES), 1)
    diag = sub == lane_e
    pad_lo = jnp.sum(jnp.where(diag, (row_start + counts)[0:1, :], 0.0), axis=1, keepdims=True)
    pad_n = jnp.sum(jnp.where(diag, (tiles * ROW_TILE - counts)[0:1, :], 0.0), axis=1, keepdims=True)
    j = lax.broadcasted_iota(jnp.int32, (N_EXPERTS, ROW_TILE), 1).astype(jnp.float32)
    pad_ref[...] = jnp.where(j < pad_n, pad_lo + j, n_rows + j).astype(jnp.int32)


def _run_plan(route_t, counts, n_rows):
    n_tok = route_t.shape[1]
    full = lambda shape: pl.BlockSpec(shape, lambda i: (0, 0))
    return pl.pallas_call(
        functools.partial(_plan_kernel, n_rows=n_rows),
        grid=(1,),
        in_specs=[full((ROUTE_ROWS, n_tok)), full((1, LANES))],
        out_specs=[full((2, n_tok)), full((1, LANES)), full((1, LANES)), full((N_EXPERTS, ROW_TILE))],
        out_shape=[
            jax.ShapeDtypeStruct((2, n_tok), jnp.int32),
            jax.ShapeDtypeStruct((1, LANES), jnp.int32),
            jax.ShapeDtypeStruct((1, LANES), jnp.int32),
            jax.ShapeDtypeStruct((N_EXPERTS, ROW_TILE), jnp.int32),
        ],
        compiler_params=pltpu.CompilerParams(dimension_semantics=("arbitrary",)),
    )(route_t, counts)


def _sc_mesh():
    return plsc.VectorSubcoreMesh(core_axis_name="core", subcore_axis_name="subcore")


def _sc_worker_id():
    return lax.axis_index("core") * (SC_WORKERS // 2) + lax.axis_index("subcore")


def _dispatch_rows(xp, pos, pad_pos, zero_rows, n_rows):
    n_tok, width = xp.shape
    second = n_tok // SC_WINDOW
    n_win = n_tok // SC_WORKERS // SC_WINDOW
    n_pad = pad_pos.size // SC_WORKERS // SC_WINDOW
    as_windows = lambda v: v.reshape(-1, SC_WINDOW)

    @functools.partial(
        pl.kernel, out_type=jax.ShapeDtypeStruct((n_rows + ROW_TILE, width), xp.dtype), mesh=_sc_mesh(),
        scratch_types=[pltpu.VMEM((n_win, SC_WINDOW), jnp.int32), pltpu.VMEM((n_win, SC_WINDOW), jnp.int32),
                       pltpu.VMEM((n_pad, SC_WINDOW), jnp.int32),
                       pltpu.VMEM((2, SC_WINDOW, width), xp.dtype), pltpu.VMEM((SC_WINDOW, width), xp.dtype),
                       pltpu.SemaphoreType.DMA((2,)), pltpu.SemaphoreType.DMA((2,)), pltpu.SemaphoreType.DMA])
    def dispatch(xp_hbm, pos_hbm, pad_hbm, zero_hbm, out_hbm,
                 idx1, idx2, idxp, buf, zbuf, lsem, ssem, psem):
        wid = _sc_worker_id()
        pltpu.sync_copy(pad_hbm.at[pl.ds(wid * n_pad, n_pad)], idxp)
        pltpu.sync_copy(zero_hbm, zbuf)
        pads = [pltpu.make_async_copy(zbuf, out_hbm.at[idxp.at[j]], psem) for j in range(n_pad)]
        for cp in pads:
            cp.start()
        pltpu.sync_copy(pos_hbm.at[pl.ds(wid * n_win, n_win)], idx1)
        pltpu.sync_copy(pos_hbm.at[pl.ds(second + wid * n_win, n_win)], idx2)

        def load(j):
            rows = xp_hbm.at[pl.ds((wid * n_win + j) * SC_WINDOW, SC_WINDOW)]
            return pltpu.make_async_copy(rows, buf.at[j % 2], lsem.at[j % 2])

        def scatters(j):
            return [pltpu.make_async_copy(buf.at[j % 2], out_hbm.at[idx.at[j]], ssem.at[j % 2])
                    for idx in (idx1, idx2)]

        load(0).start()
        for j in range(n_win):
            load(j).wait()
            for cp in scatters(j):
                cp.start()
            if j >= 1:
                for cp in scatters(j - 1):
                    cp.wait()
            if j + 1 < n_win:
                load(j + 1).start()
        for cp in scatters(n_win - 1):
            cp.wait()
        for cp in pads:
            cp.wait()

    return dispatch(xp, as_windows(pos), as_windows(pad_pos), zero_rows)


def _gather_rows(src, idx):
    n_out, width = idx.shape[0], src.shape[1]
    n_win = n_out // SC_WORKERS // GATHER_WINDOW
    nbuf = GATHER_BUFFERS

    @functools.partial(
        pl.kernel, out_type=jax.ShapeDtypeStruct((n_out, width), src.dtype), mesh=_sc_mesh(),
        scratch_types=[pltpu.VMEM((n_win, GATHER_WINDOW), jnp.int32),
                       pltpu.VMEM((nbuf, GATHER_WINDOW, width), src.dtype),
                       pltpu.SemaphoreType.DMA((nbuf,)), pltpu.SemaphoreType.DMA((nbuf,))])
    def gather(src_hbm, idx_hbm, dst_hbm, idx_v, buf, gsem, ssem):
        wid = _sc_worker_id()
        pltpu.sync_copy(idx_hbm.at[pl.ds(wid * n_win, n_win)], idx_v)

        def fetch(j):
            return pltpu.make_async_copy(src_hbm.at[idx_v.at[j]], buf.at[j % nbuf], gsem.at[j % nbuf])

        def store(j):
            rows = dst_hbm.at[pl.ds((wid * n_win + j) * GATHER_WINDOW, GATHER_WINDOW)]
            return pltpu.make_async_copy(buf.at[j % nbuf], rows, ssem.at[j % nbuf])

        for j in range(min(nbuf - 1, n_win)):
            fetch(j).start()
        for j in range(n_win):
            fetch(j).wait()
            store(j).start()
            if j + nbuf - 1 < n_win:
                if j >= 1:
                    store(j - 1).wait()
                fetch(j + nbuf - 1).start()
        for j in range(max(0, n_win - nbuf), n_win):
            store(j).wait()

    return gather(src, idx.reshape(-1, GATHER_WINDOW))


def _tile_copy(hbm, buf, sem, tile, slot, to_hbm):
    rows = hbm.at[pl.ds(pl.multiple_of(tile * ROW_TILE, ROW_TILE), ROW_TILE)]
    if to_hbm:
        return pltpu.make_async_copy(buf.at[slot], rows, sem.at[slot])
    return pltpu.make_async_copy(rows, buf.at[slot], sem.at[slot])


def _weight_copies(w_hbm, wbuf, wsem, expert):
    slot = expert % WEIGHT_BUFFERS
    return [pltpu.make_async_copy(w.at[expert], buf.at[slot], wsem.at[slot]) for w, buf in zip(w_hbm, wbuf)]


def _expert_kernel(ts_ref, nte_ref, x_hbm, wg_hbm, wu_hbm, wd_hbm, y_hbm,
                   xbuf, ybuf, xsem, ysem, wg_buf, wu_buf, wd_buf, wsem, wgu_bf, wd_bf, act_ref, *, max_tiles):
    e = pl.program_id(0)
    first = ts_ref[e]
    count = nte_ref[e]
    n_tiles = ts_ref[N_EXPERTS - 1] + nte_ref[N_EXPERTS - 1]
    ahead = TILE_BUFFERS - 2
    w_hbm = (wg_hbm, wu_hbm, wd_hbm)
    wbuf = (wg_buf, wu_buf, wd_buf)

    @pl.when(e == 0)
    def _():
        for g in range(ahead):
            @pl.when(g < n_tiles)
            def _():
                _tile_copy(x_hbm, xbuf, xsem, g, g, False).start()
        for k in range(WEIGHT_BUFFERS - 1):
            for cp in _weight_copies(w_hbm, wbuf, wsem, k):
                cp.start()

    @pl.when(e + WEIGHT_BUFFERS - 1 < N_EXPERTS)
    def _():
        for cp in _weight_copies(w_hbm, wbuf, wsem, e + WEIGHT_BUFFERS - 1):
            cp.start()

    for cp in _weight_copies(w_hbm, wbuf, wsem, e):
        cp.wait()

    def sync_x(g):
        _tile_copy(x_hbm, xbuf, xsem, g, g % TILE_BUFFERS, False).wait()

        @pl.when(g + ahead < n_tiles)
        def _():
            _tile_copy(x_hbm, xbuf, xsem, g + ahead, (g + ahead) % TILE_BUFFERS, False).start()

    def sync_y_slot(g):
        @pl.when(g >= TILE_BUFFERS)
        def _():
            _tile_copy(y_hbm, ybuf, ysem, g - TILE_BUFFERS, g % TILE_BUFFERS, True).wait()

    def up_proj(g):
        lo, hi = _unpack_bf16_pairs(xbuf[g % TILE_BUFFERS])
        return (jnp.dot(lo.astype(jnp.bfloat16), wgu_bf[0:HALF, :], preferred_element_type=jnp.float32)
                + jnp.dot(hi.astype(jnp.bfloat16), wgu_bf[HALF:, :], preferred_element_type=jnp.float32))

    def put_act(hgu, which):
        hg = hgu[:, 0:D_EXPERT]
        act_ref[which] = (hg * jax.nn.sigmoid(hg) * hgu[:, D_EXPERT:]).astype(jnp.bfloat16)

    def down_proj(g, which):
        y = jnp.dot(act_ref[which], wd_bf[...], preferred_element_type=jnp.float32)
        ybuf[g % TILE_BUFFERS] = _pack_bf16_pairs(y)

    def start_y(g):
        _tile_copy(y_hbm, ybuf, ysem, g, g % TILE_BUFFERS, True).start()

    @pl.when(count > 0)
    def _():
        slot = e % WEIGHT_BUFFERS
        wgu_bf[:, 0:D_EXPERT] = wg_buf[slot].astype(jnp.bfloat16)
        wgu_bf[:, D_EXPERT:] = wu_buf[slot].astype(jnp.bfloat16)
        wd_bf[...] = wd_buf[slot].astype(jnp.bfloat16)

        sync_x(first)
        put_act(up_proj(first), 0)

        def pair_body(i, c):
            g = first + 1 + 2 * i
            sync_x(g)
            sync_x(g + 1)
            sync_y_slot(g - 1)
            sync_y_slot(g)
            down_proj(g - 1, 0)
            hgu0 = up_proj(g)
            put_act(hgu0, 1)
            hgu1 = up_proj(g + 1)
            down_proj(g, 1)
            put_act(hgu1, 0)
            start_y(g - 1)
            start_y(g)
            return c

        n_pairs = (count - 1) // 2
        lax.fori_loop(0, n_pairs, pair_body, 0)

        @pl.when((count - 1) % 2 == 1)
        def _():
            g = first + count - 1
            sync_x(g)
            sync_y_slot(g - 1)
            down_proj(g - 1, 0)
            put_act(up_proj(g), 0)
            start_y(g - 1)

        last = first + count - 1
        sync_y_slot(last)
        down_proj(last, 0)
        start_y(last)

    @pl.when(e == N_EXPERTS - 1)
    def _():
        for k in range(TILE_BUFFERS, 0, -1):
            @pl.when(n_tiles >= k)
            def _():
                _tile_copy(y_hbm, ybuf, ysem, n_tiles - k, (n_tiles - k) % TILE_BUFFERS, True).wait()

        ybuf[0] = jnp.zeros((ROW_TILE, HALF), jnp.uint32)

        def fill(g, c):
            cp = _tile_copy(y_hbm, ybuf, ysem, g, 0, True)
            cp.start()
            cp.wait()
            return c

        lax.fori_loop(n_tiles, max_tiles, fill, 0)


def _run_experts(tile_start, tiles_per_expert, x_sorted, w_gate, w_up, w_down):
    n_rows = x_sorted.shape[0]
    max_tiles = n_rows // ROW_TILE
    hbm = pl.BlockSpec(memory_space=pl.ANY)

    grid_spec = pltpu.PrefetchScalarGridSpec(
        num_scalar_prefetch=2,
        grid=(N_EXPERTS,),
        in_specs=[hbm, hbm, hbm, hbm],
        out_specs=hbm,
        scratch_shapes=[
            pltpu.VMEM((TILE_BUFFERS, ROW_TILE, HALF), jnp.uint32),
            pltpu.VMEM((TILE_BUFFERS, ROW_TILE, HALF), jnp.uint32),
            pltpu.SemaphoreType.DMA((TILE_BUFFERS,)),
            pltpu.SemaphoreType.DMA((TILE_BUFFERS,)),
            pltpu.VMEM((WEIGHT_BUFFERS, D_MODEL, D_EXPERT), jnp.float32),
            pltpu.VMEM((WEIGHT_BUFFERS, D_MODEL, D_EXPERT), jnp.float32),
            pltpu.VMEM((WEIGHT_BUFFERS, D_EXPERT, D_MODEL), jnp.float32),
            pltpu.SemaphoreType.DMA((WEIGHT_BUFFERS,)),
            pltpu.VMEM((D_MODEL, 2 * D_EXPERT), jnp.bfloat16),
            pltpu.VMEM((D_EXPERT, D_MODEL), jnp.bfloat16),
            pltpu.VMEM((2, ROW_TILE, D_EXPERT), jnp.bfloat16),
        ],
    )
    return pl.pallas_call(
        functools.partial(_expert_kernel, max_tiles=max_tiles),
        grid_spec=grid_spec,
        out_shape=jax.ShapeDtypeStruct((n_rows, HALF), jnp.uint32),
        compiler_params=pltpu.CompilerParams(
            dimension_semantics=("arbitrary",), vmem_limit_bytes=VMEM_LIMIT),
    )(tile_start, tiles_per_expert, x_sorted, w_gate, w_up, w_down)


def _combine_rows(r0, h1_ref, p_ref, route_t_ref, ya_ref, yb_ref, w_pg, b_pg, w_ple, g_ref, b_ref, o_ref):
    rows = pl.ds(r0, SUB_TILE)
    h_hi = h1_ref[rows, :].astype(jnp.bfloat16)
    p_b = p_ref[rows, :].astype(jnp.bfloat16)
    rec_t = route_t_ref[:, rows]
    padded = jnp.concatenate([rec_t, jnp.zeros((LANES - ROUTE_ROWS, SUB_TILE), jnp.float32)], axis=0)
    route = jnp.transpose(padded)
    yield
    gate_pre = jnp.dot(h_hi, w_pg[...], preferred_element_type=jnp.float32)
    ple_pre = jnp.dot(p_b, w_ple[...], preferred_element_type=jnp.float32)
    yield
    ple = ple_pre * jax.nn.sigmoid(gate_pre + b_pg[...])
    w1 = route[:, R_W1:R_W1 + 1]
    w2 = route[:, R_W2:R_W2 + 1]
    a_lo, a_hi = _unpack_bf16_pairs(ya_ref[rows, :])
    b_lo, b_hi = _unpack_bf16_pairs(yb_ref[rows, :])
    moe = jnp.concatenate([w1 * a_lo + w2 * b_lo, w1 * a_hi + w2 * b_hi], axis=-1)
    o_ref[rows, :] = _layernorm(DEEPNORM_ALPHA * h1_ref[rows, :] + ple + moe, g_ref[...], b_ref[...])


def _combine_kernel(h1_ref, p_ref, route_t_ref, ya_ref, yb_ref, w_pg, b_pg, w_ple, g_ref, b_ref, o_ref):
    chains = [_combine_rows(r0, h1_ref, p_ref, route_t_ref, ya_ref, yb_ref, w_pg, b_pg, w_ple, g_ref, b_ref, o_ref)
              for r0 in range(0, h1_ref.shape[0], SUB_TILE)]
    for t in range(COMBINE_PHASES + len(chains) - 1):
        for k, chain in reversed(list(enumerate(chains))):
            if 0 <= t - k < COMBINE_PHASES:
                next(chain, None)


def _run_combine(h1, p2, route_t, y_tok, w_pg, b_pg, w_ple, ln2_g, ln2_b):
    n_tok = h1.shape[0]
    n_t = n_tok // TOKEN_TILE
    tok_map = lambda i: (i, 0)
    const = lambda shape: pl.BlockSpec(shape, lambda i: (0, 0), pipeline_mode=pl.Buffered(1))
    in_specs = [
        pl.BlockSpec((TOKEN_TILE, D_MODEL), tok_map),
        pl.BlockSpec((TOKEN_TILE, PLE_DIM), tok_map),
        pl.BlockSpec((ROUTE_ROWS, TOKEN_TILE), lambda i: (0, i)),
        pl.BlockSpec((TOKEN_TILE, HALF), lambda i: (i, 0)),
        pl.BlockSpec((TOKEN_TILE, HALF), lambda i: (i + n_t, 0)),
        const((D_MODEL, D_MODEL)), const((1, D_MODEL)), const((PLE_DIM, D_MODEL)),
        const((1, D_MODEL)), const((1, D_MODEL)),
    ]
    return pl.pallas_call(
        _combine_kernel,
        grid=(n_t,),
        in_specs=in_specs,
        out_specs=pl.BlockSpec((TOKEN_TILE, D_MODEL), tok_map),
        out_shape=jax.ShapeDtypeStruct((n_tok, D_MODEL), jnp.float32),
        compiler_params=pltpu.CompilerParams(dimension_semantics=("arbitrary",), vmem_limit_bytes=VMEM_LIMIT),
    )(h1, p2, route_t, y_tok, y_tok, w_pg, b_pg, w_ple, ln2_g, ln2_b)


def _split_bf16(w):
    hi = w.astype(jnp.bfloat16)
    lo = (w - hi.astype(jnp.float32)).astype(jnp.bfloat16)
    return hi, lo


def kernel(x, p, ln_in_g, ln_in_b, w_in, conv_w, conv_b, pool_w, pool_scale, w_out, ln1_g, ln1_b,
           w_rg, b_rg, w_re, b_re, w_gate, w_up, w_down, w_pg, b_pg, w_ple, ln2_g, ln2_b):
    batch, seq, _ = x.shape
    n_tok = batch * seq
    bf = jnp.bfloat16
    row = lambda v: v.reshape(1, -1)

    w_r = jnp.concatenate([w_rg[0], jnp.transpose(w_re[0], (1, 0, 2)).reshape(D_MODEL, N_EXPERTS)], axis=1)
    w_r = jnp.pad(w_r, ((0, 0), (0, LANES - w_r.shape[1])))
    w_r_hi, w_r_lo = _split_bf16(w_r)
    w_r_cat = jnp.concatenate([w_r_hi, w_r_hi, w_r_lo], axis=0)
    b_r = jnp.pad(jnp.concatenate([b_rg[0], b_re[0].reshape(-1)]), (0, LANES - N_GROUPS - N_EXPERTS)).reshape(1, LANES)

    x2 = x.reshape(n_tok, D_MODEL)
    p2 = p[0].reshape(n_tok, PLE_DIM)
    mixer_weights = (row(ln_in_g), row(ln_in_b), w_in[0], conv_w, row(conv_b[0]),
                     pool_w[0], row(pool_scale[0]), w_out[0], row(ln1_g[0]), row(ln1_b[0]),
                     w_r_cat, b_r)
    combine_weights = (w_pg[0].astype(bf), row(b_pg[0]), w_ple[0].astype(bf), row(ln2_g[0]), row(ln2_b[0]))
    expert_weights = (w_gate[0].reshape(N_EXPERTS, D_MODEL, D_EXPERT),
                      w_up[0].reshape(N_EXPERTS, D_MODEL, D_EXPERT),
                      w_down[0].reshape(N_EXPERTS, D_EXPERT, D_MODEL))
    zero_rows = jnp.zeros((SC_WINDOW, HALF), jnp.uint32)

    n_rows = -(-(2 * n_tok + N_EXPERTS * (ROW_TILE - 1)) // ROW_TILE) * ROW_TILE
    xp, h1, route_t, counts = _run_mixer(x2, *mixer_weights, batch, seq)
    pos, tile_start, tiles_per_expert, pad_pos = _run_plan(route_t, counts, n_rows)
    x_sorted = _dispatch_rows(xp, pos, pad_pos, zero_rows, n_rows)
    y_sorted = _run_experts(tile_start[0, :N_EXPERTS], tiles_per_expert[0, :N_EXPERTS], x_sorted, *expert_weights)

    y_tok = _gather_rows(y_sorted, pos.reshape(-1))
    out = _run_combine(h1, p2, route_t, y_tok, *combine_weights)
    return out.reshape(batch, seq, D_MODEL)
```

```python
import functools

import jax
import jax.numpy as jnp
from jax import lax
from jax.experimental import pallas as pl
from jax.experimental.pallas import tpu as pltpu
from jax.experimental.pallas import tpu_sc as plsc

D_MODEL = 1024
CONV_WIDTH = 512
CONV_K = 3
POOL_WIDTH = 512
POOL_WINDOWS = (2, 4, 8, 16)
POOL_GW = 128
IN_PROJ = 3 * CONV_WIDTH + POOL_WIDTH
N_GROUPS = 4
EXPERTS_PER_GROUP = 8
N_EXPERTS = N_GROUPS * EXPERTS_PER_GROUP
D_EXPERT = 256
PLE_DIM = 256
LN_EPS = 1e-5
DEEPNORM_ALPHA = 2.0 ** 0.25

LANES = 128
HALF = D_MODEL // 2
CONV_HALO = 8
POOL_HALO = 16
SEQ_TILE = 1024
SUB_TILE = 256
MIXER_PHASES = 8
COMBINE_PHASES = 3
ROW_TILE = 256
TILE_BUFFERS = 8
SC_WORKERS = 32
SC_WINDOW = 64
GATHER_WINDOW = 64
GATHER_BUFFERS = 3
WEIGHT_BUFFERS = 3
TOKEN_TILE = 1024
VMEM_LIMIT = 56 * 1024 * 1024

R_ID1, R_ID2, R_RANK1, R_RANK2, R_W1, R_W2 = range(6)
ROUTE_ROWS = 8
LOGIT_ROWS = 40


def _layernorm(x, g, b):
    mu = jnp.mean(x, axis=-1, keepdims=True)
    xc = x - mu
    var = jnp.mean(xc * xc, axis=-1, keepdims=True)
    return xc * lax.rsqrt(var + LN_EPS) * g + b


def _pack_bf16_pairs(v):
    bits = lax.bitcast_convert_type(v.astype(jnp.bfloat16).astype(jnp.float32), jnp.uint32)
    return bits[:, HALF:] | (bits[:, :HALF] >> 16)


def _unpack_bf16_pairs(w):
    lo = lax.bitcast_convert_type(w << 16, jnp.float32)
    hi = lax.bitcast_convert_type(w & jnp.uint32(0xFFFF0000), jnp.float32)
    return lo, hi


def _mixer_kernel(x_ref, lnin_g, lnin_b, w_in_f32, conv_w, conv_b, pool_w_f32, pool_scale, w_out_f32,
                  ln1_g, ln1_b, w_r, b_r,
                  xp_ref, h1_ref, route_t_ref, counts_ref,
                  zbuf, vbuf, carry, w_in, pool_w, w_out):
    b = pl.program_id(0)
    s = pl.program_id(1)
    ts = x_ref.shape[0]

    @pl.when(s == 0)
    def _():
        zbuf[0:CONV_HALO, :] = jnp.zeros((CONV_HALO, CONV_WIDTH), jnp.float32)
        vbuf[0:POOL_HALO, :] = jnp.zeros((POOL_HALO, POOL_WIDTH), jnp.float32)

    @pl.when((b == 0) & (s == 0))
    def _():
        carry[...] = jnp.zeros_like(carry)
        w_in[:, 0:2 * CONV_WIDTH] = w_in_f32[:, CONV_WIDTH:3 * CONV_WIDTH].astype(jnp.bfloat16)
        w_in[:, 2 * CONV_WIDTH:3 * CONV_WIDTH] = w_in_f32[:, 0:CONV_WIDTH].astype(jnp.bfloat16)
        w_in[:, 3 * CONV_WIDTH:] = w_in_f32[:, 3 * CONV_WIDTH:].astype(jnp.bfloat16)
        pool_w[...] = pool_w_f32[...].astype(jnp.bfloat16)
        w_out[...] = w_out_f32[...].astype(jnp.bfloat16)

    chains = [_mixer_rows(r0, s * ts + r0, x_ref, lnin_g, lnin_b, w_in, conv_w, conv_b, pool_w, pool_scale,
                          w_out, ln1_g, ln1_b, w_r, b_r,
                          xp_ref, h1_ref, route_t_ref, zbuf, vbuf, carry)
              for r0 in range(0, ts, SUB_TILE)]
    for t in range(MIXER_PHASES + len(chains) - 1):
        for k, chain in reversed(list(enumerate(chains))):
            if 0 <= t - k < MIXER_PHASES:
                next(chain, None)
    zbuf[0:CONV_HALO, :] = zbuf[ts:ts + CONV_HALO, :]
    vbuf[0:POOL_HALO, :] = vbuf[ts:ts + POOL_HALO, :]
    counts_ref[...] = jnp.transpose(carry[...])[0:1, :]


def _mixer_rows(r0, seq0, x_ref, lnin_g, lnin_b, w_in, conv_w, conv_b, pool_w, pool_scale, w_out,
                ln1_g, ln1_b, w_r, b_r,
                xp_ref, h1_ref, route_t_ref, zbuf, vbuf, carry):
    n = SUB_TILE
    rows = pl.ds(r0, n)
    h0 = _layernorm(x_ref[rows, :], lnin_g[...], lnin_b[...])
    h0b = h0.astype(jnp.bfloat16)
    yield
    u_a = jnp.dot(h0b, w_in[:, 0:2 * CONV_WIDTH], preferred_element_type=jnp.float32)
    zbuf[pl.ds(CONV_HALO + r0, n), :] = u_a[:, 0:CONV_WIDTH] * u_a[:, CONV_WIDTH:]
    yield
    u_b = jnp.dot(h0b, w_in[:, 2 * CONV_WIDTH:], preferred_element_type=jnp.float32)
    b_g = u_b[:, 0:CONV_WIDTH]
    vbuf[pl.ds(POOL_HALO + r0, n), :] = u_b[:, CONV_WIDTH:]
    yield

    zext = zbuf[pl.ds(r0, n + CONV_HALO), :]
    z1 = pltpu.roll(zext, 1, axis=0)[CONV_HALO:, :]
    z2 = pltpu.roll(zext, 2, axis=0)[CONV_HALO:, :]
    conv = z2 * conv_w[0, 0:1, :] + z1 * conv_w[0, 1:2, :] + zext[CONV_HALO:, :] * conv_w[0, 2:3, :] + conv_b[...]
    y_conv = b_g * conv

    vext = vbuf[pl.ds(r0, n + POOL_HALO), :]
    v_p = vext[POOL_HALO:, :]
    s2 = vext + pltpu.roll(vext, 1, axis=0)
    s4 = s2[:, POOL_GW:] + pltpu.roll(s2[:, POOL_GW:], 2, axis=0)
    s8 = s4[:, POOL_GW:] + pltpu.roll(s4[:, POOL_GW:], 4, axis=0)
    s16 = s8[:, POOL_GW:] + pltpu.roll(s8[:, POOL_GW:], 8, axis=0)
    wsums = (s2[POOL_HALO:, 0:POOL_GW], s4[POOL_HALO:, 0:POOL_GW],
             s8[POOL_HALO:, 0:POOL_GW], s16[POOL_HALO:, 0:POOL_GW])

    t_pos = seq0 + lax.broadcasted_iota(jnp.int32, (n, 1), 0)
    y_pool = []
    for j, w in enumerate(POOL_WINDOWS):
        inv_cnt = 1.0 / jnp.minimum(t_pos + 1, w).astype(jnp.float32)
        pooled = wsums[j] * inv_cnt - v_p[:, j * POOL_GW:(j + 1) * POOL_GW]
        y_pool.append(jnp.dot(pooled.astype(jnp.bfloat16), pool_w[j], preferred_element_type=jnp.float32))
    y_pool = jnp.concatenate(y_pool, axis=-1) * pool_scale[...]

    ycat = jnp.concatenate([y_conv, y_pool], axis=-1).astype(jnp.bfloat16)
    yield
    mix = jnp.dot(ycat, w_out[...], preferred_element_type=jnp.float32)
    yield
    h1 = _layernorm(DEEPNORM_ALPHA * h0 + mix, ln1_g[...], ln1_b[...])

    h_hi = h1.astype(jnp.bfloat16)
    h_lo = (h1 - h_hi.astype(jnp.float32)).astype(jnp.bfloat16)
    xp_ref[rows, :] = _pack_bf16_pairs(h1)
    h1_ref[rows, :] = h1
    hcat = jnp.concatenate([h_hi, h_lo, h_hi], axis=-1)
    yield
    logits = jnp.dot(hcat, w_r[...], preferred_element_type=jnp.float32) + b_r[...]
    yield
    lt = jnp.transpose(logits)[0:LOGIT_ROWS, :]
    rid = lax.broadcasted_iota(jnp.int32, (LOGIT_ROWS, n), 0).astype(jnp.float32)
    neg = jnp.float32(-jnp.inf)

    def first_argmax(vals):
        m = jnp.max(vals, axis=0, keepdims=True)
        idx = jnp.min(jnp.where(vals == m, rid, float(LOGIT_ROWS)), axis=0, keepdims=True)
        return m, idx

    g_mask = rid < N_GROUPS
    g_max, g_idx = first_argmax(jnp.where(g_mask, lt, neg))
    g_w = 1.0 / jnp.sum(jnp.where(g_mask, jnp.exp(lt - g_max), 0.0), axis=0, keepdims=True)

    e_lo = N_GROUPS + EXPERTS_PER_GROUP * g_idx
    e_vals = jnp.where((rid >= e_lo) & (rid < e_lo + EXPERTS_PER_GROUP), lt, neg)
    m1, i1 = first_argmax(e_vals)
    m2, i2 = first_argmax(jnp.where(rid == i1, neg, e_vals))
    e21 = jnp.exp(m2 - m1)
    w1 = g_w / (1.0 + e21)
    w2 = g_w * e21 / (1.0 + e21)
    id1 = i1 - N_GROUPS
    id2 = i2 - N_GROUPS

    eid = lax.broadcasted_iota(jnp.int32, (N_EXPERTS, n), 0).astype(jnp.float32)
    sel1 = eid == id1
    sel2 = eid == id2
    onehot = (sel1 | sel2).astype(jnp.float32)
    src = lax.broadcasted_iota(jnp.int32, (n, n), 0)
    dst = lax.broadcasted_iota(jnp.int32, (n, n), 1)
    earlier = (src < dst).astype(jnp.bfloat16)
    before = (jnp.dot(onehot.astype(jnp.bfloat16), earlier, preferred_element_type=jnp.float32)
              + carry[0:N_EXPERTS, 0:1])
    rank1 = jnp.sum(jnp.where(sel1, before, 0.0), axis=0, keepdims=True)
    rank2 = jnp.sum(jnp.where(sel2, before, 0.0), axis=0, keepdims=True)
    carry[0:N_EXPERTS, :] = carry[0:N_EXPERTS, :] + jnp.sum(onehot, axis=1, keepdims=True)

    rec_t = jnp.zeros((ROUTE_ROWS, n), jnp.float32)
    rec_row = lax.broadcasted_iota(jnp.int32, (ROUTE_ROWS, n), 0)
    for k, val in ((R_ID1, id1), (R_ID2, id2), (R_W1, w1), (R_W2, w2), (R_RANK1, rank1), (R_RANK2, rank2)):
        rec_t = jnp.where(rec_row == k, val, rec_t)
    route_t_ref[:, rows] = rec_t


def _run_mixer(x2, lnin_g, lnin_b, w_in, conv_w, conv_b, pool_w, pool_scale, w_out, ln1_g, ln1_b,
               w_r, b_r, batch, seq):
    n_tok = batch * seq
    n_s = seq // SEQ_TILE
    tok_map = lambda b, s: (b * n_s + s, 0)

    def const(shape):
        return pl.BlockSpec(shape, lambda b, s: (0,) * len(shape), pipeline_mode=pl.Buffered(1))

    in_specs = [
        pl.BlockSpec((SEQ_TILE, D_MODEL), tok_map),
        const((1, D_MODEL)), const((1, D_MODEL)),
        const((D_MODEL, IN_PROJ)),
        const((1, CONV_K, CONV_WIDTH)), const((1, CONV_WIDTH)),
        const((len(POOL_WINDOWS), POOL_GW, POOL_GW)), const((1, POOL_WIDTH)),
        const((D_MODEL, D_MODEL)),
        const((1, D_MODEL)), const((1, D_MODEL)),
        const((3 * D_MODEL, LANES)), const((1, LANES)),
    ]
    out_specs = [
        pl.BlockSpec((SEQ_TILE, HALF), tok_map),
        pl.BlockSpec((SEQ_TILE, D_MODEL), tok_map),
        pl.BlockSpec((ROUTE_ROWS, SEQ_TILE), lambda b, s: (0, b * n_s + s)),
        pl.BlockSpec((1, LANES), lambda b, s: (0, 0)),
    ]
    out_shape = [
        jax.ShapeDtypeStruct((n_tok, HALF), jnp.uint32),
        jax.ShapeDtypeStruct((n_tok, D_MODEL), jnp.float32),
        jax.ShapeDtypeStruct((ROUTE_ROWS, n_tok), jnp.float32),
        jax.ShapeDtypeStruct((1, LANES), jnp.float32),
    ]
    return pl.pallas_call(
        _mixer_kernel,
        grid=(batch, n_s),
        in_specs=in_specs,
        out_specs=out_specs,
        out_shape=out_shape,
        scratch_shapes=[
            pltpu.VMEM((SEQ_TILE + CONV_HALO, CONV_WIDTH), jnp.float32),
            pltpu.VMEM((SEQ_TILE + POOL_HALO, POOL_WIDTH), jnp.float32),
            pltpu.VMEM((LANES, LANES), jnp.float32),
            pltpu.VMEM((D_MODEL, IN_PROJ), jnp.bfloat16),
            pltpu.VMEM((len(POOL_WINDOWS), POOL_GW, POOL_GW), jnp.bfloat16),
            pltpu.VMEM((D_MODEL, D_MODEL), jnp.bfloat16),
        ],
        compiler_params=pltpu.CompilerParams(
            dimension_semantics=("arbitrary", "arbitrary"), vmem_limit_bytes=VMEM_LIMIT),
    )(x2, lnin_g, lnin_b, w_in, conv_w, conv_b, pool_w, pool_scale, w_out, ln1_g, ln1_b, w_r, b_r)


def _plan_kernel(rt_ref, counts_ref, pos_ref, tile_start_ref, tiles_ref, pad_ref, *, n_rows):
    lane = lax.broadcasted_iota(jnp.int32, (ROUTE_ROWS, LANES), 1)
    counts = jnp.broadcast_to(counts_ref[...], (ROUTE_ROWS, LANES))
    tiles = jnp.floor((counts + (ROW_TILE - 1)) * (1.0 / ROW_TILE))
    tile_end = tiles
    shift = 1
    while shift < N_EXPERTS:
        tile_end = tile_end + jnp.where(lane >= shift, pltpu.roll(tile_end, shift, axis=1), 0.0)
        shift *= 2
    row_start = (tile_end - tiles) * ROW_TILE
    tile_start_ref[...] = (tile_end - tiles)[0:1, :].astype(jnp.int32)
    tiles_ref[...] = tiles[0:1, :].astype(jnp.int32)

    rt = rt_ref[...]
    ids = rt[R_ID1:R_ID2 + 1, :]
    start = jnp.zeros_like(ids)
    for e in range(N_EXPERTS):
        start = jnp.where(ids == e, row_start[0:1, e:e + 1], start)
    pos_ref[...] = (start + rt[R_RANK1:R_RANK2 + 1, :]).astype(jnp.int32)

    sub = lax.broadcasted_iota(jnp.int32, (N_EXPERTS, LANES), 0)
    lane_e = lax.broadcasted_iota(jnp.int32, (N_EXPERTS, LANES), 1)
    diag = sub == lane_e
    pad_lo = jnp.sum(jnp.where(diag, (row_start + counts)[0:1, :], 0.0), axis=1, keepdims=True)
    pad_n = jnp.sum(jnp.where(diag, (tiles * ROW_TILE - counts)[0:1, :], 0.0), axis=1, keepdims=True)
    j = lax.broadcasted_iota(jnp.int32, (N_EXPERTS, ROW_TILE), 1).astype(jnp.float32)
    pad_ref[...] = jnp.where(j < pad_n, pad_lo + j, n_rows + j).astype(jnp.int32)


def _run_plan(route_t, counts, n_rows):
    n_tok = route_t.shape[1]
    full = lambda shape: pl.BlockSpec(shape, lambda i: (0, 0))
    return pl.pallas_call(
        functools.partial(_plan_kernel, n_rows=n_rows),
        grid=(1,),
        in_specs=[full((ROUTE_ROWS, n_tok)), full((1, LANES))],
        out_specs=[full((2, n_tok)), full((1, LANES)), full((1, LANES)), full((N_EXPERTS, ROW_TILE))],
        out_shape=[
            jax.ShapeDtypeStruct((2, n_tok), jnp.int32),
            jax.ShapeDtypeStruct((1, LANES), jnp.int32),
            jax.ShapeDtypeStruct((1, LANES), jnp.int32),
            jax.ShapeDtypeStruct((N_EXPERTS, ROW_TILE), jnp.int32),
        ],
        compiler_params=pltpu.CompilerParams(dimension_semantics=("arbitrary",)),
    )(route_t, counts)


def _sc_mesh():
    return plsc.VectorSubcoreMesh(core_axis_name="core", subcore_axis_name="subcore")


def _sc_worker_id():
    return lax.axis_index("core") * (SC_WORKERS // 2) + lax.axis_index("subcore")


def _dispatch_rows(xp, pos, pad_pos, zero_rows, n_rows):
    n_tok, width = xp.shape
    second = n_tok // SC_WINDOW
    n_win = n_tok // SC_WORKERS // SC_WINDOW
    n_pad = pad_pos.size // SC_WORKERS // SC_WINDOW
    as_windows = lambda v: v.reshape(-1, SC_WINDOW)

    @functools.partial(
        pl.kernel, out_type=jax.ShapeDtypeStruct((n_rows + ROW_TILE, width), xp.dtype), mesh=_sc_mesh(),
        scratch_types=[pltpu.VMEM((n_win, SC_WINDOW), jnp.int32), pltpu.VMEM((n_win, SC_WINDOW), jnp.int32),
                       pltpu.VMEM((n_pad, SC_WINDOW), jnp.int32),
                       pltpu.VMEM((2, SC_WINDOW, width), xp.dtype), pltpu.VMEM((SC_WINDOW, width), xp.dtype),
                       pltpu.SemaphoreType.DMA((2,)), pltpu.SemaphoreType.DMA((2,)), pltpu.SemaphoreType.DMA])
    def dispatch(xp_hbm, pos_hbm, pad_hbm, zero_hbm, out_hbm,
                 idx1, idx2, idxp, buf, zbuf, lsem, ssem, psem):
        wid = _sc_worker_id()
        pltpu.sync_copy(pad_hbm.at[pl.ds(wid * n_pad, n_pad)], idxp)
        pltpu.sync_copy(zero_hbm, zbuf)
        pads = [pltpu.make_async_copy(zbuf, out_hbm.at[idxp.at[j]], psem) for j in range(n_pad)]
        for cp in pads:
            cp.start()
        pltpu.sync_copy(pos_hbm.at[pl.ds(wid * n_win, n_win)], idx1)
        pltpu.sync_copy(pos_hbm.at[pl.ds(second + wid * n_win, n_win)], idx2)

        def load(j):
            rows = xp_hbm.at[pl.ds((wid * n_win + j) * SC_WINDOW, SC_WINDOW)]
            return pltpu.make_async_copy(rows, buf.at[j % 2], lsem.at[j % 2])

        def scatters(j):
            return [pltpu.make_async_copy(buf.at[j % 2], out_hbm.at[idx.at[j]], ssem.at[j % 2])
                    for idx in (idx1, idx2)]

        load(0).start()
        for j in range(n_win):
            load(j).wait()
            for cp in scatters(j):
                cp.start()
            if j >= 1:
                for cp in scatters(j - 1):
                    cp.wait()
            if j + 1 < n_win:
                load(j + 1).start()
        for cp in scatters(n_win - 1):
            cp.wait()
        for cp in pads:
            cp.wait()

    return dispatch(xp, as_windows(pos), as_windows(pad_pos), zero_rows)


def _gather_rows(src, idx):
    n_out, width = idx.shape[0], src.shape[1]
    n_win = n_out // SC_WORKERS // GATHER_WINDOW
    nbuf = GATHER_BUFFERS

    @functools.partial(
        pl.kernel, out_type=jax.ShapeDtypeStruct((n_out, width), src.dtype), mesh=_sc_mesh(),
        scratch_types=[pltpu.VMEM((n_win, GATHER_WINDOW), jnp.int32),
                       pltpu.VMEM((nbuf, GATHER_WINDOW, width), src.dtype),
                       pltpu.SemaphoreType.DMA((nbuf,)), pltpu.SemaphoreType.DMA((nbuf,))])
    def gather(src_hbm, idx_hbm, dst_hbm, idx_v, buf, gsem, ssem):
        wid = _sc_worker_id()
        pltpu.sync_copy(idx_hbm.at[pl.ds(wid * n_win, n_win)], idx_v)

        def fetch(j):
            return pltpu.make_async_copy(src_hbm.at[idx_v.at[j]], buf.at[j % nbuf], gsem.at[j % nbuf])

        def store(j):
            rows = dst_hbm.at[pl.ds((wid * n_win + j) * GATHER_WINDOW, GATHER_WINDOW)]
            return pltpu.make_async_copy(buf.at[j % nbuf], rows, ssem.at[j % nbuf])

        for j in range(min(nbuf - 1, n_win)):
            fetch(j).start()
        for j in range(n_win):
            fetch(j).wait()
            store(j).start()
            if j + nbuf - 1 < n_win:
                if j >= 1:
                    store(j - 1).wait()
                fetch(j + nbuf - 1).start()
        for j in range(max(0, n_win - nbuf), n_win):
            store(j).wait()

    return gather(src, idx.reshape(-1, GATHER_WINDOW))


def _tile_copy(hbm, buf, sem, tile, slot, to_hbm):
    rows = hbm.at[pl.ds(pl.multiple_of(tile * ROW_TILE, ROW_TILE), ROW_TILE)]
    if to_hbm:
        return pltpu.make_async_copy(buf.at[slot], rows, sem.at[slot])
    return pltpu.make_async_copy(rows, buf.at[slot], sem.at[slot])


def _weight_copies(w_hbm, wbuf, wsem, expert):
    slot = expert % WEIGHT_BUFFERS
    return [pltpu.make_async_copy(w.at[expert], buf.at[slot], wsem.at[slot]) for w, buf in zip(w_hbm, wbuf)]


def _expert_kernel(ts_ref, nte_ref, x_hbm, wg_hbm, wu_hbm, wd_hbm, y_hbm,
                   xbuf, ybuf, xsem, ysem, wg_buf, wu_buf, wd_buf, wsem, wgu_bf, wd_bf, act_ref, *, max_tiles):
    e = pl.program_id(0)
    first = ts_ref[e]
    count = nte_ref[e]
    n_tiles = ts_ref[N_EXPERTS - 1] + nte_ref[N_EXPERTS - 1]
    ahead = TILE_BUFFERS - 2
    w_hbm = (wg_hbm, wu_hbm, wd_hbm)
    wbuf = (wg_buf, wu_buf, wd_buf)

    @pl.when(e == 0)
    def _():
        for g in range(ahead):
            @pl.when(g < n_tiles)
            def _():
                _tile_copy(x_hbm, xbuf, xsem, g, g, False).start()
        for k in range(WEIGHT_BUFFERS - 1):
            for cp in _weight_copies(w_hbm, wbuf, wsem, k):
                cp.start()

    @pl.when(e + WEIGHT_BUFFERS - 1 < N_EXPERTS)
    def _():
        for cp in _weight_copies(w_hbm, wbuf, wsem, e + WEIGHT_BUFFERS - 1):
            cp.start()

    for cp in _weight_copies(w_hbm, wbuf, wsem, e):
        cp.wait()

    def sync_x(g):
        _tile_copy(x_hbm, xbuf, xsem, g, g % TILE_BUFFERS, False).wait()

        @pl.when(g + ahead < n_tiles)
        def _():
            _tile_copy(x_hbm, xbuf, xsem, g + ahead, (g + ahead) % TILE_BUFFERS, False).start()

    def sync_y_slot(g):
        @pl.when(g >= TILE_BUFFERS)
        def _():
            _tile_copy(y_hbm, ybuf, ysem, g - TILE_BUFFERS, g % TILE_BUFFERS, True).wait()

    def up_proj(g):
        lo, hi = _unpack_bf16_pairs(xbuf[g % TILE_BUFFERS])
        return (jnp.dot(lo.astype(jnp.bfloat16), wgu_bf[0:HALF, :], preferred_element_type=jnp.float32)
                + jnp.dot(hi.astype(jnp.bfloat16), wgu_bf[HALF:, :], preferred_element_type=jnp.float32))

    def put_act(hgu, which):
        hg = hgu[:, 0:D_EXPERT]
        act_ref[which] = (hg * jax.nn.sigmoid(hg) * hgu[:, D_EXPERT:]).astype(jnp.bfloat16)

    def down_proj(g, which):
        y = jnp.dot(act_ref[which], wd_bf[...], preferred_element_type=jnp.float32)
        ybuf[g % TILE_BUFFERS] = _pack_bf16_pairs(y)

    def start_y(g):
        _tile_copy(y_hbm, ybuf, ysem, g, g % TILE_BUFFERS, True).start()

    @pl.when(count > 0)
    def _():
        slot = e % WEIGHT_BUFFERS
        wgu_bf[:, 0:D_EXPERT] = wg_buf[slot].astype(jnp.bfloat16)
        wgu_bf[:, D_EXPERT:] = wu_buf[slot].astype(jnp.bfloat16)
        wd_bf[...] = wd_buf[slot].astype(jnp.bfloat16)

        sync_x(first)
        put_act(up_proj(first), 0)

        def pair_body(i, c):
            g = first + 1 + 2 * i
            sync_x(g)
            sync_x(g + 1)
            sync_y_slot(g - 1)
            sync_y_slot(g)
            down_proj(g - 1, 0)
            hgu0 = up_proj(g)
            put_act(hgu0, 1)
            hgu1 = up_proj(g + 1)
            down_proj(g, 1)
            put_act(hgu1, 0)
            start_y(g - 1)
            start_y(g)
            return c

        n_pairs = (count - 1) // 2
        lax.fori_loop(0, n_pairs, pair_body, 0)

        @pl.when((count - 1) % 2 == 1)
        def _():
            g = first + count - 1
            sync_x(g)
            sync_y_slot(g - 1)
            down_proj(g - 1, 0)
            put_act(up_proj(g), 0)
            start_y(g - 1)

        last = first + count - 1
        sync_y_slot(last)
        down_proj(last, 0)
        start_y(last)

    @pl.when(e == N_EXPERTS - 1)
    def _():
        for k in range(TILE_BUFFERS, 0, -1):
            @pl.when(n_tiles >= k)
            def _():
                _tile_copy(y_hbm, ybuf, ysem, n_tiles - k, (n_tiles - k) % TILE_BUFFERS, True).wait()

        ybuf[0] = jnp.zeros((ROW_TILE, HALF), jnp.uint32)

        def fill(g, c):
            cp = _tile_copy(y_hbm, ybuf, ysem, g, 0, True)
            cp.start()
            cp.wait()
            return c

        lax.fori_loop(n_tiles, max_tiles, fill, 0)


def _run_experts(tile_start, tiles_per_expert, x_sorted, w_gate, w_up, w_down):
    n_rows = x_sorted.shape[0]
    max_tiles = n_rows // ROW_TILE
    hbm = pl.BlockSpec(memory_space=pl.ANY)

    grid_spec = pltpu.PrefetchScalarGridSpec(
        num_scalar_prefetch=2,
        grid=(N_EXPERTS,),
        in_specs=[hbm, hbm, hbm, hbm],
        out_specs=hbm,
        scratch_shapes=[
            pltpu.VMEM((TILE_BUFFERS, ROW_TILE, HALF), jnp.uint32),
            pltpu.VMEM((TILE_BUFFERS, ROW_TILE, HALF), jnp.uint32),
            pltpu.SemaphoreType.DMA((TILE_BUFFERS,)),
            pltpu.SemaphoreType.DMA((TILE_BUFFERS,)),
            pltpu.VMEM((WEIGHT_BUFFERS, D_MODEL, D_EXPERT), jnp.float32),
            pltpu.VMEM((WEIGHT_BUFFERS, D_MODEL, D_EXPERT), jnp.float32),
            pltpu.VMEM((WEIGHT_BUFFERS, D_EXPERT, D_MODEL), jnp.float32),
            pltpu.SemaphoreType.DMA((WEIGHT_BUFFERS,)),
            pltpu.VMEM((D_MODEL, 2 * D_EXPERT), jnp.bfloat16),
            pltpu.VMEM((D_EXPERT, D_MODEL), jnp.bfloat16),
            pltpu.VMEM((2, ROW_TILE, D_EXPERT), jnp.bfloat16),
        ],
    )
    return pl.pallas_call(
        functools.partial(_expert_kernel, max_tiles=max_tiles),
        grid_spec=grid_spec,
        out_shape=jax.ShapeDtypeStruct((n_rows, HALF), jnp.uint32),
        compiler_params=pltpu.CompilerParams(
            dimension_semantics=("arbitrary",), vmem_limit_bytes=VMEM_LIMIT),
    )(tile_start, tiles_per_expert, x_sorted, w_gate, w_up, w_down)


def _combine_rows(r0, h1_ref, p_ref, route_t_ref, ya_ref, yb_ref, w_pg, b_pg, w_ple, g_ref, b_ref, o_ref):
    rows = pl.ds(r0, SUB_TILE)
    h_hi = h1_ref[rows, :].astype(jnp.bfloat16)
    p_b = p_ref[rows, :].astype(jnp.bfloat16)
    rec_t = route_t_ref[:, rows]
    padded = jnp.concatenate([rec_t, jnp.zeros((LANES - ROUTE_ROWS, SUB_TILE), jnp.float32)], axis=0)
    route = jnp.transpose(padded)
    yield
    gate_pre = jnp.dot(h_hi, w_pg[...], preferred_element_type=jnp.float32)
    ple_pre = jnp.dot(p_b, w_ple[...], preferred_element_type=jnp.float32)
    yield
    ple = ple_pre * jax.nn.sigmoid(gate_pre + b_pg[...])
    w1 = route[:, R_W1:R_W1 + 1]
    w2 = route[:, R_W2:R_W2 + 1]
    a_lo, a_hi = _unpack_bf16_pairs(ya_ref[rows, :])
    b_lo, b_hi = _unpack_bf16_pairs(yb_ref[rows, :])
    moe = jnp.concatenate([w1 * a_lo + w2 * b_lo, w1 * a_hi + w2 * b_hi], axis=-1)
    o_ref[rows, :] = _layernorm(DEEPNORM_ALPHA * h1_ref[rows, :] + ple + moe, g_ref[...], b_ref[...])


def _combine_kernel(h1_ref, p_ref, route_t_ref, ya_ref, yb_ref, w_pg, b_pg, w_ple, g_ref, b_ref, o_ref):
    chains = [_combine_rows(r0, h1_ref, p_ref, route_t_ref, ya_ref, yb_ref, w_pg, b_pg, w_ple, g_ref, b_ref, o_ref)
              for r0 in range(0, h1_ref.shape[0], SUB_TILE)]
    for t in range(COMBINE_PHASES + len(chains) - 1):
        for k, chain in reversed(list(enumerate(chains))):
            if 0 <= t - k < COMBINE_PHASES:
                next(chain, None)


def _run_combine(h1, p2, route_t, y_tok, w_pg, b_pg, w_ple, ln2_g, ln2_b):
    n_tok = h1.shape[0]
    n_t = n_tok // TOKEN_TILE
    tok_map = lambda i: (i, 0)
    const = lambda shape: pl.BlockSpec(shape, lambda i: (0, 0), pipeline_mode=pl.Buffered(1))
    in_specs = [
        pl.BlockSpec((TOKEN_TILE, D_MODEL), tok_map),
        pl.BlockSpec((TOKEN_TILE, PLE_DIM), tok_map),
        pl.BlockSpec((ROUTE_ROWS, TOKEN_TILE), lambda i: (0, i)),
        pl.BlockSpec((TOKEN_TILE, HALF), lambda i: (i, 0)),
        pl.BlockSpec((TOKEN_TILE, HALF), lambda i: (i + n_t, 0)),
        const((D_MODEL, D_MODEL)), const((1, D_MODEL)), const((PLE_DIM, D_MODEL)),
        const((1, D_MODEL)), const((1, D_MODEL)),
    ]
    return pl.pallas_call(
        _combine_kernel,
        grid=(n_t,),
        in_specs=in_specs,
        out_specs=pl.BlockSpec((TOKEN_TILE, D_MODEL), tok_map),
        out_shape=jax.ShapeDtypeStruct((n_tok, D_MODEL), jnp.float32),
        compiler_params=pltpu.CompilerParams(dimension_semantics=("arbitrary",), vmem_limit_bytes=VMEM_LIMIT),
    )(h1, p2, route_t, y_tok, y_tok, w_pg, b_pg, w_ple, ln2_g, ln2_b)


def _split_bf16(w):
    hi = w.astype(jnp.bfloat16)
    lo = (w - hi.astype(jnp.float32)).astype(jnp.bfloat16)
    return hi, lo


def kernel(x, p, ln_in_g, ln_in_b, w_in, conv_w, conv_b, pool_w, pool_scale, w_out, ln1_g, ln1_b,
           w_rg, b_rg, w_re, b_re, w_gate, w_up, w_down, w_pg, b_pg, w_ple, ln2_g, ln2_b):
    batch, seq, _ = x.shape
    n_tok = batch * seq
    bf = jnp.bfloat16
    row = lambda v: v.reshape(1, -1)

    w_r = jnp.concatenate([w_rg[0], jnp.transpose(w_re[0], (1, 0, 2)).reshape(D_MODEL, N_EXPERTS)], axis=1)
    w_r = jnp.pad(w_r, ((0, 0), (0, LANES - w_r.shape[1])))
    w_r_hi, w_r_lo = _split_bf16(w_r)
    w_r_cat = jnp.concatenate([w_r_hi, w_r_hi, w_r_lo], axis=0)
    b_r = jnp.pad(jnp.concatenate([b_rg[0], b_re[0].reshape(-1)]), (0, LANES - N_GROUPS - N_EXPERTS)).reshape(1, LANES)

    x2 = x.reshape(n_tok, D_MODEL)
    p2 = p[0].reshape(n_tok, PLE_DIM)
    mixer_weights = (row(ln_in_g), row(ln_in_b), w_in[0], conv_w, row(conv_b[0]),
                     pool_w[0], row(pool_scale[0]), w_out[0], row(ln1_g[0]), row(ln1_b[0]),
                     w_r_cat, b_r)
    combine_weights = (w_pg[0].astype(bf), row(b_pg[0]), w_ple[0].astype(bf), row(ln2_g[0]), row(ln2_b[0]))
    expert_weights = (w_gate[0].reshape(N_EXPERTS, D_MODEL, D_EXPERT),
                      w_up[0].reshape(N_EXPERTS, D_MODEL, D_EXPERT),
                      w_down[0].reshape(N_EXPERTS, D_EXPERT, D_MODEL))
    zero_rows = jnp.zeros((SC_WINDOW, HALF), jnp.uint32)

    n_rows = -(-(2 * n_tok + N_EXPERTS * (ROW_TILE - 1)) // ROW_TILE) * ROW_TILE
    xp, h1, route_t, counts = _run_mixer(x2, *mixer_weights, batch, seq)
    pos, tile_start, tiles_per_expert, pad_pos = _run_plan(route_t, counts, n_rows)
    x_sorted = _dispatch_rows(xp, pos, pad_pos, zero_rows, n_rows)
    y_sorted = _run_experts(tile_start[0, :N_EXPERTS], tiles_per_expert[0, :N_EXPERTS], x_sorted, *expert_weights)

    y_tok = _gather_rows(y_sorted, pos.reshape(-1))
    out = _run_combine(h1, p2, route_t, y_tok, *combine_weights)
    return out.reshape(batch, seq, D_MODEL)
```

```python
import functools

import jax
import jax.numpy as jnp
from jax import lax
from jax.experimental import pallas as pl
from jax.experimental.pallas import tpu as pltpu
from jax.experimental.pallas import tpu_sc as plsc

D_MODEL = 1024
CONV_WIDTH = 512
CONV_K = 3
POOL_WIDTH = 512
POOL_WINDOWS = (2, 4, 8, 16)
POOL_GW = 128
IN_PROJ = 3 * CONV_WIDTH + POOL_WIDTH
N_GROUPS = 4
EXPERTS_PER_GROUP = 8
N_EXPERTS = N_GROUPS * EXPERTS_PER_GROUP
D_EXPERT = 256
PLE_DIM = 256
LN_EPS = 1e-5
DEEPNORM_ALPHA = 2.0 ** 0.25

LANES = 128
HALF = D_MODEL // 2
CONV_HALO = 8
POOL_HALO = 16
SEQ_TILE = 1024
SUB_TILE = 256
MIXER_PHASES = 8
COMBINE_PHASES = 3
ROW_TILE = 256
TILE_BUFFERS = 8
SC_WORKERS = 32
SC_WINDOW = 64
GATHER_WINDOW = 64
GATHER_BUFFERS = 3
WEIGHT_BUFFERS = 3
TOKEN_TILE = 1024
VMEM_LIMIT = 56 * 1024 * 1024

R_ID1, R_ID2, R_RANK1, R_RANK2, R_W1, R_W2 = range(6)
ROUTE_ROWS = 8
LOGIT_ROWS = 40


def _layernorm(x, g, b):
    mu = jnp.mean(x, axis=-1, keepdims=True)
    xc = x - mu
    var = jnp.mean(xc * xc, axis=-1, keepdims=True)
    return xc * lax.rsqrt(var + LN_EPS) * g + b


def _pack_bf16_pairs(v):
    bits = lax.bitcast_convert_type(v.astype(jnp.bfloat16).astype(jnp.float32), jnp.uint32)
    return bits[:, HALF:] | (bits[:, :HALF] >> 16)


def _unpack_bf16_pairs(w):
    lo = lax.bitcast_convert_type(w << 16, jnp.float32)
    hi = lax.bitcast_convert_type(w & jnp.uint32(0xFFFF0000), jnp.float32)
    return lo, hi


def _mixer_kernel(x_ref, lnin_g, lnin_b, w_in_f32, conv_w, conv_b, pool_w_f32, pool_scale, w_out_f32,
                  ln1_g, ln1_b, w_r, b_r,
                  xp_ref, h1_ref, route_t_ref, counts_ref,
                  zbuf, vbuf, carry, w_in, pool_w, w_out):
    b = pl.program_id(0)
    s = pl.program_id(1)
    ts = x_ref.shape[0]

    @pl.when(s == 0)
    def _():
        zbuf[0:CONV_HALO, :] = jnp.zeros((CONV_HALO, CONV_WIDTH), jnp.float32)
        vbuf[0:POOL_HALO, :] = jnp.zeros((POOL_HALO, POOL_WIDTH), jnp.float32)

    @pl.when((b == 0) & (s == 0))
    def _():
        carry[...] = jnp.zeros_like(carry)
        w_in[:, 0:2 * CONV_WIDTH] = w_in_f32[:, CONV_WIDTH:3 * CONV_WIDTH].astype(jnp.bfloat16)
        w_in[:, 2 * CONV_WIDTH:3 * CONV_WIDTH] = w_in_f32[:, 0:CONV_WIDTH].astype(jnp.bfloat16)
        w_in[:, 3 * CONV_WIDTH:] = w_in_f32[:, 3 * CONV_WIDTH:].astype(jnp.bfloat16)
        pool_w[...] = pool_w_f32[...].astype(jnp.bfloat16)
        w_out[...] = w_out_f32[...].astype(jnp.bfloat16)

    chains = [_mixer_rows(r0, s * ts + r0, x_ref, lnin_g, lnin_b, w_in, conv_w, conv_b, pool_w, pool_scale,
                          w_out, ln1_g, ln1_b, w_r, b_r,
                          xp_ref, h1_ref, route_t_ref, zbuf, vbuf, carry)
              for r0 in range(0, ts, SUB_TILE)]
    for t in range(MIXER_PHASES + len(chains) - 1):
        for k, chain in reversed(list(enumerate(chains))):
            if 0 <= t - k < MIXER_PHASES:
                next(chain, None)
    zbuf[0:CONV_HALO, :] = zbuf[ts:ts + CONV_HALO, :]
    vbuf[0:POOL_HALO, :] = vbuf[ts:ts + POOL_HALO, :]
    counts_ref[...] = jnp.transpose(carry[...])[0:1, :]


def _mixer_rows(r0, seq0, x_ref, lnin_g, lnin_b, w_in, conv_w, conv_b, pool_w, pool_scale, w_out,
                ln1_g, ln1_b, w_r, b_r,
                xp_ref, h1_ref, route_t_ref, zbuf, vbuf, carry):
    n = SUB_TILE
    rows = pl.ds(r0, n)
    h0 = _layernorm(x_ref[rows, :], lnin_g[...], lnin_b[...])
    h0b = h0.astype(jnp.bfloat16)
    yield
    u_a = jnp.dot(h0b, w_in[:, 0:2 * CONV_WIDTH], preferred_element_type=jnp.float32)
    zbuf[pl.ds(CONV_HALO + r0, n), :] = u_a[:, 0:CONV_WIDTH] * u_a[:, CONV_WIDTH:]
    yield
    u_b = jnp.dot(h0b, w_in[:, 2 * CONV_WIDTH:], preferred_element_type=jnp.float32)
    b_g = u_b[:, 0:CONV_WIDTH]
    vbuf[pl.ds(POOL_HALO + r0, n), :] = u_b[:, CONV_WIDTH:]
    yield

    zext = zbuf[pl.ds(r0, n + CONV_HALO), :]
    z1 = pltpu.roll(zext, 1, axis=0)[CONV_HALO:, :]
    z2 = pltpu.roll(zext, 2, axis=0)[CONV_HALO:, :]
    conv = z2 * conv_w[0, 0:1, :] + z1 * conv_w[0, 1:2, :] + zext[CONV_HALO:, :] * conv_w[0, 2:3, :] + conv_b[...]
    y_conv = b_g * conv

    vext = vbuf[pl.ds(r0, n + POOL_HALO), :]
    v_p = vext[POOL_HALO:, :]
    s2 = vext + pltpu.roll(vext, 1, axis=0)
    s4 = s2[:, POOL_GW:] + pltpu.roll(s2[:, POOL_GW:], 2, axis=0)
    s8 = s4[:, POOL_GW:] + pltpu.roll(s4[:, POOL_GW:], 4, axis=0)
    s16 = s8[:, POOL_GW:] + pltpu.roll(s8[:, POOL_GW:], 8, axis=0)
    wsums = (s2[POOL_HALO:, 0:POOL_GW], s4[POOL_HALO:, 0:POOL_GW],
             s8[POOL_HALO:, 0:POOL_GW], s16[POOL_HALO:, 0:POOL_GW])

    t_pos = seq0 + lax.broadcasted_iota(jnp.int32, (n, 1), 0)
    y_pool = []
    for j, w in enumerate(POOL_WINDOWS):
        inv_cnt = 1.0 / jnp.minimum(t_pos + 1, w).astype(jnp.float32)
        pooled = wsums[j] * inv_cnt - v_p[:, j * POOL_GW:(j + 1) * POOL_GW]
        y_pool.append(jnp.dot(pooled.astype(jnp.bfloat16), pool_w[j], preferred_element_type=jnp.float32))
    y_pool = jnp.concatenate(y_pool, axis=-1) * pool_scale[...]

    ycat = jnp.concatenate([y_conv, y_pool], axis=-1).astype(jnp.bfloat16)
    yield
    mix = jnp.dot(ycat, w_out[...], preferred_element_type=jnp.float32)
    yield
    h1 = _layernorm(DEEPNORM_ALPHA * h0 + mix, ln1_g[...], ln1_b[...])

    h_hi = h1.astype(jnp.bfloat16)
    h_lo = (h1 - h_hi.astype(jnp.float32)).astype(jnp.bfloat16)
    xp_ref[rows, :] = _pack_bf16_pairs(h1)
    h1_ref[rows, :] = h1
    hcat = jnp.concatenate([h_hi, h_lo, h_hi], axis=-1)
    yield
    logits = jnp.dot(hcat, w_r[...], preferred_element_type=jnp.float32) + b_r[...]
    yield
    lt = jnp.transpose(logits)[0:LOGIT_ROWS, :]
    rid = lax.broadcasted_iota(jnp.int32, (LOGIT_ROWS, n), 0).astype(jnp.float32)
    neg = jnp.float32(-jnp.inf)

    def first_argmax(vals):
        m = jnp.max(vals, axis=0, keepdims=True)
        idx = jnp.min(jnp.where(vals == m, rid, float(LOGIT_ROWS)), axis=0, keepdims=True)
        return m, idx

    g_mask = rid < N_GROUPS
    g_max, g_idx = first_argmax(jnp.where(g_mask, lt, neg))
    g_w = 1.0 / jnp.sum(jnp.where(g_mask, jnp.exp(lt - g_max), 0.0), axis=0, keepdims=True)

    e_lo = N_GROUPS + EXPERTS_PER_GROUP * g_idx
    e_vals = jnp.where((rid >= e_lo) & (rid < e_lo + EXPERTS_PER_GROUP), lt, neg)
    m1, i1 = first_argmax(e_vals)
    m2, i2 = first_argmax(jnp.where(rid == i1, neg, e_vals))
    e21 = jnp.exp(m2 - m1)
    w1 = g_w / (1.0 + e21)
    w2 = g_w * e21 / (1.0 + e21)
    id1 = i1 - N_GROUPS
    id2 = i2 - N_GROUPS

    eid = lax.broadcasted_iota(jnp.int32, (N_EXPERTS, n), 0).astype(jnp.float32)
    sel1 = eid == id1
    sel2 = eid == id2
    onehot = (sel1 | sel2).astype(jnp.float32)
    src = lax.broadcasted_iota(jnp.int32, (n, n), 0)
    dst = lax.broadcasted_iota(jnp.int32, (n, n), 1)
    earlier = (src < dst).astype(jnp.bfloat16)
    before = (jnp.dot(onehot.astype(jnp.bfloat16), earlier, preferred_element_type=jnp.float32)
              + carry[0:N_EXPERTS, 0:1])
    rank1 = jnp.sum(jnp.where(sel1, before, 0.0), axis=0, keepdims=True)
    rank2 = jnp.sum(jnp.where(sel2, before, 0.0), axis=0, keepdims=True)
    carry[0:N_EXPERTS, :] = carry[0:N_EXPERTS, :] + jnp.sum(onehot, axis=1, keepdims=True)

    rec_t = jnp.zeros((ROUTE_ROWS, n), jnp.float32)
    rec_row = lax.broadcasted_iota(jnp.int32, (ROUTE_ROWS, n), 0)
    for k, val in ((R_ID1, id1), (R_ID2, id2), (R_W1, w1), (R_W2, w2), (R_RANK1, rank1), (R_RANK2, rank2)):
        rec_t = jnp.where(rec_row == k, val, rec_t)
    route_t_ref[:, rows] = rec_t


def _run_mixer(x2, lnin_g, lnin_b, w_in, conv_w, conv_b, pool_w, pool_scale, w_out, ln1_g, ln1_b,
               w_r, b_r, batch, seq):
    n_tok = batch * seq
    n_s = seq // SEQ_TILE
    tok_map = lambda b, s: (b * n_s + s, 0)

    def const(shape):
        return pl.BlockSpec(shape, lambda b, s: (0,) * len(shape), pipeline_mode=pl.Buffered(1))

    in_specs = [
        pl.BlockSpec((SEQ_TILE, D_MODEL), tok_map),
        const((1, D_MODEL)), const((1, D_MODEL)),
        const((D_MODEL, IN_PROJ)),
        const((1, CONV_K, CONV_WIDTH)), const((1, CONV_WIDTH)),
        const((len(POOL_WINDOWS), POOL_GW, POOL_GW)), const((1, POOL_WIDTH)),
        const((D_MODEL, D_MODEL)),
        const((1, D_MODEL)), const((1, D_MODEL)),
        const((3 * D_MODEL, LANES)), const((1, LANES)),
    ]
    out_specs = [
        pl.BlockSpec((SEQ_TILE, HALF), tok_map),
        pl.BlockSpec((SEQ_TILE, D_MODEL), tok_map),
        pl.BlockSpec((ROUTE_ROWS, SEQ_TILE), lambda b, s: (0, b * n_s + s)),
        pl.BlockSpec((1, LANES), lambda b, s: (0, 0)),
    ]
    out_shape = [
        jax.ShapeDtypeStruct((n_tok, HALF), jnp.uint32),
        jax.ShapeDtypeStruct((n_tok, D_MODEL), jnp.float32),
        jax.ShapeDtypeStruct((ROUTE_ROWS, n_tok), jnp.float32),
        jax.ShapeDtypeStruct((1, LANES), jnp.float32),
    ]
    return pl.pallas_call(
        _mixer_kernel,
        grid=(batch, n_s),
        in_specs=in_specs,
        out_specs=out_specs,
        out_shape=out_shape,
        scratch_shapes=[
            pltpu.VMEM((SEQ_TILE + CONV_HALO, CONV_WIDTH), jnp.float32),
            pltpu.VMEM((SEQ_TILE + POOL_HALO, POOL_WIDTH), jnp.float32),
            pltpu.VMEM((LANES, LANES), jnp.float32),
            pltpu.VMEM((D_MODEL, IN_PROJ), jnp.bfloat16),
            pltpu.VMEM((len(POOL_WINDOWS), POOL_GW, POOL_GW), jnp.bfloat16),
            pltpu.VMEM((D_MODEL, D_MODEL), jnp.bfloat16),
        ],
        compiler_params=pltpu.CompilerParams(
            dimension_semantics=("arbitrary", "arbitrary"), vmem_limit_bytes=VMEM_LIMIT),
    )(x2, lnin_g, lnin_b, w_in, conv_w, conv_b, pool_w, pool_scale, w_out, ln1_g, ln1_b, w_r, b_r)


def _plan_kernel(rt_ref, counts_ref, pos_ref, tile_start_ref, tiles_ref, pad_ref, *, n_rows):
    lane = lax.broadcasted_iota(jnp.int32, (ROUTE_ROWS, LANES), 1)
    counts = jnp.broadcast_to(counts_ref[...], (ROUTE_ROWS, LANES))
    tiles = jnp.floor((counts + (ROW_TILE - 1)) * (1.0 / ROW_TILE))
    tile_end = tiles
    shift = 1
    while shift < N_EXPERTS:
        tile_end = tile_end + jnp.where(lane >= shift, pltpu.roll(tile_end, shift, axis=1), 0.0)
        shift *= 2
    row_start = (tile_end - tiles) * ROW_TILE
    tile_start_ref[...] = (tile_end - tiles)[0:1, :].astype(jnp.int32)
    tiles_ref[...] = tiles[0:1, :].astype(jnp.int32)

    rt = rt_ref[...]
    ids = rt[R_ID1:R_ID2 + 1, :]
    start = jnp.zeros_like(ids)
    for e in range(N_EXPERTS):
        start = jnp.where(ids == e, row_start[0:1, e:e + 1], start)
    pos_ref[...] = (start + rt[R_RANK1:R_RANK2 + 1, :]).astype(jnp.int32)

    sub = lax.broadcasted_iota(jnp.int32, (N_EXPERTS, LANES), 0)
    lane_e = lax.broadcasted_iota(jnp.int32, (N_EXPERTS, LANES), 1)
    diag = sub == lane_e
    pad_lo = jnp.sum(jnp.where(diag, (row_start + counts)[0:1, :], 0.0), axis=1, keepdims=True)
    pad_n = jnp.sum(jnp.where(diag, (tiles * ROW_TILE - counts)[0:1, :], 0.0), axis=1, keepdims=True)
    j = lax.broadcasted_iota(jnp.int32, (N_EXPERTS, ROW_TILE), 1).astype(jnp.float32)
    pad_ref[...] = jnp.where(j < pad_n, pad_lo + j, n_rows + j).astype(jnp.int32)


def _run_plan(route_t, counts, n_rows):
    n_tok = route_t.shape[1]
    full = lambda shape: pl.BlockSpec(shape, lambda i: (0, 0))
    return pl.pallas_call(
        functools.partial(_plan_kernel, n_rows=n_rows),
        grid=(1,),
        in_specs=[full((ROUTE_ROWS, n_tok)), full((1, LANES))],
        out_specs=[full((2, n_tok)), full((1, LANES)), full((1, LANES)), full((N_EXPERTS, ROW_TILE))],
        out_shape=[
            jax.ShapeDtypeStruct((2, n_tok), jnp.int32),
            jax.ShapeDtypeStruct((1, LANES), jnp.int32),
            jax.ShapeDtypeStruct((1, LANES), jnp.int32),
            jax.ShapeDtypeStruct((N_EXPERTS, ROW_TILE), jnp.int32),
        ],
        compiler_params=pltpu.CompilerParams(dimension_semantics=("arbitrary",)),
    )(route_t, counts)


def _sc_mesh():
    return plsc.VectorSubcoreMesh(core_axis_name="core", subcore_axis_name="subcore")


def _sc_worker_id():
    return lax.axis_index("core") * (SC_WORKERS // 2) + lax.axis_index("subcore")


def _dispatch_rows(xp, pos, pad_pos, zero_rows, n_rows):
    n_tok, width = xp.shape
    second = n_tok // SC_WINDOW
    n_win = n_tok // SC_WORKERS // SC_WINDOW
    n_pad = pad_pos.size // SC_WORKERS // SC_WINDOW
    as_windows = lambda v: v.reshape(-1, SC_WINDOW)

    @functools.partial(
        pl.kernel, out_type=jax.ShapeDtypeStruct((n_rows + ROW_TILE, width), xp.dtype), mesh=_sc_mesh(),
        scratch_types=[pltpu.VMEM((n_win, SC_WINDOW), jnp.int32), pltpu.VMEM((n_win, SC_WINDOW), jnp.int32),
                       pltpu.VMEM((n_pad, SC_WINDOW), jnp.int32),
                       pltpu.VMEM((2, SC_WINDOW, width), xp.dtype), pltpu.VMEM((SC_WINDOW, width), xp.dtype),
                       pltpu.SemaphoreType.DMA((2,)), pltpu.SemaphoreType.DMA((2,)), pltpu.SemaphoreType.DMA])
    def dispatch(xp_hbm, pos_hbm, pad_hbm, zero_hbm, out_hbm,
                 idx1, idx2, idxp, buf, zbuf, lsem, ssem, psem):
        wid = _sc_worker_id()
        pltpu.sync_copy(pad_hbm.at[pl.ds(wid * n_pad, n_pad)], idxp)
        pltpu.sync_copy(zero_hbm, zbuf)
        pads = [pltpu.make_async_copy(zbuf, out_hbm.at[idxp.at[j]], psem) for j in range(n_pad)]
        for cp in pads:
            cp.start()
        pltpu.sync_copy(pos_hbm.at[pl.ds(wid * n_win, n_win)], idx1)
        pltpu.sync_copy(pos_hbm.at[pl.ds(second + wid * n_win, n_win)], idx2)

        def load(j):
            rows = xp_hbm.at[pl.ds((wid * n_win + j) * SC_WINDOW, SC_WINDOW)]
            return pltpu.make_async_copy(rows, buf.at[j % 2], lsem.at[j % 2])

        def scatters(j):
            return [pltpu.make_async_copy(buf.at[j % 2], out_hbm.at[idx.at[j]], ssem.at[j % 2])
                    for idx in (idx1, idx2)]

        load(0).start()
        for j in range(n_win):
            load(j).wait()
            for cp in scatters(j):
                cp.start()
            if j >= 1:
                for cp in scatters(j - 1):
                    cp.wait()
            if j + 1 < n_win:
                load(j + 1).start()
        for cp in scatters(n_win - 1):
            cp.wait()
        for cp in pads:
            cp.wait()

    return dispatch(xp, as_windows(pos), as_windows(pad_pos), zero_rows)


def _gather_rows(src, idx):
    n_out, width = idx.shape[0], src.shape[1]
    n_win = n_out // SC_WORKERS // GATHER_WINDOW
    nbuf = GATHER_BUFFERS

    @functools.partial(
        pl.kernel, out_type=jax.ShapeDtypeStruct((n_out, width), src.dtype), mesh=_sc_mesh(),
        scratch_types=[pltpu.VMEM((n_win, GATHER_WINDOW), jnp.int32),
                       pltpu.VMEM((nbuf, GATHER_WINDOW, width), src.dtype),
                       pltpu.SemaphoreType.DMA((nbuf,)), pltpu.SemaphoreType.DMA((nbuf,))])
    def gather(src_hbm, idx_hbm, dst_hbm, idx_v, buf, gsem, ssem):
        wid = _sc_worker_id()
        pltpu.sync_copy(idx_hbm.at[pl.ds(wid * n_win, n_win)], idx_v)

        def fetch(j):
            return pltpu.make_async_copy(src_hbm.at[idx_v.at[j]], buf.at[j % nbuf], gsem.at[j % nbuf])

        def store(j):
            rows = dst_hbm.at[pl.ds((wid * n_win + j) * GATHER_WINDOW, GATHER_WINDOW)]
            return pltpu.make_async_copy(buf.at[j % nbuf], rows, ssem.at[j % nbuf])

        for j in range(min(nbuf - 1, n_win)):
            fetch(j).start()
        for j in range(n_win):
            fetch(j).wait()
            store(j).start()
            if j + nbuf - 1 < n_win:
                if j >= 1:
                    store(j - 1).wait()
                fetch(j + nbuf - 1).start()
        for j in range(max(0, n_win - nbuf), n_win):
            store(j).wait()

    return gather(src, idx.reshape(-1, GATHER_WINDOW))


def _tile_copy(hbm, buf, sem, tile, slot, to_hbm):
    rows = hbm.at[pl.ds(pl.multiple_of(tile * ROW_TILE, ROW_TILE), ROW_TILE)]
    if to_hbm:
        return pltpu.make_async_copy(buf.at[slot], rows, sem.at[slot])
    return pltpu.make_async_copy(rows, buf.at[slot], sem.at[slot])


def _weight_copies(w_hbm, wbuf, wsem, expert):
    slot = expert % WEIGHT_BUFFERS
    return [pltpu.make_async_copy(w.at[expert], buf.at[slot], wsem.at[slot]) for w, buf in zip(w_hbm, wbuf)]


def _expert_kernel(ts_ref, nte_ref, x_hbm, wg_hbm, wu_hbm, wd_hbm, y_hbm,
                   xbuf, ybuf, xsem, ysem, wg_buf, wu_buf, wd_buf, wsem, wgu_bf, wd_bf, act_ref, *, max_tiles):
    e = pl.program_id(0)
    first = ts_ref[e]
    count = nte_ref[e]
    n_tiles = ts_ref[N_EXPERTS - 1] + nte_ref[N_EXPERTS - 1]
    ahead = TILE_BUFFERS - 2
    w_hbm = (wg_hbm, wu_hbm, wd_hbm)
    wbuf = (wg_buf, wu_buf, wd_buf)

    @pl.when(e == 0)
    def _():
        for g in range(ahead):
            @pl.when(g < n_tiles)
            def _():
                _tile_copy(x_hbm, xbuf, xsem, g, g, False).start()
        for k in range(WEIGHT_BUFFERS - 1):
            for cp in _weight_copies(w_hbm, wbuf, wsem, k):
                cp.start()

    @pl.when(e + WEIGHT_BUFFERS - 1 < N_EXPERTS)
    def _():
        for cp in _weight_copies(w_hbm, wbuf, wsem, e + WEIGHT_BUFFERS - 1):
            cp.start()

    for cp in _weight_copies(w_hbm, wbuf, wsem, e):
        cp.wait()

    def sync_x(g):
        _tile_copy(x_hbm, xbuf, xsem, g, g % TILE_BUFFERS, False).wait()

        @pl.when(g + ahead < n_tiles)
        def _():
            _tile_copy(x_hbm, xbuf, xsem, g + ahead, (g + ahead) % TILE_BUFFERS, False).start()

    def sync_y_slot(g):
        @pl.when(g >= TILE_BUFFERS)
        def _():
            _tile_copy(y_hbm, ybuf, ysem, g - TILE_BUFFERS, g % TILE_BUFFERS, True).wait()

    def up_proj(g):
        lo, hi = _unpack_bf16_pairs(xbuf[g % TILE_BUFFERS])
        return (jnp.dot(lo.astype(jnp.bfloat16), wgu_bf[0:HALF, :], preferred_element_type=jnp.float32)
                + jnp.dot(hi.astype(jnp.bfloat16), wgu_bf[HALF:, :], preferred_element_type=jnp.float32))

    def put_act(hgu, which):
        hg = hgu[:, 0:D_EXPERT]
        act_ref[which] = (hg * jax.nn.sigmoid(hg) * hgu[:, D_EXPERT:]).astype(jnp.bfloat16)

    def down_proj(g, which):
        y = jnp.dot(act_ref[which], wd_bf[...], preferred_element_type=jnp.float32)
        ybuf[g % TILE_BUFFERS] = _pack_bf16_pairs(y)

    def start_y(g):
        _tile_copy(y_hbm, ybuf, ysem, g, g % TILE_BUFFERS, True).start()

    @pl.when(count > 0)
    def _():
        slot = e % WEIGHT_BUFFERS
        wgu_bf[:, 0:D_EXPERT] = wg_buf[slot].astype(jnp.bfloat16)
        wgu_bf[:, D_EXPERT:] = wu_buf[slot].astype(jnp.bfloat16)
        wd_bf[...] = wd_buf[slot].astype(jnp.bfloat16)

        sync_x(first)
        put_act(up_proj(first), 0)

        def pair_body(i, c):
            g = first + 1 + 2 * i
            sync_x(g)
            sync_x(g + 1)
            sync_y_slot(g - 1)
            sync_y_slot(g)
            down_proj(g - 1, 0)
            hgu0 = up_proj(g)
            put_act(hgu0, 1)
            hgu1 = up_proj(g + 1)
            down_proj(g, 1)
            put_act(hgu1, 0)
            start_y(g - 1)
            start_y(g)
            return c

        n_pairs = (count - 1) // 2
        lax.fori_loop(0, n_pairs, pair_body, 0)

        @pl.when((count - 1) % 2 == 1)
        def _():
            g = first + count - 1
            sync_x(g)
            sync_y_slot(g - 1)
            down_proj(g - 1, 0)
            put_act(up_proj(g), 0)
            start_y(g - 1)

        last = first + count - 1
        sync_y_slot(last)
        down_proj(last, 0)
        start_y(last)

    @pl.when(e == N_EXPERTS - 1)
    def _():
        for k in range(TILE_BUFFERS, 0, -1):
            @pl.when(n_tiles >= k)
            def _():
                _tile_copy(y_hbm, ybuf, ysem, n_tiles - k, (n_tiles - k) % TILE_BUFFERS, True).wait()

        ybuf[0] = jnp.zeros((ROW_TILE, HALF), jnp.uint32)

        def fill(g, c):
            cp = _tile_copy(y_hbm, ybuf, ysem, g, 0, True)
            cp.start()
            cp.wait()
            return c

        lax.fori_loop(n_tiles, max_tiles, fill, 0)


def _run_experts(tile_start, tiles_per_expert, x_sorted, w_gate, w_up, w_down):
    n_rows = x_sorted.shape[0]
    max_tiles = n_rows // ROW_TILE
    hbm = pl.BlockSpec(memory_space=pl.ANY)

    grid_spec = pltpu.PrefetchScalarGridSpec(
        num_scalar_prefetch=2,
        grid=(N_EXPERTS,),
        in_specs=[hbm, hbm, hbm, hbm],
        out_specs=hbm,
        scratch_shapes=[
            pltpu.VMEM((TILE_BUFFERS, ROW_TILE, HALF), jnp.uint32),
            pltpu.VMEM((TILE_BUFFERS, ROW_TILE, HALF), jnp.uint32),
            pltpu.SemaphoreType.DMA((TILE_BUFFERS,)),
            pltpu.SemaphoreType.DMA((TILE_BUFFERS,)),
            pltpu.VMEM((WEIGHT_BUFFERS, D_MODEL, D_EXPERT), jnp.float32),
            pltpu.VMEM((WEIGHT_BUFFERS, D_MODEL, D_EXPERT), jnp.float32),
            pltpu.VMEM((WEIGHT_BUFFERS, D_EXPERT, D_MODEL), jnp.float32),
            pltpu.SemaphoreType.DMA((WEIGHT_BUFFERS,)),
            pltpu.VMEM((D_MODEL, 2 * D_EXPERT), jnp.bfloat16),
            pltpu.VMEM((D_EXPERT, D_MODEL), jnp.bfloat16),
            pltpu.VMEM((2, ROW_TILE, D_EXPERT), jnp.bfloat16),
        ],
    )
    return pl.pallas_call(
        functools.partial(_expert_kernel, max_tiles=max_tiles),
        grid_spec=grid_spec,
        out_shape=jax.ShapeDtypeStruct((n_rows, HALF), jnp.uint32),
        compiler_params=pltpu.CompilerParams(
            dimension_semantics=("arbitrary",), vmem_limit_bytes=VMEM_LIMIT),
    )(tile_start, tiles_per_expert, x_sorted, w_gate, w_up, w_down)


def _combine_rows(r0, h1_ref, p_ref, route_t_ref, ya_ref, yb_ref, w_pg, b_pg, w_ple, g_ref, b_ref, o_ref):
    rows = pl.ds(r0, SUB_TILE)
    h_hi = h1_ref[rows, :].astype(jnp.bfloat16)
    p_b = p_ref[rows, :].astype(jnp.bfloat16)
    rec_t = route_t_ref[:, rows]
    padded = jnp.concatenate([rec_t, jnp.zeros((LANES - ROUTE_ROWS, SUB_TILE), jnp.float32)], axis=0)
    route = jnp.transpose(padded)
    yield
    gate_pre = jnp.dot(h_hi, w_pg[...], preferred_element_type=jnp.float32)
    ple_pre = jnp.dot(p_b, w_ple[...], preferred_element_type=jnp.float32)
    ple = ple_pre * jax.nn.sigmoid(gate_pre + b_pg[...])
    yield
    w1 = route[:, R_W1:R_W1 + 1]
    w2 = route[:, R_W2:R_W2 + 1]
    a_lo, a_hi = _unpack_bf16_pairs(ya_ref[rows, :])
    b_lo, b_hi = _unpack_bf16_pairs(yb_ref[rows, :])
    moe = jnp.concatenate([w1 * a_lo + w2 * b_lo, w1 * a_hi + w2 * b_hi], axis=-1)
    o_ref[rows, :] = _layernorm(DEEPNORM_ALPHA * h1_ref[rows, :] + ple + moe, g_ref[...], b_ref[...])


def _combine_kernel(h1_ref, p_ref, route_t_ref, ya_ref, yb_ref, w_pg, b_pg, w_ple, g_ref, b_ref, o_ref):
    chains = [_combine_rows(r0, h1_ref, p_ref, route_t_ref, ya_ref, yb_ref, w_pg, b_pg, w_ple, g_ref, b_ref, o_ref)
              for r0 in range(0, h1_ref.shape[0], SUB_TILE)]
    for t in range(COMBINE_PHASES + len(chains) - 1):
        for k, chain in reversed(list(enumerate(chains))):
            if 0 <= t - k < COMBINE_PHASES:
                next(chain, None)


def _run_combine(h1, p2, route_t, y_tok, w_pg, b_pg, w_ple, ln2_g, ln2_b):
    n_tok = h1.shape[0]
    n_t = n_tok // TOKEN_TILE
    tok_map = lambda i: (i, 0)
    const = lambda shape: pl.BlockSpec(shape, lambda i: (0, 0), pipeline_mode=pl.Buffered(1))
    in_specs = [
        pl.BlockSpec((TOKEN_TILE, D_MODEL), tok_map),
        pl.BlockSpec((TOKEN_TILE, PLE_DIM), tok_map),
        pl.BlockSpec((ROUTE_ROWS, TOKEN_TILE), lambda i: (0, i)),
        pl.BlockSpec((TOKEN_TILE, HALF), lambda i: (i, 0)),
        pl.BlockSpec((TOKEN_TILE, HALF), lambda i: (i + n_t, 0)),
        const((D_MODEL, D_MODEL)), const((1, D_MODEL)), const((PLE_DIM, D_MODEL)),
        const((1, D_MODEL)), const((1, D_MODEL)),
    ]
    return pl.pallas_call(
        _combine_kernel,
        grid=(n_t,),
        in_specs=in_specs,
        out_specs=pl.BlockSpec((TOKEN_TILE, D_MODEL), tok_map),
        out_shape=jax.ShapeDtypeStruct((n_tok, D_MODEL), jnp.float32),
        compiler_params=pltpu.CompilerParams(dimension_semantics=("arbitrary",), vmem_limit_bytes=VMEM_LIMIT),
    )(h1, p2, route_t, y_tok, y_tok, w_pg, b_pg, w_ple, ln2_g, ln2_b)


def _split_bf16(w):
    hi = w.astype(jnp.bfloat16)
    lo = (w - hi.astype(jnp.float32)).astype(jnp.bfloat16)
    return hi, lo


def kernel(x, p, ln_in_g, ln_in_b, w_in, conv_w, conv_b, pool_w, pool_scale, w_out, ln1_g, ln1_b,
           w_rg, b_rg, w_re, b_re, w_gate, w_up, w_down, w_pg, b_pg, w_ple, ln2_g, ln2_b):
    batch, seq, _ = x.shape
    n_tok = batch * seq
    bf = jnp.bfloat16
    row = lambda v: v.reshape(1, -1)

    w_r = jnp.concatenate([w_rg[0], jnp.transpose(w_re[0], (1, 0, 2)).reshape(D_MODEL, N_EXPERTS)], axis=1)
    w_r = jnp.pad(w_r, ((0, 0), (0, LANES - w_r.shape[1])))
    w_r_hi, w_r_lo = _split_bf16(w_r)
    w_r_cat = jnp.concatenate([w_r_hi, w_r_hi, w_r_lo], axis=0)
    b_r = jnp.pad(jnp.concatenate([b_rg[0], b_re[0].reshape(-1)]), (0, LANES - N_GROUPS - N_EXPERTS)).reshape(1, LANES)

    x2 = x.reshape(n_tok, D_MODEL)
    p2 = p[0].reshape(n_tok, PLE_DIM)
    mixer_weights = (row(ln_in_g), row(ln_in_b), w_in[0], conv_w, row(conv_b[0]),
                     pool_w[0], row(pool_scale[0]), w_out[0], row(ln1_g[0]), row(ln1_b[0]),
                     w_r_cat, b_r)
    combine_weights = (w_pg[0].astype(bf), row(b_pg[0]), w_ple[0].astype(bf), row(ln2_g[0]), row(ln2_b[0]))
    expert_weights = (w_gate[0].reshape(N_EXPERTS, D_MODEL, D_EXPERT),
                      w_up[0].reshape(N_EXPERTS, D_MODEL, D_EXPERT),
                      w_down[0].reshape(N_EXPERTS, D_EXPERT, D_MODEL))
    zero_rows = jnp.zeros((SC_WINDOW, HALF), jnp.uint32)

    n_rows = -(-(2 * n_tok + N_EXPERTS * (ROW_TILE - 1)) // ROW_TILE) * ROW_TILE
    xp, h1, route_t, counts = _run_mixer(x2, *mixer_weights, batch, seq)
    pos, tile_start, tiles_per_expert, pad_pos = _run_plan(route_t, counts, n_rows)
    x_sorted = _dispatch_rows(xp, pos, pad_pos, zero_rows, n_rows)
    y_sorted = _run_experts(tile_start[0, :N_EXPERTS], tiles_per_expert[0, :N_EXPERTS], x_sorted, *expert_weights)

    y_tok = _gather_rows(y_sorted, pos.reshape(-1))
    out = _run_combine(h1, p2, route_t, y_tok, *combine_weights)
    return out.reshape(batch, seq, D_MODEL)
```
